```python
import math
import jax, jax.numpy as jnp
from jax import lax
import numpy as np

D_MODEL = 1024
BATCH = 8
SEQ = 4096
DEPTH = 1

N_META = 16
D_SSM = D_MODEL // 2
D_POOL = D_MODEL - D_SSM
SSM_GROUP = 16
SSM_GROUPS = D_SSM // SSM_GROUP
SSM_STATE = 64
POOL_WINDOWS = (2, 4, 8, 16)
POOL_GROUPS = len(POOL_WINDOWS)
POOL_GROUP_DIM = D_POOL // POOL_GROUPS
D_FF = ((8 * D_MODEL // 3 + 255) // 256) * 256
STEP_MIN = 1e-3
STEP_MAX = 1e-1
EPS = 1e-6

kernel_name = "hymba_s5_multiscale_pool_hybrid"


def rmsnorm(x, g):
    xf = x.astype(jnp.float32)
    return xf * lax.rsqrt(jnp.mean(xf * xf, axis=-1, keepdims=True) + EPS) * g.astype(jnp.float32)


def _complex_affine_combine(e1, e2):
    a1r, a1i, b1r, b1i = e1
    a2r, a2i, b2r, b2i = e2
    ar = a2r * a1r - a2i * a1i
    ai = a2r * a1i + a2i * a1r
    br = a2r * b1r - a2i * b1i + b2r
    bi = a2r * b1i + a2i * b1r + b2i
    return (ar, ai, br, bi)


def s5_mixer(u, lam_re, lam_im, log_step, b_re, b_im, c_re, c_im, d, glu_w, glu_b):
    L = u.shape[1]
    lr = jnp.minimum(lam_re.astype(jnp.float32), -1e-4)
    li = lam_im.astype(jnp.float32)
    step = jnp.exp(log_step.astype(jnp.float32))[:, None]
    mag = jnp.exp(lr * step)
    ang = li * step
    abr = mag * jnp.cos(ang)
    abi = mag * jnp.sin(ang)
    nr = abr - 1.0
    ni = abi
    den = lr * lr + li * li
    cr = ((nr * lr + ni * li) / den)[..., None]
    ci = ((ni * lr - nr * li) / den)[..., None]
    br = b_re.astype(jnp.float32)
    bi = b_im.astype(jnp.float32)
    bbr = cr * br - ci * bi
    bbi = cr * bi + ci * br
    uf = u.astype(jnp.float32)
    bu_r = jnp.einsum('blgh,gph->blgp', uf, bbr)
    bu_i = jnp.einsum('blgh,gph->blgp', uf, bbi)
    a_r = jnp.broadcast_to(abr[None, None], (1, L) + abr.shape)
    a_i = jnp.broadcast_to(abi[None, None], (1, L) + abi.shape)
    _, _, sr, si = lax.associative_scan(_complex_affine_combine, (a_r, a_i, bu_r, bu_i), axis=1)
    y = (jnp.einsum('blgp,ghp->blgh', sr, c_re.astype(jnp.float32))
         - jnp.einsum('blgp,ghp->blgh', si, c_im.astype(jnp.float32))
         + d.astype(jnp.float32) * uf)
    g = jax.nn.gelu(y)
    gate = jnp.einsum('blgh,ghk->blgk', g, glu_w.astype(jnp.float32)) + glu_b.astype(jnp.float32)
    return g * jax.nn.sigmoid(gate)


def pool_mixer(v, pool_w, pool_scale):
    L = v.shape[1]
    vf = v.astype(jnp.float32)
    cs = jnp.cumsum(vf, axis=1)
    t = jnp.arange(1, L + 1, dtype=jnp.float32)
    outs = []
    for k, w in enumerate(POOL_WINDOWS):
        ck = cs[:, :, k]
        lower = jnp.pad(ck, ((0, 0), (w, 0), (0, 0)))[:, :L]
        cnt = jnp.minimum(t, float(w))[None, :, None]
        outs.append((ck - lower) / cnt - vf[:, :, k])
    p = jnp.stack(outs, axis=2)
    p = jnp.einsum('blkc,kcd->blkd', p, pool_w.astype(jnp.float32))
    return p * pool_scale.astype(jnp.float32)


def _fwd_setup_inputs(seed: int = 0) -> dict:
    key = jax.random.key(seed)
    ks = jax.random.split(key, 24)
    f32 = jnp.float32
    G, H, P = SSM_GROUPS, SSM_GROUP, SSM_STATE
    n = jnp.arange(P, dtype=f32)
    x = jax.random.normal(ks[0], (BATCH, SEQ, D_MODEL), f32)
    meta_tokens = jax.random.normal(ks[1], (N_META, D_MODEL), f32)
    norm1_g = 1.0 + 0.02 * jax.random.normal(ks[2], (DEPTH, D_MODEL), f32)
    w_in = jax.random.normal(ks[3], (DEPTH, D_MODEL, D_MODEL), f32) * D_MODEL ** -0.5
    ssm_lambda_re = -0.5 + 0.01 * jax.random.normal(ks[4], (DEPTH, G, P), f32)
    ssm_lambda_im = math.pi * n + 0.01 * jax.random.normal(ks[5], (DEPTH, G, P), f32)
    ssm_log_step = jax.random.uniform(ks[6], (DEPTH, G), f32, math.log(STEP_MIN), math.log(STEP_MAX))
    ssm_b_re = jax.random.normal(ks[7], (DEPTH, G, P, H), f32) * (2.0 * H) ** -0.5
    ssm_b_im = jax.random.normal(ks[8], (DEPTH, G, P, H), f32) * (2.0 * H) ** -0.5
    ssm_c_re = jax.random.normal(ks[9], (DEPTH, G, H, P), f32) * (2.0 * P) ** -0.5 * 4.0
    ssm_c_im = jax.random.normal(ks[10], (DEPTH, G, H, P), f32) * (2.0 * P) ** -0.5 * 4.0
    ssm_d = jax.random.normal(ks[11], (DEPTH, G, H), f32)
    ssm_glu_w = jax.random.normal(ks[12], (DEPTH, G, H, H), f32) * H ** -0.5
    ssm_glu_b = 0.02 * jax.random.normal(ks[13], (DEPTH, G, H), f32)
    ssm_norm_g = 1.0 + 0.02 * jax.random.normal(ks[14], (DEPTH, D_SSM), f32)
    pool_w = jax.random.normal(ks[15], (DEPTH, POOL_GROUPS, POOL_GROUP_DIM, POOL_GROUP_DIM), f32) * POOL_GROUP_DIM ** -0.5
    pool_scale = 1.0 + 0.1 * jax.random.normal(ks[16], (DEPTH, POOL_GROUPS, POOL_GROUP_DIM), f32)
    pool_norm_g = 1.0 + 0.02 * jax.random.normal(ks[17], (DEPTH, D_POOL), f32)
    w_out = jax.random.normal(ks[18], (DEPTH, D_MODEL, D_MODEL), f32) * D_MODEL ** -0.5
    norm2_g = 1.0 + 0.02 * jax.random.normal(ks[19], (DEPTH, D_MODEL), f32)
    w_gate = jax.random.normal(ks[20], (DEPTH, D_MODEL, D_FF), f32) * D_MODEL ** -0.5
    w_up = jax.random.normal(ks[21], (DEPTH, D_MODEL, D_FF), f32) * D_MODEL ** -0.5
    w_down = jax.random.normal(ks[22], (DEPTH, D_FF, D_MODEL), f32) * D_FF ** -0.5
    final_norm_g = 1.0 + 0.02 * jax.random.normal(ks[23], (D_MODEL,), f32)
    return {"x": x, "meta_tokens": meta_tokens, "norm1_g": norm1_g, "w_in": w_in,
            "ssm_lambda_re": ssm_lambda_re, "ssm_lambda_im": ssm_lambda_im,
            "ssm_log_step": ssm_log_step, "ssm_b_re": ssm_b_re, "ssm_b_im": ssm_b_im,
            "ssm_c_re": ssm_c_re, "ssm_c_im": ssm_c_im, "ssm_d": ssm_d,
            "ssm_glu_w": ssm_glu_w, "ssm_glu_b": ssm_glu_b, "ssm_norm_g": ssm_norm_g,
            "pool_w": pool_w, "pool_scale": pool_scale, "pool_norm_g": pool_norm_g,
            "w_out": w_out, "norm2_g": norm2_g, "w_gate": w_gate, "w_up": w_up,
            "w_down": w_down, "final_norm_g": final_norm_g}


def _fwd_reference(x, meta_tokens, norm1_g, w_in, ssm_lambda_re, ssm_lambda_im, ssm_log_step,
              ssm_b_re, ssm_b_im, ssm_c_re, ssm_c_im, ssm_d, ssm_glu_w, ssm_glu_b,
              ssm_norm_g, pool_w, pool_scale, pool_norm_g, w_out, norm2_g, w_gate, w_up,
              w_down, final_norm_g):
    B = x.shape[0]
    meta = jnp.broadcast_to(meta_tokens.astype(jnp.float32)[None], (B, N_META, D_MODEL))
    h = jnp.concatenate([meta, x.astype(jnp.float32)], axis=1)
    L = h.shape[1]
    for i in range(DEPTH):
        n1 = rmsnorm(h, norm1_g[i])
        proj = n1 @ w_in[i].astype(jnp.float32)
        u = proj[..., :D_SSM].reshape(B, L, SSM_GROUPS, SSM_GROUP)
        v = proj[..., D_SSM:].reshape(B, L, POOL_GROUPS, POOL_GROUP_DIM)
        ys = s5_mixer(u, ssm_lambda_re[i], ssm_lambda_im[i], ssm_log_step[i], ssm_b_re[i],
                      ssm_b_im[i], ssm_c_re[i], ssm_c_im[i], ssm_d[i], ssm_glu_w[i],
                      ssm_glu_b[i]).reshape(B, L, D_SSM)
        yp = pool_mixer(v, pool_w[i], pool_scale[i]).reshape(B, L, D_POOL)
        mixed = jnp.concatenate([rmsnorm(ys, ssm_norm_g[i]), rmsnorm(yp, pool_norm_g[i])], axis=-1)
        h = h + mixed @ w_out[i].astype(jnp.float32)
        n2 = rmsnorm(h, norm2_g[i])
        ff = jax.nn.silu(n2 @ w_gate[i].astype(jnp.float32)) * (n2 @ w_up[i].astype(jnp.float32))
        h = h + ff @ w_down[i].astype(jnp.float32)
    out = rmsnorm(h, final_norm_g)[:, N_META:]
    return out.astype(x.dtype)


import jax as _jax
import jax.numpy as _jnp

TWIN_FORMAT = 'train_step'
FWD_PARAMS = ['x', 'meta_tokens', 'norm1_g', 'w_in', 'ssm_lambda_re', 'ssm_lambda_im', 'ssm_log_step', 'ssm_b_re', 'ssm_b_im', 'ssm_c_re', 'ssm_c_im', 'ssm_d', 'ssm_glu_w', 'ssm_glu_b', 'ssm_norm_g', 'pool_w', 'pool_scale', 'pool_norm_g', 'w_out', 'norm2_g', 'w_gate', 'w_up', 'w_down', 'final_norm_g']
TWIN_WEIGHTS = ['meta_tokens', 'norm1_g', 'w_in', 'ssm_lambda_re', 'ssm_lambda_im', 'ssm_log_step', 'ssm_b_re', 'ssm_b_im', 'ssm_c_re', 'ssm_c_im', 'ssm_d', 'ssm_glu_w', 'ssm_glu_b', 'ssm_norm_g', 'pool_w', 'pool_scale', 'pool_norm_g', 'w_out', 'norm2_g', 'w_gate', 'w_up', 'w_down', 'final_norm_g']
TWIN_DIFF_INPUT = 'x'
TWIN_INPUTS = ['x', 'meta_tokens', 'norm1_g', 'w_in', 'ssm_lambda_re', 'ssm_lambda_im', 'ssm_log_step', 'ssm_b_re', 'ssm_b_im', 'ssm_c_re', 'ssm_c_im', 'ssm_d', 'ssm_glu_w', 'ssm_glu_b', 'ssm_norm_g', 'pool_w', 'pool_scale', 'pool_norm_g', 'w_out', 'norm2_g', 'w_gate', 'w_up', 'w_down', 'final_norm_g', 'loss_target', 'm_meta_tokens', 'm_norm1_g', 'm_w_in', 'm_ssm_lambda_re', 'm_ssm_lambda_im', 'm_ssm_log_step', 'm_ssm_b_re', 'm_ssm_b_im', 'm_ssm_c_re', 'm_ssm_c_im', 'm_ssm_d', 'm_ssm_glu_w', 'm_ssm_glu_b', 'm_ssm_norm_g', 'm_pool_w', 'm_pool_scale', 'm_pool_norm_g', 'm_w_out', 'm_norm2_g', 'm_w_gate', 'm_w_up', 'm_w_down', 'm_final_norm_g', 'v_meta_tokens', 'v_norm1_g', 'v_w_in', 'v_ssm_lambda_re', 'v_ssm_lambda_im', 'v_ssm_log_step', 'v_ssm_b_re', 'v_ssm_b_im', 'v_ssm_c_re', 'v_ssm_c_im', 'v_ssm_d', 'v_ssm_glu_w', 'v_ssm_glu_b', 'v_ssm_norm_g', 'v_pool_w', 'v_pool_scale', 'v_pool_norm_g', 'v_w_out', 'v_norm2_g', 'v_w_gate', 'v_w_up', 'v_w_down', 'v_final_norm_g']
TWIN_OUTPUTS = ['loss', 'grad_x', 'grad_meta_tokens', 'grad_norm1_g', 'grad_w_in', 'grad_ssm_lambda_re', 'grad_ssm_lambda_im', 'grad_ssm_log_step', 'grad_ssm_b_re', 'grad_ssm_b_im', 'grad_ssm_c_re', 'grad_ssm_c_im', 'grad_ssm_d', 'grad_ssm_glu_w', 'grad_ssm_glu_b', 'grad_ssm_norm_g', 'grad_pool_w', 'grad_pool_scale', 'grad_pool_norm_g', 'grad_w_out', 'grad_norm2_g', 'grad_w_gate', 'grad_w_up', 'grad_w_down', 'grad_final_norm_g', 'delta_meta_tokens', 'delta_norm1_g', 'delta_w_in', 'delta_ssm_lambda_re', 'delta_ssm_lambda_im', 'delta_ssm_log_step', 'delta_ssm_b_re', 'delta_ssm_b_im', 'delta_ssm_c_re', 'delta_ssm_c_im', 'delta_ssm_d', 'delta_ssm_glu_w', 'delta_ssm_glu_b', 'delta_ssm_norm_g', 'delta_pool_w', 'delta_pool_scale', 'delta_pool_norm_g', 'delta_w_out', 'delta_norm2_g', 'delta_w_gate', 'delta_w_up', 'delta_w_down', 'delta_final_norm_g', 'new_m_meta_tokens', 'new_m_norm1_g', 'new_m_w_in', 'new_m_ssm_lambda_re', 'new_m_ssm_lambda_im', 'new_m_ssm_log_step', 'new_m_ssm_b_re', 'new_m_ssm_b_im', 'new_m_ssm_c_re', 'new_m_ssm_c_im', 'new_m_ssm_d', 'new_m_ssm_glu_w', 'new_m_ssm_glu_b', 'new_m_ssm_norm_g', 'new_m_pool_w', 'new_m_pool_scale', 'new_m_pool_norm_g', 'new_m_w_out', 'new_m_norm2_g', 'new_m_w_gate', 'new_m_w_up', 'new_m_w_down', 'new_m_final_norm_g', 'new_v_meta_tokens', 'new_v_norm1_g', 'new_v_w_in', 'new_v_ssm_lambda_re', 'new_v_ssm_lambda_im', 'new_v_ssm_log_step', 'new_v_ssm_b_re', 'new_v_ssm_b_im', 'new_v_ssm_c_re', 'new_v_ssm_c_im', 'new_v_ssm_d', 'new_v_ssm_glu_w', 'new_v_ssm_glu_b', 'new_v_ssm_norm_g', 'new_v_pool_w', 'new_v_pool_scale', 'new_v_pool_norm_g', 'new_v_w_out', 'new_v_norm2_g', 'new_v_w_gate', 'new_v_w_up', 'new_v_w_down', 'new_v_final_norm_g']
TWIN_LEAF_KINDS = {'loss': 'loss', 'grad_x': 'grad_x', 'grad_meta_tokens': 'grad_w', 'grad_norm1_g': 'grad_w', 'grad_w_in': 'grad_w', 'grad_ssm_lambda_re': 'grad_w', 'grad_ssm_lambda_im': 'grad_w', 'grad_ssm_log_step': 'grad_w', 'grad_ssm_b_re': 'grad_w', 'grad_ssm_b_im': 'grad_w', 'grad_ssm_c_re': 'grad_w', 'grad_ssm_c_im': 'grad_w', 'grad_ssm_d': 'grad_w', 'grad_ssm_glu_w': 'grad_w', 'grad_ssm_glu_b': 'grad_w', 'grad_ssm_norm_g': 'grad_w', 'grad_pool_w': 'grad_w', 'grad_pool_scale': 'grad_w', 'grad_pool_norm_g': 'grad_w', 'grad_w_out': 'grad_w', 'grad_norm2_g': 'grad_w', 'grad_w_gate': 'grad_w', 'grad_w_up': 'grad_w', 'grad_w_down': 'grad_w', 'grad_final_norm_g': 'grad_w', 'delta_meta_tokens': 'delta_w', 'delta_norm1_g': 'delta_w', 'delta_w_in': 'delta_w', 'delta_ssm_lambda_re': 'delta_w', 'delta_ssm_lambda_im': 'delta_w', 'delta_ssm_log_step': 'delta_w', 'delta_ssm_b_re': 'delta_w', 'delta_ssm_b_im': 'delta_w', 'delta_ssm_c_re': 'delta_w', 'delta_ssm_c_im': 'delta_w', 'delta_ssm_d': 'delta_w', 'delta_ssm_glu_w': 'delta_w', 'delta_ssm_glu_b': 'delta_w', 'delta_ssm_norm_g': 'delta_w', 'delta_pool_w': 'delta_w', 'delta_pool_scale': 'delta_w', 'delta_pool_norm_g': 'delta_w', 'delta_w_out': 'delta_w', 'delta_norm2_g': 'delta_w', 'delta_w_gate': 'delta_w', 'delta_w_up': 'delta_w', 'delta_w_down': 'delta_w', 'delta_final_norm_g': 'delta_w', 'new_m_meta_tokens': 'new_m', 'new_m_norm1_g': 'new_m', 'new_m_w_in': 'new_m', 'new_m_ssm_lambda_re': 'new_m', 'new_m_ssm_lambda_im': 'new_m', 'new_m_ssm_log_step': 'new_m', 'new_m_ssm_b_re': 'new_m', 'new_m_ssm_b_im': 'new_m', 'new_m_ssm_c_re': 'new_m', 'new_m_ssm_c_im': 'new_m', 'new_m_ssm_d': 'new_m', 'new_m_ssm_glu_w': 'new_m', 'new_m_ssm_glu_b': 'new_m', 'new_m_ssm_norm_g': 'new_m', 'new_m_pool_w': 'new_m', 'new_m_pool_scale': 'new_m', 'new_m_pool_norm_g': 'new_m', 'new_m_w_out': 'new_m', 'new_m_norm2_g': 'new_m', 'new_m_w_gate': 'new_m', 'new_m_w_up': 'new_m', 'new_m_w_down': 'new_m', 'new_m_final_norm_g': 'new_m', 'new_v_meta_tokens': 'new_v', 'new_v_norm1_g': 'new_v', 'new_v_w_in': 'new_v', 'new_v_ssm_lambda_re': 'new_v', 'new_v_ssm_lambda_im': 'new_v', 'new_v_ssm_log_step': 'new_v', 'new_v_ssm_b_re': 'new_v', 'new_v_ssm_b_im': 'new_v', 'new_v_ssm_c_re': 'new_v', 'new_v_ssm_c_im': 'new_v', 'new_v_ssm_d': 'new_v', 'new_v_ssm_glu_w': 'new_v', 'new_v_ssm_glu_b': 'new_v', 'new_v_ssm_norm_g': 'new_v', 'new_v_pool_w': 'new_v', 'new_v_pool_scale': 'new_v', 'new_v_pool_norm_g': 'new_v', 'new_v_w_out': 'new_v', 'new_v_norm2_g': 'new_v', 'new_v_w_gate': 'new_v', 'new_v_w_up': 'new_v', 'new_v_w_down': 'new_v', 'new_v_final_norm_g': 'new_v'}


def _forward(args):
    return _fwd_reference(*[args[k] for k in FWD_PARAMS])


def _output_shape():
    out = _jax.eval_shape(lambda: _forward(_fwd_setup_inputs(0)))
    return out.shape, out.dtype

N_MICROBATCH = 1
ADAM_LR = 0.001
ADAM_B1 = 0.9
ADAM_B2 = 0.999
ADAM_EPS = 1e-08
ADAM_WD = 0.01
ADAM_STEP = 10
PER_EXAMPLE_BATCH_AXIS = {'x': 0, 'loss_target': 0}
SHARED_INPUTS = []
_WEIGHT_DTYPES = {'meta_tokens': _jnp.float32, 'norm1_g': _jnp.float32, 'w_in': _jnp.float32, 'ssm_lambda_re': _jnp.float32, 'ssm_lambda_im': _jnp.float32, 'ssm_log_step': _jnp.float32, 'ssm_b_re': _jnp.float32, 'ssm_b_im': _jnp.float32, 'ssm_c_re': _jnp.float32, 'ssm_c_im': _jnp.float32, 'ssm_d': _jnp.float32, 'ssm_glu_w': _jnp.float32, 'ssm_glu_b': _jnp.float32, 'ssm_norm_g': _jnp.float32, 'pool_w': _jnp.float32, 'pool_scale': _jnp.float32, 'pool_norm_g': _jnp.float32, 'w_out': _jnp.float32, 'norm2_g': _jnp.float32, 'w_gate': _jnp.float32, 'w_up': _jnp.float32, 'w_down': _jnp.float32, 'final_norm_g': _jnp.float32}
MOMENT_SCALE = {'meta_tokens': 1.666805e-03, 'norm1_g': 1.441837e-01, 'w_in': 1.410081e-01, 'ssm_lambda_re': 2.882431e-02, 'ssm_lambda_im': 2.570812e-02, 'ssm_log_step': 2.814852e+01, 'ssm_b_re': 1.921134e-02, 'ssm_b_im': 1.875512e-02, 'ssm_c_re': 9.666890e-03, 'ssm_c_im': 9.164007e-03, 'ssm_d': 1.438515e-01, 'ssm_glu_w': 6.468789e-02, 'ssm_glu_b': 6.244077e-02, 'ssm_norm_g': 1.559843e-01, 'pool_w': 1.374831e-01, 'pool_scale': 1.407157e-01, 'pool_norm_g': 1.415556e-01, 'w_out': 1.389699e-01, 'norm2_g': 1.023670e-01, 'w_gate': 4.272735e-02, 'w_up': 4.160799e-02, 'w_down': 6.865631e-02, 'final_norm_g': 3.210278e+01}


def _to_microbatches(a, axis):
    t = _jnp.moveaxis(a, axis, 0)
    t = t.reshape((N_MICROBATCH, t.shape[0] // N_MICROBATCH) + t.shape[1:])
    return _jnp.moveaxis(t, 1, axis + 1)


def setup_inputs(seed: int = 0) -> dict:
    inp = _fwd_setup_inputs(seed)
    key = _jax.random.fold_in(_jax.random.key(seed), 7919)
    shape, _ = _output_shape()
    out = dict(inp)
    out["loss_target"] = _jax.random.normal(_jax.random.fold_in(key, 0), shape, _jnp.float32)
    for i, name in enumerate(TWIN_WEIGHTS):
        w = inp[name].astype(_jnp.float32)
        if MOMENT_SCALE is None:
            s = _jnp.sqrt(_jnp.mean(_jnp.square(w)) + 1e-30)
        else:
            s = MOMENT_SCALE[name]
        km, kv = _jax.random.split(_jax.random.fold_in(key, i + 1))
        out[name] = w
        out["m_" + name] = s * _jax.random.normal(km, w.shape, _jnp.float32)
        out["v_" + name] = (s * s) * _jax.random.uniform(kv, w.shape, _jnp.float32, 0.5, 1.5)
    if N_MICROBATCH > 1:
        for name, axis in PER_EXAMPLE_BATCH_AXIS.items():
            out[name] = _to_microbatches(out[name], axis)
    return {'x': out['x'], 'meta_tokens': out['meta_tokens'], 'norm1_g': out['norm1_g'], 'w_in': out['w_in'], 'ssm_lambda_re': out['ssm_lambda_re'], 'ssm_lambda_im': out['ssm_lambda_im'], 'ssm_log_step': out['ssm_log_step'], 'ssm_b_re': out['ssm_b_re'], 'ssm_b_im': out['ssm_b_im'], 'ssm_c_re': out['ssm_c_re'], 'ssm_c_im': out['ssm_c_im'], 'ssm_d': out['ssm_d'], 'ssm_glu_w': out['ssm_glu_w'], 'ssm_glu_b': out['ssm_glu_b'], 'ssm_norm_g': out['ssm_norm_g'], 'pool_w': out['pool_w'], 'pool_scale': out['pool_scale'], 'pool_norm_g': out['pool_norm_g'], 'w_out': out['w_out'], 'norm2_g': out['norm2_g'], 'w_gate': out['w_gate'], 'w_up': out['w_up'], 'w_down': out['w_down'], 'final_norm_g': out['final_norm_g'], 'loss_target': out['loss_target'], 'm_meta_tokens': out['m_meta_tokens'], 'm_norm1_g': out['m_norm1_g'], 'm_w_in': out['m_w_in'], 'm_ssm_lambda_re': out['m_ssm_lambda_re'], 'm_ssm_lambda_im': out['m_ssm_lambda_im'], 'm_ssm_log_step': out['m_ssm_log_step'], 'm_ssm_b_re': out['m_ssm_b_re'], 'm_ssm_b_im': out['m_ssm_b_im'], 'm_ssm_c_re': out['m_ssm_c_re'], 'm_ssm_c_im': out['m_ssm_c_im'], 'm_ssm_d': out['m_ssm_d'], 'm_ssm_glu_w': out['m_ssm_glu_w'], 'm_ssm_glu_b': out['m_ssm_glu_b'], 'm_ssm_norm_g': out['m_ssm_norm_g'], 'm_pool_w': out['m_pool_w'], 'm_pool_scale': out['m_pool_scale'], 'm_pool_norm_g': out['m_pool_norm_g'], 'm_w_out': out['m_w_out'], 'm_norm2_g': out['m_norm2_g'], 'm_w_gate': out['m_w_gate'], 'm_w_up': out['m_w_up'], 'm_w_down': out['m_w_down'], 'm_final_norm_g': out['m_final_norm_g'], 'v_meta_tokens': out['v_meta_tokens'], 'v_norm1_g': out['v_norm1_g'], 'v_w_in': out['v_w_in'], 'v_ssm_lambda_re': out['v_ssm_lambda_re'], 'v_ssm_lambda_im': out['v_ssm_lambda_im'], 'v_ssm_log_step': out['v_ssm_log_step'], 'v_ssm_b_re': out['v_ssm_b_re'], 'v_ssm_b_im': out['v_ssm_b_im'], 'v_ssm_c_re': out['v_ssm_c_re'], 'v_ssm_c_im': out['v_ssm_c_im'], 'v_ssm_d': out['v_ssm_d'], 'v_ssm_glu_w': out['v_ssm_glu_w'], 'v_ssm_glu_b': out['v_ssm_glu_b'], 'v_ssm_norm_g': out['v_ssm_norm_g'], 'v_pool_w': out['v_pool_w'], 'v_pool_scale': out['v_pool_scale'], 'v_pool_norm_g': out['v_pool_norm_g'], 'v_w_out': out['v_w_out'], 'v_norm2_g': out['v_norm2_g'], 'v_w_gate': out['v_w_gate'], 'v_w_up': out['v_w_up'], 'v_w_down': out['v_w_down'], 'v_final_norm_g': out['v_final_norm_g']}


def _loss(weights, diff, rest, loss_target):
    with _jax.named_scope("forward"):
        args = {**rest, TWIN_DIFF_INPUT: diff, **{k: w.astype(_WEIGHT_DTYPES[k]) for k, w in weights.items()}}
        y = _forward(args)
    with _jax.named_scope("loss_head"):
        err = _jnp.square(y.astype(_jnp.float32) - loss_target)
        return 0.5 * _jnp.sum(_jnp.mean(err, axis=-1)) if err.ndim else 0.5 * err


def _adamw(w, g, m, v):
    m = ADAM_B1 * m + (1.0 - ADAM_B1) * g
    v = ADAM_B2 * v + (1.0 - ADAM_B2) * _jnp.square(g)
    m_hat = m / (1.0 - ADAM_B1 ** ADAM_STEP)
    v_hat = v / (1.0 - ADAM_B2 ** ADAM_STEP)
    delta = -ADAM_LR * (m_hat / (_jnp.sqrt(v_hat) + ADAM_EPS) + ADAM_WD * w)
    return delta, m, v


def reference(x, meta_tokens, norm1_g, w_in, ssm_lambda_re, ssm_lambda_im, ssm_log_step, ssm_b_re, ssm_b_im, ssm_c_re, ssm_c_im, ssm_d, ssm_glu_w, ssm_glu_b, ssm_norm_g, pool_w, pool_scale, pool_norm_g, w_out, norm2_g, w_gate, w_up, w_down, final_norm_g, loss_target, m_meta_tokens, m_norm1_g, m_w_in, m_ssm_lambda_re, m_ssm_lambda_im, m_ssm_log_step, m_ssm_b_re, m_ssm_b_im, m_ssm_c_re, m_ssm_c_im, m_ssm_d, m_ssm_glu_w, m_ssm_glu_b, m_ssm_norm_g, m_pool_w, m_pool_scale, m_pool_norm_g, m_w_out, m_norm2_g, m_w_gate, m_w_up, m_w_down, m_final_norm_g, v_meta_tokens, v_norm1_g, v_w_in, v_ssm_lambda_re, v_ssm_lambda_im, v_ssm_log_step, v_ssm_b_re, v_ssm_b_im, v_ssm_c_re, v_ssm_c_im, v_ssm_d, v_ssm_glu_w, v_ssm_glu_b, v_ssm_norm_g, v_pool_w, v_pool_scale, v_pool_norm_g, v_w_out, v_norm2_g, v_w_gate, v_w_up, v_w_down, v_final_norm_g):
    given = dict(x=x, meta_tokens=meta_tokens, norm1_g=norm1_g, w_in=w_in, ssm_lambda_re=ssm_lambda_re, ssm_lambda_im=ssm_lambda_im, ssm_log_step=ssm_log_step, ssm_b_re=ssm_b_re, ssm_b_im=ssm_b_im, ssm_c_re=ssm_c_re, ssm_c_im=ssm_c_im, ssm_d=ssm_d, ssm_glu_w=ssm_glu_w, ssm_glu_b=ssm_glu_b, ssm_norm_g=ssm_norm_g, pool_w=pool_w, pool_scale=pool_scale, pool_norm_g=pool_norm_g, w_out=w_out, norm2_g=norm2_g, w_gate=w_gate, w_up=w_up, w_down=w_down, final_norm_g=final_norm_g, loss_target=loss_target, m_meta_tokens=m_meta_tokens, m_norm1_g=m_norm1_g, m_w_in=m_w_in, m_ssm_lambda_re=m_ssm_lambda_re, m_ssm_lambda_im=m_ssm_lambda_im, m_ssm_log_step=m_ssm_log_step, m_ssm_b_re=m_ssm_b_re, m_ssm_b_im=m_ssm_b_im, m_ssm_c_re=m_ssm_c_re, m_ssm_c_im=m_ssm_c_im, m_ssm_d=m_ssm_d, m_ssm_glu_w=m_ssm_glu_w, m_ssm_glu_b=m_ssm_glu_b, m_ssm_norm_g=m_ssm_norm_g, m_pool_w=m_pool_w, m_pool_scale=m_pool_scale, m_pool_norm_g=m_pool_norm_g, m_w_out=m_w_out, m_norm2_g=m_norm2_g, m_w_gate=m_w_gate, m_w_up=m_w_up, m_w_down=m_w_down, m_final_norm_g=m_final_norm_g, v_meta_tokens=v_meta_tokens, v_norm1_g=v_norm1_g, v_w_in=v_w_in, v_ssm_lambda_re=v_ssm_lambda_re, v_ssm_lambda_im=v_ssm_lambda_im, v_ssm_log_step=v_ssm_log_step, v_ssm_b_re=v_ssm_b_re, v_ssm_b_im=v_ssm_b_im, v_ssm_c_re=v_ssm_c_re, v_ssm_c_im=v_ssm_c_im, v_ssm_d=v_ssm_d, v_ssm_glu_w=v_ssm_glu_w, v_ssm_glu_b=v_ssm_glu_b, v_ssm_norm_g=v_ssm_norm_g, v_pool_w=v_pool_w, v_pool_scale=v_pool_scale, v_pool_norm_g=v_pool_norm_g, v_w_out=v_w_out, v_norm2_g=v_norm2_g, v_w_gate=v_w_gate, v_w_up=v_w_up, v_w_down=v_w_down, v_final_norm_g=v_final_norm_g)
    weights = {n: given[n] for n in TWIN_WEIGHTS}
    shared = {n: given[n] for n in SHARED_INPUTS}
    per_example = {n: given[n] for n in ['x']}
    grad_fn = _jax.value_and_grad(_loss, argnums=(0, 1))

    def one_microbatch(ex, loss_target):
        ex = dict(ex)
        diff = ex.pop(TWIN_DIFF_INPUT)
        return grad_fn(weights, diff, {**shared, **ex}, loss_target)

    if N_MICROBATCH == 1:
        loss, (grad_w, grad_x) = one_microbatch(per_example, given["loss_target"])
    else:
        def body(carry, xs):
            loss_sum, grad_sum = carry
            l_k, (gw_k, gx_k) = one_microbatch(xs[0], xs[1])
            with _jax.named_scope("update"):
                return (loss_sum + l_k, _jax.tree.map(_jnp.add, grad_sum, gw_k)), gx_k

        init = (_jnp.zeros((), _jnp.float32), _jax.tree.map(_jnp.zeros_like, weights))
        (loss, grad_w), grad_x = _jax.lax.scan(body, init, (per_example, given["loss_target"]))
    with _jax.named_scope("update"):
        delta_w, new_m, new_v = {}, {}, {}
        for n in TWIN_WEIGHTS:
            delta_w[n], new_m[n], new_v[n] = _adamw(weights[n], grad_w[n], given["m_" + n], given["v_" + n])
    return (loss, grad_x, *[grad_w[n] for n in TWIN_WEIGHTS], *[delta_w[n] for n in TWIN_WEIGHTS],
            *[new_m[n] for n in TWIN_WEIGHTS], *[new_v[n] for n in TWIN_WEIGHTS])
```

```python
import functools
import math

import jax
import jax.numpy as jnp
from jax import lax
from jax.experimental import pallas as pl
from jax.experimental.pallas import tpu as pltpu

F32 = jnp.float32
BF16 = jnp.bfloat16
MESH = pl.DeviceIdType.MESH
AXES = ("x", "y", "c")

D_MODEL = 1024
D_SSM = 512
D_POOL = 512
N_META = 16
SSM_GROUP = 16
SSM_GROUPS = 32
SSM_STATE = 64
N_STATE = SSM_GROUPS * SSM_STATE
STATE_BLOCKS = N_STATE // 128
SUPER = 4
POOL_WINDOWS = (2, 4, 8, 16)
POOL_HALO = 16
D_FF = 2816
N_SHARD = 4
FF_SHARD = D_FF // N_SHARD
EPS = 1e-6
ADAM_LR, ADAM_B1, ADAM_B2, ADAM_EPS, ADAM_WD, ADAM_STEP = 0.001, 0.9, 0.999, 1e-08, 0.01, 10
VMEM_LIMIT = 56 * 1024 * 1024


def _plan(n_rows):
    if n_rows > 2048:
        tm, tc = 416, 320
    else:
        tm, tc = 128, 64
    step = tm * tc // math.gcd(tm, tc)
    return -(-n_rows // step) * step, tm, tc


def _params(sem=None):
    return pltpu.CompilerParams(dimension_semantics=sem, vmem_limit_bytes=VMEM_LIMIT)


def _dot(a, b):
    return jnp.dot(a, b, preferred_element_type=F32)


def _dot_nt(a, b):
    return lax.dot_general(a, b, (((1,), (1,)), ((), ())), preferred_element_type=F32)


def _dot_tn(a, b):
    return lax.dot_general(a, b, (((0,), (0,)), ((), ())), preferred_element_type=F32)


def _sigmoid(x):
    return 1.0 / (1.0 + jnp.exp(-x))


_GELU_C = math.sqrt(2.0 / math.pi)


def _gelu_and_grad(y):
    y2 = y * y
    t = jnp.tanh(_GELU_C * (y + 0.044715 * y * y2))
    g = 0.5 * y * (1.0 + t)
    dg = 0.5 * (1.0 + t) + 0.5 * y * (1.0 - t * t) * (_GELU_C * (1.0 + 3.0 * 0.044715 * y2))
    return g, dg


def _rms(x):
    return lax.rsqrt(jnp.mean(x * x, axis=-1, keepdims=True) + EPS)


def _rms_bwd(dn, xhat, r):
    return r * (dn - xhat * jnp.mean(dn * xhat, axis=-1, keepdims=True))


def _full(shape):
    nd = len(shape)
    return pl.BlockSpec(shape, lambda *_: (0,) * nd)


def _fwd_in(h0, g1, w_in_b, tm):
    n_pad = h0.shape[0]

    def body(h_ref, g_ref, w_ref, u_ref, v_ref):
        h = h_ref[...]
        n1 = (h * _rms(h) * g_ref[...]).astype(BF16)
        proj = _dot(n1, w_ref[...])
        u_ref[...] = proj[:, :D_SSM]
        v_ref[...] = proj[:, D_SSM:]

    row = lambda w: pl.BlockSpec((tm, w), lambda i: (i, 0))
    return pl.pallas_call(
        body, grid=(n_pad // tm,), name="fwd_in",
        in_specs=[row(D_MODEL), _full((1, D_MODEL)), _full((D_MODEL, D_MODEL))],
        out_specs=[row(D_SSM), row(D_POOL)],
        out_shape=[jax.ShapeDtypeStruct((n_pad, D_SSM), F32), jax.ShapeDtypeStruct((n_pad, D_POOL), F32)],
        compiler_params=_params(("parallel",)),
    )(h0, g1, w_in_b)


def _fwd_ffn(h0, ms, mp, w_out_b, g2, wg_b, wu_b, wd_b, gf, target, tm, n_valid):
    n_pad = h0.shape[0]
    nt = n_pad // tm

    def body(h0_ref, ms_ref, mp_ref, wo_ref, g2_ref, wg_ref, wu_ref, wd_ref, gf_ref, tgt_ref,
             h1_ref, n2_ref, a_ref, b_ref, ff_ref, dh2_ref, loss_ref, dgf_ref, acc):
        i, q = pl.program_id(0), pl.program_id(1)

        @pl.when((i == 0) & (q == 0))
        def _():
            loss_ref[...] = jnp.zeros_like(loss_ref)
            dgf_ref[...] = jnp.zeros_like(dgf_ref)

        @pl.when(q == 0)
        def _():
            h1 = h0_ref[...] + _dot(ms_ref[...], wo_ref[:D_SSM, :]) + _dot(mp_ref[...], wo_ref[D_SSM:, :])
            h1_ref[...] = h1
            acc[...] = h1
            n2_ref[...] = (h1 * _rms(h1) * g2_ref[...]).astype(BF16)

        n2 = n2_ref[...]
        a = _dot(n2, wg_ref[0])
        b = _dot(n2, wu_ref[0])
        a_ref[0] = a
        b_ref[0] = b
        ff = (a * _sigmoid(a) * b).astype(BF16)
        ff_ref[0] = ff
        acc[...] += _dot(ff, wd_ref[0])

        @pl.when(q == N_SHARD - 1)
        def _():
            h2 = acc[...]
            r = _rms(h2)
            xhat = h2 * r
            gf_row = gf_ref[...]
            rows = i * tm + lax.broadcasted_iota(jnp.int32, (tm, 1), 0)
            valid = (rows >= N_META) & (rows < n_valid)
            diff = jnp.where(valid, xhat * gf_row - tgt_ref[...], 0.0)
            loss_ref[...] += jnp.full(loss_ref.shape, 0.5 / D_MODEL, F32) * jnp.sum(diff * diff)
            dout = diff * (1.0 / D_MODEL)
            dgf_ref[...] += jnp.sum(dout * xhat, axis=0, keepdims=True)
            dh2_ref[...] = _rms_bwd(dout * gf_row, xhat, r)

    row = lambda w: pl.BlockSpec((tm, w), lambda i, q: (i, 0))
    shard_cols = pl.BlockSpec((1, D_MODEL, FF_SHARD), lambda i, q: (q, 0, 0))
    shard_rows = pl.BlockSpec((1, FF_SHARD, D_MODEL), lambda i, q: (q, 0, 0))
    act = pl.BlockSpec((1, tm, FF_SHARD), lambda i, q: (q, i, 0))
    sds = jax.ShapeDtypeStruct
    return pl.pallas_call(
        body, grid=(nt, N_SHARD), name="fwd_ffn",
        in_specs=[row(D_MODEL), row(D_SSM), row(D_POOL), _full((D_MODEL, D_MODEL)), _full((1, D_MODEL)),
                  shard_cols, shard_cols, shard_rows, _full((1, D_MODEL)), row(D_MODEL)],
        out_specs=[row(D_MODEL), row(D_MODEL), act, act, act, row(D_MODEL), _full((8, 128)), _full((1, D_MODEL))],
        out_shape=[sds((n_pad, D_MODEL), F32), sds((n_pad, D_MODEL), BF16),
                   sds((N_SHARD, n_pad, FF_SHARD), F32), sds((N_SHARD, n_pad, FF_SHARD), F32),
                   sds((N_SHARD, n_pad, FF_SHARD), BF16), sds((n_pad, D_MODEL), F32),
                   sds((8, 128), F32), sds((1, D_MODEL), F32)],
        scratch_shapes=[pltpu.VMEM((tm, D_MODEL), F32)],
        compiler_params=_params(("arbitrary", "arbitrary")),
    )(h0, ms, mp, w_out_b, g2, wg_b, wu_b, wd_b, gf, target)


def _bwd_ffn(dh2, a, b, wg_b, wu_b, wd_b, h1, g2, tm):
    n_pad = dh2.shape[0]

    def body(dh2_ref, a_ref, b_ref, wg_ref, wu_ref, wd_ref, h1_ref, g2_ref, da_ref, db_ref, dh1_ref, dg2_ref, acc):
        i, q = pl.program_id(0), pl.program_id(1)

        @pl.when((i == 0) & (q == 0))
        def _():
            dg2_ref[...] = jnp.zeros_like(dg2_ref)

        dff = _dot_nt(dh2_ref[...].astype(BF16), wd_ref[0])
        a_v, b_v = a_ref[0], b_ref[0]
        sig = _sigmoid(a_v)
        da = (dff * b_v * sig * (1.0 + a_v * (1.0 - sig))).astype(BF16)
        db = (dff * a_v * sig).astype(BF16)
        da_ref[0] = da
        db_ref[0] = db
        part = _dot_nt(da, wg_ref[0]) + _dot_nt(db, wu_ref[0])

        @pl.when(q == 0)
        def _():
            acc[...] = part

        @pl.when(q > 0)
        def _():
            acc[...] += part

        @pl.when(q == N_SHARD - 1)
        def _():
            h1 = h1_ref[...]
            r = _rms(h1)
            xhat = h1 * r
            dn2 = acc[...]
            dg2_ref[...] += jnp.sum(dn2 * xhat, axis=0, keepdims=True)
            dh1_ref[...] = dh2_ref[...] + _rms_bwd(dn2 * g2_ref[...], xhat, r)

    row = lambda w: pl.BlockSpec((tm, w), lambda i, q: (i, 0))
    shard_cols = pl.BlockSpec((1, D_MODEL, FF_SHARD), lambda i, q: (q, 0, 0))
    shard_rows = pl.BlockSpec((1, FF_SHARD, D_MODEL), lambda i, q: (q, 0, 0))
    act = pl.BlockSpec((1, tm, FF_SHARD), lambda i, q: (q, i, 0))
    sds = jax.ShapeDtypeStruct
    return pl.pallas_call(
        body, grid=(n_pad // tm, N_SHARD), name="bwd_ffn",
        in_specs=[row(D_MODEL), act, act, shard_cols, shard_cols, shard_rows, row(D_MODEL), _full((1, D_MODEL))],
        out_specs=[act, act, row(D_MODEL), _full((1, D_MODEL))],
        out_shape=[sds((N_SHARD, n_pad, FF_SHARD), BF16), sds((N_SHARD, n_pad, FF_SHARD), BF16),
                   sds((n_pad, D_MODEL), F32), sds((1, D_MODEL), F32)],
        scratch_shapes=[pltpu.VMEM((tm, D_MODEL), F32)],
        compiler_params=_params(("arbitrary", "arbitrary")),
    )(dh2, a, b, wg_b, wu_b, wd_b, h1, g2)


def _grad_ffn(n2, da, db, ff, dh2, tm):
    n_pad = n2.shape[0]

    def body(n2_ref, da_ref, db_ref, ff_ref, dh2_ref, dwg_ref, dwu_ref, dwd_ref):
        i = pl.program_id(1)
        n2_v = n2_ref[...]
        gg = _dot_tn(n2_v, da_ref[0])
        gu = _dot_tn(n2_v, db_ref[0])
        gd = _dot_tn(ff_ref[0], dh2_ref[...].astype(BF16))

        @pl.when(i == 0)
        def _():
            dwg_ref[0] = gg
            dwu_ref[0] = gu
            dwd_ref[0] = gd

        @pl.when(i > 0)
        def _():
            dwg_ref[0] += gg
            dwu_ref[0] += gu
            dwd_ref[0] += gd

    row = lambda w: pl.BlockSpec((tm, w), lambda q, i: (i, 0))
    act = pl.BlockSpec((1, tm, FF_SHARD), lambda q, i: (q, i, 0))
    sds = jax.ShapeDtypeStruct
    return pl.pallas_call(
        body, grid=(N_SHARD, n_pad // tm), name="grad_ffn",
        in_specs=[row(D_MODEL), act, act, act, row(D_MODEL)],
        out_specs=[pl.BlockSpec((1, D_MODEL, FF_SHARD), lambda q, i: (q, 0, 0)),
                   pl.BlockSpec((1, D_MODEL, FF_SHARD), lambda q, i: (q, 0, 0)),
                   pl.BlockSpec((1, FF_SHARD, D_MODEL), lambda q, i: (q, 0, 0))],
        out_shape=[sds((N_SHARD, D_MODEL, FF_SHARD), F32), sds((N_SHARD, D_MODEL, FF_SHARD), F32),
                   sds((N_SHARD, FF_SHARD, D_MODEL), F32)],
        compiler_params=_params(("parallel", "arbitrary")),
    )(n2, da, db, ff, dh2)


def _bwd_out(dh1, ms, mp, w_out_b, tm):
    n_pad = dh1.shape[0]

    def body(dh1_ref, ms_ref, mp_ref, wo_ref, dms_ref, dmp_ref, dwo_ref):
        i = pl.program_id(0)

        @pl.when(i == 0)
        def _():
            dwo_ref[...] = jnp.zeros_like(dwo_ref)

        d = dh1_ref[...].astype(BF16)
        dms_ref[...] = _dot_nt(d, wo_ref[:D_SSM, :])
        dmp_ref[...] = _dot_nt(d, wo_ref[D_SSM:, :])
        dwo_ref[:D_SSM, :] += _dot_tn(ms_ref[...], d)
        dwo_ref[D_SSM:, :] += _dot_tn(mp_ref[...], d)

    row = lambda w: pl.BlockSpec((tm, w), lambda i: (i, 0))
    sds = jax.ShapeDtypeStruct
    return pl.pallas_call(
        body, grid=(n_pad // tm,), name="bwd_out",
        in_specs=[row(D_MODEL), row(D_SSM), row(D_POOL), _full((D_MODEL, D_MODEL))],
        out_specs=[row(D_SSM), row(D_POOL), _full((D_MODEL, D_MODEL))],
        out_shape=[sds((n_pad, D_SSM), F32), sds((n_pad, D_POOL), F32), sds((D_MODEL, D_MODEL), F32)],
        compiler_params=_params(("arbitrary",)),
    )(dh1, ms, mp, w_out_b)


def _bwd_in(du, dv, h0, dh1, g1, w_in_b, tm):
    n_pad = h0.shape[0]

    def body(du_ref, dv_ref, h0_ref, dh1_ref, g1_ref, w_ref, dh0_ref, dwi_ref, dg1_ref):
        i = pl.program_id(0)

        @pl.when(i == 0)
        def _():
            dwi_ref[...] = jnp.zeros_like(dwi_ref)
            dg1_ref[...] = jnp.zeros_like(dg1_ref)

        dub = du_ref[...].astype(BF16)
        dvb = dv_ref[...].astype(BF16)
        dn1 = _dot_nt(dub, w_ref[:, :D_SSM]) + _dot_nt(dvb, w_ref[:, D_SSM:])
        h = h0_ref[...]
        r = _rms(h)
        xhat = h * r
        g_row = g1_ref[...]
        n1 = (xhat * g_row).astype(BF16)
        dwi_ref[:, :D_SSM] += _dot_tn(n1, dub)
        dwi_ref[:, D_SSM:] += _dot_tn(n1, dvb)
        dg1_ref[...] += jnp.sum(dn1 * xhat, axis=0, keepdims=True)
        dh0_ref[...] = dh1_ref[...] + _rms_bwd(dn1 * g_row, xhat, r)

    row = lambda w: pl.BlockSpec((tm, w), lambda i: (i, 0))
    sds = jax.ShapeDtypeStruct
    return pl.pallas_call(
        body, grid=(n_pad // tm,), name="bwd_in",
        in_specs=[row(D_SSM), row(D_POOL), row(D_MODEL), row(D_MODEL), _full((1, D_MODEL)), _full((D_MODEL, D_MODEL))],
        out_specs=[row(D_MODEL), _full((D_MODEL, D_MODEL)), _full((1, D_MODEL))],
        out_shape=[sds((n_pad, D_MODEL), F32), sds((D_MODEL, D_MODEL), F32), sds((1, D_MODEL), F32)],
        compiler_params=_params(("arbitrary",)),
    )(du, dv, h0, dh1, g1, w_in_b)


def _planes_store(ref, j, val, tc):
    for i in range(4):
        ref[pl.ds(4 * j + i, tc, stride=STATE_BLOCKS), :] = val[:, 128 * i:128 * (i + 1)]


def _planes_load(ref, j, tc):
    return jnp.concatenate([ref[pl.ds(4 * j + i, tc, stride=STATE_BLOCKS), :] for i in range(4)], axis=1)


def _s5_tail(y, glu_ref, glub):
    g, dgelu = _gelu_and_grad(y)
    gb = g.astype(BF16)
    gate = jnp.concatenate([_dot(gb[:, 128 * j:128 * (j + 1)], glu_ref[j]) for j in range(SUPER)], axis=1) + glub
    sig = _sigmoid(gate)
    return g, gb, dgelu, sig, g * sig


def _s5_fwd(u, lam, bbr, bbi, crt, cit, vecs, glu, tc):
    n_pad = u.shape[0]

    def body(u_ref, lam_ref, bbr_ref, bbi_ref, crt_ref, cit_ref, vec_ref, glu_ref,
             sr_ref, si_ref, y_ref, ms_ref, hr_s, hi_s):
        @pl.when(pl.program_id(0) == 0)
        def _():
            hr_s[...] = jnp.zeros_like(hr_s)
            hi_s[...] = jnp.zeros_like(hi_s)

        u_v = u_ref[...]
        ub = u_v.astype(BF16)
        for j in range(SUPER):
            uj = ub[:, 128 * j:128 * (j + 1)]
            _planes_store(sr_ref, j, _dot(uj, bbr_ref[j]), tc)
            _planes_store(si_ref, j, _dot(uj, bbi_ref[j]), tc)

        ar = lam_ref[0:STATE_BLOCKS, :]
        ai = lam_ref[STATE_BLOCKS:, :]

        def step(t, carry):
            hr, hi = carry
            o = pl.multiple_of(t * STATE_BLOCKS, STATE_BLOCKS)
            nr = ar * hr - ai * hi + sr_ref[pl.ds(o, STATE_BLOCKS), :]
            ni = ar * hi + ai * hr + si_ref[pl.ds(o, STATE_BLOCKS), :]
            sr_ref[pl.ds(o, STATE_BLOCKS), :] = nr
            si_ref[pl.ds(o, STATE_BLOCKS), :] = ni
            return nr, ni

        hr, hi = lax.fori_loop(0, tc, step, (hr_s[...], hi_s[...]), unroll=4)
        hr_s[...] = hr
        hi_s[...] = hi

        d_row, glub, gs = vec_ref[0:1, :], vec_ref[1:2, :], vec_ref[2:3, :]
        ys_c = []
        for j in range(SUPER):
            sr_j = _planes_load(sr_ref, j, tc).astype(BF16)
            si_j = _planes_load(si_ref, j, tc).astype(BF16)
            ys_c.append(_dot(sr_j, crt_ref[j]) - _dot(si_j, cit_ref[j]))
        y = jnp.concatenate(ys_c, axis=1) + d_row * u_v
        y_ref[...] = y
        _, _, _, _, ys = _s5_tail(y, glu_ref, glub)
        ms_ref[...] = (ys * _rms(ys) * gs).astype(BF16)

    chunk = lambda w: pl.BlockSpec((tc, w), lambda c: (c, 0))
    states = pl.BlockSpec((tc * STATE_BLOCKS, 128), lambda c: (c, 0))
    sds = jax.ShapeDtypeStruct
    return pl.pallas_call(
        body, grid=(n_pad // tc,), name="s5_fwd",
        in_specs=[chunk(D_SSM), _full((2 * STATE_BLOCKS, 128)), _full((SUPER, 128, 512)), _full((SUPER, 128, 512)),
                  _full((SUPER, 512, 128)), _full((SUPER, 512, 128)), _full((8, D_SSM)), _full((SUPER, 128, 128))],
        out_specs=[states, states, chunk(D_SSM), chunk(D_SSM)],
        out_shape=[sds((n_pad * STATE_BLOCKS, 128), F32), sds((n_pad * STATE_BLOCKS, 128), F32),
                   sds((n_pad, D_SSM), F32), sds((n_pad, D_SSM), BF16)],
        scratch_shapes=[pltpu.VMEM((STATE_BLOCKS, 128), F32), pltpu.VMEM((STATE_BLOCKS, 128), F32)],
        compiler_params=_params(("arbitrary",)),
    )(u, lam, bbr, bbi, crt, cit, vecs, glu)


def _s5_bwd(dms, y, u, sr, si, lam, bbr, bbi, crt, cit, vecs, glu, tc):
    n_pad = u.shape[0]
    nc = n_pad // tc

    def body(dms_ref, y_ref, u_ref, sr_ref, si_ref, pr_ref, pi_ref, lam_ref, bbr_ref, bbi_ref, crt_ref, cit_ref,
             vec_ref, glu_ref, du_ref, dbbr_ref, dbbi_ref, dcrt_ref, dcit_ref, dglu_ref, dvec_ref, dlam_ref,
             qr_s, qi_s, cr_s, ci_s):
        c = pl.program_id(0)

        @pl.when(c == 0)
        def _():
            for ref in (dbbr_ref, dbbi_ref, dcrt_ref, dcit_ref, dglu_ref, dvec_ref, dlam_ref, cr_s, ci_s):
                ref[...] = jnp.zeros_like(ref)

        d_row, glub, gs = vec_ref[0:1, :], vec_ref[1:2, :], vec_ref[2:3, :]
        y_v, u_v = y_ref[...], u_ref[...]
        ub = u_v.astype(BF16)
        g, gb, dgelu, sig, ys = _s5_tail(y_v, glu_ref, glub)
        r = _rms(ys)
        xhat = ys * r
        dm = dms_ref[...]
        dys = _rms_bwd(dm * gs, xhat, r)
        dgate = dys * g * sig * (1.0 - sig)
        dgateb = dgate.astype(BF16)
        dg = dys * sig + jnp.concatenate(
            [_dot_nt(dgateb[:, 128 * j:128 * (j + 1)], glu_ref[j]) for j in range(SUPER)], axis=1)
        dy = dg * dgelu
        dyb = dy.astype(BF16)
        dvec_ref[0:1, :] += jnp.sum(dy * u_v, axis=0, keepdims=True)
        dvec_ref[1:2, :] += jnp.sum(dgate, axis=0, keepdims=True)
        dvec_ref[2:3, :] += jnp.sum(dm * xhat, axis=0, keepdims=True)

        for j in range(SUPER):
            cols = slice(128 * j, 128 * (j + 1))
            dglu_ref[j] += _dot_tn(gb[:, cols], dgateb[:, cols])
            dcrt_ref[j] += _dot_tn(_planes_load(sr_ref, j, tc).astype(BF16), dyb[:, cols])
            dcit_ref[j] -= _dot_tn(_planes_load(si_ref, j, tc).astype(BF16), dyb[:, cols])
            _planes_store(qr_s, j, _dot_nt(dyb[:, cols], crt_ref[j]), tc)
            _planes_store(qi_s, j, -_dot_nt(dyb[:, cols], cit_ref[j]), tc)

        ar = lam_ref[0:STATE_BLOCKS, :]
        ai = lam_ref[STATE_BLOCKS:, :]

        def step(n, carry):
            qr, qi = carry
            o = pl.multiple_of((tc - 1 - n) * STATE_BLOCKS, STATE_BLOCKS)
            nr = ar * qr + ai * qi + qr_s[pl.ds(o, STATE_BLOCKS), :]
            ni = ar * qi - ai * qr + qi_s[pl.ds(o, STATE_BLOCKS), :]
            qr_s[pl.ds(o, STATE_BLOCKS), :] = nr
            qi_s[pl.ds(o, STATE_BLOCKS), :] = ni
            return nr, ni

        qr, qi = lax.fori_loop(0, tc, step, (cr_s[...], ci_s[...]), unroll=4)
        cr_s[...] = qr
        ci_s[...] = qi

        first = c == nc - 1
        row0 = lax.broadcasted_iota(jnp.int32, (tc, 1), 0) == 0
        for k in range(STATE_BLOCKS):
            plane = pl.ds(k, tc, stride=STATE_BLOCKS)
            pr = jnp.where(first, 0.0, pr_ref[k:k + 1, :])
            pi = jnp.where(first, 0.0, pi_ref[k:k + 1, :])
            hpr = jnp.where(row0, pr, pltpu.roll(sr_ref[plane, :], 1, 0))
            hpi = jnp.where(row0, pi, pltpu.roll(si_ref[plane, :], 1, 0))
            q_r, q_i = qr_s[plane, :], qi_s[plane, :]
            dlam_ref[k:k + 1, :] += jnp.sum(q_r * hpr + q_i * hpi, axis=0, keepdims=True)
            dlam_ref[STATE_BLOCKS + k:STATE_BLOCKS + k + 1, :] += jnp.sum(q_i * hpr - q_r * hpi, axis=0, keepdims=True)

        du_c = []
        for j in range(SUPER):
            cols = slice(128 * j, 128 * (j + 1))
            qr_j = _planes_load(qr_s, j, tc).astype(BF16)
            qi_j = _planes_load(qi_s, j, tc).astype(BF16)
            du_c.append(_dot_nt(qr_j, bbr_ref[j]) + _dot_nt(qi_j, bbi_ref[j]))
            dbbr_ref[j] += _dot_tn(ub[:, cols], qr_j)
            dbbi_ref[j] += _dot_tn(ub[:, cols], qi_j)
        du_ref[...] = jnp.concatenate(du_c, axis=1) + dy * d_row

    rev = lambda c: nc - 1 - c
    chunk = lambda w: pl.BlockSpec((tc, w), lambda c: (rev(c), 0))
    states = pl.BlockSpec((tc * STATE_BLOCKS, 128), lambda c: (rev(c), 0))
    prev = pl.BlockSpec((STATE_BLOCKS, 128), lambda c: (jnp.maximum(rev(c) * tc - 1, 0), 0))
    sds = jax.ShapeDtypeStruct
    return pl.pallas_call(
        body, grid=(nc,), name="s5_bwd",
        in_specs=[chunk(D_SSM), chunk(D_SSM), chunk(D_SSM), states, states, prev, prev,
                  _full((2 * STATE_BLOCKS, 128)), _full((SUPER, 128, 512)), _full((SUPER, 128, 512)),
                  _full((SUPER, 512, 128)), _full((SUPER, 512, 128)), _full((8, D_SSM)), _full((SUPER, 128, 128))],
        out_specs=[chunk(D_SSM), _full((SUPER, 128, 512)), _full((SUPER, 128, 512)), _full((SUPER, 512, 128)),
                   _full((SUPER, 512, 128)), _full((SUPER, 128, 128)), _full((8, D_SSM)), _full((2 * STATE_BLOCKS, 128))],
        out_shape=[sds((n_pad, D_SSM), F32), sds((SUPER, 128, 512), F32), sds((SUPER, 128, 512), F32),
                   sds((SUPER, 512, 128), F32), sds((SUPER, 512, 128), F32), sds((SUPER, 128, 128), F32),
                   sds((8, D_SSM), F32), sds((2 * STATE_BLOCKS, 128), F32)],
        scratch_shapes=[pltpu.VMEM((tc * STATE_BLOCKS, 128), F32), pltpu.VMEM((tc * STATE_BLOCKS, 128), F32),
                        pltpu.VMEM((STATE_BLOCKS, 128), F32), pltpu.VMEM((STATE_BLOCKS, 128), F32)],
        compiler_params=_params(("arbitrary",)),
    )(dms, y, u, sr, si, sr, si, lam, bbr, bbi, crt, cit, vecs, glu)


def _inv_count(c_idx, tc, w):
    t = c_idx * tc + lax.broadcasted_iota(jnp.int32, (tc, 1), 0)
    return 1.0 / jnp.minimum(t + 1, w).astype(F32)


def _pool_fwd(v, pw_b, vecs, tc):
    n_pad = v.shape[0]

    def body(v_ref, pw_ref, vec_ref, feat_ref, mp_ref, hist):
        c = pl.program_id(0)

        @pl.when(c == 0)
        def _():
            hist[...] = jnp.zeros_like(hist)

        v_v = v_ref[...]
        ext = jnp.concatenate([hist[...], v_v], axis=0)
        hist[...] = v_v[tc - POOL_HALO:, :]
        feats, ps = [], []
        for k, w in enumerate(POOL_WINDOWS):
            cols = slice(128 * k, 128 * (k + 1))
            s = ext[:, cols]
            sh = 1
            while sh < w:
                s = s + pltpu.roll(s, sh, 0)
                sh *= 2
            f = (s[POOL_HALO:, :] * _inv_count(c, tc, w) - v_v[:, cols]).astype(BF16)
            feats.append(f)
            ps.append(_dot(f, pw_ref[k]))
        feat_ref[...] = jnp.concatenate(feats, axis=1)
        yp = jnp.concatenate(ps, axis=1) * vec_ref[0:1, :]
        mp_ref[...] = (yp * _rms(yp) * vec_ref[1:2, :]).astype(BF16)

    chunk = lambda w: pl.BlockSpec((tc, w), lambda c: (c, 0))
    sds = jax.ShapeDtypeStruct
    return pl.pallas_call(
        body, grid=(n_pad // tc,), name="pool_fwd",
        in_specs=[chunk(D_POOL), _full((4, 128, 128)), _full((8, D_POOL))],
        out_specs=[chunk(D_POOL), chunk(D_POOL)],
        out_shape=[sds((n_pad, D_POOL), BF16), sds((n_pad, D_POOL), BF16)],
        scratch_shapes=[pltpu.VMEM((POOL_HALO, D_POOL), F32)],
        compiler_params=_params(("arbitrary",)),
    )(v, pw_b, vecs)


def _pool_bwd(dmp, feat, pw_b, vecs, tc):
    n_pad = dmp.shape[0]
    nc = n_pad // tc

    def body(dmp_ref, feat_ref, pw_ref, vec_ref, dv_ref, dpw_ref, dvec_ref, fut):
        c = pl.program_id(0)

        @pl.when(c == 0)
        def _():
            fut[...] = jnp.zeros_like(fut)
            dpw_ref[...] = jnp.zeros_like(dpw_ref)
            dvec_ref[...] = jnp.zeros_like(dvec_ref)

        scale, gp = vec_ref[0:1, :], vec_ref[1:2, :]
        feat_v = feat_ref[...]
        p = jnp.concatenate([_dot(feat_v[:, 128 * k:128 * (k + 1)], pw_ref[k]) for k in range(4)], axis=1)
        yp = p * scale
        r = _rms(yp)
        xhat = yp * r
        dm = dmp_ref[...]
        dyp = _rms_bwd(dm * gp, xhat, r)
        dvec_ref[0:1, :] += jnp.sum(dyp * p, axis=0, keepdims=True)
        dvec_ref[1:2, :] += jnp.sum(dm * xhat, axis=0, keepdims=True)
        dpb = (dyp * scale).astype(BF16)
        es, dfs = [], []
        for k, w in enumerate(POOL_WINDOWS):
            cols = slice(128 * k, 128 * (k + 1))
            dpw_ref[k] += _dot_tn(feat_v[:, cols], dpb[:, cols])
            df = _dot_nt(dpb[:, cols], pw_ref[k])
            dfs.append(df)
            es.append(df * _inv_count(nc - 1 - c, tc, w))
        e = jnp.concatenate(es, axis=1)
        ext = jnp.concatenate([e, fut[...]], axis=0)
        fut[...] = e[:POOL_HALO, :]
        n_ext = tc + POOL_HALO
        dvs = []
        for k, w in enumerate(POOL_WINDOWS):
            s = ext[:, 128 * k:128 * (k + 1)]
            sh = 1
            while sh < w:
                s = s + pltpu.roll(s, n_ext - sh, 0)
                sh *= 2
            dvs.append(s[:tc, :] - dfs[k])
        dv_ref[...] = jnp.concatenate(dvs, axis=1)

    chunk = lambda w: pl.BlockSpec((tc, w), lambda c: (nc - 1 - c, 0))
    sds = jax.ShapeDtypeStruct
    return pl.pallas_call(
        body, grid=(nc,), name="pool_bwd",
        in_specs=[chunk(D_POOL), chunk(D_POOL), _full((4, 128, 128)), _full((8, D_POOL))],
        out_specs=[chunk(D_POOL), _full((4, 128, 128)), _full((8, D_POOL))],
        out_shape=[sds((n_pad, D_POOL), F32), sds((4, 128, 128), F32), sds((8, D_POOL), F32)],
        scratch_shapes=[pltpu.VMEM((POOL_HALO, D_POOL), F32)],
        compiler_params=_params(("arbitrary",)),
    )(dmp, feat, pw_b, vecs)


def _place():
    x, y, c = lax.axis_index("x"), lax.axis_index("y"), lax.axis_index("c")
    chips = [(1 - x, y), (x, 1 - y), (1 - x, 1 - y)]
    return x, y, c, chips


_ANY = pl.BlockSpec(memory_space=pl.ANY)


def _cast_shards(shards):
    def body(*refs):
        n = len(refs) // 2
        for i in range(n):
            refs[n + i][...] = refs[i][...].astype(BF16)

    return pl.pallas_call(
        body, name="cast_shards",
        out_shape=[jax.ShapeDtypeStruct(s.shape, BF16) for s in shards],
        compiler_params=_params(),
    )(*shards)


def _gather_shards(halves):
    n = len(halves)

    def body(*refs):
        ins, outs = refs[:n], refs[n:2 * n]
        ici_send, ici_recv, d2d_send, d2d_recv, local = refs[2 * n:]
        x, y, c, chips = _place()
        q = 2 * x + y
        sibling = (x, y, 1 - c)
        mine = [pltpu.make_async_copy(ins[i], outs[i].at[q], local.at[i]) for i in range(n)]
        for cp in mine:
            cp.start()

        def ici(i, j, src, shard, to):
            return pltpu.make_async_remote_copy(src_ref=src, dst_ref=outs[i].at[shard, c], send_sem=ici_send.at[i, j],
                                                recv_sem=ici_recv.at[i, j], device_id=to, device_id_type=MESH)

        def d2d(i, j, shard, half):
            return pltpu.make_async_remote_copy(src_ref=outs[i].at[shard, c], dst_ref=outs[i].at[shard, half],
                                                send_sem=d2d_send.at[i, j], recv_sem=d2d_recv.at[i, j],
                                                device_id=sibling, device_id_type=MESH)

        sends = [ici(i, j, ins[i].at[c], q, (*chip, c)) for i in range(n) for j, chip in enumerate(chips)]
        for cp in sends:
            cp.start()
        passed = []
        for i in range(n):
            for j, (cx, cy) in enumerate(chips):
                ici(i, j, ins[i].at[c], 2 * cx + cy, (cx, cy, c)).wait_recv()
                cp = d2d(i, j, 2 * cx + cy, c)
                cp.start()
                passed.append(cp)
        for i in range(n):
            for j, (cx, cy) in enumerate(chips):
                d2d(i, j, 2 * cx + cy, 1 - c).wait_recv()
        for cp in sends + passed:
            cp.wait_send()
        for cp in mine:
            cp.wait()

    return pl.pallas_call(
        body, name="gather_shards",
        in_specs=[_ANY] * n, out_specs=[_ANY] * n,
        out_shape=[jax.ShapeDtypeStruct((N_SHARD,) + h.shape, h.dtype) for h in halves],
        scratch_shapes=[pltpu.SemaphoreType.DMA((n, 3)), pltpu.SemaphoreType.DMA((n, 3)),
                        pltpu.SemaphoreType.DMA((n, 3)), pltpu.SemaphoreType.DMA((n, 3)),
                        pltpu.SemaphoreType.DMA((n,))],
    )(*halves)


def _swap_halves(grads):
    n = len(grads)

    def body(*refs):
        ins, outs = refs[:n], refs[n:2 * n]
        send, recv = refs[2 * n:]
        x, y, c, _ = _place()
        cps = [pltpu.make_async_remote_copy(src_ref=ins[i].at[:, 1 - c], dst_ref=outs[i], send_sem=send.at[i],
                                            recv_sem=recv.at[i], device_id=(x, y, 1 - c), device_id_type=MESH)
               for i in range(n)]
        for cp in cps:
            cp.start()
        for cp in cps:
            cp.wait()

    return pl.pallas_call(
        body, name="swap_halves",
        in_specs=[_ANY] * n, out_specs=[_ANY] * n,
        out_shape=[jax.ShapeDtypeStruct((N_SHARD,) + g.shape[2:], F32) for g in grads],
        scratch_shapes=[pltpu.SemaphoreType.DMA((n,)), pltpu.SemaphoreType.DMA((n,))],
    )(*grads)


def _exchange_chips(parts):
    n = len(parts)

    def body(*refs):
        ins, outs = refs[:n], refs[n:2 * n]
        send, recv = refs[2 * n:]
        x, y, c, chips = _place()
        cps = [pltpu.make_async_remote_copy(src_ref=ins[i].at[2 * cx + cy], dst_ref=outs[i].at[j], send_sem=send.at[i, j],
                                            recv_sem=recv.at[i, j], device_id=(cx, cy, c), device_id_type=MESH)
               for i in range(n) for j, (cx, cy) in enumerate(chips)]
        for cp in cps:
            cp.start()
        for cp in cps:
            cp.wait()

    return pl.pallas_call(
        body, name="exchange_chips",
        in_specs=[_ANY] * n, out_specs=[_ANY] * n,
        out_shape=[jax.ShapeDtypeStruct((3,) + p.shape[1:], BF16) for p in parts],
        scratch_shapes=[pltpu.SemaphoreType.DMA((n, 3)), pltpu.SemaphoreType.DMA((n, 3))],
    )(*parts)


def _join_halves(halves):
    n = len(halves)

    def body(*refs):
        ins, outs = refs[:n], refs[n:2 * n]
        send, recv, local = refs[2 * n:]
        x, y, c, _ = _place()
        mine = [pltpu.make_async_copy(ins[i], outs[i].at[c], local.at[i]) for i in range(n)]
        cps = [pltpu.make_async_remote_copy(src_ref=ins[i], dst_ref=outs[i].at[c], send_sem=send.at[i],
                                            recv_sem=recv.at[i], device_id=(x, y, 1 - c), device_id_type=MESH)
               for i in range(n)]
        for cp in mine + cps:
            cp.start()
        for i in range(n):
            cps[i].wait_send()
            pltpu.make_async_remote_copy(src_ref=ins[i], dst_ref=outs[i].at[1 - c], send_sem=send.at[i],
                                         recv_sem=recv.at[i], device_id=(x, y, 1 - c), device_id_type=MESH).wait_recv()
        for cp in mine:
            cp.wait()

    return pl.pallas_call(
        body, name="join_halves",
        in_specs=[_ANY] * n, out_specs=[_ANY] * n,
        out_shape=[jax.ShapeDtypeStruct((2,) + h.shape, F32) for h in halves],
        scratch_shapes=[pltpu.SemaphoreType.DMA((n,)), pltpu.SemaphoreType.DMA((n,)), pltpu.SemaphoreType.DMA((n,))],
    )(*halves)


def _gather_small(part):
    m_per, n = part.shape

    def body(x_ref, out_ref, send_sems, recv_sems, local_sem):
        x, y, c, chips = _place()
        me, sibling = (x, y, c), (x, y, 1 - c)

        def rows(px, py, pc):
            return out_ref.at[pl.ds((4 * px + 2 * py + pc) * m_per, m_per), :]

        def copy(k, block, to, src=None):
            return pltpu.make_async_remote_copy(src_ref=rows(*block) if src is None else src, dst_ref=rows(*block),
                                                send_sem=send_sems.at[k], recv_sem=recv_sems.at[k],
                                                device_id=to, device_id_type=MESH)

        mine = pltpu.make_async_copy(x_ref, rows(*me), local_sem)
        mine.start()
        first = [copy(0, me, sibling, src=x_ref)]
        first += [copy(1 + j, me, (*chip, c), src=x_ref) for j, chip in enumerate(chips)]
        for cp in first:
            cp.start()
        passed = [copy(4 + j, (*chip, c), sibling) for j, chip in enumerate(chips)]
        for j, chip in enumerate(chips):
            copy(1 + j, (*chip, c), me).wait_recv()
            passed[j].start()
        copy(0, sibling, me).wait_recv()
        for j, chip in enumerate(chips):
            copy(4 + j, (*chip, 1 - c), me).wait_recv()
        for cp in first + passed:
            cp.wait_send()
        mine.wait()

    return pl.pallas_call(
        body, name="gather_small",
        out_shape=jax.ShapeDtypeStruct((8 * m_per, n), F32),
        in_specs=[pl.BlockSpec(memory_space=pltpu.VMEM)], out_specs=pl.BlockSpec(memory_space=pltpu.VMEM),
        scratch_shapes=[pltpu.SemaphoreType.DMA((7,)), pltpu.SemaphoreType.DMA((7,)), pltpu.SemaphoreType.DMA],
        compiler_params=_params(),
    )(part)


N_SPLIT = 2


def _sum_siblings(grads, recvd, place):
    n = len(grads)

    def body(place_ref, *refs):
        g_refs, r_refs, sb_refs, own_refs = (refs[k * n:(k + 1) * n] for k in range(4))
        s = pl.program_id(1)
        for i in range(n):
            tot = g_refs[i][0, 0] + r_refs[i][0]
            sb_refs[i][0] = tot.astype(BF16)

            @pl.when(s == place_ref[0])
            def _():
                own_refs[i][...] = tot

    in_specs, sb_specs, own_specs, sb_shapes, own_shapes = [], [], [], [], []
    for g in grads:
        _, _, r, cdim = g.shape
        rb = r // N_SPLIT
        in_specs.append(pl.BlockSpec((1, 1, rb, cdim), lambda b, s, p: (s, p[1], b, 0)))
        sb_specs.append(pl.BlockSpec((1, rb, cdim), lambda b, s, p: (s, b, 0)))
        own_specs.append(pl.BlockSpec((rb, cdim), lambda b, s, p: (b, 0)))
        sb_shapes.append(jax.ShapeDtypeStruct((N_SHARD, r, cdim), BF16))
        own_shapes.append(jax.ShapeDtypeStruct((r, cdim), F32))
    out = pl.pallas_call(
        body, name="sum_siblings",
        grid_spec=pltpu.PrefetchScalarGridSpec(
            num_scalar_prefetch=1, grid=(N_SPLIT, N_SHARD),
            in_specs=in_specs + sb_specs, out_specs=sb_specs + own_specs),
        out_shape=sb_shapes + own_shapes,
        compiler_params=_params(("parallel", "arbitrary")),
    )(place, *grads, *recvd)
    return out[:n], out[n:]


def _sum_chips(own, recvd):
    n = len(own)

    def body(*refs):
        o_refs, r_refs, out_refs = (refs[k * n:(k + 1) * n] for k in range(3))
        for i in range(n):
            tot = o_refs[i][...]
            for j in range(3):
                tot = tot + r_refs[i][j].astype(F32)
            out_refs[i][...] = tot

    o_specs, r_specs = [], []
    for o in own:
        r, cdim = o.shape
        rb = r // N_SPLIT
        o_specs.append(pl.BlockSpec((rb, cdim), lambda b: (b, 0)))
        r_specs.append(pl.BlockSpec((3, rb, cdim), lambda b: (0, b, 0)))
    return pl.pallas_call(
        body, name="sum_chips", grid=(N_SPLIT,),
        in_specs=o_specs + r_specs, out_specs=o_specs,
        out_shape=[jax.ShapeDtypeStruct(o.shape, F32) for o in own],
        compiler_params=_params(("parallel",)),
    )(*own, *recvd)


def _adamw_math(w, g, m, v):
    m = ADAM_B1 * m + (1.0 - ADAM_B1) * g
    v = ADAM_B2 * v + (1.0 - ADAM_B2) * (g * g)
    m_hat = m / (1.0 - ADAM_B1 ** ADAM_STEP)
    v_hat = v / (1.0 - ADAM_B2 ** ADAM_STEP)
    delta = -ADAM_LR * (m_hat / (jnp.sqrt(v_hat) + ADAM_EPS) + ADAM_WD * w)
    return delta, m, v


def _adamw(name, ws, gs, ms, vs, n_split):
    n = len(ws)

    def body(*refs):
        w_r, g_r, m_r, v_r, d_o, m_o, v_o = (refs[k * n:(k + 1) * n] for k in range(7))
        for i in range(n):
            d, m, v = _adamw_math(w_r[i][...], g_r[i][...], m_r[i][...], v_r[i][...])
            d_o[i][...] = d
            m_o[i][...] = m
            v_o[i][...] = v

    specs = [pl.BlockSpec((w.shape[0] // n_split, w.shape[1]), lambda b: (b, 0)) for w in ws]
    shapes = [jax.ShapeDtypeStruct(w.shape, F32) for w in ws]
    out = pl.pallas_call(
        body, name=name, grid=(n_split,),
        in_specs=specs * 4, out_specs=specs * 3, out_shape=shapes * 3,
        compiler_params=_params(("parallel",)),
    )(*ws, *gs, *ms, *vs)
    return out[:n], out[n:2 * n], out[2 * n:]


def _reduce_small(gathered, w, m, v):
    rows = w.shape[0]

    def body(ga_ref, w_ref, m_ref, v_ref, g_out, d_out, m_out, v_out):
        g = ga_ref[0:rows, :]
        for k in range(1, 8):
            g = g + ga_ref[k * rows:(k + 1) * rows, :]
        g_out[...] = g
        d, mm, vv = _adamw_math(w_ref[...], g, m_ref[...], v_ref[...])
        d_out[...] = d
        m_out[...] = mm
        v_out[...] = vv

    return pl.pallas_call(
        body, name="reduce_small",
        out_shape=[jax.ShapeDtypeStruct(w.shape, F32)] * 4,
        compiler_params=_params(),
    )(gathered, w, m, v)


def _s5_operands(lam_re, lam_im, log_step, b_re, b_im, c_re, c_im, glu_w):
    lr = jnp.minimum(lam_re, -1e-4)
    li = lam_im
    step = jnp.exp(log_step)[:, None]
    mag = jnp.exp(lr * step)
    ang = li * step
    abr = mag * jnp.cos(ang)
    abi = mag * jnp.sin(ang)
    nr = abr - 1.0
    ni = abi
    den = lr * lr + li * li
    cr = ((nr * lr + ni * li) / den)[..., None]
    ci = ((ni * lr - nr * li) / den)[..., None]
    bbr = cr * b_re - ci * b_im
    bbi = cr * b_im + ci * b_re
    eye = jnp.eye(8, dtype=F32)
    g, h, p = SSM_GROUPS // SUPER, SSM_GROUP, SSM_STATE

    def b_layout(t):
        return jnp.einsum("ab,japh->jahbp", eye, t.reshape(SUPER, g, p, h)).reshape(SUPER, g * h, g * p)

    def c_layout(t):
        return jnp.einsum("ab,jahp->jbpah", eye, t.reshape(SUPER, g, h, p)).reshape(SUPER, g * p, g * h)

    glu = jnp.einsum("ab,jahk->jahbk", eye, glu_w.reshape(SUPER, g, h, h)).reshape(SUPER, g * h, g * h)
    lam = jnp.concatenate([abr.reshape(STATE_BLOCKS, 128), abi.reshape(STATE_BLOCKS, 128)], axis=0)
    return lam, b_layout(bbr), b_layout(bbi), c_layout(c_re), c_layout(c_im), glu


def _pad_rows(a, rows):
    return jnp.pad(a, ((0, rows - a.shape[0]), (0, 0)))


def _pack(parts):
    rows = []
    for a in parts:
        flat = a.reshape(-1)
        n = -(-flat.shape[0] // 128)
        rows.append(jnp.pad(flat, (0, n * 128 - flat.shape[0])).reshape(n, 128))
    out = jnp.concatenate(rows, axis=0)
    return _pad_rows(out, -(-out.shape[0] // 8) * 8)


def _unpack(packed, like):
    out, at = [], 0
    for a in like:
        n = -(-a.size // 128)
        out.append(packed[at:at + n].reshape(-1)[:a.size].reshape(a.shape))
        at += n
    return out


SMALL = ("norm1_g", "ssm_lambda_re", "ssm_lambda_im", "ssm_log_step", "ssm_b_re", "ssm_b_im", "ssm_c_re", "ssm_c_im",
         "ssm_d", "ssm_glu_w", "ssm_glu_b", "ssm_norm_g", "pool_w", "pool_scale", "pool_norm_g", "norm2_g",
         "final_norm_g")
LARGE = ("w_in", "w_out", "w_gate", "w_up", "w_down")
WEIGHTS = ("meta_tokens", "norm1_g", "w_in", "ssm_lambda_re", "ssm_lambda_im", "ssm_log_step", "ssm_b_re", "ssm_b_im",
           "ssm_c_re", "ssm_c_im", "ssm_d", "ssm_glu_w", "ssm_glu_b", "ssm_norm_g", "pool_w", "pool_scale",
           "pool_norm_g", "w_out", "norm2_g", "w_gate", "w_up", "w_down", "final_norm_g")


def _step(x, target, w, m, v):
    seq = x.shape[1]
    n_rows = N_META + seq
    n_pad, tm, tc = _plan(n_rows)
    xq, yq, cq = lax.axis_index("x"), lax.axis_index("y"), lax.axis_index("c")
    place = jnp.stack([2 * xq + yq, cq]).astype(jnp.int32)

    def halves(a2d):
        return a2d.reshape(2, a2d.shape[0] // 2, a2d.shape[1])

    shards = [halves(w[k][0]) for k in LARGE]
    shards_b = _cast_shards(shards)
    full = _gather_shards(list(shards_b) + [halves(w["meta_tokens"])])
    w_in_b = full[0].reshape(D_MODEL, D_MODEL)
    w_out_b = full[1].reshape(D_MODEL, D_MODEL)
    wg_b = full[2].reshape(N_SHARD, D_MODEL, FF_SHARD)
    wu_b = full[3].reshape(N_SHARD, D_MODEL, FF_SHARD)
    wd_b = full[4].reshape(N_SHARD, FF_SHARD, D_MODEL)
    meta = full[5].reshape(N_SHARD, N_META, D_MODEL // N_SHARD).transpose(1, 0, 2).reshape(N_META, D_MODEL)

    h0 = _pad_rows(jnp.concatenate([meta, x[0]], axis=0), n_pad)
    tgt = _pad_rows(jnp.concatenate([jnp.zeros((N_META, D_MODEL), F32), target[0]], axis=0), n_pad)
    s5_in = (w["ssm_lambda_re"][0], w["ssm_lambda_im"][0], w["ssm_log_step"][0], w["ssm_b_re"][0], w["ssm_b_im"][0],
             w["ssm_c_re"][0], w["ssm_c_im"][0], w["ssm_glu_w"][0])
    (lam, bbr, bbi, crt, cit, glu), s5_vjp = jax.vjp(_s5_operands, *s5_in)
    bbr_b, bbi_b, crt_b, cit_b, glu_b16 = (t.astype(BF16) for t in (bbr, bbi, crt, cit, glu))
    s5_vecs = _pad_rows(jnp.concatenate([w["ssm_d"].reshape(1, D_SSM), w["ssm_glu_b"].reshape(1, D_SSM),
                                         w["ssm_norm_g"].reshape(1, D_SSM)], axis=0), 8)
    pool_vecs = _pad_rows(jnp.concatenate([w["pool_scale"].reshape(1, D_POOL), w["pool_norm_g"].reshape(1, D_POOL)],
                                          axis=0), 8)
    pw_b = w["pool_w"][0].astype(BF16)
    g1, g2, gf = w["norm1_g"].reshape(1, D_MODEL), w["norm2_g"].reshape(1, D_MODEL), w["final_norm_g"].reshape(1, D_MODEL)

    u, vv = _fwd_in(h0, g1, w_in_b, tm)
    sr, si, y, ms = _s5_fwd(u, lam, bbr_b, bbi_b, crt_b, cit_b, s5_vecs, glu_b16, tc)
    feat, mp = _pool_fwd(vv, pw_b, pool_vecs, tc)
    h1, n2, a, b, ff, dh2, loss_acc, dgf = _fwd_ffn(h0, ms, mp, w_out_b, g2, wg_b, wu_b, wd_b, gf, tgt, tm, n_rows)
    loss = lax.psum(loss_acc[0, 0], AXES)

    da, db, dh1, dg2 = _bwd_ffn(dh2, a, b, wg_b, wu_b, wd_b, h1, g2, tm)
    dwg, dwu, dwd = _grad_ffn(n2, da, db, ff, dh2, tm)
    dms, dmp, dwo = _bwd_out(dh1, ms, mp, w_out_b, tm)
    du, dbbr, dbbi, dcrt, dcit, dglu, ds5v, dlam = _s5_bwd(dms, y, u, sr, si, lam, bbr_b, bbi_b, crt_b, cit_b,
                                                           s5_vecs, glu_b16, tc)
    dv, dpw, dpoolv = _pool_bwd(dmp, feat, pw_b, pool_vecs, tc)
    dh0, dwi, dg1 = _bwd_in(du, dv, h0, dh1, g1, w_in_b, tm)
    d_lre, d_lim, d_lstep, d_bre, d_bim, d_cre, d_cim, d_gluw = s5_vjp((dlam, dbbr, dbbi, dcrt, dcit, dglu))
    grad_x = dh0[N_META:n_rows][None]

    def quarters(t):
        if t.ndim == 2:
            t = t.reshape(N_SHARD, t.shape[0] // N_SHARD, t.shape[1])
        return t.reshape(N_SHARD, 2, t.shape[1] // 2, t.shape[2])

    big = [quarters(t) for t in (dwi, dwo, dwg, dwu, dwd)]
    from_sibling = _swap_halves(big)
    parts_b, own = _sum_siblings(big, from_sibling, place)
    from_chips = _exchange_chips(parts_b)
    reduced_half = _sum_chips(own, from_chips)
    joined = _join_halves(reduced_half)
    g_large = [j.reshape(j.shape[0] * j.shape[1], j.shape[2]) for j in joined]
    w2d = [w[k][0] for k in LARGE]
    m2d = [m[k][0] for k in LARGE]
    v2d = [v[k][0] for k in LARGE]
    d_large, m_large, v_large = _adamw("adamw_large", w2d, g_large, m2d, v2d, 8)

    small_g = {
        "norm1_g": dg1, "ssm_lambda_re": d_lre, "ssm_lambda_im": d_lim, "ssm_log_step": d_lstep, "ssm_b_re": d_bre,
        "ssm_b_im": d_bim, "ssm_c_re": d_cre, "ssm_c_im": d_cim, "ssm_d": ds5v[0], "ssm_glu_w": d_gluw,
        "ssm_glu_b": ds5v[1], "ssm_norm_g": ds5v[2], "pool_w": dpw, "pool_scale": dpoolv[0], "pool_norm_g": dpoolv[1],
        "norm2_g": dg2, "final_norm_g": dgf,
    }
    like = [w[k] for k in SMALL]
    packed_g = _pack([small_g[k].reshape(w[k].shape) for k in SMALL] + [dh0[:N_META]])
    rows = packed_g.shape[0]
    packed = lambda t: _pad_rows(_pack([t[k] for k in SMALL]), rows)
    gathered = _gather_small(packed_g)
    g_pk, d_pk, m_pk, v_pk = _reduce_small(gathered, packed(w), packed(m), packed(v))
    g_small = _unpack(g_pk, like + [jax.ShapeDtypeStruct((N_META, D_MODEL), F32)])
    d_small, m_small, v_small = (_unpack(t, like) for t in (d_pk, m_pk, v_pk))
    q = place[0]
    g_meta = lax.dynamic_slice_in_dim(g_small[-1], q * (D_MODEL // N_SHARD), D_MODEL // N_SHARD, axis=1)
    d_meta, m_meta, v_meta = _adamw("adamw_meta", [w["meta_tokens"]], [g_meta], [m["meta_tokens"]],
                                    [v["meta_tokens"]], 1)

    grads, deltas, new_m, new_v = {}, {}, {}, {}
    for i, k in enumerate(SMALL):
        grads[k], deltas[k], new_m[k], new_v[k] = g_small[i], d_small[i], m_small[i], v_small[i]
    for i, k in enumerate(LARGE):
        shape = w[k].shape
        grads[k], deltas[k], new_m[k], new_v[k] = (t.reshape(shape) for t in
                                                   (g_large[i], d_large[i], m_large[i], v_large[i]))
    grads["meta_tokens"], deltas["meta_tokens"] = g_meta, d_meta[0]
    new_m["meta_tokens"], new_v["meta_tokens"] = m_meta[0], v_meta[0]
    return (loss, grad_x, *[grads[k] for k in WEIGHTS], *[deltas[k] for k in WEIGHTS],
            *[new_m[k] for k in WEIGHTS], *[new_v[k] for k in WEIGHTS])


def kernel(x, meta_tokens, norm1_g, w_in, ssm_lambda_re, ssm_lambda_im, ssm_log_step, ssm_b_re, ssm_b_im, ssm_c_re, ssm_c_im, ssm_d, ssm_glu_w, ssm_glu_b, ssm_norm_g, pool_w, pool_scale, pool_norm_g, w_out, norm2_g, w_gate, w_up, w_down, final_norm_g, loss_target, m_meta_tokens, m_norm1_g, m_w_in, m_ssm_lambda_re, m_ssm_lambda_im, m_ssm_log_step, m_ssm_b_re, m_ssm_b_im, m_ssm_c_re, m_ssm_c_im, m_ssm_d, m_ssm_glu_w, m_ssm_glu_b, m_ssm_norm_g, m_pool_w, m_pool_scale, m_pool_norm_g, m_w_out, m_norm2_g, m_w_gate, m_w_up, m_w_down, m_final_norm_g, v_meta_tokens, v_norm1_g, v_w_in, v_ssm_lambda_re, v_ssm_lambda_im, v_ssm_log_step, v_ssm_b_re, v_ssm_b_im, v_ssm_c_re, v_ssm_c_im, v_ssm_d, v_ssm_glu_w, v_ssm_glu_b, v_ssm_norm_g, v_pool_w, v_pool_scale, v_pool_norm_g, v_w_out, v_norm2_g, v_w_gate, v_w_up, v_w_down, v_final_norm_g):
    w = dict(meta_tokens=meta_tokens, norm1_g=norm1_g, w_in=w_in, ssm_lambda_re=ssm_lambda_re, ssm_lambda_im=ssm_lambda_im, ssm_log_step=ssm_log_step, ssm_b_re=ssm_b_re, ssm_b_im=ssm_b_im, ssm_c_re=ssm_c_re, ssm_c_im=ssm_c_im, ssm_d=ssm_d, ssm_glu_w=ssm_glu_w, ssm_glu_b=ssm_glu_b, ssm_norm_g=ssm_norm_g, pool_w=pool_w, pool_scale=pool_scale, pool_norm_g=pool_norm_g, w_out=w_out, norm2_g=norm2_g, w_gate=w_gate, w_up=w_up, w_down=w_down, final_norm_g=final_norm_g)
    m = dict(meta_tokens=m_meta_tokens, norm1_g=m_norm1_g, w_in=m_w_in, ssm_lambda_re=m_ssm_lambda_re, ssm_lambda_im=m_ssm_lambda_im, ssm_log_step=m_ssm_log_step, ssm_b_re=m_ssm_b_re, ssm_b_im=m_ssm_b_im, ssm_c_re=m_ssm_c_re, ssm_c_im=m_ssm_c_im, ssm_d=m_ssm_d, ssm_glu_w=m_ssm_glu_w, ssm_glu_b=m_ssm_glu_b, ssm_norm_g=m_ssm_norm_g, pool_w=m_pool_w, pool_scale=m_pool_scale, pool_norm_g=m_pool_norm_g, w_out=m_w_out, norm2_g=m_norm2_g, w_gate=m_w_gate, w_up=m_w_up, w_down=m_w_down, final_norm_g=m_final_norm_g)
    v = dict(meta_tokens=v_meta_tokens, norm1_g=v_norm1_g, w_in=v_w_in, ssm_lambda_re=v_ssm_lambda_re, ssm_lambda_im=v_ssm_lambda_im, ssm_log_step=v_ssm_log_step, ssm_b_re=v_ssm_b_re, ssm_b_im=v_ssm_b_im, ssm_c_re=v_ssm_c_re, ssm_c_im=v_ssm_c_im, ssm_d=v_ssm_d, ssm_glu_w=v_ssm_glu_w, ssm_glu_b=v_ssm_glu_b, ssm_norm_g=v_ssm_norm_g, pool_w=v_pool_w, pool_scale=v_pool_scale, pool_norm_g=v_pool_norm_g, w_out=v_w_out, norm2_g=v_norm2_g, w_gate=v_w_gate, w_up=v_w_up, w_down=v_w_down, final_norm_g=v_final_norm_g)
    return _step(x, loss_target, w, m, v)
```

```python
import functools
import math

import jax
import jax.numpy as jnp
from jax import lax
from jax.experimental import pallas as pl
from jax.experimental.pallas import tpu as pltpu

F32 = jnp.float32
BF16 = jnp.bfloat16
MESH = pl.DeviceIdType.MESH
AXES = ("x", "y", "c")

D_MODEL = 1024
D_SSM = 512
D_POOL = 512
N_META = 16
SSM_GROUP = 16
SSM_GROUPS = 32
SSM_STATE = 64
N_STATE = SSM_GROUPS * SSM_STATE
STATE_BLOCKS = N_STATE // 128
SUPER = 4
POOL_WINDOWS = (2, 4, 8, 16)
POOL_HALO = 16
D_FF = 2816
N_SHARD = 4
FF_SHARD = D_FF // N_SHARD
EPS = 1e-6
ADAM_LR, ADAM_B1, ADAM_B2, ADAM_EPS, ADAM_WD, ADAM_STEP = 0.001, 0.9, 0.999, 1e-08, 0.01, 10
VMEM_LIMIT = 56 * 1024 * 1024


def _plan(n_rows):
    if n_rows > 2048:
        tm, tc = 416, 320
    else:
        tm, tc = 128, 64
    step = tm * tc // math.gcd(tm, tc)
    return -(-n_rows // step) * step, tm, tc


def _params(sem=None):
    return pltpu.CompilerParams(dimension_semantics=sem, vmem_limit_bytes=VMEM_LIMIT)


def _dot(a, b):
    return jnp.dot(a, b, preferred_element_type=F32)


def _dot_nt(a, b):
    return lax.dot_general(a, b, (((1,), (1,)), ((), ())), preferred_element_type=F32)


def _dot_tn(a, b):
    return lax.dot_general(a, b, (((0,), (0,)), ((), ())), preferred_element_type=F32)


def _sigmoid(x):
    return 1.0 / (1.0 + jnp.exp(-x))


_GELU_C = math.sqrt(2.0 / math.pi)


def _gelu_and_grad(y):
    y2 = y * y
    t = jnp.tanh(_GELU_C * (y + 0.044715 * y * y2))
    g = 0.5 * y * (1.0 + t)
    dg = 0.5 * (1.0 + t) + 0.5 * y * (1.0 - t * t) * (_GELU_C * (1.0 + 3.0 * 0.044715 * y2))
    return g, dg


def _rms(x):
    return lax.rsqrt(jnp.mean(x * x, axis=-1, keepdims=True) + EPS)


def _rms_bwd(dn, xhat, r):
    return r * (dn - xhat * jnp.mean(dn * xhat, axis=-1, keepdims=True))


def _full(shape):
    nd = len(shape)
    return pl.BlockSpec(shape, lambda *_: (0,) * nd)


def _fwd_in(h0, g1, w_in_b, tm):
    n_pad = h0.shape[0]

    def body(h_ref, g_ref, w_ref, u_ref, v_ref):
        h = h_ref[...]
        n1 = (h * _rms(h) * g_ref[...]).astype(BF16)
        proj = _dot(n1, w_ref[...])
        u_ref[...] = proj[:, :D_SSM]
        v_ref[...] = proj[:, D_SSM:]

    row = lambda w: pl.BlockSpec((tm, w), lambda i: (i, 0))
    return pl.pallas_call(
        body, grid=(n_pad // tm,), name="fwd_in",
        in_specs=[row(D_MODEL), _full((1, D_MODEL)), _full((D_MODEL, D_MODEL))],
        out_specs=[row(D_SSM), row(D_POOL)],
        out_shape=[jax.ShapeDtypeStruct((n_pad, D_SSM), F32), jax.ShapeDtypeStruct((n_pad, D_POOL), F32)],
        compiler_params=_params(("parallel",)),
    )(h0, g1, w_in_b)


def _fwd_ffn(h0, ms, mp, w_out_b, g2, wg_b, wu_b, wd_b, gf, target, tm, n_valid):
    n_pad = h0.shape[0]
    nt = n_pad // tm

    def body(h0_ref, ms_ref, mp_ref, wo_ref, g2_ref, wg_ref, wu_ref, wd_ref, gf_ref, tgt_ref,
             h1_ref, n2_ref, a_ref, b_ref, ff_ref, dh2_ref, loss_ref, dgf_ref, acc):
        i, q = pl.program_id(0), pl.program_id(1)

        @pl.when((i == 0) & (q == 0))
        def _():
            loss_ref[...] = jnp.zeros_like(loss_ref)
            dgf_ref[...] = jnp.zeros_like(dgf_ref)

        @pl.when(q == 0)
        def _():
            h1 = h0_ref[...] + _dot(ms_ref[...], wo_ref[:D_SSM, :]) + _dot(mp_ref[...], wo_ref[D_SSM:, :])
            h1_ref[...] = h1
            acc[...] = h1
            n2_ref[...] = (h1 * _rms(h1) * g2_ref[...]).astype(BF16)

        n2 = n2_ref[...]
        a = _dot_nt(n2, wg_ref[0])
        b = _dot_nt(n2, wu_ref[0])
        a_ref[0] = a
        b_ref[0] = b
        ff = (a * _sigmoid(a) * b).astype(BF16)
        ff_ref[0] = ff
        acc[...] += _dot(ff, wd_ref[0])

        @pl.when(q == N_SHARD - 1)
        def _():
            h2 = acc[...]
            r = _rms(h2)
            xhat = h2 * r
            gf_row = gf_ref[...]
            rows = i * tm + lax.broadcasted_iota(jnp.int32, (tm, 1), 0)
            valid = (rows >= N_META) & (rows < n_valid)
            diff = jnp.where(valid, xhat * gf_row - tgt_ref[...], 0.0)
            loss_ref[...] += jnp.full(loss_ref.shape, 0.5 / D_MODEL, F32) * jnp.sum(diff * diff)
            dout = diff * (1.0 / D_MODEL)
            dgf_ref[...] += jnp.sum(dout * xhat, axis=0, keepdims=True)
            dh2_ref[...] = _rms_bwd(dout * gf_row, xhat, r)

    row = lambda w: pl.BlockSpec((tm, w), lambda i, q: (i, 0))
    shard_rows = pl.BlockSpec((1, FF_SHARD, D_MODEL), lambda i, q: (q, 0, 0))
    act = pl.BlockSpec((1, tm, FF_SHARD), lambda i, q: (q, i, 0))
    sds = jax.ShapeDtypeStruct
    return pl.pallas_call(
        body, grid=(nt, N_SHARD), name="fwd_ffn",
        in_specs=[row(D_MODEL), row(D_SSM), row(D_POOL), _full((D_MODEL, D_MODEL)), _full((1, D_MODEL)),
                  shard_rows, shard_rows, shard_rows, _full((1, D_MODEL)), row(D_MODEL)],
        out_specs=[row(D_MODEL), row(D_MODEL), act, act, act, row(D_MODEL), _full((8, 128)), _full((1, D_MODEL))],
        out_shape=[sds((n_pad, D_MODEL), F32), sds((n_pad, D_MODEL), BF16),
                   sds((N_SHARD, n_pad, FF_SHARD), F32), sds((N_SHARD, n_pad, FF_SHARD), F32),
                   sds((N_SHARD, n_pad, FF_SHARD), BF16), sds((n_pad, D_MODEL), F32),
                   sds((8, 128), F32), sds((1, D_MODEL), F32)],
        scratch_shapes=[pltpu.VMEM((tm, D_MODEL), F32)],
        compiler_params=_params(("arbitrary", "arbitrary")),
    )(h0, ms, mp, w_out_b, g2, wg_b, wu_b, wd_b, gf, target)


def _bwd_ffn(dh2, a, b, wg_b, wu_b, wd_b, h1, g2, tm):
    n_pad = dh2.shape[0]

    def body(dh2_ref, a_ref, b_ref, wg_ref, wu_ref, wd_ref, h1_ref, g2_ref, da_ref, db_ref, dh1_ref, dg2_ref, acc):
        i, q = pl.program_id(0), pl.program_id(1)

        @pl.when((i == 0) & (q == 0))
        def _():
            dg2_ref[...] = jnp.zeros_like(dg2_ref)

        dff = _dot_nt(dh2_ref[...].astype(BF16), wd_ref[0])
        a_v, b_v = a_ref[0], b_ref[0]
        sig = _sigmoid(a_v)
        da = (dff * b_v * sig * (1.0 + a_v * (1.0 - sig))).astype(BF16)
        db = (dff * a_v * sig).astype(BF16)
        da_ref[0] = da
        db_ref[0] = db
        part = _dot(da, wg_ref[0]) + _dot(db, wu_ref[0])

        @pl.when(q == 0)
        def _():
            acc[...] = part

        @pl.when(q > 0)
        def _():
            acc[...] += part

        @pl.when(q == N_SHARD - 1)
        def _():
            h1 = h1_ref[...]
            r = _rms(h1)
            xhat = h1 * r
            dn2 = acc[...]
            dg2_ref[...] += jnp.sum(dn2 * xhat, axis=0, keepdims=True)
            dh1_ref[...] = dh2_ref[...] + _rms_bwd(dn2 * g2_ref[...], xhat, r)

    row = lambda w: pl.BlockSpec((tm, w), lambda i, q: (i, 0))
    shard_rows = pl.BlockSpec((1, FF_SHARD, D_MODEL), lambda i, q: (q, 0, 0))
    act = pl.BlockSpec((1, tm, FF_SHARD), lambda i, q: (q, i, 0))
    sds = jax.ShapeDtypeStruct
    return pl.pallas_call(
        body, grid=(n_pad // tm, N_SHARD), name="bwd_ffn",
        in_specs=[row(D_MODEL), act, act, shard_rows, shard_rows, shard_rows, row(D_MODEL), _full((1, D_MODEL))],
        out_specs=[act, act, row(D_MODEL), _full((1, D_MODEL))],
        out_shape=[sds((N_SHARD, n_pad, FF_SHARD), BF16), sds((N_SHARD, n_pad, FF_SHARD), BF16),
                   sds((n_pad, D_MODEL), F32), sds((1, D_MODEL), F32)],
        scratch_shapes=[pltpu.VMEM((tm, D_MODEL), F32)],
        compiler_params=_params(("arbitrary", "arbitrary")),
    )(dh2, a, b, wg_b, wu_b, wd_b, h1, g2)


def _grad_ffn(n2, da, db, ff, dh2, tm):
    n_pad = n2.shape[0]

    def body(n2_ref, da_ref, db_ref, ff_ref, dh2_ref, dwg_ref, dwu_ref, dwd_ref):
        i = pl.program_id(1)
        n2_v = n2_ref[...]
        gg = _dot_tn(da_ref[0], n2_v)
        gu = _dot_tn(db_ref[0], n2_v)
        gd = _dot_tn(ff_ref[0], dh2_ref[...].astype(BF16))

        @pl.when(i == 0)
        def _():
            dwg_ref[0] = gg
            dwu_ref[0] = gu
            dwd_ref[0] = gd

        @pl.when(i > 0)
        def _():
            dwg_ref[0] += gg
            dwu_ref[0] += gu
            dwd_ref[0] += gd

    row = lambda w: pl.BlockSpec((tm, w), lambda q, i: (i, 0))
    act = pl.BlockSpec((1, tm, FF_SHARD), lambda q, i: (q, i, 0))
    sds = jax.ShapeDtypeStruct
    return pl.pallas_call(
        body, grid=(N_SHARD, n_pad // tm), name="grad_ffn",
        in_specs=[row(D_MODEL), act, act, act, row(D_MODEL)],
        out_specs=[pl.BlockSpec((1, FF_SHARD, D_MODEL), lambda q, i: (q, 0, 0))] * 3,
        out_shape=[sds((N_SHARD, FF_SHARD, D_MODEL), F32)] * 3,
        compiler_params=_params(("parallel", "arbitrary")),
    )(n2, da, db, ff, dh2)


def _bwd_out(dh1, ms, mp, w_out_b, tm):
    n_pad = dh1.shape[0]

    def body(dh1_ref, ms_ref, mp_ref, wo_ref, dms_ref, dmp_ref, dwo_ref):
        i = pl.program_id(0)

        @pl.when(i == 0)
        def _():
            dwo_ref[...] = jnp.zeros_like(dwo_ref)

        d = dh1_ref[...].astype(BF16)
        dms_ref[...] = _dot_nt(d, wo_ref[:D_SSM, :])
        dmp_ref[...] = _dot_nt(d, wo_ref[D_SSM:, :])
        dwo_ref[:D_SSM, :] += _dot_tn(ms_ref[...], d)
        dwo_ref[D_SSM:, :] += _dot_tn(mp_ref[...], d)

    row = lambda w: pl.BlockSpec((tm, w), lambda i: (i, 0))
    sds = jax.ShapeDtypeStruct
    return pl.pallas_call(
        body, grid=(n_pad // tm,), name="bwd_out",
        in_specs=[row(D_MODEL), row(D_SSM), row(D_POOL), _full((D_MODEL, D_MODEL))],
        out_specs=[row(D_SSM), row(D_POOL), _full((D_MODEL, D_MODEL))],
        out_shape=[sds((n_pad, D_SSM), F32), sds((n_pad, D_POOL), F32), sds((D_MODEL, D_MODEL), F32)],
        compiler_params=_params(("arbitrary",)),
    )(dh1, ms, mp, w_out_b)


def _bwd_in(du, dv, h0, dh1, g1, w_in_b, tm):
    n_pad = h0.shape[0]

    def body(du_ref, dv_ref, h0_ref, dh1_ref, g1_ref, w_ref, dh0_ref, dwi_ref, dg1_ref):
        i = pl.program_id(0)

        @pl.when(i == 0)
        def _():
            dwi_ref[...] = jnp.zeros_like(dwi_ref)
            dg1_ref[...] = jnp.zeros_like(dg1_ref)

        dub = du_ref[...].astype(BF16)
        dvb = dv_ref[...].astype(BF16)
        dn1 = _dot_nt(dub, w_ref[:, :D_SSM]) + _dot_nt(dvb, w_ref[:, D_SSM:])
        h = h0_ref[...]
        r = _rms(h)
        xhat = h * r
        g_row = g1_ref[...]
        n1 = (xhat * g_row).astype(BF16)
        dwi_ref[:, :D_SSM] += _dot_tn(n1, dub)
        dwi_ref[:, D_SSM:] += _dot_tn(n1, dvb)
        dg1_ref[...] += jnp.sum(dn1 * xhat, axis=0, keepdims=True)
        dh0_ref[...] = dh1_ref[...] + _rms_bwd(dn1 * g_row, xhat, r)

    row = lambda w: pl.BlockSpec((tm, w), lambda i: (i, 0))
    sds = jax.ShapeDtypeStruct
    return pl.pallas_call(
        body, grid=(n_pad // tm,), name="bwd_in",
        in_specs=[row(D_SSM), row(D_POOL), row(D_MODEL), row(D_MODEL), _full((1, D_MODEL)), _full((D_MODEL, D_MODEL))],
        out_specs=[row(D_MODEL), _full((D_MODEL, D_MODEL)), _full((1, D_MODEL))],
        out_shape=[sds((n_pad, D_MODEL), F32), sds((D_MODEL, D_MODEL), F32), sds((1, D_MODEL), F32)],
        compiler_params=_params(("arbitrary",)),
    )(du, dv, h0, dh1, g1, w_in_b)


def _planes_store(ref, j, val, tc):
    for i in range(4):
        ref[pl.ds(4 * j + i, tc, stride=STATE_BLOCKS), :] = val[:, 128 * i:128 * (i + 1)]


def _planes_load(ref, j, tc):
    return jnp.concatenate([ref[pl.ds(4 * j + i, tc, stride=STATE_BLOCKS), :] for i in range(4)], axis=1)


def _s5_tail(y, glu_ref, glub):
    g, dgelu = _gelu_and_grad(y)
    gb = g.astype(BF16)
    gate = jnp.concatenate([_dot(gb[:, 128 * j:128 * (j + 1)], glu_ref[j]) for j in range(SUPER)], axis=1) + glub
    sig = _sigmoid(gate)
    return g, gb, dgelu, sig, g * sig


def _s5_fwd(u, lam, bbr, bbi, crt, cit, vecs, glu, tc):
    n_pad = u.shape[0]

    def body(u_ref, lam_ref, bbr_ref, bbi_ref, crt_ref, cit_ref, vec_ref, glu_ref,
             sr_ref, si_ref, y_ref, ms_ref, hr_s, hi_s):
        @pl.when(pl.program_id(0) == 0)
        def _():
            hr_s[...] = jnp.zeros_like(hr_s)
            hi_s[...] = jnp.zeros_like(hi_s)

        u_v = u_ref[...]
        ub = u_v.astype(BF16)
        for j in range(SUPER):
            uj = ub[:, 128 * j:128 * (j + 1)]
            _planes_store(sr_ref, j, _dot(uj, bbr_ref[j]), tc)
            _planes_store(si_ref, j, _dot(uj, bbi_ref[j]), tc)

        ar = lam_ref[0:STATE_BLOCKS, :]
        ai = lam_ref[STATE_BLOCKS:, :]

        def step(t, carry):
            hr, hi = carry
            o = pl.multiple_of(t * STATE_BLOCKS, STATE_BLOCKS)
            nr = ar * hr - ai * hi + sr_ref[pl.ds(o, STATE_BLOCKS), :]
            ni = ar * hi + ai * hr + si_ref[pl.ds(o, STATE_BLOCKS), :]
            sr_ref[pl.ds(o, STATE_BLOCKS), :] = nr
            si_ref[pl.ds(o, STATE_BLOCKS), :] = ni
            return nr, ni

        hr, hi = lax.fori_loop(0, tc, step, (hr_s[...], hi_s[...]), unroll=4)
        hr_s[...] = hr
        hi_s[...] = hi

        d_row, glub, gs = vec_ref[0:1, :], vec_ref[1:2, :], vec_ref[2:3, :]
        ys_c = []
        for j in range(SUPER):
            sr_j = _planes_load(sr_ref, j, tc).astype(BF16)
            si_j = _planes_load(si_ref, j, tc).astype(BF16)
            ys_c.append(_dot(sr_j, crt_ref[j]) - _dot(si_j, cit_ref[j]))
        y = jnp.concatenate(ys_c, axis=1) + d_row * u_v
        y_ref[...] = y
        _, _, _, _, ys = _s5_tail(y, glu_ref, glub)
        ms_ref[...] = (ys * _rms(ys) * gs).astype(BF16)

    chunk = lambda w: pl.BlockSpec((tc, w), lambda c: (c, 0))
    states = pl.BlockSpec((tc * STATE_BLOCKS, 128), lambda c: (c, 0))
    sds = jax.ShapeDtypeStruct
    return pl.pallas_call(
        body, grid=(n_pad // tc,), name="s5_fwd",
        in_specs=[chunk(D_SSM), _full((2 * STATE_BLOCKS, 128)), _full((SUPER, 128, 512)), _full((SUPER, 128, 512)),
                  _full((SUPER, 512, 128)), _full((SUPER, 512, 128)), _full((8, D_SSM)), _full((SUPER, 128, 128))],
        out_specs=[states, states, chunk(D_SSM), chunk(D_SSM)],
        out_shape=[sds((n_pad * STATE_BLOCKS, 128), F32), sds((n_pad * STATE_BLOCKS, 128), F32),
                   sds((n_pad, D_SSM), F32), sds((n_pad, D_SSM), BF16)],
        scratch_shapes=[pltpu.VMEM((STATE_BLOCKS, 128), F32), pltpu.VMEM((STATE_BLOCKS, 128), F32)],
        compiler_params=_params(("arbitrary",)),
    )(u, lam, bbr, bbi, crt, cit, vecs, glu)


def _s5_bwd(dms, y, u, sr, si, lam, bbr, bbi, crt, cit, vecs, glu, tc):
    n_pad = u.shape[0]
    nc = n_pad // tc

    def body(dms_ref, y_ref, u_ref, sr_ref, si_ref, pr_ref, pi_ref, lam_ref, bbr_ref, bbi_ref, crt_ref, cit_ref,
             vec_ref, glu_ref, du_ref, dbbr_ref, dbbi_ref, dcrt_ref, dcit_ref, dglu_ref, dvec_ref, dlam_ref,
             qr_s, qi_s, cr_s, ci_s):
        c = pl.program_id(0)

        @pl.when(c == 0)
        def _():
            for ref in (dbbr_ref, dbbi_ref, dcrt_ref, dcit_ref, dglu_ref, dvec_ref, dlam_ref, cr_s, ci_s):
                ref[...] = jnp.zeros_like(ref)

        d_row, glub, gs = vec_ref[0:1, :], vec_ref[1:2, :], vec_ref[2:3, :]
        y_v, u_v = y_ref[...], u_ref[...]
        ub = u_v.astype(BF16)
        g, gb, dgelu, sig, ys = _s5_tail(y_v, glu_ref, glub)
        r = _rms(ys)
        xhat = ys * r
        dm = dms_ref[...]
        dys = _rms_bwd(dm * gs, xhat, r)
        dgate = dys * g * sig * (1.0 - sig)
        dgateb = dgate.astype(BF16)
        dg = dys * sig + jnp.concatenate(
            [_dot_nt(dgateb[:, 128 * j:128 * (j + 1)], glu_ref[j]) for j in range(SUPER)], axis=1)
        dy = dg * dgelu
        dyb = dy.astype(BF16)
        dvec_ref[0:1, :] += jnp.sum(dy * u_v, axis=0, keepdims=True)
        dvec_ref[1:2, :] += jnp.sum(dgate, axis=0, keepdims=True)
        dvec_ref[2:3, :] += jnp.sum(dm * xhat, axis=0, keepdims=True)

        for j in range(SUPER):
            cols = slice(128 * j, 128 * (j + 1))
            dglu_ref[j] += _dot_tn(gb[:, cols], dgateb[:, cols])
            dcrt_ref[j] += _dot_tn(_planes_load(sr_ref, j, tc).astype(BF16), dyb[:, cols])
            dcit_ref[j] -= _dot_tn(_planes_load(si_ref, j, tc).astype(BF16), dyb[:, cols])
            _planes_store(qr_s, j, _dot_nt(dyb[:, cols], crt_ref[j]), tc)
            _planes_store(qi_s, j, -_dot_nt(dyb[:, cols], cit_ref[j]), tc)

        ar = lam_ref[0:STATE_BLOCKS, :]
        ai = lam_ref[STATE_BLOCKS:, :]

        def step(n, carry):
            qr, qi = carry
            o = pl.multiple_of((tc - 1 - n) * STATE_BLOCKS, STATE_BLOCKS)
            nr = ar * qr + ai * qi + qr_s[pl.ds(o, STATE_BLOCKS), :]
            ni = ar * qi - ai * qr + qi_s[pl.ds(o, STATE_BLOCKS), :]
            qr_s[pl.ds(o, STATE_BLOCKS), :] = nr
            qi_s[pl.ds(o, STATE_BLOCKS), :] = ni
            return nr, ni

        qr, qi = lax.fori_loop(0, tc, step, (cr_s[...], ci_s[...]), unroll=4)
        cr_s[...] = qr
        ci_s[...] = qi

        first = c == nc - 1
        row0 = lax.broadcasted_iota(jnp.int32, (tc, 1), 0) == 0
        for k in range(STATE_BLOCKS):
            plane = pl.ds(k, tc, stride=STATE_BLOCKS)
            pr = jnp.where(first, 0.0, pr_ref[k:k + 1, :])
            pi = jnp.where(first, 0.0, pi_ref[k:k + 1, :])
            hpr = jnp.where(row0, pr, pltpu.roll(sr_ref[plane, :], 1, 0))
            hpi = jnp.where(row0, pi, pltpu.roll(si_ref[plane, :], 1, 0))
            q_r, q_i = qr_s[plane, :], qi_s[plane, :]
            dlam_ref[k:k + 1, :] += jnp.sum(q_r * hpr + q_i * hpi, axis=0, keepdims=True)
            dlam_ref[STATE_BLOCKS + k:STATE_BLOCKS + k + 1, :] += jnp.sum(q_i * hpr - q_r * hpi, axis=0, keepdims=True)

        du_c = []
        for j in range(SUPER):
            cols = slice(128 * j, 128 * (j + 1))
            qr_j = _planes_load(qr_s, j, tc).astype(BF16)
            qi_j = _planes_load(qi_s, j, tc).astype(BF16)
            du_c.append(_dot_nt(qr_j, bbr_ref[j]) + _dot_nt(qi_j, bbi_ref[j]))
            dbbr_ref[j] += _dot_tn(ub[:, cols], qr_j)
            dbbi_ref[j] += _dot_tn(ub[:, cols], qi_j)
        du_ref[...] = jnp.concatenate(du_c, axis=1) + dy * d_row

    rev = lambda c: nc - 1 - c
    chunk = lambda w: pl.BlockSpec((tc, w), lambda c: (rev(c), 0))
    states = pl.BlockSpec((tc * STATE_BLOCKS, 128), lambda c: (rev(c), 0))
    prev = pl.BlockSpec((STATE_BLOCKS, 128), lambda c: (jnp.maximum(rev(c) * tc - 1, 0), 0))
    sds = jax.ShapeDtypeStruct
    return pl.pallas_call(
        body, grid=(nc,), name="s5_bwd",
        in_specs=[chunk(D_SSM), chunk(D_SSM), chunk(D_SSM), states, states, prev, prev,
                  _full((2 * STATE_BLOCKS, 128)), _full((SUPER, 128, 512)), _full((SUPER, 128, 512)),
                  _full((SUPER, 512, 128)), _full((SUPER, 512, 128)), _full((8, D_SSM)), _full((SUPER, 128, 128))],
        out_specs=[chunk(D_SSM), _full((SUPER, 128, 512)), _full((SUPER, 128, 512)), _full((SUPER, 512, 128)),
                   _full((SUPER, 512, 128)), _full((SUPER, 128, 128)), _full((8, D_SSM)), _full((2 * STATE_BLOCKS, 128))],
        out_shape=[sds((n_pad, D_SSM), F32), sds((SUPER, 128, 512), F32), sds((SUPER, 128, 512), F32),
                   sds((SUPER, 512, 128), F32), sds((SUPER, 512, 128), F32), sds((SUPER, 128, 128), F32),
                   sds((8, D_SSM), F32), sds((2 * STATE_BLOCKS, 128), F32)],
        scratch_shapes=[pltpu.VMEM((tc * STATE_BLOCKS, 128), F32), pltpu.VMEM((tc * STATE_BLOCKS, 128), F32),
                        pltpu.VMEM((STATE_BLOCKS, 128), F32), pltpu.VMEM((STATE_BLOCKS, 128), F32)],
        compiler_params=_params(("arbitrary",)),
    )(dms, y, u, sr, si, sr, si, lam, bbr, bbi, crt, cit, vecs, glu)


def _inv_count(c_idx, tc, w):
    t = c_idx * tc + lax.broadcasted_iota(jnp.int32, (tc, 1), 0)
    return 1.0 / jnp.minimum(t + 1, w).astype(F32)


def _pool_fwd(v, pw_b, vecs, tc):
    n_pad = v.shape[0]

    def body(v_ref, pw_ref, vec_ref, feat_ref, mp_ref, hist):
        c = pl.program_id(0)

        @pl.when(c == 0)
        def _():
            hist[...] = jnp.zeros_like(hist)

        v_v = v_ref[...]
        ext = jnp.concatenate([hist[...], v_v], axis=0)
        hist[...] = v_v[tc - POOL_HALO:, :]
        feats, ps = [], []
        for k, w in enumerate(POOL_WINDOWS):
            cols = slice(128 * k, 128 * (k + 1))
            s = ext[:, cols]
            sh = 1
            while sh < w:
                s = s + pltpu.roll(s, sh, 0)
                sh *= 2
            f = (s[POOL_HALO:, :] * _inv_count(c, tc, w) - v_v[:, cols]).astype(BF16)
            feats.append(f)
            ps.append(_dot(f, pw_ref[k]))
        feat_ref[...] = jnp.concatenate(feats, axis=1)
        yp = jnp.concatenate(ps, axis=1) * vec_ref[0:1, :]
        mp_ref[...] = (yp * _rms(yp) * vec_ref[1:2, :]).astype(BF16)

    chunk = lambda w: pl.BlockSpec((tc, w), lambda c: (c, 0))
    sds = jax.ShapeDtypeStruct
    return pl.pallas_call(
        body, grid=(n_pad // tc,), name="pool_fwd",
        in_specs=[chunk(D_POOL), _full((4, 128, 128)), _full((8, D_POOL))],
        out_specs=[chunk(D_POOL), chunk(D_POOL)],
        out_shape=[sds((n_pad, D_POOL), BF16), sds((n_pad, D_POOL), BF16)],
        scratch_shapes=[pltpu.VMEM((POOL_HALO, D_POOL), F32)],
        compiler_params=_params(("arbitrary",)),
    )(v, pw_b, vecs)


def _pool_bwd(dmp, feat, pw_b, vecs, tc):
    n_pad = dmp.shape[0]
    nc = n_pad // tc

    def body(dmp_ref, feat_ref, pw_ref, vec_ref, dv_ref, dpw_ref, dvec_ref, fut):
        c = pl.program_id(0)

        @pl.when(c == 0)
        def _():
            fut[...] = jnp.zeros_like(fut)
            dpw_ref[...] = jnp.zeros_like(dpw_ref)
            dvec_ref[...] = jnp.zeros_like(dvec_ref)

        scale, gp = vec_ref[0:1, :], vec_ref[1:2, :]
        feat_v = feat_ref[...]
        p = jnp.concatenate([_dot(feat_v[:, 128 * k:128 * (k + 1)], pw_ref[k]) for k in range(4)], axis=1)
        yp = p * scale
        r = _rms(yp)
        xhat = yp * r
        dm = dmp_ref[...]
        dyp = _rms_bwd(dm * gp, xhat, r)
        dvec_ref[0:1, :] += jnp.sum(dyp * p, axis=0, keepdims=True)
        dvec_ref[1:2, :] += jnp.sum(dm * xhat, axis=0, keepdims=True)
        dpb = (dyp * scale).astype(BF16)
        es, dfs = [], []
        for k, w in enumerate(POOL_WINDOWS):
            cols = slice(128 * k, 128 * (k + 1))
            dpw_ref[k] += _dot_tn(feat_v[:, cols], dpb[:, cols])
            df = _dot_nt(dpb[:, cols], pw_ref[k])
            dfs.append(df)
            es.append(df * _inv_count(nc - 1 - c, tc, w))
        e = jnp.concatenate(es, axis=1)
        ext = jnp.concatenate([e, fut[...]], axis=0)
        fut[...] = e[:POOL_HALO, :]
        n_ext = tc + POOL_HALO
        dvs = []
        for k, w in enumerate(POOL_WINDOWS):
            s = ext[:, 128 * k:128 * (k + 1)]
            sh = 1
            while sh < w:
                s = s + pltpu.roll(s, n_ext - sh, 0)
                sh *= 2
            dvs.append(s[:tc, :] - dfs[k])
        dv_ref[...] = jnp.concatenate(dvs, axis=1)

    chunk = lambda w: pl.BlockSpec((tc, w), lambda c: (nc - 1 - c, 0))
    sds = jax.ShapeDtypeStruct
    return pl.pallas_call(
        body, grid=(nc,), name="pool_bwd",
        in_specs=[chunk(D_POOL), chunk(D_POOL), _full((4, 128, 128)), _full((8, D_POOL))],
        out_specs=[chunk(D_POOL), _full((4, 128, 128)), _full((8, D_POOL))],
        out_shape=[sds((n_pad, D_POOL), F32), sds((4, 128, 128), F32), sds((8, D_POOL), F32)],
        scratch_shapes=[pltpu.VMEM((POOL_HALO, D_POOL), F32)],
        compiler_params=_params(("arbitrary",)),
    )(dmp, feat, pw_b, vecs)


def _place():
    x, y, c = lax.axis_index("x"), lax.axis_index("y"), lax.axis_index("c")
    chips = [(1 - x, y), (x, 1 - y), (1 - x, 1 - y)]
    return x, y, c, chips


_ANY = pl.BlockSpec(memory_space=pl.ANY)


def _cast_shards(shards, dtypes, place):
    n = len(shards)

    def body(place_ref, *refs):
        for i in range(n):
            refs[n + i][0] = refs[i][...].astype(dtypes[i])

    return pl.pallas_call(
        body, name="cast_shards",
        grid_spec=pltpu.PrefetchScalarGridSpec(
            num_scalar_prefetch=1, grid=(1,),
            in_specs=[pl.BlockSpec(s.shape, lambda i, p: (0, 0, 0)) for s in shards],
            out_specs=[pl.BlockSpec((1,) + s.shape, lambda i, p: (p[0], 0, 0, 0)) for s in shards]),
        out_shape=[jax.ShapeDtypeStruct((N_SHARD,) + s.shape, dt) for s, dt in zip(shards, dtypes)],
        compiler_params=_params(("arbitrary",)),
    )(place, *shards)


def _gather_shards(full):
    n = len(full)

    def body(*refs):
        outs = refs[n:2 * n]
        ici_send, ici_recv, d2d_send, d2d_recv = refs[2 * n:]
        x, y, c, chips = _place()
        q = 2 * x + y
        sibling = (x, y, 1 - c)

        def ici(i, j, shard, to):
            return pltpu.make_async_remote_copy(src_ref=outs[i].at[q, c], dst_ref=outs[i].at[shard, c],
                                                send_sem=ici_send.at[i, j], recv_sem=ici_recv.at[i, j],
                                                device_id=to, device_id_type=MESH)

        def d2d(i, j, shard, half):
            return pltpu.make_async_remote_copy(src_ref=outs[i].at[shard, c], dst_ref=outs[i].at[shard, half],
                                                send_sem=d2d_send.at[i, j], recv_sem=d2d_recv.at[i, j],
                                                device_id=sibling, device_id_type=MESH)

        sends = [ici(i, j, q, (*chip, c)) for i in range(n) for j, chip in enumerate(chips)]
        for cp in sends:
            cp.start()
        passed = []
        for i in range(n):
            for j, (cx, cy) in enumerate(chips):
                ici(i, j, 2 * cx + cy, (cx, cy, c)).wait_recv()
                cp = d2d(i, j, 2 * cx + cy, c)
                cp.start()
                passed.append(cp)
        for i in range(n):
            for j, (cx, cy) in enumerate(chips):
                d2d(i, j, 2 * cx + cy, 1 - c).wait_recv()
        for cp in sends + passed:
            cp.wait_send()

    return pl.pallas_call(
        body, name="gather_shards",
        in_specs=[_ANY] * n, out_specs=[_ANY] * n,
        out_shape=[jax.ShapeDtypeStruct(f.shape, f.dtype) for f in full],
        input_output_aliases={i: i for i in range(n)},
        scratch_shapes=[pltpu.SemaphoreType.DMA((n, 3)), pltpu.SemaphoreType.DMA((n, 3)),
                        pltpu.SemaphoreType.DMA((n, 3)), pltpu.SemaphoreType.DMA((n, 3))],
    )(*full)


def _swap_halves(grads):
    n = len(grads)

    def body(*refs):
        ins, outs = refs[:n], refs[n:2 * n]
        send, recv = refs[2 * n:]
        x, y, c, _ = _place()
        cps = [pltpu.make_async_remote_copy(src_ref=ins[i].at[:, 1 - c], dst_ref=outs[i], send_sem=send.at[i],
                                            recv_sem=recv.at[i], device_id=(x, y, 1 - c), device_id_type=MESH)
               for i in range(n)]
        for cp in cps:
            cp.start()
        for cp in cps:
            cp.wait()

    return pl.pallas_call(
        body, name="swap_halves",
        in_specs=[_ANY] * n, out_specs=[_ANY] * n,
        out_shape=[jax.ShapeDtypeStruct((N_SHARD,) + g.shape[2:], F32) for g in grads],
        scratch_shapes=[pltpu.SemaphoreType.DMA((n,)), pltpu.SemaphoreType.DMA((n,))],
    )(*grads)


def _exchange_chips(parts):
    n = len(parts)

    def body(*refs):
        ins, outs = refs[:n], refs[n:2 * n]
        send, recv = refs[2 * n:]
        x, y, c, chips = _place()
        cps = [pltpu.make_async_remote_copy(src_ref=ins[i].at[2 * cx + cy], dst_ref=outs[i].at[j], send_sem=send.at[i, j],
                                            recv_sem=recv.at[i, j], device_id=(cx, cy, c), device_id_type=MESH)
               for i in range(n) for j, (cx, cy) in enumerate(chips)]
        for cp in cps:
            cp.start()
        for cp in cps:
            cp.wait()

    return pl.pallas_call(
        body, name="exchange_chips",
        in_specs=[_ANY] * n, out_specs=[_ANY] * n,
        out_shape=[jax.ShapeDtypeStruct((3,) + p.shape[1:], BF16) for p in parts],
        scratch_shapes=[pltpu.SemaphoreType.DMA((n, 3)), pltpu.SemaphoreType.DMA((n, 3))],
    )(*parts)


def _join_halves(pairs):
    n = len(pairs)

    def body(*refs):
        outs = refs[n:2 * n]
        send, recv = refs[2 * n:]
        x, y, c, _ = _place()
        cps = [pltpu.make_async_remote_copy(src_ref=outs[i].at[c], dst_ref=outs[i].at[c], send_sem=send.at[i],
                                            recv_sem=recv.at[i], device_id=(x, y, 1 - c), device_id_type=MESH)
               for i in range(n)]
        for cp in cps:
            cp.start()
        for i in range(n):
            cps[i].wait_send()
            pltpu.make_async_remote_copy(src_ref=outs[i].at[c], dst_ref=outs[i].at[1 - c], send_sem=send.at[i],
                                         recv_sem=recv.at[i], device_id=(x, y, 1 - c), device_id_type=MESH).wait_recv()

    return pl.pallas_call(
        body, name="join_halves",
        in_specs=[_ANY] * n, out_specs=[_ANY] * n,
        out_shape=[jax.ShapeDtypeStruct(p.shape, F32) for p in pairs],
        input_output_aliases={i: i for i in range(n)},
        scratch_shapes=[pltpu.SemaphoreType.DMA((n,)), pltpu.SemaphoreType.DMA((n,))],
    )(*pairs)


def _gather_small(part):
    m_per, n = part.shape

    def body(x_ref, out_ref, send_sems, recv_sems, local_sem):
        x, y, c, chips = _place()
        me, sibling = (x, y, c), (x, y, 1 - c)

        def rows(px, py, pc):
            return out_ref.at[pl.ds((4 * px + 2 * py + pc) * m_per, m_per), :]

        def copy(k, block, to, src=None):
            return pltpu.make_async_remote_copy(src_ref=rows(*block) if src is None else src, dst_ref=rows(*block),
                                                send_sem=send_sems.at[k], recv_sem=recv_sems.at[k],
                                                device_id=to, device_id_type=MESH)

        mine = pltpu.make_async_copy(x_ref, rows(*me), local_sem)
        mine.start()
        first = [copy(0, me, sibling, src=x_ref)]
        first += [copy(1 + j, me, (*chip, c), src=x_ref) for j, chip in enumerate(chips)]
        for cp in first:
            cp.start()
        passed = [copy(4 + j, (*chip, c), sibling) for j, chip in enumerate(chips)]
        for j, chip in enumerate(chips):
            copy(1 + j, (*chip, c), me).wait_recv()
            passed[j].start()
        copy(0, sibling, me).wait_recv()
        for j, chip in enumerate(chips):
            copy(4 + j, (*chip, 1 - c), me).wait_recv()
        for cp in first + passed:
            cp.wait_send()
        mine.wait()

    return pl.pallas_call(
        body, name="gather_small",
        out_shape=jax.ShapeDtypeStruct((8 * m_per, n), F32),
        in_specs=[pl.BlockSpec(memory_space=pltpu.VMEM)], out_specs=pl.BlockSpec(memory_space=pltpu.VMEM),
        scratch_shapes=[pltpu.SemaphoreType.DMA((7,)), pltpu.SemaphoreType.DMA((7,)), pltpu.SemaphoreType.DMA],
        compiler_params=_params(),
    )(part)


N_SPLIT = 2


def _sum_siblings(grads, recvd, place):
    n = len(grads)

    def body(place_ref, *refs):
        g_refs, r_refs, sb_refs, own_refs = (refs[k * n:(k + 1) * n] for k in range(4))
        s = pl.program_id(1)
        for i in range(n):
            tot = g_refs[i][0, 0] + r_refs[i][0]
            sb_refs[i][0] = tot.astype(BF16)

            @pl.when(s == place_ref[0])
            def _():
                own_refs[i][...] = tot

    in_specs, sb_specs, own_specs, sb_shapes, own_shapes = [], [], [], [], []
    for g in grads:
        _, _, r, cdim = g.shape
        rb = r // N_SPLIT
        in_specs.append(pl.BlockSpec((1, 1, rb, cdim), lambda b, s, p: (s, p[1], b, 0)))
        sb_specs.append(pl.BlockSpec((1, rb, cdim), lambda b, s, p: (s, b, 0)))
        own_specs.append(pl.BlockSpec((rb, cdim), lambda b, s, p: (b, 0)))
        sb_shapes.append(jax.ShapeDtypeStruct((N_SHARD, r, cdim), BF16))
        own_shapes.append(jax.ShapeDtypeStruct((r, cdim), F32))
    out = pl.pallas_call(
        body, name="sum_siblings",
        grid_spec=pltpu.PrefetchScalarGridSpec(
            num_scalar_prefetch=1, grid=(N_SPLIT, N_SHARD),
            in_specs=in_specs + sb_specs, out_specs=sb_specs + own_specs),
        out_shape=sb_shapes + own_shapes,
        compiler_params=_params(("parallel", "arbitrary")),
    )(place, *grads, *recvd)
    return out[:n], out[n:]


def _sum_chips(own, recvd, place):
    n = len(own)

    def body(place_ref, *refs):
        o_refs, r_refs, out_refs = (refs[k * n:(k + 1) * n] for k in range(3))
        for i in range(n):
            tot = o_refs[i][...]
            for j in range(3):
                tot = tot + r_refs[i][j].astype(F32)
            out_refs[i][0] = tot

    o_specs, r_specs, out_specs = [], [], []
    for o in own:
        r, cdim = o.shape
        rb = r // N_SPLIT
        o_specs.append(pl.BlockSpec((rb, cdim), lambda b, p: (b, 0)))
        r_specs.append(pl.BlockSpec((3, rb, cdim), lambda b, p: (0, b, 0)))
        out_specs.append(pl.BlockSpec((1, rb, cdim), lambda b, p: (p[1], b, 0)))
    return pl.pallas_call(
        body, name="sum_chips",
        grid_spec=pltpu.PrefetchScalarGridSpec(num_scalar_prefetch=1, grid=(N_SPLIT,),
                                               in_specs=o_specs + r_specs, out_specs=out_specs),
        out_shape=[jax.ShapeDtypeStruct((2,) + o.shape, F32) for o in own],
        compiler_params=_params(("parallel",)),
    )(place, *own, *recvd)


def _adamw_math(w, g, m, v):
    m = ADAM_B1 * m + (1.0 - ADAM_B1) * g
    v = ADAM_B2 * v + (1.0 - ADAM_B2) * (g * g)
    m_hat = m / (1.0 - ADAM_B1 ** ADAM_STEP)
    v_hat = v / (1.0 - ADAM_B2 ** ADAM_STEP)
    delta = -ADAM_LR * (m_hat / (jnp.sqrt(v_hat) + ADAM_EPS) + ADAM_WD * w)
    return delta, m, v


def _adamw(name, ws, gs, ms, vs, n_split):
    n = len(ws)

    def body(*refs):
        w_r, g_r, m_r, v_r, d_o, m_o, v_o = (refs[k * n:(k + 1) * n] for k in range(7))
        for i in range(n):
            d, m, v = _adamw_math(w_r[i][...], g_r[i][...], m_r[i][...], v_r[i][...])
            d_o[i][...] = d
            m_o[i][...] = m
            v_o[i][...] = v

    specs = [pl.BlockSpec((w.shape[0] // n_split, w.shape[1]), lambda b: (b, 0)) for w in ws]
    shapes = [jax.ShapeDtypeStruct(w.shape, F32) for w in ws]
    out = pl.pallas_call(
        body, name=name, grid=(n_split,),
        in_specs=specs * 4, out_specs=specs * 3, out_shape=shapes * 3,
        compiler_params=_params(("parallel",)),
    )(*ws, *gs, *ms, *vs)
    return out[:n], out[n:2 * n], out[2 * n:]


def _reduce_small(gathered, w, m, v):
    rows = w.shape[0]

    def body(ga_ref, w_ref, m_ref, v_ref, g_out, d_out, m_out, v_out):
        g = ga_ref[0:rows, :]
        for k in range(1, 8):
            g = g + ga_ref[k * rows:(k + 1) * rows, :]
        g_out[...] = g
        d, mm, vv = _adamw_math(w_ref[...], g, m_ref[...], v_ref[...])
        d_out[...] = d
        m_out[...] = mm
        v_out[...] = vv

    return pl.pallas_call(
        body, name="reduce_small",
        out_shape=[jax.ShapeDtypeStruct(w.shape, F32)] * 4,
        compiler_params=_params(),
    )(gathered, w, m, v)


def _s5_operands(lam_re, lam_im, log_step, b_re, b_im, c_re, c_im, glu_w):
    lr = jnp.minimum(lam_re, -1e-4)
    li = lam_im
    step = jnp.exp(log_step)[:, None]
    mag = jnp.exp(lr * step)
    ang = li * step
    abr = mag * jnp.cos(ang)
    abi = mag * jnp.sin(ang)
    nr = abr - 1.0
    ni = abi
    den = lr * lr + li * li
    cr = ((nr * lr + ni * li) / den)[..., None]
    ci = ((ni * lr - nr * li) / den)[..., None]
    bbr = cr * b_re - ci * b_im
    bbi = cr * b_im + ci * b_re
    eye = jnp.eye(8, dtype=F32)
    g, h, p = SSM_GROUPS // SUPER, SSM_GROUP, SSM_STATE

    def b_layout(t):
        return jnp.einsum("ab,japh->jahbp", eye, t.reshape(SUPER, g, p, h)).reshape(SUPER, g * h, g * p)

    def c_layout(t):
        return jnp.einsum("ab,jahp->jbpah", eye, t.reshape(SUPER, g, h, p)).reshape(SUPER, g * p, g * h)

    glu = jnp.einsum("ab,jahk->jahbk", eye, glu_w.reshape(SUPER, g, h, h)).reshape(SUPER, g * h, g * h)
    lam = jnp.concatenate([abr.reshape(STATE_BLOCKS, 128), abi.reshape(STATE_BLOCKS, 128)], axis=0)
    return lam, b_layout(bbr), b_layout(bbi), c_layout(c_re), c_layout(c_im), glu


def _pad_rows(a, rows):
    return jnp.pad(a, ((0, rows - a.shape[0]), (0, 0)))


def _pack(parts):
    rows = []
    for a in parts:
        flat = a.reshape(-1)
        n = -(-flat.shape[0] // 128)
        rows.append(jnp.pad(flat, (0, n * 128 - flat.shape[0])).reshape(n, 128))
    out = jnp.concatenate(rows, axis=0)
    return _pad_rows(out, -(-out.shape[0] // 8) * 8)


def _unpack(packed, like):
    out, at = [], 0
    for a in like:
        n = -(-a.size // 128)
        out.append(packed[at:at + n].reshape(-1)[:a.size].reshape(a.shape))
        at += n
    return out


SMALL = ("norm1_g", "ssm_lambda_re", "ssm_lambda_im", "ssm_log_step", "ssm_b_re", "ssm_b_im", "ssm_c_re", "ssm_c_im",
         "ssm_d", "ssm_glu_w", "ssm_glu_b", "ssm_norm_g", "pool_w", "pool_scale", "pool_norm_g", "norm2_g",
         "final_norm_g")
LARGE = ("w_in", "w_out", "w_gate", "w_up", "w_down")
WEIGHTS = ("meta_tokens", "norm1_g", "w_in", "ssm_lambda_re", "ssm_lambda_im", "ssm_log_step", "ssm_b_re", "ssm_b_im",
           "ssm_c_re", "ssm_c_im", "ssm_d", "ssm_glu_w", "ssm_glu_b", "ssm_norm_g", "pool_w", "pool_scale",
           "pool_norm_g", "w_out", "norm2_g", "w_gate", "w_up", "w_down", "final_norm_g")


def _step(x, target, w, m, v):
    seq = x.shape[1]
    n_rows = N_META + seq
    n_pad, tm, tc = _plan(n_rows)
    xq, yq, cq = lax.axis_index("x"), lax.axis_index("y"), lax.axis_index("c")
    place = jnp.stack([2 * xq + yq, cq]).astype(jnp.int32)

    def halves(a2d):
        return a2d.reshape(2, a2d.shape[0] // 2, a2d.shape[1])

    def local2d(t):
        return {"w_gate": lambda a: a[0].T, "w_up": lambda a: a[0].T}.get(t, lambda a: a[0])

    shards = [halves(local2d(k)(w[k])) for k in LARGE] + [halves(w["meta_tokens"])]
    full = _gather_shards(_cast_shards(shards, [BF16] * len(LARGE) + [F32], place))
    w_in_b = full[0].reshape(D_MODEL, D_MODEL)
    w_out_b = full[1].reshape(D_MODEL, D_MODEL)
    wg_b, wu_b, wd_b = (full[k].reshape(N_SHARD, FF_SHARD, D_MODEL) for k in (2, 3, 4))
    meta = full[5].reshape(N_SHARD, N_META, D_MODEL // N_SHARD).transpose(1, 0, 2).reshape(N_META, D_MODEL)

    h0 = _pad_rows(jnp.concatenate([meta, x[0]], axis=0), n_pad)
    tgt = _pad_rows(jnp.concatenate([jnp.zeros((N_META, D_MODEL), F32), target[0]], axis=0), n_pad)
    s5_in = (w["ssm_lambda_re"][0], w["ssm_lambda_im"][0], w["ssm_log_step"][0], w["ssm_b_re"][0], w["ssm_b_im"][0],
             w["ssm_c_re"][0], w["ssm_c_im"][0], w["ssm_glu_w"][0])
    (lam, bbr, bbi, crt, cit, glu), s5_vjp = jax.vjp(_s5_operands, *s5_in)
    bbr_b, bbi_b, crt_b, cit_b, glu_b16 = (t.astype(BF16) for t in (bbr, bbi, crt, cit, glu))
    s5_vecs = _pad_rows(jnp.concatenate([w["ssm_d"].reshape(1, D_SSM), w["ssm_glu_b"].reshape(1, D_SSM),
                                         w["ssm_norm_g"].reshape(1, D_SSM)], axis=0), 8)
    pool_vecs = _pad_rows(jnp.concatenate([w["pool_scale"].reshape(1, D_POOL), w["pool_norm_g"].reshape(1, D_POOL)],
                                          axis=0), 8)
    pw_b = w["pool_w"][0].astype(BF16)
    g1, g2, gf = w["norm1_g"].reshape(1, D_MODEL), w["norm2_g"].reshape(1, D_MODEL), w["final_norm_g"].reshape(1, D_MODEL)

    u, vv = _fwd_in(h0, g1, w_in_b, tm)
    sr, si, y, ms = _s5_fwd(u, lam, bbr_b, bbi_b, crt_b, cit_b, s5_vecs, glu_b16, tc)
    feat, mp = _pool_fwd(vv, pw_b, pool_vecs, tc)
    h1, n2, a, b, ff, dh2, loss_acc, dgf = _fwd_ffn(h0, ms, mp, w_out_b, g2, wg_b, wu_b, wd_b, gf, tgt, tm, n_rows)
    loss = lax.psum(loss_acc[0, 0], AXES)

    da, db, dh1, dg2 = _bwd_ffn(dh2, a, b, wg_b, wu_b, wd_b, h1, g2, tm)
    dwg, dwu, dwd = _grad_ffn(n2, da, db, ff, dh2, tm)
    dms, dmp, dwo = _bwd_out(dh1, ms, mp, w_out_b, tm)
    du, dbbr, dbbi, dcrt, dcit, dglu, ds5v, dlam = _s5_bwd(dms, y, u, sr, si, lam, bbr_b, bbi_b, crt_b, cit_b,
                                                           s5_vecs, glu_b16, tc)
    dv, dpw, dpoolv = _pool_bwd(dmp, feat, pw_b, pool_vecs, tc)
    dh0, dwi, dg1 = _bwd_in(du, dv, h0, dh1, g1, w_in_b, tm)
    d_lre, d_lim, d_lstep, d_bre, d_bim, d_cre, d_cim, d_gluw = s5_vjp((dlam, dbbr, dbbi, dcrt, dcit, dglu))
    grad_x = dh0[N_META:n_rows][None]

    def quarters(t):
        if t.ndim == 2:
            t = t.reshape(N_SHARD, t.shape[0] // N_SHARD, t.shape[1])
        return t.reshape(N_SHARD, 2, t.shape[1] // 2, t.shape[2])

    big = [quarters(t) for t in (dwi, dwo, dwg, dwu, dwd)]
    from_sibling = _swap_halves(big)
    parts_b, own = _sum_siblings(big, from_sibling, place)
    from_chips = _exchange_chips(parts_b)
    joined = _join_halves(_sum_chips(own, from_chips, place))
    g_large = [j.reshape(j.shape[0] * j.shape[1], j.shape[2]) for j in joined]
    w2d, m2d, v2d = ([local2d(k)(t[k]) for k in LARGE] for t in (w, m, v))
    d_large, m_large, v_large = _adamw("adamw_large", w2d, g_large, m2d, v2d, 8)

    small_g = {
        "norm1_g": dg1, "ssm_lambda_re": d_lre, "ssm_lambda_im": d_lim, "ssm_log_step": d_lstep, "ssm_b_re": d_bre,
        "ssm_b_im": d_bim, "ssm_c_re": d_cre, "ssm_c_im": d_cim, "ssm_d": ds5v[0], "ssm_glu_w": d_gluw,
        "ssm_glu_b": ds5v[1], "ssm_norm_g": ds5v[2], "pool_w": dpw, "pool_scale": dpoolv[0], "pool_norm_g": dpoolv[1],
        "norm2_g": dg2, "final_norm_g": dgf,
    }
    like = [w[k] for k in SMALL]
    packed_g = _pack([small_g[k].reshape(w[k].shape) for k in SMALL] + [dh0[:N_META]])
    rows = packed_g.shape[0]
    packed = lambda t: _pad_rows(_pack([t[k] for k in SMALL]), rows)
    gathered = _gather_small(packed_g)
    g_pk, d_pk, m_pk, v_pk = _reduce_small(gathered, packed(w), packed(m), packed(v))
    g_small = _unpack(g_pk, like + [jax.ShapeDtypeStruct((N_META, D_MODEL), F32)])
    d_small, m_small, v_small = (_unpack(t, like) for t in (d_pk, m_pk, v_pk))
    q = place[0]
    g_meta = lax.dynamic_slice_in_dim(g_small[-1], q * (D_MODEL // N_SHARD), D_MODEL // N_SHARD, axis=1)
    d_meta, m_meta, v_meta = _adamw("adamw_meta", [w["meta_tokens"]], [g_meta], [m["meta_tokens"]],
                                    [v["meta_tokens"]], 1)

    grads, deltas, new_m, new_v = {}, {}, {}, {}
    for i, k in enumerate(SMALL):
        grads[k], deltas[k], new_m[k], new_v[k] = g_small[i], d_small[i], m_small[i], v_small[i]
    for i, k in enumerate(LARGE):
        back = (lambda t: t.T[None]) if k in ("w_gate", "w_up") else (lambda t: t[None])
        grads[k], deltas[k], new_m[k], new_v[k] = (back(t) for t in (g_large[i], d_large[i], m_large[i], v_large[i]))
    grads["meta_tokens"], deltas["meta_tokens"] = g_meta, d_meta[0]
    new_m["meta_tokens"], new_v["meta_tokens"] = m_meta[0], v_meta[0]
    return (loss, grad_x, *[grads[k] for k in WEIGHTS], *[deltas[k] for k in WEIGHTS],
            *[new_m[k] for k in WEIGHTS], *[new_v[k] for k in WEIGHTS])


def kernel(x, meta_tokens, norm1_g, w_in, ssm_lambda_re, ssm_lambda_im, ssm_log_step, ssm_b_re, ssm_b_im, ssm_c_re, ssm_c_im, ssm_d, ssm_glu_w, ssm_glu_b, ssm_norm_g, pool_w, pool_scale, pool_norm_g, w_out, norm2_g, w_gate, w_up, w_down, final_norm_g, loss_target, m_meta_tokens, m_norm1_g, m_w_in, m_ssm_lambda_re, m_ssm_lambda_im, m_ssm_log_step, m_ssm_b_re, m_ssm_b_im, m_ssm_c_re, m_ssm_c_im, m_ssm_d, m_ssm_glu_w, m_ssm_glu_b, m_ssm_norm_g, m_pool_w, m_pool_scale, m_pool_norm_g, m_w_out, m_norm2_g, m_w_gate, m_w_up, m_w_down, m_final_norm_g, v_meta_tokens, v_norm1_g, v_w_in, v_ssm_lambda_re, v_ssm_lambda_im, v_ssm_log_step, v_ssm_b_re, v_ssm_b_im, v_ssm_c_re, v_ssm_c_im, v_ssm_d, v_ssm_glu_w, v_ssm_glu_b, v_ssm_norm_g, v_pool_w, v_pool_scale, v_pool_norm_g, v_w_out, v_norm2_g, v_w_gate, v_w_up, v_w_down, v_final_norm_g):
    w = dict(meta_tokens=meta_tokens, norm1_g=norm1_g, w_in=w_in, ssm_lambda_re=ssm_lambda_re, ssm_lambda_im=ssm_lambda_im, ssm_log_step=ssm_log_step, ssm_b_re=ssm_b_re, ssm_b_im=ssm_b_im, ssm_c_re=ssm_c_re, ssm_c_im=ssm_c_im, ssm_d=ssm_d, ssm_glu_w=ssm_glu_w, ssm_glu_b=ssm_glu_b, ssm_norm_g=ssm_norm_g, pool_w=pool_w, pool_scale=pool_scale, pool_norm_g=pool_norm_g, w_out=w_out, norm2_g=norm2_g, w_gate=w_gate, w_up=w_up, w_down=w_down, final_norm_g=final_norm_g)
    m = dict(meta_tokens=m_meta_tokens, norm1_g=m_norm1_g, w_in=m_w_in, ssm_lambda_re=m_ssm_lambda_re, ssm_lambda_im=m_ssm_lambda_im, ssm_log_step=m_ssm_log_step, ssm_b_re=m_ssm_b_re, ssm_b_im=m_ssm_b_im, ssm_c_re=m_ssm_c_re, ssm_c_im=m_ssm_c_im, ssm_d=m_ssm_d, ssm_glu_w=m_ssm_glu_w, ssm_glu_b=m_ssm_glu_b, ssm_norm_g=m_ssm_norm_g, pool_w=m_pool_w, pool_scale=m_pool_scale, pool_norm_g=m_pool_norm_g, w_out=m_w_out, norm2_g=m_norm2_g, w_gate=m_w_gate, w_up=m_w_up, w_down=m_w_down, final_norm_g=m_final_norm_g)
    v = dict(meta_tokens=v_meta_tokens, norm1_g=v_norm1_g, w_in=v_w_in, ssm_lambda_re=v_ssm_lambda_re, ssm_lambda_im=v_ssm_lambda_im, ssm_log_step=v_ssm_log_step, ssm_b_re=v_ssm_b_re, ssm_b_im=v_ssm_b_im, ssm_c_re=v_ssm_c_re, ssm_c_im=v_ssm_c_im, ssm_d=v_ssm_d, ssm_glu_w=v_ssm_glu_w, ssm_glu_b=v_ssm_glu_b, ssm_norm_g=v_ssm_norm_g, pool_w=v_pool_w, pool_scale=v_pool_scale, pool_norm_g=v_pool_norm_g, w_out=v_w_out, norm2_g=v_norm2_g, w_gate=v_w_gate, w_up=v_w_up, w_down=v_w_down, final_norm_g=v_final_norm_g)
    return _step(x, loss_target, w, m, v)
```

```python
import functools
import math

import jax
import jax.numpy as jnp
from jax import lax
from jax.experimental import pallas as pl
from jax.experimental.pallas import tpu as pltpu

F32 = jnp.float32
BF16 = jnp.bfloat16
MESH = pl.DeviceIdType.MESH
AXES = ("x", "y", "c")

D_MODEL = 1024
D_SSM = 512
D_POOL = 512
N_META = 16
SSM_GROUP = 16
SSM_GROUPS = 32
SSM_STATE = 64
N_STATE = SSM_GROUPS * SSM_STATE
STATE_BLOCKS = N_STATE // 128
SUPER = 4
POOL_WINDOWS = (2, 4, 8, 16)
POOL_HALO = 16
D_FF = 2816
N_SHARD = 4
FF_SHARD = D_FF // N_SHARD
EPS = 1e-6
ADAM_LR, ADAM_B1, ADAM_B2, ADAM_EPS, ADAM_WD, ADAM_STEP = 0.001, 0.9, 0.999, 1e-08, 0.01, 10
VMEM_LIMIT = 56 * 1024 * 1024


def _plan(n_rows):
    if n_rows > 2048:
        tm, tc = 416, 320
    else:
        tm, tc = 128, 64
    step = tm * tc // math.gcd(tm, tc)
    return -(-n_rows // step) * step, tm, tc


def _params(sem=None):
    return pltpu.CompilerParams(dimension_semantics=sem, vmem_limit_bytes=VMEM_LIMIT)


def _dot(a, b):
    return jnp.dot(a, b, preferred_element_type=F32)


def _dot_nt(a, b):
    return lax.dot_general(a, b, (((1,), (1,)), ((), ())), preferred_element_type=F32)


def _dot_tn(a, b):
    return lax.dot_general(a, b, (((0,), (0,)), ((), ())), preferred_element_type=F32)


def _sigmoid(x):
    return 1.0 / (1.0 + jnp.exp(-x))


_GELU_C = math.sqrt(2.0 / math.pi)


def _gelu_and_grad(y):
    y2 = y * y
    t = jnp.tanh(_GELU_C * (y + 0.044715 * y * y2))
    g = 0.5 * y * (1.0 + t)
    dg = 0.5 * (1.0 + t) + 0.5 * y * (1.0 - t * t) * (_GELU_C * (1.0 + 3.0 * 0.044715 * y2))
    return g, dg


def _rms(x):
    return lax.rsqrt(jnp.mean(x * x, axis=-1, keepdims=True) + EPS)


def _rms_bwd(dn, xhat, r):
    return r * (dn - xhat * jnp.mean(dn * xhat, axis=-1, keepdims=True))


def _full(shape):
    nd = len(shape)
    return pl.BlockSpec(shape, lambda *_: (0,) * nd)


def _fwd_in(h0, g1, w_in_b, tm, token):
    n_pad = h0.shape[0]

    def body(h_ref, g_ref, w_ref, token_ref, u_ref, v_ref):
        h = h_ref[...]
        n1 = (h * _rms(h) * g_ref[...]).astype(BF16)
        proj = _dot(n1, w_ref[...])
        u_ref[...] = proj[:, :D_SSM]
        v_ref[...] = proj[:, D_SSM:]

    row = lambda w: pl.BlockSpec((tm, w), lambda i: (i, 0))
    return pl.pallas_call(
        body, grid=(n_pad // tm,), name="fwd_in",
        in_specs=[row(D_MODEL), _full((1, D_MODEL)), _full((D_MODEL, D_MODEL)), _ANY],
        out_specs=[row(D_SSM), row(D_POOL)],
        out_shape=[jax.ShapeDtypeStruct((n_pad, D_SSM), F32), jax.ShapeDtypeStruct((n_pad, D_POOL), F32)],
        compiler_params=_params(("parallel",)),
    )(h0, g1, w_in_b, token)


def _fwd_ffn(h0, ms, mp, w_out_b, g2, wg_b, wu_b, wd_b, gf, target, tm, n_valid):
    n_pad = h0.shape[0]
    nt = n_pad // tm

    def body(h0_ref, ms_ref, mp_ref, wo_ref, g2_ref, wg_ref, wu_ref, wd_ref, gf_ref, tgt_ref,
             h1_ref, n2_ref, a_ref, b_ref, ff_ref, dh2_ref, loss_ref, dgf_ref, acc):
        i, q = pl.program_id(0), pl.program_id(1)

        @pl.when((i == 0) & (q == 0))
        def _():
            loss_ref[...] = jnp.zeros_like(loss_ref)
            dgf_ref[...] = jnp.zeros_like(dgf_ref)

        @pl.when(q == 0)
        def _():
            h1 = h0_ref[...] + _dot(ms_ref[...], wo_ref[:D_SSM, :]) + _dot(mp_ref[...], wo_ref[D_SSM:, :])
            h1_ref[...] = h1
            acc[...] = h1
            n2_ref[...] = (h1 * _rms(h1) * g2_ref[...]).astype(BF16)

        n2 = n2_ref[...]
        a = _dot_nt(n2, wg_ref[0])
        b = _dot_nt(n2, wu_ref[0])
        a_ref[0] = a
        b_ref[0] = b
        ff = (a * _sigmoid(a) * b).astype(BF16)
        ff_ref[0] = ff
        acc[...] += _dot(ff, wd_ref[0])

        @pl.when(q == N_SHARD - 1)
        def _():
            h2 = acc[...]
            r = _rms(h2)
            xhat = h2 * r
            gf_row = gf_ref[...]
            rows = i * tm + lax.broadcasted_iota(jnp.int32, (tm, 1), 0)
            valid = (rows >= N_META) & (rows < n_valid)
            diff = jnp.where(valid, xhat * gf_row - tgt_ref[...], 0.0)
            loss_ref[...] += jnp.full(loss_ref.shape, 0.5 / D_MODEL, F32) * jnp.sum(diff * diff)
            dout = diff * (1.0 / D_MODEL)
            dgf_ref[...] += jnp.sum(dout * xhat, axis=0, keepdims=True)
            dh2_ref[...] = _rms_bwd(dout * gf_row, xhat, r)

    row = lambda w: pl.BlockSpec((tm, w), lambda i, q: (i, 0))
    shard_rows = pl.BlockSpec((1, FF_SHARD, D_MODEL), lambda i, q: (q, 0, 0))
    act = pl.BlockSpec((1, tm, FF_SHARD), lambda i, q: (q, i, 0))
    sds = jax.ShapeDtypeStruct
    return pl.pallas_call(
        body, grid=(nt, N_SHARD), name="fwd_ffn",
        in_specs=[row(D_MODEL), row(D_SSM), row(D_POOL), _full((D_MODEL, D_MODEL)), _full((1, D_MODEL)),
                  shard_rows, shard_rows, shard_rows, _full((1, D_MODEL)), row(D_MODEL)],
        out_specs=[row(D_MODEL), row(D_MODEL), act, act, act, row(D_MODEL), _full((8, 128)), _full((1, D_MODEL))],
        out_shape=[sds((n_pad, D_MODEL), F32), sds((n_pad, D_MODEL), BF16),
                   sds((N_SHARD, n_pad, FF_SHARD), F32), sds((N_SHARD, n_pad, FF_SHARD), F32),
                   sds((N_SHARD, n_pad, FF_SHARD), BF16), sds((n_pad, D_MODEL), F32),
                   sds((8, 128), F32), sds((1, D_MODEL), F32)],
        scratch_shapes=[pltpu.VMEM((tm, D_MODEL), F32)],
        compiler_params=_params(("arbitrary", "arbitrary")),
    )(h0, ms, mp, w_out_b, g2, wg_b, wu_b, wd_b, gf, target)


def _bwd_ffn(dh2, a, b, wg_b, wu_b, wd_b, h1, g2, tm):
    n_pad = dh2.shape[0]

    def body(dh2_ref, a_ref, b_ref, wg_ref, wu_ref, wd_ref, h1_ref, g2_ref, da_ref, db_ref, dh1_ref, dg2_ref, acc):
        i, q = pl.program_id(0), pl.program_id(1)

        @pl.when((i == 0) & (q == 0))
        def _():
            dg2_ref[...] = jnp.zeros_like(dg2_ref)

        dff = _dot_nt(dh2_ref[...].astype(BF16), wd_ref[0])
        a_v, b_v = a_ref[0], b_ref[0]
        sig = _sigmoid(a_v)
        da = (dff * b_v * sig * (1.0 + a_v * (1.0 - sig))).astype(BF16)
        db = (dff * a_v * sig).astype(BF16)
        da_ref[0] = da
        db_ref[0] = db
        part = _dot(da, wg_ref[0]) + _dot(db, wu_ref[0])

        @pl.when(q == 0)
        def _():
            acc[...] = part

        @pl.when(q > 0)
        def _():
            acc[...] += part

        @pl.when(q == N_SHARD - 1)
        def _():
            h1 = h1_ref[...]
            r = _rms(h1)
            xhat = h1 * r
            dn2 = acc[...]
            dg2_ref[...] += jnp.sum(dn2 * xhat, axis=0, keepdims=True)
            dh1_ref[...] = dh2_ref[...] + _rms_bwd(dn2 * g2_ref[...], xhat, r)

    row = lambda w: pl.BlockSpec((tm, w), lambda i, q: (i, 0))
    shard_rows = pl.BlockSpec((1, FF_SHARD, D_MODEL), lambda i, q: (q, 0, 0))
    act = pl.BlockSpec((1, tm, FF_SHARD), lambda i, q: (q, i, 0))
    sds = jax.ShapeDtypeStruct
    return pl.pallas_call(
        body, grid=(n_pad // tm, N_SHARD), name="bwd_ffn",
        in_specs=[row(D_MODEL), act, act, shard_rows, shard_rows, shard_rows, row(D_MODEL), _full((1, D_MODEL))],
        out_specs=[act, act, row(D_MODEL), _full((1, D_MODEL))],
        out_shape=[sds((N_SHARD, n_pad, FF_SHARD), BF16), sds((N_SHARD, n_pad, FF_SHARD), BF16),
                   sds((n_pad, D_MODEL), F32), sds((1, D_MODEL), F32)],
        scratch_shapes=[pltpu.VMEM((tm, D_MODEL), F32)],
        compiler_params=_params(("arbitrary", "arbitrary")),
    )(dh2, a, b, wg_b, wu_b, wd_b, h1, g2)


def _grad_ffn(n2, da, db, ff, dh2, tm):
    n_pad = n2.shape[0]

    def body(n2_ref, da_ref, db_ref, ff_ref, dh2_ref, dwg_ref, dwu_ref, dwd_ref):
        i = pl.program_id(1)
        n2_v = n2_ref[...]
        gg = _dot_tn(da_ref[0], n2_v)
        gu = _dot_tn(db_ref[0], n2_v)
        gd = _dot_tn(ff_ref[0], dh2_ref[...].astype(BF16))

        @pl.when(i == 0)
        def _():
            dwg_ref[0] = gg
            dwu_ref[0] = gu
            dwd_ref[0] = gd

        @pl.when(i > 0)
        def _():
            dwg_ref[0] += gg
            dwu_ref[0] += gu
            dwd_ref[0] += gd

    row = lambda w: pl.BlockSpec((tm, w), lambda q, i: (i, 0))
    act = pl.BlockSpec((1, tm, FF_SHARD), lambda q, i: (q, i, 0))
    sds = jax.ShapeDtypeStruct
    return pl.pallas_call(
        body, grid=(N_SHARD, n_pad // tm), name="grad_ffn",
        in_specs=[row(D_MODEL), act, act, act, row(D_MODEL)],
        out_specs=[pl.BlockSpec((1, FF_SHARD, D_MODEL), lambda q, i: (q, 0, 0))] * 3,
        out_shape=[sds((N_SHARD, FF_SHARD, D_MODEL), F32)] * 3,
        compiler_params=_params(("parallel", "arbitrary")),
    )(n2, da, db, ff, dh2)


def _bwd_out(dh1, ms, mp, w_out_b, tm, token):
    n_pad = dh1.shape[0]

    def body(dh1_ref, ms_ref, mp_ref, wo_ref, token_ref, dms_ref, dmp_ref, dwo_ref):
        i = pl.program_id(0)

        @pl.when(i == 0)
        def _():
            dwo_ref[...] = jnp.zeros_like(dwo_ref)

        d = dh1_ref[...].astype(BF16)
        dms_ref[...] = _dot_nt(d, wo_ref[:D_SSM, :])
        dmp_ref[...] = _dot_nt(d, wo_ref[D_SSM:, :])
        dwo_ref[:D_SSM, :] += _dot_tn(ms_ref[...], d)
        dwo_ref[D_SSM:, :] += _dot_tn(mp_ref[...], d)

    row = lambda w: pl.BlockSpec((tm, w), lambda i: (i, 0))
    sds = jax.ShapeDtypeStruct
    return pl.pallas_call(
        body, grid=(n_pad // tm,), name="bwd_out",
        in_specs=[row(D_MODEL), row(D_SSM), row(D_POOL), _full((D_MODEL, D_MODEL)), _ANY],
        out_specs=[row(D_SSM), row(D_POOL), _full((D_MODEL, D_MODEL))],
        out_shape=[sds((n_pad, D_SSM), F32), sds((n_pad, D_POOL), F32), sds((D_MODEL, D_MODEL), F32)],
        compiler_params=_params(("arbitrary",)),
    )(dh1, ms, mp, w_out_b, token)


def _bwd_in(du, dv, h0, dh1, g1, w_in_b, tm):
    n_pad = h0.shape[0]

    def body(du_ref, dv_ref, h0_ref, dh1_ref, g1_ref, w_ref, dh0_ref, dwi_ref, dg1_ref):
        i = pl.program_id(0)

        @pl.when(i == 0)
        def _():
            dwi_ref[...] = jnp.zeros_like(dwi_ref)
            dg1_ref[...] = jnp.zeros_like(dg1_ref)

        dub = du_ref[...].astype(BF16)
        dvb = dv_ref[...].astype(BF16)
        dn1 = _dot_nt(dub, w_ref[:, :D_SSM]) + _dot_nt(dvb, w_ref[:, D_SSM:])
        h = h0_ref[...]
        r = _rms(h)
        xhat = h * r
        g_row = g1_ref[...]
        n1 = (xhat * g_row).astype(BF16)
        dwi_ref[:, :D_SSM] += _dot_tn(n1, dub)
        dwi_ref[:, D_SSM:] += _dot_tn(n1, dvb)
        dg1_ref[...] += jnp.sum(dn1 * xhat, axis=0, keepdims=True)
        dh0_ref[...] = dh1_ref[...] + _rms_bwd(dn1 * g_row, xhat, r)

    row = lambda w: pl.BlockSpec((tm, w), lambda i: (i, 0))
    sds = jax.ShapeDtypeStruct
    return pl.pallas_call(
        body, grid=(n_pad // tm,), name="bwd_in",
        in_specs=[row(D_SSM), row(D_POOL), row(D_MODEL), row(D_MODEL), _full((1, D_MODEL)), _full((D_MODEL, D_MODEL))],
        out_specs=[row(D_MODEL), _full((D_MODEL, D_MODEL)), _full((1, D_MODEL))],
        out_shape=[sds((n_pad, D_MODEL), F32), sds((D_MODEL, D_MODEL), F32), sds((1, D_MODEL), F32)],
        compiler_params=_params(("arbitrary",)),
    )(du, dv, h0, dh1, g1, w_in_b)


def _planes_store(ref, j, val, tc):
    for i in range(4):
        ref[pl.ds(4 * j + i, tc, stride=STATE_BLOCKS), :] = val[:, 128 * i:128 * (i + 1)]


def _planes_load(ref, j, tc):
    return jnp.concatenate([ref[pl.ds(4 * j + i, tc, stride=STATE_BLOCKS), :] for i in range(4)], axis=1)


def _s5_tail(y, glu_ref, glub):
    g, dgelu = _gelu_and_grad(y)
    gb = g.astype(BF16)
    gate = jnp.concatenate([_dot(gb[:, 128 * j:128 * (j + 1)], glu_ref[j]) for j in range(SUPER)], axis=1) + glub
    sig = _sigmoid(gate)
    return g, gb, dgelu, sig, g * sig


def _s5_fwd(u, lam, bbr, bbi, crt, cit, vecs, glu, tc):
    n_pad = u.shape[0]

    def body(u_ref, lam_ref, bbr_ref, bbi_ref, crt_ref, cit_ref, vec_ref, glu_ref,
             sr_ref, si_ref, y_ref, ms_ref, hr_s, hi_s):
        @pl.when(pl.program_id(0) == 0)
        def _():
            hr_s[...] = jnp.zeros_like(hr_s)
            hi_s[...] = jnp.zeros_like(hi_s)

        u_v = u_ref[...]
        ub = u_v.astype(BF16)
        for j in range(SUPER):
            uj = ub[:, 128 * j:128 * (j + 1)]
            _planes_store(sr_ref, j, _dot(uj, bbr_ref[j]), tc)
            _planes_store(si_ref, j, _dot(uj, bbi_ref[j]), tc)

        ar = lam_ref[0:STATE_BLOCKS, :]
        ai = lam_ref[STATE_BLOCKS:, :]

        def step(t, carry):
            hr, hi = carry
            o = pl.multiple_of(t * STATE_BLOCKS, STATE_BLOCKS)
            nr = ar * hr - ai * hi + sr_ref[pl.ds(o, STATE_BLOCKS), :]
            ni = ar * hi + ai * hr + si_ref[pl.ds(o, STATE_BLOCKS), :]
            sr_ref[pl.ds(o, STATE_BLOCKS), :] = nr
            si_ref[pl.ds(o, STATE_BLOCKS), :] = ni
            return nr, ni

        hr, hi = lax.fori_loop(0, tc, step, (hr_s[...], hi_s[...]), unroll=4)
        hr_s[...] = hr
        hi_s[...] = hi

        d_row, glub, gs = vec_ref[0:1, :], vec_ref[1:2, :], vec_ref[2:3, :]
        ys_c = []
        for j in range(SUPER):
            sr_j = _planes_load(sr_ref, j, tc).astype(BF16)
            si_j = _planes_load(si_ref, j, tc).astype(BF16)
            ys_c.append(_dot(sr_j, crt_ref[j]) - _dot(si_j, cit_ref[j]))
        y = jnp.concatenate(ys_c, axis=1) + d_row * u_v
        y_ref[...] = y
        _, _, _, _, ys = _s5_tail(y, glu_ref, glub)
        ms_ref[...] = (ys * _rms(ys) * gs).astype(BF16)

    chunk = lambda w: pl.BlockSpec((tc, w), lambda c: (c, 0))
    states = pl.BlockSpec((tc * STATE_BLOCKS, 128), lambda c: (c, 0))
    sds = jax.ShapeDtypeStruct
    return pl.pallas_call(
        body, grid=(n_pad // tc,), name="s5_fwd",
        in_specs=[chunk(D_SSM), _full((2 * STATE_BLOCKS, 128)), _full((SUPER, 128, 512)), _full((SUPER, 128, 512)),
                  _full((SUPER, 512, 128)), _full((SUPER, 512, 128)), _full((8, D_SSM)), _full((SUPER, 128, 128))],
        out_specs=[states, states, chunk(D_SSM), chunk(D_SSM)],
        out_shape=[sds((n_pad * STATE_BLOCKS, 128), F32), sds((n_pad * STATE_BLOCKS, 128), F32),
                   sds((n_pad, D_SSM), F32), sds((n_pad, D_SSM), BF16)],
        scratch_shapes=[pltpu.VMEM((STATE_BLOCKS, 128), F32), pltpu.VMEM((STATE_BLOCKS, 128), F32)],
        compiler_params=_params(("arbitrary",)),
    )(u, lam, bbr, bbi, crt, cit, vecs, glu)


def _s5_bwd(dms, y, u, sr, si, lam, bbr, bbi, crt, cit, vecs, glu, tc, token):
    n_pad = u.shape[0]
    nc = n_pad // tc

    def body(dms_ref, y_ref, u_ref, sr_ref, si_ref, pr_ref, pi_ref, lam_ref, bbr_ref, bbi_ref, crt_ref, cit_ref,
             vec_ref, glu_ref, token_ref, du_ref, dbbr_ref, dbbi_ref, dcrt_ref, dcit_ref, dglu_ref, dvec_ref, dlam_ref,
             qr_s, qi_s, cr_s, ci_s):
        c = pl.program_id(0)

        @pl.when(c == 0)
        def _():
            for ref in (dbbr_ref, dbbi_ref, dcrt_ref, dcit_ref, dglu_ref, dvec_ref, dlam_ref, cr_s, ci_s):
                ref[...] = jnp.zeros_like(ref)

        d_row, glub, gs = vec_ref[0:1, :], vec_ref[1:2, :], vec_ref[2:3, :]
        y_v, u_v = y_ref[...], u_ref[...]
        ub = u_v.astype(BF16)
        g, gb, dgelu, sig, ys = _s5_tail(y_v, glu_ref, glub)
        r = _rms(ys)
        xhat = ys * r
        dm = dms_ref[...]
        dys = _rms_bwd(dm * gs, xhat, r)
        dgate = dys * g * sig * (1.0 - sig)
        dgateb = dgate.astype(BF16)
        dg = dys * sig + jnp.concatenate(
            [_dot_nt(dgateb[:, 128 * j:128 * (j + 1)], glu_ref[j]) for j in range(SUPER)], axis=1)
        dy = dg * dgelu
        dyb = dy.astype(BF16)
        dvec_ref[0:1, :] += jnp.sum(dy * u_v, axis=0, keepdims=True)
        dvec_ref[1:2, :] += jnp.sum(dgate, axis=0, keepdims=True)
        dvec_ref[2:3, :] += jnp.sum(dm * xhat, axis=0, keepdims=True)

        for j in range(SUPER):
            cols = slice(128 * j, 128 * (j + 1))
            dglu_ref[j] += _dot_tn(gb[:, cols], dgateb[:, cols])
            dcrt_ref[j] += _dot_tn(_planes_load(sr_ref, j, tc).astype(BF16), dyb[:, cols])
            dcit_ref[j] -= _dot_tn(_planes_load(si_ref, j, tc).astype(BF16), dyb[:, cols])
            _planes_store(qr_s, j, _dot_nt(dyb[:, cols], crt_ref[j]), tc)
            _planes_store(qi_s, j, -_dot_nt(dyb[:, cols], cit_ref[j]), tc)

        ar = lam_ref[0:STATE_BLOCKS, :]
        ai = lam_ref[STATE_BLOCKS:, :]

        def step(n, carry):
            qr, qi = carry
            o = pl.multiple_of((tc - 1 - n) * STATE_BLOCKS, STATE_BLOCKS)
            nr = ar * qr + ai * qi + qr_s[pl.ds(o, STATE_BLOCKS), :]
            ni = ar * qi - ai * qr + qi_s[pl.ds(o, STATE_BLOCKS), :]
            qr_s[pl.ds(o, STATE_BLOCKS), :] = nr
            qi_s[pl.ds(o, STATE_BLOCKS), :] = ni
            return nr, ni

        qr, qi = lax.fori_loop(0, tc, step, (cr_s[...], ci_s[...]), unroll=4)
        cr_s[...] = qr
        ci_s[...] = qi

        first = c == nc - 1
        row0 = lax.broadcasted_iota(jnp.int32, (tc, 1), 0) == 0
        for k in range(STATE_BLOCKS):
            plane = pl.ds(k, tc, stride=STATE_BLOCKS)
            pr = jnp.where(first, 0.0, pr_ref[k:k + 1, :])
            pi = jnp.where(first, 0.0, pi_ref[k:k + 1, :])
            hpr = jnp.where(row0, pr, pltpu.roll(sr_ref[plane, :], 1, 0))
            hpi = jnp.where(row0, pi, pltpu.roll(si_ref[plane, :], 1, 0))
            q_r, q_i = qr_s[plane, :], qi_s[plane, :]
            dlam_ref[k:k + 1, :] += jnp.sum(q_r * hpr + q_i * hpi, axis=0, keepdims=True)
            dlam_ref[STATE_BLOCKS + k:STATE_BLOCKS + k + 1, :] += jnp.sum(q_i * hpr - q_r * hpi, axis=0, keepdims=True)

        du_c = []
        for j in range(SUPER):
            cols = slice(128 * j, 128 * (j + 1))
            qr_j = _planes_load(qr_s, j, tc).astype(BF16)
            qi_j = _planes_load(qi_s, j, tc).astype(BF16)
            du_c.append(_dot_nt(qr_j, bbr_ref[j]) + _dot_nt(qi_j, bbi_ref[j]))
            dbbr_ref[j] += _dot_tn(ub[:, cols], qr_j)
            dbbi_ref[j] += _dot_tn(ub[:, cols], qi_j)
        du_ref[...] = jnp.concatenate(du_c, axis=1) + dy * d_row

    rev = lambda c: nc - 1 - c
    chunk = lambda w: pl.BlockSpec((tc, w), lambda c: (rev(c), 0))
    states = pl.BlockSpec((tc * STATE_BLOCKS, 128), lambda c: (rev(c), 0))
    prev = pl.BlockSpec((STATE_BLOCKS, 128), lambda c: (jnp.maximum(rev(c) * tc - 1, 0), 0))
    sds = jax.ShapeDtypeStruct
    return pl.pallas_call(
        body, grid=(nc,), name="s5_bwd",
        in_specs=[chunk(D_SSM), chunk(D_SSM), chunk(D_SSM), states, states, prev, prev,
                  _full((2 * STATE_BLOCKS, 128)), _full((SUPER, 128, 512)), _full((SUPER, 128, 512)),
                  _full((SUPER, 512, 128)), _full((SUPER, 512, 128)), _full((8, D_SSM)), _full((SUPER, 128, 128)), _ANY],
        out_specs=[chunk(D_SSM), _full((SUPER, 128, 512)), _full((SUPER, 128, 512)), _full((SUPER, 512, 128)),
                   _full((SUPER, 512, 128)), _full((SUPER, 128, 128)), _full((8, D_SSM)), _full((2 * STATE_BLOCKS, 128))],
        out_shape=[sds((n_pad, D_SSM), F32), sds((SUPER, 128, 512), F32), sds((SUPER, 128, 512), F32),
                   sds((SUPER, 512, 128), F32), sds((SUPER, 512, 128), F32), sds((SUPER, 128, 128), F32),
                   sds((8, D_SSM), F32), sds((2 * STATE_BLOCKS, 128), F32)],
        scratch_shapes=[pltpu.VMEM((tc * STATE_BLOCKS, 128), F32), pltpu.VMEM((tc * STATE_BLOCKS, 128), F32),
                        pltpu.VMEM((STATE_BLOCKS, 128), F32), pltpu.VMEM((STATE_BLOCKS, 128), F32)],
        compiler_params=_params(("arbitrary",)),
    )(dms, y, u, sr, si, sr, si, lam, bbr, bbi, crt, cit, vecs, glu, token)


def _inv_count(c_idx, tc, w):
    t = c_idx * tc + lax.broadcasted_iota(jnp.int32, (tc, 1), 0)
    return 1.0 / jnp.minimum(t + 1, w).astype(F32)


def _pool_fwd(v, pw_b, vecs, tc):
    n_pad = v.shape[0]

    def body(v_ref, pw_ref, vec_ref, feat_ref, mp_ref, hist):
        c = pl.program_id(0)

        @pl.when(c == 0)
        def _():
            hist[...] = jnp.zeros_like(hist)

        v_v = v_ref[...]
        ext = jnp.concatenate([hist[...], v_v], axis=0)
        hist[...] = v_v[tc - POOL_HALO:, :]
        feats, ps = [], []
        for k, w in enumerate(POOL_WINDOWS):
            cols = slice(128 * k, 128 * (k + 1))
            s = ext[:, cols]
            sh = 1
            while sh < w:
                s = s + pltpu.roll(s, sh, 0)
                sh *= 2
            f = (s[POOL_HALO:, :] * _inv_count(c, tc, w) - v_v[:, cols]).astype(BF16)
            feats.append(f)
            ps.append(_dot(f, pw_ref[k]))
        feat_ref[...] = jnp.concatenate(feats, axis=1)
        yp = jnp.concatenate(ps, axis=1) * vec_ref[0:1, :]
        mp_ref[...] = (yp * _rms(yp) * vec_ref[1:2, :]).astype(BF16)

    chunk = lambda w: pl.BlockSpec((tc, w), lambda c: (c, 0))
    sds = jax.ShapeDtypeStruct
    return pl.pallas_call(
        body, grid=(n_pad // tc,), name="pool_fwd",
        in_specs=[chunk(D_POOL), _full((4, 128, 128)), _full((8, D_POOL))],
        out_specs=[chunk(D_POOL), chunk(D_POOL)],
        out_shape=[sds((n_pad, D_POOL), BF16), sds((n_pad, D_POOL), BF16)],
        scratch_shapes=[pltpu.VMEM((POOL_HALO, D_POOL), F32)],
        compiler_params=_params(("arbitrary",)),
    )(v, pw_b, vecs)


def _pool_bwd(dmp, feat, pw_b, vecs, tc):
    n_pad = dmp.shape[0]
    nc = n_pad // tc

    def body(dmp_ref, feat_ref, pw_ref, vec_ref, dv_ref, dpw_ref, dvec_ref, fut):
        c = pl.program_id(0)

        @pl.when(c == 0)
        def _():
            fut[...] = jnp.zeros_like(fut)
            dpw_ref[...] = jnp.zeros_like(dpw_ref)
            dvec_ref[...] = jnp.zeros_like(dvec_ref)

        scale, gp = vec_ref[0:1, :], vec_ref[1:2, :]
        feat_v = feat_ref[...]
        p = jnp.concatenate([_dot(feat_v[:, 128 * k:128 * (k + 1)], pw_ref[k]) for k in range(4)], axis=1)
        yp = p * scale
        r = _rms(yp)
        xhat = yp * r
        dm = dmp_ref[...]
        dyp = _rms_bwd(dm * gp, xhat, r)
        dvec_ref[0:1, :] += jnp.sum(dyp * p, axis=0, keepdims=True)
        dvec_ref[1:2, :] += jnp.sum(dm * xhat, axis=0, keepdims=True)
        dpb = (dyp * scale).astype(BF16)
        es, dfs = [], []
        for k, w in enumerate(POOL_WINDOWS):
            cols = slice(128 * k, 128 * (k + 1))
            dpw_ref[k] += _dot_tn(feat_v[:, cols], dpb[:, cols])
            df = _dot_nt(dpb[:, cols], pw_ref[k])
            dfs.append(df)
            es.append(df * _inv_count(nc - 1 - c, tc, w))
        e = jnp.concatenate(es, axis=1)
        ext = jnp.concatenate([e, fut[...]], axis=0)
        fut[...] = e[:POOL_HALO, :]
        n_ext = tc + POOL_HALO
        dvs = []
        for k, w in enumerate(POOL_WINDOWS):
            s = ext[:, 128 * k:128 * (k + 1)]
            sh = 1
            while sh < w:
                s = s + pltpu.roll(s, n_ext - sh, 0)
                sh *= 2
            dvs.append(s[:tc, :] - dfs[k])
        dv_ref[...] = jnp.concatenate(dvs, axis=1)

    chunk = lambda w: pl.BlockSpec((tc, w), lambda c: (nc - 1 - c, 0))
    sds = jax.ShapeDtypeStruct
    return pl.pallas_call(
        body, grid=(nc,), name="pool_bwd",
        in_specs=[chunk(D_POOL), chunk(D_POOL), _full((4, 128, 128)), _full((8, D_POOL))],
        out_specs=[chunk(D_POOL), _full((4, 128, 128)), _full((8, D_POOL))],
        out_shape=[sds((n_pad, D_POOL), F32), sds((4, 128, 128), F32), sds((8, D_POOL), F32)],
        scratch_shapes=[pltpu.VMEM((POOL_HALO, D_POOL), F32)],
        compiler_params=_params(("arbitrary",)),
    )(dmp, feat, pw_b, vecs)


def _place():
    x, y, c = lax.axis_index("x"), lax.axis_index("y"), lax.axis_index("c")
    chips = [(1 - x, y), (x, 1 - y), (1 - x, 1 - y)]
    return x, y, c, chips


_ANY = pl.BlockSpec(memory_space=pl.ANY)


def _cast_shards(shards, dtypes, place):
    n = len(shards)

    def body(place_ref, *refs):
        for i in range(n):
            refs[n + i][0] = refs[i][...].astype(dtypes[i])

    return pl.pallas_call(
        body, name="cast_shards",
        grid_spec=pltpu.PrefetchScalarGridSpec(
            num_scalar_prefetch=1, grid=(1,),
            in_specs=[pl.BlockSpec(s.shape, lambda i, p: (0, 0, 0)) for s in shards],
            out_specs=[pl.BlockSpec((1,) + s.shape, lambda i, p: (p[0], 0, 0, 0)) for s in shards]),
        out_shape=[jax.ShapeDtypeStruct((N_SHARD,) + s.shape, dt) for s, dt in zip(shards, dtypes)],
        compiler_params=_params(("arbitrary",)),
    )(place, *shards)


def _gather_shards(full):
    n = len(full)

    def body(*refs):
        outs = refs[n:2 * n]
        ici_send, ici_recv, d2d_send, d2d_recv = refs[2 * n:]
        x, y, c, chips = _place()
        q = 2 * x + y
        sibling = (x, y, 1 - c)

        def ici(i, j, shard, to):
            return pltpu.make_async_remote_copy(src_ref=outs[i].at[q, c], dst_ref=outs[i].at[shard, c],
                                                send_sem=ici_send.at[i, j], recv_sem=ici_recv.at[i, j],
                                                device_id=to, device_id_type=MESH)

        def d2d(i, j, shard, half):
            return pltpu.make_async_remote_copy(src_ref=outs[i].at[shard, c], dst_ref=outs[i].at[shard, half],
                                                send_sem=d2d_send.at[i, j], recv_sem=d2d_recv.at[i, j],
                                                device_id=sibling, device_id_type=MESH)

        sends = [ici(i, j, q, (*chip, c)) for i in range(n) for j, chip in enumerate(chips)]
        for cp in sends:
            cp.start()
        passed = []
        for i in range(n):
            for j, (cx, cy) in enumerate(chips):
                ici(i, j, 2 * cx + cy, (cx, cy, c)).wait_recv()
                cp = d2d(i, j, 2 * cx + cy, c)
                cp.start()
                passed.append(cp)
        for i in range(n):
            for j, (cx, cy) in enumerate(chips):
                d2d(i, j, 2 * cx + cy, 1 - c).wait_recv()
        for cp in sends + passed:
            cp.wait_send()

    return pl.pallas_call(
        body, name="gather_shards",
        in_specs=[_ANY] * n, out_specs=[_ANY] * n,
        out_shape=[jax.ShapeDtypeStruct(f.shape, f.dtype) for f in full],
        input_output_aliases={i: i for i in range(n)},
        scratch_shapes=[pltpu.SemaphoreType.DMA((n, 3)), pltpu.SemaphoreType.DMA((n, 3)),
                        pltpu.SemaphoreType.DMA((n, 3)), pltpu.SemaphoreType.DMA((n, 3))],
    )(*full)


def _forward_halves(full):
    n = len(full)

    def body(*refs):
        outs = refs[n:2 * n]
        send, recv = refs[2 * n:]
        x, y, c, chips = _place()

        def d2d(i, j, shard, half):
            return pltpu.make_async_remote_copy(src_ref=outs[i].at[shard, c], dst_ref=outs[i].at[shard, half],
                                                send_sem=send.at[i, j], recv_sem=recv.at[i, j],
                                                device_id=(x, y, 1 - c), device_id_type=MESH)

        cps = [d2d(i, j, 2 * cx + cy, c) for i in range(n) for j, (cx, cy) in enumerate(chips)]
        for cp in cps:
            cp.start()
        for i in range(n):
            for j, (cx, cy) in enumerate(chips):
                d2d(i, j, 2 * cx + cy, 1 - c).wait_recv()
        for cp in cps:
            cp.wait_send()

    return pl.pallas_call(
        body, name="forward_halves",
        in_specs=[_ANY] * n, out_specs=[_ANY] * n,
        out_shape=[jax.ShapeDtypeStruct(f.shape, f.dtype) for f in full],
        input_output_aliases={i: i for i in range(n)},
        scratch_shapes=[pltpu.SemaphoreType.DMA((n, 3)), pltpu.SemaphoreType.DMA((n, 3))],
    )(*full)


_HBM = pl.BlockSpec(memory_space=pltpu.HBM)
_SEM = pl.BlockSpec(memory_space=pltpu.SEMAPHORE)
_EFFECT = pltpu.SideEffectType.DATAFLOW_SIDE_EFFECTING


def _copies_start(name, arrays, sem_shape, build):
    n = len(arrays)

    def body(*refs):
        outs = refs[n:2 * n]
        send, recv, token = refs[2 * n:]
        sends, _ = build(outs, send, recv)
        for cp in sends:
            cp.start()
        token[...] = jnp.zeros_like(token)

    out = pl.pallas_call(
        body, name=name, in_specs=[_HBM] * n,
        out_specs=[_HBM] * n + [_SEM, _SEM, pl.BlockSpec(memory_space=pltpu.VMEM)],
        out_shape=[pltpu.HBM(a.shape, a.dtype) for a in arrays]
        + [pltpu.SemaphoreType.DMA(sem_shape), pltpu.SemaphoreType.DMA(sem_shape), jax.ShapeDtypeStruct((8, 128), F32)],
        input_output_aliases={i: i for i in range(n)},
        compiler_params=pltpu.CompilerParams(has_side_effects=_EFFECT),
    )(*[pltpu.with_memory_space_constraint(a, pltpu.HBM) for a in arrays])
    return list(out[:n]), (out[n], out[n + 1]), out[n + 2]


def _copies_wait(name, arrays, sems, after, build):
    n = len(arrays)

    def body(*refs):
        ins = refs[:n]
        send, recv = refs[n], refs[n + 1]
        sends, recvs = build(ins, send, recv)
        for cp in sends:
            cp.wait_send()
        for cp in recvs:
            cp.wait_recv()

    return list(pl.pallas_call(
        body, name=name, in_specs=[_HBM] * n + [_SEM, _SEM, _ANY], out_specs=[_HBM] * n,
        out_shape=[pltpu.HBM(a.shape, a.dtype) for a in arrays],
        input_output_aliases={i: i for i in range(n)},
        compiler_params=pltpu.CompilerParams(has_side_effects=_EFFECT),
    )(*arrays, *sems, after))


def _remote(src, dst, send_sem, recv_sem, to):
    return pltpu.make_async_remote_copy(src_ref=src, dst_ref=dst, send_sem=send_sem, recv_sem=recv_sem,
                                        device_id=to, device_id_type=MESH)


def _build_gather(refs, send, recv):
    x, y, c, chips = _place()
    q = 2 * x + y
    pairs = [(i, j, chip) for i in range(len(refs)) for j, chip in enumerate(chips)]
    sends = [_remote(refs[i].at[q, c], refs[i].at[q, c], send.at[3 * i + j], recv.at[3 * i + j], (cx, cy, c))
             for i, j, (cx, cy) in pairs]
    recvs = [_remote(refs[i].at[q, c], refs[i].at[2 * cx + cy, c], send.at[3 * i + j], recv.at[3 * i + j], (cx, cy, c))
             for i, j, (cx, cy) in pairs]
    return sends, recvs


def _build_swap(refs, send, recv):
    x, y, c, _ = _place()
    n = len(refs) // 2
    cps = [_remote(refs[i].at[:, 1 - c], refs[n + i], send.at[i], recv.at[i], (x, y, 1 - c)) for i in range(n)]
    return cps, cps


def _build_exchange(refs, send, recv):
    x, y, c, chips = _place()
    n = len(refs) // 2
    cps = [_remote(refs[i].at[2 * cx + cy], refs[n + i].at[j], send.at[3 * i + j], recv.at[3 * i + j], (cx, cy, c))
           for i in range(n) for j, (cx, cy) in enumerate(chips)]
    return cps, cps


def _swap_halves(grads):
    n = len(grads)

    def body(*refs):
        ins, outs = refs[:n], refs[n:2 * n]
        send, recv = refs[2 * n:]
        x, y, c, _ = _place()
        cps = [pltpu.make_async_remote_copy(src_ref=ins[i].at[:, 1 - c], dst_ref=outs[i], send_sem=send.at[i],
                                            recv_sem=recv.at[i], device_id=(x, y, 1 - c), device_id_type=MESH)
               for i in range(n)]
        for cp in cps:
            cp.start()
        for cp in cps:
            cp.wait()

    return pl.pallas_call(
        body, name="swap_halves",
        in_specs=[_ANY] * n, out_specs=[_ANY] * n,
        out_shape=[jax.ShapeDtypeStruct((N_SHARD,) + g.shape[2:], F32) for g in grads],
        scratch_shapes=[pltpu.SemaphoreType.DMA((n,)), pltpu.SemaphoreType.DMA((n,))],
    )(*grads)


def _exchange_chips(parts):
    n = len(parts)

    def body(*refs):
        ins, outs = refs[:n], refs[n:2 * n]
        send, recv = refs[2 * n:]
        x, y, c, chips = _place()
        cps = [pltpu.make_async_remote_copy(src_ref=ins[i].at[2 * cx + cy], dst_ref=outs[i].at[j], send_sem=send.at[i, j],
                                            recv_sem=recv.at[i, j], device_id=(cx, cy, c), device_id_type=MESH)
               for i in range(n) for j, (cx, cy) in enumerate(chips)]
        for cp in cps:
            cp.start()
        for cp in cps:
            cp.wait()

    return pl.pallas_call(
        body, name="exchange_chips",
        in_specs=[_ANY] * n, out_specs=[_ANY] * n,
        out_shape=[jax.ShapeDtypeStruct((3,) + p.shape[1:], BF16) for p in parts],
        scratch_shapes=[pltpu.SemaphoreType.DMA((n, 3)), pltpu.SemaphoreType.DMA((n, 3))],
    )(*parts)


def _join_halves(pairs):
    n = len(pairs)

    def body(*refs):
        outs = refs[n:2 * n]
        send, recv = refs[2 * n:]
        x, y, c, _ = _place()
        cps = [pltpu.make_async_remote_copy(src_ref=outs[i].at[c], dst_ref=outs[i].at[c], send_sem=send.at[i],
                                            recv_sem=recv.at[i], device_id=(x, y, 1 - c), device_id_type=MESH)
               for i in range(n)]
        for cp in cps:
            cp.start()
        for i in range(n):
            cps[i].wait_send()
            pltpu.make_async_remote_copy(src_ref=outs[i].at[c], dst_ref=outs[i].at[1 - c], send_sem=send.at[i],
                                         recv_sem=recv.at[i], device_id=(x, y, 1 - c), device_id_type=MESH).wait_recv()

    return pl.pallas_call(
        body, name="join_halves",
        in_specs=[_ANY] * n, out_specs=[_ANY] * n,
        out_shape=[jax.ShapeDtypeStruct(p.shape, F32) for p in pairs],
        input_output_aliases={i: i for i in range(n)},
        scratch_shapes=[pltpu.SemaphoreType.DMA((n,)), pltpu.SemaphoreType.DMA((n,))],
    )(*pairs)


def _gather_small(part):
    m_per, n = part.shape

    def body(x_ref, out_ref, send_sems, recv_sems, local_sem):
        x, y, c, chips = _place()
        me, sibling = (x, y, c), (x, y, 1 - c)

        def rows(px, py, pc):
            return out_ref.at[pl.ds((4 * px + 2 * py + pc) * m_per, m_per), :]

        def copy(k, block, to, src=None):
            return pltpu.make_async_remote_copy(src_ref=rows(*block) if src is None else src, dst_ref=rows(*block),
                                                send_sem=send_sems.at[k], recv_sem=recv_sems.at[k],
                                                device_id=to, device_id_type=MESH)

        mine = pltpu.make_async_copy(x_ref, rows(*me), local_sem)
        mine.start()
        first = [copy(0, me, sibling, src=x_ref)]
        first += [copy(1 + j, me, (*chip, c), src=x_ref) for j, chip in enumerate(chips)]
        for cp in first:
            cp.start()
        passed = [copy(4 + j, (*chip, c), sibling) for j, chip in enumerate(chips)]
        for j, chip in enumerate(chips):
            copy(1 + j, (*chip, c), me).wait_recv()
            passed[j].start()
        copy(0, sibling, me).wait_recv()
        for j, chip in enumerate(chips):
            copy(4 + j, (*chip, 1 - c), me).wait_recv()
        for cp in first + passed:
            cp.wait_send()
        mine.wait()

    return pl.pallas_call(
        body, name="gather_small",
        out_shape=jax.ShapeDtypeStruct((8 * m_per, n), F32),
        in_specs=[pl.BlockSpec(memory_space=pltpu.VMEM)], out_specs=pl.BlockSpec(memory_space=pltpu.VMEM),
        scratch_shapes=[pltpu.SemaphoreType.DMA((7,)), pltpu.SemaphoreType.DMA((7,)), pltpu.SemaphoreType.DMA],
        compiler_params=_params(),
    )(part)


N_SPLIT = 2


def _sum_siblings(tag, grads, recvd, place):
    n = len(grads)

    def body(place_ref, *refs):
        g_refs, r_refs, sb_refs, own_refs = (refs[k * n:(k + 1) * n] for k in range(4))
        s = pl.program_id(1)
        for i in range(n):
            tot = g_refs[i][0, 0] + r_refs[i][0]
            sb_refs[i][0] = tot.astype(BF16)

            @pl.when(s == place_ref[0])
            def _():
                own_refs[i][...] = tot

    in_specs, sb_specs, own_specs, sb_shapes, own_shapes = [], [], [], [], []
    for g in grads:
        _, _, r, cdim = g.shape
        rb = r // N_SPLIT
        in_specs.append(pl.BlockSpec((1, 1, rb, cdim), lambda b, s, p: (s, p[1], b, 0)))
        sb_specs.append(pl.BlockSpec((1, rb, cdim), lambda b, s, p: (s, b, 0)))
        own_specs.append(pl.BlockSpec((rb, cdim), lambda b, s, p: (b, 0)))
        sb_shapes.append(jax.ShapeDtypeStruct((N_SHARD, r, cdim), BF16))
        own_shapes.append(jax.ShapeDtypeStruct((r, cdim), F32))
    out = pl.pallas_call(
        body, name="sum_siblings_" + tag,
        grid_spec=pltpu.PrefetchScalarGridSpec(
            num_scalar_prefetch=1, grid=(N_SPLIT, N_SHARD),
            in_specs=in_specs + sb_specs, out_specs=sb_specs + own_specs),
        out_shape=sb_shapes + own_shapes,
        compiler_params=_params(("parallel", "arbitrary")),
    )(place, *grads, *recvd)
    return out[:n], out[n:]


def _sum_chips(own, recvd, place):
    n = len(own)

    def body(place_ref, *refs):
        o_refs, r_refs, out_refs = (refs[k * n:(k + 1) * n] for k in range(3))
        for i in range(n):
            tot = o_refs[i][...]
            for j in range(3):
                tot = tot + r_refs[i][j].astype(F32)
            out_refs[i][0] = tot

    o_specs, r_specs, out_specs = [], [], []
    for o in own:
        r, cdim = o.shape
        rb = r // N_SPLIT
        o_specs.append(pl.BlockSpec((rb, cdim), lambda b, p: (b, 0)))
        r_specs.append(pl.BlockSpec((3, rb, cdim), lambda b, p: (0, b, 0)))
        out_specs.append(pl.BlockSpec((1, rb, cdim), lambda b, p: (p[1], b, 0)))
    return pl.pallas_call(
        body, name="sum_chips",
        grid_spec=pltpu.PrefetchScalarGridSpec(num_scalar_prefetch=1, grid=(N_SPLIT,),
                                               in_specs=o_specs + r_specs, out_specs=out_specs),
        out_shape=[jax.ShapeDtypeStruct((2,) + o.shape, F32) for o in own],
        compiler_params=_params(("parallel",)),
    )(place, *own, *recvd)


def _adamw_math(w, g, m, v):
    m = ADAM_B1 * m + (1.0 - ADAM_B1) * g
    v = ADAM_B2 * v + (1.0 - ADAM_B2) * (g * g)
    m_hat = m / (1.0 - ADAM_B1 ** ADAM_STEP)
    v_hat = v / (1.0 - ADAM_B2 ** ADAM_STEP)
    delta = -ADAM_LR * (m_hat / (jnp.sqrt(v_hat) + ADAM_EPS) + ADAM_WD * w)
    return delta, m, v


def _adamw(name, ws, gs, ms, vs, n_split):
    n = len(ws)

    def body(*refs):
        w_r, g_r, m_r, v_r, d_o, m_o, v_o = (refs[k * n:(k + 1) * n] for k in range(7))
        for i in range(n):
            d, m, v = _adamw_math(w_r[i][...], g_r[i][...], m_r[i][...], v_r[i][...])
            d_o[i][...] = d
            m_o[i][...] = m
            v_o[i][...] = v

    specs = [pl.BlockSpec((w.shape[0] // n_split, w.shape[1]), lambda b: (b, 0)) for w in ws]
    shapes = [jax.ShapeDtypeStruct(w.shape, F32) for w in ws]
    out = pl.pallas_call(
        body, name=name, grid=(n_split,),
        in_specs=specs * 4, out_specs=specs * 3, out_shape=shapes * 3,
        compiler_params=_params(("parallel",)),
    )(*ws, *gs, *ms, *vs)
    return out[:n], out[n:2 * n], out[2 * n:]


def _reduce_small(gathered, w, m, v):
    rows = w.shape[0]

    def body(ga_ref, w_ref, m_ref, v_ref, g_out, d_out, m_out, v_out):
        g = ga_ref[0:rows, :]
        for k in range(1, 8):
            g = g + ga_ref[k * rows:(k + 1) * rows, :]
        g_out[...] = g
        d, mm, vv = _adamw_math(w_ref[...], g, m_ref[...], v_ref[...])
        d_out[...] = d
        m_out[...] = mm
        v_out[...] = vv

    return pl.pallas_call(
        body, name="reduce_small",
        out_shape=[jax.ShapeDtypeStruct(w.shape, F32)] * 4,
        compiler_params=_params(),
    )(gathered, w, m, v)


def _s5_operands(lam_re, lam_im, log_step, b_re, b_im, c_re, c_im, glu_w):
    lr = jnp.minimum(lam_re, -1e-4)
    li = lam_im
    step = jnp.exp(log_step)[:, None]
    mag = jnp.exp(lr * step)
    ang = li * step
    abr = mag * jnp.cos(ang)
    abi = mag * jnp.sin(ang)
    nr = abr - 1.0
    ni = abi
    den = lr * lr + li * li
    cr = ((nr * lr + ni * li) / den)[..., None]
    ci = ((ni * lr - nr * li) / den)[..., None]
    bbr = cr * b_re - ci * b_im
    bbi = cr * b_im + ci * b_re
    eye = jnp.eye(8, dtype=F32)
    g, h, p = SSM_GROUPS // SUPER, SSM_GROUP, SSM_STATE

    def b_layout(t):
        return jnp.einsum("ab,japh->jahbp", eye, t.reshape(SUPER, g, p, h)).reshape(SUPER, g * h, g * p)

    def c_layout(t):
        return jnp.einsum("ab,jahp->jbpah", eye, t.reshape(SUPER, g, h, p)).reshape(SUPER, g * p, g * h)

    glu = jnp.einsum("ab,jahk->jahbk", eye, glu_w.reshape(SUPER, g, h, h)).reshape(SUPER, g * h, g * h)
    lam = jnp.concatenate([abr.reshape(STATE_BLOCKS, 128), abi.reshape(STATE_BLOCKS, 128)], axis=0)
    return lam, b_layout(bbr), b_layout(bbi), c_layout(c_re), c_layout(c_im), glu


def _pad_rows(a, rows):
    return jnp.pad(a, ((0, rows - a.shape[0]), (0, 0)))


def _pack(parts):
    rows = []
    for a in parts:
        flat = a.reshape(-1)
        n = -(-flat.shape[0] // 128)
        rows.append(jnp.pad(flat, (0, n * 128 - flat.shape[0])).reshape(n, 128))
    out = jnp.concatenate(rows, axis=0)
    return _pad_rows(out, -(-out.shape[0] // 8) * 8)


def _unpack(packed, like):
    out, at = [], 0
    for a in like:
        n = -(-a.size // 128)
        out.append(packed[at:at + n].reshape(-1)[:a.size].reshape(a.shape))
        at += n
    return out


SMALL = ("norm1_g", "ssm_lambda_re", "ssm_lambda_im", "ssm_log_step", "ssm_b_re", "ssm_b_im", "ssm_c_re", "ssm_c_im",
         "ssm_d", "ssm_glu_w", "ssm_glu_b", "ssm_norm_g", "pool_w", "pool_scale", "pool_norm_g", "norm2_g",
         "final_norm_g")
LARGE = ("w_in", "w_out", "w_gate", "w_up", "w_down")
WEIGHTS = ("meta_tokens", "norm1_g", "w_in", "ssm_lambda_re", "ssm_lambda_im", "ssm_log_step", "ssm_b_re", "ssm_b_im",
           "ssm_c_re", "ssm_c_im", "ssm_d", "ssm_glu_w", "ssm_glu_b", "ssm_norm_g", "pool_w", "pool_scale",
           "pool_norm_g", "w_out", "norm2_g", "w_gate", "w_up", "w_down", "final_norm_g")


def _step(x, target, w, m, v):
    seq = x.shape[1]
    n_rows = N_META + seq
    n_pad, tm, tc = _plan(n_rows)
    xq, yq, cq = lax.axis_index("x"), lax.axis_index("y"), lax.axis_index("c")
    place = jnp.stack([2 * xq + yq, cq]).astype(jnp.int32)

    def halves(a2d):
        return a2d.reshape(2, a2d.shape[0] // 2, a2d.shape[1])

    def local2d(t):
        return {"w_gate": lambda a: a[0].T, "w_up": lambda a: a[0].T}.get(t, lambda a: a[0])

    shards = [halves(local2d(k)(w[k])) for k in LARGE] + [halves(w["meta_tokens"])]
    full = _cast_shards(shards, [BF16] * len(LARGE) + [F32], place)
    w_in_full, meta_full = _gather_shards([full[0], full[5]])
    late, gather_sems, gather_token = _copies_start("gather_start", list(full[1:5]), (12,), _build_gather)
    w_in_b = w_in_full.reshape(D_MODEL, D_MODEL)
    meta = meta_full.reshape(N_SHARD, N_META, D_MODEL // N_SHARD).transpose(1, 0, 2).reshape(N_META, D_MODEL)

    h0 = _pad_rows(jnp.concatenate([meta, x[0]], axis=0), n_pad)
    tgt = _pad_rows(jnp.concatenate([jnp.zeros((N_META, D_MODEL), F32), target[0]], axis=0), n_pad)
    s5_in = (w["ssm_lambda_re"][0], w["ssm_lambda_im"][0], w["ssm_log_step"][0], w["ssm_b_re"][0], w["ssm_b_im"][0],
             w["ssm_c_re"][0], w["ssm_c_im"][0], w["ssm_glu_w"][0])
    (lam, bbr, bbi, crt, cit, glu), s5_vjp = jax.vjp(_s5_operands, *s5_in)
    bbr_b, bbi_b, crt_b, cit_b, glu_b16 = (t.astype(BF16) for t in (bbr, bbi, crt, cit, glu))
    s5_vecs = _pad_rows(jnp.concatenate([w["ssm_d"].reshape(1, D_SSM), w["ssm_glu_b"].reshape(1, D_SSM),
                                         w["ssm_norm_g"].reshape(1, D_SSM)], axis=0), 8)
    pool_vecs = _pad_rows(jnp.concatenate([w["pool_scale"].reshape(1, D_POOL), w["pool_norm_g"].reshape(1, D_POOL)],
                                          axis=0), 8)
    pw_b = w["pool_w"][0].astype(BF16)
    g1, g2, gf = w["norm1_g"].reshape(1, D_MODEL), w["norm2_g"].reshape(1, D_MODEL), w["final_norm_g"].reshape(1, D_MODEL)

    u, vv = _fwd_in(h0, g1, w_in_b, tm, gather_token)
    sr, si, y, ms = _s5_fwd(u, lam, bbr_b, bbi_b, crt_b, cit_b, s5_vecs, glu_b16, tc)
    feat, mp = _pool_fwd(vv, pw_b, pool_vecs, tc)
    late = _forward_halves(_copies_wait("gather_wait", late, gather_sems, mp, _build_gather))
    w_out_b = late[0].reshape(D_MODEL, D_MODEL)
    wg_b, wu_b, wd_b = (t.reshape(N_SHARD, FF_SHARD, D_MODEL) for t in late[1:])
    h1, n2, a, b, ff, dh2, loss_acc, dgf = _fwd_ffn(h0, ms, mp, w_out_b, g2, wg_b, wu_b, wd_b, gf, tgt, tm, n_rows)
    loss = lax.psum(loss_acc[0, 0], AXES)

    def quarters(t):
        if t.ndim == 2:
            t = t.reshape(N_SHARD, t.shape[0] // N_SHARD, t.shape[1])
        return t.reshape(N_SHARD, 2, t.shape[1] // 2, t.shape[2])

    def landing(like, lead, dtype):
        return [lax.empty((lead,) + t.shape[2:], dtype) for t in like]

    da, db, dh1, dg2 = _bwd_ffn(dh2, a, b, wg_b, wu_b, wd_b, h1, g2, tm)
    ffn_g = [quarters(t) for t in _grad_ffn(n2, da, db, ff, dh2, tm)]
    nf = len(ffn_g)
    moved, swap_sems, swap_token = _copies_start("swap_start", ffn_g + landing(ffn_g, N_SHARD, F32), (nf,), _build_swap)
    dms, dmp, dwo = _bwd_out(dh1, ms, mp, w_out_b, tm, swap_token)
    moved = _copies_wait("swap_wait", moved, swap_sems, dwo, _build_swap)
    ffn_parts, ffn_own = _sum_siblings("ffn", moved[:nf], moved[nf:], place)
    moved, exch_sems, exch_token = _copies_start("exchange_start", list(ffn_parts) + landing(ffn_g, 3, BF16), (3 * nf,),
                                                 _build_exchange)
    du, dbbr, dbbi, dcrt, dcit, dglu, ds5v, dlam = _s5_bwd(dms, y, u, sr, si, lam, bbr_b, bbi_b, crt_b, cit_b,
                                                           s5_vecs, glu_b16, tc, exch_token)
    dv, dpw, dpoolv = _pool_bwd(dmp, feat, pw_b, pool_vecs, tc)
    dh0, dwi, dg1 = _bwd_in(du, dv, h0, dh1, g1, w_in_b, tm)
    ffn_from_chips = _copies_wait("exchange_wait", moved, exch_sems, dh0, _build_exchange)[nf:]
    d_lre, d_lim, d_lstep, d_bre, d_bim, d_cre, d_cim, d_gluw = s5_vjp((dlam, dbbr, dbbi, dcrt, dcit, dglu))
    grad_x = dh0[N_META:n_rows][None]

    mix_g = [quarters(t) for t in (dwi, dwo)]
    mix_parts, mix_own = _sum_siblings("mix", mix_g, _swap_halves(mix_g), place)
    mix_from_chips = _exchange_chips(mix_parts)
    joined = _join_halves(_sum_chips(list(mix_own) + list(ffn_own), list(mix_from_chips) + list(ffn_from_chips), place))
    g_large = [j.reshape(j.shape[0] * j.shape[1], j.shape[2]) for j in joined]
    w2d, m2d, v2d = ([local2d(k)(t[k]) for k in LARGE] for t in (w, m, v))
    d_large, m_large, v_large = _adamw("adamw_large", w2d, g_large, m2d, v2d, 8)

    small_g = {
        "norm1_g": dg1, "ssm_lambda_re": d_lre, "ssm_lambda_im": d_lim, "ssm_log_step": d_lstep, "ssm_b_re": d_bre,
        "ssm_b_im": d_bim, "ssm_c_re": d_cre, "ssm_c_im": d_cim, "ssm_d": ds5v[0], "ssm_glu_w": d_gluw,
        "ssm_glu_b": ds5v[1], "ssm_norm_g": ds5v[2], "pool_w": dpw, "pool_scale": dpoolv[0], "pool_norm_g": dpoolv[1],
        "norm2_g": dg2, "final_norm_g": dgf,
    }
    like = [w[k] for k in SMALL]
    packed_g = _pack([small_g[k].reshape(w[k].shape) for k in SMALL] + [dh0[:N_META]])
    rows = packed_g.shape[0]
    packed = lambda t: _pad_rows(_pack([t[k] for k in SMALL]), rows)
    gathered = _gather_small(packed_g)
    g_pk, d_pk, m_pk, v_pk = _reduce_small(gathered, packed(w), packed(m), packed(v))
    g_small = _unpack(g_pk, like + [jax.ShapeDtypeStruct((N_META, D_MODEL), F32)])
    d_small, m_small, v_small = (_unpack(t, like) for t in (d_pk, m_pk, v_pk))
    q = place[0]
    g_meta = lax.dynamic_slice_in_dim(g_small[-1], q * (D_MODEL // N_SHARD), D_MODEL // N_SHARD, axis=1)
    d_meta, m_meta, v_meta = _adamw("adamw_meta", [w["meta_tokens"]], [g_meta], [m["meta_tokens"]],
                                    [v["meta_tokens"]], 1)

    grads, deltas, new_m, new_v = {}, {}, {}, {}
    for i, k in enumerate(SMALL):
        grads[k], deltas[k], new_m[k], new_v[k] = g_small[i], d_small[i], m_small[i], v_small[i]
    for i, k in enumerate(LARGE):
        back = (lambda t: t.T[None]) if k in ("w_gate", "w_up") else (lambda t: t[None])
        grads[k], deltas[k], new_m[k], new_v[k] = (back(t) for t in (g_large[i], d_large[i], m_large[i], v_large[i]))
    grads["meta_tokens"], deltas["meta_tokens"] = g_meta, d_meta[0]
    new_m["meta_tokens"], new_v["meta_tokens"] = m_meta[0], v_meta[0]
    return (loss, grad_x, *[grads[k] for k in WEIGHTS], *[deltas[k] for k in WEIGHTS],
            *[new_m[k] for k in WEIGHTS], *[new_v[k] for k in WEIGHTS])


def kernel(x, meta_tokens, norm1_g, w_in, ssm_lambda_re, ssm_lambda_im, ssm_log_step, ssm_b_re, ssm_b_im, ssm_c_re, ssm_c_im, ssm_d, ssm_glu_w, ssm_glu_b, ssm_norm_g, pool_w, pool_scale, pool_norm_g, w_out, norm2_g, w_gate, w_up, w_down, final_norm_g, loss_target, m_meta_tokens, m_norm1_g, m_w_in, m_ssm_lambda_re, m_ssm_lambda_im, m_ssm_log_step, m_ssm_b_re, m_ssm_b_im, m_ssm_c_re, m_ssm_c_im, m_ssm_d, m_ssm_glu_w, m_ssm_glu_b, m_ssm_norm_g, m_pool_w, m_pool_scale, m_pool_norm_g, m_w_out, m_norm2_g, m_w_gate, m_w_up, m_w_down, m_final_norm_g, v_meta_tokens, v_norm1_g, v_w_in, v_ssm_lambda_re, v_ssm_lambda_im, v_ssm_log_step, v_ssm_b_re, v_ssm_b_im, v_ssm_c_re, v_ssm_c_im, v_ssm_d, v_ssm_glu_w, v_ssm_glu_b, v_ssm_norm_g, v_pool_w, v_pool_scale, v_pool_norm_g, v_w_out, v_norm2_g, v_w_gate, v_w_up, v_w_down, v_final_norm_g):
    w = dict(meta_tokens=meta_tokens, norm1_g=norm1_g, w_in=w_in, ssm_lambda_re=ssm_lambda_re, ssm_lambda_im=ssm_lambda_im, ssm_log_step=ssm_log_step, ssm_b_re=ssm_b_re, ssm_b_im=ssm_b_im, ssm_c_re=ssm_c_re, ssm_c_im=ssm_c_im, ssm_d=ssm_d, ssm_glu_w=ssm_glu_w, ssm_glu_b=ssm_glu_b, ssm_norm_g=ssm_norm_g, pool_w=pool_w, pool_scale=pool_scale, pool_norm_g=pool_norm_g, w_out=w_out, norm2_g=norm2_g, w_gate=w_gate, w_up=w_up, w_down=w_down, final_norm_g=final_norm_g)
    m = dict(meta_tokens=m_meta_tokens, norm1_g=m_norm1_g, w_in=m_w_in, ssm_lambda_re=m_ssm_lambda_re, ssm_lambda_im=m_ssm_lambda_im, ssm_log_step=m_ssm_log_step, ssm_b_re=m_ssm_b_re, ssm_b_im=m_ssm_b_im, ssm_c_re=m_ssm_c_re, ssm_c_im=m_ssm_c_im, ssm_d=m_ssm_d, ssm_glu_w=m_ssm_glu_w, ssm_glu_b=m_ssm_glu_b, ssm_norm_g=m_ssm_norm_g, pool_w=m_pool_w, pool_scale=m_pool_scale, pool_norm_g=m_pool_norm_g, w_out=m_w_out, norm2_g=m_norm2_g, w_gate=m_w_gate, w_up=m_w_up, w_down=m_w_down, final_norm_g=m_final_norm_g)
    v = dict(meta_tokens=v_meta_tokens, norm1_g=v_norm1_g, w_in=v_w_in, ssm_lambda_re=v_ssm_lambda_re, ssm_lambda_im=v_ssm_lambda_im, ssm_log_step=v_ssm_log_step, ssm_b_re=v_ssm_b_re, ssm_b_im=v_ssm_b_im, ssm_c_re=v_ssm_c_re, ssm_c_im=v_ssm_c_im, ssm_d=v_ssm_d, ssm_glu_w=v_ssm_glu_w, ssm_glu_b=v_ssm_glu_b, ssm_norm_g=v_ssm_norm_g, pool_w=v_pool_w, pool_scale=v_pool_scale, pool_norm_g=v_pool_norm_g, w_out=v_w_out, norm2_g=v_norm2_g, w_gate=v_w_gate, w_up=v_w_up, w_down=v_w_down, final_norm_g=v_final_norm_g)
    return _step(x, loss_target, w, m, v)
```

```python
import functools
import math

import jax
import jax.numpy as jnp
from jax import lax
from jax.experimental import pallas as pl
from jax.experimental.pallas import tpu as pltpu

F32 = jnp.float32
BF16 = jnp.bfloat16
MESH = pl.DeviceIdType.MESH
AXES = ("x", "y", "c")

D_MODEL = 1024
D_SSM = 512
D_POOL = 512
N_META = 16
SSM_GROUP = 16
SSM_GROUPS = 32
SSM_STATE = 64
N_STATE = SSM_GROUPS * SSM_STATE
STATE_BLOCKS = N_STATE // 128
SUPER = 4
POOL_WINDOWS = (2, 4, 8, 16)
POOL_HALO = 16
D_FF = 2816
N_SHARD = 4
FF_SHARD = D_FF // N_SHARD
EPS = 1e-6
ADAM_LR, ADAM_B1, ADAM_B2, ADAM_EPS, ADAM_WD, ADAM_STEP = 0.001, 0.9, 0.999, 1e-08, 0.01, 10
VMEM_LIMIT = 56 * 1024 * 1024


def _plan(n_rows):
    if n_rows > 2048:
        tm, tc = 416, 320
    else:
        tm, tc = 128, 64
    step = math.lcm(tm, tc)
    return -(-n_rows // step) * step, tm, tc


def _params(sem=None):
    return pltpu.CompilerParams(dimension_semantics=sem, vmem_limit_bytes=VMEM_LIMIT)


def _dot(a, b):
    return jnp.dot(a, b, preferred_element_type=F32)


def _dot_nt(a, b):
    return lax.dot_general(a, b, (((1,), (1,)), ((), ())), preferred_element_type=F32)


def _dot_tn(a, b):
    return lax.dot_general(a, b, (((0,), (0,)), ((), ())), preferred_element_type=F32)


def _sigmoid(x):
    return 1.0 / (1.0 + jnp.exp(-x))


_GELU_C = math.sqrt(2.0 / math.pi)


def _gelu_and_grad(y):
    y2 = y * y
    t = jnp.tanh(_GELU_C * (y + 0.044715 * y * y2))
    g = 0.5 * y * (1.0 + t)
    dg = 0.5 * (1.0 + t) + 0.5 * y * (1.0 - t * t) * (_GELU_C * (1.0 + 3.0 * 0.044715 * y2))
    return g, dg


def _rms(x):
    return lax.rsqrt(jnp.mean(x * x, axis=-1, keepdims=True) + EPS)


def _rms_bwd(dn, xhat, r):
    return r * (dn - xhat * jnp.mean(dn * xhat, axis=-1, keepdims=True))


def _full(shape):
    nd = len(shape)
    return pl.BlockSpec(shape, lambda *_: (0,) * nd)


def _fwd_in(h0, g1, w_in_b, tm, token):
    n_pad = h0.shape[0]

    def body(h_ref, g_ref, w_ref, token_ref, u_ref, v_ref):
        h = h_ref[...]
        n1 = (h * _rms(h) * g_ref[...]).astype(BF16)
        proj = _dot(n1, w_ref[...])
        u_ref[...] = proj[:, :D_SSM]
        v_ref[...] = proj[:, D_SSM:]

    row = lambda w: pl.BlockSpec((tm, w), lambda i: (i, 0))
    return pl.pallas_call(
        body, grid=(n_pad // tm,), name="fwd_in",
        in_specs=[row(D_MODEL), _full((1, D_MODEL)), _full((D_MODEL, D_MODEL)), _ANY],
        out_specs=[row(D_SSM), row(D_POOL)],
        out_shape=[jax.ShapeDtypeStruct((n_pad, D_SSM), F32), jax.ShapeDtypeStruct((n_pad, D_POOL), F32)],
        compiler_params=_params(("parallel",)),
    )(h0, g1, w_in_b, token)


def _fwd_ffn(h0, ms, mp, w_out_b, g2, wg_b, wu_b, wd_b, gf, target, tm, n_valid):
    n_pad = h0.shape[0]
    nt = n_pad // tm

    def body(h0_ref, ms_ref, mp_ref, wo_ref, g2_ref, wg_ref, wu_ref, wd_ref, gf_ref, tgt_ref,
             h1_ref, n2_ref, a_ref, b_ref, ff_ref, dh2_ref, loss_ref, dgf_ref, acc):
        i, q = pl.program_id(0), pl.program_id(1)

        @pl.when((i == 0) & (q == 0))
        def _():
            loss_ref[...] = jnp.zeros_like(loss_ref)
            dgf_ref[...] = jnp.zeros_like(dgf_ref)

        @pl.when(q == 0)
        def _():
            h1 = h0_ref[...] + _dot(ms_ref[...], wo_ref[:D_SSM, :]) + _dot(mp_ref[...], wo_ref[D_SSM:, :])
            h1_ref[...] = h1
            acc[...] = h1
            n2_ref[...] = (h1 * _rms(h1) * g2_ref[...]).astype(BF16)

        n2 = n2_ref[...]
        a = _dot_nt(n2, wg_ref[0])
        b = _dot_nt(n2, wu_ref[0])
        a_ref[0] = a
        b_ref[0] = b
        ff = (a * _sigmoid(a) * b).astype(BF16)
        ff_ref[0] = ff
        acc[...] += _dot(ff, wd_ref[0])

        @pl.when(q == N_SHARD - 1)
        def _():
            h2 = acc[...]
            r = _rms(h2)
            xhat = h2 * r
            gf_row = gf_ref[...]
            rows = i * tm + lax.broadcasted_iota(jnp.int32, (tm, 1), 0)
            valid = (rows >= N_META) & (rows < n_valid)
            diff = jnp.where(valid, xhat * gf_row - tgt_ref[...], 0.0)
            loss_ref[...] += jnp.full(loss_ref.shape, 0.5 / D_MODEL, F32) * jnp.sum(diff * diff)
            dout = diff * (1.0 / D_MODEL)
            dgf_ref[...] += jnp.sum(dout * xhat, axis=0, keepdims=True)
            dh2_ref[...] = _rms_bwd(dout * gf_row, xhat, r)

    row = lambda w: pl.BlockSpec((tm, w), lambda i, q: (i, 0))
    shard_rows = pl.BlockSpec((1, FF_SHARD, D_MODEL), lambda i, q: (q, 0, 0))
    act = pl.BlockSpec((1, tm, FF_SHARD), lambda i, q: (q, i, 0))
    sds = jax.ShapeDtypeStruct
    return pl.pallas_call(
        body, grid=(nt, N_SHARD), name="fwd_ffn",
        in_specs=[row(D_MODEL), row(D_SSM), row(D_POOL), _full((D_MODEL, D_MODEL)), _full((1, D_MODEL)),
                  shard_rows, shard_rows, shard_rows, _full((1, D_MODEL)), row(D_MODEL)],
        out_specs=[row(D_MODEL), row(D_MODEL), act, act, act, row(D_MODEL), _full((8, 128)), _full((1, D_MODEL))],
        out_shape=[sds((n_pad, D_MODEL), F32), sds((n_pad, D_MODEL), BF16),
                   sds((N_SHARD, n_pad, FF_SHARD), F32), sds((N_SHARD, n_pad, FF_SHARD), F32),
                   sds((N_SHARD, n_pad, FF_SHARD), BF16), sds((n_pad, D_MODEL), F32),
                   sds((8, 128), F32), sds((1, D_MODEL), F32)],
        scratch_shapes=[pltpu.VMEM((tm, D_MODEL), F32)],
        compiler_params=_params(("arbitrary", "arbitrary")),
    )(h0, ms, mp, w_out_b, g2, wg_b, wu_b, wd_b, gf, target)


def _bwd_ffn(dh2, a, b, wg_b, wu_b, wd_b, h1, g2, tm):
    n_pad = dh2.shape[0]

    def body(dh2_ref, a_ref, b_ref, wg_ref, wu_ref, wd_ref, h1_ref, g2_ref, da_ref, db_ref, dh1_ref, dg2_ref, acc):
        i, q = pl.program_id(0), pl.program_id(1)

        @pl.when((i == 0) & (q == 0))
        def _():
            dg2_ref[...] = jnp.zeros_like(dg2_ref)

        dff = _dot_nt(dh2_ref[...].astype(BF16), wd_ref[0])
        a_v, b_v = a_ref[0], b_ref[0]
        sig = _sigmoid(a_v)
        da = (dff * b_v * sig * (1.0 + a_v * (1.0 - sig))).astype(BF16)
        db = (dff * a_v * sig).astype(BF16)
        da_ref[0] = da
        db_ref[0] = db
        part = _dot(da, wg_ref[0]) + _dot(db, wu_ref[0])

        @pl.when(q == 0)
        def _():
            acc[...] = part

        @pl.when(q > 0)
        def _():
            acc[...] += part

        @pl.when(q == N_SHARD - 1)
        def _():
            h1 = h1_ref[...]
            r = _rms(h1)
            xhat = h1 * r
            dn2 = acc[...]
            dg2_ref[...] += jnp.sum(dn2 * xhat, axis=0, keepdims=True)
            dh1_ref[...] = dh2_ref[...] + _rms_bwd(dn2 * g2_ref[...], xhat, r)

    row = lambda w: pl.BlockSpec((tm, w), lambda i, q: (i, 0))
    shard_rows = pl.BlockSpec((1, FF_SHARD, D_MODEL), lambda i, q: (q, 0, 0))
    act = pl.BlockSpec((1, tm, FF_SHARD), lambda i, q: (q, i, 0))
    sds = jax.ShapeDtypeStruct
    return pl.pallas_call(
        body, grid=(n_pad // tm, N_SHARD), name="bwd_ffn",
        in_specs=[row(D_MODEL), act, act, shard_rows, shard_rows, shard_rows, row(D_MODEL), _full((1, D_MODEL))],
        out_specs=[act, act, row(D_MODEL), _full((1, D_MODEL))],
        out_shape=[sds((N_SHARD, n_pad, FF_SHARD), BF16), sds((N_SHARD, n_pad, FF_SHARD), BF16),
                   sds((n_pad, D_MODEL), F32), sds((1, D_MODEL), F32)],
        scratch_shapes=[pltpu.VMEM((tm, D_MODEL), F32)],
        compiler_params=_params(("arbitrary", "arbitrary")),
    )(dh2, a, b, wg_b, wu_b, wd_b, h1, g2)


def _grad_ffn(n2, da, db, ff, dh2, tm):
    n_pad = n2.shape[0]

    def body(n2_ref, da_ref, db_ref, ff_ref, dh2_ref, dwg_ref, dwu_ref, dwd_ref):
        i = pl.program_id(1)
        n2_v = n2_ref[...]
        gg = _dot_tn(da_ref[0], n2_v)
        gu = _dot_tn(db_ref[0], n2_v)
        gd = _dot_tn(ff_ref[0], dh2_ref[...].astype(BF16))

        @pl.when(i == 0)
        def _():
            dwg_ref[0] = gg
            dwu_ref[0] = gu
            dwd_ref[0] = gd

        @pl.when(i > 0)
        def _():
            dwg_ref[0] += gg
            dwu_ref[0] += gu
            dwd_ref[0] += gd

    row = lambda w: pl.BlockSpec((tm, w), lambda q, i: (i, 0))
    act = pl.BlockSpec((1, tm, FF_SHARD), lambda q, i: (q, i, 0))
    sds = jax.ShapeDtypeStruct
    return pl.pallas_call(
        body, grid=(N_SHARD, n_pad // tm), name="grad_ffn",
        in_specs=[row(D_MODEL), act, act, act, row(D_MODEL)],
        out_specs=[pl.BlockSpec((1, FF_SHARD, D_MODEL), lambda q, i: (q, 0, 0))] * 3,
        out_shape=[sds((N_SHARD, FF_SHARD, D_MODEL), F32)] * 3,
        compiler_params=_params(("parallel", "arbitrary")),
    )(n2, da, db, ff, dh2)


def _bwd_out(dh1, ms, mp, w_out_b, tm, token):
    n_pad = dh1.shape[0]

    def body(dh1_ref, ms_ref, mp_ref, wo_ref, token_ref, dms_ref, dmp_ref, dwo_ref):
        i = pl.program_id(0)

        @pl.when(i == 0)
        def _():
            dwo_ref[...] = jnp.zeros_like(dwo_ref)

        d = dh1_ref[...].astype(BF16)
        dms_ref[...] = _dot_nt(d, wo_ref[:D_SSM, :])
        dmp_ref[...] = _dot_nt(d, wo_ref[D_SSM:, :])
        dwo_ref[:D_SSM, :] += _dot_tn(ms_ref[...], d)
        dwo_ref[D_SSM:, :] += _dot_tn(mp_ref[...], d)

    row = lambda w: pl.BlockSpec((tm, w), lambda i: (i, 0))
    sds = jax.ShapeDtypeStruct
    return pl.pallas_call(
        body, grid=(n_pad // tm,), name="bwd_out",
        in_specs=[row(D_MODEL), row(D_SSM), row(D_POOL), _full((D_MODEL, D_MODEL)), _ANY],
        out_specs=[row(D_SSM), row(D_POOL), _full((D_MODEL, D_MODEL))],
        out_shape=[sds((n_pad, D_SSM), F32), sds((n_pad, D_POOL), F32), sds((D_MODEL, D_MODEL), F32)],
        compiler_params=_params(("arbitrary",)),
    )(dh1, ms, mp, w_out_b, token)


def _bwd_in(du, dv, h0, dh1, g1, w_in_b, tm):
    n_pad = h0.shape[0]

    def body(du_ref, dv_ref, h0_ref, dh1_ref, g1_ref, w_ref, dh0_ref, dwi_ref, dg1_ref):
        i = pl.program_id(0)

        @pl.when(i == 0)
        def _():
            dwi_ref[...] = jnp.zeros_like(dwi_ref)
            dg1_ref[...] = jnp.zeros_like(dg1_ref)

        dub = du_ref[...].astype(BF16)
        dvb = dv_ref[...].astype(BF16)
        dn1 = _dot_nt(dub, w_ref[:, :D_SSM]) + _dot_nt(dvb, w_ref[:, D_SSM:])
        h = h0_ref[...]
        r = _rms(h)
        xhat = h * r
        g_row = g1_ref[...]
        n1 = (xhat * g_row).astype(BF16)
        dwi_ref[:, :D_SSM] += _dot_tn(n1, dub)
        dwi_ref[:, D_SSM:] += _dot_tn(n1, dvb)
        dg1_ref[...] += jnp.sum(dn1 * xhat, axis=0, keepdims=True)
        dh0_ref[...] = dh1_ref[...] + _rms_bwd(dn1 * g_row, xhat, r)

    row = lambda w: pl.BlockSpec((tm, w), lambda i: (i, 0))
    sds = jax.ShapeDtypeStruct
    return pl.pallas_call(
        body, grid=(n_pad // tm,), name="bwd_in",
        in_specs=[row(D_SSM), row(D_POOL), row(D_MODEL), row(D_MODEL), _full((1, D_MODEL)), _full((D_MODEL, D_MODEL))],
        out_specs=[row(D_MODEL), _full((D_MODEL, D_MODEL)), _full((1, D_MODEL))],
        out_shape=[sds((n_pad, D_MODEL), F32), sds((D_MODEL, D_MODEL), F32), sds((1, D_MODEL), F32)],
        compiler_params=_params(("arbitrary",)),
    )(du, dv, h0, dh1, g1, w_in_b)


def _planes_store(ref, j, val, tc):
    for i in range(4):
        ref[pl.ds(4 * j + i, tc, stride=STATE_BLOCKS), :] = val[:, 128 * i:128 * (i + 1)]


def _planes_load(ref, j, tc):
    return jnp.concatenate([ref[pl.ds(4 * j + i, tc, stride=STATE_BLOCKS), :] for i in range(4)], axis=1)


def _s5_tail(y, glu_ref, glub):
    g, dgelu = _gelu_and_grad(y)
    gb = g.astype(BF16)
    gate = jnp.concatenate([_dot(gb[:, 128 * j:128 * (j + 1)], glu_ref[j]) for j in range(SUPER)], axis=1) + glub
    sig = _sigmoid(gate)
    return g, gb, dgelu, sig, g * sig


def _s5_fwd(u, lam, bbr, bbi, crt, cit, vecs, glu, tc):
    n_pad = u.shape[0]

    def body(u_ref, lam_ref, bbr_ref, bbi_ref, crt_ref, cit_ref, vec_ref, glu_ref,
             sr_ref, si_ref, y_ref, ms_ref, hr_s, hi_s):
        @pl.when(pl.program_id(0) == 0)
        def _():
            hr_s[...] = jnp.zeros_like(hr_s)
            hi_s[...] = jnp.zeros_like(hi_s)

        u_v = u_ref[...]
        ub = u_v.astype(BF16)
        for j in range(SUPER):
            uj = ub[:, 128 * j:128 * (j + 1)]
            _planes_store(sr_ref, j, _dot(uj, bbr_ref[j]), tc)
            _planes_store(si_ref, j, _dot(uj, bbi_ref[j]), tc)

        ar = lam_ref[0:STATE_BLOCKS, :]
        ai = lam_ref[STATE_BLOCKS:, :]

        def step(t, carry):
            hr, hi = carry
            o = pl.multiple_of(t * STATE_BLOCKS, STATE_BLOCKS)
            nr = ar * hr - ai * hi + sr_ref[pl.ds(o, STATE_BLOCKS), :]
            ni = ar * hi + ai * hr + si_ref[pl.ds(o, STATE_BLOCKS), :]
            sr_ref[pl.ds(o, STATE_BLOCKS), :] = nr
            si_ref[pl.ds(o, STATE_BLOCKS), :] = ni
            return nr, ni

        hr, hi = lax.fori_loop(0, tc, step, (hr_s[...], hi_s[...]), unroll=4)
        hr_s[...] = hr
        hi_s[...] = hi

        d_row, glub, gs = vec_ref[0:1, :], vec_ref[1:2, :], vec_ref[2:3, :]
        ys_c = []
        for j in range(SUPER):
            sr_j = _planes_load(sr_ref, j, tc).astype(BF16)
            si_j = _planes_load(si_ref, j, tc).astype(BF16)
            ys_c.append(_dot(sr_j, crt_ref[j]) - _dot(si_j, cit_ref[j]))
        y = jnp.concatenate(ys_c, axis=1) + d_row * u_v
        y_ref[...] = y
        _, _, _, _, ys = _s5_tail(y, glu_ref, glub)
        ms_ref[...] = (ys * _rms(ys) * gs).astype(BF16)

    chunk = lambda w: pl.BlockSpec((tc, w), lambda c: (c, 0))
    states = pl.BlockSpec((tc * STATE_BLOCKS, 128), lambda c: (c, 0))
    sds = jax.ShapeDtypeStruct
    return pl.pallas_call(
        body, grid=(n_pad // tc,), name="s5_fwd",
        in_specs=[chunk(D_SSM), _full((2 * STATE_BLOCKS, 128)), _full((SUPER, 128, 512)), _full((SUPER, 128, 512)),
                  _full((SUPER, 512, 128)), _full((SUPER, 512, 128)), _full((8, D_SSM)), _full((SUPER, 128, 128))],
        out_specs=[states, states, chunk(D_SSM), chunk(D_SSM)],
        out_shape=[sds((n_pad * STATE_BLOCKS, 128), F32), sds((n_pad * STATE_BLOCKS, 128), F32),
                   sds((n_pad, D_SSM), F32), sds((n_pad, D_SSM), BF16)],
        scratch_shapes=[pltpu.VMEM((STATE_BLOCKS, 128), F32), pltpu.VMEM((STATE_BLOCKS, 128), F32)],
        compiler_params=_params(("arbitrary",)),
    )(u, lam, bbr, bbi, crt, cit, vecs, glu)


def _s5_bwd(dms, y, u, sr, si, lam, bbr, bbi, crt, cit, vecs, glu, tc, token):
    n_pad = u.shape[0]
    nc = n_pad // tc

    def body(dms_ref, y_ref, u_ref, sr_ref, si_ref, pr_ref, pi_ref, lam_ref, bbr_ref, bbi_ref, crt_ref, cit_ref,
             vec_ref, glu_ref, token_ref, du_ref, dbbr_ref, dbbi_ref, dcrt_ref, dcit_ref, dglu_ref, dvec_ref, dlam_ref,
             qr_s, qi_s, cr_s, ci_s):
        c = pl.program_id(0)

        @pl.when(c == 0)
        def _():
            for ref in (dbbr_ref, dbbi_ref, dcrt_ref, dcit_ref, dglu_ref, dvec_ref, dlam_ref, cr_s, ci_s):
                ref[...] = jnp.zeros_like(ref)

        d_row, glub, gs = vec_ref[0:1, :], vec_ref[1:2, :], vec_ref[2:3, :]
        y_v, u_v = y_ref[...], u_ref[...]
        ub = u_v.astype(BF16)
        g, gb, dgelu, sig, ys = _s5_tail(y_v, glu_ref, glub)
        r = _rms(ys)
        xhat = ys * r
        dm = dms_ref[...]
        dys = _rms_bwd(dm * gs, xhat, r)
        dgate = dys * g * sig * (1.0 - sig)
        dgateb = dgate.astype(BF16)
        dg = dys * sig + jnp.concatenate(
            [_dot_nt(dgateb[:, 128 * j:128 * (j + 1)], glu_ref[j]) for j in range(SUPER)], axis=1)
        dy = dg * dgelu
        dyb = dy.astype(BF16)
        dvec_ref[0:1, :] += jnp.sum(dy * u_v, axis=0, keepdims=True)
        dvec_ref[1:2, :] += jnp.sum(dgate, axis=0, keepdims=True)
        dvec_ref[2:3, :] += jnp.sum(dm * xhat, axis=0, keepdims=True)

        for j in range(SUPER):
            cols = slice(128 * j, 128 * (j + 1))
            dglu_ref[j] += _dot_tn(gb[:, cols], dgateb[:, cols])
            dcrt_ref[j] += _dot_tn(_planes_load(sr_ref, j, tc).astype(BF16), dyb[:, cols])
            dcit_ref[j] -= _dot_tn(_planes_load(si_ref, j, tc).astype(BF16), dyb[:, cols])
            _planes_store(qr_s, j, _dot_nt(dyb[:, cols], crt_ref[j]), tc)
            _planes_store(qi_s, j, -_dot_nt(dyb[:, cols], cit_ref[j]), tc)

        ar = lam_ref[0:STATE_BLOCKS, :]
        ai = lam_ref[STATE_BLOCKS:, :]

        def step(n, carry):
            qr, qi = carry
            o = pl.multiple_of((tc - 1 - n) * STATE_BLOCKS, STATE_BLOCKS)
            nr = ar * qr + ai * qi + qr_s[pl.ds(o, STATE_BLOCKS), :]
            ni = ar * qi - ai * qr + qi_s[pl.ds(o, STATE_BLOCKS), :]
            qr_s[pl.ds(o, STATE_BLOCKS), :] = nr
            qi_s[pl.ds(o, STATE_BLOCKS), :] = ni
            return nr, ni

        qr, qi = lax.fori_loop(0, tc, step, (cr_s[...], ci_s[...]), unroll=4)
        cr_s[...] = qr
        ci_s[...] = qi

        first = c == nc - 1
        row0 = lax.broadcasted_iota(jnp.int32, (tc, 1), 0) == 0
        for k in range(STATE_BLOCKS):
            plane = pl.ds(k, tc, stride=STATE_BLOCKS)
            pr = jnp.where(first, 0.0, pr_ref[k:k + 1, :])
            pi = jnp.where(first, 0.0, pi_ref[k:k + 1, :])
            hpr = jnp.where(row0, pr, pltpu.roll(sr_ref[plane, :], 1, 0))
            hpi = jnp.where(row0, pi, pltpu.roll(si_ref[plane, :], 1, 0))
            q_r, q_i = qr_s[plane, :], qi_s[plane, :]
            dlam_ref[k:k + 1, :] += jnp.sum(q_r * hpr + q_i * hpi, axis=0, keepdims=True)
            dlam_ref[STATE_BLOCKS + k:STATE_BLOCKS + k + 1, :] += jnp.sum(q_i * hpr - q_r * hpi, axis=0, keepdims=True)

        du_c = []
        for j in range(SUPER):
            cols = slice(128 * j, 128 * (j + 1))
            qr_j = _planes_load(qr_s, j, tc).astype(BF16)
            qi_j = _planes_load(qi_s, j, tc).astype(BF16)
            du_c.append(_dot_nt(qr_j, bbr_ref[j]) + _dot_nt(qi_j, bbi_ref[j]))
            dbbr_ref[j] += _dot_tn(ub[:, cols], qr_j)
            dbbi_ref[j] += _dot_tn(ub[:, cols], qi_j)
        du_ref[...] = jnp.concatenate(du_c, axis=1) + dy * d_row

    rev = lambda c: nc - 1 - c
    chunk = lambda w: pl.BlockSpec((tc, w), lambda c: (rev(c), 0))
    states = pl.BlockSpec((tc * STATE_BLOCKS, 128), lambda c: (rev(c), 0))
    prev = pl.BlockSpec((STATE_BLOCKS, 128), lambda c: (jnp.maximum(rev(c) * tc - 1, 0), 0))
    sds = jax.ShapeDtypeStruct
    return pl.pallas_call(
        body, grid=(nc,), name="s5_bwd",
        in_specs=[chunk(D_SSM), chunk(D_SSM), chunk(D_SSM), states, states, prev, prev,
                  _full((2 * STATE_BLOCKS, 128)), _full((SUPER, 128, 512)), _full((SUPER, 128, 512)),
                  _full((SUPER, 512, 128)), _full((SUPER, 512, 128)), _full((8, D_SSM)), _full((SUPER, 128, 128)), _ANY],
        out_specs=[chunk(D_SSM), _full((SUPER, 128, 512)), _full((SUPER, 128, 512)), _full((SUPER, 512, 128)),
                   _full((SUPER, 512, 128)), _full((SUPER, 128, 128)), _full((8, D_SSM)), _full((2 * STATE_BLOCKS, 128))],
        out_shape=[sds((n_pad, D_SSM), F32), sds((SUPER, 128, 512), F32), sds((SUPER, 128, 512), F32),
                   sds((SUPER, 512, 128), F32), sds((SUPER, 512, 128), F32), sds((SUPER, 128, 128), F32),
                   sds((8, D_SSM), F32), sds((2 * STATE_BLOCKS, 128), F32)],
        scratch_shapes=[pltpu.VMEM((tc * STATE_BLOCKS, 128), F32), pltpu.VMEM((tc * STATE_BLOCKS, 128), F32),
                        pltpu.VMEM((STATE_BLOCKS, 128), F32), pltpu.VMEM((STATE_BLOCKS, 128), F32)],
        compiler_params=_params(("arbitrary",)),
    )(dms, y, u, sr, si, sr, si, lam, bbr, bbi, crt, cit, vecs, glu, token)


def _inv_count(c_idx, tc, w):
    t = c_idx * tc + lax.broadcasted_iota(jnp.int32, (tc, 1), 0)
    return 1.0 / jnp.minimum(t + 1, w).astype(F32)


def _pool_fwd(v, pw_b, vecs, tc):
    n_pad = v.shape[0]

    def body(v_ref, pw_ref, vec_ref, feat_ref, mp_ref, hist):
        c = pl.program_id(0)

        @pl.when(c == 0)
        def _():
            hist[...] = jnp.zeros_like(hist)

        v_v = v_ref[...]
        ext = jnp.concatenate([hist[...], v_v], axis=0)
        hist[...] = v_v[tc - POOL_HALO:, :]
        feats, ps = [], []
        for k, w in enumerate(POOL_WINDOWS):
            cols = slice(128 * k, 128 * (k + 1))
            s = ext[:, cols]
            sh = 1
            while sh < w:
                s = s + pltpu.roll(s, sh, 0)
                sh *= 2
            f = (s[POOL_HALO:, :] * _inv_count(c, tc, w) - v_v[:, cols]).astype(BF16)
            feats.append(f)
            ps.append(_dot(f, pw_ref[k]))
        feat_ref[...] = jnp.concatenate(feats, axis=1)
        yp = jnp.concatenate(ps, axis=1) * vec_ref[0:1, :]
        mp_ref[...] = (yp * _rms(yp) * vec_ref[1:2, :]).astype(BF16)

    chunk = lambda w: pl.BlockSpec((tc, w), lambda c: (c, 0))
    sds = jax.ShapeDtypeStruct
    return pl.pallas_call(
        body, grid=(n_pad // tc,), name="pool_fwd",
        in_specs=[chunk(D_POOL), _full((4, 128, 128)), _full((8, D_POOL))],
        out_specs=[chunk(D_POOL), chunk(D_POOL)],
        out_shape=[sds((n_pad, D_POOL), BF16), sds((n_pad, D_POOL), BF16)],
        scratch_shapes=[pltpu.VMEM((POOL_HALO, D_POOL), F32)],
        compiler_params=_params(("arbitrary",)),
    )(v, pw_b, vecs)


def _pool_bwd(dmp, feat, pw_b, vecs, tc):
    n_pad = dmp.shape[0]
    nc = n_pad // tc

    def body(dmp_ref, feat_ref, pw_ref, vec_ref, dv_ref, dpw_ref, dvec_ref, fut):
        c = pl.program_id(0)

        @pl.when(c == 0)
        def _():
            fut[...] = jnp.zeros_like(fut)
            dpw_ref[...] = jnp.zeros_like(dpw_ref)
            dvec_ref[...] = jnp.zeros_like(dvec_ref)

        scale, gp = vec_ref[0:1, :], vec_ref[1:2, :]
        feat_v = feat_ref[...]
        p = jnp.concatenate([_dot(feat_v[:, 128 * k:128 * (k + 1)], pw_ref[k]) for k in range(4)], axis=1)
        yp = p * scale
        r = _rms(yp)
        xhat = yp * r
        dm = dmp_ref[...]
        dyp = _rms_bwd(dm * gp, xhat, r)
        dvec_ref[0:1, :] += jnp.sum(dyp * p, axis=0, keepdims=True)
        dvec_ref[1:2, :] += jnp.sum(dm * xhat, axis=0, keepdims=True)
        dpb = (dyp * scale).astype(BF16)
        es, dfs = [], []
        for k, w in enumerate(POOL_WINDOWS):
            cols = slice(128 * k, 128 * (k + 1))
            dpw_ref[k] += _dot_tn(feat_v[:, cols], dpb[:, cols])
            df = _dot_nt(dpb[:, cols], pw_ref[k])
            dfs.append(df)
            es.append(df * _inv_count(nc - 1 - c, tc, w))
        e = jnp.concatenate(es, axis=1)
        ext = jnp.concatenate([e, fut[...]], axis=0)
        fut[...] = e[:POOL_HALO, :]
        n_ext = tc + POOL_HALO
        dvs = []
        for k, w in enumerate(POOL_WINDOWS):
            s = ext[:, 128 * k:128 * (k + 1)]
            sh = 1
            while sh < w:
                s = s + pltpu.roll(s, n_ext - sh, 0)
                sh *= 2
            dvs.append(s[:tc, :] - dfs[k])
        dv_ref[...] = jnp.concatenate(dvs, axis=1)

    chunk = lambda w: pl.BlockSpec((tc, w), lambda c: (nc - 1 - c, 0))
    sds = jax.ShapeDtypeStruct
    return pl.pallas_call(
        body, grid=(nc,), name="pool_bwd",
        in_specs=[chunk(D_POOL), chunk(D_POOL), _full((4, 128, 128)), _full((8, D_POOL))],
        out_specs=[chunk(D_POOL), _full((4, 128, 128)), _full((8, D_POOL))],
        out_shape=[sds((n_pad, D_POOL), F32), sds((4, 128, 128), F32), sds((8, D_POOL), F32)],
        scratch_shapes=[pltpu.VMEM((POOL_HALO, D_POOL), F32)],
        compiler_params=_params(("arbitrary",)),
    )(dmp, feat, pw_b, vecs)


def _place():
    x, y, c = lax.axis_index("x"), lax.axis_index("y"), lax.axis_index("c")
    chips = [(1 - x, y), (x, 1 - y), (1 - x, 1 - y)]
    return x, y, c, chips


_ANY = pl.BlockSpec(memory_space=pl.ANY)


def _cast_shards(shards, dtypes, place):
    n = len(shards)

    def body(place_ref, *refs):
        for i in range(n):
            refs[n + i][0] = refs[i][...].astype(dtypes[i])

    return pl.pallas_call(
        body, name="cast_shards",
        grid_spec=pltpu.PrefetchScalarGridSpec(
            num_scalar_prefetch=1, grid=(1,),
            in_specs=[pl.BlockSpec(s.shape, lambda i, p: (0, 0, 0)) for s in shards],
            out_specs=[pl.BlockSpec((1,) + s.shape, lambda i, p: (p[0], 0, 0, 0)) for s in shards]),
        out_shape=[jax.ShapeDtypeStruct((N_SHARD,) + s.shape, dt) for s, dt in zip(shards, dtypes)],
        compiler_params=_params(("arbitrary",)),
    )(place, *shards)


def _gather_shards(full):
    n = len(full)

    def body(*refs):
        outs = refs[n:2 * n]
        ici_send, ici_recv, d2d_send, d2d_recv = refs[2 * n:]
        x, y, c, chips = _place()
        q = 2 * x + y
        sibling = (x, y, 1 - c)

        def ici(i, j, shard, to):
            return pltpu.make_async_remote_copy(src_ref=outs[i].at[q, c], dst_ref=outs[i].at[shard, c],
                                                send_sem=ici_send.at[i, j], recv_sem=ici_recv.at[i, j],
                                                device_id=to, device_id_type=MESH)

        def d2d(i, j, shard, half):
            return pltpu.make_async_remote_copy(src_ref=outs[i].at[shard, c], dst_ref=outs[i].at[shard, half],
                                                send_sem=d2d_send.at[i, j], recv_sem=d2d_recv.at[i, j],
                                                device_id=sibling, device_id_type=MESH)

        sends = [ici(i, j, q, (*chip, c)) for i in range(n) for j, chip in enumerate(chips)]
        for cp in sends:
            cp.start()
        passed = []
        for i in range(n):
            for j, (cx, cy) in enumerate(chips):
                ici(i, j, 2 * cx + cy, (cx, cy, c)).wait_recv()
                cp = d2d(i, j, 2 * cx + cy, c)
                cp.start()
                passed.append(cp)
        for i in range(n):
            for j, (cx, cy) in enumerate(chips):
                d2d(i, j, 2 * cx + cy, 1 - c).wait_recv()
        for cp in sends + passed:
            cp.wait_send()

    return pl.pallas_call(
        body, name="gather_shards",
        in_specs=[_ANY] * n, out_specs=[_ANY] * n,
        out_shape=[jax.ShapeDtypeStruct(f.shape, f.dtype) for f in full],
        input_output_aliases={i: i for i in range(n)},
        scratch_shapes=[pltpu.SemaphoreType.DMA((n, 3)), pltpu.SemaphoreType.DMA((n, 3)),
                        pltpu.SemaphoreType.DMA((n, 3)), pltpu.SemaphoreType.DMA((n, 3))],
    )(*full)


def _forward_halves(full):
    n = len(full)

    def body(*refs):
        outs = refs[n:2 * n]
        send, recv = refs[2 * n:]
        x, y, c, chips = _place()

        def d2d(i, j, shard, half):
            return pltpu.make_async_remote_copy(src_ref=outs[i].at[shard, c], dst_ref=outs[i].at[shard, half],
                                                send_sem=send.at[i, j], recv_sem=recv.at[i, j],
                                                device_id=(x, y, 1 - c), device_id_type=MESH)

        cps = [d2d(i, j, 2 * cx + cy, c) for i in range(n) for j, (cx, cy) in enumerate(chips)]
        for cp in cps:
            cp.start()
        for i in range(n):
            for j, (cx, cy) in enumerate(chips):
                d2d(i, j, 2 * cx + cy, 1 - c).wait_recv()
        for cp in cps:
            cp.wait_send()

    return pl.pallas_call(
        body, name="forward_halves",
        in_specs=[_ANY] * n, out_specs=[_ANY] * n,
        out_shape=[jax.ShapeDtypeStruct(f.shape, f.dtype) for f in full],
        input_output_aliases={i: i for i in range(n)},
        scratch_shapes=[pltpu.SemaphoreType.DMA((n, 3)), pltpu.SemaphoreType.DMA((n, 3))],
    )(*full)


_HBM = pl.BlockSpec(memory_space=pltpu.HBM)
_SEM = pl.BlockSpec(memory_space=pltpu.SEMAPHORE)
_EFFECT = pltpu.SideEffectType.DATAFLOW_SIDE_EFFECTING


def _copies_start(name, arrays, sem_shape, build):
    n = len(arrays)

    def body(*refs):
        outs = refs[n:2 * n]
        send, recv, token = refs[2 * n:]
        sends, _ = build(outs, send, recv)
        for cp in sends:
            cp.start()
        token[...] = jnp.zeros_like(token)

    out = pl.pallas_call(
        body, name=name, in_specs=[_HBM] * n,
        out_specs=[_HBM] * n + [_SEM, _SEM, pl.BlockSpec(memory_space=pltpu.VMEM)],
        out_shape=[pltpu.HBM(a.shape, a.dtype) for a in arrays]
        + [pltpu.SemaphoreType.DMA(sem_shape), pltpu.SemaphoreType.DMA(sem_shape), jax.ShapeDtypeStruct((8, 128), F32)],
        input_output_aliases={i: i for i in range(n)},
        compiler_params=pltpu.CompilerParams(has_side_effects=_EFFECT),
    )(*[pltpu.with_memory_space_constraint(a, pltpu.HBM) for a in arrays])
    return list(out[:n]), (out[n], out[n + 1]), out[n + 2]


def _copies_wait(name, arrays, sems, after, build):
    n = len(arrays)

    def body(*refs):
        ins = refs[:n]
        send, recv = refs[n], refs[n + 1]
        sends, recvs = build(ins, send, recv)
        for cp in sends:
            cp.wait_send()
        for cp in recvs:
            cp.wait_recv()

    return list(pl.pallas_call(
        body, name=name, in_specs=[_HBM] * n + [_SEM, _SEM, _ANY], out_specs=[_HBM] * n,
        out_shape=[pltpu.HBM(a.shape, a.dtype) for a in arrays],
        input_output_aliases={i: i for i in range(n)},
        compiler_params=pltpu.CompilerParams(has_side_effects=_EFFECT),
    )(*arrays, *sems, after))


def _remote(src, dst, send_sem, recv_sem, to):
    return pltpu.make_async_remote_copy(src_ref=src, dst_ref=dst, send_sem=send_sem, recv_sem=recv_sem,
                                        device_id=to, device_id_type=MESH)


def _build_gather(refs, send, recv):
    x, y, c, chips = _place()
    q = 2 * x + y
    pairs = [(i, j, chip) for i in range(len(refs)) for j, chip in enumerate(chips)]
    sends = [_remote(refs[i].at[q, c], refs[i].at[q, c], send.at[3 * i + j], recv.at[3 * i + j], (cx, cy, c))
             for i, j, (cx, cy) in pairs]
    recvs = [_remote(refs[i].at[q, c], refs[i].at[2 * cx + cy, c], send.at[3 * i + j], recv.at[3 * i + j], (cx, cy, c))
             for i, j, (cx, cy) in pairs]
    return sends, recvs


def _build_swap(refs, send, recv):
    x, y, c, _ = _place()
    n = len(refs) // 2
    cps = [_remote(refs[i].at[:, 1 - c], refs[n + i], send.at[i], recv.at[i], (x, y, 1 - c)) for i in range(n)]
    return cps, cps


def _build_exchange(refs, send, recv):
    x, y, c, chips = _place()
    n = len(refs) // 2
    cps = [_remote(refs[i].at[2 * cx + cy], refs[n + i].at[j], send.at[3 * i + j], recv.at[3 * i + j], (cx, cy, c))
           for i in range(n) for j, (cx, cy) in enumerate(chips)]
    return cps, cps


def _swap_halves(grads):
    n = len(grads)

    def body(*refs):
        ins, outs = refs[:n], refs[n:2 * n]
        send, recv = refs[2 * n:]
        x, y, c, _ = _place()
        cps = [pltpu.make_async_remote_copy(src_ref=ins[i].at[:, 1 - c], dst_ref=outs[i], send_sem=send.at[i],
                                            recv_sem=recv.at[i], device_id=(x, y, 1 - c), device_id_type=MESH)
               for i in range(n)]
        for cp in cps:
            cp.start()
        for cp in cps:
            cp.wait()

    return pl.pallas_call(
        body, name="swap_halves",
        in_specs=[_ANY] * n, out_specs=[_ANY] * n,
        out_shape=[jax.ShapeDtypeStruct((N_SHARD,) + g.shape[2:], F32) for g in grads],
        scratch_shapes=[pltpu.SemaphoreType.DMA((n,)), pltpu.SemaphoreType.DMA((n,))],
    )(*grads)


def _join_halves(pairs):
    n = len(pairs)

    def body(*refs):
        outs = refs[n:2 * n]
        send, recv = refs[2 * n:]
        x, y, c, _ = _place()
        cps = [pltpu.make_async_remote_copy(src_ref=outs[i].at[c], dst_ref=outs[i].at[c], send_sem=send.at[i],
                                            recv_sem=recv.at[i], device_id=(x, y, 1 - c), device_id_type=MESH)
               for i in range(n)]
        for cp in cps:
            cp.start()
        for i in range(n):
            cps[i].wait_send()
            pltpu.make_async_remote_copy(src_ref=outs[i].at[c], dst_ref=outs[i].at[1 - c], send_sem=send.at[i],
                                         recv_sem=recv.at[i], device_id=(x, y, 1 - c), device_id_type=MESH).wait_recv()

    return pl.pallas_call(
        body, name="join_halves",
        in_specs=[_ANY] * n, out_specs=[_ANY] * n,
        out_shape=[jax.ShapeDtypeStruct(p.shape, F32) for p in pairs],
        input_output_aliases={i: i for i in range(n)},
        scratch_shapes=[pltpu.SemaphoreType.DMA((n,)), pltpu.SemaphoreType.DMA((n,))],
    )(*pairs)


def _gather_small(part, token):
    m_per, n = part.shape

    def body(x_ref, token_ref, out_ref, send_sems, recv_sems, local_sem):
        x, y, c, chips = _place()
        me, sibling = (x, y, c), (x, y, 1 - c)

        def rows(px, py, pc):
            return out_ref.at[pl.ds((4 * px + 2 * py + pc) * m_per, m_per), :]

        def copy(k, block, to, src=None):
            return pltpu.make_async_remote_copy(src_ref=rows(*block) if src is None else src, dst_ref=rows(*block),
                                                send_sem=send_sems.at[k], recv_sem=recv_sems.at[k],
                                                device_id=to, device_id_type=MESH)

        mine = pltpu.make_async_copy(x_ref, rows(*me), local_sem)
        mine.start()
        first = [copy(0, me, sibling, src=x_ref)]
        first += [copy(1 + j, me, (*chip, c), src=x_ref) for j, chip in enumerate(chips)]
        for cp in first:
            cp.start()
        passed = [copy(4 + j, (*chip, c), sibling) for j, chip in enumerate(chips)]
        for j, chip in enumerate(chips):
            copy(1 + j, (*chip, c), me).wait_recv()
            passed[j].start()
        copy(0, sibling, me).wait_recv()
        for j, chip in enumerate(chips):
            copy(4 + j, (*chip, 1 - c), me).wait_recv()
        for cp in first + passed:
            cp.wait_send()
        mine.wait()

    return pl.pallas_call(
        body, name="gather_small",
        out_shape=jax.ShapeDtypeStruct((8 * m_per, n), F32),
        in_specs=[pl.BlockSpec(memory_space=pltpu.VMEM), _ANY], out_specs=pl.BlockSpec(memory_space=pltpu.VMEM),
        scratch_shapes=[pltpu.SemaphoreType.DMA((7,)), pltpu.SemaphoreType.DMA((7,)), pltpu.SemaphoreType.DMA],
        compiler_params=_params(),
    )(part, token)


N_SPLIT = 2


def _sum_siblings(tag, grads, recvd, place):
    n = len(grads)

    def body(place_ref, *refs):
        g_refs, r_refs, sb_refs, own_refs = (refs[k * n:(k + 1) * n] for k in range(4))
        s = pl.program_id(1)
        for i in range(n):
            tot = g_refs[i][0, 0] + r_refs[i][0]
            sb_refs[i][0] = tot.astype(BF16)

            @pl.when(s == place_ref[0])
            def _():
                own_refs[i][...] = tot

    in_specs, sb_specs, own_specs, sb_shapes, own_shapes = [], [], [], [], []
    for g in grads:
        _, _, r, cdim = g.shape
        rb = r // N_SPLIT
        in_specs.append(pl.BlockSpec((1, 1, rb, cdim), lambda b, s, p: (s, p[1], b, 0)))
        sb_specs.append(pl.BlockSpec((1, rb, cdim), lambda b, s, p: (s, b, 0)))
        own_specs.append(pl.BlockSpec((rb, cdim), lambda b, s, p: (b, 0)))
        sb_shapes.append(jax.ShapeDtypeStruct((N_SHARD, r, cdim), BF16))
        own_shapes.append(jax.ShapeDtypeStruct((r, cdim), F32))
    out = pl.pallas_call(
        body, name="sum_siblings_" + tag,
        grid_spec=pltpu.PrefetchScalarGridSpec(
            num_scalar_prefetch=1, grid=(N_SPLIT, N_SHARD),
            in_specs=in_specs + sb_specs, out_specs=sb_specs + own_specs),
        out_shape=sb_shapes + own_shapes,
        compiler_params=_params(("parallel", "arbitrary")),
    )(place, *grads, *recvd)
    return out[:n], out[n:]


def _sum_chips(own, recvd, place):
    n = len(own)

    def body(place_ref, *refs):
        o_refs, r_refs, out_refs = (refs[k * n:(k + 1) * n] for k in range(3))
        for i in range(n):
            tot = o_refs[i][...]
            for j in range(3):
                tot = tot + r_refs[i][j].astype(F32)
            out_refs[i][0] = tot

    o_specs, r_specs, out_specs = [], [], []
    for o in own:
        r, cdim = o.shape
        rb = r // N_SPLIT
        o_specs.append(pl.BlockSpec((rb, cdim), lambda b, p: (b, 0)))
        r_specs.append(pl.BlockSpec((3, rb, cdim), lambda b, p: (0, b, 0)))
        out_specs.append(pl.BlockSpec((1, rb, cdim), lambda b, p: (p[1], b, 0)))
    return pl.pallas_call(
        body, name="sum_chips",
        grid_spec=pltpu.PrefetchScalarGridSpec(num_scalar_prefetch=1, grid=(N_SPLIT,),
                                               in_specs=o_specs + r_specs, out_specs=out_specs),
        out_shape=[jax.ShapeDtypeStruct((2,) + o.shape, F32) for o in own],
        compiler_params=_params(("parallel",)),
    )(place, *own, *recvd)


def _adamw_math(w, g, m, v):
    m = ADAM_B1 * m + (1.0 - ADAM_B1) * g
    v = ADAM_B2 * v + (1.0 - ADAM_B2) * (g * g)
    m_hat = m / (1.0 - ADAM_B1 ** ADAM_STEP)
    v_hat = v / (1.0 - ADAM_B2 ** ADAM_STEP)
    delta = -ADAM_LR * (m_hat / (jnp.sqrt(v_hat) + ADAM_EPS) + ADAM_WD * w)
    return delta, m, v


def _adamw(name, ws, gs, ms, vs, n_split):
    n = len(ws)

    def body(*refs):
        w_r, g_r, m_r, v_r, d_o, m_o, v_o = (refs[k * n:(k + 1) * n] for k in range(7))
        for i in range(n):
            d, m, v = _adamw_math(w_r[i][...], g_r[i][...], m_r[i][...], v_r[i][...])
            d_o[i][...] = d
            m_o[i][...] = m
            v_o[i][...] = v

    specs = [pl.BlockSpec((w.shape[0] // n_split, w.shape[1]), lambda b: (b, 0)) for w in ws]
    shapes = [jax.ShapeDtypeStruct(w.shape, F32) for w in ws]
    out = pl.pallas_call(
        body, name=name, grid=(n_split,),
        in_specs=specs * 4, out_specs=specs * 3, out_shape=shapes * 3,
        compiler_params=_params(("parallel",)),
    )(*ws, *gs, *ms, *vs)
    return out[:n], out[n:2 * n], out[2 * n:]


def _reduce_small(gathered, w, m, v):
    rows = w.shape[0]

    def body(ga_ref, w_ref, m_ref, v_ref, g_out, d_out, m_out, v_out):
        g = ga_ref[0:rows, :]
        for k in range(1, 8):
            g = g + ga_ref[k * rows:(k + 1) * rows, :]
        g_out[...] = g
        d, mm, vv = _adamw_math(w_ref[...], g, m_ref[...], v_ref[...])
        d_out[...] = d
        m_out[...] = mm
        v_out[...] = vv

    return pl.pallas_call(
        body, name="reduce_small",
        out_shape=[jax.ShapeDtypeStruct(w.shape, F32)] * 4,
        compiler_params=_params(),
    )(gathered, w, m, v)


def _s5_operands(lam_re, lam_im, log_step, b_re, b_im, c_re, c_im, glu_w):
    lr = jnp.minimum(lam_re, -1e-4)
    li = lam_im
    step = jnp.exp(log_step)[:, None]
    mag = jnp.exp(lr * step)
    ang = li * step
    abr = mag * jnp.cos(ang)
    abi = mag * jnp.sin(ang)
    nr = abr - 1.0
    ni = abi
    den = lr * lr + li * li
    cr = ((nr * lr + ni * li) / den)[..., None]
    ci = ((ni * lr - nr * li) / den)[..., None]
    bbr = cr * b_re - ci * b_im
    bbi = cr * b_im + ci * b_re
    eye = jnp.eye(8, dtype=F32)
    g, h, p = SSM_GROUPS // SUPER, SSM_GROUP, SSM_STATE

    def b_layout(t):
        return jnp.einsum("ab,japh->jahbp", eye, t.reshape(SUPER, g, p, h)).reshape(SUPER, g * h, g * p)

    def c_layout(t):
        return jnp.einsum("ab,jahp->jbpah", eye, t.reshape(SUPER, g, h, p)).reshape(SUPER, g * p, g * h)

    glu = jnp.einsum("ab,jahk->jahbk", eye, glu_w.reshape(SUPER, g, h, h)).reshape(SUPER, g * h, g * h)
    lam = jnp.concatenate([abr.reshape(STATE_BLOCKS, 128), abi.reshape(STATE_BLOCKS, 128)], axis=0)
    return lam, b_layout(bbr), b_layout(bbi), c_layout(c_re), c_layout(c_im), glu


def _pad_rows(a, rows):
    return jnp.pad(a, ((0, rows - a.shape[0]), (0, 0)))


def _pack(parts):
    rows = []
    for a in parts:
        flat = a.reshape(-1)
        n = -(-flat.shape[0] // 128)
        rows.append(jnp.pad(flat, (0, n * 128 - flat.shape[0])).reshape(n, 128))
    out = jnp.concatenate(rows, axis=0)
    return _pad_rows(out, -(-out.shape[0] // 8) * 8)


def _unpack(packed, like):
    out, at = [], 0
    for a in like:
        n = -(-a.size // 128)
        out.append(packed[at:at + n].reshape(-1)[:a.size].reshape(a.shape))
        at += n
    return out


SMALL = ("norm1_g", "ssm_lambda_re", "ssm_lambda_im", "ssm_log_step", "ssm_b_re", "ssm_b_im", "ssm_c_re", "ssm_c_im",
         "ssm_d", "ssm_glu_w", "ssm_glu_b", "ssm_norm_g", "pool_w", "pool_scale", "pool_norm_g", "norm2_g",
         "final_norm_g")
LARGE = ("w_in", "w_out", "w_gate", "w_up", "w_down")
WEIGHTS = ("meta_tokens", "norm1_g", "w_in", "ssm_lambda_re", "ssm_lambda_im", "ssm_log_step", "ssm_b_re", "ssm_b_im",
           "ssm_c_re", "ssm_c_im", "ssm_d", "ssm_glu_w", "ssm_glu_b", "ssm_norm_g", "pool_w", "pool_scale",
           "pool_norm_g", "w_out", "norm2_g", "w_gate", "w_up", "w_down", "final_norm_g")


def _step(x, target, w, m, v):
    seq = x.shape[1]
    n_rows = N_META + seq
    n_pad, tm, tc = _plan(n_rows)
    xq, yq, cq = lax.axis_index("x"), lax.axis_index("y"), lax.axis_index("c")
    place = jnp.stack([2 * xq + yq, cq]).astype(jnp.int32)

    def halves(a2d):
        return a2d.reshape(2, a2d.shape[0] // 2, a2d.shape[1])

    def local2d(t):
        return {"w_gate": lambda a: a[0].T, "w_up": lambda a: a[0].T}.get(t, lambda a: a[0])

    shards = [halves(local2d(k)(w[k])) for k in LARGE] + [halves(w["meta_tokens"])]
    full = _cast_shards(shards, [BF16] * len(LARGE) + [F32], place)
    w_in_full, meta_full = _gather_shards([full[0], full[5]])
    late, gather_sems, gather_token = _copies_start("gather_start", list(full[1:5]), (12,), _build_gather)
    w_in_b = w_in_full.reshape(D_MODEL, D_MODEL)
    meta = meta_full.reshape(N_SHARD, N_META, D_MODEL // N_SHARD).transpose(1, 0, 2).reshape(N_META, D_MODEL)

    h0 = _pad_rows(jnp.concatenate([meta, x[0]], axis=0), n_pad)
    tgt = _pad_rows(jnp.concatenate([jnp.zeros((N_META, D_MODEL), F32), target[0]], axis=0), n_pad)
    s5_in = (w["ssm_lambda_re"][0], w["ssm_lambda_im"][0], w["ssm_log_step"][0], w["ssm_b_re"][0], w["ssm_b_im"][0],
             w["ssm_c_re"][0], w["ssm_c_im"][0], w["ssm_glu_w"][0])
    (lam, bbr, bbi, crt, cit, glu), s5_vjp = jax.vjp(_s5_operands, *s5_in)
    bbr_b, bbi_b, crt_b, cit_b, glu_b16 = (t.astype(BF16) for t in (bbr, bbi, crt, cit, glu))
    s5_vecs = _pad_rows(jnp.concatenate([w["ssm_d"].reshape(1, D_SSM), w["ssm_glu_b"].reshape(1, D_SSM),
                                         w["ssm_norm_g"].reshape(1, D_SSM)], axis=0), 8)
    pool_vecs = _pad_rows(jnp.concatenate([w["pool_scale"].reshape(1, D_POOL), w["pool_norm_g"].reshape(1, D_POOL)],
                                          axis=0), 8)
    pw_b = w["pool_w"][0].astype(BF16)
    g1, g2, gf = w["norm1_g"].reshape(1, D_MODEL), w["norm2_g"].reshape(1, D_MODEL), w["final_norm_g"].reshape(1, D_MODEL)

    u, vv = _fwd_in(h0, g1, w_in_b, tm, gather_token)
    sr, si, y, ms = _s5_fwd(u, lam, bbr_b, bbi_b, crt_b, cit_b, s5_vecs, glu_b16, tc)
    feat, mp = _pool_fwd(vv, pw_b, pool_vecs, tc)
    late = _forward_halves(_copies_wait("gather_wait", late, gather_sems, mp, _build_gather))
    w_out_b = late[0].reshape(D_MODEL, D_MODEL)
    wg_b, wu_b, wd_b = (t.reshape(N_SHARD, FF_SHARD, D_MODEL) for t in late[1:])
    h1, n2, a, b, ff, dh2, loss_acc, dgf = _fwd_ffn(h0, ms, mp, w_out_b, g2, wg_b, wu_b, wd_b, gf, tgt, tm, n_rows)
    loss = lax.psum(loss_acc[0, 0], AXES)

    def quarters(t):
        if t.ndim == 2:
            t = t.reshape(N_SHARD, t.shape[0] // N_SHARD, t.shape[1])
        return t.reshape(N_SHARD, 2, t.shape[1] // 2, t.shape[2])

    def landing(like, lead, dtype):
        return [lax.empty((lead,) + t.shape[2:], dtype) for t in like]

    da, db, dh1, dg2 = _bwd_ffn(dh2, a, b, wg_b, wu_b, wd_b, h1, g2, tm)
    ffn_g = [quarters(t) for t in _grad_ffn(n2, da, db, ff, dh2, tm)]
    nf = len(ffn_g)
    moved, swap_sems, swap_token = _copies_start("swap_start", ffn_g + landing(ffn_g, N_SHARD, F32), (nf,), _build_swap)
    dms, dmp, dwo = _bwd_out(dh1, ms, mp, w_out_b, tm, swap_token)
    moved = _copies_wait("swap_wait", moved, swap_sems, dwo, _build_swap)
    ffn_parts, ffn_own = _sum_siblings("ffn", moved[:nf], moved[nf:], place)
    moved, exch_sems, exch_token = _copies_start("exchange_start", list(ffn_parts) + landing(ffn_g, 3, BF16), (3 * nf,),
                                                 _build_exchange)
    du, dbbr, dbbi, dcrt, dcit, dglu, ds5v, dlam = _s5_bwd(dms, y, u, sr, si, lam, bbr_b, bbi_b, crt_b, cit_b,
                                                           s5_vecs, glu_b16, tc, exch_token)
    dv, dpw, dpoolv = _pool_bwd(dmp, feat, pw_b, pool_vecs, tc)
    dh0, dwi, dg1 = _bwd_in(du, dv, h0, dh1, g1, w_in_b, tm)
    ffn_from_chips = _copies_wait("exchange_wait", moved, exch_sems, dh0, _build_exchange)[nf:]
    d_lre, d_lim, d_lstep, d_bre, d_bim, d_cre, d_cim, d_gluw = s5_vjp((dlam, dbbr, dbbi, dcrt, dcit, dglu))
    grad_x = dh0[N_META:n_rows][None]

    mix_g = [quarters(t) for t in (dwi, dwo)]
    mix_parts, mix_own = _sum_siblings("mix", mix_g, _swap_halves(mix_g), place)
    moved, mix_sems, mix_token = _copies_start("mix_exchange_start", list(mix_parts) + landing(mix_g, 3, BF16),
                                               (3 * len(mix_g),), _build_exchange)

    small_g = {
        "norm1_g": dg1, "ssm_lambda_re": d_lre, "ssm_lambda_im": d_lim, "ssm_log_step": d_lstep, "ssm_b_re": d_bre,
        "ssm_b_im": d_bim, "ssm_c_re": d_cre, "ssm_c_im": d_cim, "ssm_d": ds5v[0], "ssm_glu_w": d_gluw,
        "ssm_glu_b": ds5v[1], "ssm_norm_g": ds5v[2], "pool_w": dpw, "pool_scale": dpoolv[0], "pool_norm_g": dpoolv[1],
        "norm2_g": dg2, "final_norm_g": dgf,
    }
    like = [w[k] for k in SMALL]
    packed_g = _pack([small_g[k].reshape(w[k].shape) for k in SMALL] + [dh0[:N_META]])
    rows = packed_g.shape[0]
    packed = lambda t: _pad_rows(_pack([t[k] for k in SMALL]), rows)
    gathered = _gather_small(packed_g, mix_token)
    g_pk, d_pk, m_pk, v_pk = _reduce_small(gathered, packed(w), packed(m), packed(v))

    mix_from_chips = _copies_wait("mix_exchange_wait", moved, mix_sems, g_pk, _build_exchange)[len(mix_g):]
    joined = _join_halves(_sum_chips(list(mix_own) + list(ffn_own), list(mix_from_chips) + list(ffn_from_chips), place))
    g_large = [j.reshape(j.shape[0] * j.shape[1], j.shape[2]) for j in joined]
    w2d, m2d, v2d = ([local2d(k)(t[k]) for k in LARGE] for t in (w, m, v))
    d_large, m_large, v_large = _adamw("adamw_large", w2d, g_large, m2d, v2d, 8)
    g_small = _unpack(g_pk, like + [jax.ShapeDtypeStruct((N_META, D_MODEL), F32)])
    d_small, m_small, v_small = (_unpack(t, like) for t in (d_pk, m_pk, v_pk))
    q = place[0]
    g_meta = lax.dynamic_slice_in_dim(g_small[-1], q * (D_MODEL // N_SHARD), D_MODEL // N_SHARD, axis=1)
    d_meta, m_meta, v_meta = _adamw("adamw_meta", [w["meta_tokens"]], [g_meta], [m["meta_tokens"]],
                                    [v["meta_tokens"]], 1)

    grads, deltas, new_m, new_v = {}, {}, {}, {}
    for i, k in enumerate(SMALL):
        grads[k], deltas[k], new_m[k], new_v[k] = g_small[i], d_small[i], m_small[i], v_small[i]
    for i, k in enumerate(LARGE):
        back = (lambda t: t.T[None]) if k in ("w_gate", "w_up") else (lambda t: t[None])
        grads[k], deltas[k], new_m[k], new_v[k] = (back(t) for t in (g_large[i], d_large[i], m_large[i], v_large[i]))
    grads["meta_tokens"], deltas["meta_tokens"] = g_meta, d_meta[0]
    new_m["meta_tokens"], new_v["meta_tokens"] = m_meta[0], v_meta[0]
    return (loss, grad_x, *[grads[k] for k in WEIGHTS], *[deltas[k] for k in WEIGHTS],
            *[new_m[k] for k in WEIGHTS], *[new_v[k] for k in WEIGHTS])


def kernel(x, meta_tokens, norm1_g, w_in, ssm_lambda_re, ssm_lambda_im, ssm_log_step, ssm_b_re, ssm_b_im, ssm_c_re, ssm_c_im, ssm_d, ssm_glu_w, ssm_glu_b, ssm_norm_g, pool_w, pool_scale, pool_norm_g, w_out, norm2_g, w_gate, w_up, w_down, final_norm_g, loss_target, m_meta_tokens, m_norm1_g, m_w_in, m_ssm_lambda_re, m_ssm_lambda_im, m_ssm_log_step, m_ssm_b_re, m_ssm_b_im, m_ssm_c_re, m_ssm_c_im, m_ssm_d, m_ssm_glu_w, m_ssm_glu_b, m_ssm_norm_g, m_pool_w, m_pool_scale, m_pool_norm_g, m_w_out, m_norm2_g, m_w_gate, m_w_up, m_w_down, m_final_norm_g, v_meta_tokens, v_norm1_g, v_w_in, v_ssm_lambda_re, v_ssm_lambda_im, v_ssm_log_step, v_ssm_b_re, v_ssm_b_im, v_ssm_c_re, v_ssm_c_im, v_ssm_d, v_ssm_glu_w, v_ssm_glu_b, v_ssm_norm_g, v_pool_w, v_pool_scale, v_pool_norm_g, v_w_out, v_norm2_g, v_w_gate, v_w_up, v_w_down, v_final_norm_g):
    w = dict(meta_tokens=meta_tokens, norm1_g=norm1_g, w_in=w_in, ssm_lambda_re=ssm_lambda_re, ssm_lambda_im=ssm_lambda_im, ssm_log_step=ssm_log_step, ssm_b_re=ssm_b_re, ssm_b_im=ssm_b_im, ssm_c_re=ssm_c_re, ssm_c_im=ssm_c_im, ssm_d=ssm_d, ssm_glu_w=ssm_glu_w, ssm_glu_b=ssm_glu_b, ssm_norm_g=ssm_norm_g, pool_w=pool_w, pool_scale=pool_scale, pool_norm_g=pool_norm_g, w_out=w_out, norm2_g=norm2_g, w_gate=w_gate, w_up=w_up, w_down=w_down, final_norm_g=final_norm_g)
    m = dict(meta_tokens=m_meta_tokens, norm1_g=m_norm1_g, w_in=m_w_in, ssm_lambda_re=m_ssm_lambda_re, ssm_lambda_im=m_ssm_lambda_im, ssm_log_step=m_ssm_log_step, ssm_b_re=m_ssm_b_re, ssm_b_im=m_ssm_b_im, ssm_c_re=m_ssm_c_re, ssm_c_im=m_ssm_c_im, ssm_d=m_ssm_d, ssm_glu_w=m_ssm_glu_w, ssm_glu_b=m_ssm_glu_b, ssm_norm_g=m_ssm_norm_g, pool_w=m_pool_w, pool_scale=m_pool_scale, pool_norm_g=m_pool_norm_g, w_out=m_w_out, norm2_g=m_norm2_g, w_gate=m_w_gate, w_up=m_w_up, w_down=m_w_down, final_norm_g=m_final_norm_g)
    v = dict(meta_tokens=v_meta_tokens, norm1_g=v_norm1_g, w_in=v_w_in, ssm_lambda_re=v_ssm_lambda_re, ssm_lambda_im=v_ssm_lambda_im, ssm_log_step=v_ssm_log_step, ssm_b_re=v_ssm_b_re, ssm_b_im=v_ssm_b_im, ssm_c_re=v_ssm_c_re, ssm_c_im=v_ssm_c_im, ssm_d=v_ssm_d, ssm_glu_w=v_ssm_glu_w, ssm_glu_b=v_ssm_glu_b, ssm_norm_g=v_ssm_norm_g, pool_w=v_pool_w, pool_scale=v_pool_scale, pool_norm_g=v_pool_norm_g, w_out=v_w_out, norm2_g=v_norm2_g, w_gate=v_w_gate, w_up=v_w_up, w_down=v_w_down, final_norm_g=v_final_norm_g)
    return _step(x, loss_target, w, m, v)
```

```python
import functools
import math

import jax
import jax.numpy as jnp
from jax import lax
from jax.experimental import pallas as pl
from jax.experimental.pallas import tpu as pltpu

F32 = jnp.float32
BF16 = jnp.bfloat16
MESH = pl.DeviceIdType.MESH
AXES = ("x", "y", "c")

D_MODEL = 1024
D_SSM = 512
D_POOL = 512
N_META = 16
SSM_GROUP = 16
SSM_GROUPS = 32
SSM_STATE = 64
N_STATE = SSM_GROUPS * SSM_STATE
STATE_BLOCKS = N_STATE // 128
SUPER = 4
POOL_WINDOWS = (2, 4, 8, 16)
POOL_HALO = 16
D_FF = 2816
N_SHARD = 4
FF_SHARD = D_FF // N_SHARD
EPS = 1e-6
ADAM_LR, ADAM_B1, ADAM_B2, ADAM_EPS, ADAM_WD, ADAM_STEP = 0.001, 0.9, 0.999, 1e-08, 0.01, 10
VMEM_LIMIT = 56 * 1024 * 1024


def _plan(n_rows):
    if n_rows > 2048:
        tm, tc = 416, 320
    else:
        tm, tc = 128, 64
    step = math.lcm(tm, tc)
    return -(-n_rows // step) * step, tm, tc


def _params(sem=None):
    return pltpu.CompilerParams(dimension_semantics=sem, vmem_limit_bytes=VMEM_LIMIT)


def _dot(a, b):
    return jnp.dot(a, b, preferred_element_type=F32)


def _dot_nt(a, b):
    return lax.dot_general(a, b, (((1,), (1,)), ((), ())), preferred_element_type=F32)


def _dot_tn(a, b):
    return lax.dot_general(a, b, (((0,), (0,)), ((), ())), preferred_element_type=F32)


def _sigmoid(x):
    return 1.0 / (1.0 + jnp.exp(-x))


_GELU_C = math.sqrt(2.0 / math.pi)


def _gelu_and_grad(y):
    y2 = y * y
    t = jnp.tanh(_GELU_C * (y + 0.044715 * y * y2))
    g = 0.5 * y * (1.0 + t)
    dg = 0.5 * (1.0 + t) + 0.5 * y * (1.0 - t * t) * (_GELU_C * (1.0 + 3.0 * 0.044715 * y2))
    return g, dg


def _rms(x):
    return lax.rsqrt(jnp.mean(x * x, axis=-1, keepdims=True) + EPS)


def _rms_bwd(dn, xhat, r):
    return r * (dn - xhat * jnp.mean(dn * xhat, axis=-1, keepdims=True))


def _full(shape):
    nd = len(shape)
    return pl.BlockSpec(shape, lambda *_: (0,) * nd)


def _fwd_in(h0, g1, w_in_b, tm, token):
    n_pad = h0.shape[0]

    def body(h_ref, g_ref, w_ref, token_ref, u_ref, v_ref):
        h = h_ref[...]
        n1 = (h * _rms(h) * g_ref[...]).astype(BF16)
        proj = _dot(n1, w_ref[...])
        for i in range(4):
            u_ref[i] = proj[:, 128 * i:128 * (i + 1)]
        v_ref[...] = proj[:, D_SSM:]

    row = lambda w: pl.BlockSpec((tm, w), lambda i: (i, 0))
    return pl.pallas_call(
        body, grid=(n_pad // tm,), name="fwd_in",
        in_specs=[row(D_MODEL), _full((1, D_MODEL)), _full((D_MODEL, D_MODEL)), _ANY],
        out_specs=[pl.BlockSpec((4, tm, 128), lambda i: (0, i, 0)), row(D_POOL)],
        out_shape=[jax.ShapeDtypeStruct((4, n_pad, 128), F32), jax.ShapeDtypeStruct((n_pad, D_POOL), F32)],
        compiler_params=_params(("parallel",)),
    )(h0, g1, w_in_b, token)


def _fwd_ffn(h0, ms, mp, w_out_b, g2, wg_b, wu_b, wd_b, gf, target, tm, n_valid):
    n_pad = h0.shape[0]
    nt = n_pad // tm

    def body(h0_ref, ms_ref, mp_ref, wo_ref, g2_ref, wg_ref, wu_ref, wd_ref, gf_ref, tgt_ref,
             h1_ref, n2_ref, a_ref, b_ref, ff_ref, dh2_ref, loss_ref, dgf_ref, acc):
        i, q = pl.program_id(0), pl.program_id(1)

        @pl.when((i == 0) & (q == 0))
        def _():
            loss_ref[...] = jnp.zeros_like(loss_ref)
            dgf_ref[...] = jnp.zeros_like(dgf_ref)

        @pl.when(q == 0)
        def _():
            h1 = h0_ref[...] + _dot(ms_ref[...], wo_ref[:D_SSM, :]) + _dot(mp_ref[...], wo_ref[D_SSM:, :])
            h1_ref[...] = h1
            acc[...] = h1
            n2_ref[...] = (h1 * _rms(h1) * g2_ref[...]).astype(BF16)

        n2 = n2_ref[...]
        a = _dot_nt(n2, wg_ref[0])
        b = _dot_nt(n2, wu_ref[0])
        a_ref[0] = a
        b_ref[0] = b
        ff = (a * _sigmoid(a) * b).astype(BF16)
        ff_ref[0] = ff
        acc[...] += _dot(ff, wd_ref[0])

        @pl.when(q == N_SHARD - 1)
        def _():
            h2 = acc[...]
            r = _rms(h2)
            xhat = h2 * r
            gf_row = gf_ref[...]
            rows = i * tm + lax.broadcasted_iota(jnp.int32, (tm, 1), 0)
            valid = (rows >= N_META) & (rows < n_valid)
            diff = jnp.where(valid, xhat * gf_row - tgt_ref[...], 0.0)
            loss_ref[...] += jnp.full(loss_ref.shape, 0.5 / D_MODEL, F32) * jnp.sum(diff * diff)
            dout = diff * (1.0 / D_MODEL)
            dgf_ref[...] += jnp.sum(dout * xhat, axis=0, keepdims=True)
            dh2_ref[...] = _rms_bwd(dout * gf_row, xhat, r)

    row = lambda w: pl.BlockSpec((tm, w), lambda i, q: (i, 0))
    shard_rows = pl.BlockSpec((1, FF_SHARD, D_MODEL), lambda i, q: (q, 0, 0))
    act = pl.BlockSpec((1, tm, FF_SHARD), lambda i, q: (q, i, 0))
    sds = jax.ShapeDtypeStruct
    return pl.pallas_call(
        body, grid=(nt, N_SHARD), name="fwd_ffn",
        in_specs=[row(D_MODEL), row(D_SSM), row(D_POOL), _full((D_MODEL, D_MODEL)), _full((1, D_MODEL)),
                  shard_rows, shard_rows, shard_rows, _full((1, D_MODEL)), row(D_MODEL)],
        out_specs=[row(D_MODEL), row(D_MODEL), act, act, act, row(D_MODEL), _full((8, 128)), _full((1, D_MODEL))],
        out_shape=[sds((n_pad, D_MODEL), F32), sds((n_pad, D_MODEL), BF16),
                   sds((N_SHARD, n_pad, FF_SHARD), F32), sds((N_SHARD, n_pad, FF_SHARD), F32),
                   sds((N_SHARD, n_pad, FF_SHARD), BF16), sds((n_pad, D_MODEL), F32),
                   sds((8, 128), F32), sds((1, D_MODEL), F32)],
        scratch_shapes=[pltpu.VMEM((tm, D_MODEL), F32)],
        compiler_params=_params(("arbitrary", "arbitrary")),
    )(h0, ms, mp, w_out_b, g2, wg_b, wu_b, wd_b, gf, target)


def _bwd_ffn(dh2, a, b, wg_b, wu_b, wd_b, h1, g2, tm):
    n_pad = dh2.shape[0]

    def body(dh2_ref, a_ref, b_ref, wg_ref, wu_ref, wd_ref, h1_ref, g2_ref, da_ref, db_ref, dh1_ref, dg2_ref, acc):
        i, q = pl.program_id(0), pl.program_id(1)

        @pl.when((i == 0) & (q == 0))
        def _():
            dg2_ref[...] = jnp.zeros_like(dg2_ref)

        dff = _dot_nt(dh2_ref[...].astype(BF16), wd_ref[0])
        a_v, b_v = a_ref[0], b_ref[0]
        sig = _sigmoid(a_v)
        da = (dff * b_v * sig * (1.0 + a_v * (1.0 - sig))).astype(BF16)
        db = (dff * a_v * sig).astype(BF16)
        da_ref[0] = da
        db_ref[0] = db
        part = _dot(da, wg_ref[0]) + _dot(db, wu_ref[0])

        @pl.when(q == 0)
        def _():
            acc[...] = part

        @pl.when(q > 0)
        def _():
            acc[...] += part

        @pl.when(q == N_SHARD - 1)
        def _():
            h1 = h1_ref[...]
            r = _rms(h1)
            xhat = h1 * r
            dn2 = acc[...]
            dg2_ref[...] += jnp.sum(dn2 * xhat, axis=0, keepdims=True)
            dh1_ref[...] = dh2_ref[...] + _rms_bwd(dn2 * g2_ref[...], xhat, r)

    row = lambda w: pl.BlockSpec((tm, w), lambda i, q: (i, 0))
    shard_rows = pl.BlockSpec((1, FF_SHARD, D_MODEL), lambda i, q: (q, 0, 0))
    act = pl.BlockSpec((1, tm, FF_SHARD), lambda i, q: (q, i, 0))
    sds = jax.ShapeDtypeStruct
    return pl.pallas_call(
        body, grid=(n_pad // tm, N_SHARD), name="bwd_ffn",
        in_specs=[row(D_MODEL), act, act, shard_rows, shard_rows, shard_rows, row(D_MODEL), _full((1, D_MODEL))],
        out_specs=[act, act, row(D_MODEL), _full((1, D_MODEL))],
        out_shape=[sds((N_SHARD, n_pad, FF_SHARD), BF16), sds((N_SHARD, n_pad, FF_SHARD), BF16),
                   sds((n_pad, D_MODEL), F32), sds((1, D_MODEL), F32)],
        scratch_shapes=[pltpu.VMEM((tm, D_MODEL), F32)],
        compiler_params=_params(("arbitrary", "arbitrary")),
    )(dh2, a, b, wg_b, wu_b, wd_b, h1, g2)


def _grad_ffn(n2, da, db, ff, dh2, tm):
    n_pad = n2.shape[0]

    def body(n2_ref, da_ref, db_ref, ff_ref, dh2_ref, dwg_ref, dwu_ref, dwd_ref):
        i = pl.program_id(1)
        n2_v = n2_ref[...]
        gg = _dot_tn(da_ref[0], n2_v)
        gu = _dot_tn(db_ref[0], n2_v)
        gd = _dot_tn(ff_ref[0], dh2_ref[...].astype(BF16))

        @pl.when(i == 0)
        def _():
            dwg_ref[0] = gg
            dwu_ref[0] = gu
            dwd_ref[0] = gd

        @pl.when(i > 0)
        def _():
            dwg_ref[0] += gg
            dwu_ref[0] += gu
            dwd_ref[0] += gd

    row = lambda w: pl.BlockSpec((tm, w), lambda q, i: (i, 0))
    act = pl.BlockSpec((1, tm, FF_SHARD), lambda q, i: (q, i, 0))
    sds = jax.ShapeDtypeStruct
    return pl.pallas_call(
        body, grid=(N_SHARD, n_pad // tm), name="grad_ffn",
        in_specs=[row(D_MODEL), act, act, act, row(D_MODEL)],
        out_specs=[pl.BlockSpec((1, FF_SHARD, D_MODEL), lambda q, i: (q, 0, 0))] * 3,
        out_shape=[sds((N_SHARD, FF_SHARD, D_MODEL), F32)] * 3,
        compiler_params=_params(("parallel", "arbitrary")),
    )(n2, da, db, ff, dh2)


def _bwd_out(dh1, ms, mp, w_out_b, tm, token):
    n_pad = dh1.shape[0]

    def body(dh1_ref, ms_ref, mp_ref, wo_ref, token_ref, dms_ref, dmp_ref, dwo_ref):
        i = pl.program_id(0)

        @pl.when(i == 0)
        def _():
            dwo_ref[...] = jnp.zeros_like(dwo_ref)

        d = dh1_ref[...].astype(BF16)
        dms = _dot_nt(d, wo_ref[:D_SSM, :])
        for k in range(4):
            dms_ref[k] = dms[:, 128 * k:128 * (k + 1)]
        dmp_ref[...] = _dot_nt(d, wo_ref[D_SSM:, :])
        dwo_ref[:D_SSM, :] += _dot_tn(ms_ref[...], d)
        dwo_ref[D_SSM:, :] += _dot_tn(mp_ref[...], d)

    row = lambda w: pl.BlockSpec((tm, w), lambda i: (i, 0))
    sds = jax.ShapeDtypeStruct
    return pl.pallas_call(
        body, grid=(n_pad // tm,), name="bwd_out",
        in_specs=[row(D_MODEL), row(D_SSM), row(D_POOL), _full((D_MODEL, D_MODEL)), _ANY],
        out_specs=[pl.BlockSpec((4, tm, 128), lambda i: (0, i, 0)), row(D_POOL), _full((D_MODEL, D_MODEL))],
        out_shape=[sds((4, n_pad, 128), F32), sds((n_pad, D_POOL), F32), sds((D_MODEL, D_MODEL), F32)],
        compiler_params=_params(("arbitrary",)),
    )(dh1, ms, mp, w_out_b, token)


def _bwd_in(du, dv, h0, dh1, g1, w_in_b, tm):
    n_pad = h0.shape[0]

    def body(du_ref, dv_ref, h0_ref, dh1_ref, g1_ref, w_ref, dh0_ref, dwi_ref, dg1_ref):
        i = pl.program_id(0)

        @pl.when(i == 0)
        def _():
            dwi_ref[...] = jnp.zeros_like(dwi_ref)
            dg1_ref[...] = jnp.zeros_like(dg1_ref)

        dub = du_ref[...].astype(BF16)
        dvb = dv_ref[...].astype(BF16)
        dn1 = _dot_nt(dub, w_ref[:, :D_SSM]) + _dot_nt(dvb, w_ref[:, D_SSM:])
        h = h0_ref[...]
        r = _rms(h)
        xhat = h * r
        g_row = g1_ref[...]
        n1 = (xhat * g_row).astype(BF16)
        dwi_ref[:, :D_SSM] += _dot_tn(n1, dub)
        dwi_ref[:, D_SSM:] += _dot_tn(n1, dvb)
        dg1_ref[...] += jnp.sum(dn1 * xhat, axis=0, keepdims=True)
        dh0_ref[...] = dh1_ref[...] + _rms_bwd(dn1 * g_row, xhat, r)

    row = lambda w: pl.BlockSpec((tm, w), lambda i: (i, 0))
    sds = jax.ShapeDtypeStruct
    return pl.pallas_call(
        body, grid=(n_pad // tm,), name="bwd_in",
        in_specs=[row(D_SSM), row(D_POOL), row(D_MODEL), row(D_MODEL), _full((1, D_MODEL)), _full((D_MODEL, D_MODEL))],
        out_specs=[row(D_MODEL), _full((D_MODEL, D_MODEL)), _full((1, D_MODEL))],
        out_shape=[sds((n_pad, D_MODEL), F32), sds((D_MODEL, D_MODEL), F32), sds((1, D_MODEL), F32)],
        compiler_params=_params(("arbitrary",)),
    )(du, dv, h0, dh1, g1, w_in_b)


SEGMENTS = 8


def _interleaved(ref, seg):
    return jnp.concatenate(
        [jnp.concatenate([ref[i, pl.ds(j, SEGMENTS, stride=seg), :] for i in range(4)], axis=1) for j in range(seg)],
        axis=0)


def _time_order(scratch, val, seg):
    for i in range(4):
        scratch[i] = val[:, 128 * i:128 * (i + 1)]
    tiles = []
    for m in range(val.shape[0] // 8):
        s, j0 = divmod(8 * m, seg)
        tiles.append(jnp.concatenate(
            [scratch[i, pl.ds(8 * j0 + s, 8, stride=SEGMENTS), :] for i in range(4)], axis=1))
    return jnp.concatenate(tiles, axis=0)


def _power_table(lam_ref, pw_r, pw_i, seg):
    a_r = jnp.broadcast_to(lam_ref[0:1, :], (SEGMENTS, N_STATE))
    a_i = jnp.broadcast_to(lam_ref[1:2, :], (SEGMENTS, N_STATE))
    p_r, p_i = a_r, a_i
    for k in range(seg):
        pw_r[SEGMENTS * k:SEGMENTS * (k + 1), :] = p_r
        pw_i[SEGMENTS * k:SEGMENTS * (k + 1), :] = p_i
        p_r, p_i = p_r * a_r - p_i * a_i, p_r * a_i + p_i * a_r


def _segment_scan(xr_ref, xi_ref, cols, pw_r, pw_i, hr_s, hi_s, seg, reverse):
    sign = -1.0 if reverse else 1.0
    a_r, a_i = pw_r[0:SEGMENTS, cols], sign * pw_i[0:SEGMENTS, cols]

    def step(n, carry):
        hr, hi = carry
        o = pl.multiple_of((seg - 1 - n if reverse else n) * SEGMENTS, SEGMENTS)
        nr = a_r * hr - a_i * hi + xr_ref[pl.ds(o, SEGMENTS), cols]
        ni = a_r * hi + a_i * hr + xi_ref[pl.ds(o, SEGMENTS), cols]
        xr_ref[pl.ds(o, SEGMENTS), cols] = nr
        xi_ref[pl.ds(o, SEGMENTS), cols] = ni
        return nr, ni

    zero = jnp.zeros((SEGMENTS, cols.stop - cols.start), F32)
    e_r, e_i = lax.fori_loop(0, seg, step, (zero, zero), unroll=2)

    top = SEGMENTS * (seg - 1)
    ls_r, ls_i = pw_r[top:top + 1, cols], sign * pw_i[top:top + 1, cols]
    c_r, c_i = hr_s[0:1, cols], hi_s[0:1, cols]
    in_r, in_i = [None] * SEGMENTS, [None] * SEGMENTS
    for s in (range(SEGMENTS - 1, -1, -1) if reverse else range(SEGMENTS)):
        in_r[s], in_i[s] = c_r, c_i
        c_r, c_i = (e_r[s:s + 1, :] + ls_r * c_r - ls_i * c_i, e_i[s:s + 1, :] + ls_r * c_i + ls_i * c_r)
    hr_s[0:1, cols] = c_r
    hi_s[0:1, cols] = c_i
    cm_r, cm_i = jnp.concatenate(in_r, axis=0), jnp.concatenate(in_i, axis=0)

    def fix(jj, _):
        o = pl.multiple_of(jj * SEGMENTS, SEGMENTS)
        k = pl.multiple_of((seg - 1 - jj if reverse else jj) * SEGMENTS, SEGMENTS)
        p_r, p_i = pw_r[pl.ds(k, SEGMENTS), cols], sign * pw_i[pl.ds(k, SEGMENTS), cols]
        xr_ref[pl.ds(o, SEGMENTS), cols] += p_r * cm_r - p_i * cm_i
        xi_ref[pl.ds(o, SEGMENTS), cols] += p_r * cm_i + p_i * cm_r
        return 0

    lax.fori_loop(0, seg, fix, 0, unroll=2)


def _s5_tail(y, glu_ref, glub):
    g, dgelu = _gelu_and_grad(y)
    gb = g.astype(BF16)
    gate = jnp.concatenate([_dot(gb[:, 128 * j:128 * (j + 1)], glu_ref[j]) for j in range(SUPER)], axis=1) + glub
    sig = _sigmoid(gate)
    return g, gb, dgelu, sig, g * sig


def _s5_fwd(u4, lam, bbr, bbi, crt, cit, vecs, glu, tc):
    n_pad = u4.shape[1]
    seg = tc // SEGMENTS

    def body(u_ref, lam_ref, bbr_ref, bbi_ref, crt_ref, cit_ref, vec_ref, glu_ref,
             sr_ref, si_ref, y_ref, ms_ref, hr_s, hi_s, pw_r, pw_i, lanes):
        @pl.when(pl.program_id(0) == 0)
        def _():
            hr_s[...] = jnp.zeros_like(hr_s)
            hi_s[...] = jnp.zeros_like(hi_s)
            _power_table(lam_ref, pw_r, pw_i, seg)

        u_v = _interleaved(u_ref, seg)
        ub = u_v.astype(BF16)
        for j in range(SUPER):
            uj = ub[:, 128 * j:128 * (j + 1)]
            sr_ref[:, 512 * j:512 * (j + 1)] = _dot(uj, bbr_ref[j])
            si_ref[:, 512 * j:512 * (j + 1)] = _dot(uj, bbi_ref[j])
        for j in range(SUPER):
            _segment_scan(sr_ref, si_ref, slice(512 * j, 512 * (j + 1)), pw_r, pw_i, hr_s, hi_s, seg, False)

        d_row, glub, gs = vec_ref[0:1, :], vec_ref[1:2, :], vec_ref[2:3, :]
        ys_c = []
        for j in range(SUPER):
            sr_j = sr_ref[:, 512 * j:512 * (j + 1)].astype(BF16)
            si_j = si_ref[:, 512 * j:512 * (j + 1)].astype(BF16)
            ys_c.append(_dot(sr_j, crt_ref[j]) - _dot(si_j, cit_ref[j]))
        y = jnp.concatenate(ys_c, axis=1) + d_row * u_v
        y_ref[...] = y
        _, _, _, _, ys = _s5_tail(y, glu_ref, glub)
        ms_ref[...] = _time_order(lanes, ys * _rms(ys) * gs, seg).astype(BF16)

    chunk = lambda w: pl.BlockSpec((tc, w), lambda c: (c, 0))
    lane_blocks = pl.BlockSpec((4, tc, 128), lambda c: (0, c, 0))
    sds = jax.ShapeDtypeStruct
    return pl.pallas_call(
        body, grid=(n_pad // tc,), name="s5_fwd",
        in_specs=[lane_blocks, _full((8, N_STATE)), _full((SUPER, 128, 512)), _full((SUPER, 128, 512)),
                  _full((SUPER, 512, 128)), _full((SUPER, 512, 128)), _full((8, D_SSM)), _full((SUPER, 128, 128))],
        out_specs=[chunk(N_STATE), chunk(N_STATE), chunk(D_SSM), chunk(D_SSM)],
        out_shape=[sds((n_pad, N_STATE), F32), sds((n_pad, N_STATE), F32),
                   sds((n_pad, D_SSM), F32), sds((n_pad, D_SSM), BF16)],
        scratch_shapes=[pltpu.VMEM((8, N_STATE), F32), pltpu.VMEM((8, N_STATE), F32),
                        pltpu.VMEM((tc, N_STATE), F32), pltpu.VMEM((tc, N_STATE), F32),
                        pltpu.VMEM((4, tc, 128), F32)],
        compiler_params=_params(("arbitrary",)),
    )(u4, lam, bbr, bbi, crt, cit, vecs, glu)


def _s5_bwd(dms4, y, u4, sr, si, lam, bbr, bbi, crt, cit, vecs, glu, tc, token):
    n_pad = u4.shape[1]
    nc = n_pad // tc
    seg = tc // SEGMENTS

    def body(dms_ref, y_ref, u_ref, sr_ref, si_ref, pr_ref, pi_ref, lam_ref, bbr_ref, bbi_ref, crt_ref, cit_ref,
             vec_ref, glu_ref, token_ref, du_ref, dbbr_ref, dbbi_ref, dcrt_ref, dcit_ref, dglu_ref, dvec_ref, dlam_ref,
             qr_s, qi_s, cr_s, ci_s, pw_r, pw_i, lanes):
        c = pl.program_id(0)

        @pl.when(c == 0)
        def _():
            for ref in (dbbr_ref, dbbi_ref, dcrt_ref, dcit_ref, dglu_ref, dvec_ref, dlam_ref, cr_s, ci_s):
                ref[...] = jnp.zeros_like(ref)
            _power_table(lam_ref, pw_r, pw_i, seg)

        d_row, glub, gs = vec_ref[0:1, :], vec_ref[1:2, :], vec_ref[2:3, :]
        y_v, u_v = y_ref[...], _interleaved(u_ref, seg)
        ub = u_v.astype(BF16)
        g, gb, dgelu, sig, ys = _s5_tail(y_v, glu_ref, glub)
        r = _rms(ys)
        xhat = ys * r
        dm = _interleaved(dms_ref, seg)
        dys = _rms_bwd(dm * gs, xhat, r)
        dgate = dys * g * sig * (1.0 - sig)
        dgateb = dgate.astype(BF16)
        dg = dys * sig + jnp.concatenate(
            [_dot_nt(dgateb[:, 128 * j:128 * (j + 1)], glu_ref[j]) for j in range(SUPER)], axis=1)
        dy = dg * dgelu
        dyb = dy.astype(BF16)
        dvec_ref[0:1, :] += jnp.sum(dy * u_v, axis=0, keepdims=True)
        dvec_ref[1:2, :] += jnp.sum(dgate, axis=0, keepdims=True)
        dvec_ref[2:3, :] += jnp.sum(dm * xhat, axis=0, keepdims=True)

        for j in range(SUPER):
            cols, states = slice(128 * j, 128 * (j + 1)), slice(512 * j, 512 * (j + 1))
            dglu_ref[j] += _dot_tn(gb[:, cols], dgateb[:, cols])
            dcrt_ref[j] += _dot_tn(sr_ref[:, states].astype(BF16), dyb[:, cols])
            dcit_ref[j] -= _dot_tn(si_ref[:, states].astype(BF16), dyb[:, cols])
            qr_s[:, states] = _dot_nt(dyb[:, cols], crt_ref[j])
            qi_s[:, states] = -_dot_nt(dyb[:, cols], cit_ref[j])

        first = c == nc - 1
        row0 = lax.broadcasted_iota(jnp.int32, (SEGMENTS, 1), 0) == 0
        last = (seg - 1) * SEGMENTS
        for j in range(SUPER):
            states = slice(512 * j, 512 * (j + 1))
            _segment_scan(qr_s, qi_s, states, pw_r, pw_i, cr_s, ci_s, seg, True)

            before_r = jnp.where(first, 0.0, pltpu.roll(pr_ref[:, states], 1, 0))
            before_i = jnp.where(first, 0.0, pltpu.roll(pi_ref[:, states], 1, 0))
            hp_r = jnp.where(row0, before_r, pltpu.roll(sr_ref[pl.ds(last, SEGMENTS), states], 1, 0))
            hp_i = jnp.where(row0, before_i, pltpu.roll(si_ref[pl.ds(last, SEGMENTS), states], 1, 0))
            q_r, q_i = qr_s[pl.ds(0, SEGMENTS), states], qi_s[pl.ds(0, SEGMENTS), states]

            def dlam_step(jj, acc):
                o = pl.multiple_of(jj * SEGMENTS, SEGMENTS)
                above = pl.multiple_of((jj - 1) * SEGMENTS, SEGMENTS)
                h_r, h_i = sr_ref[pl.ds(above, SEGMENTS), states], si_ref[pl.ds(above, SEGMENTS), states]
                t_r, t_i = qr_s[pl.ds(o, SEGMENTS), states], qi_s[pl.ds(o, SEGMENTS), states]
                return acc[0] + t_r * h_r + t_i * h_i, acc[1] + t_i * h_r - t_r * h_i

            acc = lax.fori_loop(1, seg, dlam_step, (q_r * hp_r + q_i * hp_i, q_i * hp_r - q_r * hp_i), unroll=2)
            dlam_ref[0:SEGMENTS, states] += acc[0]
            dlam_ref[SEGMENTS:, states] += acc[1]

        du_c = []
        for j in range(SUPER):
            cols, states = slice(128 * j, 128 * (j + 1)), slice(512 * j, 512 * (j + 1))
            qr_j = qr_s[:, states].astype(BF16)
            qi_j = qi_s[:, states].astype(BF16)
            du_c.append(_dot_nt(qr_j, bbr_ref[j]) + _dot_nt(qi_j, bbi_ref[j]))
            dbbr_ref[j] += _dot_tn(ub[:, cols], qr_j)
            dbbi_ref[j] += _dot_tn(ub[:, cols], qi_j)
        du_ref[...] = _time_order(lanes, jnp.concatenate(du_c, axis=1) + dy * d_row, seg)

    rev = lambda c: nc - 1 - c
    chunk = lambda w: pl.BlockSpec((tc, w), lambda c: (rev(c), 0))
    lane_blocks = pl.BlockSpec((4, tc, 128), lambda c: (0, rev(c), 0))
    prev = pl.BlockSpec((SEGMENTS, N_STATE), lambda c: (jnp.maximum(rev(c) * seg - 1, 0), 0))
    sds = jax.ShapeDtypeStruct
    return pl.pallas_call(
        body, grid=(nc,), name="s5_bwd",
        in_specs=[lane_blocks, chunk(D_SSM), lane_blocks, chunk(N_STATE), chunk(N_STATE), prev, prev,
                  _full((8, N_STATE)), _full((SUPER, 128, 512)), _full((SUPER, 128, 512)),
                  _full((SUPER, 512, 128)), _full((SUPER, 512, 128)), _full((8, D_SSM)), _full((SUPER, 128, 128)), _ANY],
        out_specs=[chunk(D_SSM), _full((SUPER, 128, 512)), _full((SUPER, 128, 512)), _full((SUPER, 512, 128)),
                   _full((SUPER, 512, 128)), _full((SUPER, 128, 128)), _full((8, D_SSM)), _full((2 * SEGMENTS, N_STATE))],
        out_shape=[sds((n_pad, D_SSM), F32), sds((SUPER, 128, 512), F32), sds((SUPER, 128, 512), F32),
                   sds((SUPER, 512, 128), F32), sds((SUPER, 512, 128), F32), sds((SUPER, 128, 128), F32),
                   sds((8, D_SSM), F32), sds((2 * SEGMENTS, N_STATE), F32)],
        scratch_shapes=[pltpu.VMEM((tc, N_STATE), F32), pltpu.VMEM((tc, N_STATE), F32),
                        pltpu.VMEM((8, N_STATE), F32), pltpu.VMEM((8, N_STATE), F32),
                        pltpu.VMEM((tc, N_STATE), F32), pltpu.VMEM((tc, N_STATE), F32),
                        pltpu.VMEM((4, tc, 128), F32)],
        compiler_params=_params(("arbitrary",)),
    )(dms4, y, u4, sr, si, sr, si, lam, bbr, bbi, crt, cit, vecs, glu, token)


def _inv_count(c_idx, tc, w):
    t = c_idx * tc + lax.broadcasted_iota(jnp.int32, (tc, 1), 0)
    return 1.0 / jnp.minimum(t + 1, w).astype(F32)


def _pool_fwd(v, pw_b, vecs, tc):
    n_pad = v.shape[0]

    def body(v_ref, pw_ref, vec_ref, feat_ref, mp_ref, hist):
        c = pl.program_id(0)

        @pl.when(c == 0)
        def _():
            hist[...] = jnp.zeros_like(hist)

        v_v = v_ref[...]
        ext = jnp.concatenate([hist[...], v_v], axis=0)
        hist[...] = v_v[tc - POOL_HALO:, :]
        feats, ps = [], []
        for k, w in enumerate(POOL_WINDOWS):
            cols = slice(128 * k, 128 * (k + 1))
            s = ext[:, cols]
            sh = 1
            while sh < w:
                s = s + pltpu.roll(s, sh, 0)
                sh *= 2
            f = (s[POOL_HALO:, :] * _inv_count(c, tc, w) - v_v[:, cols]).astype(BF16)
            feats.append(f)
            ps.append(_dot(f, pw_ref[k]))
        feat_ref[...] = jnp.concatenate(feats, axis=1)
        yp = jnp.concatenate(ps, axis=1) * vec_ref[0:1, :]
        mp_ref[...] = (yp * _rms(yp) * vec_ref[1:2, :]).astype(BF16)

    chunk = lambda w: pl.BlockSpec((tc, w), lambda c: (c, 0))
    sds = jax.ShapeDtypeStruct
    return pl.pallas_call(
        body, grid=(n_pad // tc,), name="pool_fwd",
        in_specs=[chunk(D_POOL), _full((4, 128, 128)), _full((8, D_POOL))],
        out_specs=[chunk(D_POOL), chunk(D_POOL)],
        out_shape=[sds((n_pad, D_POOL), BF16), sds((n_pad, D_POOL), BF16)],
        scratch_shapes=[pltpu.VMEM((POOL_HALO, D_POOL), F32)],
        compiler_params=_params(("arbitrary",)),
    )(v, pw_b, vecs)


def _pool_bwd(dmp, feat, pw_b, vecs, tc):
    n_pad = dmp.shape[0]
    nc = n_pad // tc

    def body(dmp_ref, feat_ref, pw_ref, vec_ref, dv_ref, dpw_ref, dvec_ref, fut):
        c = pl.program_id(0)

        @pl.when(c == 0)
        def _():
            fut[...] = jnp.zeros_like(fut)
            dpw_ref[...] = jnp.zeros_like(dpw_ref)
            dvec_ref[...] = jnp.zeros_like(dvec_ref)

        scale, gp = vec_ref[0:1, :], vec_ref[1:2, :]
        feat_v = feat_ref[...]
        p = jnp.concatenate([_dot(feat_v[:, 128 * k:128 * (k + 1)], pw_ref[k]) for k in range(4)], axis=1)
        yp = p * scale
        r = _rms(yp)
        xhat = yp * r
        dm = dmp_ref[...]
        dyp = _rms_bwd(dm * gp, xhat, r)
        dvec_ref[0:1, :] += jnp.sum(dyp * p, axis=0, keepdims=True)
        dvec_ref[1:2, :] += jnp.sum(dm * xhat, axis=0, keepdims=True)
        dpb = (dyp * scale).astype(BF16)
        es, dfs = [], []
        for k, w in enumerate(POOL_WINDOWS):
            cols = slice(128 * k, 128 * (k + 1))
            dpw_ref[k] += _dot_tn(feat_v[:, cols], dpb[:, cols])
            df = _dot_nt(dpb[:, cols], pw_ref[k])
            dfs.append(df)
            es.append(df * _inv_count(nc - 1 - c, tc, w))
        e = jnp.concatenate(es, axis=1)
        ext = jnp.concatenate([e, fut[...]], axis=0)
        fut[...] = e[:POOL_HALO, :]
        n_ext = tc + POOL_HALO
        dvs = []
        for k, w in enumerate(POOL_WINDOWS):
            s = ext[:, 128 * k:128 * (k + 1)]
            sh = 1
            while sh < w:
                s = s + pltpu.roll(s, n_ext - sh, 0)
                sh *= 2
            dvs.append(s[:tc, :] - dfs[k])
        dv_ref[...] = jnp.concatenate(dvs, axis=1)

    chunk = lambda w: pl.BlockSpec((tc, w), lambda c: (nc - 1 - c, 0))
    sds = jax.ShapeDtypeStruct
    return pl.pallas_call(
        body, grid=(nc,), name="pool_bwd",
        in_specs=[chunk(D_POOL), chunk(D_POOL), _full((4, 128, 128)), _full((8, D_POOL))],
        out_specs=[chunk(D_POOL), _full((4, 128, 128)), _full((8, D_POOL))],
        out_shape=[sds((n_pad, D_POOL), F32), sds((4, 128, 128), F32), sds((8, D_POOL), F32)],
        scratch_shapes=[pltpu.VMEM((POOL_HALO, D_POOL), F32)],
        compiler_params=_params(("arbitrary",)),
    )(dmp, feat, pw_b, vecs)


def _place():
    x, y, c = lax.axis_index("x"), lax.axis_index("y"), lax.axis_index("c")
    chips = [(1 - x, y), (x, 1 - y), (1 - x, 1 - y)]
    return x, y, c, chips


_ANY = pl.BlockSpec(memory_space=pl.ANY)


def _cast_shards(shards, dtypes, place):
    n = len(shards)

    def body(place_ref, *refs):
        for i in range(n):
            refs[n + i][0] = refs[i][...].astype(dtypes[i])

    return pl.pallas_call(
        body, name="cast_shards",
        grid_spec=pltpu.PrefetchScalarGridSpec(
            num_scalar_prefetch=1, grid=(1,),
            in_specs=[pl.BlockSpec(s.shape, lambda i, p: (0, 0, 0)) for s in shards],
            out_specs=[pl.BlockSpec((1,) + s.shape, lambda i, p: (p[0], 0, 0, 0)) for s in shards]),
        out_shape=[jax.ShapeDtypeStruct((N_SHARD,) + s.shape, dt) for s, dt in zip(shards, dtypes)],
        compiler_params=_params(("arbitrary",)),
    )(place, *shards)


def _gather_shards(full):
    n = len(full)

    def body(*refs):
        outs = refs[n:2 * n]
        ici_send, ici_recv, d2d_send, d2d_recv = refs[2 * n:]
        x, y, c, chips = _place()
        q = 2 * x + y
        sibling = (x, y, 1 - c)

        def ici(i, j, shard, to):
            return pltpu.make_async_remote_copy(src_ref=outs[i].at[q, c], dst_ref=outs[i].at[shard, c],
                                                send_sem=ici_send.at[i, j], recv_sem=ici_recv.at[i, j],
                                                device_id=to, device_id_type=MESH)

        def d2d(i, j, shard, half):
            return pltpu.make_async_remote_copy(src_ref=outs[i].at[shard, c], dst_ref=outs[i].at[shard, half],
                                                send_sem=d2d_send.at[i, j], recv_sem=d2d_recv.at[i, j],
                                                device_id=sibling, device_id_type=MESH)

        sends = [ici(i, j, q, (*chip, c)) for i in range(n) for j, chip in enumerate(chips)]
        for cp in sends:
            cp.start()
        passed = []
        for i in range(n):
            for j, (cx, cy) in enumerate(chips):
                ici(i, j, 2 * cx + cy, (cx, cy, c)).wait_recv()
                cp = d2d(i, j, 2 * cx + cy, c)
                cp.start()
                passed.append(cp)
        for i in range(n):
            for j, (cx, cy) in enumerate(chips):
                d2d(i, j, 2 * cx + cy, 1 - c).wait_recv()
        for cp in sends + passed:
            cp.wait_send()

    return pl.pallas_call(
        body, name="gather_shards",
        in_specs=[_ANY] * n, out_specs=[_ANY] * n,
        out_shape=[jax.ShapeDtypeStruct(f.shape, f.dtype) for f in full],
        input_output_aliases={i: i for i in range(n)},
        scratch_shapes=[pltpu.SemaphoreType.DMA((n, 3)), pltpu.SemaphoreType.DMA((n, 3)),
                        pltpu.SemaphoreType.DMA((n, 3)), pltpu.SemaphoreType.DMA((n, 3))],
    )(*full)


def _forward_halves(full):
    n = len(full)

    def body(*refs):
        outs = refs[n:2 * n]
        send, recv = refs[2 * n:]
        x, y, c, chips = _place()

        def d2d(i, j, shard, half):
            return pltpu.make_async_remote_copy(src_ref=outs[i].at[shard, c], dst_ref=outs[i].at[shard, half],
                                                send_sem=send.at[i, j], recv_sem=recv.at[i, j],
                                                device_id=(x, y, 1 - c), device_id_type=MESH)

        cps = [d2d(i, j, 2 * cx + cy, c) for i in range(n) for j, (cx, cy) in enumerate(chips)]
        for cp in cps:
            cp.start()
        for i in range(n):
            for j, (cx, cy) in enumerate(chips):
                d2d(i, j, 2 * cx + cy, 1 - c).wait_recv()
        for cp in cps:
            cp.wait_send()

    return pl.pallas_call(
        body, name="forward_halves",
        in_specs=[_ANY] * n, out_specs=[_ANY] * n,
        out_shape=[jax.ShapeDtypeStruct(f.shape, f.dtype) for f in full],
        input_output_aliases={i: i for i in range(n)},
        scratch_shapes=[pltpu.SemaphoreType.DMA((n, 3)), pltpu.SemaphoreType.DMA((n, 3))],
    )(*full)


_HBM = pl.BlockSpec(memory_space=pltpu.HBM)
_SEM = pl.BlockSpec(memory_space=pltpu.SEMAPHORE)
_EFFECT = pltpu.SideEffectType.DATAFLOW_SIDE_EFFECTING


def _copies_start(name, arrays, sem_shape, build):
    n = len(arrays)

    def body(*refs):
        outs = refs[n:2 * n]
        send, recv, token = refs[2 * n:]
        sends, _ = build(outs, send, recv)
        for cp in sends:
            cp.start()
        token[...] = jnp.zeros_like(token)

    out = pl.pallas_call(
        body, name=name, in_specs=[_HBM] * n,
        out_specs=[_HBM] * n + [_SEM, _SEM, pl.BlockSpec(memory_space=pltpu.VMEM)],
        out_shape=[pltpu.HBM(a.shape, a.dtype) for a in arrays]
        + [pltpu.SemaphoreType.DMA(sem_shape), pltpu.SemaphoreType.DMA(sem_shape), jax.ShapeDtypeStruct((8, 128), F32)],
        input_output_aliases={i: i for i in range(n)},
        compiler_params=pltpu.CompilerParams(has_side_effects=_EFFECT),
    )(*[pltpu.with_memory_space_constraint(a, pltpu.HBM) for a in arrays])
    return list(out[:n]), (out[n], out[n + 1]), out[n + 2]


def _copies_wait(name, arrays, sems, after, build):
    n = len(arrays)

    def body(*refs):
        ins = refs[:n]
        send, recv = refs[n], refs[n + 1]
        sends, recvs = build(ins, send, recv)
        for cp in sends:
            cp.wait_send()
        for cp in recvs:
            cp.wait_recv()

    return list(pl.pallas_call(
        body, name=name, in_specs=[_HBM] * n + [_SEM, _SEM, _ANY], out_specs=[_HBM] * n,
        out_shape=[pltpu.HBM(a.shape, a.dtype) for a in arrays],
        input_output_aliases={i: i for i in range(n)},
        compiler_params=pltpu.CompilerParams(has_side_effects=_EFFECT),
    )(*arrays, *sems, after))


def _remote(src, dst, send_sem, recv_sem, to):
    return pltpu.make_async_remote_copy(src_ref=src, dst_ref=dst, send_sem=send_sem, recv_sem=recv_sem,
                                        device_id=to, device_id_type=MESH)


def _build_gather(refs, send, recv):
    x, y, c, chips = _place()
    q = 2 * x + y
    pairs = [(i, j, chip) for i in range(len(refs)) for j, chip in enumerate(chips)]
    sends = [_remote(refs[i].at[q, c], refs[i].at[q, c], send.at[3 * i + j], recv.at[3 * i + j], (cx, cy, c))
             for i, j, (cx, cy) in pairs]
    recvs = [_remote(refs[i].at[q, c], refs[i].at[2 * cx + cy, c], send.at[3 * i + j], recv.at[3 * i + j], (cx, cy, c))
             for i, j, (cx, cy) in pairs]
    return sends, recvs


def _build_swap(refs, send, recv):
    x, y, c, _ = _place()
    n = len(refs) // 2
    cps = [_remote(refs[i].at[:, 1 - c], refs[n + i], send.at[i], recv.at[i], (x, y, 1 - c)) for i in range(n)]
    return cps, cps


def _build_exchange(refs, send, recv):
    x, y, c, chips = _place()
    n = len(refs) // 2
    cps = [_remote(refs[i].at[2 * cx + cy], refs[n + i].at[j], send.at[3 * i + j], recv.at[3 * i + j], (cx, cy, c))
           for i in range(n) for j, (cx, cy) in enumerate(chips)]
    return cps, cps


def _swap_halves(grads):
    n = len(grads)

    def body(*refs):
        ins, outs = refs[:n], refs[n:2 * n]
        send, recv = refs[2 * n:]
        x, y, c, _ = _place()
        cps = [pltpu.make_async_remote_copy(src_ref=ins[i].at[:, 1 - c], dst_ref=outs[i], send_sem=send.at[i],
                                            recv_sem=recv.at[i], device_id=(x, y, 1 - c), device_id_type=MESH)
               for i in range(n)]
        for cp in cps:
            cp.start()
        for cp in cps:
            cp.wait()

    return pl.pallas_call(
        body, name="swap_halves",
        in_specs=[_ANY] * n, out_specs=[_ANY] * n,
        out_shape=[jax.ShapeDtypeStruct((N_SHARD,) + g.shape[2:], F32) for g in grads],
        scratch_shapes=[pltpu.SemaphoreType.DMA((n,)), pltpu.SemaphoreType.DMA((n,))],
    )(*grads)


def _join_halves(pairs):
    n = len(pairs)

    def body(*refs):
        outs = refs[n:2 * n]
        send, recv = refs[2 * n:]
        x, y, c, _ = _place()
        cps = [pltpu.make_async_remote_copy(src_ref=outs[i].at[c], dst_ref=outs[i].at[c], send_sem=send.at[i],
                                            recv_sem=recv.at[i], device_id=(x, y, 1 - c), device_id_type=MESH)
               for i in range(n)]
        for cp in cps:
            cp.start()
        for i in range(n):
            cps[i].wait_send()
            pltpu.make_async_remote_copy(src_ref=outs[i].at[c], dst_ref=outs[i].at[1 - c], send_sem=send.at[i],
                                         recv_sem=recv.at[i], device_id=(x, y, 1 - c), device_id_type=MESH).wait_recv()

    return pl.pallas_call(
        body, name="join_halves",
        in_specs=[_ANY] * n, out_specs=[_ANY] * n,
        out_shape=[jax.ShapeDtypeStruct(p.shape, F32) for p in pairs],
        input_output_aliases={i: i for i in range(n)},
        scratch_shapes=[pltpu.SemaphoreType.DMA((n,)), pltpu.SemaphoreType.DMA((n,))],
    )(*pairs)


def _gather_small(part, token):
    m_per, n = part.shape

    def body(x_ref, token_ref, out_ref, send_sems, recv_sems, local_sem):
        x, y, c, chips = _place()
        me, sibling = (x, y, c), (x, y, 1 - c)

        def rows(px, py, pc):
            return out_ref.at[pl.ds((4 * px + 2 * py + pc) * m_per, m_per), :]

        def copy(k, block, to, src=None):
            return pltpu.make_async_remote_copy(src_ref=rows(*block) if src is None else src, dst_ref=rows(*block),
                                                send_sem=send_sems.at[k], recv_sem=recv_sems.at[k],
                                                device_id=to, device_id_type=MESH)

        mine = pltpu.make_async_copy(x_ref, rows(*me), local_sem)
        mine.start()
        first = [copy(0, me, sibling, src=x_ref)]
        first += [copy(1 + j, me, (*chip, c), src=x_ref) for j, chip in enumerate(chips)]
        for cp in first:
            cp.start()
        passed = [copy(4 + j, (*chip, c), sibling) for j, chip in enumerate(chips)]
        for j, chip in enumerate(chips):
            copy(1 + j, (*chip, c), me).wait_recv()
            passed[j].start()
        copy(0, sibling, me).wait_recv()
        for j, chip in enumerate(chips):
            copy(4 + j, (*chip, 1 - c), me).wait_recv()
        for cp in first + passed:
            cp.wait_send()
        mine.wait()

    return pl.pallas_call(
        body, name="gather_small",
        out_shape=jax.ShapeDtypeStruct((8 * m_per, n), F32),
        in_specs=[pl.BlockSpec(memory_space=pltpu.VMEM), _ANY], out_specs=pl.BlockSpec(memory_space=pltpu.VMEM),
        scratch_shapes=[pltpu.SemaphoreType.DMA((7,)), pltpu.SemaphoreType.DMA((7,)), pltpu.SemaphoreType.DMA],
        compiler_params=_params(),
    )(part, token)


N_SPLIT = 2


def _sum_siblings(tag, grads, recvd, place):
    n = len(grads)

    def body(place_ref, *refs):
        g_refs, r_refs, sb_refs, own_refs = (refs[k * n:(k + 1) * n] for k in range(4))
        s = pl.program_id(1)
        for i in range(n):
            tot = g_refs[i][0, 0] + r_refs[i][0]
            sb_refs[i][0] = tot.astype(BF16)

            @pl.when(s == place_ref[0])
            def _():
                own_refs[i][...] = tot

    in_specs, sb_specs, own_specs, sb_shapes, own_shapes = [], [], [], [], []
    for g in grads:
        _, _, r, cdim = g.shape
        rb = r // N_SPLIT
        in_specs.append(pl.BlockSpec((1, 1, rb, cdim), lambda b, s, p: (s, p[1], b, 0)))
        sb_specs.append(pl.BlockSpec((1, rb, cdim), lambda b, s, p: (s, b, 0)))
        own_specs.append(pl.BlockSpec((rb, cdim), lambda b, s, p: (b, 0)))
        sb_shapes.append(jax.ShapeDtypeStruct((N_SHARD, r, cdim), BF16))
        own_shapes.append(jax.ShapeDtypeStruct((r, cdim), F32))
    out = pl.pallas_call(
        body, name="sum_siblings_" + tag,
        grid_spec=pltpu.PrefetchScalarGridSpec(
            num_scalar_prefetch=1, grid=(N_SPLIT, N_SHARD),
            in_specs=in_specs + sb_specs, out_specs=sb_specs + own_specs),
        out_shape=sb_shapes + own_shapes,
        compiler_params=_params(("parallel", "arbitrary")),
    )(place, *grads, *recvd)
    return out[:n], out[n:]


def _sum_chips(own, recvd, place):
    n = len(own)

    def body(place_ref, *refs):
        o_refs, r_refs, out_refs = (refs[k * n:(k + 1) * n] for k in range(3))
        for i in range(n):
            tot = o_refs[i][...]
            for j in range(3):
                tot = tot + r_refs[i][j].astype(F32)
            out_refs[i][0] = tot

    o_specs, r_specs, out_specs = [], [], []
    for o in own:
        r, cdim = o.shape
        rb = r // N_SPLIT
        o_specs.append(pl.BlockSpec((rb, cdim), lambda b, p: (b, 0)))
        r_specs.append(pl.BlockSpec((3, rb, cdim), lambda b, p: (0, b, 0)))
        out_specs.append(pl.BlockSpec((1, rb, cdim), lambda b, p: (p[1], b, 0)))
    return pl.pallas_call(
        body, name="sum_chips",
        grid_spec=pltpu.PrefetchScalarGridSpec(num_scalar_prefetch=1, grid=(N_SPLIT,),
                                               in_specs=o_specs + r_specs, out_specs=out_specs),
        out_shape=[jax.ShapeDtypeStruct((2,) + o.shape, F32) for o in own],
        compiler_params=_params(("parallel",)),
    )(place, *own, *recvd)


def _adamw_math(w, g, m, v):
    m = ADAM_B1 * m + (1.0 - ADAM_B1) * g
    v = ADAM_B2 * v + (1.0 - ADAM_B2) * (g * g)
    m_hat = m / (1.0 - ADAM_B1 ** ADAM_STEP)
    v_hat = v / (1.0 - ADAM_B2 ** ADAM_STEP)
    delta = -ADAM_LR * (m_hat / (jnp.sqrt(v_hat) + ADAM_EPS) + ADAM_WD * w)
    return delta, m, v


def _adamw(name, ws, gs, ms, vs, n_split):
    n = len(ws)

    def body(*refs):
        w_r, g_r, m_r, v_r, d_o, m_o, v_o = (refs[k * n:(k + 1) * n] for k in range(7))
        for i in range(n):
            d, m, v = _adamw_math(w_r[i][...], g_r[i][...], m_r[i][...], v_r[i][...])
            d_o[i][...] = d
            m_o[i][...] = m
            v_o[i][...] = v

    specs = [pl.BlockSpec((w.shape[0] // n_split, w.shape[1]), lambda b: (b, 0)) for w in ws]
    shapes = [jax.ShapeDtypeStruct(w.shape, F32) for w in ws]
    out = pl.pallas_call(
        body, name=name, grid=(n_split,),
        in_specs=specs * 4, out_specs=specs * 3, out_shape=shapes * 3,
        compiler_params=_params(("parallel",)),
    )(*ws, *gs, *ms, *vs)
    return out[:n], out[n:2 * n], out[2 * n:]


def _reduce_small(gathered, w, m, v):
    rows = w.shape[0]

    def body(ga_ref, w_ref, m_ref, v_ref, g_out, d_out, m_out, v_out):
        g = ga_ref[0:rows, :]
        for k in range(1, 8):
            g = g + ga_ref[k * rows:(k + 1) * rows, :]
        g_out[...] = g
        d, mm, vv = _adamw_math(w_ref[...], g, m_ref[...], v_ref[...])
        d_out[...] = d
        m_out[...] = mm
        v_out[...] = vv

    return pl.pallas_call(
        body, name="reduce_small",
        out_shape=[jax.ShapeDtypeStruct(w.shape, F32)] * 4,
        compiler_params=_params(),
    )(gathered, w, m, v)


def _s5_operands(lam_re, lam_im, log_step, b_re, b_im, c_re, c_im, glu_w):
    lr = jnp.minimum(lam_re, -1e-4)
    li = lam_im
    step = jnp.exp(log_step)[:, None]
    mag = jnp.exp(lr * step)
    ang = li * step
    abr = mag * jnp.cos(ang)
    abi = mag * jnp.sin(ang)
    nr = abr - 1.0
    ni = abi
    den = lr * lr + li * li
    cr = ((nr * lr + ni * li) / den)[..., None]
    ci = ((ni * lr - nr * li) / den)[..., None]
    bbr = cr * b_re - ci * b_im
    bbi = cr * b_im + ci * b_re
    eye = jnp.eye(8, dtype=F32)
    g, h, p = SSM_GROUPS // SUPER, SSM_GROUP, SSM_STATE

    def b_layout(t):
        return jnp.einsum("ab,japh->jahbp", eye, t.reshape(SUPER, g, p, h)).reshape(SUPER, g * h, g * p)

    def c_layout(t):
        return jnp.einsum("ab,jahp->jbpah", eye, t.reshape(SUPER, g, h, p)).reshape(SUPER, g * p, g * h)

    glu = jnp.einsum("ab,jahk->jahbk", eye, glu_w.reshape(SUPER, g, h, h)).reshape(SUPER, g * h, g * h)
    lam = _pad_rows(jnp.concatenate([abr.reshape(1, N_STATE), abi.reshape(1, N_STATE)], axis=0), 8)
    return lam, b_layout(bbr), b_layout(bbi), c_layout(c_re), c_layout(c_im), glu


def _pad_rows(a, rows):
    return jnp.pad(a, ((0, rows - a.shape[0]), (0, 0)))


def _pack(parts):
    rows = []
    for a in parts:
        flat = a.reshape(-1)
        n = -(-flat.shape[0] // 128)
        rows.append(jnp.pad(flat, (0, n * 128 - flat.shape[0])).reshape(n, 128))
    out = jnp.concatenate(rows, axis=0)
    return _pad_rows(out, -(-out.shape[0] // 8) * 8)


def _unpack(packed, like):
    out, at = [], 0
    for a in like:
        n = -(-a.size // 128)
        out.append(packed[at:at + n].reshape(-1)[:a.size].reshape(a.shape))
        at += n
    return out


SMALL = ("norm1_g", "ssm_lambda_re", "ssm_lambda_im", "ssm_log_step", "ssm_b_re", "ssm_b_im", "ssm_c_re", "ssm_c_im",
         "ssm_d", "ssm_glu_w", "ssm_glu_b", "ssm_norm_g", "pool_w", "pool_scale", "pool_norm_g", "norm2_g",
         "final_norm_g")
LARGE = ("w_in", "w_out", "w_gate", "w_up", "w_down")
WEIGHTS = ("meta_tokens", "norm1_g", "w_in", "ssm_lambda_re", "ssm_lambda_im", "ssm_log_step", "ssm_b_re", "ssm_b_im",
           "ssm_c_re", "ssm_c_im", "ssm_d", "ssm_glu_w", "ssm_glu_b", "ssm_norm_g", "pool_w", "pool_scale",
           "pool_norm_g", "w_out", "norm2_g", "w_gate", "w_up", "w_down", "final_norm_g")


def _step(x, target, w, m, v):
    seq = x.shape[1]
    n_rows = N_META + seq
    n_pad, tm, tc = _plan(n_rows)
    xq, yq, cq = lax.axis_index("x"), lax.axis_index("y"), lax.axis_index("c")
    place = jnp.stack([2 * xq + yq, cq]).astype(jnp.int32)

    def halves(a2d):
        return a2d.reshape(2, a2d.shape[0] // 2, a2d.shape[1])

    def local2d(t):
        return {"w_gate": lambda a: a[0].T, "w_up": lambda a: a[0].T}.get(t, lambda a: a[0])

    shards = [halves(local2d(k)(w[k])) for k in LARGE] + [halves(w["meta_tokens"])]
    full = _cast_shards(shards, [BF16] * len(LARGE) + [F32], place)
    w_in_full, meta_full = _gather_shards([full[0], full[5]])
    late, gather_sems, gather_token = _copies_start("gather_start", list(full[1:5]), (12,), _build_gather)
    w_in_b = w_in_full.reshape(D_MODEL, D_MODEL)
    meta = meta_full.reshape(N_SHARD, N_META, D_MODEL // N_SHARD).transpose(1, 0, 2).reshape(N_META, D_MODEL)

    h0 = _pad_rows(jnp.concatenate([meta, x[0]], axis=0), n_pad)
    tgt = _pad_rows(jnp.concatenate([jnp.zeros((N_META, D_MODEL), F32), target[0]], axis=0), n_pad)
    s5_in = (w["ssm_lambda_re"][0], w["ssm_lambda_im"][0], w["ssm_log_step"][0], w["ssm_b_re"][0], w["ssm_b_im"][0],
             w["ssm_c_re"][0], w["ssm_c_im"][0], w["ssm_glu_w"][0])
    (lam, bbr, bbi, crt, cit, glu), s5_vjp = jax.vjp(_s5_operands, *s5_in)
    bbr_b, bbi_b, crt_b, cit_b, glu_b16 = (t.astype(BF16) for t in (bbr, bbi, crt, cit, glu))
    s5_vecs = _pad_rows(jnp.concatenate([w["ssm_d"].reshape(1, D_SSM), w["ssm_glu_b"].reshape(1, D_SSM),
                                         w["ssm_norm_g"].reshape(1, D_SSM)], axis=0), 8)
    pool_vecs = _pad_rows(jnp.concatenate([w["pool_scale"].reshape(1, D_POOL), w["pool_norm_g"].reshape(1, D_POOL)],
                                          axis=0), 8)
    pw_b = w["pool_w"][0].astype(BF16)
    g1, g2, gf = w["norm1_g"].reshape(1, D_MODEL), w["norm2_g"].reshape(1, D_MODEL), w["final_norm_g"].reshape(1, D_MODEL)

    u, vv = _fwd_in(h0, g1, w_in_b, tm, gather_token)
    sr, si, y, ms = _s5_fwd(u, lam, bbr_b, bbi_b, crt_b, cit_b, s5_vecs, glu_b16, tc)
    feat, mp = _pool_fwd(vv, pw_b, pool_vecs, tc)
    late = _forward_halves(_copies_wait("gather_wait", late, gather_sems, mp, _build_gather))
    w_out_b = late[0].reshape(D_MODEL, D_MODEL)
    wg_b, wu_b, wd_b = (t.reshape(N_SHARD, FF_SHARD, D_MODEL) for t in late[1:])
    h1, n2, a, b, ff, dh2, loss_acc, dgf = _fwd_ffn(h0, ms, mp, w_out_b, g2, wg_b, wu_b, wd_b, gf, tgt, tm, n_rows)
    loss = lax.psum(loss_acc[0, 0], AXES)

    def quarters(t):
        if t.ndim == 2:
            t = t.reshape(N_SHARD, t.shape[0] // N_SHARD, t.shape[1])
        return t.reshape(N_SHARD, 2, t.shape[1] // 2, t.shape[2])

    def landing(like, lead, dtype):
        return [lax.empty((lead,) + t.shape[2:], dtype) for t in like]

    da, db, dh1, dg2 = _bwd_ffn(dh2, a, b, wg_b, wu_b, wd_b, h1, g2, tm)
    ffn_g = [quarters(t) for t in _grad_ffn(n2, da, db, ff, dh2, tm)]
    nf = len(ffn_g)
    moved, swap_sems, swap_token = _copies_start("swap_start", ffn_g + landing(ffn_g, N_SHARD, F32), (nf,), _build_swap)
    dms, dmp, dwo = _bwd_out(dh1, ms, mp, w_out_b, tm, swap_token)
    moved = _copies_wait("swap_wait", moved, swap_sems, dwo, _build_swap)
    ffn_parts, ffn_own = _sum_siblings("ffn", moved[:nf], moved[nf:], place)
    moved, exch_sems, exch_token = _copies_start("exchange_start", list(ffn_parts) + landing(ffn_g, 3, BF16), (3 * nf,),
                                                 _build_exchange)
    du, dbbr, dbbi, dcrt, dcit, dglu, ds5v, dlam = _s5_bwd(dms, y, u, sr, si, lam, bbr_b, bbi_b, crt_b, cit_b,
                                                           s5_vecs, glu_b16, tc, exch_token)
    dv, dpw, dpoolv = _pool_bwd(dmp, feat, pw_b, pool_vecs, tc)
    dh0, dwi, dg1 = _bwd_in(du, dv, h0, dh1, g1, w_in_b, tm)
    ffn_from_chips = _copies_wait("exchange_wait", moved, exch_sems, dh0, _build_exchange)[nf:]
    dlam = _pad_rows(jnp.concatenate([jnp.sum(dlam[:SEGMENTS], axis=0, keepdims=True),
                                      jnp.sum(dlam[SEGMENTS:], axis=0, keepdims=True)], axis=0), 8)
    d_lre, d_lim, d_lstep, d_bre, d_bim, d_cre, d_cim, d_gluw = s5_vjp((dlam, dbbr, dbbi, dcrt, dcit, dglu))
    grad_x = dh0[N_META:n_rows][None]

    mix_g = [quarters(t) for t in (dwi, dwo)]
    mix_parts, mix_own = _sum_siblings("mix", mix_g, _swap_halves(mix_g), place)
    moved, mix_sems, mix_token = _copies_start("mix_exchange_start", list(mix_parts) + landing(mix_g, 3, BF16),
                                               (3 * len(mix_g),), _build_exchange)

    small_g = {
        "norm1_g": dg1, "ssm_lambda_re": d_lre, "ssm_lambda_im": d_lim, "ssm_log_step": d_lstep, "ssm_b_re": d_bre,
        "ssm_b_im": d_bim, "ssm_c_re": d_cre, "ssm_c_im": d_cim, "ssm_d": ds5v[0], "ssm_glu_w": d_gluw,
        "ssm_glu_b": ds5v[1], "ssm_norm_g": ds5v[2], "pool_w": dpw, "pool_scale": dpoolv[0], "pool_norm_g": dpoolv[1],
        "norm2_g": dg2, "final_norm_g": dgf,
    }
    like = [w[k] for k in SMALL]
    packed_g = _pack([small_g[k].reshape(w[k].shape) for k in SMALL] + [dh0[:N_META]])
    rows = packed_g.shape[0]
    packed = lambda t: _pad_rows(_pack([t[k] for k in SMALL]), rows)
    gathered = _gather_small(packed_g, mix_token)
    g_pk, d_pk, m_pk, v_pk = _reduce_small(gathered, packed(w), packed(m), packed(v))

    mix_from_chips = _copies_wait("mix_exchange_wait", moved, mix_sems, g_pk, _build_exchange)[len(mix_g):]
    joined = _join_halves(_sum_chips(list(mix_own) + list(ffn_own), list(mix_from_chips) + list(ffn_from_chips), place))
    g_large = [j.reshape(j.shape[0] * j.shape[1], j.shape[2]) for j in joined]
    w2d, m2d, v2d = ([local2d(k)(t[k]) for k in LARGE] for t in (w, m, v))
    d_large, m_large, v_large = _adamw("adamw_large", w2d, g_large, m2d, v2d, 8)
    g_small = _unpack(g_pk, like + [jax.ShapeDtypeStruct((N_META, D_MODEL), F32)])
    d_small, m_small, v_small = (_unpack(t, like) for t in (d_pk, m_pk, v_pk))
    q = place[0]
    g_meta = lax.dynamic_slice_in_dim(g_small[-1], q * (D_MODEL // N_SHARD), D_MODEL // N_SHARD, axis=1)
    d_meta, m_meta, v_meta = _adamw("adamw_meta", [w["meta_tokens"]], [g_meta], [m["meta_tokens"]],
                                    [v["meta_tokens"]], 1)

    grads, deltas, new_m, new_v = {}, {}, {}, {}
    for i, k in enumerate(SMALL):
        grads[k], deltas[k], new_m[k], new_v[k] = g_small[i], d_small[i], m_small[i], v_small[i]
    for i, k in enumerate(LARGE):
        back = (lambda t: t.T[None]) if k in ("w_gate", "w_up") else (lambda t: t[None])
        grads[k], deltas[k], new_m[k], new_v[k] = (back(t) for t in (g_large[i], d_large[i], m_large[i], v_large[i]))
    grads["meta_tokens"], deltas["meta_tokens"] = g_meta, d_meta[0]
    new_m["meta_tokens"], new_v["meta_tokens"] = m_meta[0], v_meta[0]
    return (loss, grad_x, *[grads[k] for k in WEIGHTS], *[deltas[k] for k in WEIGHTS],
            *[new_m[k] for k in WEIGHTS], *[new_v[k] for k in WEIGHTS])


def kernel(x, meta_tokens, norm1_g, w_in, ssm_lambda_re, ssm_lambda_im, ssm_log_step, ssm_b_re, ssm_b_im, ssm_c_re, ssm_c_im, ssm_d, ssm_glu_w, ssm_glu_b, ssm_norm_g, pool_w, pool_scale, pool_norm_g, w_out, norm2_g, w_gate, w_up, w_down, final_norm_g, loss_target, m_meta_tokens, m_norm1_g, m_w_in, m_ssm_lambda_re, m_ssm_lambda_im, m_ssm_log_step, m_ssm_b_re, m_ssm_b_im, m_ssm_c_re, m_ssm_c_im, m_ssm_d, m_ssm_glu_w, m_ssm_glu_b, m_ssm_norm_g, m_pool_w, m_pool_scale, m_pool_norm_g, m_w_out, m_norm2_g, m_w_gate, m_w_up, m_w_down, m_final_norm_g, v_meta_tokens, v_norm1_g, v_w_in, v_ssm_lambda_re, v_ssm_lambda_im, v_ssm_log_step, v_ssm_b_re, v_ssm_b_im, v_ssm_c_re, v_ssm_c_im, v_ssm_d, v_ssm_glu_w, v_ssm_glu_b, v_ssm_norm_g, v_pool_w, v_pool_scale, v_pool_norm_g, v_w_out, v_norm2_g, v_w_gate, v_w_up, v_w_down, v_final_norm_g):
    w = dict(meta_tokens=meta_tokens, norm1_g=norm1_g, w_in=w_in, ssm_lambda_re=ssm_lambda_re, ssm_lambda_im=ssm_lambda_im, ssm_log_step=ssm_log_step, ssm_b_re=ssm_b_re, ssm_b_im=ssm_b_im, ssm_c_re=ssm_c_re, ssm_c_im=ssm_c_im, ssm_d=ssm_d, ssm_glu_w=ssm_glu_w, ssm_glu_b=ssm_glu_b, ssm_norm_g=ssm_norm_g, pool_w=pool_w, pool_scale=pool_scale, pool_norm_g=pool_norm_g, w_out=w_out, norm2_g=norm2_g, w_gate=w_gate, w_up=w_up, w_down=w_down, final_norm_g=final_norm_g)
    m = dict(meta_tokens=m_meta_tokens, norm1_g=m_norm1_g, w_in=m_w_in, ssm_lambda_re=m_ssm_lambda_re, ssm_lambda_im=m_ssm_lambda_im, ssm_log_step=m_ssm_log_step, ssm_b_re=m_ssm_b_re, ssm_b_im=m_ssm_b_im, ssm_c_re=m_ssm_c_re, ssm_c_im=m_ssm_c_im, ssm_d=m_ssm_d, ssm_glu_w=m_ssm_glu_w, ssm_glu_b=m_ssm_glu_b, ssm_norm_g=m_ssm_norm_g, pool_w=m_pool_w, pool_scale=m_pool_scale, pool_norm_g=m_pool_norm_g, w_out=m_w_out, norm2_g=m_norm2_g, w_gate=m_w_gate, w_up=m_w_up, w_down=m_w_down, final_norm_g=m_final_norm_g)
    v = dict(meta_tokens=v_meta_tokens, norm1_g=v_norm1_g, w_in=v_w_in, ssm_lambda_re=v_ssm_lambda_re, ssm_lambda_im=v_ssm_lambda_im, ssm_log_step=v_ssm_log_step, ssm_b_re=v_ssm_b_re, ssm_b_im=v_ssm_b_im, ssm_c_re=v_ssm_c_re, ssm_c_im=v_ssm_c_im, ssm_d=v_ssm_d, ssm_glu_w=v_ssm_glu_w, ssm_glu_b=v_ssm_glu_b, ssm_norm_g=v_ssm_norm_g, pool_w=v_pool_w, pool_scale=v_pool_scale, pool_norm_g=v_pool_norm_g, w_out=v_w_out, norm2_g=v_norm2_g, w_gate=v_w_gate, w_up=v_w_up, w_down=v_w_down, final_norm_g=v_final_norm_g)
    return _step(x, loss_target, w, m, v)
```

```python
import functools
import math

import jax
import jax.numpy as jnp
from jax import lax
from jax.experimental import pallas as pl
from jax.experimental.pallas import tpu as pltpu

F32 = jnp.float32
BF16 = jnp.bfloat16
MESH = pl.DeviceIdType.MESH
AXES = ("x", "y", "c")

D_MODEL = 1024
D_SSM = 512
D_POOL = 512
N_META = 16
SSM_GROUP = 16
SSM_GROUPS = 32
SSM_STATE = 64
N_STATE = SSM_GROUPS * SSM_STATE
STATE_BLOCKS = N_STATE // 128
SUPER = 4
POOL_WINDOWS = (2, 4, 8, 16)
POOL_HALO = 16
D_FF = 2816
N_SHARD = 4
FF_SHARD = D_FF // N_SHARD
EPS = 1e-6
ADAM_LR, ADAM_B1, ADAM_B2, ADAM_EPS, ADAM_WD, ADAM_STEP = 0.001, 0.9, 0.999, 1e-08, 0.01, 10
VMEM_LIMIT = 56 * 1024 * 1024


def _plan(n_rows):
    if n_rows > 2048:
        tm, tc = 416, 320
    else:
        tm, tc = 128, 64
    step = math.lcm(tm, tc)
    return -(-n_rows // step) * step, tm, tc


def _params(sem=None):
    return pltpu.CompilerParams(dimension_semantics=sem, vmem_limit_bytes=VMEM_LIMIT)


def _dot(a, b):
    return jnp.dot(a, b, preferred_element_type=F32)


def _dot_nt(a, b):
    return lax.dot_general(a, b, (((1,), (1,)), ((), ())), preferred_element_type=F32)


def _dot_tn(a, b):
    return lax.dot_general(a, b, (((0,), (0,)), ((), ())), preferred_element_type=F32)


def _sigmoid(x):
    return 1.0 / (1.0 + jnp.exp(-x))


_GELU_C = math.sqrt(2.0 / math.pi)


def _gelu_and_grad(y):
    y2 = y * y
    t = jnp.tanh(_GELU_C * (y + 0.044715 * y * y2))
    g = 0.5 * y * (1.0 + t)
    dg = 0.5 * (1.0 + t) + 0.5 * y * (1.0 - t * t) * (_GELU_C * (1.0 + 3.0 * 0.044715 * y2))
    return g, dg


def _rms(x):
    return lax.rsqrt(jnp.mean(x * x, axis=-1, keepdims=True) + EPS)


def _rms_bwd(dn, xhat, r):
    return r * (dn - xhat * jnp.mean(dn * xhat, axis=-1, keepdims=True))


def _full(shape):
    nd = len(shape)
    return pl.BlockSpec(shape, lambda *_: (0,) * nd)


def _fwd_in(h0, g1, w_in_b, tm, token):
    n_pad = h0.shape[0]

    def body(h_ref, g_ref, w_ref, token_ref, u_ref, v_ref):
        h = h_ref[...]
        n1 = (h * _rms(h) * g_ref[...]).astype(BF16)
        proj = _dot(n1, w_ref[...])
        for i in range(4):
            u_ref[i] = proj[:, 128 * i:128 * (i + 1)]
        v_ref[...] = proj[:, D_SSM:]

    row = lambda w: pl.BlockSpec((tm, w), lambda i: (i, 0))
    return pl.pallas_call(
        body, grid=(n_pad // tm,), name="fwd_in",
        in_specs=[row(D_MODEL), _full((1, D_MODEL)), _full((D_MODEL, D_MODEL)), _ANY],
        out_specs=[pl.BlockSpec((4, tm, 128), lambda i: (0, i, 0)), row(D_POOL)],
        out_shape=[jax.ShapeDtypeStruct((4, n_pad, 128), F32), jax.ShapeDtypeStruct((n_pad, D_POOL), F32)],
        compiler_params=_params(("parallel",)),
    )(h0, g1, w_in_b, token)


def _fwd_ffn(h0, ms, mp, w_out_b, g2, wg_b, wu_b, wd_b, gf, target, tm, n_valid):
    n_pad = h0.shape[0]
    nt = n_pad // tm

    def body(h0_ref, ms_ref, mp_ref, wo_ref, g2_ref, wg_ref, wu_ref, wd_ref, gf_ref, tgt_ref,
             h1_ref, n2_ref, a_ref, b_ref, ff_ref, dh2_ref, loss_ref, dgf_ref, acc):
        i, q = pl.program_id(0), pl.program_id(1)

        @pl.when((i == 0) & (q == 0))
        def _():
            loss_ref[...] = jnp.zeros_like(loss_ref)
            dgf_ref[...] = jnp.zeros_like(dgf_ref)

        @pl.when(q == 0)
        def _():
            h1 = h0_ref[...] + _dot(ms_ref[...], wo_ref[:D_SSM, :]) + _dot(mp_ref[...], wo_ref[D_SSM:, :])
            h1_ref[...] = h1
            acc[...] = h1
            n2_ref[...] = (h1 * _rms(h1) * g2_ref[...]).astype(BF16)

        n2 = n2_ref[...]
        a = _dot_nt(n2, wg_ref[0])
        b = _dot_nt(n2, wu_ref[0])
        a_ref[0] = a
        b_ref[0] = b
        ff = (a * _sigmoid(a) * b).astype(BF16)
        ff_ref[0] = ff
        acc[...] += _dot(ff, wd_ref[0])

        @pl.when(q == N_SHARD - 1)
        def _():
            h2 = acc[...]
            r = _rms(h2)
            xhat = h2 * r
            gf_row = gf_ref[...]
            rows = i * tm + lax.broadcasted_iota(jnp.int32, (tm, 1), 0)
            valid = (rows >= N_META) & (rows < n_valid)
            diff = jnp.where(valid, xhat * gf_row - tgt_ref[...], 0.0)
            loss_ref[...] += jnp.full(loss_ref.shape, 0.5 / D_MODEL, F32) * jnp.sum(diff * diff)
            dout = diff * (1.0 / D_MODEL)
            dgf_ref[...] += jnp.sum(dout * xhat, axis=0, keepdims=True)
            dh2_ref[...] = _rms_bwd(dout * gf_row, xhat, r)

    row = lambda w: pl.BlockSpec((tm, w), lambda i, q: (i, 0))
    shard_rows = pl.BlockSpec((1, FF_SHARD, D_MODEL), lambda i, q: (q, 0, 0))
    act = pl.BlockSpec((1, tm, FF_SHARD), lambda i, q: (q, i, 0))
    sds = jax.ShapeDtypeStruct
    return pl.pallas_call(
        body, grid=(nt, N_SHARD), name="fwd_ffn",
        in_specs=[row(D_MODEL), row(D_SSM), row(D_POOL), _full((D_MODEL, D_MODEL)), _full((1, D_MODEL)),
                  shard_rows, shard_rows, shard_rows, _full((1, D_MODEL)), row(D_MODEL)],
        out_specs=[row(D_MODEL), row(D_MODEL), act, act, act, row(D_MODEL), _full((8, 128)), _full((1, D_MODEL))],
        out_shape=[sds((n_pad, D_MODEL), F32), sds((n_pad, D_MODEL), BF16),
                   sds((N_SHARD, n_pad, FF_SHARD), F32), sds((N_SHARD, n_pad, FF_SHARD), F32),
                   sds((N_SHARD, n_pad, FF_SHARD), BF16), sds((n_pad, D_MODEL), F32),
                   sds((8, 128), F32), sds((1, D_MODEL), F32)],
        scratch_shapes=[pltpu.VMEM((tm, D_MODEL), F32)],
        compiler_params=_params(("arbitrary", "arbitrary")),
    )(h0, ms, mp, w_out_b, g2, wg_b, wu_b, wd_b, gf, target)


def _bwd_ffn(dh2, a, b, wg_b, wu_b, wd_b, h1, g2, tm):
    n_pad = dh2.shape[0]

    def body(dh2_ref, a_ref, b_ref, wg_ref, wu_ref, wd_ref, h1_ref, g2_ref, da_ref, db_ref, dh1_ref, dg2_ref, acc):
        i, q = pl.program_id(0), pl.program_id(1)

        @pl.when((i == 0) & (q == 0))
        def _():
            dg2_ref[...] = jnp.zeros_like(dg2_ref)

        dff = _dot_nt(dh2_ref[...].astype(BF16), wd_ref[0])
        a_v, b_v = a_ref[0], b_ref[0]
        sig = _sigmoid(a_v)
        da = (dff * b_v * sig * (1.0 + a_v * (1.0 - sig))).astype(BF16)
        db = (dff * a_v * sig).astype(BF16)
        da_ref[0] = da
        db_ref[0] = db
        part = _dot(da, wg_ref[0]) + _dot(db, wu_ref[0])

        @pl.when(q == 0)
        def _():
            acc[...] = part

        @pl.when(q > 0)
        def _():
            acc[...] += part

        @pl.when(q == N_SHARD - 1)
        def _():
            h1 = h1_ref[...]
            r = _rms(h1)
            xhat = h1 * r
            dn2 = acc[...]
            dg2_ref[...] += jnp.sum(dn2 * xhat, axis=0, keepdims=True)
            dh1_ref[...] = dh2_ref[...] + _rms_bwd(dn2 * g2_ref[...], xhat, r)

    row = lambda w: pl.BlockSpec((tm, w), lambda i, q: (i, 0))
    shard_rows = pl.BlockSpec((1, FF_SHARD, D_MODEL), lambda i, q: (q, 0, 0))
    act = pl.BlockSpec((1, tm, FF_SHARD), lambda i, q: (q, i, 0))
    sds = jax.ShapeDtypeStruct
    return pl.pallas_call(
        body, grid=(n_pad // tm, N_SHARD), name="bwd_ffn",
        in_specs=[row(D_MODEL), act, act, shard_rows, shard_rows, shard_rows, row(D_MODEL), _full((1, D_MODEL))],
        out_specs=[act, act, row(D_MODEL), _full((1, D_MODEL))],
        out_shape=[sds((N_SHARD, n_pad, FF_SHARD), BF16), sds((N_SHARD, n_pad, FF_SHARD), BF16),
                   sds((n_pad, D_MODEL), F32), sds((1, D_MODEL), F32)],
        scratch_shapes=[pltpu.VMEM((tm, D_MODEL), F32)],
        compiler_params=_params(("arbitrary", "arbitrary")),
    )(dh2, a, b, wg_b, wu_b, wd_b, h1, g2)


def _grad_ffn(n2, da, db, ff, dh2, tm):
    n_pad = n2.shape[0]

    def body(n2_ref, da_ref, db_ref, ff_ref, dh2_ref, dwg_ref, dwu_ref, dwd_ref):
        i = pl.program_id(1)
        n2_v = n2_ref[...]
        gg = _dot_tn(da_ref[0], n2_v)
        gu = _dot_tn(db_ref[0], n2_v)
        gd = _dot_tn(ff_ref[0], dh2_ref[...].astype(BF16))

        @pl.when(i == 0)
        def _():
            dwg_ref[0] = gg
            dwu_ref[0] = gu
            dwd_ref[0] = gd

        @pl.when(i > 0)
        def _():
            dwg_ref[0] += gg
            dwu_ref[0] += gu
            dwd_ref[0] += gd

    row = lambda w: pl.BlockSpec((tm, w), lambda q, i: (i, 0))
    act = pl.BlockSpec((1, tm, FF_SHARD), lambda q, i: (q, i, 0))
    sds = jax.ShapeDtypeStruct
    return pl.pallas_call(
        body, grid=(N_SHARD, n_pad // tm), name="grad_ffn",
        in_specs=[row(D_MODEL), act, act, act, row(D_MODEL)],
        out_specs=[pl.BlockSpec((1, FF_SHARD, D_MODEL), lambda q, i: (q, 0, 0))] * 3,
        out_shape=[sds((N_SHARD, FF_SHARD, D_MODEL), F32)] * 3,
        compiler_params=_params(("parallel", "arbitrary")),
    )(n2, da, db, ff, dh2)


def _bwd_out(dh1, ms, mp, w_out_b, tm, token):
    n_pad = dh1.shape[0]

    def body(dh1_ref, ms_ref, mp_ref, wo_ref, token_ref, dms_ref, dmp_ref, dwo_ref):
        i = pl.program_id(0)

        @pl.when(i == 0)
        def _():
            dwo_ref[...] = jnp.zeros_like(dwo_ref)

        d = dh1_ref[...].astype(BF16)
        dms = _dot_nt(d, wo_ref[:D_SSM, :])
        for k in range(4):
            dms_ref[k] = dms[:, 128 * k:128 * (k + 1)]
        dmp_ref[...] = _dot_nt(d, wo_ref[D_SSM:, :])
        dwo_ref[:D_SSM, :] += _dot_tn(ms_ref[...], d)
        dwo_ref[D_SSM:, :] += _dot_tn(mp_ref[...], d)

    row = lambda w: pl.BlockSpec((tm, w), lambda i: (i, 0))
    sds = jax.ShapeDtypeStruct
    return pl.pallas_call(
        body, grid=(n_pad // tm,), name="bwd_out",
        in_specs=[row(D_MODEL), row(D_SSM), row(D_POOL), _full((D_MODEL, D_MODEL)), _ANY],
        out_specs=[pl.BlockSpec((4, tm, 128), lambda i: (0, i, 0)), row(D_POOL), _full((D_MODEL, D_MODEL))],
        out_shape=[sds((4, n_pad, 128), F32), sds((n_pad, D_POOL), F32), sds((D_MODEL, D_MODEL), F32)],
        compiler_params=_params(("arbitrary",)),
    )(dh1, ms, mp, w_out_b, token)


def _bwd_in(du, dv, h0, dh1, g1, w_in_b, tm):
    n_pad = h0.shape[0]

    def body(du_ref, dv_ref, h0_ref, dh1_ref, g1_ref, w_ref, dh0_ref, dwi_ref, dg1_ref):
        i = pl.program_id(0)

        @pl.when(i == 0)
        def _():
            dwi_ref[...] = jnp.zeros_like(dwi_ref)
            dg1_ref[...] = jnp.zeros_like(dg1_ref)

        dub = du_ref[...].astype(BF16)
        dvb = dv_ref[...].astype(BF16)
        dn1 = _dot_nt(dub, w_ref[:, :D_SSM]) + _dot_nt(dvb, w_ref[:, D_SSM:])
        h = h0_ref[...]
        r = _rms(h)
        xhat = h * r
        g_row = g1_ref[...]
        n1 = (xhat * g_row).astype(BF16)
        dwi_ref[:, :D_SSM] += _dot_tn(n1, dub)
        dwi_ref[:, D_SSM:] += _dot_tn(n1, dvb)
        dg1_ref[...] += jnp.sum(dn1 * xhat, axis=0, keepdims=True)
        dh0_ref[...] = dh1_ref[...] + _rms_bwd(dn1 * g_row, xhat, r)

    row = lambda w: pl.BlockSpec((tm, w), lambda i: (i, 0))
    sds = jax.ShapeDtypeStruct
    return pl.pallas_call(
        body, grid=(n_pad // tm,), name="bwd_in",
        in_specs=[row(D_SSM), row(D_POOL), row(D_MODEL), row(D_MODEL), _full((1, D_MODEL)), _full((D_MODEL, D_MODEL))],
        out_specs=[row(D_MODEL), _full((D_MODEL, D_MODEL)), _full((1, D_MODEL))],
        out_shape=[sds((n_pad, D_MODEL), F32), sds((D_MODEL, D_MODEL), F32), sds((1, D_MODEL), F32)],
        compiler_params=_params(("arbitrary",)),
    )(du, dv, h0, dh1, g1, w_in_b)


SEGMENTS = 8


def _interleaved(ref, seg):
    return jnp.concatenate(
        [jnp.concatenate([ref[i, pl.ds(j, SEGMENTS, stride=seg), :] for i in range(4)], axis=1) for j in range(seg)],
        axis=0)


def _time_order(scratch, val, seg):
    for i in range(4):
        scratch[i] = val[:, 128 * i:128 * (i + 1)]
    tiles = []
    for m in range(val.shape[0] // 8):
        s, j0 = divmod(8 * m, seg)
        tiles.append(jnp.concatenate(
            [scratch[i, pl.ds(8 * j0 + s, 8, stride=SEGMENTS), :] for i in range(4)], axis=1))
    return jnp.concatenate(tiles, axis=0)


def _power_table(lam_ref, pw_r, pw_i, seg):
    a_r = jnp.broadcast_to(lam_ref[0:1, :], (SEGMENTS, N_STATE))
    a_i = jnp.broadcast_to(lam_ref[1:2, :], (SEGMENTS, N_STATE))
    p_r, p_i = a_r, a_i
    for k in range(seg):
        pw_r[SEGMENTS * k:SEGMENTS * (k + 1), :] = p_r
        pw_i[SEGMENTS * k:SEGMENTS * (k + 1), :] = p_i
        p_r, p_i = p_r * a_r - p_i * a_i, p_r * a_i + p_i * a_r


def _segment_scan(xr_ref, xi_ref, cols, pw_r, pw_i, hr_s, hi_s, seg, reverse):
    sign = -1.0 if reverse else 1.0
    a_r, a_i = pw_r[0:SEGMENTS, cols], sign * pw_i[0:SEGMENTS, cols]

    def step(n, carry):
        hr, hi = carry
        o = pl.multiple_of((seg - 1 - n if reverse else n) * SEGMENTS, SEGMENTS)
        nr = a_r * hr - a_i * hi + xr_ref[pl.ds(o, SEGMENTS), cols]
        ni = a_r * hi + a_i * hr + xi_ref[pl.ds(o, SEGMENTS), cols]
        xr_ref[pl.ds(o, SEGMENTS), cols] = nr
        xi_ref[pl.ds(o, SEGMENTS), cols] = ni
        return nr, ni

    zero = jnp.zeros((SEGMENTS, cols.stop - cols.start), F32)
    e_r, e_i = lax.fori_loop(0, seg, step, (zero, zero), unroll=2)

    top = SEGMENTS * (seg - 1)
    ls_r, ls_i = pw_r[top:top + 1, cols], sign * pw_i[top:top + 1, cols]
    c_r, c_i = hr_s[0:1, cols], hi_s[0:1, cols]
    in_r, in_i = [None] * SEGMENTS, [None] * SEGMENTS
    for s in (range(SEGMENTS - 1, -1, -1) if reverse else range(SEGMENTS)):
        in_r[s], in_i[s] = c_r, c_i
        c_r, c_i = (e_r[s:s + 1, :] + ls_r * c_r - ls_i * c_i, e_i[s:s + 1, :] + ls_r * c_i + ls_i * c_r)
    hr_s[0:1, cols] = c_r
    hi_s[0:1, cols] = c_i
    cm_r, cm_i = jnp.concatenate(in_r, axis=0), jnp.concatenate(in_i, axis=0)

    def fix(jj, _):
        o = pl.multiple_of(jj * SEGMENTS, SEGMENTS)
        k = pl.multiple_of((seg - 1 - jj if reverse else jj) * SEGMENTS, SEGMENTS)
        p_r, p_i = pw_r[pl.ds(k, SEGMENTS), cols], sign * pw_i[pl.ds(k, SEGMENTS), cols]
        xr_ref[pl.ds(o, SEGMENTS), cols] += p_r * cm_r - p_i * cm_i
        xi_ref[pl.ds(o, SEGMENTS), cols] += p_r * cm_i + p_i * cm_r
        return 0

    lax.fori_loop(0, seg, fix, 0, unroll=2)


def _s5_tail(y, glu_ref, glub):
    g, dgelu = _gelu_and_grad(y)
    gb = g.astype(BF16)
    gate = jnp.concatenate([_dot(gb[:, 128 * j:128 * (j + 1)], glu_ref[j]) for j in range(SUPER)], axis=1) + glub
    sig = _sigmoid(gate)
    return g, gb, dgelu, sig, g * sig


def _s5_fwd(u4, lam, bbr, bbi, crt, cit, vecs, glu, tc):
    n_pad = u4.shape[1]
    seg = tc // SEGMENTS

    def body(u_ref, lam_ref, bbr_ref, bbi_ref, crt_ref, cit_ref, vec_ref, glu_ref,
             sr_ref, si_ref, y_ref, ms_ref, hr_s, hi_s, pw_r, pw_i, lanes):
        @pl.when(pl.program_id(0) == 0)
        def _():
            hr_s[...] = jnp.zeros_like(hr_s)
            hi_s[...] = jnp.zeros_like(hi_s)
            _power_table(lam_ref, pw_r, pw_i, seg)

        u_v = _interleaved(u_ref, seg)
        ub = u_v.astype(BF16)
        for j in range(SUPER):
            uj = ub[:, 128 * j:128 * (j + 1)]
            sr_ref[:, 512 * j:512 * (j + 1)] = _dot(uj, bbr_ref[j])
            si_ref[:, 512 * j:512 * (j + 1)] = _dot(uj, bbi_ref[j])
        for j in range(SUPER):
            _segment_scan(sr_ref, si_ref, slice(512 * j, 512 * (j + 1)), pw_r, pw_i, hr_s, hi_s, seg, False)

        d_row, glub, gs = vec_ref[0:1, :], vec_ref[1:2, :], vec_ref[2:3, :]
        ys_c = []
        for j in range(SUPER):
            sr_j = sr_ref[:, 512 * j:512 * (j + 1)].astype(BF16)
            si_j = si_ref[:, 512 * j:512 * (j + 1)].astype(BF16)
            ys_c.append(_dot(sr_j, crt_ref[j]) - _dot(si_j, cit_ref[j]))
        y = jnp.concatenate(ys_c, axis=1) + d_row * u_v
        y_ref[...] = y
        _, _, _, _, ys = _s5_tail(y, glu_ref, glub)
        ms_ref[...] = _time_order(lanes, ys * _rms(ys) * gs, seg).astype(BF16)

    chunk = lambda w: pl.BlockSpec((tc, w), lambda c: (c, 0))
    lane_blocks = pl.BlockSpec((4, tc, 128), lambda c: (0, c, 0))
    sds = jax.ShapeDtypeStruct
    return pl.pallas_call(
        body, grid=(n_pad // tc,), name="s5_fwd",
        in_specs=[lane_blocks, _full((8, N_STATE)), _full((SUPER, 128, 512)), _full((SUPER, 128, 512)),
                  _full((SUPER, 512, 128)), _full((SUPER, 512, 128)), _full((8, D_SSM)), _full((SUPER, 128, 128))],
        out_specs=[chunk(N_STATE), chunk(N_STATE), chunk(D_SSM), chunk(D_SSM)],
        out_shape=[sds((n_pad, N_STATE), F32), sds((n_pad, N_STATE), F32),
                   sds((n_pad, D_SSM), F32), sds((n_pad, D_SSM), BF16)],
        scratch_shapes=[pltpu.VMEM((8, N_STATE), F32), pltpu.VMEM((8, N_STATE), F32),
                        pltpu.VMEM((tc, N_STATE), F32), pltpu.VMEM((tc, N_STATE), F32),
                        pltpu.VMEM((4, tc, 128), F32)],
        compiler_params=_params(("arbitrary",)),
    )(u4, lam, bbr, bbi, crt, cit, vecs, glu)


def _s5_bwd(dms4, y, u4, sr, si, lam, bbr, bbi, crt, cit, vecs, glu, tc, token):
    n_pad = u4.shape[1]
    nc = n_pad // tc
    seg = tc // SEGMENTS

    def body(dms_ref, y_ref, u_ref, sr_ref, si_ref, pr_ref, pi_ref, lam_ref, bbr_ref, bbi_ref, crt_ref, cit_ref,
             vec_ref, glu_ref, token_ref, du_ref, dbbr_ref, dbbi_ref, dcrt_ref, dcit_ref, dglu_ref, dvec_ref, dlam_ref,
             qr_s, qi_s, cr_s, ci_s, pw_r, pw_i, lanes):
        c = pl.program_id(0)

        @pl.when(c == 0)
        def _():
            for ref in (dbbr_ref, dbbi_ref, dcrt_ref, dcit_ref, dglu_ref, dvec_ref, dlam_ref, cr_s, ci_s):
                ref[...] = jnp.zeros_like(ref)
            _power_table(lam_ref, pw_r, pw_i, seg)

        d_row, glub, gs = vec_ref[0:1, :], vec_ref[1:2, :], vec_ref[2:3, :]
        y_v, u_v = y_ref[...], _interleaved(u_ref, seg)
        ub = u_v.astype(BF16)
        g, gb, dgelu, sig, ys = _s5_tail(y_v, glu_ref, glub)
        r = _rms(ys)
        xhat = ys * r
        dm = _interleaved(dms_ref, seg)
        dys = _rms_bwd(dm * gs, xhat, r)
        dgate = dys * g * sig * (1.0 - sig)
        dgateb = dgate.astype(BF16)
        dg = dys * sig + jnp.concatenate(
            [_dot_nt(dgateb[:, 128 * j:128 * (j + 1)], glu_ref[j]) for j in range(SUPER)], axis=1)
        dy = dg * dgelu
        dyb = dy.astype(BF16)
        dvec_ref[0:1, :] += jnp.sum(dy * u_v, axis=0, keepdims=True)
        dvec_ref[1:2, :] += jnp.sum(dgate, axis=0, keepdims=True)
        dvec_ref[2:3, :] += jnp.sum(dm * xhat, axis=0, keepdims=True)

        for j in range(SUPER):
            cols, states = slice(128 * j, 128 * (j + 1)), slice(512 * j, 512 * (j + 1))
            dglu_ref[j] += _dot_tn(gb[:, cols], dgateb[:, cols])
            dcrt_ref[j] += _dot_tn(sr_ref[:, states].astype(BF16), dyb[:, cols])
            dcit_ref[j] -= _dot_tn(si_ref[:, states].astype(BF16), dyb[:, cols])
            qr_s[:, states] = _dot_nt(dyb[:, cols], crt_ref[j])
            qi_s[:, states] = -_dot_nt(dyb[:, cols], cit_ref[j])

        first = c == nc - 1
        row0 = lax.broadcasted_iota(jnp.int32, (SEGMENTS, 1), 0) == 0
        last = (seg - 1) * SEGMENTS
        for j in range(SUPER):
            states = slice(512 * j, 512 * (j + 1))
            _segment_scan(qr_s, qi_s, states, pw_r, pw_i, cr_s, ci_s, seg, True)

            before_r = jnp.where(first, 0.0, pltpu.roll(pr_ref[:, states], 1, 0))
            before_i = jnp.where(first, 0.0, pltpu.roll(pi_ref[:, states], 1, 0))
            hp_r = jnp.where(row0, before_r, pltpu.roll(sr_ref[pl.ds(last, SEGMENTS), states], 1, 0))
            hp_i = jnp.where(row0, before_i, pltpu.roll(si_ref[pl.ds(last, SEGMENTS), states], 1, 0))
            q_r, q_i = qr_s[pl.ds(0, SEGMENTS), states], qi_s[pl.ds(0, SEGMENTS), states]

            def dlam_step(jj, acc):
                o = pl.multiple_of(jj * SEGMENTS, SEGMENTS)
                above = pl.multiple_of((jj - 1) * SEGMENTS, SEGMENTS)
                h_r, h_i = sr_ref[pl.ds(above, SEGMENTS), states], si_ref[pl.ds(above, SEGMENTS), states]
                t_r, t_i = qr_s[pl.ds(o, SEGMENTS), states], qi_s[pl.ds(o, SEGMENTS), states]
                return acc[0] + t_r * h_r + t_i * h_i, acc[1] + t_i * h_r - t_r * h_i

            acc = lax.fori_loop(1, seg, dlam_step, (q_r * hp_r + q_i * hp_i, q_i * hp_r - q_r * hp_i), unroll=2)
            dlam_ref[0:SEGMENTS, states] += acc[0]
            dlam_ref[SEGMENTS:, states] += acc[1]

        du_c = []
        for j in range(SUPER):
            cols, states = slice(128 * j, 128 * (j + 1)), slice(512 * j, 512 * (j + 1))
            qr_j = qr_s[:, states].astype(BF16)
            qi_j = qi_s[:, states].astype(BF16)
            du_c.append(_dot_nt(qr_j, bbr_ref[j]) + _dot_nt(qi_j, bbi_ref[j]))
            dbbr_ref[j] += _dot_tn(ub[:, cols], qr_j)
            dbbi_ref[j] += _dot_tn(ub[:, cols], qi_j)
        du_ref[...] = _time_order(lanes, jnp.concatenate(du_c, axis=1) + dy * d_row, seg)

    rev = lambda c: nc - 1 - c
    chunk = lambda w: pl.BlockSpec((tc, w), lambda c: (rev(c), 0))
    lane_blocks = pl.BlockSpec((4, tc, 128), lambda c: (0, rev(c), 0))
    prev = pl.BlockSpec((SEGMENTS, N_STATE), lambda c: (jnp.maximum(rev(c) * seg - 1, 0), 0))
    sds = jax.ShapeDtypeStruct
    return pl.pallas_call(
        body, grid=(nc,), name="s5_bwd",
        in_specs=[lane_blocks, chunk(D_SSM), lane_blocks, chunk(N_STATE), chunk(N_STATE), prev, prev,
                  _full((8, N_STATE)), _full((SUPER, 128, 512)), _full((SUPER, 128, 512)),
                  _full((SUPER, 512, 128)), _full((SUPER, 512, 128)), _full((8, D_SSM)), _full((SUPER, 128, 128)), _ANY],
        out_specs=[chunk(D_SSM), _full((SUPER, 128, 512)), _full((SUPER, 128, 512)), _full((SUPER, 512, 128)),
                   _full((SUPER, 512, 128)), _full((SUPER, 128, 128)), _full((8, D_SSM)), _full((2 * SEGMENTS, N_STATE))],
        out_shape=[sds((n_pad, D_SSM), F32), sds((SUPER, 128, 512), F32), sds((SUPER, 128, 512), F32),
                   sds((SUPER, 512, 128), F32), sds((SUPER, 512, 128), F32), sds((SUPER, 128, 128), F32),
                   sds((8, D_SSM), F32), sds((2 * SEGMENTS, N_STATE), F32)],
        scratch_shapes=[pltpu.VMEM((tc, N_STATE), F32), pltpu.VMEM((tc, N_STATE), F32),
                        pltpu.VMEM((8, N_STATE), F32), pltpu.VMEM((8, N_STATE), F32),
                        pltpu.VMEM((tc, N_STATE), F32), pltpu.VMEM((tc, N_STATE), F32),
                        pltpu.VMEM((4, tc, 128), F32)],
        compiler_params=_params(("arbitrary",)),
    )(dms4, y, u4, sr, si, sr, si, lam, bbr, bbi, crt, cit, vecs, glu, token)


def _inv_count(c_idx, tc, w):
    t = c_idx * tc + lax.broadcasted_iota(jnp.int32, (tc, 1), 0)
    return 1.0 / jnp.minimum(t + 1, w).astype(F32)


def _pool_fwd(v, pw_b, vecs, tc):
    n_pad = v.shape[0]

    def body(v_ref, pw_ref, vec_ref, feat_ref, mp_ref, hist):
        c = pl.program_id(0)

        @pl.when(c == 0)
        def _():
            hist[...] = jnp.zeros_like(hist)

        v_v = v_ref[...]
        ext = jnp.concatenate([hist[...], v_v], axis=0)
        hist[...] = v_v[tc - POOL_HALO:, :]
        feats, ps = [], []
        for k, w in enumerate(POOL_WINDOWS):
            cols = slice(128 * k, 128 * (k + 1))
            s = ext[:, cols]
            sh = 1
            while sh < w:
                s = s + pltpu.roll(s, sh, 0)
                sh *= 2
            f = (s[POOL_HALO:, :] * _inv_count(c, tc, w) - v_v[:, cols]).astype(BF16)
            feats.append(f)
            ps.append(_dot(f, pw_ref[k]))
        feat_ref[...] = jnp.concatenate(feats, axis=1)
        yp = jnp.concatenate(ps, axis=1) * vec_ref[0:1, :]
        mp_ref[...] = (yp * _rms(yp) * vec_ref[1:2, :]).astype(BF16)

    chunk = lambda w: pl.BlockSpec((tc, w), lambda c: (c, 0))
    sds = jax.ShapeDtypeStruct
    return pl.pallas_call(
        body, grid=(n_pad // tc,), name="pool_fwd",
        in_specs=[chunk(D_POOL), _full((4, 128, 128)), _full((8, D_POOL))],
        out_specs=[chunk(D_POOL), chunk(D_POOL)],
        out_shape=[sds((n_pad, D_POOL), BF16), sds((n_pad, D_POOL), BF16)],
        scratch_shapes=[pltpu.VMEM((POOL_HALO, D_POOL), F32)],
        compiler_params=_params(("arbitrary",)),
    )(v, pw_b, vecs)


def _pool_bwd(dmp, feat, pw_b, vecs, tc):
    n_pad = dmp.shape[0]
    nc = n_pad // tc

    def body(dmp_ref, feat_ref, pw_ref, vec_ref, dv_ref, dpw_ref, dvec_ref, fut):
        c = pl.program_id(0)

        @pl.when(c == 0)
        def _():
            fut[...] = jnp.zeros_like(fut)
            dpw_ref[...] = jnp.zeros_like(dpw_ref)
            dvec_ref[...] = jnp.zeros_like(dvec_ref)

        scale, gp = vec_ref[0:1, :], vec_ref[1:2, :]
        feat_v = feat_ref[...]
        p = jnp.concatenate([_dot(feat_v[:, 128 * k:128 * (k + 1)], pw_ref[k]) for k in range(4)], axis=1)
        yp = p * scale
        r = _rms(yp)
        xhat = yp * r
        dm = dmp_ref[...]
        dyp = _rms_bwd(dm * gp, xhat, r)
        dvec_ref[0:1, :] += jnp.sum(dyp * p, axis=0, keepdims=True)
        dvec_ref[1:2, :] += jnp.sum(dm * xhat, axis=0, keepdims=True)
        dpb = (dyp * scale).astype(BF16)
        es, dfs = [], []
        for k, w in enumerate(POOL_WINDOWS):
            cols = slice(128 * k, 128 * (k + 1))
            dpw_ref[k] += _dot_tn(feat_v[:, cols], dpb[:, cols])
            df = _dot_nt(dpb[:, cols], pw_ref[k])
            dfs.append(df)
            es.append(df * _inv_count(nc - 1 - c, tc, w))
        e = jnp.concatenate(es, axis=1)
        ext = jnp.concatenate([e, fut[...]], axis=0)
        fut[...] = e[:POOL_HALO, :]
        n_ext = tc + POOL_HALO
        dvs = []
        for k, w in enumerate(POOL_WINDOWS):
            s = ext[:, 128 * k:128 * (k + 1)]
            sh = 1
            while sh < w:
                s = s + pltpu.roll(s, n_ext - sh, 0)
                sh *= 2
            dvs.append(s[:tc, :] - dfs[k])
        dv_ref[...] = jnp.concatenate(dvs, axis=1)

    chunk = lambda w: pl.BlockSpec((tc, w), lambda c: (nc - 1 - c, 0))
    sds = jax.ShapeDtypeStruct
    return pl.pallas_call(
        body, grid=(nc,), name="pool_bwd",
        in_specs=[chunk(D_POOL), chunk(D_POOL), _full((4, 128, 128)), _full((8, D_POOL))],
        out_specs=[chunk(D_POOL), _full((4, 128, 128)), _full((8, D_POOL))],
        out_shape=[sds((n_pad, D_POOL), F32), sds((4, 128, 128), F32), sds((8, D_POOL), F32)],
        scratch_shapes=[pltpu.VMEM((POOL_HALO, D_POOL), F32)],
        compiler_params=_params(("arbitrary",)),
    )(dmp, feat, pw_b, vecs)


def _place():
    x, y, c = lax.axis_index("x"), lax.axis_index("y"), lax.axis_index("c")
    chips = [(1 - x, y), (x, 1 - y), (1 - x, 1 - y)]
    return x, y, c, chips


_ANY = pl.BlockSpec(memory_space=pl.ANY)


def _cast_shards(shards, dtypes, place):
    n = len(shards)

    def body(place_ref, *refs):
        for i in range(n):
            refs[n + i][0] = refs[i][...].astype(dtypes[i])

    return pl.pallas_call(
        body, name="cast_shards",
        grid_spec=pltpu.PrefetchScalarGridSpec(
            num_scalar_prefetch=1, grid=(1,),
            in_specs=[pl.BlockSpec(s.shape, lambda i, p: (0, 0, 0)) for s in shards],
            out_specs=[pl.BlockSpec((1,) + s.shape, lambda i, p: (p[0], 0, 0, 0)) for s in shards]),
        out_shape=[jax.ShapeDtypeStruct((N_SHARD,) + s.shape, dt) for s, dt in zip(shards, dtypes)],
        compiler_params=_params(("arbitrary",)),
    )(place, *shards)


def _gather_shards(full):
    n = len(full)

    def body(*refs):
        outs = refs[n:2 * n]
        ici_send, ici_recv, d2d_send, d2d_recv = refs[2 * n:]
        x, y, c, chips = _place()
        q = 2 * x + y
        sibling = (x, y, 1 - c)

        def ici(i, j, shard, to):
            return pltpu.make_async_remote_copy(src_ref=outs[i].at[q, c], dst_ref=outs[i].at[shard, c],
                                                send_sem=ici_send.at[i, j], recv_sem=ici_recv.at[i, j],
                                                device_id=to, device_id_type=MESH)

        def d2d(i, j, shard, half):
            return pltpu.make_async_remote_copy(src_ref=outs[i].at[shard, c], dst_ref=outs[i].at[shard, half],
                                                send_sem=d2d_send.at[i, j], recv_sem=d2d_recv.at[i, j],
                                                device_id=sibling, device_id_type=MESH)

        sends = [ici(i, j, q, (*chip, c)) for i in range(n) for j, chip in enumerate(chips)]
        for cp in sends:
            cp.start()
        passed = []
        for i in range(n):
            for j, (cx, cy) in enumerate(chips):
                ici(i, j, 2 * cx + cy, (cx, cy, c)).wait_recv()
                cp = d2d(i, j, 2 * cx + cy, c)
                cp.start()
                passed.append(cp)
        for i in range(n):
            for j, (cx, cy) in enumerate(chips):
                d2d(i, j, 2 * cx + cy, 1 - c).wait_recv()
        for cp in sends + passed:
            cp.wait_send()

    return pl.pallas_call(
        body, name="gather_shards",
        in_specs=[_ANY] * n, out_specs=[_ANY] * n,
        out_shape=[jax.ShapeDtypeStruct(f.shape, f.dtype) for f in full],
        input_output_aliases={i: i for i in range(n)},
        scratch_shapes=[pltpu.SemaphoreType.DMA((n, 3)), pltpu.SemaphoreType.DMA((n, 3)),
                        pltpu.SemaphoreType.DMA((n, 3)), pltpu.SemaphoreType.DMA((n, 3))],
    )(*full)


def _forward_halves(full):
    n = len(full)

    def body(*refs):
        outs = refs[n:2 * n]
        send, recv = refs[2 * n:]
        x, y, c, chips = _place()

        def d2d(i, j, shard, half):
            return pltpu.make_async_remote_copy(src_ref=outs[i].at[shard, c], dst_ref=outs[i].at[shard, half],
                                                send_sem=send.at[i, j], recv_sem=recv.at[i, j],
                                                device_id=(x, y, 1 - c), device_id_type=MESH)

        cps = [d2d(i, j, 2 * cx + cy, c) for i in range(n) for j, (cx, cy) in enumerate(chips)]
        for cp in cps:
            cp.start()
        for i in range(n):
            for j, (cx, cy) in enumerate(chips):
                d2d(i, j, 2 * cx + cy, 1 - c).wait_recv()
        for cp in cps:
            cp.wait_send()

    return pl.pallas_call(
        body, name="forward_halves",
        in_specs=[_ANY] * n, out_specs=[_ANY] * n,
        out_shape=[jax.ShapeDtypeStruct(f.shape, f.dtype) for f in full],
        input_output_aliases={i: i for i in range(n)},
        scratch_shapes=[pltpu.SemaphoreType.DMA((n, 3)), pltpu.SemaphoreType.DMA((n, 3))],
    )(*full)


_HBM = pl.BlockSpec(memory_space=pltpu.HBM)
_SEM = pl.BlockSpec(memory_space=pltpu.SEMAPHORE)
_EFFECT = pltpu.SideEffectType.DATAFLOW_SIDE_EFFECTING


def _copies_start(name, arrays, sem_shape, build):
    n = len(arrays)

    def body(*refs):
        outs = refs[n:2 * n]
        send, recv, token = refs[2 * n:]
        sends, _ = build(outs, send, recv)
        for cp in sends:
            cp.start()
        token[...] = jnp.zeros_like(token)

    out = pl.pallas_call(
        body, name=name, in_specs=[_HBM] * n,
        out_specs=[_HBM] * n + [_SEM, _SEM, pl.BlockSpec(memory_space=pltpu.VMEM)],
        out_shape=[pltpu.HBM(a.shape, a.dtype) for a in arrays]
        + [pltpu.SemaphoreType.DMA(sem_shape), pltpu.SemaphoreType.DMA(sem_shape), jax.ShapeDtypeStruct((8, 128), F32)],
        input_output_aliases={i: i for i in range(n)},
        compiler_params=pltpu.CompilerParams(has_side_effects=_EFFECT),
    )(*[pltpu.with_memory_space_constraint(a, pltpu.HBM) for a in arrays])
    return list(out[:n]), (out[n], out[n + 1]), out[n + 2]


def _copies_wait(name, arrays, sems, after, build):
    n = len(arrays)

    def body(*refs):
        ins = refs[:n]
        send, recv = refs[n], refs[n + 1]
        sends, recvs = build(ins, send, recv)
        for cp in sends:
            cp.wait_send()
        for cp in recvs:
            cp.wait_recv()

    return list(pl.pallas_call(
        body, name=name, in_specs=[_HBM] * n + [_SEM, _SEM, _ANY], out_specs=[_HBM] * n,
        out_shape=[pltpu.HBM(a.shape, a.dtype) for a in arrays],
        input_output_aliases={i: i for i in range(n)},
        compiler_params=pltpu.CompilerParams(has_side_effects=_EFFECT),
    )(*arrays, *sems, after))


def _remote(src, dst, send_sem, recv_sem, to):
    return pltpu.make_async_remote_copy(src_ref=src, dst_ref=dst, send_sem=send_sem, recv_sem=recv_sem,
                                        device_id=to, device_id_type=MESH)


def _build_gather(refs, send, recv):
    x, y, c, chips = _place()
    q = 2 * x + y
    pairs = [(i, j, chip) for i in range(len(refs)) for j, chip in enumerate(chips)]
    sends = [_remote(refs[i].at[q, c], refs[i].at[q, c], send.at[3 * i + j], recv.at[3 * i + j], (cx, cy, c))
             for i, j, (cx, cy) in pairs]
    recvs = [_remote(refs[i].at[q, c], refs[i].at[2 * cx + cy, c], send.at[3 * i + j], recv.at[3 * i + j], (cx, cy, c))
             for i, j, (cx, cy) in pairs]
    return sends, recvs


def _build_swap(refs, send, recv):
    x, y, c, _ = _place()
    n = len(refs) // 2
    cps = [_remote(refs[i].at[:, 1 - c], refs[n + i], send.at[i], recv.at[i], (x, y, 1 - c)) for i in range(n)]
    return cps, cps


def _build_exchange(refs, send, recv):
    x, y, c, chips = _place()
    n = len(refs) // 2
    cps = [_remote(refs[i].at[2 * cx + cy], refs[n + i].at[j], send.at[3 * i + j], recv.at[3 * i + j], (cx, cy, c))
           for i in range(n) for j, (cx, cy) in enumerate(chips)]
    return cps, cps


def _build_spread(refs, send, recv):
    x, y, c, _ = _place()
    flip = lambda bit, on: bit + on - 2 * bit * on
    cps = [_remote(refs[0], refs[1].at[r - 1], send.at[r - 1], recv.at[r - 1],
                   (flip(x, r >> 2 & 1), flip(y, r >> 1 & 1), flip(c, r & 1))) for r in range(1, 8)]
    return cps, cps


def _swap_halves(grads, token):
    n = len(grads)

    def body(*refs):
        ins, outs = refs[:n], refs[n + 1:2 * n + 1]
        send, recv = refs[2 * n + 1:]
        x, y, c, _ = _place()
        cps = [pltpu.make_async_remote_copy(src_ref=ins[i].at[:, 1 - c], dst_ref=outs[i], send_sem=send.at[i],
                                            recv_sem=recv.at[i], device_id=(x, y, 1 - c), device_id_type=MESH)
               for i in range(n)]
        for cp in cps:
            cp.start()
        for cp in cps:
            cp.wait()

    return pl.pallas_call(
        body, name="swap_halves",
        in_specs=[_ANY] * (n + 1), out_specs=[_ANY] * n,
        out_shape=[jax.ShapeDtypeStruct((N_SHARD,) + g.shape[2:], F32) for g in grads],
        scratch_shapes=[pltpu.SemaphoreType.DMA((n,)), pltpu.SemaphoreType.DMA((n,))],
    )(*grads, token)


def _join_halves(pairs):
    n = len(pairs)

    def body(*refs):
        outs = refs[n:2 * n]
        send, recv = refs[2 * n:]
        x, y, c, _ = _place()
        cps = [pltpu.make_async_remote_copy(src_ref=outs[i].at[c], dst_ref=outs[i].at[c], send_sem=send.at[i],
                                            recv_sem=recv.at[i], device_id=(x, y, 1 - c), device_id_type=MESH)
               for i in range(n)]
        for cp in cps:
            cp.start()
        for i in range(n):
            cps[i].wait_send()
            pltpu.make_async_remote_copy(src_ref=outs[i].at[c], dst_ref=outs[i].at[1 - c], send_sem=send.at[i],
                                         recv_sem=recv.at[i], device_id=(x, y, 1 - c), device_id_type=MESH).wait_recv()

    return pl.pallas_call(
        body, name="join_halves",
        in_specs=[_ANY] * n, out_specs=[_ANY] * n,
        out_shape=[jax.ShapeDtypeStruct(p.shape, F32) for p in pairs],
        input_output_aliases={i: i for i in range(n)},
        scratch_shapes=[pltpu.SemaphoreType.DMA((n,)), pltpu.SemaphoreType.DMA((n,))],
    )(*pairs)


N_SPLIT = 2


def _sum_siblings(tag, grads, recvd, place):
    n = len(grads)

    def body(place_ref, *refs):
        g_refs, r_refs, sb_refs, own_refs = (refs[k * n:(k + 1) * n] for k in range(4))
        s = pl.program_id(1)
        for i in range(n):
            tot = g_refs[i][0, 0] + r_refs[i][0]
            sb_refs[i][0] = tot.astype(BF16)

            @pl.when(s == place_ref[0])
            def _():
                own_refs[i][...] = tot

    in_specs, sb_specs, own_specs, sb_shapes, own_shapes = [], [], [], [], []
    for g in grads:
        _, _, r, cdim = g.shape
        rb = r // N_SPLIT
        in_specs.append(pl.BlockSpec((1, 1, rb, cdim), lambda b, s, p: (s, p[1], b, 0)))
        sb_specs.append(pl.BlockSpec((1, rb, cdim), lambda b, s, p: (s, b, 0)))
        own_specs.append(pl.BlockSpec((rb, cdim), lambda b, s, p: (b, 0)))
        sb_shapes.append(jax.ShapeDtypeStruct((N_SHARD, r, cdim), BF16))
        own_shapes.append(jax.ShapeDtypeStruct((r, cdim), F32))
    out = pl.pallas_call(
        body, name="sum_siblings_" + tag,
        grid_spec=pltpu.PrefetchScalarGridSpec(
            num_scalar_prefetch=1, grid=(N_SPLIT, N_SHARD),
            in_specs=in_specs + sb_specs, out_specs=sb_specs + own_specs),
        out_shape=sb_shapes + own_shapes,
        compiler_params=_params(("parallel", "arbitrary")),
    )(place, *grads, *recvd)
    return out[:n], out[n:]


def _sum_chips(own, recvd, place):
    n = len(own)

    def body(place_ref, *refs):
        o_refs, r_refs, out_refs = (refs[k * n:(k + 1) * n] for k in range(3))
        for i in range(n):
            tot = o_refs[i][...]
            for j in range(3):
                tot = tot + r_refs[i][j].astype(F32)
            out_refs[i][0] = tot

    o_specs, r_specs, out_specs = [], [], []
    for o in own:
        r, cdim = o.shape
        rb = r // N_SPLIT
        o_specs.append(pl.BlockSpec((rb, cdim), lambda b, p: (b, 0)))
        r_specs.append(pl.BlockSpec((3, rb, cdim), lambda b, p: (0, b, 0)))
        out_specs.append(pl.BlockSpec((1, rb, cdim), lambda b, p: (p[1], b, 0)))
    return pl.pallas_call(
        body, name="sum_chips",
        grid_spec=pltpu.PrefetchScalarGridSpec(num_scalar_prefetch=1, grid=(N_SPLIT,),
                                               in_specs=o_specs + r_specs, out_specs=out_specs),
        out_shape=[jax.ShapeDtypeStruct((2,) + o.shape, F32) for o in own],
        compiler_params=_params(("parallel",)),
    )(place, *own, *recvd)


def _adamw_math(w, g, m, v):
    m = ADAM_B1 * m + (1.0 - ADAM_B1) * g
    v = ADAM_B2 * v + (1.0 - ADAM_B2) * (g * g)
    m_hat = m / (1.0 - ADAM_B1 ** ADAM_STEP)
    v_hat = v / (1.0 - ADAM_B2 ** ADAM_STEP)
    delta = -ADAM_LR * (m_hat / (jnp.sqrt(v_hat) + ADAM_EPS) + ADAM_WD * w)
    return delta, m, v


def _adamw(name, ws, gs, ms, vs, n_split):
    n = len(ws)

    def body(*refs):
        w_r, g_r, m_r, v_r, d_o, m_o, v_o = (refs[k * n:(k + 1) * n] for k in range(7))
        for i in range(n):
            d, m, v = _adamw_math(w_r[i][...], g_r[i][...], m_r[i][...], v_r[i][...])
            d_o[i][...] = d
            m_o[i][...] = m
            v_o[i][...] = v

    specs = [pl.BlockSpec((w.shape[0] // n_split, w.shape[1]), lambda b: (b, 0)) for w in ws]
    shapes = [jax.ShapeDtypeStruct(w.shape, F32) for w in ws]
    out = pl.pallas_call(
        body, name=name, grid=(n_split,),
        in_specs=specs * 4, out_specs=specs * 3, out_shape=shapes * 3,
        compiler_params=_params(("parallel",)),
    )(*ws, *gs, *ms, *vs)
    return out[:n], out[n:2 * n], out[2 * n:]


def _reduce_small(own, received, w, m, v):
    def body(own_ref, recv_ref, w_ref, m_ref, v_ref, g_out, d_out, m_out, v_out):
        me = 4 * lax.axis_index("x") + 2 * lax.axis_index("y") + lax.axis_index("c")
        g = None
        for k in range(8):
            mine = me == k
            part = jnp.where(mine, own_ref[...], recv_ref[jnp.where(mine, 0, jnp.bitwise_xor(me, k) - 1)])
            g = part if g is None else g + part
        g_out[...] = g
        d, mm, vv = _adamw_math(w_ref[...], g, m_ref[...], v_ref[...])
        d_out[...] = d
        m_out[...] = mm
        v_out[...] = vv

    return pl.pallas_call(
        body, name="reduce_small",
        out_shape=[jax.ShapeDtypeStruct(w.shape, F32)] * 4,
        compiler_params=_params(),
    )(own, received, w, m, v)


def _s5_operands(lam_re, lam_im, log_step, b_re, b_im, c_re, c_im, glu_w):
    lr = jnp.minimum(lam_re, -1e-4)
    li = lam_im
    step = jnp.exp(log_step)[:, None]
    mag = jnp.exp(lr * step)
    ang = li * step
    abr = mag * jnp.cos(ang)
    abi = mag * jnp.sin(ang)
    nr = abr - 1.0
    ni = abi
    den = lr * lr + li * li
    cr = ((nr * lr + ni * li) / den)[..., None]
    ci = ((ni * lr - nr * li) / den)[..., None]
    bbr = cr * b_re - ci * b_im
    bbi = cr * b_im + ci * b_re
    eye = jnp.eye(8, dtype=F32)
    g, h, p = SSM_GROUPS // SUPER, SSM_GROUP, SSM_STATE

    def b_layout(t):
        return jnp.einsum("ab,japh->jahbp", eye, t.reshape(SUPER, g, p, h)).reshape(SUPER, g * h, g * p)

    def c_layout(t):
        return jnp.einsum("ab,jahp->jbpah", eye, t.reshape(SUPER, g, h, p)).reshape(SUPER, g * p, g * h)

    glu = jnp.einsum("ab,jahk->jahbk", eye, glu_w.reshape(SUPER, g, h, h)).reshape(SUPER, g * h, g * h)
    lam = _pad_rows(jnp.concatenate([abr.reshape(1, N_STATE), abi.reshape(1, N_STATE)], axis=0), 8)
    return lam, b_layout(bbr), b_layout(bbi), c_layout(c_re), c_layout(c_im), glu


def _pad_rows(a, rows):
    return jnp.pad(a, ((0, rows - a.shape[0]), (0, 0)))


def _pack(parts):
    rows = []
    for a in parts:
        flat = a.reshape(-1)
        n = -(-flat.shape[0] // 128)
        rows.append(jnp.pad(flat, (0, n * 128 - flat.shape[0])).reshape(n, 128))
    out = jnp.concatenate(rows, axis=0)
    return _pad_rows(out, -(-out.shape[0] // 8) * 8)


def _unpack(packed, like):
    out, at = [], 0
    for a in like:
        n = -(-a.size // 128)
        out.append(packed[at:at + n].reshape(-1)[:a.size].reshape(a.shape))
        at += n
    return out


SMALL = ("norm1_g", "ssm_lambda_re", "ssm_lambda_im", "ssm_log_step", "ssm_b_re", "ssm_b_im", "ssm_c_re", "ssm_c_im",
         "ssm_d", "ssm_glu_w", "ssm_glu_b", "ssm_norm_g", "pool_w", "pool_scale", "pool_norm_g", "norm2_g",
         "final_norm_g")
LARGE = ("w_in", "w_out", "w_gate", "w_up", "w_down")
WEIGHTS = ("meta_tokens", "norm1_g", "w_in", "ssm_lambda_re", "ssm_lambda_im", "ssm_log_step", "ssm_b_re", "ssm_b_im",
           "ssm_c_re", "ssm_c_im", "ssm_d", "ssm_glu_w", "ssm_glu_b", "ssm_norm_g", "pool_w", "pool_scale",
           "pool_norm_g", "w_out", "norm2_g", "w_gate", "w_up", "w_down", "final_norm_g")


def _step(x, target, w, m, v):
    seq = x.shape[1]
    n_rows = N_META + seq
    n_pad, tm, tc = _plan(n_rows)
    xq, yq, cq = lax.axis_index("x"), lax.axis_index("y"), lax.axis_index("c")
    place = jnp.stack([2 * xq + yq, cq]).astype(jnp.int32)

    def halves(a2d):
        return a2d.reshape(2, a2d.shape[0] // 2, a2d.shape[1])

    def local2d(t):
        return {"w_gate": lambda a: a[0].T, "w_up": lambda a: a[0].T}.get(t, lambda a: a[0])

    shards = [halves(local2d(k)(w[k])) for k in LARGE] + [halves(w["meta_tokens"])]
    full = _cast_shards(shards, [BF16] * len(LARGE) + [F32], place)
    w_in_full, meta_full = _gather_shards([full[0], full[5]])
    late, gather_sems, gather_token = _copies_start("gather_start", list(full[1:5]), (12,), _build_gather)
    w_in_b = w_in_full.reshape(D_MODEL, D_MODEL)
    meta = meta_full.reshape(N_SHARD, N_META, D_MODEL // N_SHARD).transpose(1, 0, 2).reshape(N_META, D_MODEL)

    h0 = _pad_rows(jnp.concatenate([meta, x[0]], axis=0), n_pad)
    tgt = _pad_rows(jnp.concatenate([jnp.zeros((N_META, D_MODEL), F32), target[0]], axis=0), n_pad)
    s5_in = (w["ssm_lambda_re"][0], w["ssm_lambda_im"][0], w["ssm_log_step"][0], w["ssm_b_re"][0], w["ssm_b_im"][0],
             w["ssm_c_re"][0], w["ssm_c_im"][0], w["ssm_glu_w"][0])
    (lam, bbr, bbi, crt, cit, glu), s5_vjp = jax.vjp(_s5_operands, *s5_in)
    bbr_b, bbi_b, crt_b, cit_b, glu_b16 = (t.astype(BF16) for t in (bbr, bbi, crt, cit, glu))
    s5_vecs = _pad_rows(jnp.concatenate([w["ssm_d"].reshape(1, D_SSM), w["ssm_glu_b"].reshape(1, D_SSM),
                                         w["ssm_norm_g"].reshape(1, D_SSM)], axis=0), 8)
    pool_vecs = _pad_rows(jnp.concatenate([w["pool_scale"].reshape(1, D_POOL), w["pool_norm_g"].reshape(1, D_POOL)],
                                          axis=0), 8)
    pw_b = w["pool_w"][0].astype(BF16)
    g1, g2, gf = w["norm1_g"].reshape(1, D_MODEL), w["norm2_g"].reshape(1, D_MODEL), w["final_norm_g"].reshape(1, D_MODEL)

    u, vv = _fwd_in(h0, g1, w_in_b, tm, gather_token)
    sr, si, y, ms = _s5_fwd(u, lam, bbr_b, bbi_b, crt_b, cit_b, s5_vecs, glu_b16, tc)
    feat, mp = _pool_fwd(vv, pw_b, pool_vecs, tc)
    late = _forward_halves(_copies_wait("gather_wait", late, gather_sems, mp, _build_gather))
    w_out_b = late[0].reshape(D_MODEL, D_MODEL)
    wg_b, wu_b, wd_b = (t.reshape(N_SHARD, FF_SHARD, D_MODEL) for t in late[1:])
    h1, n2, a, b, ff, dh2, loss_acc, dgf = _fwd_ffn(h0, ms, mp, w_out_b, g2, wg_b, wu_b, wd_b, gf, tgt, tm, n_rows)

    def quarters(t):
        if t.ndim == 2:
            t = t.reshape(N_SHARD, t.shape[0] // N_SHARD, t.shape[1])
        return t.reshape(N_SHARD, 2, t.shape[1] // 2, t.shape[2])

    def landing(like, lead, dtype):
        return [lax.empty((lead,) + t.shape[2:], dtype) for t in like]

    da, db, dh1, dg2 = _bwd_ffn(dh2, a, b, wg_b, wu_b, wd_b, h1, g2, tm)
    ffn_g = [quarters(t) for t in _grad_ffn(n2, da, db, ff, dh2, tm)]
    nf = len(ffn_g)
    moved, swap_sems, swap_token = _copies_start("swap_start", ffn_g + landing(ffn_g, N_SHARD, F32), (nf,), _build_swap)
    dms, dmp, dwo = _bwd_out(dh1, ms, mp, w_out_b, tm, swap_token)
    moved = _copies_wait("swap_wait", moved, swap_sems, dwo, _build_swap)
    ffn_parts, ffn_own = _sum_siblings("ffn", moved[:nf], moved[nf:], place)
    moved, exch_sems, exch_token = _copies_start("exchange_start", list(ffn_parts) + landing(ffn_g, 3, BF16), (3 * nf,),
                                                 _build_exchange)
    du, dbbr, dbbi, dcrt, dcit, dglu, ds5v, dlam = _s5_bwd(dms, y, u, sr, si, lam, bbr_b, bbi_b, crt_b, cit_b,
                                                           s5_vecs, glu_b16, tc, exch_token)
    dv, dpw, dpoolv = _pool_bwd(dmp, feat, pw_b, pool_vecs, tc)
    dh0, dwi, dg1 = _bwd_in(du, dv, h0, dh1, g1, w_in_b, tm)
    ffn_from_chips = _copies_wait("exchange_wait", moved, exch_sems, dh0, _build_exchange)[nf:]
    dlam = _pad_rows(jnp.concatenate([jnp.sum(dlam[:SEGMENTS], axis=0, keepdims=True),
                                      jnp.sum(dlam[SEGMENTS:], axis=0, keepdims=True)], axis=0), 8)
    d_lre, d_lim, d_lstep, d_bre, d_bim, d_cre, d_cim, d_gluw = s5_vjp((dlam, dbbr, dbbi, dcrt, dcit, dglu))
    grad_x = dh0[N_META:n_rows][None]

    small_g = {
        "norm1_g": dg1, "ssm_lambda_re": d_lre, "ssm_lambda_im": d_lim, "ssm_log_step": d_lstep, "ssm_b_re": d_bre,
        "ssm_b_im": d_bim, "ssm_c_re": d_cre, "ssm_c_im": d_cim, "ssm_d": ds5v[0], "ssm_glu_w": d_gluw,
        "ssm_glu_b": ds5v[1], "ssm_norm_g": ds5v[2], "pool_w": dpw, "pool_scale": dpoolv[0], "pool_norm_g": dpoolv[1],
        "norm2_g": dg2, "final_norm_g": dgf,
    }
    like = [w[k] for k in SMALL]
    packed_g = _pack([small_g[k].reshape(w[k].shape) for k in SMALL] + [dh0[:N_META], loss_acc[0:1, 0:1]])
    rows = packed_g.shape[0]
    packed = lambda t: _pad_rows(_pack([t[k] for k in SMALL]), rows)
    spread, small_sems, small_token = _copies_start(
        "small_start", [packed_g, lax.empty((7,) + packed_g.shape, F32)], (7,), _build_spread)

    mix_g = [quarters(t) for t in (dwi, dwo)]
    mix_parts, mix_own = _sum_siblings("mix", mix_g, _swap_halves(mix_g, small_token), place)
    moved, mix_sems, mix_token = _copies_start("mix_exchange_start", list(mix_parts) + landing(mix_g, 3, BF16),
                                               (3 * len(mix_g),), _build_exchange)
    mix_from_chips = _copies_wait("mix_exchange_wait", moved, mix_sems, mix_token, _build_exchange)[len(mix_g):]
    joined = _join_halves(_sum_chips(list(mix_own) + list(ffn_own), list(mix_from_chips) + list(ffn_from_chips), place))
    g_large = [j.reshape(j.shape[0] * j.shape[1], j.shape[2]) for j in joined]
    w2d, m2d, v2d = ([local2d(k)(t[k]) for k in LARGE] for t in (w, m, v))
    d_large, m_large, v_large = _adamw("adamw_large", w2d, g_large, m2d, v2d, 8)

    own_g, received = _copies_wait("small_wait", spread, small_sems, d_large[0], _build_spread)
    g_pk, d_pk, m_pk, v_pk = _reduce_small(own_g, received, packed(w), packed(m), packed(v))
    g_small = _unpack(g_pk, like + [jax.ShapeDtypeStruct((N_META, D_MODEL), F32), jax.ShapeDtypeStruct((1, 1), F32)])
    loss = g_small.pop()[0, 0]
    d_small, m_small, v_small = (_unpack(t, like) for t in (d_pk, m_pk, v_pk))
    q = place[0]
    g_meta = lax.dynamic_slice_in_dim(g_small[-1], q * (D_MODEL // N_SHARD), D_MODEL // N_SHARD, axis=1)
    d_meta, m_meta, v_meta = _adamw("adamw_meta", [w["meta_tokens"]], [g_meta], [m["meta_tokens"]],
                                    [v["meta_tokens"]], 1)

    grads, deltas, new_m, new_v = {}, {}, {}, {}
    for i, k in enumerate(SMALL):
        grads[k], deltas[k], new_m[k], new_v[k] = g_small[i], d_small[i], m_small[i], v_small[i]
    for i, k in enumerate(LARGE):
        back = (lambda t: t.T[None]) if k in ("w_gate", "w_up") else (lambda t: t[None])
        grads[k], deltas[k], new_m[k], new_v[k] = (back(t) for t in (g_large[i], d_large[i], m_large[i], v_large[i]))
    grads["meta_tokens"], deltas["meta_tokens"] = g_meta, d_meta[0]
    new_m["meta_tokens"], new_v["meta_tokens"] = m_meta[0], v_meta[0]
    return (loss, grad_x, *[grads[k] for k in WEIGHTS], *[deltas[k] for k in WEIGHTS],
            *[new_m[k] for k in WEIGHTS], *[new_v[k] for k in WEIGHTS])


def kernel(x, meta_tokens, norm1_g, w_in, ssm_lambda_re, ssm_lambda_im, ssm_log_step, ssm_b_re, ssm_b_im, ssm_c_re, ssm_c_im, ssm_d, ssm_glu_w, ssm_glu_b, ssm_norm_g, pool_w, pool_scale, pool_norm_g, w_out, norm2_g, w_gate, w_up, w_down, final_norm_g, loss_target, m_meta_tokens, m_norm1_g, m_w_in, m_ssm_lambda_re, m_ssm_lambda_im, m_ssm_log_step, m_ssm_b_re, m_ssm_b_im, m_ssm_c_re, m_ssm_c_im, m_ssm_d, m_ssm_glu_w, m_ssm_glu_b, m_ssm_norm_g, m_pool_w, m_pool_scale, m_pool_norm_g, m_w_out, m_norm2_g, m_w_gate, m_w_up, m_w_down, m_final_norm_g, v_meta_tokens, v_norm1_g, v_w_in, v_ssm_lambda_re, v_ssm_lambda_im, v_ssm_log_step, v_ssm_b_re, v_ssm_b_im, v_ssm_c_re, v_ssm_c_im, v_ssm_d, v_ssm_glu_w, v_ssm_glu_b, v_ssm_norm_g, v_pool_w, v_pool_scale, v_pool_norm_g, v_w_out, v_norm2_g, v_w_gate, v_w_up, v_w_down, v_final_norm_g):
    w = dict(meta_tokens=meta_tokens, norm1_g=norm1_g, w_in=w_in, ssm_lambda_re=ssm_lambda_re, ssm_lambda_im=ssm_lambda_im, ssm_log_step=ssm_log_step, ssm_b_re=ssm_b_re, ssm_b_im=ssm_b_im, ssm_c_re=ssm_c_re, ssm_c_im=ssm_c_im, ssm_d=ssm_d, ssm_glu_w=ssm_glu_w, ssm_glu_b=ssm_glu_b, ssm_norm_g=ssm_norm_g, pool_w=pool_w, pool_scale=pool_scale, pool_norm_g=pool_norm_g, w_out=w_out, norm2_g=norm2_g, w_gate=w_gate, w_up=w_up, w_down=w_down, final_norm_g=final_norm_g)
    m = dict(meta_tokens=m_meta_tokens, norm1_g=m_norm1_g, w_in=m_w_in, ssm_lambda_re=m_ssm_lambda_re, ssm_lambda_im=m_ssm_lambda_im, ssm_log_step=m_ssm_log_step, ssm_b_re=m_ssm_b_re, ssm_b_im=m_ssm_b_im, ssm_c_re=m_ssm_c_re, ssm_c_im=m_ssm_c_im, ssm_d=m_ssm_d, ssm_glu_w=m_ssm_glu_w, ssm_glu_b=m_ssm_glu_b, ssm_norm_g=m_ssm_norm_g, pool_w=m_pool_w, pool_scale=m_pool_scale, pool_norm_g=m_pool_norm_g, w_out=m_w_out, norm2_g=m_norm2_g, w_gate=m_w_gate, w_up=m_w_up, w_down=m_w_down, final_norm_g=m_final_norm_g)
    v = dict(meta_tokens=v_meta_tokens, norm1_g=v_norm1_g, w_in=v_w_in, ssm_lambda_re=v_ssm_lambda_re, ssm_lambda_im=v_ssm_lambda_im, ssm_log_step=v_ssm_log_step, ssm_b_re=v_ssm_b_re, ssm_b_im=v_ssm_b_im, ssm_c_re=v_ssm_c_re, ssm_c_im=v_ssm_c_im, ssm_d=v_ssm_d, ssm_glu_w=v_ssm_glu_w, ssm_glu_b=v_ssm_glu_b, ssm_norm_g=v_ssm_norm_g, pool_w=v_pool_w, pool_scale=v_pool_scale, pool_norm_g=v_pool_norm_g, w_out=v_w_out, norm2_g=v_norm2_g, w_gate=v_w_gate, w_up=v_w_up, w_down=v_w_down, final_norm_g=v_final_norm_g)
    return _step(x, loss_target, w, m, v)
```

```python
import functools
import math

import jax
import jax.numpy as jnp
from jax import lax
from jax.experimental import pallas as pl
from jax.experimental.pallas import tpu as pltpu

F32 = jnp.float32
BF16 = jnp.bfloat16
MESH = pl.DeviceIdType.MESH
AXES = ("x", "y", "c")

D_MODEL = 1024
D_SSM = 512
D_POOL = 512
N_META = 16
SSM_GROUP = 16
SSM_GROUPS = 32
SSM_STATE = 64
N_STATE = SSM_GROUPS * SSM_STATE
STATE_BLOCKS = N_STATE // 128
SUPER = 4
POOL_WINDOWS = (2, 4, 8, 16)
POOL_HALO = 16
D_FF = 2816
N_SHARD = 4
FF_SHARD = D_FF // N_SHARD
EPS = 1e-6
ADAM_LR, ADAM_B1, ADAM_B2, ADAM_EPS, ADAM_WD, ADAM_STEP = 0.001, 0.9, 0.999, 1e-08, 0.01, 10
VMEM_LIMIT = 56 * 1024 * 1024


def _plan(n_rows):
    if n_rows > 2048:
        tm, tc, tg = 416, 320, 1040
    else:
        tm, tc, tg = 128, 64, 128
    step = math.lcm(tm, tc, tg)
    return -(-n_rows // step) * step, tm, tc, tg


def _params(sem=None):
    return pltpu.CompilerParams(dimension_semantics=sem, vmem_limit_bytes=VMEM_LIMIT)


def _dot(a, b):
    return jnp.dot(a, b, preferred_element_type=F32)


def _dot_nt(a, b):
    return lax.dot_general(a, b, (((1,), (1,)), ((), ())), preferred_element_type=F32)


def _dot_tn(a, b):
    return lax.dot_general(a, b, (((0,), (0,)), ((), ())), preferred_element_type=F32)


def _sigmoid(x):
    return 1.0 / (1.0 + jnp.exp(-x))


_GELU_C = math.sqrt(2.0 / math.pi)


def _gelu_and_grad(y):
    y2 = y * y
    t = jnp.tanh(_GELU_C * (y + 0.044715 * y * y2))
    g = 0.5 * y * (1.0 + t)
    dg = 0.5 * (1.0 + t) + 0.5 * y * (1.0 - t * t) * (_GELU_C * (1.0 + 3.0 * 0.044715 * y2))
    return g, dg


def _rms(x):
    return lax.rsqrt(jnp.mean(x * x, axis=-1, keepdims=True) + EPS)


def _rms_bwd(dn, xhat, r):
    return r * (dn - xhat * jnp.mean(dn * xhat, axis=-1, keepdims=True))


def _full(shape):
    nd = len(shape)
    return pl.BlockSpec(shape, lambda *_: (0,) * nd)


def _fwd_in(h0, g1, w_in_b, tm, token):
    n_pad = h0.shape[0]

    def body(h_ref, g_ref, w_ref, token_ref, u_ref, v_ref):
        h = h_ref[...]
        n1 = (h * _rms(h) * g_ref[...]).astype(BF16)
        proj = _dot(n1, w_ref[...])
        for i in range(4):
            u_ref[i] = proj[:, 128 * i:128 * (i + 1)]
        v_ref[...] = proj[:, D_SSM:]

    row = lambda w: pl.BlockSpec((tm, w), lambda i: (i, 0))
    return pl.pallas_call(
        body, grid=(n_pad // tm,), name="fwd_in",
        in_specs=[row(D_MODEL), _full((1, D_MODEL)), _full((D_MODEL, D_MODEL)), _ANY],
        out_specs=[pl.BlockSpec((4, tm, 128), lambda i: (0, i, 0)), row(D_POOL)],
        out_shape=[jax.ShapeDtypeStruct((4, n_pad, 128), F32), jax.ShapeDtypeStruct((n_pad, D_POOL), F32)],
        compiler_params=_params(("parallel",)),
    )(h0, g1, w_in_b, token)


def _fwd_ffn(h0, ms, mp, w_out_b, g2, wg_b, wu_b, wd_b, gf, target, tm, n_valid):
    n_pad = h0.shape[0]
    nt = n_pad // tm

    def body(h0_ref, ms_ref, mp_ref, wo_ref, g2_ref, wg_ref, wu_ref, wd_ref, gf_ref, tgt_ref,
             h1_ref, n2_ref, a_ref, b_ref, ff_ref, dh2_ref, dh2b_ref, loss_ref, dgf_ref, acc):
        i, q = pl.program_id(0), pl.program_id(1)

        @pl.when((i == 0) & (q == 0))
        def _():
            loss_ref[...] = jnp.zeros_like(loss_ref)
            dgf_ref[...] = jnp.zeros_like(dgf_ref)

        @pl.when(q == 0)
        def _():
            h1 = h0_ref[...] + _dot(ms_ref[...], wo_ref[:D_SSM, :]) + _dot(mp_ref[...], wo_ref[D_SSM:, :])
            h1_ref[...] = h1
            acc[...] = h1
            n2_ref[...] = (h1 * _rms(h1) * g2_ref[...]).astype(BF16)

        n2 = n2_ref[...]
        a = _dot_nt(n2, wg_ref[0])
        b = _dot_nt(n2, wu_ref[0])
        a_ref[0] = a.astype(BF16)
        b_ref[0] = b.astype(BF16)
        ff = (a * _sigmoid(a) * b).astype(BF16)
        ff_ref[0] = ff
        acc[...] += _dot(ff, wd_ref[0])

        @pl.when(q == N_SHARD - 1)
        def _():
            h2 = acc[...]
            r = _rms(h2)
            xhat = h2 * r
            gf_row = gf_ref[...]
            rows = i * tm + lax.broadcasted_iota(jnp.int32, (tm, 1), 0)
            valid = (rows >= N_META) & (rows < n_valid)
            diff = jnp.where(valid, xhat * gf_row - tgt_ref[...], 0.0)
            loss_ref[...] += jnp.full(loss_ref.shape, 0.5 / D_MODEL, F32) * jnp.sum(diff * diff)
            dout = diff * (1.0 / D_MODEL)
            dgf_ref[...] += jnp.sum(dout * xhat, axis=0, keepdims=True)
            dh2 = _rms_bwd(dout * gf_row, xhat, r)
            dh2_ref[...] = dh2
            dh2b_ref[...] = dh2.astype(BF16)

    row = lambda w: pl.BlockSpec((tm, w), lambda i, q: (i, 0))
    shard_rows = pl.BlockSpec((1, FF_SHARD, D_MODEL), lambda i, q: (q, 0, 0))
    act = pl.BlockSpec((1, tm, FF_SHARD), lambda i, q: (q, i, 0))
    sds = jax.ShapeDtypeStruct
    return pl.pallas_call(
        body, grid=(nt, N_SHARD), name="fwd_ffn",
        in_specs=[row(D_MODEL), row(D_SSM), row(D_POOL), _full((D_MODEL, D_MODEL)), _full((1, D_MODEL)),
                  shard_rows, shard_rows, shard_rows, _full((1, D_MODEL)), row(D_MODEL)],
        out_specs=[row(D_MODEL), row(D_MODEL), act, act, act, row(D_MODEL), row(D_MODEL), _full((8, 128)),
                   _full((1, D_MODEL))],
        out_shape=[sds((n_pad, D_MODEL), F32), sds((n_pad, D_MODEL), BF16),
                   sds((N_SHARD, n_pad, FF_SHARD), BF16), sds((N_SHARD, n_pad, FF_SHARD), BF16),
                   sds((N_SHARD, n_pad, FF_SHARD), BF16), sds((n_pad, D_MODEL), F32), sds((n_pad, D_MODEL), BF16),
                   sds((8, 128), F32), sds((1, D_MODEL), F32)],
        scratch_shapes=[pltpu.VMEM((tm, D_MODEL), F32)],
        compiler_params=_params(("arbitrary", "arbitrary")),
    )(h0, ms, mp, w_out_b, g2, wg_b, wu_b, wd_b, gf, target)


def _bwd_ffn(dh2, a, b, wg_b, wu_b, wd_b, h1, g2, tm):
    n_pad = dh2.shape[0]

    def body(dh2_ref, a_ref, b_ref, wg_ref, wu_ref, wd_ref, h1_ref, g2_ref, da_ref, db_ref, dh1_ref, dg2_ref, acc):
        i, q = pl.program_id(0), pl.program_id(1)

        @pl.when((i == 0) & (q == 0))
        def _():
            dg2_ref[...] = jnp.zeros_like(dg2_ref)

        dff = _dot_nt(dh2_ref[...].astype(BF16), wd_ref[0])
        a_v, b_v = a_ref[0].astype(F32), b_ref[0].astype(F32)
        sig = _sigmoid(a_v)
        da = (dff * b_v * sig * (1.0 + a_v * (1.0 - sig))).astype(BF16)
        db = (dff * a_v * sig).astype(BF16)
        da_ref[0] = da
        db_ref[0] = db
        part = _dot(da, wg_ref[0]) + _dot(db, wu_ref[0])

        @pl.when(q == 0)
        def _():
            acc[...] = part

        @pl.when(q > 0)
        def _():
            acc[...] += part

        @pl.when(q == N_SHARD - 1)
        def _():
            h1 = h1_ref[...]
            r = _rms(h1)
            xhat = h1 * r
            dn2 = acc[...]
            dg2_ref[...] += jnp.sum(dn2 * xhat, axis=0, keepdims=True)
            dh1_ref[...] = dh2_ref[...] + _rms_bwd(dn2 * g2_ref[...], xhat, r)

    row = lambda w: pl.BlockSpec((tm, w), lambda i, q: (i, 0))
    shard_rows = pl.BlockSpec((1, FF_SHARD, D_MODEL), lambda i, q: (q, 0, 0))
    act = pl.BlockSpec((1, tm, FF_SHARD), lambda i, q: (q, i, 0))
    sds = jax.ShapeDtypeStruct
    return pl.pallas_call(
        body, grid=(n_pad // tm, N_SHARD), name="bwd_ffn",
        in_specs=[row(D_MODEL), act, act, shard_rows, shard_rows, shard_rows, row(D_MODEL), _full((1, D_MODEL))],
        out_specs=[act, act, row(D_MODEL), _full((1, D_MODEL))],
        out_shape=[sds((N_SHARD, n_pad, FF_SHARD), BF16), sds((N_SHARD, n_pad, FF_SHARD), BF16),
                   sds((n_pad, D_MODEL), F32), sds((1, D_MODEL), F32)],
        scratch_shapes=[pltpu.VMEM((tm, D_MODEL), F32)],
        compiler_params=_params(("arbitrary", "arbitrary")),
    )(dh2, a, b, wg_b, wu_b, wd_b, h1, g2)


def _grad_ffn(n2, da, db, ff, dh2b, tm):
    n_pad = n2.shape[0]

    def body(n2_ref, da_ref, db_ref, ff_ref, dh2_ref, dwg_ref, dwu_ref, dwd_ref):
        i = pl.program_id(1)
        n2_v = n2_ref[...]
        gg = _dot_tn(da_ref[0], n2_v)
        gu = _dot_tn(db_ref[0], n2_v)
        gd = _dot_tn(ff_ref[0], dh2_ref[...])

        @pl.when(i == 0)
        def _():
            dwg_ref[0] = gg
            dwu_ref[0] = gu
            dwd_ref[0] = gd

        @pl.when(i > 0)
        def _():
            dwg_ref[0] += gg
            dwu_ref[0] += gu
            dwd_ref[0] += gd

    row = lambda w: pl.BlockSpec((tm, w), lambda q, i: (i, 0))
    act = pl.BlockSpec((1, tm, FF_SHARD), lambda q, i: (q, i, 0))
    sds = jax.ShapeDtypeStruct
    return pl.pallas_call(
        body, grid=(N_SHARD, n_pad // tm), name="grad_ffn",
        in_specs=[row(D_MODEL), act, act, act, row(D_MODEL)],
        out_specs=[pl.BlockSpec((1, FF_SHARD, D_MODEL), lambda q, i: (q, 0, 0))] * 3,
        out_shape=[sds((N_SHARD, FF_SHARD, D_MODEL), F32)] * 3,
        compiler_params=_params(("parallel", "arbitrary")),
    )(n2, da, db, ff, dh2b)


def _bwd_out(dh1, ms, mp, w_out_b, tm, token):
    n_pad = dh1.shape[0]

    def body(dh1_ref, ms_ref, mp_ref, wo_ref, token_ref, dms_ref, dmp_ref, dwo_ref):
        i = pl.program_id(0)

        @pl.when(i == 0)
        def _():
            dwo_ref[...] = jnp.zeros_like(dwo_ref)

        d = dh1_ref[...].astype(BF16)
        dms = _dot_nt(d, wo_ref[:D_SSM, :])
        for k in range(4):
            dms_ref[k] = dms[:, 128 * k:128 * (k + 1)]
        dmp_ref[...] = _dot_nt(d, wo_ref[D_SSM:, :])
        dwo_ref[:D_SSM, :] += _dot_tn(ms_ref[...], d)
        dwo_ref[D_SSM:, :] += _dot_tn(mp_ref[...], d)

    row = lambda w: pl.BlockSpec((tm, w), lambda i: (i, 0))
    sds = jax.ShapeDtypeStruct
    return pl.pallas_call(
        body, grid=(n_pad // tm,), name="bwd_out",
        in_specs=[row(D_MODEL), row(D_SSM), row(D_POOL), _full((D_MODEL, D_MODEL)), _ANY],
        out_specs=[pl.BlockSpec((4, tm, 128), lambda i: (0, i, 0)), row(D_POOL), _full((D_MODEL, D_MODEL))],
        out_shape=[sds((4, n_pad, 128), F32), sds((n_pad, D_POOL), F32), sds((D_MODEL, D_MODEL), F32)],
        compiler_params=_params(("arbitrary",)),
    )(dh1, ms, mp, w_out_b, token)


def _bwd_in(du, dv, h0, dh1, g1, w_in_b, tm):
    n_pad = h0.shape[0]

    def body(du_ref, dv_ref, h0_ref, dh1_ref, g1_ref, w_ref, dh0_ref, dwi_ref, dg1_ref):
        i = pl.program_id(0)

        @pl.when(i == 0)
        def _():
            dwi_ref[...] = jnp.zeros_like(dwi_ref)
            dg1_ref[...] = jnp.zeros_like(dg1_ref)

        dub = du_ref[...].astype(BF16)
        dvb = dv_ref[...].astype(BF16)
        dn1 = _dot_nt(dub, w_ref[:, :D_SSM]) + _dot_nt(dvb, w_ref[:, D_SSM:])
        h = h0_ref[...]
        r = _rms(h)
        xhat = h * r
        g_row = g1_ref[...]
        n1 = (xhat * g_row).astype(BF16)
        dwi_ref[:, :D_SSM] += _dot_tn(n1, dub)
        dwi_ref[:, D_SSM:] += _dot_tn(n1, dvb)
        dg1_ref[...] += jnp.sum(dn1 * xhat, axis=0, keepdims=True)
        dh0_ref[...] = dh1_ref[...] + _rms_bwd(dn1 * g_row, xhat, r)

    row = lambda w: pl.BlockSpec((tm, w), lambda i: (i, 0))
    sds = jax.ShapeDtypeStruct
    return pl.pallas_call(
        body, grid=(n_pad // tm,), name="bwd_in",
        in_specs=[row(D_SSM), row(D_POOL), row(D_MODEL), row(D_MODEL), _full((1, D_MODEL)), _full((D_MODEL, D_MODEL))],
        out_specs=[row(D_MODEL), _full((D_MODEL, D_MODEL)), _full((1, D_MODEL))],
        out_shape=[sds((n_pad, D_MODEL), F32), sds((D_MODEL, D_MODEL), F32), sds((1, D_MODEL), F32)],
        compiler_params=_params(("arbitrary",)),
    )(du, dv, h0, dh1, g1, w_in_b)


SEGMENTS = 8


def _interleaved(ref, seg):
    return jnp.concatenate(
        [jnp.concatenate([ref[i, pl.ds(j, SEGMENTS, stride=seg), :] for i in range(4)], axis=1) for j in range(seg)],
        axis=0)


def _time_order(scratch, val, seg):
    for i in range(4):
        scratch[i] = val[:, 128 * i:128 * (i + 1)]
    tiles = []
    for m in range(val.shape[0] // 8):
        s, j0 = divmod(8 * m, seg)
        tiles.append(jnp.concatenate(
            [scratch[i, pl.ds(8 * j0 + s, 8, stride=SEGMENTS), :] for i in range(4)], axis=1))
    return jnp.concatenate(tiles, axis=0)


def _power_table(lam_ref, pw_r, pw_i, seg):
    a_r = jnp.broadcast_to(lam_ref[0:1, :], (SEGMENTS, N_STATE))
    a_i = jnp.broadcast_to(lam_ref[1:2, :], (SEGMENTS, N_STATE))
    p_r, p_i = a_r, a_i
    for k in range(seg):
        pw_r[SEGMENTS * k:SEGMENTS * (k + 1), :] = p_r
        pw_i[SEGMENTS * k:SEGMENTS * (k + 1), :] = p_i
        p_r, p_i = p_r * a_r - p_i * a_i, p_r * a_i + p_i * a_r


def _segment_scan(xr_ref, xi_ref, cols, pw_r, pw_i, hr_s, hi_s, seg, reverse):
    sign = -1.0 if reverse else 1.0
    a_r, a_i = pw_r[0:SEGMENTS, cols], sign * pw_i[0:SEGMENTS, cols]

    def step(n, carry):
        hr, hi = carry
        o = pl.multiple_of((seg - 1 - n if reverse else n) * SEGMENTS, SEGMENTS)
        nr = a_r * hr - a_i * hi + xr_ref[pl.ds(o, SEGMENTS), cols]
        ni = a_r * hi + a_i * hr + xi_ref[pl.ds(o, SEGMENTS), cols]
        xr_ref[pl.ds(o, SEGMENTS), cols] = nr
        xi_ref[pl.ds(o, SEGMENTS), cols] = ni
        return nr, ni

    zero = jnp.zeros((SEGMENTS, cols.stop - cols.start), F32)
    e_r, e_i = lax.fori_loop(0, seg, step, (zero, zero), unroll=2)

    top = SEGMENTS * (seg - 1)
    ls_r, ls_i = pw_r[top:top + 1, cols], sign * pw_i[top:top + 1, cols]
    c_r, c_i = hr_s[0:1, cols], hi_s[0:1, cols]
    in_r, in_i = [None] * SEGMENTS, [None] * SEGMENTS
    for s in (range(SEGMENTS - 1, -1, -1) if reverse else range(SEGMENTS)):
        in_r[s], in_i[s] = c_r, c_i
        c_r, c_i = (e_r[s:s + 1, :] + ls_r * c_r - ls_i * c_i, e_i[s:s + 1, :] + ls_r * c_i + ls_i * c_r)
    hr_s[0:1, cols] = c_r
    hi_s[0:1, cols] = c_i
    cm_r, cm_i = jnp.concatenate(in_r, axis=0), jnp.concatenate(in_i, axis=0)

    def fix(jj, _):
        o = pl.multiple_of(jj * SEGMENTS, SEGMENTS)
        k = pl.multiple_of((seg - 1 - jj if reverse else jj) * SEGMENTS, SEGMENTS)
        p_r, p_i = pw_r[pl.ds(k, SEGMENTS), cols], sign * pw_i[pl.ds(k, SEGMENTS), cols]
        xr_ref[pl.ds(o, SEGMENTS), cols] += p_r * cm_r - p_i * cm_i
        xi_ref[pl.ds(o, SEGMENTS), cols] += p_r * cm_i + p_i * cm_r
        return 0

    lax.fori_loop(0, seg, fix, 0, unroll=2)


def _s5_tail(y, glu_ref, glub):
    g, dgelu = _gelu_and_grad(y)
    gb = g.astype(BF16)
    gate = jnp.concatenate([_dot(gb[:, 128 * j:128 * (j + 1)], glu_ref[j]) for j in range(SUPER)], axis=1) + glub
    sig = _sigmoid(gate)
    return g, gb, dgelu, sig, g * sig


def _s5_fwd(u4, lam, bbr, bbi, crt, cit, vecs, glu, tc):
    n_pad = u4.shape[1]
    seg = tc // SEGMENTS

    def body(u_ref, lam_ref, bbr_ref, bbi_ref, crt_ref, cit_ref, vec_ref, glu_ref,
             sr_ref, si_ref, y_ref, ms_ref, hr_s, hi_s, pw_r, pw_i, lanes):
        @pl.when(pl.program_id(0) == 0)
        def _():
            hr_s[...] = jnp.zeros_like(hr_s)
            hi_s[...] = jnp.zeros_like(hi_s)
            _power_table(lam_ref, pw_r, pw_i, seg)

        u_v = _interleaved(u_ref, seg)
        ub = u_v.astype(BF16)
        for j in range(SUPER):
            uj = ub[:, 128 * j:128 * (j + 1)]
            sr_ref[:, 512 * j:512 * (j + 1)] = _dot(uj, bbr_ref[j])
            si_ref[:, 512 * j:512 * (j + 1)] = _dot(uj, bbi_ref[j])
        for j in range(SUPER):
            _segment_scan(sr_ref, si_ref, slice(512 * j, 512 * (j + 1)), pw_r, pw_i, hr_s, hi_s, seg, False)

        d_row, glub, gs = vec_ref[0:1, :], vec_ref[1:2, :], vec_ref[2:3, :]
        ys_c = []
        for j in range(SUPER):
            sr_j = sr_ref[:, 512 * j:512 * (j + 1)].astype(BF16)
            si_j = si_ref[:, 512 * j:512 * (j + 1)].astype(BF16)
            ys_c.append(_dot(sr_j, crt_ref[j]) - _dot(si_j, cit_ref[j]))
        y = jnp.concatenate(ys_c, axis=1) + d_row * u_v
        y_ref[...] = y
        _, _, _, _, ys = _s5_tail(y, glu_ref, glub)
        ms_ref[...] = _time_order(lanes, ys * _rms(ys) * gs, seg).astype(BF16)

    chunk = lambda w: pl.BlockSpec((tc, w), lambda c: (c, 0))
    lane_blocks = pl.BlockSpec((4, tc, 128), lambda c: (0, c, 0))
    sds = jax.ShapeDtypeStruct
    return pl.pallas_call(
        body, grid=(n_pad // tc,), name="s5_fwd",
        in_specs=[lane_blocks, _full((8, N_STATE)), _full((SUPER, 128, 512)), _full((SUPER, 128, 512)),
                  _full((SUPER, 512, 128)), _full((SUPER, 512, 128)), _full((8, D_SSM)), _full((SUPER, 128, 128))],
        out_specs=[chunk(N_STATE), chunk(N_STATE), chunk(D_SSM), chunk(D_SSM)],
        out_shape=[sds((n_pad, N_STATE), F32), sds((n_pad, N_STATE), F32),
                   sds((n_pad, D_SSM), F32), sds((n_pad, D_SSM), BF16)],
        scratch_shapes=[pltpu.VMEM((8, N_STATE), F32), pltpu.VMEM((8, N_STATE), F32),
                        pltpu.VMEM((tc, N_STATE), F32), pltpu.VMEM((tc, N_STATE), F32),
                        pltpu.VMEM((4, tc, 128), F32)],
        compiler_params=_params(("arbitrary",)),
    )(u4, lam, bbr, bbi, crt, cit, vecs, glu)


def _s5_bwd(dms4, y, u4, sr, si, lam, bbr, bbi, crt, cit, vecs, glu, tc, token):
    n_pad = u4.shape[1]
    nc = n_pad // tc
    seg = tc // SEGMENTS

    def body(dms_ref, y_ref, u_ref, sr_ref, si_ref, pr_ref, pi_ref, lam_ref, bbr_ref, bbi_ref, crt_ref, cit_ref,
             vec_ref, glu_ref, token_ref, du_ref, dbbr_ref, dbbi_ref, dcrt_ref, dcit_ref, dglu_ref, dvec_ref, dlam_ref,
             qr_s, qi_s, cr_s, ci_s, pw_r, pw_i, lanes):
        c = pl.program_id(0)

        @pl.when(c == 0)
        def _():
            for ref in (dbbr_ref, dbbi_ref, dcrt_ref, dcit_ref, dglu_ref, dvec_ref, dlam_ref, cr_s, ci_s):
                ref[...] = jnp.zeros_like(ref)
            _power_table(lam_ref, pw_r, pw_i, seg)

        d_row, glub, gs = vec_ref[0:1, :], vec_ref[1:2, :], vec_ref[2:3, :]
        y_v, u_v = y_ref[...], _interleaved(u_ref, seg)
        ub = u_v.astype(BF16)
        g, gb, dgelu, sig, ys = _s5_tail(y_v, glu_ref, glub)
        r = _rms(ys)
        xhat = ys * r
        dm = _interleaved(dms_ref, seg)
        dys = _rms_bwd(dm * gs, xhat, r)
        dgate = dys * g * sig * (1.0 - sig)
        dgateb = dgate.astype(BF16)
        dg = dys * sig + jnp.concatenate(
            [_dot_nt(dgateb[:, 128 * j:128 * (j + 1)], glu_ref[j]) for j in range(SUPER)], axis=1)
        dy = dg * dgelu
        dyb = dy.astype(BF16)
        dvec_ref[0:1, :] += jnp.sum(dy * u_v, axis=0, keepdims=True)
        dvec_ref[1:2, :] += jnp.sum(dgate, axis=0, keepdims=True)
        dvec_ref[2:3, :] += jnp.sum(dm * xhat, axis=0, keepdims=True)

        for j in range(SUPER):
            cols, states = slice(128 * j, 128 * (j + 1)), slice(512 * j, 512 * (j + 1))
            dglu_ref[j] += _dot_tn(gb[:, cols], dgateb[:, cols])
            dcrt_ref[j] += _dot_tn(sr_ref[:, states].astype(BF16), dyb[:, cols])
            dcit_ref[j] -= _dot_tn(si_ref[:, states].astype(BF16), dyb[:, cols])
            qr_s[:, states] = _dot_nt(dyb[:, cols], crt_ref[j])
            qi_s[:, states] = -_dot_nt(dyb[:, cols], cit_ref[j])

        first = c == nc - 1
        row0 = lax.broadcasted_iota(jnp.int32, (SEGMENTS, 1), 0) == 0
        last = (seg - 1) * SEGMENTS
        for j in range(SUPER):
            states = slice(512 * j, 512 * (j + 1))
            _segment_scan(qr_s, qi_s, states, pw_r, pw_i, cr_s, ci_s, seg, True)

            before_r = jnp.where(first, 0.0, pltpu.roll(pr_ref[:, states], 1, 0))
            before_i = jnp.where(first, 0.0, pltpu.roll(pi_ref[:, states], 1, 0))
            hp_r = jnp.where(row0, before_r, pltpu.roll(sr_ref[pl.ds(last, SEGMENTS), states], 1, 0))
            hp_i = jnp.where(row0, before_i, pltpu.roll(si_ref[pl.ds(last, SEGMENTS), states], 1, 0))
            q_r, q_i = qr_s[pl.ds(0, SEGMENTS), states], qi_s[pl.ds(0, SEGMENTS), states]

            def dlam_step(jj, acc):
                o = pl.multiple_of(jj * SEGMENTS, SEGMENTS)
                above = pl.multiple_of((jj - 1) * SEGMENTS, SEGMENTS)
                h_r, h_i = sr_ref[pl.ds(above, SEGMENTS), states], si_ref[pl.ds(above, SEGMENTS), states]
                t_r, t_i = qr_s[pl.ds(o, SEGMENTS), states], qi_s[pl.ds(o, SEGMENTS), states]
                return acc[0] + t_r * h_r + t_i * h_i, acc[1] + t_i * h_r - t_r * h_i

            acc = lax.fori_loop(1, seg, dlam_step, (q_r * hp_r + q_i * hp_i, q_i * hp_r - q_r * hp_i), unroll=2)
            dlam_ref[0:SEGMENTS, states] += acc[0]
            dlam_ref[SEGMENTS:, states] += acc[1]

        du_c = []
        for j in range(SUPER):
            cols, states = slice(128 * j, 128 * (j + 1)), slice(512 * j, 512 * (j + 1))
            qr_j = qr_s[:, states].astype(BF16)
            qi_j = qi_s[:, states].astype(BF16)
            du_c.append(_dot_nt(qr_j, bbr_ref[j]) + _dot_nt(qi_j, bbi_ref[j]))
            dbbr_ref[j] += _dot_tn(ub[:, cols], qr_j)
            dbbi_ref[j] += _dot_tn(ub[:, cols], qi_j)
        du_ref[...] = _time_order(lanes, jnp.concatenate(du_c, axis=1) + dy * d_row, seg)

    rev = lambda c: nc - 1 - c
    chunk = lambda w: pl.BlockSpec((tc, w), lambda c: (rev(c), 0))
    lane_blocks = pl.BlockSpec((4, tc, 128), lambda c: (0, rev(c), 0))
    prev = pl.BlockSpec((SEGMENTS, N_STATE), lambda c: (jnp.maximum(rev(c) * seg - 1, 0), 0))
    sds = jax.ShapeDtypeStruct
    return pl.pallas_call(
        body, grid=(nc,), name="s5_bwd",
        in_specs=[lane_blocks, chunk(D_SSM), lane_blocks, chunk(N_STATE), chunk(N_STATE), prev, prev,
                  _full((8, N_STATE)), _full((SUPER, 128, 512)), _full((SUPER, 128, 512)),
                  _full((SUPER, 512, 128)), _full((SUPER, 512, 128)), _full((8, D_SSM)), _full((SUPER, 128, 128)), _ANY],
        out_specs=[chunk(D_SSM), _full((SUPER, 128, 512)), _full((SUPER, 128, 512)), _full((SUPER, 512, 128)),
                   _full((SUPER, 512, 128)), _full((SUPER, 128, 128)), _full((8, D_SSM)), _full((2 * SEGMENTS, N_STATE))],
        out_shape=[sds((n_pad, D_SSM), F32), sds((SUPER, 128, 512), F32), sds((SUPER, 128, 512), F32),
                   sds((SUPER, 512, 128), F32), sds((SUPER, 512, 128), F32), sds((SUPER, 128, 128), F32),
                   sds((8, D_SSM), F32), sds((2 * SEGMENTS, N_STATE), F32)],
        scratch_shapes=[pltpu.VMEM((tc, N_STATE), F32), pltpu.VMEM((tc, N_STATE), F32),
                        pltpu.VMEM((8, N_STATE), F32), pltpu.VMEM((8, N_STATE), F32),
                        pltpu.VMEM((tc, N_STATE), F32), pltpu.VMEM((tc, N_STATE), F32),
                        pltpu.VMEM((4, tc, 128), F32)],
        compiler_params=_params(("arbitrary",)),
    )(dms4, y, u4, sr, si, sr, si, lam, bbr, bbi, crt, cit, vecs, glu, token)


def _inv_count(c_idx, tc, w):
    t = c_idx * tc + lax.broadcasted_iota(jnp.int32, (tc, 1), 0)
    return 1.0 / jnp.minimum(t + 1, w).astype(F32)


def _pool_fwd(v, pw_b, vecs, tc):
    n_pad = v.shape[0]

    def body(v_ref, pw_ref, vec_ref, feat_ref, mp_ref, hist):
        c = pl.program_id(0)

        @pl.when(c == 0)
        def _():
            hist[...] = jnp.zeros_like(hist)

        v_v = v_ref[...]
        ext = jnp.concatenate([hist[...], v_v], axis=0)
        hist[...] = v_v[tc - POOL_HALO:, :]
        feats, ps = [], []
        for k, w in enumerate(POOL_WINDOWS):
            cols = slice(128 * k, 128 * (k + 1))
            s = ext[:, cols]
            sh = 1
            while sh < w:
                s = s + pltpu.roll(s, sh, 0)
                sh *= 2
            f = (s[POOL_HALO:, :] * _inv_count(c, tc, w) - v_v[:, cols]).astype(BF16)
            feats.append(f)
            ps.append(_dot(f, pw_ref[k]))
        feat_ref[...] = jnp.concatenate(feats, axis=1)
        yp = jnp.concatenate(ps, axis=1) * vec_ref[0:1, :]
        mp_ref[...] = (yp * _rms(yp) * vec_ref[1:2, :]).astype(BF16)

    chunk = lambda w: pl.BlockSpec((tc, w), lambda c: (c, 0))
    sds = jax.ShapeDtypeStruct
    return pl.pallas_call(
        body, grid=(n_pad // tc,), name="pool_fwd",
        in_specs=[chunk(D_POOL), _full((4, 128, 128)), _full((8, D_POOL))],
        out_specs=[chunk(D_POOL), chunk(D_POOL)],
        out_shape=[sds((n_pad, D_POOL), BF16), sds((n_pad, D_POOL), BF16)],
        scratch_shapes=[pltpu.VMEM((POOL_HALO, D_POOL), F32)],
        compiler_params=_params(("arbitrary",)),
    )(v, pw_b, vecs)


def _pool_bwd(dmp, feat, pw_b, vecs, tc):
    n_pad = dmp.shape[0]
    nc = n_pad // tc

    def body(dmp_ref, feat_ref, pw_ref, vec_ref, dv_ref, dpw_ref, dvec_ref, fut):
        c = pl.program_id(0)

        @pl.when(c == 0)
        def _():
            fut[...] = jnp.zeros_like(fut)
            dpw_ref[...] = jnp.zeros_like(dpw_ref)
            dvec_ref[...] = jnp.zeros_like(dvec_ref)

        scale, gp = vec_ref[0:1, :], vec_ref[1:2, :]
        feat_v = feat_ref[...]
        p = jnp.concatenate([_dot(feat_v[:, 128 * k:128 * (k + 1)], pw_ref[k]) for k in range(4)], axis=1)
        yp = p * scale
        r = _rms(yp)
        xhat = yp * r
        dm = dmp_ref[...]
        dyp = _rms_bwd(dm * gp, xhat, r)
        dvec_ref[0:1, :] += jnp.sum(dyp * p, axis=0, keepdims=True)
        dvec_ref[1:2, :] += jnp.sum(dm * xhat, axis=0, keepdims=True)
        dpb = (dyp * scale).astype(BF16)
        es, dfs = [], []
        for k, w in enumerate(POOL_WINDOWS):
            cols = slice(128 * k, 128 * (k + 1))
            dpw_ref[k] += _dot_tn(feat_v[:, cols], dpb[:, cols])
            df = _dot_nt(dpb[:, cols], pw_ref[k])
            dfs.append(df)
            es.append(df * _inv_count(nc - 1 - c, tc, w))
        e = jnp.concatenate(es, axis=1)
        ext = jnp.concatenate([e, fut[...]], axis=0)
        fut[...] = e[:POOL_HALO, :]
        n_ext = tc + POOL_HALO
        dvs = []
        for k, w in enumerate(POOL_WINDOWS):
            s = ext[:, 128 * k:128 * (k + 1)]
            sh = 1
            while sh < w:
                s = s + pltpu.roll(s, n_ext - sh, 0)
                sh *= 2
            dvs.append(s[:tc, :] - dfs[k])
        dv_ref[...] = jnp.concatenate(dvs, axis=1)

    chunk = lambda w: pl.BlockSpec((tc, w), lambda c: (nc - 1 - c, 0))
    sds = jax.ShapeDtypeStruct
    return pl.pallas_call(
        body, grid=(nc,), name="pool_bwd",
        in_specs=[chunk(D_POOL), chunk(D_POOL), _full((4, 128, 128)), _full((8, D_POOL))],
        out_specs=[chunk(D_POOL), _full((4, 128, 128)), _full((8, D_POOL))],
        out_shape=[sds((n_pad, D_POOL), F32), sds((4, 128, 128), F32), sds((8, D_POOL), F32)],
        scratch_shapes=[pltpu.VMEM((POOL_HALO, D_POOL), F32)],
        compiler_params=_params(("arbitrary",)),
    )(dmp, feat, pw_b, vecs)


def _place():
    x, y, c = lax.axis_index("x"), lax.axis_index("y"), lax.axis_index("c")
    chips = [(1 - x, y), (x, 1 - y), (1 - x, 1 - y)]
    return x, y, c, chips


_ANY = pl.BlockSpec(memory_space=pl.ANY)


def _cast_shards(shards, dtypes, place):
    n = len(shards)

    def body(place_ref, *refs):
        for i in range(n):
            refs[n + i][0] = refs[i][...].astype(dtypes[i])

    return pl.pallas_call(
        body, name="cast_shards",
        grid_spec=pltpu.PrefetchScalarGridSpec(
            num_scalar_prefetch=1, grid=(1,),
            in_specs=[pl.BlockSpec(s.shape, lambda i, p: (0, 0, 0)) for s in shards],
            out_specs=[pl.BlockSpec((1,) + s.shape, lambda i, p: (p[0], 0, 0, 0)) for s in shards]),
        out_shape=[jax.ShapeDtypeStruct((N_SHARD,) + s.shape, dt) for s, dt in zip(shards, dtypes)],
        compiler_params=_params(("arbitrary",)),
    )(place, *shards)


def _gather_shards(full):
    n = len(full)

    def body(*refs):
        outs = refs[n:2 * n]
        ici_send, ici_recv, d2d_send, d2d_recv = refs[2 * n:]
        x, y, c, chips = _place()
        q = 2 * x + y
        sibling = (x, y, 1 - c)

        def ici(i, j, shard, to):
            return pltpu.make_async_remote_copy(src_ref=outs[i].at[q, c], dst_ref=outs[i].at[shard, c],
                                                send_sem=ici_send.at[i, j], recv_sem=ici_recv.at[i, j],
                                                device_id=to, device_id_type=MESH)

        def d2d(i, j, shard, half):
            return pltpu.make_async_remote_copy(src_ref=outs[i].at[shard, c], dst_ref=outs[i].at[shard, half],
                                                send_sem=d2d_send.at[i, j], recv_sem=d2d_recv.at[i, j],
                                                device_id=sibling, device_id_type=MESH)

        sends = [ici(i, j, q, (*chip, c)) for i in range(n) for j, chip in enumerate(chips)]
        for cp in sends:
            cp.start()
        passed = []
        for i in range(n):
            for j, (cx, cy) in enumerate(chips):
                ici(i, j, 2 * cx + cy, (cx, cy, c)).wait_recv()
                cp = d2d(i, j, 2 * cx + cy, c)
                cp.start()
                passed.append(cp)
        for i in range(n):
            for j, (cx, cy) in enumerate(chips):
                d2d(i, j, 2 * cx + cy, 1 - c).wait_recv()
        for cp in sends + passed:
            cp.wait_send()

    return pl.pallas_call(
        body, name="gather_shards",
        in_specs=[_ANY] * n, out_specs=[_ANY] * n,
        out_shape=[jax.ShapeDtypeStruct(f.shape, f.dtype) for f in full],
        input_output_aliases={i: i for i in range(n)},
        scratch_shapes=[pltpu.SemaphoreType.DMA((n, 3)), pltpu.SemaphoreType.DMA((n, 3)),
                        pltpu.SemaphoreType.DMA((n, 3)), pltpu.SemaphoreType.DMA((n, 3))],
    )(*full)


def _forward_halves(full):
    n = len(full)

    def body(*refs):
        outs = refs[n:2 * n]
        send, recv = refs[2 * n:]
        x, y, c, chips = _place()

        def d2d(i, j, shard, half):
            return pltpu.make_async_remote_copy(src_ref=outs[i].at[shard, c], dst_ref=outs[i].at[shard, half],
                                                send_sem=send.at[i, j], recv_sem=recv.at[i, j],
                                                device_id=(x, y, 1 - c), device_id_type=MESH)

        cps = [d2d(i, j, 2 * cx + cy, c) for i in range(n) for j, (cx, cy) in enumerate(chips)]
        for cp in cps:
            cp.start()
        for i in range(n):
            for j, (cx, cy) in enumerate(chips):
                d2d(i, j, 2 * cx + cy, 1 - c).wait_recv()
        for cp in cps:
            cp.wait_send()

    return pl.pallas_call(
        body, name="forward_halves",
        in_specs=[_ANY] * n, out_specs=[_ANY] * n,
        out_shape=[jax.ShapeDtypeStruct(f.shape, f.dtype) for f in full],
        input_output_aliases={i: i for i in range(n)},
        scratch_shapes=[pltpu.SemaphoreType.DMA((n, 3)), pltpu.SemaphoreType.DMA((n, 3))],
    )(*full)


_HBM = pl.BlockSpec(memory_space=pltpu.HBM)
_SEM = pl.BlockSpec(memory_space=pltpu.SEMAPHORE)
_EFFECT = pltpu.SideEffectType.DATAFLOW_SIDE_EFFECTING


def _copies_start(name, arrays, sem_shape, build, after=None):
    n = len(arrays)
    extra = [] if after is None else [after]

    def body(*refs):
        outs = refs[n + len(extra):2 * n + len(extra)]
        send, recv, token = refs[2 * n + len(extra):]
        sends, _ = build(outs, send, recv)
        for cp in sends:
            cp.start()
        token[...] = jnp.zeros_like(token)

    out = pl.pallas_call(
        body, name=name, in_specs=[_HBM] * n + [_ANY] * len(extra),
        out_specs=[_HBM] * n + [_SEM, _SEM, pl.BlockSpec(memory_space=pltpu.VMEM)],
        out_shape=[pltpu.HBM(a.shape, a.dtype) for a in arrays]
        + [pltpu.SemaphoreType.DMA(sem_shape), pltpu.SemaphoreType.DMA(sem_shape), jax.ShapeDtypeStruct((8, 128), F32)],
        input_output_aliases={i: i for i in range(n)},
        compiler_params=pltpu.CompilerParams(has_side_effects=_EFFECT),
    )(*[pltpu.with_memory_space_constraint(a, pltpu.HBM) for a in arrays], *extra)
    return list(out[:n]), (out[n], out[n + 1]), out[n + 2]


def _copies_wait(name, arrays, sems, after, build):
    n = len(arrays)

    def body(*refs):
        ins = refs[:n]
        send, recv = refs[n], refs[n + 1]
        sends, recvs = build(ins, send, recv)
        for cp in sends:
            cp.wait_send()
        for cp in recvs:
            cp.wait_recv()

    return list(pl.pallas_call(
        body, name=name, in_specs=[_HBM] * n + [_SEM, _SEM, _ANY], out_specs=[_HBM] * n,
        out_shape=[pltpu.HBM(a.shape, a.dtype) for a in arrays],
        input_output_aliases={i: i for i in range(n)},
        compiler_params=pltpu.CompilerParams(has_side_effects=_EFFECT),
    )(*arrays, *sems, after))


def _remote(src, dst, send_sem, recv_sem, to):
    return pltpu.make_async_remote_copy(src_ref=src, dst_ref=dst, send_sem=send_sem, recv_sem=recv_sem,
                                        device_id=to, device_id_type=MESH)


def _build_gather(refs, send, recv):
    x, y, c, chips = _place()
    q = 2 * x + y
    pairs = [(i, j, chip) for i in range(len(refs)) for j, chip in enumerate(chips)]
    sends = [_remote(refs[i].at[q, c], refs[i].at[q, c], send.at[3 * i + j], recv.at[3 * i + j], (cx, cy, c))
             for i, j, (cx, cy) in pairs]
    recvs = [_remote(refs[i].at[q, c], refs[i].at[2 * cx + cy, c], send.at[3 * i + j], recv.at[3 * i + j], (cx, cy, c))
             for i, j, (cx, cy) in pairs]
    return sends, recvs


def _build_swap(refs, send, recv):
    x, y, c, _ = _place()
    n = len(refs) // 2
    cps = [_remote(refs[i].at[:, 1 - c], refs[n + i], send.at[i], recv.at[i], (x, y, 1 - c)) for i in range(n)]
    return cps, cps


def _build_exchange(refs, send, recv):
    x, y, c, chips = _place()
    n = len(refs) // 2
    cps = [_remote(refs[i].at[2 * cx + cy], refs[n + i].at[j], send.at[3 * i + j], recv.at[3 * i + j], (cx, cy, c))
           for i in range(n) for j, (cx, cy) in enumerate(chips)]
    return cps, cps


def _build_spread(refs, send, recv):
    x, y, c, _ = _place()
    flip = lambda bit, on: bit + on - 2 * bit * on
    cps = [_remote(refs[0], refs[1].at[r - 1], send.at[r - 1], recv.at[r - 1],
                   (flip(x, r >> 2 & 1), flip(y, r >> 1 & 1), flip(c, r & 1))) for r in (1, 2, 4, 6)]
    return cps, cps


def _forward_small(landed):
    def body(in_ref, out_ref, send, recv):
        x, y, c, _ = _place()
        cps = [_remote(out_ref.at[r - 1], out_ref.at[r], send.at[k], recv.at[k], (x, y, 1 - c))
               for k, r in enumerate((2, 4, 6))]
        for cp in cps:
            cp.start()
        for cp in cps:
            cp.wait()

    return pl.pallas_call(
        body, name="forward_small",
        in_specs=[_ANY], out_specs=_ANY, out_shape=jax.ShapeDtypeStruct(landed.shape, F32),
        input_output_aliases={0: 0},
        scratch_shapes=[pltpu.SemaphoreType.DMA((3,)), pltpu.SemaphoreType.DMA((3,))],
    )(landed)


def _swap_halves(grads):
    n = len(grads)

    def body(*refs):
        ins, outs = refs[:n], refs[n:2 * n]
        send, recv = refs[2 * n:]
        x, y, c, _ = _place()
        cps = [pltpu.make_async_remote_copy(src_ref=ins[i].at[:, 1 - c], dst_ref=outs[i], send_sem=send.at[i],
                                            recv_sem=recv.at[i], device_id=(x, y, 1 - c), device_id_type=MESH)
               for i in range(n)]
        for cp in cps:
            cp.start()
        for cp in cps:
            cp.wait()

    return pl.pallas_call(
        body, name="swap_halves",
        in_specs=[_ANY] * n, out_specs=[_ANY] * n,
        out_shape=[jax.ShapeDtypeStruct((N_SHARD,) + g.shape[2:], F32) for g in grads],
        scratch_shapes=[pltpu.SemaphoreType.DMA((n,)), pltpu.SemaphoreType.DMA((n,))],
    )(*grads)


def _join_halves(pairs):
    n = len(pairs)

    def body(*refs):
        outs = refs[n:2 * n]
        send, recv = refs[2 * n:]
        x, y, c, _ = _place()
        cps = [pltpu.make_async_remote_copy(src_ref=outs[i].at[c], dst_ref=outs[i].at[c], send_sem=send.at[i],
                                            recv_sem=recv.at[i], device_id=(x, y, 1 - c), device_id_type=MESH)
               for i in range(n)]
        for cp in cps:
            cp.start()
        for i in range(n):
            cps[i].wait_send()
            pltpu.make_async_remote_copy(src_ref=outs[i].at[c], dst_ref=outs[i].at[1 - c], send_sem=send.at[i],
                                         recv_sem=recv.at[i], device_id=(x, y, 1 - c), device_id_type=MESH).wait_recv()

    return pl.pallas_call(
        body, name="join_halves",
        in_specs=[_ANY] * n, out_specs=[_ANY] * n,
        out_shape=[jax.ShapeDtypeStruct(p.shape, F32) for p in pairs],
        input_output_aliases={i: i for i in range(n)},
        scratch_shapes=[pltpu.SemaphoreType.DMA((n,)), pltpu.SemaphoreType.DMA((n,))],
    )(*pairs)


N_SPLIT = 2


def _sum_siblings(tag, grads, recvd, place):
    n = len(grads)

    def body(place_ref, *refs):
        g_refs, r_refs, sb_refs, own_refs = (refs[k * n:(k + 1) * n] for k in range(4))
        s = pl.program_id(1)
        for i in range(n):
            tot = g_refs[i][0, 0] + r_refs[i][0]
            sb_refs[i][0] = tot.astype(BF16)

            @pl.when(s == place_ref[0])
            def _():
                own_refs[i][...] = tot

    in_specs, sb_specs, own_specs, sb_shapes, own_shapes = [], [], [], [], []
    for g in grads:
        _, _, r, cdim = g.shape
        rb = r // N_SPLIT
        in_specs.append(pl.BlockSpec((1, 1, rb, cdim), lambda b, s, p: (s, p[1], b, 0)))
        sb_specs.append(pl.BlockSpec((1, rb, cdim), lambda b, s, p: (s, b, 0)))
        own_specs.append(pl.BlockSpec((rb, cdim), lambda b, s, p: (b, 0)))
        sb_shapes.append(jax.ShapeDtypeStruct((N_SHARD, r, cdim), BF16))
        own_shapes.append(jax.ShapeDtypeStruct((r, cdim), F32))
    out = pl.pallas_call(
        body, name="sum_siblings_" + tag,
        grid_spec=pltpu.PrefetchScalarGridSpec(
            num_scalar_prefetch=1, grid=(N_SPLIT, N_SHARD),
            in_specs=in_specs + sb_specs, out_specs=sb_specs + own_specs),
        out_shape=sb_shapes + own_shapes,
        compiler_params=_params(("parallel", "arbitrary")),
    )(place, *grads, *recvd)
    return out[:n], out[n:]


def _sum_chips(own, recvd, place):
    n = len(own)

    def body(place_ref, *refs):
        o_refs, r_refs, out_refs = (refs[k * n:(k + 1) * n] for k in range(3))
        for i in range(n):
            tot = o_refs[i][...]
            for j in range(3):
                tot = tot + r_refs[i][j].astype(F32)
            out_refs[i][0] = tot

    o_specs, r_specs, out_specs = [], [], []
    for o in own:
        r, cdim = o.shape
        rb = r // N_SPLIT
        o_specs.append(pl.BlockSpec((rb, cdim), lambda b, p: (b, 0)))
        r_specs.append(pl.BlockSpec((3, rb, cdim), lambda b, p: (0, b, 0)))
        out_specs.append(pl.BlockSpec((1, rb, cdim), lambda b, p: (p[1], b, 0)))
    return pl.pallas_call(
        body, name="sum_chips",
        grid_spec=pltpu.PrefetchScalarGridSpec(num_scalar_prefetch=1, grid=(N_SPLIT,),
                                               in_specs=o_specs + r_specs, out_specs=out_specs),
        out_shape=[jax.ShapeDtypeStruct((2,) + o.shape, F32) for o in own],
        compiler_params=_params(("parallel",)),
    )(place, *own, *recvd)


def _adamw_math(w, g, m, v):
    m = ADAM_B1 * m + (1.0 - ADAM_B1) * g
    v = ADAM_B2 * v + (1.0 - ADAM_B2) * (g * g)
    m_hat = m / (1.0 - ADAM_B1 ** ADAM_STEP)
    v_hat = v / (1.0 - ADAM_B2 ** ADAM_STEP)
    delta = -ADAM_LR * (m_hat / (jnp.sqrt(v_hat) + ADAM_EPS) + ADAM_WD * w)
    return delta, m, v


def _adamw(name, ws, gs, ms, vs, n_split):
    n = len(ws)

    def body(*refs):
        w_r, g_r, m_r, v_r, d_o, m_o, v_o = (refs[k * n:(k + 1) * n] for k in range(7))
        for i in range(n):
            d, m, v = _adamw_math(w_r[i][...], g_r[i][...], m_r[i][...], v_r[i][...])
            d_o[i][...] = d
            m_o[i][...] = m
            v_o[i][...] = v

    specs = [pl.BlockSpec((w.shape[0] // n_split, w.shape[1]), lambda b: (b, 0)) for w in ws]
    shapes = [jax.ShapeDtypeStruct(w.shape, F32) for w in ws]
    out = pl.pallas_call(
        body, name=name, grid=(n_split,),
        in_specs=specs * 4, out_specs=specs * 3, out_shape=shapes * 3,
        compiler_params=_params(("parallel",)),
    )(*ws, *gs, *ms, *vs)
    return out[:n], out[n:2 * n], out[2 * n:]


def _reduce_small(own, received, w, m, v):
    def body(own_ref, recv_ref, w_ref, m_ref, v_ref, g_out, d_out, m_out, v_out):
        me = 4 * lax.axis_index("x") + 2 * lax.axis_index("y") + lax.axis_index("c")
        g = None
        for k in range(8):
            mine = me == k
            part = jnp.where(mine, own_ref[...], recv_ref[jnp.where(mine, 0, jnp.bitwise_xor(me, k) - 1)])
            g = part if g is None else g + part
        g_out[...] = g
        d, mm, vv = _adamw_math(w_ref[...], g, m_ref[...], v_ref[...])
        d_out[...] = d
        m_out[...] = mm
        v_out[...] = vv

    return pl.pallas_call(
        body, name="reduce_small",
        out_shape=[jax.ShapeDtypeStruct(w.shape, F32)] * 4,
        compiler_params=_params(),
    )(own, received, w, m, v)


def _s5_operands(lam_re, lam_im, log_step, b_re, b_im, c_re, c_im, glu_w):
    lr = jnp.minimum(lam_re, -1e-4)
    li = lam_im
    step = jnp.exp(log_step)[:, None]
    mag = jnp.exp(lr * step)
    ang = li * step
    abr = mag * jnp.cos(ang)
    abi = mag * jnp.sin(ang)
    nr = abr - 1.0
    ni = abi
    den = lr * lr + li * li
    cr = ((nr * lr + ni * li) / den)[..., None]
    ci = ((ni * lr - nr * li) / den)[..., None]
    bbr = cr * b_re - ci * b_im
    bbi = cr * b_im + ci * b_re
    eye = jnp.eye(8, dtype=F32)
    g, h, p = SSM_GROUPS // SUPER, SSM_GROUP, SSM_STATE

    def b_layout(t):
        return jnp.einsum("ab,japh->jahbp", eye, t.reshape(SUPER, g, p, h)).reshape(SUPER, g * h, g * p)

    def c_layout(t):
        return jnp.einsum("ab,jahp->jbpah", eye, t.reshape(SUPER, g, h, p)).reshape(SUPER, g * p, g * h)

    glu = jnp.einsum("ab,jahk->jahbk", eye, glu_w.reshape(SUPER, g, h, h)).reshape(SUPER, g * h, g * h)
    lam = _pad_rows(jnp.concatenate([abr.reshape(1, N_STATE), abi.reshape(1, N_STATE)], axis=0), 8)
    return lam, b_layout(bbr), b_layout(bbi), c_layout(c_re), c_layout(c_im), glu


def _pad_rows(a, rows):
    return jnp.pad(a, ((0, rows - a.shape[0]), (0, 0)))


def _pack(parts):
    rows = []
    for a in parts:
        flat = a.reshape(-1)
        n = -(-flat.shape[0] // 128)
        rows.append(jnp.pad(flat, (0, n * 128 - flat.shape[0])).reshape(n, 128))
    out = jnp.concatenate(rows, axis=0)
    return _pad_rows(out, -(-out.shape[0] // 8) * 8)


def _unpack(packed, like):
    out, at = [], 0
    for a in like:
        n = -(-a.size // 128)
        out.append(packed[at:at + n].reshape(-1)[:a.size].reshape(a.shape))
        at += n
    return out


SMALL = ("norm1_g", "ssm_lambda_re", "ssm_lambda_im", "ssm_log_step", "ssm_b_re", "ssm_b_im", "ssm_c_re", "ssm_c_im",
         "ssm_d", "ssm_glu_w", "ssm_glu_b", "ssm_norm_g", "pool_w", "pool_scale", "pool_norm_g", "norm2_g",
         "final_norm_g")
LARGE = ("w_in", "w_out", "w_gate", "w_up", "w_down")
WEIGHTS = ("meta_tokens", "norm1_g", "w_in", "ssm_lambda_re", "ssm_lambda_im", "ssm_log_step", "ssm_b_re", "ssm_b_im",
           "ssm_c_re", "ssm_c_im", "ssm_d", "ssm_glu_w", "ssm_glu_b", "ssm_norm_g", "pool_w", "pool_scale",
           "pool_norm_g", "w_out", "norm2_g", "w_gate", "w_up", "w_down", "final_norm_g")


def _step(x, target, w, m, v):
    seq = x.shape[1]
    n_rows = N_META + seq
    n_pad, tm, tc, tg = _plan(n_rows)
    xq, yq, cq = lax.axis_index("x"), lax.axis_index("y"), lax.axis_index("c")
    place = jnp.stack([2 * xq + yq, cq]).astype(jnp.int32)

    def halves(a2d):
        return a2d.reshape(2, a2d.shape[0] // 2, a2d.shape[1])

    def local2d(t):
        return {"w_gate": lambda a: a[0].T, "w_up": lambda a: a[0].T}.get(t, lambda a: a[0])

    shards = [halves(local2d(k)(w[k])) for k in LARGE] + [halves(w["meta_tokens"])]
    full = _cast_shards(shards, [BF16] * len(LARGE) + [F32], place)
    w_in_full, meta_full = _gather_shards([full[0], full[5]])
    late, gather_sems, gather_token = _copies_start("gather_start", list(full[1:5]), (12,), _build_gather)
    w_in_b = w_in_full.reshape(D_MODEL, D_MODEL)
    meta = meta_full.reshape(N_SHARD, N_META, D_MODEL // N_SHARD).transpose(1, 0, 2).reshape(N_META, D_MODEL)

    h0 = _pad_rows(jnp.concatenate([meta, x[0]], axis=0), n_pad)
    tgt = _pad_rows(jnp.concatenate([jnp.zeros((N_META, D_MODEL), F32), target[0]], axis=0), n_pad)
    s5_in = (w["ssm_lambda_re"][0], w["ssm_lambda_im"][0], w["ssm_log_step"][0], w["ssm_b_re"][0], w["ssm_b_im"][0],
             w["ssm_c_re"][0], w["ssm_c_im"][0], w["ssm_glu_w"][0])
    (lam, bbr, bbi, crt, cit, glu), s5_vjp = jax.vjp(_s5_operands, *s5_in)
    bbr_b, bbi_b, crt_b, cit_b, glu_b16 = (t.astype(BF16) for t in (bbr, bbi, crt, cit, glu))
    s5_vecs = _pad_rows(jnp.concatenate([w["ssm_d"].reshape(1, D_SSM), w["ssm_glu_b"].reshape(1, D_SSM),
                                         w["ssm_norm_g"].reshape(1, D_SSM)], axis=0), 8)
    pool_vecs = _pad_rows(jnp.concatenate([w["pool_scale"].reshape(1, D_POOL), w["pool_norm_g"].reshape(1, D_POOL)],
                                          axis=0), 8)
    pw_b = w["pool_w"][0].astype(BF16)
    g1, g2, gf = w["norm1_g"].reshape(1, D_MODEL), w["norm2_g"].reshape(1, D_MODEL), w["final_norm_g"].reshape(1, D_MODEL)

    u, vv = _fwd_in(h0, g1, w_in_b, tm, gather_token)
    sr, si, y, ms = _s5_fwd(u, lam, bbr_b, bbi_b, crt_b, cit_b, s5_vecs, glu_b16, tc)
    feat, mp = _pool_fwd(vv, pw_b, pool_vecs, tc)
    late = _forward_halves(_copies_wait("gather_wait", late, gather_sems, mp, _build_gather))
    w_out_b = late[0].reshape(D_MODEL, D_MODEL)
    wg_b, wu_b, wd_b = (t.reshape(N_SHARD, FF_SHARD, D_MODEL) for t in late[1:])
    h1, n2, a, b, ff, dh2, dh2b, loss_acc, dgf = _fwd_ffn(h0, ms, mp, w_out_b, g2, wg_b, wu_b, wd_b, gf, tgt, tm, n_rows)

    def quarters(t):
        if t.ndim == 2:
            t = t.reshape(N_SHARD, t.shape[0] // N_SHARD, t.shape[1])
        return t.reshape(N_SHARD, 2, t.shape[1] // 2, t.shape[2])

    def landing(like, lead, dtype):
        return [lax.empty((lead,) + t.shape[2:], dtype) for t in like]

    da, db, dh1, dg2 = _bwd_ffn(dh2, a, b, wg_b, wu_b, wd_b, h1, g2, tm)
    ffn_g = [quarters(t) for t in _grad_ffn(n2, da, db, ff, dh2b, tg)]
    nf = len(ffn_g)
    moved, swap_sems, swap_token = _copies_start("swap_start", ffn_g + landing(ffn_g, N_SHARD, F32), (nf,), _build_swap)
    dms, dmp, dwo = _bwd_out(dh1, ms, mp, w_out_b, tm, swap_token)
    moved = _copies_wait("swap_wait", moved, swap_sems, dwo, _build_swap)
    ffn_parts, ffn_own = _sum_siblings("ffn", moved[:nf], moved[nf:], place)
    moved, exch_sems, exch_token = _copies_start("exchange_start", list(ffn_parts) + landing(ffn_g, 3, BF16), (3 * nf,),
                                                 _build_exchange)
    du, dbbr, dbbi, dcrt, dcit, dglu, ds5v, dlam = _s5_bwd(dms, y, u, sr, si, lam, bbr_b, bbi_b, crt_b, cit_b,
                                                           s5_vecs, glu_b16, tc, exch_token)
    dv, dpw, dpoolv = _pool_bwd(dmp, feat, pw_b, pool_vecs, tc)
    dh0, dwi, dg1 = _bwd_in(du, dv, h0, dh1, g1, w_in_b, tm)
    ffn_from_chips = _copies_wait("exchange_wait", moved, exch_sems, dh0, _build_exchange)[nf:]
    dlam = _pad_rows(jnp.concatenate([jnp.sum(dlam[:SEGMENTS], axis=0, keepdims=True),
                                      jnp.sum(dlam[SEGMENTS:], axis=0, keepdims=True)], axis=0), 8)
    d_lre, d_lim, d_lstep, d_bre, d_bim, d_cre, d_cim, d_gluw = s5_vjp((dlam, dbbr, dbbi, dcrt, dcit, dglu))
    grad_x = dh0[N_META:n_rows][None]

    small_g = {
        "norm1_g": dg1, "ssm_lambda_re": d_lre, "ssm_lambda_im": d_lim, "ssm_log_step": d_lstep, "ssm_b_re": d_bre,
        "ssm_b_im": d_bim, "ssm_c_re": d_cre, "ssm_c_im": d_cim, "ssm_d": ds5v[0], "ssm_glu_w": d_gluw,
        "ssm_glu_b": ds5v[1], "ssm_norm_g": ds5v[2], "pool_w": dpw, "pool_scale": dpoolv[0], "pool_norm_g": dpoolv[1],
        "norm2_g": dg2, "final_norm_g": dgf,
    }
    like = [w[k] for k in SMALL]
    packed_g = _pack([small_g[k].reshape(w[k].shape) for k in SMALL] + [dh0[:N_META], loss_acc[0:1, 0:1]])
    rows = packed_g.shape[0]
    packed = lambda t: _pad_rows(_pack([t[k] for k in SMALL]), rows)

    mix_g = [quarters(t) for t in (dwi, dwo)]
    mix_parts, mix_own = _sum_siblings("mix", mix_g, _swap_halves(mix_g), place)
    moved, mix_sems, mix_token = _copies_start("mix_exchange_start", list(mix_parts) + landing(mix_g, 3, BF16),
                                               (3 * len(mix_g),), _build_exchange)
    spread, small_sems, small_token = _copies_start(
        "small_start", [packed_g, lax.empty((7,) + packed_g.shape, F32)], (7,), _build_spread, after=mix_token)
    mix_from_chips = _copies_wait("mix_exchange_wait", moved, mix_sems, small_token, _build_exchange)[len(mix_g):]
    joined = _join_halves(_sum_chips(list(mix_own) + list(ffn_own), list(mix_from_chips) + list(ffn_from_chips), place))
    g_large = [j.reshape(j.shape[0] * j.shape[1], j.shape[2]) for j in joined]
    w2d, m2d, v2d = ([local2d(k)(t[k]) for k in LARGE] for t in (w, m, v))
    d_large, m_large, v_large = _adamw("adamw_large", w2d, g_large, m2d, v2d, 8)

    own_g, landed = _copies_wait("small_wait", spread, small_sems, d_large[0], _build_spread)
    g_pk, d_pk, m_pk, v_pk = _reduce_small(own_g, _forward_small(landed), packed(w), packed(m), packed(v))
    g_small = _unpack(g_pk, like + [jax.ShapeDtypeStruct((N_META, D_MODEL), F32), jax.ShapeDtypeStruct((1, 1), F32)])
    loss = g_small.pop()[0, 0]
    d_small, m_small, v_small = (_unpack(t, like) for t in (d_pk, m_pk, v_pk))
    q = place[0]
    g_meta = lax.dynamic_slice_in_dim(g_small[-1], q * (D_MODEL // N_SHARD), D_MODEL // N_SHARD, axis=1)
    d_meta, m_meta, v_meta = _adamw("adamw_meta", [w["meta_tokens"]], [g_meta], [m["meta_tokens"]],
                                    [v["meta_tokens"]], 1)

    grads, deltas, new_m, new_v = {}, {}, {}, {}
    for i, k in enumerate(SMALL):
        grads[k], deltas[k], new_m[k], new_v[k] = g_small[i], d_small[i], m_small[i], v_small[i]
    for i, k in enumerate(LARGE):
        back = (lambda t: t.T[None]) if k in ("w_gate", "w_up") else (lambda t: t[None])
        grads[k], deltas[k], new_m[k], new_v[k] = (back(t) for t in (g_large[i], d_large[i], m_large[i], v_large[i]))
    grads["meta_tokens"], deltas["meta_tokens"] = g_meta, d_meta[0]
    new_m["meta_tokens"], new_v["meta_tokens"] = m_meta[0], v_meta[0]
    return (loss, grad_x, *[grads[k] for k in WEIGHTS], *[deltas[k] for k in WEIGHTS],
            *[new_m[k] for k in WEIGHTS], *[new_v[k] for k in WEIGHTS])


def kernel(x, meta_tokens, norm1_g, w_in, ssm_lambda_re, ssm_lambda_im, ssm_log_step, ssm_b_re, ssm_b_im, ssm_c_re, ssm_c_im, ssm_d, ssm_glu_w, ssm_glu_b, ssm_norm_g, pool_w, pool_scale, pool_norm_g, w_out, norm2_g, w_gate, w_up, w_down, final_norm_g, loss_target, m_meta_tokens, m_norm1_g, m_w_in, m_ssm_lambda_re, m_ssm_lambda_im, m_ssm_log_step, m_ssm_b_re, m_ssm_b_im, m_ssm_c_re, m_ssm_c_im, m_ssm_d, m_ssm_glu_w, m_ssm_glu_b, m_ssm_norm_g, m_pool_w, m_pool_scale, m_pool_norm_g, m_w_out, m_norm2_g, m_w_gate, m_w_up, m_w_down, m_final_norm_g, v_meta_tokens, v_norm1_g, v_w_in, v_ssm_lambda_re, v_ssm_lambda_im, v_ssm_log_step, v_ssm_b_re, v_ssm_b_im, v_ssm_c_re, v_ssm_c_im, v_ssm_d, v_ssm_glu_w, v_ssm_glu_b, v_ssm_norm_g, v_pool_w, v_pool_scale, v_pool_norm_g, v_w_out, v_norm2_g, v_w_gate, v_w_up, v_w_down, v_final_norm_g):
    w = dict(meta_tokens=meta_tokens, norm1_g=norm1_g, w_in=w_in, ssm_lambda_re=ssm_lambda_re, ssm_lambda_im=ssm_lambda_im, ssm_log_step=ssm_log_step, ssm_b_re=ssm_b_re, ssm_b_im=ssm_b_im, ssm_c_re=ssm_c_re, ssm_c_im=ssm_c_im, ssm_d=ssm_d, ssm_glu_w=ssm_glu_w, ssm_glu_b=ssm_glu_b, ssm_norm_g=ssm_norm_g, pool_w=pool_w, pool_scale=pool_scale, pool_norm_g=pool_norm_g, w_out=w_out, norm2_g=norm2_g, w_gate=w_gate, w_up=w_up, w_down=w_down, final_norm_g=final_norm_g)
    m = dict(meta_tokens=m_meta_tokens, norm1_g=m_norm1_g, w_in=m_w_in, ssm_lambda_re=m_ssm_lambda_re, ssm_lambda_im=m_ssm_lambda_im, ssm_log_step=m_ssm_log_step, ssm_b_re=m_ssm_b_re, ssm_b_im=m_ssm_b_im, ssm_c_re=m_ssm_c_re, ssm_c_im=m_ssm_c_im, ssm_d=m_ssm_d, ssm_glu_w=m_ssm_glu_w, ssm_glu_b=m_ssm_glu_b, ssm_norm_g=m_ssm_norm_g, pool_w=m_pool_w, pool_scale=m_pool_scale, pool_norm_g=m_pool_norm_g, w_out=m_w_out, norm2_g=m_norm2_g, w_gate=m_w_gate, w_up=m_w_up, w_down=m_w_down, final_norm_g=m_final_norm_g)
    v = dict(meta_tokens=v_meta_tokens, norm1_g=v_norm1_g, w_in=v_w_in, ssm_lambda_re=v_ssm_lambda_re, ssm_lambda_im=v_ssm_lambda_im, ssm_log_step=v_ssm_log_step, ssm_b_re=v_ssm_b_re, ssm_b_im=v_ssm_b_im, ssm_c_re=v_ssm_c_re, ssm_c_im=v_ssm_c_im, ssm_d=v_ssm_d, ssm_glu_w=v_ssm_glu_w, ssm_glu_b=v_ssm_glu_b, ssm_norm_g=v_ssm_norm_g, pool_w=v_pool_w, pool_scale=v_pool_scale, pool_norm_g=v_pool_norm_g, w_out=v_w_out, norm2_g=v_norm2_g, w_gate=v_w_gate, w_up=v_w_up, w_down=v_w_down, final_norm_g=v_final_norm_g)
    return _step(x, loss_target, w, m, v)
```

```python
import functools
import math

import jax
import jax.numpy as jnp
from jax import lax
from jax.experimental import pallas as pl
from jax.experimental.pallas import tpu as pltpu

F32 = jnp.float32
BF16 = jnp.bfloat16
MESH = pl.DeviceIdType.MESH
AXES = ("x", "y", "c")

D_MODEL = 1024
D_SSM = 512
D_POOL = 512
N_META = 16
SSM_GROUP = 16
SSM_GROUPS = 32
SSM_STATE = 64
N_STATE = SSM_GROUPS * SSM_STATE
STATE_BLOCKS = N_STATE // 128
SUPER = 4
POOL_WINDOWS = (2, 4, 8, 16)
POOL_HALO = 16
D_FF = 2816
N_SHARD = 4
FF_SHARD = D_FF // N_SHARD
EPS = 1e-6
ADAM_LR, ADAM_B1, ADAM_B2, ADAM_EPS, ADAM_WD, ADAM_STEP = 0.001, 0.9, 0.999, 1e-08, 0.01, 10
VMEM_LIMIT = 56 * 1024 * 1024


def _plan(n_rows):
    if n_rows > 2048:
        tm, tc, tg = 416, 320, 1040
    else:
        tm, tc, tg = 128, 64, 128
    step = math.lcm(tm, tc, tg)
    return -(-n_rows // step) * step, tm, tc, tg


def _params(sem=None):
    return pltpu.CompilerParams(dimension_semantics=sem, vmem_limit_bytes=VMEM_LIMIT)


def _dot(a, b):
    return jnp.dot(a, b, preferred_element_type=F32)


def _dot_nt(a, b):
    return lax.dot_general(a, b, (((1,), (1,)), ((), ())), preferred_element_type=F32)


def _dot_tn(a, b):
    return lax.dot_general(a, b, (((0,), (0,)), ((), ())), preferred_element_type=F32)


def _sigmoid(x):
    return 1.0 / (1.0 + jnp.exp(-x))


_GELU_C = math.sqrt(2.0 / math.pi)


def _gelu_and_grad(y):
    y2 = y * y
    t = jnp.tanh(_GELU_C * (y + 0.044715 * y * y2))
    g = 0.5 * y * (1.0 + t)
    dg = 0.5 * (1.0 + t) + 0.5 * y * (1.0 - t * t) * (_GELU_C * (1.0 + 3.0 * 0.044715 * y2))
    return g, dg


def _rms(x):
    return lax.rsqrt(jnp.mean(x * x, axis=-1, keepdims=True) + EPS)


def _rms_bwd(dn, xhat, r):
    return r * (dn - xhat * jnp.mean(dn * xhat, axis=-1, keepdims=True))


def _full(shape):
    nd = len(shape)
    return pl.BlockSpec(shape, lambda *_: (0,) * nd)


def _fwd_in(h0, g1, w_in_b, tm, token):
    n_pad = h0.shape[0]

    def body(h_ref, g_ref, w_ref, token_ref, u_ref, v_ref):
        h = h_ref[...]
        n1 = (h * _rms(h) * g_ref[...]).astype(BF16)
        proj = _dot(n1, w_ref[...])
        for i in range(4):
            u_ref[i] = proj[:, 128 * i:128 * (i + 1)]
        v_ref[...] = proj[:, D_SSM:]

    row = lambda w: pl.BlockSpec((tm, w), lambda i: (i, 0))
    return pl.pallas_call(
        body, grid=(n_pad // tm,), name="fwd_in",
        in_specs=[row(D_MODEL), _full((1, D_MODEL)), _full((D_MODEL, D_MODEL)), _ANY],
        out_specs=[pl.BlockSpec((4, tm, 128), lambda i: (0, i, 0)), row(D_POOL)],
        out_shape=[jax.ShapeDtypeStruct((4, n_pad, 128), F32), jax.ShapeDtypeStruct((n_pad, D_POOL), F32)],
        compiler_params=_params(("parallel",)),
    )(h0, g1, w_in_b, token)


def _fwd_ffn(h0, ms, mp, w_out_b, g2, wg_b, wu_b, wd_b, gf, target, tm, n_valid):
    n_pad = h0.shape[0]
    nt = n_pad // tm

    def body(h0_ref, ms_ref, mp_ref, wo_ref, g2_ref, wg_ref, wu_ref, wd_ref, gf_ref, tgt_ref,
             h1_ref, n2_ref, a_ref, b_ref, ff_ref, dh2_ref, dh2b_ref, loss_ref, dgf_ref, acc):
        i, q = pl.program_id(0), pl.program_id(1)

        @pl.when((i == 0) & (q == 0))
        def _():
            loss_ref[...] = jnp.zeros_like(loss_ref)
            dgf_ref[...] = jnp.zeros_like(dgf_ref)

        @pl.when(q == 0)
        def _():
            h1 = h0_ref[...] + _dot(ms_ref[...], wo_ref[:D_SSM, :]) + _dot(mp_ref[...], wo_ref[D_SSM:, :])
            h1_ref[...] = h1
            acc[...] = h1
            n2_ref[...] = (h1 * _rms(h1) * g2_ref[...]).astype(BF16)

        n2 = n2_ref[...]
        a = _dot_nt(n2, wg_ref[0])
        b = _dot_nt(n2, wu_ref[0])
        a_ref[0] = a.astype(BF16)
        b_ref[0] = b.astype(BF16)
        ff = (a * _sigmoid(a) * b).astype(BF16)
        ff_ref[0] = ff
        acc[...] += _dot(ff, wd_ref[0])

        @pl.when(q == N_SHARD - 1)
        def _():
            h2 = acc[...]
            r = _rms(h2)
            xhat = h2 * r
            gf_row = gf_ref[...]
            rows = i * tm + lax.broadcasted_iota(jnp.int32, (tm, 1), 0)
            valid = (rows >= N_META) & (rows < n_valid)
            diff = jnp.where(valid, xhat * gf_row - tgt_ref[...], 0.0)
            loss_ref[...] += jnp.full(loss_ref.shape, 0.5 / D_MODEL, F32) * jnp.sum(diff * diff)
            dout = diff * (1.0 / D_MODEL)
            dgf_ref[...] += jnp.sum(dout * xhat, axis=0, keepdims=True)
            dh2 = _rms_bwd(dout * gf_row, xhat, r)
            dh2_ref[...] = dh2
            dh2b_ref[...] = dh2.astype(BF16)

    row = lambda w: pl.BlockSpec((tm, w), lambda i, q: (i, 0))
    shard_rows = pl.BlockSpec((1, FF_SHARD, D_MODEL), lambda i, q: (q, 0, 0))
    act = pl.BlockSpec((1, tm, FF_SHARD), lambda i, q: (q, i, 0))
    sds = jax.ShapeDtypeStruct
    return pl.pallas_call(
        body, grid=(nt, N_SHARD), name="fwd_ffn",
        in_specs=[row(D_MODEL), row(D_SSM), row(D_POOL), _full((D_MODEL, D_MODEL)), _full((1, D_MODEL)),
                  shard_rows, shard_rows, shard_rows, _full((1, D_MODEL)), row(D_MODEL)],
        out_specs=[row(D_MODEL), row(D_MODEL), act, act, act, row(D_MODEL), row(D_MODEL), _full((8, 128)),
                   _full((1, D_MODEL))],
        out_shape=[sds((n_pad, D_MODEL), F32), sds((n_pad, D_MODEL), BF16),
                   sds((N_SHARD, n_pad, FF_SHARD), BF16), sds((N_SHARD, n_pad, FF_SHARD), BF16),
                   sds((N_SHARD, n_pad, FF_SHARD), BF16), sds((n_pad, D_MODEL), F32), sds((n_pad, D_MODEL), BF16),
                   sds((8, 128), F32), sds((1, D_MODEL), F32)],
        scratch_shapes=[pltpu.VMEM((tm, D_MODEL), F32)],
        compiler_params=_params(("arbitrary", "arbitrary")),
    )(h0, ms, mp, w_out_b, g2, wg_b, wu_b, wd_b, gf, target)


def _bwd_ffn(dh2, a, b, wg_b, wu_b, wd_b, h1, g2, tm):
    n_pad = dh2.shape[0]

    def body(dh2_ref, a_ref, b_ref, wg_ref, wu_ref, wd_ref, h1_ref, g2_ref, da_ref, db_ref, dh1_ref, dg2_ref, acc):
        i, q = pl.program_id(0), pl.program_id(1)

        @pl.when((i == 0) & (q == 0))
        def _():
            dg2_ref[...] = jnp.zeros_like(dg2_ref)

        dff = _dot_nt(dh2_ref[...].astype(BF16), wd_ref[0])
        a_v, b_v = a_ref[0].astype(F32), b_ref[0].astype(F32)
        sig = _sigmoid(a_v)
        da = (dff * b_v * sig * (1.0 + a_v * (1.0 - sig))).astype(BF16)
        db = (dff * a_v * sig).astype(BF16)
        da_ref[0] = da
        db_ref[0] = db
        part = _dot(da, wg_ref[0]) + _dot(db, wu_ref[0])

        @pl.when(q == 0)
        def _():
            acc[...] = part

        @pl.when(q > 0)
        def _():
            acc[...] += part

        @pl.when(q == N_SHARD - 1)
        def _():
            h1 = h1_ref[...]
            r = _rms(h1)
            xhat = h1 * r
            dn2 = acc[...]
            dg2_ref[...] += jnp.sum(dn2 * xhat, axis=0, keepdims=True)
            dh1_ref[...] = dh2_ref[...] + _rms_bwd(dn2 * g2_ref[...], xhat, r)

    row = lambda w: pl.BlockSpec((tm, w), lambda i, q: (i, 0))
    shard_rows = pl.BlockSpec((1, FF_SHARD, D_MODEL), lambda i, q: (q, 0, 0))
    act = pl.BlockSpec((1, tm, FF_SHARD), lambda i, q: (q, i, 0))
    sds = jax.ShapeDtypeStruct
    return pl.pallas_call(
        body, grid=(n_pad // tm, N_SHARD), name="bwd_ffn",
        in_specs=[row(D_MODEL), act, act, shard_rows, shard_rows, shard_rows, row(D_MODEL), _full((1, D_MODEL))],
        out_specs=[act, act, row(D_MODEL), _full((1, D_MODEL))],
        out_shape=[sds((N_SHARD, n_pad, FF_SHARD), BF16), sds((N_SHARD, n_pad, FF_SHARD), BF16),
                   sds((n_pad, D_MODEL), F32), sds((1, D_MODEL), F32)],
        scratch_shapes=[pltpu.VMEM((tm, D_MODEL), F32)],
        compiler_params=_params(("arbitrary", "arbitrary")),
    )(dh2, a, b, wg_b, wu_b, wd_b, h1, g2)


def _grad_ffn(n2, da, db, ff, dh2b, tm):
    n_pad = n2.shape[0]

    def body(n2_ref, da_ref, db_ref, ff_ref, dh2_ref, dwg_ref, dwu_ref, dwd_ref):
        i = pl.program_id(1)
        n2_v = n2_ref[...]
        gg = _dot_tn(da_ref[0], n2_v)
        gu = _dot_tn(db_ref[0], n2_v)
        gd = _dot_tn(ff_ref[0], dh2_ref[...])

        @pl.when(i == 0)
        def _():
            dwg_ref[0] = gg
            dwu_ref[0] = gu
            dwd_ref[0] = gd

        @pl.when(i > 0)
        def _():
            dwg_ref[0] += gg
            dwu_ref[0] += gu
            dwd_ref[0] += gd

    row = lambda w: pl.BlockSpec((tm, w), lambda q, i: (i, 0))
    act = pl.BlockSpec((1, tm, FF_SHARD), lambda q, i: (q, i, 0))
    sds = jax.ShapeDtypeStruct
    return pl.pallas_call(
        body, grid=(N_SHARD, n_pad // tm), name="grad_ffn",
        in_specs=[row(D_MODEL), act, act, act, row(D_MODEL)],
        out_specs=[pl.BlockSpec((1, FF_SHARD, D_MODEL), lambda q, i: (q, 0, 0))] * 3,
        out_shape=[sds((N_SHARD, FF_SHARD, D_MODEL), F32)] * 3,
        compiler_params=_params(("parallel", "arbitrary")),
    )(n2, da, db, ff, dh2b)


def _bwd_out(dh1, ms, mp, w_out_b, tm, token):
    n_pad = dh1.shape[0]

    def body(dh1_ref, ms_ref, mp_ref, wo_ref, token_ref, dms_ref, dmp_ref, dwo_ref):
        i = pl.program_id(0)

        @pl.when(i == 0)
        def _():
            dwo_ref[...] = jnp.zeros_like(dwo_ref)

        d = dh1_ref[...].astype(BF16)
        dms = _dot_nt(d, wo_ref[:D_SSM, :])
        for k in range(4):
            dms_ref[k] = dms[:, 128 * k:128 * (k + 1)]
        dmp_ref[...] = _dot_nt(d, wo_ref[D_SSM:, :])
        dwo_ref[:D_SSM, :] += _dot_tn(ms_ref[...], d)
        dwo_ref[D_SSM:, :] += _dot_tn(mp_ref[...], d)

    row = lambda w: pl.BlockSpec((tm, w), lambda i: (i, 0))
    sds = jax.ShapeDtypeStruct
    return pl.pallas_call(
        body, grid=(n_pad // tm,), name="bwd_out",
        in_specs=[row(D_MODEL), row(D_SSM), row(D_POOL), _full((D_MODEL, D_MODEL)), _ANY],
        out_specs=[pl.BlockSpec((4, tm, 128), lambda i: (0, i, 0)), row(D_POOL), _full((D_MODEL, D_MODEL))],
        out_shape=[sds((4, n_pad, 128), F32), sds((n_pad, D_POOL), F32), sds((D_MODEL, D_MODEL), F32)],
        compiler_params=_params(("arbitrary",)),
    )(dh1, ms, mp, w_out_b, token)


def _bwd_in(du, dv, h0, dh1, g1, w_in_b, tm):
    n_pad = h0.shape[0]

    def body(du_ref, dv_ref, h0_ref, dh1_ref, g1_ref, w_ref, dh0_ref, dwi_ref, dg1_ref):
        i = pl.program_id(0)

        @pl.when(i == 0)
        def _():
            dwi_ref[...] = jnp.zeros_like(dwi_ref)
            dg1_ref[...] = jnp.zeros_like(dg1_ref)

        dub = du_ref[...].astype(BF16)
        dvb = dv_ref[...].astype(BF16)
        dn1 = _dot_nt(dub, w_ref[:, :D_SSM]) + _dot_nt(dvb, w_ref[:, D_SSM:])
        h = h0_ref[...]
        r = _rms(h)
        xhat = h * r
        g_row = g1_ref[...]
        n1 = (xhat * g_row).astype(BF16)
        dwi_ref[:, :D_SSM] += _dot_tn(n1, dub)
        dwi_ref[:, D_SSM:] += _dot_tn(n1, dvb)
        dg1_ref[...] += jnp.sum(dn1 * xhat, axis=0, keepdims=True)
        dh0_ref[...] = dh1_ref[...] + _rms_bwd(dn1 * g_row, xhat, r)

    row = lambda w: pl.BlockSpec((tm, w), lambda i: (i, 0))
    sds = jax.ShapeDtypeStruct
    return pl.pallas_call(
        body, grid=(n_pad // tm,), name="bwd_in",
        in_specs=[row(D_SSM), row(D_POOL), row(D_MODEL), row(D_MODEL), _full((1, D_MODEL)), _full((D_MODEL, D_MODEL))],
        out_specs=[row(D_MODEL), _full((D_MODEL, D_MODEL)), _full((1, D_MODEL))],
        out_shape=[sds((n_pad, D_MODEL), F32), sds((D_MODEL, D_MODEL), F32), sds((1, D_MODEL), F32)],
        compiler_params=_params(("arbitrary",)),
    )(du, dv, h0, dh1, g1, w_in_b)


SEGMENTS = 8


def _interleaved(ref, seg):
    return jnp.concatenate(
        [jnp.concatenate([ref[i, pl.ds(j, SEGMENTS, stride=seg), :] for i in range(4)], axis=1) for j in range(seg)],
        axis=0)


def _time_order(scratch, val, seg):
    for i in range(4):
        scratch[i] = val[:, 128 * i:128 * (i + 1)]
    tiles = []
    for m in range(val.shape[0] // 8):
        s, j0 = divmod(8 * m, seg)
        tiles.append(jnp.concatenate(
            [scratch[i, pl.ds(8 * j0 + s, 8, stride=SEGMENTS), :] for i in range(4)], axis=1))
    return jnp.concatenate(tiles, axis=0)


def _power_table(lam_ref, pw_r, pw_i, seg):
    a_r = jnp.broadcast_to(lam_ref[0:1, :], (SEGMENTS, N_STATE))
    a_i = jnp.broadcast_to(lam_ref[1:2, :], (SEGMENTS, N_STATE))
    p_r, p_i = a_r, a_i
    for k in range(seg):
        pw_r[SEGMENTS * k:SEGMENTS * (k + 1), :] = p_r
        pw_i[SEGMENTS * k:SEGMENTS * (k + 1), :] = p_i
        p_r, p_i = p_r * a_r - p_i * a_i, p_r * a_i + p_i * a_r


def _segment_scan(xr_ref, xi_ref, cols, pw_r, pw_i, hr_s, hi_s, seg, reverse):
    sign = -1.0 if reverse else 1.0
    a_r, a_i = pw_r[0:SEGMENTS, cols], sign * pw_i[0:SEGMENTS, cols]

    def step(n, carry):
        hr, hi = carry
        o = pl.multiple_of((seg - 1 - n if reverse else n) * SEGMENTS, SEGMENTS)
        nr = a_r * hr - a_i * hi + xr_ref[pl.ds(o, SEGMENTS), cols]
        ni = a_r * hi + a_i * hr + xi_ref[pl.ds(o, SEGMENTS), cols]
        xr_ref[pl.ds(o, SEGMENTS), cols] = nr
        xi_ref[pl.ds(o, SEGMENTS), cols] = ni
        return nr, ni

    zero = jnp.zeros((SEGMENTS, cols.stop - cols.start), F32)
    e_r, e_i = lax.fori_loop(0, seg, step, (zero, zero), unroll=2)

    top = SEGMENTS * (seg - 1)
    ls_r, ls_i = pw_r[top:top + 1, cols], sign * pw_i[top:top + 1, cols]
    c_r, c_i = hr_s[0:1, cols], hi_s[0:1, cols]
    in_r, in_i = [None] * SEGMENTS, [None] * SEGMENTS
    for s in (range(SEGMENTS - 1, -1, -1) if reverse else range(SEGMENTS)):
        in_r[s], in_i[s] = c_r, c_i
        c_r, c_i = (e_r[s:s + 1, :] + ls_r * c_r - ls_i * c_i, e_i[s:s + 1, :] + ls_r * c_i + ls_i * c_r)
    hr_s[0:1, cols] = c_r
    hi_s[0:1, cols] = c_i
    cm_r, cm_i = jnp.concatenate(in_r, axis=0), jnp.concatenate(in_i, axis=0)

    def fix(jj, _):
        o = pl.multiple_of(jj * SEGMENTS, SEGMENTS)
        k = pl.multiple_of((seg - 1 - jj if reverse else jj) * SEGMENTS, SEGMENTS)
        p_r, p_i = pw_r[pl.ds(k, SEGMENTS), cols], sign * pw_i[pl.ds(k, SEGMENTS), cols]
        xr_ref[pl.ds(o, SEGMENTS), cols] += p_r * cm_r - p_i * cm_i
        xi_ref[pl.ds(o, SEGMENTS), cols] += p_r * cm_i + p_i * cm_r
        return 0

    lax.fori_loop(0, seg, fix, 0, unroll=2)


def _s5_tail(y, glu_ref, glub):
    g, dgelu = _gelu_and_grad(y)
    gb = g.astype(BF16)
    gate = jnp.concatenate([_dot(gb[:, 128 * j:128 * (j + 1)], glu_ref[j]) for j in range(SUPER)], axis=1) + glub
    sig = _sigmoid(gate)
    return g, gb, dgelu, sig, g * sig


def _s5_fwd(u4, lam, bbr, bbi, crt, cit, vecs, glu, tc):
    n_pad = u4.shape[1]
    seg = tc // SEGMENTS

    def body(u_ref, lam_ref, bbr_ref, bbi_ref, crt_ref, cit_ref, vec_ref, glu_ref,
             sr_ref, si_ref, y_ref, ms_ref, hr_s, hi_s, pw_r, pw_i, lanes):
        @pl.when(pl.program_id(0) == 0)
        def _():
            hr_s[...] = jnp.zeros_like(hr_s)
            hi_s[...] = jnp.zeros_like(hi_s)
            _power_table(lam_ref, pw_r, pw_i, seg)

        u_v = _interleaved(u_ref, seg)
        ub = u_v.astype(BF16)
        for j in range(SUPER):
            uj = ub[:, 128 * j:128 * (j + 1)]
            sr_ref[:, 512 * j:512 * (j + 1)] = _dot(uj, bbr_ref[j])
            si_ref[:, 512 * j:512 * (j + 1)] = _dot(uj, bbi_ref[j])
        for j in range(SUPER):
            _segment_scan(sr_ref, si_ref, slice(512 * j, 512 * (j + 1)), pw_r, pw_i, hr_s, hi_s, seg, False)

        d_row, glub, gs = vec_ref[0:1, :], vec_ref[1:2, :], vec_ref[2:3, :]
        ys_c = []
        for j in range(SUPER):
            sr_j = sr_ref[:, 512 * j:512 * (j + 1)].astype(BF16)
            si_j = si_ref[:, 512 * j:512 * (j + 1)].astype(BF16)
            ys_c.append(_dot(sr_j, crt_ref[j]) - _dot(si_j, cit_ref[j]))
        y = jnp.concatenate(ys_c, axis=1) + d_row * u_v
        y_ref[...] = y
        _, _, _, _, ys = _s5_tail(y, glu_ref, glub)
        ms_ref[...] = _time_order(lanes, ys * _rms(ys) * gs, seg).astype(BF16)

    chunk = lambda w: pl.BlockSpec((tc, w), lambda c: (c, 0))
    lane_blocks = pl.BlockSpec((4, tc, 128), lambda c: (0, c, 0))
    sds = jax.ShapeDtypeStruct
    return pl.pallas_call(
        body, grid=(n_pad // tc,), name="s5_fwd",
        in_specs=[lane_blocks, _full((8, N_STATE)), _full((SUPER, 128, 512)), _full((SUPER, 128, 512)),
                  _full((SUPER, 512, 128)), _full((SUPER, 512, 128)), _full((8, D_SSM)), _full((SUPER, 128, 128))],
        out_specs=[chunk(N_STATE), chunk(N_STATE), chunk(D_SSM), chunk(D_SSM)],
        out_shape=[sds((n_pad, N_STATE), F32), sds((n_pad, N_STATE), F32),
                   sds((n_pad, D_SSM), F32), sds((n_pad, D_SSM), BF16)],
        scratch_shapes=[pltpu.VMEM((8, N_STATE), F32), pltpu.VMEM((8, N_STATE), F32),
                        pltpu.VMEM((tc, N_STATE), F32), pltpu.VMEM((tc, N_STATE), F32),
                        pltpu.VMEM((4, tc, 128), F32)],
        compiler_params=_params(("arbitrary",)),
    )(u4, lam, bbr, bbi, crt, cit, vecs, glu)


def _s5_bwd(dms4, y, u4, sr, si, lam, bbr, bbi, crt, cit, vecs, glu, tc, token):
    n_pad = u4.shape[1]
    nc = n_pad // tc
    seg = tc // SEGMENTS

    def body(dms_ref, y_ref, u_ref, sr_ref, si_ref, pr_ref, pi_ref, lam_ref, bbr_ref, bbi_ref, crt_ref, cit_ref,
             vec_ref, glu_ref, token_ref, du_ref, dbbr_ref, dbbi_ref, dcrt_ref, dcit_ref, dglu_ref, dvec_ref, dlam_ref,
             qr_s, qi_s, cr_s, ci_s, pw_r, pw_i, lanes):
        c = pl.program_id(0)

        @pl.when(c == 0)
        def _():
            for ref in (dbbr_ref, dbbi_ref, dcrt_ref, dcit_ref, dglu_ref, dvec_ref, dlam_ref, cr_s, ci_s):
                ref[...] = jnp.zeros_like(ref)
            _power_table(lam_ref, pw_r, pw_i, seg)

        d_row, glub, gs = vec_ref[0:1, :], vec_ref[1:2, :], vec_ref[2:3, :]
        y_v, u_v = y_ref[...], _interleaved(u_ref, seg)
        ub = u_v.astype(BF16)
        g, gb, dgelu, sig, ys = _s5_tail(y_v, glu_ref, glub)
        r = _rms(ys)
        xhat = ys * r
        dm = _interleaved(dms_ref, seg)
        dys = _rms_bwd(dm * gs, xhat, r)
        dgate = dys * g * sig * (1.0 - sig)
        dgateb = dgate.astype(BF16)
        dg = dys * sig + jnp.concatenate(
            [_dot_nt(dgateb[:, 128 * j:128 * (j + 1)], glu_ref[j]) for j in range(SUPER)], axis=1)
        dy = dg * dgelu
        dyb = dy.astype(BF16)
        dvec_ref[0:1, :] += jnp.sum(dy * u_v, axis=0, keepdims=True)
        dvec_ref[1:2, :] += jnp.sum(dgate, axis=0, keepdims=True)
        dvec_ref[2:3, :] += jnp.sum(dm * xhat, axis=0, keepdims=True)

        for j in range(SUPER):
            cols, states = slice(128 * j, 128 * (j + 1)), slice(512 * j, 512 * (j + 1))
            dglu_ref[j] += _dot_tn(gb[:, cols], dgateb[:, cols])
            dcrt_ref[j] += _dot_tn(sr_ref[:, states].astype(BF16), dyb[:, cols])
            dcit_ref[j] -= _dot_tn(si_ref[:, states].astype(BF16), dyb[:, cols])
            qr_s[:, states] = _dot_nt(dyb[:, cols], crt_ref[j])
            qi_s[:, states] = -_dot_nt(dyb[:, cols], cit_ref[j])

        first = c == nc - 1
        row0 = lax.broadcasted_iota(jnp.int32, (SEGMENTS, 1), 0) == 0
        last = (seg - 1) * SEGMENTS
        for j in range(SUPER):
            states = slice(512 * j, 512 * (j + 1))
            _segment_scan(qr_s, qi_s, states, pw_r, pw_i, cr_s, ci_s, seg, True)

            before_r = jnp.where(first, 0.0, pltpu.roll(pr_ref[:, states], 1, 0))
            before_i = jnp.where(first, 0.0, pltpu.roll(pi_ref[:, states], 1, 0))
            hp_r = jnp.where(row0, before_r, pltpu.roll(sr_ref[pl.ds(last, SEGMENTS), states], 1, 0))
            hp_i = jnp.where(row0, before_i, pltpu.roll(si_ref[pl.ds(last, SEGMENTS), states], 1, 0))
            q_r, q_i = qr_s[pl.ds(0, SEGMENTS), states], qi_s[pl.ds(0, SEGMENTS), states]

            def dlam_step(jj, acc):
                o = pl.multiple_of(jj * SEGMENTS, SEGMENTS)
                above = pl.multiple_of((jj - 1) * SEGMENTS, SEGMENTS)
                h_r, h_i = sr_ref[pl.ds(above, SEGMENTS), states], si_ref[pl.ds(above, SEGMENTS), states]
                t_r, t_i = qr_s[pl.ds(o, SEGMENTS), states], qi_s[pl.ds(o, SEGMENTS), states]
                return acc[0] + t_r * h_r + t_i * h_i, acc[1] + t_i * h_r - t_r * h_i

            acc = lax.fori_loop(1, seg, dlam_step, (q_r * hp_r + q_i * hp_i, q_i * hp_r - q_r * hp_i), unroll=2)
            dlam_ref[0:SEGMENTS, states] += acc[0]
            dlam_ref[SEGMENTS:, states] += acc[1]

        du_c = []
        for j in range(SUPER):
            cols, states = slice(128 * j, 128 * (j + 1)), slice(512 * j, 512 * (j + 1))
            qr_j = qr_s[:, states].astype(BF16)
            qi_j = qi_s[:, states].astype(BF16)
            du_c.append(_dot_nt(qr_j, bbr_ref[j]) + _dot_nt(qi_j, bbi_ref[j]))
            dbbr_ref[j] += _dot_tn(ub[:, cols], qr_j)
            dbbi_ref[j] += _dot_tn(ub[:, cols], qi_j)
        du_ref[...] = _time_order(lanes, jnp.concatenate(du_c, axis=1) + dy * d_row, seg)

    rev = lambda c: nc - 1 - c
    chunk = lambda w: pl.BlockSpec((tc, w), lambda c: (rev(c), 0))
    lane_blocks = pl.BlockSpec((4, tc, 128), lambda c: (0, rev(c), 0))
    prev = pl.BlockSpec((SEGMENTS, N_STATE), lambda c: (jnp.maximum(rev(c) * seg - 1, 0), 0))
    sds = jax.ShapeDtypeStruct
    return pl.pallas_call(
        body, grid=(nc,), name="s5_bwd",
        in_specs=[lane_blocks, chunk(D_SSM), lane_blocks, chunk(N_STATE), chunk(N_STATE), prev, prev,
                  _full((8, N_STATE)), _full((SUPER, 128, 512)), _full((SUPER, 128, 512)),
                  _full((SUPER, 512, 128)), _full((SUPER, 512, 128)), _full((8, D_SSM)), _full((SUPER, 128, 128)), _ANY],
        out_specs=[chunk(D_SSM), _full((SUPER, 128, 512)), _full((SUPER, 128, 512)), _full((SUPER, 512, 128)),
                   _full((SUPER, 512, 128)), _full((SUPER, 128, 128)), _full((8, D_SSM)), _full((2 * SEGMENTS, N_STATE))],
        out_shape=[sds((n_pad, D_SSM), F32), sds((SUPER, 128, 512), F32), sds((SUPER, 128, 512), F32),
                   sds((SUPER, 512, 128), F32), sds((SUPER, 512, 128), F32), sds((SUPER, 128, 128), F32),
                   sds((8, D_SSM), F32), sds((2 * SEGMENTS, N_STATE), F32)],
        scratch_shapes=[pltpu.VMEM((tc, N_STATE), F32), pltpu.VMEM((tc, N_STATE), F32),
                        pltpu.VMEM((8, N_STATE), F32), pltpu.VMEM((8, N_STATE), F32),
                        pltpu.VMEM((tc, N_STATE), F32), pltpu.VMEM((tc, N_STATE), F32),
                        pltpu.VMEM((4, tc, 128), F32)],
        compiler_params=_params(("arbitrary",)),
    )(dms4, y, u4, sr, si, sr, si, lam, bbr, bbi, crt, cit, vecs, glu, token)


def _inv_count(c_idx, tc, w):
    t = c_idx * tc + lax.broadcasted_iota(jnp.int32, (tc, 1), 0)
    return 1.0 / jnp.minimum(t + 1, w).astype(F32)


def _pool_fwd(v, pw_b, vecs, tc):
    n_pad = v.shape[0]

    def body(v_ref, pw_ref, vec_ref, feat_ref, mp_ref, hist):
        c = pl.program_id(0)

        @pl.when(c == 0)
        def _():
            hist[...] = jnp.zeros_like(hist)

        v_v = v_ref[...]
        ext = jnp.concatenate([hist[...], v_v], axis=0)
        hist[...] = v_v[tc - POOL_HALO:, :]
        feats, ps = [], []
        for k, w in enumerate(POOL_WINDOWS):
            cols = slice(128 * k, 128 * (k + 1))
            s = ext[:, cols]
            sh = 1
            while sh < w:
                s = s + pltpu.roll(s, sh, 0)
                sh *= 2
            f = (s[POOL_HALO:, :] * _inv_count(c, tc, w) - v_v[:, cols]).astype(BF16)
            feats.append(f)
            ps.append(_dot(f, pw_ref[k]))
        feat_ref[...] = jnp.concatenate(feats, axis=1)
        yp = jnp.concatenate(ps, axis=1) * vec_ref[0:1, :]
        mp_ref[...] = (yp * _rms(yp) * vec_ref[1:2, :]).astype(BF16)

    chunk = lambda w: pl.BlockSpec((tc, w), lambda c: (c, 0))
    sds = jax.ShapeDtypeStruct
    return pl.pallas_call(
        body, grid=(n_pad // tc,), name="pool_fwd",
        in_specs=[chunk(D_POOL), _full((4, 128, 128)), _full((8, D_POOL))],
        out_specs=[chunk(D_POOL), chunk(D_POOL)],
        out_shape=[sds((n_pad, D_POOL), BF16), sds((n_pad, D_POOL), BF16)],
        scratch_shapes=[pltpu.VMEM((POOL_HALO, D_POOL), F32)],
        compiler_params=_params(("arbitrary",)),
    )(v, pw_b, vecs)


def _pool_bwd(dmp, feat, pw_b, vecs, tc):
    n_pad = dmp.shape[0]
    nc = n_pad // tc

    def body(dmp_ref, feat_ref, pw_ref, vec_ref, dv_ref, dpw_ref, dvec_ref, fut):
        c = pl.program_id(0)

        @pl.when(c == 0)
        def _():
            fut[...] = jnp.zeros_like(fut)
            dpw_ref[...] = jnp.zeros_like(dpw_ref)
            dvec_ref[...] = jnp.zeros_like(dvec_ref)

        scale, gp = vec_ref[0:1, :], vec_ref[1:2, :]
        feat_v = feat_ref[...]
        p = jnp.concatenate([_dot(feat_v[:, 128 * k:128 * (k + 1)], pw_ref[k]) for k in range(4)], axis=1)
        yp = p * scale
        r = _rms(yp)
        xhat = yp * r
        dm = dmp_ref[...]
        dyp = _rms_bwd(dm * gp, xhat, r)
        dvec_ref[0:1, :] += jnp.sum(dyp * p, axis=0, keepdims=True)
        dvec_ref[1:2, :] += jnp.sum(dm * xhat, axis=0, keepdims=True)
        dpb = (dyp * scale).astype(BF16)
        es, dfs = [], []
        for k, w in enumerate(POOL_WINDOWS):
            cols = slice(128 * k, 128 * (k + 1))
            dpw_ref[k] += _dot_tn(feat_v[:, cols], dpb[:, cols])
            df = _dot_nt(dpb[:, cols], pw_ref[k])
            dfs.append(df)
            es.append(df * _inv_count(nc - 1 - c, tc, w))
        e = jnp.concatenate(es, axis=1)
        ext = jnp.concatenate([e, fut[...]], axis=0)
        fut[...] = e[:POOL_HALO, :]
        n_ext = tc + POOL_HALO
        dvs = []
        for k, w in enumerate(POOL_WINDOWS):
            s = ext[:, 128 * k:128 * (k + 1)]
            sh = 1
            while sh < w:
                s = s + pltpu.roll(s, n_ext - sh, 0)
                sh *= 2
            dvs.append(s[:tc, :] - dfs[k])
        dv_ref[...] = jnp.concatenate(dvs, axis=1)

    chunk = lambda w: pl.BlockSpec((tc, w), lambda c: (nc - 1 - c, 0))
    sds = jax.ShapeDtypeStruct
    return pl.pallas_call(
        body, grid=(nc,), name="pool_bwd",
        in_specs=[chunk(D_POOL), chunk(D_POOL), _full((4, 128, 128)), _full((8, D_POOL))],
        out_specs=[chunk(D_POOL), _full((4, 128, 128)), _full((8, D_POOL))],
        out_shape=[sds((n_pad, D_POOL), F32), sds((4, 128, 128), F32), sds((8, D_POOL), F32)],
        scratch_shapes=[pltpu.VMEM((POOL_HALO, D_POOL), F32)],
        compiler_params=_params(("arbitrary",)),
    )(dmp, feat, pw_b, vecs)


def _place():
    x, y, c = lax.axis_index("x"), lax.axis_index("y"), lax.axis_index("c")
    chips = [(1 - x, y), (x, 1 - y), (1 - x, 1 - y)]
    return x, y, c, chips


_ANY = pl.BlockSpec(memory_space=pl.ANY)


def _cast_shards(shards, dtypes, place):
    n = len(shards)

    def body(place_ref, *refs):
        for i in range(n):
            refs[n + i][0] = refs[i][...].astype(dtypes[i])

    return pl.pallas_call(
        body, name="cast_shards",
        grid_spec=pltpu.PrefetchScalarGridSpec(
            num_scalar_prefetch=1, grid=(1,),
            in_specs=[pl.BlockSpec(s.shape, lambda i, p: (0, 0, 0)) for s in shards],
            out_specs=[pl.BlockSpec((1,) + s.shape, lambda i, p: (p[0], 0, 0, 0)) for s in shards]),
        out_shape=[jax.ShapeDtypeStruct((N_SHARD,) + s.shape, dt) for s, dt in zip(shards, dtypes)],
        compiler_params=_params(("arbitrary",)),
    )(place, *shards)


def _gather_shards(full):
    n = len(full)

    def body(*refs):
        outs = refs[n:2 * n]
        ici_send, ici_recv, d2d_send, d2d_recv = refs[2 * n:]
        x, y, c, chips = _place()
        q = 2 * x + y
        sibling = (x, y, 1 - c)

        def ici(i, j, shard, to):
            return pltpu.make_async_remote_copy(src_ref=outs[i].at[q, c], dst_ref=outs[i].at[shard, c],
                                                send_sem=ici_send.at[i, j], recv_sem=ici_recv.at[i, j],
                                                device_id=to, device_id_type=MESH)

        def d2d(i, j, shard, half):
            return pltpu.make_async_remote_copy(src_ref=outs[i].at[shard, c], dst_ref=outs[i].at[shard, half],
                                                send_sem=d2d_send.at[i, j], recv_sem=d2d_recv.at[i, j],
                                                device_id=sibling, device_id_type=MESH)

        sends = [ici(i, j, q, (*chip, c)) for i in range(n) for j, chip in enumerate(chips)]
        for cp in sends:
            cp.start()
        passed = []
        for i in range(n):
            for j, (cx, cy) in enumerate(chips):
                ici(i, j, 2 * cx + cy, (cx, cy, c)).wait_recv()
                cp = d2d(i, j, 2 * cx + cy, c)
                cp.start()
                passed.append(cp)
        for i in range(n):
            for j, (cx, cy) in enumerate(chips):
                d2d(i, j, 2 * cx + cy, 1 - c).wait_recv()
        for cp in sends + passed:
            cp.wait_send()

    return pl.pallas_call(
        body, name="gather_shards",
        in_specs=[_ANY] * n, out_specs=[_ANY] * n,
        out_shape=[jax.ShapeDtypeStruct(f.shape, f.dtype) for f in full],
        input_output_aliases={i: i for i in range(n)},
        scratch_shapes=[pltpu.SemaphoreType.DMA((n, 3)), pltpu.SemaphoreType.DMA((n, 3)),
                        pltpu.SemaphoreType.DMA((n, 3)), pltpu.SemaphoreType.DMA((n, 3))],
    )(*full)


def _forward_halves(full):
    n = len(full)

    def body(*refs):
        outs = refs[n:2 * n]
        send, recv = refs[2 * n:]
        x, y, c, chips = _place()

        def d2d(i, j, shard, half):
            return pltpu.make_async_remote_copy(src_ref=outs[i].at[shard, c], dst_ref=outs[i].at[shard, half],
                                                send_sem=send.at[i, j], recv_sem=recv.at[i, j],
                                                device_id=(x, y, 1 - c), device_id_type=MESH)

        cps = [d2d(i, j, 2 * cx + cy, c) for i in range(n) for j, (cx, cy) in enumerate(chips)]
        for cp in cps:
            cp.start()
        for i in range(n):
            for j, (cx, cy) in enumerate(chips):
                d2d(i, j, 2 * cx + cy, 1 - c).wait_recv()
        for cp in cps:
            cp.wait_send()

    return pl.pallas_call(
        body, name="forward_halves",
        in_specs=[_ANY] * n, out_specs=[_ANY] * n,
        out_shape=[jax.ShapeDtypeStruct(f.shape, f.dtype) for f in full],
        input_output_aliases={i: i for i in range(n)},
        scratch_shapes=[pltpu.SemaphoreType.DMA((n, 3)), pltpu.SemaphoreType.DMA((n, 3))],
    )(*full)


_HBM = pl.BlockSpec(memory_space=pltpu.HBM)
_SEM = pl.BlockSpec(memory_space=pltpu.SEMAPHORE)
_EFFECT = pltpu.SideEffectType.DATAFLOW_SIDE_EFFECTING


def _copies_start(name, arrays, sem_shape, build, after=None):
    n = len(arrays)
    extra = [] if after is None else [after]

    def body(*refs):
        outs = refs[n + len(extra):2 * n + len(extra)]
        send, recv, token = refs[2 * n + len(extra):]
        sends, _ = build(outs, send, recv)
        for cp in sends:
            cp.start()
        token[...] = jnp.zeros_like(token)

    out = pl.pallas_call(
        body, name=name, in_specs=[_HBM] * n + [_ANY] * len(extra),
        out_specs=[_HBM] * n + [_SEM, _SEM, pl.BlockSpec(memory_space=pltpu.VMEM)],
        out_shape=[pltpu.HBM(a.shape, a.dtype) for a in arrays]
        + [pltpu.SemaphoreType.DMA(sem_shape), pltpu.SemaphoreType.DMA(sem_shape), jax.ShapeDtypeStruct((8, 128), F32)],
        input_output_aliases={i: i for i in range(n)},
        compiler_params=pltpu.CompilerParams(has_side_effects=_EFFECT),
    )(*[pltpu.with_memory_space_constraint(a, pltpu.HBM) for a in arrays], *extra)
    return list(out[:n]), (out[n], out[n + 1]), out[n + 2]


def _copies_wait(name, arrays, sems, after, build):
    n = len(arrays)

    def body(*refs):
        ins = refs[:n]
        send, recv = refs[n], refs[n + 1]
        sends, recvs = build(ins, send, recv)
        for cp in sends:
            cp.wait_send()
        for cp in recvs:
            cp.wait_recv()

    return list(pl.pallas_call(
        body, name=name, in_specs=[_HBM] * n + [_SEM, _SEM, _ANY], out_specs=[_HBM] * n,
        out_shape=[pltpu.HBM(a.shape, a.dtype) for a in arrays],
        input_output_aliases={i: i for i in range(n)},
        compiler_params=pltpu.CompilerParams(has_side_effects=_EFFECT),
    )(*arrays, *sems, after))


def _remote(src, dst, send_sem, recv_sem, to):
    return pltpu.make_async_remote_copy(src_ref=src, dst_ref=dst, send_sem=send_sem, recv_sem=recv_sem,
                                        device_id=to, device_id_type=MESH)


def _build_gather(refs, send, recv):
    x, y, c, chips = _place()
    q = 2 * x + y
    pairs = [(i, j, chip) for i in range(len(refs)) for j, chip in enumerate(chips)]
    sends = [_remote(refs[i].at[q, c], refs[i].at[q, c], send.at[3 * i + j], recv.at[3 * i + j], (cx, cy, c))
             for i, j, (cx, cy) in pairs]
    recvs = [_remote(refs[i].at[q, c], refs[i].at[2 * cx + cy, c], send.at[3 * i + j], recv.at[3 * i + j], (cx, cy, c))
             for i, j, (cx, cy) in pairs]
    return sends, recvs


def _build_swap(refs, send, recv):
    x, y, c, _ = _place()
    n = len(refs) // 2
    cps = [_remote(refs[i].at[:, 1 - c], refs[n + i], send.at[i], recv.at[i], (x, y, 1 - c)) for i in range(n)]
    return cps, cps


def _build_exchange(refs, send, recv):
    x, y, c, chips = _place()
    n = len(refs) // 2
    cps = [_remote(refs[i].at[2 * cx + cy], refs[n + i].at[j], send.at[3 * i + j], recv.at[3 * i + j], (cx, cy, c))
           for i in range(n) for j, (cx, cy) in enumerate(chips)]
    return cps, cps


def _build_spread(refs, send, recv):
    x, y, c, _ = _place()
    flip = lambda bit, on: bit + on - 2 * bit * on
    cps = [_remote(refs[0], refs[1].at[r - 1], send.at[r - 1], recv.at[r - 1],
                   (flip(x, r >> 2 & 1), flip(y, r >> 1 & 1), flip(c, r & 1))) for r in (1, 2, 4, 6)]
    return cps, cps


def _forward_small(landed):
    def body(in_ref, out_ref, send, recv):
        x, y, c, _ = _place()
        cps = [_remote(out_ref.at[r - 1], out_ref.at[r], send.at[k], recv.at[k], (x, y, 1 - c))
               for k, r in enumerate((2, 4, 6))]
        for cp in cps:
            cp.start()
        for cp in cps:
            cp.wait()

    return pl.pallas_call(
        body, name="forward_small",
        in_specs=[_ANY], out_specs=_ANY, out_shape=jax.ShapeDtypeStruct(landed.shape, F32),
        input_output_aliases={0: 0},
        scratch_shapes=[pltpu.SemaphoreType.DMA((3,)), pltpu.SemaphoreType.DMA((3,))],
    )(landed)


def _swap_halves(grads):
    n = len(grads)

    def body(*refs):
        ins, outs = refs[:n], refs[n:2 * n]
        send, recv = refs[2 * n:]
        x, y, c, _ = _place()
        cps = [pltpu.make_async_remote_copy(src_ref=ins[i].at[:, 1 - c], dst_ref=outs[i], send_sem=send.at[i],
                                            recv_sem=recv.at[i], device_id=(x, y, 1 - c), device_id_type=MESH)
               for i in range(n)]
        for cp in cps:
            cp.start()
        for cp in cps:
            cp.wait()

    return pl.pallas_call(
        body, name="swap_halves",
        in_specs=[_ANY] * n, out_specs=[_ANY] * n,
        out_shape=[jax.ShapeDtypeStruct((N_SHARD,) + g.shape[2:], F32) for g in grads],
        scratch_shapes=[pltpu.SemaphoreType.DMA((n,)), pltpu.SemaphoreType.DMA((n,))],
    )(*grads)


def _join_halves(pairs):
    n = len(pairs)

    def body(*refs):
        outs = refs[n:2 * n]
        send, recv = refs[2 * n:]
        x, y, c, _ = _place()
        cps = [pltpu.make_async_remote_copy(src_ref=outs[i].at[c], dst_ref=outs[i].at[c], send_sem=send.at[i],
                                            recv_sem=recv.at[i], device_id=(x, y, 1 - c), device_id_type=MESH)
               for i in range(n)]
        for cp in cps:
            cp.start()
        for i in range(n):
            cps[i].wait_send()
            pltpu.make_async_remote_copy(src_ref=outs[i].at[c], dst_ref=outs[i].at[1 - c], send_sem=send.at[i],
                                         recv_sem=recv.at[i], device_id=(x, y, 1 - c), device_id_type=MESH).wait_recv()

    return pl.pallas_call(
        body, name="join_halves",
        in_specs=[_ANY] * n, out_specs=[_ANY] * n,
        out_shape=[jax.ShapeDtypeStruct(p.shape, F32) for p in pairs],
        input_output_aliases={i: i for i in range(n)},
        scratch_shapes=[pltpu.SemaphoreType.DMA((n,)), pltpu.SemaphoreType.DMA((n,))],
    )(*pairs)


N_SPLIT = 2


def _sum_siblings(tag, grads, recvd, place):
    n = len(grads)

    def body(place_ref, *refs):
        g_refs, r_refs, sb_refs, own_refs = (refs[k * n:(k + 1) * n] for k in range(4))
        s = pl.program_id(1)
        for i in range(n):
            tot = g_refs[i][0, 0] + r_refs[i][0]
            sb_refs[i][0] = tot.astype(BF16)

            @pl.when(s == place_ref[0])
            def _():
                own_refs[i][...] = tot

    in_specs, sb_specs, own_specs, sb_shapes, own_shapes = [], [], [], [], []
    for g in grads:
        _, _, r, cdim = g.shape
        rb = r // N_SPLIT
        in_specs.append(pl.BlockSpec((1, 1, rb, cdim), lambda b, s, p: (s, p[1], b, 0)))
        sb_specs.append(pl.BlockSpec((1, rb, cdim), lambda b, s, p: (s, b, 0)))
        own_specs.append(pl.BlockSpec((rb, cdim), lambda b, s, p: (b, 0)))
        sb_shapes.append(jax.ShapeDtypeStruct((N_SHARD, r, cdim), BF16))
        own_shapes.append(jax.ShapeDtypeStruct((r, cdim), F32))
    out = pl.pallas_call(
        body, name="sum_siblings_" + tag,
        grid_spec=pltpu.PrefetchScalarGridSpec(
            num_scalar_prefetch=1, grid=(N_SPLIT, N_SHARD),
            in_specs=in_specs + sb_specs, out_specs=sb_specs + own_specs),
        out_shape=sb_shapes + own_shapes,
        compiler_params=_params(("parallel", "arbitrary")),
    )(place, *grads, *recvd)
    return out[:n], out[n:]


def _sum_chips(own, recvd, place):
    n = len(own)

    def body(place_ref, *refs):
        o_refs, r_refs, out_refs = (refs[k * n:(k + 1) * n] for k in range(3))
        for i in range(n):
            tot = o_refs[i][...]
            for j in range(3):
                tot = tot + r_refs[i][j].astype(F32)
            out_refs[i][0] = tot

    o_specs, r_specs, out_specs = [], [], []
    for o in own:
        r, cdim = o.shape
        rb = r // N_SPLIT
        o_specs.append(pl.BlockSpec((rb, cdim), lambda b, p: (b, 0)))
        r_specs.append(pl.BlockSpec((3, rb, cdim), lambda b, p: (0, b, 0)))
        out_specs.append(pl.BlockSpec((1, rb, cdim), lambda b, p: (p[1], b, 0)))
    return pl.pallas_call(
        body, name="sum_chips",
        grid_spec=pltpu.PrefetchScalarGridSpec(num_scalar_prefetch=1, grid=(N_SPLIT,),
                                               in_specs=o_specs + r_specs, out_specs=out_specs),
        out_shape=[jax.ShapeDtypeStruct((2,) + o.shape, F32) for o in own],
        compiler_params=_params(("parallel",)),
    )(place, *own, *recvd)


def _adamw_math(w, g, m, v):
    m = ADAM_B1 * m + (1.0 - ADAM_B1) * g
    v = ADAM_B2 * v + (1.0 - ADAM_B2) * (g * g)
    m_hat = m / (1.0 - ADAM_B1 ** ADAM_STEP)
    v_hat = v / (1.0 - ADAM_B2 ** ADAM_STEP)
    delta = -ADAM_LR * (m_hat / (jnp.sqrt(v_hat) + ADAM_EPS) + ADAM_WD * w)
    return delta, m, v


def _adamw(name, ws, gs, ms, vs, n_split):
    n = len(ws)

    def body(*refs):
        w_r, g_r, m_r, v_r, d_o, m_o, v_o = (refs[k * n:(k + 1) * n] for k in range(7))
        for i in range(n):
            d, m, v = _adamw_math(w_r[i][...], g_r[i][...], m_r[i][...], v_r[i][...])
            d_o[i][...] = d
            m_o[i][...] = m
            v_o[i][...] = v

    specs = [pl.BlockSpec((w.shape[0] // n_split, w.shape[1]), lambda b: (b, 0)) for w in ws]
    shapes = [jax.ShapeDtypeStruct(w.shape, F32) for w in ws]
    out = pl.pallas_call(
        body, name=name, grid=(n_split,),
        in_specs=specs * 4, out_specs=specs * 3, out_shape=shapes * 3,
        compiler_params=_params(("parallel",)),
    )(*ws, *gs, *ms, *vs)
    return out[:n], out[n:2 * n], out[2 * n:]


def _reduce_small(own, received):
    def body(own_ref, recv_ref, g_out):
        me = 4 * lax.axis_index("x") + 2 * lax.axis_index("y") + lax.axis_index("c")
        g = None
        for k in range(8):
            mine = me == k
            part = jnp.where(mine, own_ref[...], recv_ref[jnp.where(mine, 0, jnp.bitwise_xor(me, k) - 1)])
            g = part if g is None else g + part
        g_out[...] = g

    return pl.pallas_call(
        body, name="reduce_small",
        out_shape=jax.ShapeDtypeStruct(own.shape, F32),
        compiler_params=_params(),
    )(own, received)


def _adamw_small(ws, gs, ms, vs):
    n = len(ws)

    def body(*refs):
        w_r, g_r, m_r, v_r, d_o, m_o, v_o = (refs[k * n:(k + 1) * n] for k in range(7))
        for i in range(n):
            d, mm, vv = _adamw_math(w_r[i][...], g_r[i][...], m_r[i][...], v_r[i][...])
            d_o[i][...] = d
            m_o[i][...] = mm
            v_o[i][...] = vv

    out = pl.pallas_call(
        body, name="adamw_small",
        out_shape=[jax.ShapeDtypeStruct(t.shape, F32) for t in ws] * 3,
        compiler_params=_params(),
    )(*ws, *gs, *ms, *vs)
    return out[:n], out[n:2 * n], out[2 * n:]


def _s5_operands(lam_re, lam_im, log_step, b_re, b_im, c_re, c_im, glu_w):
    lr = jnp.minimum(lam_re, -1e-4)
    li = lam_im
    step = jnp.exp(log_step)[:, None]
    mag = jnp.exp(lr * step)
    ang = li * step
    abr = mag * jnp.cos(ang)
    abi = mag * jnp.sin(ang)
    nr = abr - 1.0
    ni = abi
    den = lr * lr + li * li
    cr = ((nr * lr + ni * li) / den)[..., None]
    ci = ((ni * lr - nr * li) / den)[..., None]
    bbr = cr * b_re - ci * b_im
    bbi = cr * b_im + ci * b_re
    eye = jnp.eye(8, dtype=F32)
    g, h, p = SSM_GROUPS // SUPER, SSM_GROUP, SSM_STATE

    def b_layout(t):
        return jnp.einsum("ab,japh->jahbp", eye, t.reshape(SUPER, g, p, h)).reshape(SUPER, g * h, g * p)

    def c_layout(t):
        return jnp.einsum("ab,jahp->jbpah", eye, t.reshape(SUPER, g, h, p)).reshape(SUPER, g * p, g * h)

    glu = jnp.einsum("ab,jahk->jahbk", eye, glu_w.reshape(SUPER, g, h, h)).reshape(SUPER, g * h, g * h)
    lam = _pad_rows(jnp.concatenate([abr.reshape(1, N_STATE), abi.reshape(1, N_STATE)], axis=0), 8)
    return lam, b_layout(bbr), b_layout(bbi), c_layout(c_re), c_layout(c_im), glu


def _pad_rows(a, rows):
    return jnp.pad(a, ((0, rows - a.shape[0]), (0, 0)))


def _pack(parts):
    rows = []
    for a in parts:
        flat = a.reshape(-1)
        n = -(-flat.shape[0] // 128)
        rows.append(jnp.pad(flat, (0, n * 128 - flat.shape[0])).reshape(n, 128))
    out = jnp.concatenate(rows, axis=0)
    return _pad_rows(out, -(-out.shape[0] // 8) * 8)


def _unpack(packed, like):
    out, at = [], 0
    for a in like:
        n = -(-a.size // 128)
        out.append(packed[at:at + n].reshape(-1)[:a.size].reshape(a.shape))
        at += n
    return out


SMALL = ("norm1_g", "ssm_lambda_re", "ssm_lambda_im", "ssm_log_step", "ssm_b_re", "ssm_b_im", "ssm_c_re", "ssm_c_im",
         "ssm_d", "ssm_glu_w", "ssm_glu_b", "ssm_norm_g", "pool_w", "pool_scale", "pool_norm_g", "norm2_g",
         "final_norm_g")
LARGE = ("w_in", "w_out", "w_gate", "w_up", "w_down")
WEIGHTS = ("meta_tokens", "norm1_g", "w_in", "ssm_lambda_re", "ssm_lambda_im", "ssm_log_step", "ssm_b_re", "ssm_b_im",
           "ssm_c_re", "ssm_c_im", "ssm_d", "ssm_glu_w", "ssm_glu_b", "ssm_norm_g", "pool_w", "pool_scale",
           "pool_norm_g", "w_out", "norm2_g", "w_gate", "w_up", "w_down", "final_norm_g")


def _step(x, target, w, m, v):
    seq = x.shape[1]
    n_rows = N_META + seq
    n_pad, tm, tc, tg = _plan(n_rows)
    xq, yq, cq = lax.axis_index("x"), lax.axis_index("y"), lax.axis_index("c")
    place = jnp.stack([2 * xq + yq, cq]).astype(jnp.int32)

    def halves(a2d):
        return a2d.reshape(2, a2d.shape[0] // 2, a2d.shape[1])

    def local2d(t):
        return {"w_gate": lambda a: a[0].T, "w_up": lambda a: a[0].T}.get(t, lambda a: a[0])

    shards = [halves(local2d(k)(w[k])) for k in LARGE] + [halves(w["meta_tokens"])]
    full = _cast_shards(shards, [BF16] * len(LARGE) + [F32], place)
    w_in_full, meta_full = _gather_shards([full[0], full[5]])
    late, gather_sems, gather_token = _copies_start("gather_start", list(full[1:5]), (12,), _build_gather,
                                                    after=w_in_full)
    w_in_b = w_in_full.reshape(D_MODEL, D_MODEL)
    meta = meta_full.reshape(N_SHARD, N_META, D_MODEL // N_SHARD).transpose(1, 0, 2).reshape(N_META, D_MODEL)

    h0 = _pad_rows(jnp.concatenate([meta, x[0]], axis=0), n_pad)
    tgt = _pad_rows(jnp.concatenate([jnp.zeros((N_META, D_MODEL), F32), target[0]], axis=0), n_pad)
    s5_in = (w["ssm_lambda_re"][0], w["ssm_lambda_im"][0], w["ssm_log_step"][0], w["ssm_b_re"][0], w["ssm_b_im"][0],
             w["ssm_c_re"][0], w["ssm_c_im"][0], w["ssm_glu_w"][0])
    (lam, bbr, bbi, crt, cit, glu), s5_vjp = jax.vjp(_s5_operands, *s5_in)
    bbr_b, bbi_b, crt_b, cit_b, glu_b16 = (t.astype(BF16) for t in (bbr, bbi, crt, cit, glu))
    s5_vecs = _pad_rows(jnp.concatenate([w["ssm_d"].reshape(1, D_SSM), w["ssm_glu_b"].reshape(1, D_SSM),
                                         w["ssm_norm_g"].reshape(1, D_SSM)], axis=0), 8)
    pool_vecs = _pad_rows(jnp.concatenate([w["pool_scale"].reshape(1, D_POOL), w["pool_norm_g"].reshape(1, D_POOL)],
                                          axis=0), 8)
    pw_b = w["pool_w"][0].astype(BF16)
    g1, g2, gf = w["norm1_g"].reshape(1, D_MODEL), w["norm2_g"].reshape(1, D_MODEL), w["final_norm_g"].reshape(1, D_MODEL)

    u, vv = _fwd_in(h0, g1, w_in_b, tm, gather_token)
    sr, si, y, ms = _s5_fwd(u, lam, bbr_b, bbi_b, crt_b, cit_b, s5_vecs, glu_b16, tc)
    feat, mp = _pool_fwd(vv, pw_b, pool_vecs, tc)
    late = _forward_halves(_copies_wait("gather_wait", late, gather_sems, mp, _build_gather))
    w_out_b = late[0].reshape(D_MODEL, D_MODEL)
    wg_b, wu_b, wd_b = (t.reshape(N_SHARD, FF_SHARD, D_MODEL) for t in late[1:])
    h1, n2, a, b, ff, dh2, dh2b, loss_acc, dgf = _fwd_ffn(h0, ms, mp, w_out_b, g2, wg_b, wu_b, wd_b, gf, tgt, tm, n_rows)

    def quarters(t):
        if t.ndim == 2:
            t = t.reshape(N_SHARD, t.shape[0] // N_SHARD, t.shape[1])
        return t.reshape(N_SHARD, 2, t.shape[1] // 2, t.shape[2])

    def landing(like, lead, dtype):
        return [lax.empty((lead,) + t.shape[2:], dtype) for t in like]

    da, db, dh1, dg2 = _bwd_ffn(dh2, a, b, wg_b, wu_b, wd_b, h1, g2, tm)
    ffn_g = [quarters(t) for t in _grad_ffn(n2, da, db, ff, dh2b, tg)]
    nf = len(ffn_g)
    moved, swap_sems, swap_token = _copies_start("swap_start", ffn_g + landing(ffn_g, N_SHARD, F32), (nf,), _build_swap)
    dms, dmp, dwo = _bwd_out(dh1, ms, mp, w_out_b, tm, swap_token)
    moved = _copies_wait("swap_wait", moved, swap_sems, dwo, _build_swap)
    ffn_parts, ffn_own = _sum_siblings("ffn", moved[:nf], moved[nf:], place)
    moved, exch_sems, exch_token = _copies_start("exchange_start", list(ffn_parts) + landing(ffn_g, 3, BF16), (3 * nf,),
                                                 _build_exchange)
    du, dbbr, dbbi, dcrt, dcit, dglu, ds5v, dlam = _s5_bwd(dms, y, u, sr, si, lam, bbr_b, bbi_b, crt_b, cit_b,
                                                           s5_vecs, glu_b16, tc, exch_token)
    dv, dpw, dpoolv = _pool_bwd(dmp, feat, pw_b, pool_vecs, tc)
    dh0, dwi, dg1 = _bwd_in(du, dv, h0, dh1, g1, w_in_b, tm)
    ffn_from_chips = _copies_wait("exchange_wait", moved, exch_sems, dh0, _build_exchange)[nf:]
    dlam = _pad_rows(jnp.concatenate([jnp.sum(dlam[:SEGMENTS], axis=0, keepdims=True),
                                      jnp.sum(dlam[SEGMENTS:], axis=0, keepdims=True)], axis=0), 8)
    d_lre, d_lim, d_lstep, d_bre, d_bim, d_cre, d_cim, d_gluw = s5_vjp((dlam, dbbr, dbbi, dcrt, dcit, dglu))
    grad_x = dh0[N_META:n_rows][None]

    small_g = {
        "norm1_g": dg1, "ssm_lambda_re": d_lre, "ssm_lambda_im": d_lim, "ssm_log_step": d_lstep, "ssm_b_re": d_bre,
        "ssm_b_im": d_bim, "ssm_c_re": d_cre, "ssm_c_im": d_cim, "ssm_d": ds5v[0], "ssm_glu_w": d_gluw,
        "ssm_glu_b": ds5v[1], "ssm_norm_g": ds5v[2], "pool_w": dpw, "pool_scale": dpoolv[0], "pool_norm_g": dpoolv[1],
        "norm2_g": dg2, "final_norm_g": dgf,
    }
    like = [w[k] for k in SMALL]
    packed_g = _pack([small_g[k].reshape(w[k].shape) for k in SMALL] + [dh0[:N_META], loss_acc[0:1, 0:1]])

    mix_g = [quarters(t) for t in (dwi, dwo)]
    mix_parts, mix_own = _sum_siblings("mix", mix_g, _swap_halves(mix_g), place)
    moved, mix_sems, mix_token = _copies_start("mix_exchange_start", list(mix_parts) + landing(mix_g, 3, BF16),
                                               (3 * len(mix_g),), _build_exchange)
    spread, small_sems, small_token = _copies_start(
        "small_start", [packed_g, lax.empty((7,) + packed_g.shape, F32)], (7,), _build_spread, after=mix_token)
    mix_from_chips = _copies_wait("mix_exchange_wait", moved, mix_sems, small_token, _build_exchange)[len(mix_g):]
    joined = _join_halves(_sum_chips(list(mix_own) + list(ffn_own), list(mix_from_chips) + list(ffn_from_chips), place))
    g_large = [j.reshape(j.shape[0] * j.shape[1], j.shape[2]) for j in joined]
    w2d, m2d, v2d = ([local2d(k)(t[k]) for k in LARGE] for t in (w, m, v))
    d_large, m_large, v_large = _adamw("adamw_large", w2d, g_large, m2d, v2d, 8)

    own_g, landed = _copies_wait("small_wait", spread, small_sems, d_large[0], _build_spread)
    g_pk = _reduce_small(own_g, _forward_small(landed))
    g_small = _unpack(g_pk, like + [jax.ShapeDtypeStruct((N_META, D_MODEL), F32), jax.ShapeDtypeStruct((1, 1), F32)])
    loss = g_small.pop()[0, 0]
    rows2d = lambda t: t.reshape(1, -1) if t.ndim == 1 else t
    d_small, m_small, v_small = _adamw_small(*([rows2d(t) for t in ts] for ts in (
        like, g_small[:-1], [m[k] for k in SMALL], [v[k] for k in SMALL])))
    d_small, m_small, v_small = ([t.reshape(w[k].shape) for t, k in zip(ts, SMALL)] for ts in (d_small, m_small, v_small))
    q = place[0]
    g_meta = lax.dynamic_slice_in_dim(g_small[-1], q * (D_MODEL // N_SHARD), D_MODEL // N_SHARD, axis=1)
    d_meta, m_meta, v_meta = _adamw("adamw_meta", [w["meta_tokens"]], [g_meta], [m["meta_tokens"]],
                                    [v["meta_tokens"]], 1)

    grads, deltas, new_m, new_v = {}, {}, {}, {}
    for i, k in enumerate(SMALL):
        grads[k], deltas[k], new_m[k], new_v[k] = g_small[i], d_small[i], m_small[i], v_small[i]
    for i, k in enumerate(LARGE):
        back = (lambda t: t.T[None]) if k in ("w_gate", "w_up") else (lambda t: t[None])
        grads[k], deltas[k], new_m[k], new_v[k] = (back(t) for t in (g_large[i], d_large[i], m_large[i], v_large[i]))
    grads["meta_tokens"], deltas["meta_tokens"] = g_meta, d_meta[0]
    new_m["meta_tokens"], new_v["meta_tokens"] = m_meta[0], v_meta[0]
    return (loss, grad_x, *[grads[k] for k in WEIGHTS], *[deltas[k] for k in WEIGHTS],
            *[new_m[k] for k in WEIGHTS], *[new_v[k] for k in WEIGHTS])


def kernel(x, meta_tokens, norm1_g, w_in, ssm_lambda_re, ssm_lambda_im, ssm_log_step, ssm_b_re, ssm_b_im, ssm_c_re, ssm_c_im, ssm_d, ssm_glu_w, ssm_glu_b, ssm_norm_g, pool_w, pool_scale, pool_norm_g, w_out, norm2_g, w_gate, w_up, w_down, final_norm_g, loss_target, m_meta_tokens, m_norm1_g, m_w_in, m_ssm_lambda_re, m_ssm_lambda_im, m_ssm_log_step, m_ssm_b_re, m_ssm_b_im, m_ssm_c_re, m_ssm_c_im, m_ssm_d, m_ssm_glu_w, m_ssm_glu_b, m_ssm_norm_g, m_pool_w, m_pool_scale, m_pool_norm_g, m_w_out, m_norm2_g, m_w_gate, m_w_up, m_w_down, m_final_norm_g, v_meta_tokens, v_norm1_g, v_w_in, v_ssm_lambda_re, v_ssm_lambda_im, v_ssm_log_step, v_ssm_b_re, v_ssm_b_im, v_ssm_c_re, v_ssm_c_im, v_ssm_d, v_ssm_glu_w, v_ssm_glu_b, v_ssm_norm_g, v_pool_w, v_pool_scale, v_pool_norm_g, v_w_out, v_norm2_g, v_w_gate, v_w_up, v_w_down, v_final_norm_g):
    w = dict(meta_tokens=meta_tokens, norm1_g=norm1_g, w_in=w_in, ssm_lambda_re=ssm_lambda_re, ssm_lambda_im=ssm_lambda_im, ssm_log_step=ssm_log_step, ssm_b_re=ssm_b_re, ssm_b_im=ssm_b_im, ssm_c_re=ssm_c_re, ssm_c_im=ssm_c_im, ssm_d=ssm_d, ssm_glu_w=ssm_glu_w, ssm_glu_b=ssm_glu_b, ssm_norm_g=ssm_norm_g, pool_w=pool_w, pool_scale=pool_scale, pool_norm_g=pool_norm_g, w_out=w_out, norm2_g=norm2_g, w_gate=w_gate, w_up=w_up, w_down=w_down, final_norm_g=final_norm_g)
    m = dict(meta_tokens=m_meta_tokens, norm1_g=m_norm1_g, w_in=m_w_in, ssm_lambda_re=m_ssm_lambda_re, ssm_lambda_im=m_ssm_lambda_im, ssm_log_step=m_ssm_log_step, ssm_b_re=m_ssm_b_re, ssm_b_im=m_ssm_b_im, ssm_c_re=m_ssm_c_re, ssm_c_im=m_ssm_c_im, ssm_d=m_ssm_d, ssm_glu_w=m_ssm_glu_w, ssm_glu_b=m_ssm_glu_b, ssm_norm_g=m_ssm_norm_g, pool_w=m_pool_w, pool_scale=m_pool_scale, pool_norm_g=m_pool_norm_g, w_out=m_w_out, norm2_g=m_norm2_g, w_gate=m_w_gate, w_up=m_w_up, w_down=m_w_down, final_norm_g=m_final_norm_g)
    v = dict(meta_tokens=v_meta_tokens, norm1_g=v_norm1_g, w_in=v_w_in, ssm_lambda_re=v_ssm_lambda_re, ssm_lambda_im=v_ssm_lambda_im, ssm_log_step=v_ssm_log_step, ssm_b_re=v_ssm_b_re, ssm_b_im=v_ssm_b_im, ssm_c_re=v_ssm_c_re, ssm_c_im=v_ssm_c_im, ssm_d=v_ssm_d, ssm_glu_w=v_ssm_glu_w, ssm_glu_b=v_ssm_glu_b, ssm_norm_g=v_ssm_norm_g, pool_w=v_pool_w, pool_scale=v_pool_scale, pool_norm_g=v_pool_norm_g, w_out=v_w_out, norm2_g=v_norm2_g, w_gate=v_w_gate, w_up=v_w_up, w_down=v_w_down, final_norm_g=v_final_norm_g)
    return _step(x, loss_target, w, m, v)
```

```python
import functools
import math

import jax
import jax.numpy as jnp
from jax import lax
from jax.experimental import pallas as pl
from jax.experimental.pallas import tpu as pltpu

F32 = jnp.float32
BF16 = jnp.bfloat16
MESH = pl.DeviceIdType.MESH
AXES = ("x", "y", "c")

D_MODEL = 1024
D_SSM = 512
D_POOL = 512
N_META = 16
SSM_GROUP = 16
SSM_GROUPS = 32
SSM_STATE = 64
N_STATE = SSM_GROUPS * SSM_STATE
STATE_BLOCKS = N_STATE // 128
SUPER = 4
POOL_WINDOWS = (2, 4, 8, 16)
POOL_HALO = 16
D_FF = 2816
N_SHARD = 4
FF_SHARD = D_FF // N_SHARD
EPS = 1e-6
ADAM_LR, ADAM_B1, ADAM_B2, ADAM_EPS, ADAM_WD, ADAM_STEP = 0.001, 0.9, 0.999, 1e-08, 0.01, 10
VMEM_LIMIT = 56 * 1024 * 1024


def _plan(n_rows):
    if n_rows > 2048:
        tm, tc, tg = 416, 320, 1040
    else:
        tm, tc, tg = 128, 64, 128
    step = math.lcm(tm, tc, tg)
    return -(-n_rows // step) * step, tm, tc, tg


def _params(sem=None):
    return pltpu.CompilerParams(dimension_semantics=sem, vmem_limit_bytes=VMEM_LIMIT)


def _dot(a, b):
    return jnp.dot(a, b, preferred_element_type=F32)


def _dot_nt(a, b):
    return lax.dot_general(a, b, (((1,), (1,)), ((), ())), preferred_element_type=F32)


def _dot_tn(a, b):
    return lax.dot_general(a, b, (((0,), (0,)), ((), ())), preferred_element_type=F32)


def _sigmoid(x):
    return 0.5 * jnp.tanh(0.5 * x) + 0.5


_GELU_C = math.sqrt(2.0 / math.pi)


def _gelu_and_grad(y):
    y2 = y * y
    t = jnp.tanh(_GELU_C * (y + 0.044715 * y * y2))
    g = 0.5 * y * (1.0 + t)
    dg = 0.5 * (1.0 + t) + 0.5 * y * (1.0 - t * t) * (_GELU_C * (1.0 + 3.0 * 0.044715 * y2))
    return g, dg


def _rms(x):
    return lax.rsqrt(jnp.mean(x * x, axis=-1, keepdims=True) + EPS)


def _rms_bwd(dn, xhat, r):
    return r * (dn - xhat * jnp.mean(dn * xhat, axis=-1, keepdims=True))


def _full(shape):
    nd = len(shape)
    return pl.BlockSpec(shape, lambda *_: (0,) * nd)


def _fwd_in(h0, g1, w_in_b, tm, token):
    n_pad = h0.shape[0]

    def body(h_ref, g_ref, w_ref, token_ref, u_ref, v_ref):
        h = h_ref[...]
        n1 = (h * _rms(h) * g_ref[...]).astype(BF16)
        proj = _dot(n1, w_ref[...])
        for i in range(4):
            u_ref[i] = proj[:, 128 * i:128 * (i + 1)]
        v_ref[...] = proj[:, D_SSM:]

    row = lambda w: pl.BlockSpec((tm, w), lambda i: (i, 0))
    return pl.pallas_call(
        body, grid=(n_pad // tm,), name="fwd_in",
        in_specs=[row(D_MODEL), _full((1, D_MODEL)), _full((D_MODEL, D_MODEL)), _ANY],
        out_specs=[pl.BlockSpec((4, tm, 128), lambda i: (0, i, 0)), row(D_POOL)],
        out_shape=[jax.ShapeDtypeStruct((4, n_pad, 128), F32), jax.ShapeDtypeStruct((n_pad, D_POOL), F32)],
        compiler_params=_params(("parallel",)),
    )(h0, g1, w_in_b, token)


def _fwd_ffn(h0, ms, mp, w_out_b, g2, wg_b, wu_b, wd_b, gf, target, tm, n_valid):
    n_pad = h0.shape[0]
    nt = n_pad // tm

    def body(h0_ref, ms_ref, mp_ref, wo_ref, g2_ref, wg_ref, wu_ref, wd_ref, gf_ref, tgt_ref,
             h1_ref, n2_ref, a_ref, b_ref, ff_ref, dh2_ref, dh2b_ref, loss_ref, dgf_ref, acc):
        i, q = pl.program_id(0), pl.program_id(1)

        @pl.when((i == 0) & (q == 0))
        def _():
            loss_ref[...] = jnp.zeros_like(loss_ref)
            dgf_ref[...] = jnp.zeros_like(dgf_ref)

        @pl.when(q == 0)
        def _():
            h1 = h0_ref[...] + _dot(ms_ref[...], wo_ref[:D_SSM, :]) + _dot(mp_ref[...], wo_ref[D_SSM:, :])
            h1_ref[...] = h1
            acc[...] = h1
            n2_ref[...] = (h1 * _rms(h1) * g2_ref[...]).astype(BF16)

        n2 = n2_ref[...]
        a = _dot_nt(n2, wg_ref[0])
        b = _dot_nt(n2, wu_ref[0])
        a_ref[0] = a.astype(BF16)
        b_ref[0] = b.astype(BF16)
        ff = (a * _sigmoid(a) * b).astype(BF16)
        ff_ref[0] = ff
        acc[...] += _dot(ff, wd_ref[0])

        @pl.when(q == N_SHARD - 1)
        def _():
            h2 = acc[...]
            r = _rms(h2)
            xhat = h2 * r
            gf_row = gf_ref[...]
            rows = i * tm + lax.broadcasted_iota(jnp.int32, (tm, 1), 0)
            valid = (rows >= N_META) & (rows < n_valid)
            diff = jnp.where(valid, xhat * gf_row - tgt_ref[...], 0.0)
            loss_ref[...] += jnp.full(loss_ref.shape, 0.5 / D_MODEL, F32) * jnp.sum(diff * diff)
            dout = diff * (1.0 / D_MODEL)
            dgf_ref[...] += jnp.sum(dout * xhat, axis=0, keepdims=True)
            dh2 = _rms_bwd(dout * gf_row, xhat, r)
            dh2_ref[...] = dh2
            dh2b_ref[...] = dh2.astype(BF16)

    row = lambda w: pl.BlockSpec((tm, w), lambda i, q: (i, 0))
    shard_rows = pl.BlockSpec((1, FF_SHARD, D_MODEL), lambda i, q: (q, 0, 0))
    act = pl.BlockSpec((1, tm, FF_SHARD), lambda i, q: (q, i, 0))
    sds = jax.ShapeDtypeStruct
    return pl.pallas_call(
        body, grid=(nt, N_SHARD), name="fwd_ffn",
        in_specs=[row(D_MODEL), row(D_SSM), row(D_POOL), _full((D_MODEL, D_MODEL)), _full((1, D_MODEL)),
                  shard_rows, shard_rows, shard_rows, _full((1, D_MODEL)), row(D_MODEL)],
        out_specs=[row(D_MODEL), row(D_MODEL), act, act, act, row(D_MODEL), row(D_MODEL), _full((8, 128)),
                   _full((1, D_MODEL))],
        out_shape=[sds((n_pad, D_MODEL), F32), sds((n_pad, D_MODEL), BF16),
                   sds((N_SHARD, n_pad, FF_SHARD), BF16), sds((N_SHARD, n_pad, FF_SHARD), BF16),
                   sds((N_SHARD, n_pad, FF_SHARD), BF16), sds((n_pad, D_MODEL), F32), sds((n_pad, D_MODEL), BF16),
                   sds((8, 128), F32), sds((1, D_MODEL), F32)],
        scratch_shapes=[pltpu.VMEM((tm, D_MODEL), F32)],
        compiler_params=_params(("arbitrary", "arbitrary")),
    )(h0, ms, mp, w_out_b, g2, wg_b, wu_b, wd_b, gf, target)


def _bwd_ffn(dh2, a, b, wg_b, wu_b, wd_b, h1, g2, tm):
    n_pad = dh2.shape[0]

    def body(dh2_ref, a_ref, b_ref, wg_ref, wu_ref, wd_ref, h1_ref, g2_ref, da_ref, db_ref, dh1_ref, dg2_ref, acc):
        i, q = pl.program_id(0), pl.program_id(1)

        @pl.when((i == 0) & (q == 0))
        def _():
            dg2_ref[...] = jnp.zeros_like(dg2_ref)

        dff = _dot_nt(dh2_ref[...].astype(BF16), wd_ref[0])
        a_v, b_v = a_ref[0].astype(F32), b_ref[0].astype(F32)
        sig = _sigmoid(a_v)
        da = (dff * b_v * sig * (1.0 + a_v * (1.0 - sig))).astype(BF16)
        db = (dff * a_v * sig).astype(BF16)
        da_ref[0] = da
        db_ref[0] = db
        part = _dot(da, wg_ref[0]) + _dot(db, wu_ref[0])

        @pl.when(q == 0)
        def _():
            acc[...] = part

        @pl.when(q > 0)
        def _():
            acc[...] += part

        @pl.when(q == N_SHARD - 1)
        def _():
            h1 = h1_ref[...]
            r = _rms(h1)
            xhat = h1 * r
            dn2 = acc[...]
            dg2_ref[...] += jnp.sum(dn2 * xhat, axis=0, keepdims=True)
            dh1_ref[...] = dh2_ref[...] + _rms_bwd(dn2 * g2_ref[...], xhat, r)

    row = lambda w: pl.BlockSpec((tm, w), lambda i, q: (i, 0))
    shard_rows = pl.BlockSpec((1, FF_SHARD, D_MODEL), lambda i, q: (q, 0, 0))
    act = pl.BlockSpec((1, tm, FF_SHARD), lambda i, q: (q, i, 0))
    sds = jax.ShapeDtypeStruct
    return pl.pallas_call(
        body, grid=(n_pad // tm, N_SHARD), name="bwd_ffn",
        in_specs=[row(D_MODEL), act, act, shard_rows, shard_rows, shard_rows, row(D_MODEL), _full((1, D_MODEL))],
        out_specs=[act, act, row(D_MODEL), _full((1, D_MODEL))],
        out_shape=[sds((N_SHARD, n_pad, FF_SHARD), BF16), sds((N_SHARD, n_pad, FF_SHARD), BF16),
                   sds((n_pad, D_MODEL), F32), sds((1, D_MODEL), F32)],
        scratch_shapes=[pltpu.VMEM((tm, D_MODEL), F32)],
        compiler_params=_params(("arbitrary", "arbitrary")),
    )(dh2, a, b, wg_b, wu_b, wd_b, h1, g2)


def _grad_ffn(n2, da, db, ff, dh2b, tm):
    n_pad = n2.shape[0]

    def body(n2_ref, da_ref, db_ref, ff_ref, dh2_ref, dwg_ref, dwu_ref, dwd_ref):
        i = pl.program_id(1)
        n2_v = n2_ref[...]
        gg = _dot_tn(da_ref[0], n2_v)
        gu = _dot_tn(db_ref[0], n2_v)
        gd = _dot_tn(ff_ref[0], dh2_ref[...])

        @pl.when(i == 0)
        def _():
            dwg_ref[0] = gg
            dwu_ref[0] = gu
            dwd_ref[0] = gd

        @pl.when(i > 0)
        def _():
            dwg_ref[0] += gg
            dwu_ref[0] += gu
            dwd_ref[0] += gd

    row = lambda w: pl.BlockSpec((tm, w), lambda q, i: (i, 0))
    act = pl.BlockSpec((1, tm, FF_SHARD), lambda q, i: (q, i, 0))
    sds = jax.ShapeDtypeStruct
    return pl.pallas_call(
        body, grid=(N_SHARD, n_pad // tm), name="grad_ffn",
        in_specs=[row(D_MODEL), act, act, act, row(D_MODEL)],
        out_specs=[pl.BlockSpec((1, FF_SHARD, D_MODEL), lambda q, i: (q, 0, 0))] * 3,
        out_shape=[sds((N_SHARD, FF_SHARD, D_MODEL), F32)] * 3,
        compiler_params=_params(("parallel", "arbitrary")),
    )(n2, da, db, ff, dh2b)


def _bwd_out(dh1, ms, mp, w_out_b, tm, token):
    n_pad = dh1.shape[0]

    def body(dh1_ref, ms_ref, mp_ref, wo_ref, token_ref, dms_ref, dmp_ref, dwo_ref):
        i = pl.program_id(0)

        @pl.when(i == 0)
        def _():
            dwo_ref[...] = jnp.zeros_like(dwo_ref)

        d = dh1_ref[...].astype(BF16)
        dms = _dot_nt(d, wo_ref[:D_SSM, :])
        for k in range(4):
            dms_ref[k] = dms[:, 128 * k:128 * (k + 1)]
        dmp_ref[...] = _dot_nt(d, wo_ref[D_SSM:, :])
        dwo_ref[:D_SSM, :] += _dot_tn(ms_ref[...], d)
        dwo_ref[D_SSM:, :] += _dot_tn(mp_ref[...], d)

    row = lambda w: pl.BlockSpec((tm, w), lambda i: (i, 0))
    sds = jax.ShapeDtypeStruct
    return pl.pallas_call(
        body, grid=(n_pad // tm,), name="bwd_out",
        in_specs=[row(D_MODEL), row(D_SSM), row(D_POOL), _full((D_MODEL, D_MODEL)), _ANY],
        out_specs=[pl.BlockSpec((4, tm, 128), lambda i: (0, i, 0)), row(D_POOL), _full((D_MODEL, D_MODEL))],
        out_shape=[sds((4, n_pad, 128), F32), sds((n_pad, D_POOL), F32), sds((D_MODEL, D_MODEL), F32)],
        compiler_params=_params(("arbitrary",)),
    )(dh1, ms, mp, w_out_b, token)


def _bwd_in(du, dv, h0, dh1, g1, w_in_b, tm):
    n_pad = h0.shape[0]

    def body(du_ref, dv_ref, h0_ref, dh1_ref, g1_ref, w_ref, dh0_ref, dwi_ref, dg1_ref):
        i = pl.program_id(0)

        @pl.when(i == 0)
        def _():
            dwi_ref[...] = jnp.zeros_like(dwi_ref)
            dg1_ref[...] = jnp.zeros_like(dg1_ref)

        dub = du_ref[...].astype(BF16)
        dvb = dv_ref[...].astype(BF16)
        dn1 = _dot_nt(dub, w_ref[:, :D_SSM]) + _dot_nt(dvb, w_ref[:, D_SSM:])
        h = h0_ref[...]
        r = _rms(h)
        xhat = h * r
        g_row = g1_ref[...]
        n1 = (xhat * g_row).astype(BF16)
        dwi_ref[:, :D_SSM] += _dot_tn(n1, dub)
        dwi_ref[:, D_SSM:] += _dot_tn(n1, dvb)
        dg1_ref[...] += jnp.sum(dn1 * xhat, axis=0, keepdims=True)
        dh0_ref[...] = dh1_ref[...] + _rms_bwd(dn1 * g_row, xhat, r)

    row = lambda w: pl.BlockSpec((tm, w), lambda i: (i, 0))
    sds = jax.ShapeDtypeStruct
    return pl.pallas_call(
        body, grid=(n_pad // tm,), name="bwd_in",
        in_specs=[row(D_SSM), row(D_POOL), row(D_MODEL), row(D_MODEL), _full((1, D_MODEL)), _full((D_MODEL, D_MODEL))],
        out_specs=[row(D_MODEL), _full((D_MODEL, D_MODEL)), _full((1, D_MODEL))],
        out_shape=[sds((n_pad, D_MODEL), F32), sds((D_MODEL, D_MODEL), F32), sds((1, D_MODEL), F32)],
        compiler_params=_params(("arbitrary",)),
    )(du, dv, h0, dh1, g1, w_in_b)


SEGMENTS = 8


def _interleaved(ref, seg):
    return jnp.concatenate(
        [jnp.concatenate([ref[i, pl.ds(j, SEGMENTS, stride=seg), :] for i in range(4)], axis=1) for j in range(seg)],
        axis=0)


def _time_order(scratch, val, seg):
    for i in range(4):
        scratch[i] = val[:, 128 * i:128 * (i + 1)]
    tiles = []
    for m in range(val.shape[0] // 8):
        s, j0 = divmod(8 * m, seg)
        tiles.append(jnp.concatenate(
            [scratch[i, pl.ds(8 * j0 + s, 8, stride=SEGMENTS), :] for i in range(4)], axis=1))
    return jnp.concatenate(tiles, axis=0)


def _power_table(lam_ref, pw_r, pw_i, seg):
    a_r = jnp.broadcast_to(lam_ref[0:1, :], (SEGMENTS, N_STATE))
    a_i = jnp.broadcast_to(lam_ref[1:2, :], (SEGMENTS, N_STATE))
    p_r, p_i = a_r, a_i
    for k in range(seg):
        pw_r[SEGMENTS * k:SEGMENTS * (k + 1), :] = p_r
        pw_i[SEGMENTS * k:SEGMENTS * (k + 1), :] = p_i
        p_r, p_i = p_r * a_r - p_i * a_i, p_r * a_i + p_i * a_r


def _segment_scan(xr_ref, xi_ref, cols, pw_r, pw_i, hr_s, hi_s, seg, reverse):
    sign = -1.0 if reverse else 1.0
    a_r, a_i = pw_r[0:SEGMENTS, cols], sign * pw_i[0:SEGMENTS, cols]

    def step(n, carry):
        hr, hi = carry
        o = pl.multiple_of((seg - 1 - n if reverse else n) * SEGMENTS, SEGMENTS)
        nr = a_r * hr - a_i * hi + xr_ref[pl.ds(o, SEGMENTS), cols]
        ni = a_r * hi + a_i * hr + xi_ref[pl.ds(o, SEGMENTS), cols]
        xr_ref[pl.ds(o, SEGMENTS), cols] = nr
        xi_ref[pl.ds(o, SEGMENTS), cols] = ni
        return nr, ni

    zero = jnp.zeros((SEGMENTS, cols.stop - cols.start), F32)
    e_r, e_i = lax.fori_loop(0, seg, step, (zero, zero), unroll=2)

    top = SEGMENTS * (seg - 1)
    ls_r, ls_i = pw_r[top:top + 1, cols], sign * pw_i[top:top + 1, cols]
    c_r, c_i = hr_s[0:1, cols], hi_s[0:1, cols]
    in_r, in_i = [None] * SEGMENTS, [None] * SEGMENTS
    for s in (range(SEGMENTS - 1, -1, -1) if reverse else range(SEGMENTS)):
        in_r[s], in_i[s] = c_r, c_i
        c_r, c_i = (e_r[s:s + 1, :] + ls_r * c_r - ls_i * c_i, e_i[s:s + 1, :] + ls_r * c_i + ls_i * c_r)
    hr_s[0:1, cols] = c_r
    hi_s[0:1, cols] = c_i
    cm_r, cm_i = jnp.concatenate(in_r, axis=0), jnp.concatenate(in_i, axis=0)

    def fix(jj, _):
        o = pl.multiple_of(jj * SEGMENTS, SEGMENTS)
        k = pl.multiple_of((seg - 1 - jj if reverse else jj) * SEGMENTS, SEGMENTS)
        p_r, p_i = pw_r[pl.ds(k, SEGMENTS), cols], sign * pw_i[pl.ds(k, SEGMENTS), cols]
        xr_ref[pl.ds(o, SEGMENTS), cols] += p_r * cm_r - p_i * cm_i
        xi_ref[pl.ds(o, SEGMENTS), cols] += p_r * cm_i + p_i * cm_r
        return 0

    lax.fori_loop(0, seg, fix, 0, unroll=2)


def _s5_tail(y, glu_ref, glub):
    g, dgelu = _gelu_and_grad(y)
    gb = g.astype(BF16)
    gate = jnp.concatenate([_dot(gb[:, 128 * j:128 * (j + 1)], glu_ref[j]) for j in range(SUPER)], axis=1) + glub
    sig = _sigmoid(gate)
    return g, gb, dgelu, sig, g * sig


def _s5_fwd(u4, lam, bbr, bbi, crt, cit, vecs, glu, tc):
    n_pad = u4.shape[1]
    seg = tc // SEGMENTS

    def body(u_ref, lam_ref, bbr_ref, bbi_ref, crt_ref, cit_ref, vec_ref, glu_ref,
             sr_ref, si_ref, y_ref, ms_ref, hr_s, hi_s, pw_r, pw_i, lanes):
        @pl.when(pl.program_id(0) == 0)
        def _():
            hr_s[...] = jnp.zeros_like(hr_s)
            hi_s[...] = jnp.zeros_like(hi_s)
            _power_table(lam_ref, pw_r, pw_i, seg)

        u_v = _interleaved(u_ref, seg)
        ub = u_v.astype(BF16)
        for j in range(SUPER):
            uj = ub[:, 128 * j:128 * (j + 1)]
            sr_ref[:, 512 * j:512 * (j + 1)] = _dot(uj, bbr_ref[j])
            si_ref[:, 512 * j:512 * (j + 1)] = _dot(uj, bbi_ref[j])
        for j in range(SUPER):
            _segment_scan(sr_ref, si_ref, slice(512 * j, 512 * (j + 1)), pw_r, pw_i, hr_s, hi_s, seg, False)

        d_row, glub, gs = vec_ref[0:1, :], vec_ref[1:2, :], vec_ref[2:3, :]
        ys_c = []
        for j in range(SUPER):
            sr_j = sr_ref[:, 512 * j:512 * (j + 1)].astype(BF16)
            si_j = si_ref[:, 512 * j:512 * (j + 1)].astype(BF16)
            ys_c.append(_dot(sr_j, crt_ref[j]) - _dot(si_j, cit_ref[j]))
        y = jnp.concatenate(ys_c, axis=1) + d_row * u_v
        y_ref[...] = y
        _, _, _, _, ys = _s5_tail(y, glu_ref, glub)
        ms_ref[...] = _time_order(lanes, ys * _rms(ys) * gs, seg).astype(BF16)

    chunk = lambda w: pl.BlockSpec((tc, w), lambda c: (c, 0))
    lane_blocks = pl.BlockSpec((4, tc, 128), lambda c: (0, c, 0))
    sds = jax.ShapeDtypeStruct
    return pl.pallas_call(
        body, grid=(n_pad // tc,), name="s5_fwd",
        in_specs=[lane_blocks, _full((8, N_STATE)), _full((SUPER, 128, 512)), _full((SUPER, 128, 512)),
                  _full((SUPER, 512, 128)), _full((SUPER, 512, 128)), _full((8, D_SSM)), _full((SUPER, 128, 128))],
        out_specs=[chunk(N_STATE), chunk(N_STATE), chunk(D_SSM), chunk(D_SSM)],
        out_shape=[sds((n_pad, N_STATE), F32), sds((n_pad, N_STATE), F32),
                   sds((n_pad, D_SSM), F32), sds((n_pad, D_SSM), BF16)],
        scratch_shapes=[pltpu.VMEM((8, N_STATE), F32), pltpu.VMEM((8, N_STATE), F32),
                        pltpu.VMEM((tc, N_STATE), F32), pltpu.VMEM((tc, N_STATE), F32),
                        pltpu.VMEM((4, tc, 128), F32)],
        compiler_params=_params(("arbitrary",)),
    )(u4, lam, bbr, bbi, crt, cit, vecs, glu)


def _s5_bwd(dms4, y, u4, sr, si, lam, bbr, bbi, crt, cit, vecs, glu, tc, token):
    n_pad = u4.shape[1]
    nc = n_pad // tc
    seg = tc // SEGMENTS

    def body(dms_ref, y_ref, u_ref, sr_ref, si_ref, pr_ref, pi_ref, lam_ref, bbr_ref, bbi_ref, crt_ref, cit_ref,
             vec_ref, glu_ref, token_ref, du_ref, dbbr_ref, dbbi_ref, dcrt_ref, dcit_ref, dglu_ref, dvec_ref, dlam_ref,
             qr_s, qi_s, cr_s, ci_s, pw_r, pw_i, lanes):
        c = pl.program_id(0)

        @pl.when(c == 0)
        def _():
            for ref in (dbbr_ref, dbbi_ref, dcrt_ref, dcit_ref, dglu_ref, dvec_ref, dlam_ref, cr_s, ci_s):
                ref[...] = jnp.zeros_like(ref)
            _power_table(lam_ref, pw_r, pw_i, seg)

        d_row, glub, gs = vec_ref[0:1, :], vec_ref[1:2, :], vec_ref[2:3, :]
        y_v, u_v = y_ref[...], _interleaved(u_ref, seg)
        ub = u_v.astype(BF16)
        g, gb, dgelu, sig, ys = _s5_tail(y_v, glu_ref, glub)
        r = _rms(ys)
        xhat = ys * r
        dm = _interleaved(dms_ref, seg)
        dys = _rms_bwd(dm * gs, xhat, r)
        dgate = dys * g * sig * (1.0 - sig)
        dgateb = dgate.astype(BF16)
        dg = dys * sig + jnp.concatenate(
            [_dot_nt(dgateb[:, 128 * j:128 * (j + 1)], glu_ref[j]) for j in range(SUPER)], axis=1)
        dy = dg * dgelu
        dyb = dy.astype(BF16)
        dvec_ref[0:1, :] += jnp.sum(dy * u_v, axis=0, keepdims=True)
        dvec_ref[1:2, :] += jnp.sum(dgate, axis=0, keepdims=True)
        dvec_ref[2:3, :] += jnp.sum(dm * xhat, axis=0, keepdims=True)

        for j in range(SUPER):
            cols, states = slice(128 * j, 128 * (j + 1)), slice(512 * j, 512 * (j + 1))
            dglu_ref[j] += _dot_tn(gb[:, cols], dgateb[:, cols])
            dcrt_ref[j] += _dot_tn(sr_ref[:, states].astype(BF16), dyb[:, cols])
            dcit_ref[j] -= _dot_tn(si_ref[:, states].astype(BF16), dyb[:, cols])
            qr_s[:, states] = _dot_nt(dyb[:, cols], crt_ref[j])
            qi_s[:, states] = -_dot_nt(dyb[:, cols], cit_ref[j])

        first = c == nc - 1
        row0 = lax.broadcasted_iota(jnp.int32, (SEGMENTS, 1), 0) == 0
        last = (seg - 1) * SEGMENTS
        for j in range(SUPER):
            states = slice(512 * j, 512 * (j + 1))
            _segment_scan(qr_s, qi_s, states, pw_r, pw_i, cr_s, ci_s, seg, True)

            before_r = jnp.where(first, 0.0, pltpu.roll(pr_ref[:, states], 1, 0))
            before_i = jnp.where(first, 0.0, pltpu.roll(pi_ref[:, states], 1, 0))
            hp_r = jnp.where(row0, before_r, pltpu.roll(sr_ref[pl.ds(last, SEGMENTS), states], 1, 0))
            hp_i = jnp.where(row0, before_i, pltpu.roll(si_ref[pl.ds(last, SEGMENTS), states], 1, 0))
            q_r, q_i = qr_s[pl.ds(0, SEGMENTS), states], qi_s[pl.ds(0, SEGMENTS), states]

            def dlam_step(jj, acc):
                o = pl.multiple_of(jj * SEGMENTS, SEGMENTS)
                above = pl.multiple_of((jj - 1) * SEGMENTS, SEGMENTS)
                h_r, h_i = sr_ref[pl.ds(above, SEGMENTS), states], si_ref[pl.ds(above, SEGMENTS), states]
                t_r, t_i = qr_s[pl.ds(o, SEGMENTS), states], qi_s[pl.ds(o, SEGMENTS), states]
                return acc[0] + t_r * h_r + t_i * h_i, acc[1] + t_i * h_r - t_r * h_i

            acc = lax.fori_loop(1, seg, dlam_step, (q_r * hp_r + q_i * hp_i, q_i * hp_r - q_r * hp_i), unroll=2)
            dlam_ref[0:SEGMENTS, states] += acc[0]
            dlam_ref[SEGMENTS:, states] += acc[1]

        du_c = []
        for j in range(SUPER):
            cols, states = slice(128 * j, 128 * (j + 1)), slice(512 * j, 512 * (j + 1))
            qr_j = qr_s[:, states].astype(BF16)
            qi_j = qi_s[:, states].astype(BF16)
            du_c.append(_dot_nt(qr_j, bbr_ref[j]) + _dot_nt(qi_j, bbi_ref[j]))
            dbbr_ref[j] += _dot_tn(ub[:, cols], qr_j)
            dbbi_ref[j] += _dot_tn(ub[:, cols], qi_j)
        du_ref[...] = _time_order(lanes, jnp.concatenate(du_c, axis=1) + dy * d_row, seg)

    rev = lambda c: nc - 1 - c
    chunk = lambda w: pl.BlockSpec((tc, w), lambda c: (rev(c), 0))
    lane_blocks = pl.BlockSpec((4, tc, 128), lambda c: (0, rev(c), 0))
    prev = pl.BlockSpec((SEGMENTS, N_STATE), lambda c: (jnp.maximum(rev(c) * seg - 1, 0), 0))
    sds = jax.ShapeDtypeStruct
    return pl.pallas_call(
        body, grid=(nc,), name="s5_bwd",
        in_specs=[lane_blocks, chunk(D_SSM), lane_blocks, chunk(N_STATE), chunk(N_STATE), prev, prev,
                  _full((8, N_STATE)), _full((SUPER, 128, 512)), _full((SUPER, 128, 512)),
                  _full((SUPER, 512, 128)), _full((SUPER, 512, 128)), _full((8, D_SSM)), _full((SUPER, 128, 128)), _ANY],
        out_specs=[chunk(D_SSM), _full((SUPER, 128, 512)), _full((SUPER, 128, 512)), _full((SUPER, 512, 128)),
                   _full((SUPER, 512, 128)), _full((SUPER, 128, 128)), _full((8, D_SSM)), _full((2 * SEGMENTS, N_STATE))],
        out_shape=[sds((n_pad, D_SSM), F32), sds((SUPER, 128, 512), F32), sds((SUPER, 128, 512), F32),
                   sds((SUPER, 512, 128), F32), sds((SUPER, 512, 128), F32), sds((SUPER, 128, 128), F32),
                   sds((8, D_SSM), F32), sds((2 * SEGMENTS, N_STATE), F32)],
        scratch_shapes=[pltpu.VMEM((tc, N_STATE), F32), pltpu.VMEM((tc, N_STATE), F32),
                        pltpu.VMEM((8, N_STATE), F32), pltpu.VMEM((8, N_STATE), F32),
                        pltpu.VMEM((tc, N_STATE), F32), pltpu.VMEM((tc, N_STATE), F32),
                        pltpu.VMEM((4, tc, 128), F32)],
        compiler_params=_params(("arbitrary",)),
    )(dms4, y, u4, sr, si, sr, si, lam, bbr, bbi, crt, cit, vecs, glu, token)


def _inv_count(c_idx, tc, w):
    t = c_idx * tc + lax.broadcasted_iota(jnp.int32, (tc, 1), 0)
    return 1.0 / jnp.minimum(t + 1, w).astype(F32)


def _pool_fwd(v, pw_b, vecs, tc):
    n_pad = v.shape[0]

    def body(v_ref, pw_ref, vec_ref, feat_ref, mp_ref, hist):
        c = pl.program_id(0)

        @pl.when(c == 0)
        def _():
            hist[...] = jnp.zeros_like(hist)

        v_v = v_ref[...]
        ext = jnp.concatenate([hist[...], v_v], axis=0)
        hist[...] = v_v[tc - POOL_HALO:, :]
        feats, ps = [], []
        for k, w in enumerate(POOL_WINDOWS):
            cols = slice(128 * k, 128 * (k + 1))
            s = ext[:, cols]
            sh = 1
            while sh < w:
                s = s + pltpu.roll(s, sh, 0)
                sh *= 2
            f = (s[POOL_HALO:, :] * _inv_count(c, tc, w) - v_v[:, cols]).astype(BF16)
            feats.append(f)
            ps.append(_dot(f, pw_ref[k]))
        feat_ref[...] = jnp.concatenate(feats, axis=1)
        yp = jnp.concatenate(ps, axis=1) * vec_ref[0:1, :]
        mp_ref[...] = (yp * _rms(yp) * vec_ref[1:2, :]).astype(BF16)

    chunk = lambda w: pl.BlockSpec((tc, w), lambda c: (c, 0))
    sds = jax.ShapeDtypeStruct
    return pl.pallas_call(
        body, grid=(n_pad // tc,), name="pool_fwd",
        in_specs=[chunk(D_POOL), _full((4, 128, 128)), _full((8, D_POOL))],
        out_specs=[chunk(D_POOL), chunk(D_POOL)],
        out_shape=[sds((n_pad, D_POOL), BF16), sds((n_pad, D_POOL), BF16)],
        scratch_shapes=[pltpu.VMEM((POOL_HALO, D_POOL), F32)],
        compiler_params=_params(("arbitrary",)),
    )(v, pw_b, vecs)


def _pool_bwd(dmp, feat, pw_b, vecs, tc):
    n_pad = dmp.shape[0]
    nc = n_pad // tc

    def body(dmp_ref, feat_ref, pw_ref, vec_ref, dv_ref, dpw_ref, dvec_ref, fut):
        c = pl.program_id(0)

        @pl.when(c == 0)
        def _():
            fut[...] = jnp.zeros_like(fut)
            dpw_ref[...] = jnp.zeros_like(dpw_ref)
            dvec_ref[...] = jnp.zeros_like(dvec_ref)

        scale, gp = vec_ref[0:1, :], vec_ref[1:2, :]
        feat_v = feat_ref[...]
        p = jnp.concatenate([_dot(feat_v[:, 128 * k:128 * (k + 1)], pw_ref[k]) for k in range(4)], axis=1)
        yp = p * scale
        r = _rms(yp)
        xhat = yp * r
        dm = dmp_ref[...]
        dyp = _rms_bwd(dm * gp, xhat, r)
        dvec_ref[0:1, :] += jnp.sum(dyp * p, axis=0, keepdims=True)
        dvec_ref[1:2, :] += jnp.sum(dm * xhat, axis=0, keepdims=True)
        dpb = (dyp * scale).astype(BF16)
        es, dfs = [], []
        for k, w in enumerate(POOL_WINDOWS):
            cols = slice(128 * k, 128 * (k + 1))
            dpw_ref[k] += _dot_tn(feat_v[:, cols], dpb[:, cols])
            df = _dot_nt(dpb[:, cols], pw_ref[k])
            dfs.append(df)
            es.append(df * _inv_count(nc - 1 - c, tc, w))
        e = jnp.concatenate(es, axis=1)
        ext = jnp.concatenate([e, fut[...]], axis=0)
        fut[...] = e[:POOL_HALO, :]
        n_ext = tc + POOL_HALO
        dvs = []
        for k, w in enumerate(POOL_WINDOWS):
            s = ext[:, 128 * k:128 * (k + 1)]
            sh = 1
            while sh < w:
                s = s + pltpu.roll(s, n_ext - sh, 0)
                sh *= 2
            dvs.append(s[:tc, :] - dfs[k])
        dv_ref[...] = jnp.concatenate(dvs, axis=1)

    chunk = lambda w: pl.BlockSpec((tc, w), lambda c: (nc - 1 - c, 0))
    sds = jax.ShapeDtypeStruct
    return pl.pallas_call(
        body, grid=(nc,), name="pool_bwd",
        in_specs=[chunk(D_POOL), chunk(D_POOL), _full((4, 128, 128)), _full((8, D_POOL))],
        out_specs=[chunk(D_POOL), _full((4, 128, 128)), _full((8, D_POOL))],
        out_shape=[sds((n_pad, D_POOL), F32), sds((4, 128, 128), F32), sds((8, D_POOL), F32)],
        scratch_shapes=[pltpu.VMEM((POOL_HALO, D_POOL), F32)],
        compiler_params=_params(("arbitrary",)),
    )(dmp, feat, pw_b, vecs)


def _place():
    x, y, c = lax.axis_index("x"), lax.axis_index("y"), lax.axis_index("c")
    chips = [(1 - x, y), (x, 1 - y), (1 - x, 1 - y)]
    return x, y, c, chips


_ANY = pl.BlockSpec(memory_space=pl.ANY)


def _cast_shards(shards, dtypes, place):
    n = len(shards)

    def body(place_ref, *refs):
        for i in range(n):
            refs[n + i][0] = refs[i][...].astype(dtypes[i])

    return pl.pallas_call(
        body, name="cast_shards",
        grid_spec=pltpu.PrefetchScalarGridSpec(
            num_scalar_prefetch=1, grid=(1,),
            in_specs=[pl.BlockSpec(s.shape, lambda i, p: (0, 0, 0)) for s in shards],
            out_specs=[pl.BlockSpec((1,) + s.shape, lambda i, p: (p[0], 0, 0, 0)) for s in shards]),
        out_shape=[jax.ShapeDtypeStruct((N_SHARD,) + s.shape, dt) for s, dt in zip(shards, dtypes)],
        compiler_params=_params(("arbitrary",)),
    )(place, *shards)


def _gather_shards(full):
    n = len(full)

    def body(*refs):
        outs = refs[n:2 * n]
        ici_send, ici_recv, d2d_send, d2d_recv = refs[2 * n:]
        x, y, c, chips = _place()
        q = 2 * x + y
        sibling = (x, y, 1 - c)

        def ici(i, j, shard, to):
            return pltpu.make_async_remote_copy(src_ref=outs[i].at[q, c], dst_ref=outs[i].at[shard, c],
                                                send_sem=ici_send.at[i, j], recv_sem=ici_recv.at[i, j],
                                                device_id=to, device_id_type=MESH)

        def d2d(i, j, shard, half):
            return pltpu.make_async_remote_copy(src_ref=outs[i].at[shard, c], dst_ref=outs[i].at[shard, half],
                                                send_sem=d2d_send.at[i, j], recv_sem=d2d_recv.at[i, j],
                                                device_id=sibling, device_id_type=MESH)

        sends = [ici(i, j, q, (*chip, c)) for i in range(n) for j, chip in enumerate(chips)]
        for cp in sends:
            cp.start()
        passed = []
        for i in range(n):
            for j, (cx, cy) in enumerate(chips):
                ici(i, j, 2 * cx + cy, (cx, cy, c)).wait_recv()
                cp = d2d(i, j, 2 * cx + cy, c)
                cp.start()
                passed.append(cp)
        for i in range(n):
            for j, (cx, cy) in enumerate(chips):
                d2d(i, j, 2 * cx + cy, 1 - c).wait_recv()
        for cp in sends + passed:
            cp.wait_send()

    return pl.pallas_call(
        body, name="gather_shards",
        in_specs=[_ANY] * n, out_specs=[_ANY] * n,
        out_shape=[jax.ShapeDtypeStruct(f.shape, f.dtype) for f in full],
        input_output_aliases={i: i for i in range(n)},
        scratch_shapes=[pltpu.SemaphoreType.DMA((n, 3)), pltpu.SemaphoreType.DMA((n, 3)),
                        pltpu.SemaphoreType.DMA((n, 3)), pltpu.SemaphoreType.DMA((n, 3))],
    )(*full)


def _forward_halves(full):
    n = len(full)

    def body(*refs):
        outs = refs[n:2 * n]
        send, recv = refs[2 * n:]
        x, y, c, chips = _place()

        def d2d(i, j, shard, half):
            return pltpu.make_async_remote_copy(src_ref=outs[i].at[shard, c], dst_ref=outs[i].at[shard, half],
                                                send_sem=send.at[i, j], recv_sem=recv.at[i, j],
                                                device_id=(x, y, 1 - c), device_id_type=MESH)

        cps = [d2d(i, j, 2 * cx + cy, c) for i in range(n) for j, (cx, cy) in enumerate(chips)]
        for cp in cps:
            cp.start()
        for i in range(n):
            for j, (cx, cy) in enumerate(chips):
                d2d(i, j, 2 * cx + cy, 1 - c).wait_recv()
        for cp in cps:
            cp.wait_send()

    return pl.pallas_call(
        body, name="forward_halves",
        in_specs=[_ANY] * n, out_specs=[_ANY] * n,
        out_shape=[jax.ShapeDtypeStruct(f.shape, f.dtype) for f in full],
        input_output_aliases={i: i for i in range(n)},
        scratch_shapes=[pltpu.SemaphoreType.DMA((n, 3)), pltpu.SemaphoreType.DMA((n, 3))],
    )(*full)


_HBM = pl.BlockSpec(memory_space=pltpu.HBM)
_SEM = pl.BlockSpec(memory_space=pltpu.SEMAPHORE)
_EFFECT = pltpu.SideEffectType.DATAFLOW_SIDE_EFFECTING


def _copies_start(name, arrays, sem_shape, build, after=None):
    n = len(arrays)
    extra = [] if after is None else [after]

    def body(*refs):
        outs = refs[n + len(extra):2 * n + len(extra)]
        send, recv, token = refs[2 * n + len(extra):]
        sends, _ = build(outs, send, recv)
        for cp in sends:
            cp.start()
        token[...] = jnp.zeros_like(token)

    out = pl.pallas_call(
        body, name=name, in_specs=[_HBM] * n + [_ANY] * len(extra),
        out_specs=[_HBM] * n + [_SEM, _SEM, pl.BlockSpec(memory_space=pltpu.VMEM)],
        out_shape=[pltpu.HBM(a.shape, a.dtype) for a in arrays]
        + [pltpu.SemaphoreType.DMA(sem_shape), pltpu.SemaphoreType.DMA(sem_shape), jax.ShapeDtypeStruct((8, 128), F32)],
        input_output_aliases={i: i for i in range(n)},
        compiler_params=pltpu.CompilerParams(has_side_effects=_EFFECT),
    )(*[pltpu.with_memory_space_constraint(a, pltpu.HBM) for a in arrays], *extra)
    return list(out[:n]), (out[n], out[n + 1]), out[n + 2]


def _copies_wait(name, arrays, sems, after, build):
    n = len(arrays)

    def body(*refs):
        ins = refs[:n]
        send, recv = refs[n], refs[n + 1]
        sends, recvs = build(ins, send, recv)
        for cp in sends:
            cp.wait_send()
        for cp in recvs:
            cp.wait_recv()

    return list(pl.pallas_call(
        body, name=name, in_specs=[_HBM] * n + [_SEM, _SEM] + [_ANY] * len(after), out_specs=[_HBM] * n,
        out_shape=[pltpu.HBM(a.shape, a.dtype) for a in arrays],
        input_output_aliases={i: i for i in range(n)},
        compiler_params=pltpu.CompilerParams(has_side_effects=_EFFECT),
    )(*arrays, *sems, *after))


def _remote(src, dst, send_sem, recv_sem, to):
    return pltpu.make_async_remote_copy(src_ref=src, dst_ref=dst, send_sem=send_sem, recv_sem=recv_sem,
                                        device_id=to, device_id_type=MESH)


def _build_gather(refs, send, recv):
    x, y, c, chips = _place()
    q = 2 * x + y
    pairs = [(i, j, chip) for i in range(len(refs)) for j, chip in enumerate(chips)]
    sends = [_remote(refs[i].at[q, c], refs[i].at[q, c], send.at[3 * i + j], recv.at[3 * i + j], (cx, cy, c))
             for i, j, (cx, cy) in pairs]
    recvs = [_remote(refs[i].at[q, c], refs[i].at[2 * cx + cy, c], send.at[3 * i + j], recv.at[3 * i + j], (cx, cy, c))
             for i, j, (cx, cy) in pairs]
    return sends, recvs


def _build_swap(refs, send, recv):
    x, y, c, _ = _place()
    n = len(refs) // 2
    cps = [_remote(refs[i].at[:, 1 - c], refs[n + i], send.at[i], recv.at[i], (x, y, 1 - c)) for i in range(n)]
    return cps, cps


def _build_exchange(refs, send, recv):
    x, y, c, chips = _place()
    n = len(refs) // 2
    cps = [_remote(refs[i].at[2 * cx + cy], refs[n + i].at[j], send.at[3 * i + j], recv.at[3 * i + j], (cx, cy, c))
           for i in range(n) for j, (cx, cy) in enumerate(chips)]
    return cps, cps


def _build_spread(refs, send, recv):
    x, y, c, _ = _place()
    flip = lambda bit, on: bit + on - 2 * bit * on
    cps = [_remote(refs[0], refs[1].at[r - 1], send.at[r - 1], recv.at[r - 1],
                   (flip(x, r >> 2 & 1), flip(y, r >> 1 & 1), flip(c, r & 1))) for r in (1, 2, 4, 6)]
    return cps, cps


def _forward_small(landed):
    def body(in_ref, out_ref, send, recv):
        x, y, c, _ = _place()
        cps = [_remote(out_ref.at[r - 1], out_ref.at[r], send.at[k], recv.at[k], (x, y, 1 - c))
               for k, r in enumerate((2, 4, 6))]
        for cp in cps:
            cp.start()
        for cp in cps:
            cp.wait()

    return pl.pallas_call(
        body, name="forward_small",
        in_specs=[_ANY], out_specs=_ANY, out_shape=jax.ShapeDtypeStruct(landed.shape, F32),
        input_output_aliases={0: 0},
        scratch_shapes=[pltpu.SemaphoreType.DMA((3,)), pltpu.SemaphoreType.DMA((3,))],
    )(landed)


def _swap_halves(grads):
    n = len(grads)

    def body(*refs):
        ins, outs = refs[:n], refs[n:2 * n]
        send, recv = refs[2 * n:]
        x, y, c, _ = _place()
        cps = [pltpu.make_async_remote_copy(src_ref=ins[i].at[:, 1 - c], dst_ref=outs[i], send_sem=send.at[i],
                                            recv_sem=recv.at[i], device_id=(x, y, 1 - c), device_id_type=MESH)
               for i in range(n)]
        for cp in cps:
            cp.start()
        for cp in cps:
            cp.wait()

    return pl.pallas_call(
        body, name="swap_halves",
        in_specs=[_ANY] * n, out_specs=[_ANY] * n,
        out_shape=[jax.ShapeDtypeStruct((N_SHARD,) + g.shape[2:], F32) for g in grads],
        scratch_shapes=[pltpu.SemaphoreType.DMA((n,)), pltpu.SemaphoreType.DMA((n,))],
    )(*grads)


def _join_halves(pairs):
    n = len(pairs)

    def body(*refs):
        outs = refs[n:2 * n]
        send, recv = refs[2 * n:]
        x, y, c, _ = _place()
        cps = [pltpu.make_async_remote_copy(src_ref=outs[i].at[c], dst_ref=outs[i].at[c], send_sem=send.at[i],
                                            recv_sem=recv.at[i], device_id=(x, y, 1 - c), device_id_type=MESH)
               for i in range(n)]
        for cp in cps:
            cp.start()
        for i in range(n):
            cps[i].wait_send()
            pltpu.make_async_remote_copy(src_ref=outs[i].at[c], dst_ref=outs[i].at[1 - c], send_sem=send.at[i],
                                         recv_sem=recv.at[i], device_id=(x, y, 1 - c), device_id_type=MESH).wait_recv()

    return pl.pallas_call(
        body, name="join_halves",
        in_specs=[_ANY] * n, out_specs=[_ANY] * n,
        out_shape=[jax.ShapeDtypeStruct(p.shape, F32) for p in pairs],
        input_output_aliases={i: i for i in range(n)},
        scratch_shapes=[pltpu.SemaphoreType.DMA((n,)), pltpu.SemaphoreType.DMA((n,))],
    )(*pairs)


N_SPLIT = 2


def _sum_siblings(tag, grads, recvd, place):
    n = len(grads)

    def body(place_ref, *refs):
        g_refs, r_refs, sb_refs, own_refs = (refs[k * n:(k + 1) * n] for k in range(4))
        s = pl.program_id(1)
        for i in range(n):
            tot = g_refs[i][0, 0] + r_refs[i][0]
            sb_refs[i][0] = tot.astype(BF16)

            @pl.when(s == place_ref[0])
            def _():
                own_refs[i][...] = tot

    in_specs, sb_specs, own_specs, sb_shapes, own_shapes = [], [], [], [], []
    for g in grads:
        _, _, r, cdim = g.shape
        rb = r // N_SPLIT
        in_specs.append(pl.BlockSpec((1, 1, rb, cdim), lambda b, s, p: (s, p[1], b, 0)))
        sb_specs.append(pl.BlockSpec((1, rb, cdim), lambda b, s, p: (s, b, 0)))
        own_specs.append(pl.BlockSpec((rb, cdim), lambda b, s, p: (b, 0)))
        sb_shapes.append(jax.ShapeDtypeStruct((N_SHARD, r, cdim), BF16))
        own_shapes.append(jax.ShapeDtypeStruct((r, cdim), F32))
    out = pl.pallas_call(
        body, name="sum_siblings_" + tag,
        grid_spec=pltpu.PrefetchScalarGridSpec(
            num_scalar_prefetch=1, grid=(N_SPLIT, N_SHARD),
            in_specs=in_specs + sb_specs, out_specs=sb_specs + own_specs),
        out_shape=sb_shapes + own_shapes,
        compiler_params=_params(("parallel", "arbitrary")),
    )(place, *grads, *recvd)
    return out[:n], out[n:]


def _sum_chips(own, recvd, place):
    n = len(own)

    def body(place_ref, *refs):
        o_refs, r_refs, out_refs = (refs[k * n:(k + 1) * n] for k in range(3))
        for i in range(n):
            tot = o_refs[i][...]
            for j in range(3):
                tot = tot + r_refs[i][j].astype(F32)
            out_refs[i][0] = tot

    o_specs, r_specs, out_specs = [], [], []
    for o in own:
        r, cdim = o.shape
        rb = r // N_SPLIT
        o_specs.append(pl.BlockSpec((rb, cdim), lambda b, p: (b, 0)))
        r_specs.append(pl.BlockSpec((3, rb, cdim), lambda b, p: (0, b, 0)))
        out_specs.append(pl.BlockSpec((1, rb, cdim), lambda b, p: (p[1], b, 0)))
    return pl.pallas_call(
        body, name="sum_chips",
        grid_spec=pltpu.PrefetchScalarGridSpec(num_scalar_prefetch=1, grid=(N_SPLIT,),
                                               in_specs=o_specs + r_specs, out_specs=out_specs),
        out_shape=[jax.ShapeDtypeStruct((2,) + o.shape, F32) for o in own],
        compiler_params=_params(("parallel",)),
    )(place, *own, *recvd)


def _adamw_math(w, g, m, v):
    m = ADAM_B1 * m + (1.0 - ADAM_B1) * g
    v = ADAM_B2 * v + (1.0 - ADAM_B2) * (g * g)
    m_hat = m / (1.0 - ADAM_B1 ** ADAM_STEP)
    v_hat = v / (1.0 - ADAM_B2 ** ADAM_STEP)
    delta = -ADAM_LR * (m_hat / (jnp.sqrt(v_hat) + ADAM_EPS) + ADAM_WD * w)
    return delta, m, v


def _adamw(name, ws, gs, ms, vs, n_split):
    n = len(ws)

    def body(*refs):
        w_r, g_r, m_r, v_r, d_o, m_o, v_o = (refs[k * n:(k + 1) * n] for k in range(7))
        for i in range(n):
            d, m, v = _adamw_math(w_r[i][...], g_r[i][...], m_r[i][...], v_r[i][...])
            d_o[i][...] = d
            m_o[i][...] = m
            v_o[i][...] = v

    specs = [pl.BlockSpec((w.shape[0] // n_split, w.shape[1]), lambda b: (b, 0)) for w in ws]
    shapes = [jax.ShapeDtypeStruct(w.shape, F32) for w in ws]
    out = pl.pallas_call(
        body, name=name, grid=(n_split,),
        in_specs=specs * 4, out_specs=specs * 3, out_shape=shapes * 3,
        compiler_params=_params(("parallel",)),
    )(*ws, *gs, *ms, *vs)
    return out[:n], out[n:2 * n], out[2 * n:]


def _reduce_small(own, received):
    def body(own_ref, recv_ref, g_out):
        me = 4 * lax.axis_index("x") + 2 * lax.axis_index("y") + lax.axis_index("c")
        g = None
        for k in range(8):
            mine = me == k
            part = jnp.where(mine, own_ref[...], recv_ref[jnp.where(mine, 0, jnp.bitwise_xor(me, k) - 1)])
            g = part if g is None else g + part
        g_out[...] = g

    return pl.pallas_call(
        body, name="reduce_small",
        out_shape=jax.ShapeDtypeStruct(own.shape, F32),
        compiler_params=_params(),
    )(own, received)


def _adamw_small(ws, gs, ms, vs):
    n = len(ws)

    def body(*refs):
        w_r, g_r, m_r, v_r, d_o, m_o, v_o = (refs[k * n:(k + 1) * n] for k in range(7))
        for i in range(n):
            d, mm, vv = _adamw_math(w_r[i][...], g_r[i][...], m_r[i][...], v_r[i][...])
            d_o[i][...] = d
            m_o[i][...] = mm
            v_o[i][...] = vv

    out = pl.pallas_call(
        body, name="adamw_small",
        out_shape=[jax.ShapeDtypeStruct(t.shape, F32) for t in ws] * 3,
        compiler_params=_params(),
    )(*ws, *gs, *ms, *vs)
    return out[:n], out[n:2 * n], out[2 * n:]


def _s5_operands(lam_re, lam_im, log_step, b_re, b_im, c_re, c_im, glu_w):
    lr = jnp.minimum(lam_re, -1e-4)
    li = lam_im
    step = jnp.exp(log_step)[:, None]
    mag = jnp.exp(lr * step)
    ang = li * step
    abr = mag * jnp.cos(ang)
    abi = mag * jnp.sin(ang)
    nr = abr - 1.0
    ni = abi
    den = lr * lr + li * li
    cr = ((nr * lr + ni * li) / den)[..., None]
    ci = ((ni * lr - nr * li) / den)[..., None]
    bbr = cr * b_re - ci * b_im
    bbi = cr * b_im + ci * b_re
    eye = jnp.eye(8, dtype=F32)
    g, h, p = SSM_GROUPS // SUPER, SSM_GROUP, SSM_STATE

    def b_layout(t):
        return jnp.einsum("ab,japh->jahbp", eye, t.reshape(SUPER, g, p, h)).reshape(SUPER, g * h, g * p)

    def c_layout(t):
        return jnp.einsum("ab,jahp->jbpah", eye, t.reshape(SUPER, g, h, p)).reshape(SUPER, g * p, g * h)

    glu = jnp.einsum("ab,jahk->jahbk", eye, glu_w.reshape(SUPER, g, h, h)).reshape(SUPER, g * h, g * h)
    lam = _pad_rows(jnp.concatenate([abr.reshape(1, N_STATE), abi.reshape(1, N_STATE)], axis=0), 8)
    return lam, b_layout(bbr), b_layout(bbi), c_layout(c_re), c_layout(c_im), glu


def _pad_rows(a, rows):
    return jnp.pad(a, ((0, rows - a.shape[0]), (0, 0)))


def _pack(parts):
    rows = []
    for a in parts:
        flat = a.reshape(-1)
        n = -(-flat.shape[0] // 128)
        rows.append(jnp.pad(flat, (0, n * 128 - flat.shape[0])).reshape(n, 128))
    out = jnp.concatenate(rows, axis=0)
    return _pad_rows(out, -(-out.shape[0] // 8) * 8)


def _unpack(packed, like):
    out, at = [], 0
    for a in like:
        n = -(-a.size // 128)
        out.append(packed[at:at + n].reshape(-1)[:a.size].reshape(a.shape))
        at += n
    return out


SMALL = ("norm1_g", "ssm_lambda_re", "ssm_lambda_im", "ssm_log_step", "ssm_b_re", "ssm_b_im", "ssm_c_re", "ssm_c_im",
         "ssm_d", "ssm_glu_w", "ssm_glu_b", "ssm_norm_g", "pool_w", "pool_scale", "pool_norm_g", "norm2_g",
         "final_norm_g")
LARGE = ("w_in", "w_out", "w_gate", "w_up", "w_down")
WEIGHTS = ("meta_tokens", "norm1_g", "w_in", "ssm_lambda_re", "ssm_lambda_im", "ssm_log_step", "ssm_b_re", "ssm_b_im",
           "ssm_c_re", "ssm_c_im", "ssm_d", "ssm_glu_w", "ssm_glu_b", "ssm_norm_g", "pool_w", "pool_scale",
           "pool_norm_g", "w_out", "norm2_g", "w_gate", "w_up", "w_down", "final_norm_g")


def _step(x, target, w, m, v):
    seq = x.shape[1]
    n_rows = N_META + seq
    n_pad, tm, tc, tg = _plan(n_rows)
    xq, yq, cq = lax.axis_index("x"), lax.axis_index("y"), lax.axis_index("c")
    place = jnp.stack([2 * xq + yq, cq]).astype(jnp.int32)

    def halves(a2d):
        return a2d.reshape(2, a2d.shape[0] // 2, a2d.shape[1])

    def local2d(t):
        return {"w_gate": lambda a: a[0].T, "w_up": lambda a: a[0].T}.get(t, lambda a: a[0])

    shards = [halves(local2d(k)(w[k])) for k in LARGE] + [halves(w["meta_tokens"])]
    full = _cast_shards(shards, [BF16] * len(LARGE) + [F32], place)
    w_in_full, meta_full = _gather_shards([full[0], full[5]])
    late, gather_sems, gather_token = _copies_start("gather_start", list(full[1:5]), (12,), _build_gather,
                                                    after=w_in_full)
    w_in_b = w_in_full.reshape(D_MODEL, D_MODEL)
    meta = meta_full.reshape(N_SHARD, N_META, D_MODEL // N_SHARD).transpose(1, 0, 2).reshape(N_META, D_MODEL)

    h0 = _pad_rows(jnp.concatenate([meta, x[0]], axis=0), n_pad)
    tgt = _pad_rows(jnp.concatenate([jnp.zeros((N_META, D_MODEL), F32), target[0]], axis=0), n_pad)
    s5_in = (w["ssm_lambda_re"][0], w["ssm_lambda_im"][0], w["ssm_log_step"][0], w["ssm_b_re"][0], w["ssm_b_im"][0],
             w["ssm_c_re"][0], w["ssm_c_im"][0], w["ssm_glu_w"][0])
    (lam, bbr, bbi, crt, cit, glu), s5_vjp = jax.vjp(_s5_operands, *s5_in)
    bbr_b, bbi_b, crt_b, cit_b, glu_b16 = (t.astype(BF16) for t in (bbr, bbi, crt, cit, glu))
    s5_vecs = _pad_rows(jnp.concatenate([w["ssm_d"].reshape(1, D_SSM), w["ssm_glu_b"].reshape(1, D_SSM),
                                         w["ssm_norm_g"].reshape(1, D_SSM)], axis=0), 8)
    pool_vecs = _pad_rows(jnp.concatenate([w["pool_scale"].reshape(1, D_POOL), w["pool_norm_g"].reshape(1, D_POOL)],
                                          axis=0), 8)
    pw_b = w["pool_w"][0].astype(BF16)
    g1, g2, gf = w["norm1_g"].reshape(1, D_MODEL), w["norm2_g"].reshape(1, D_MODEL), w["final_norm_g"].reshape(1, D_MODEL)

    u, vv = _fwd_in(h0, g1, w_in_b, tm, gather_token)
    sr, si, y, ms = _s5_fwd(u, lam, bbr_b, bbi_b, crt_b, cit_b, s5_vecs, glu_b16, tc)
    feat, mp = _pool_fwd(vv, pw_b, pool_vecs, tc)
    late = _forward_halves(_copies_wait("gather_wait", late, gather_sems, [ms, mp], _build_gather))
    w_out_b = late[0].reshape(D_MODEL, D_MODEL)
    wg_b, wu_b, wd_b = (t.reshape(N_SHARD, FF_SHARD, D_MODEL) for t in late[1:])
    h1, n2, a, b, ff, dh2, dh2b, loss_acc, dgf = _fwd_ffn(h0, ms, mp, w_out_b, g2, wg_b, wu_b, wd_b, gf, tgt, tm, n_rows)

    def quarters(t):
        if t.ndim == 2:
            t = t.reshape(N_SHARD, t.shape[0] // N_SHARD, t.shape[1])
        return t.reshape(N_SHARD, 2, t.shape[1] // 2, t.shape[2])

    def landing(like, lead, dtype):
        return [lax.empty((lead,) + t.shape[2:], dtype) for t in like]

    da, db, dh1, dg2 = _bwd_ffn(dh2, a, b, wg_b, wu_b, wd_b, h1, g2, tm)
    ffn_g = [quarters(t) for t in _grad_ffn(n2, da, db, ff, dh2b, tg)]
    nf = len(ffn_g)
    moved, swap_sems, swap_token = _copies_start("swap_start", ffn_g + landing(ffn_g, N_SHARD, F32), (nf,), _build_swap)
    dms, dmp, dwo = _bwd_out(dh1, ms, mp, w_out_b, tm, swap_token)
    moved = _copies_wait("swap_wait", moved, swap_sems, [dwo], _build_swap)
    ffn_parts, ffn_own = _sum_siblings("ffn", moved[:nf], moved[nf:], place)
    moved, exch_sems, exch_token = _copies_start("exchange_start", list(ffn_parts) + landing(ffn_g, 3, BF16), (3 * nf,),
                                                 _build_exchange)
    du, dbbr, dbbi, dcrt, dcit, dglu, ds5v, dlam = _s5_bwd(dms, y, u, sr, si, lam, bbr_b, bbi_b, crt_b, cit_b,
                                                           s5_vecs, glu_b16, tc, exch_token)
    dv, dpw, dpoolv = _pool_bwd(dmp, feat, pw_b, pool_vecs, tc)
    dh0, dwi, dg1 = _bwd_in(du, dv, h0, dh1, g1, w_in_b, tm)
    ffn_from_chips = _copies_wait("exchange_wait", moved, exch_sems, [dh0], _build_exchange)[nf:]
    dlam = _pad_rows(jnp.concatenate([jnp.sum(dlam[:SEGMENTS], axis=0, keepdims=True),
                                      jnp.sum(dlam[SEGMENTS:], axis=0, keepdims=True)], axis=0), 8)
    d_lre, d_lim, d_lstep, d_bre, d_bim, d_cre, d_cim, d_gluw = s5_vjp((dlam, dbbr, dbbi, dcrt, dcit, dglu))
    grad_x = dh0[N_META:n_rows][None]

    small_g = {
        "norm1_g": dg1, "ssm_lambda_re": d_lre, "ssm_lambda_im": d_lim, "ssm_log_step": d_lstep, "ssm_b_re": d_bre,
        "ssm_b_im": d_bim, "ssm_c_re": d_cre, "ssm_c_im": d_cim, "ssm_d": ds5v[0], "ssm_glu_w": d_gluw,
        "ssm_glu_b": ds5v[1], "ssm_norm_g": ds5v[2], "pool_w": dpw, "pool_scale": dpoolv[0], "pool_norm_g": dpoolv[1],
        "norm2_g": dg2, "final_norm_g": dgf,
    }
    like = [w[k] for k in SMALL]
    packed_g = _pack([small_g[k].reshape(w[k].shape) for k in SMALL] + [dh0[:N_META], loss_acc[0:1, 0:1]])

    mix_g = [quarters(t) for t in (dwi, dwo)]
    mix_parts, mix_own = _sum_siblings("mix", mix_g, _swap_halves(mix_g), place)
    moved, mix_sems, mix_token = _copies_start("mix_exchange_start", list(mix_parts) + landing(mix_g, 3, BF16),
                                               (3 * len(mix_g),), _build_exchange)
    spread, small_sems, small_token = _copies_start(
        "small_start", [packed_g, lax.empty((7,) + packed_g.shape, F32)], (7,), _build_spread, after=mix_token)
    mix_from_chips = _copies_wait("mix_exchange_wait", moved, mix_sems, [small_token], _build_exchange)[len(mix_g):]
    joined = _join_halves(_sum_chips(list(mix_own) + list(ffn_own), list(mix_from_chips) + list(ffn_from_chips), place))
    g_large = [j.reshape(j.shape[0] * j.shape[1], j.shape[2]) for j in joined]
    w2d, m2d, v2d = ([local2d(k)(t[k]) for k in LARGE] for t in (w, m, v))
    d_large, m_large, v_large = _adamw("adamw_large", w2d, g_large, m2d, v2d, 8)

    own_g, landed = _copies_wait("small_wait", spread, small_sems, [d_large[0]], _build_spread)
    g_pk = _reduce_small(own_g, _forward_small(landed))
    g_small = _unpack(g_pk, like + [jax.ShapeDtypeStruct((N_META, D_MODEL), F32), jax.ShapeDtypeStruct((1, 1), F32)])
    loss = g_small.pop()[0, 0]
    rows2d = lambda t: t.reshape(1, -1) if t.ndim == 1 else t
    d_small, m_small, v_small = _adamw_small(*([rows2d(t) for t in ts] for ts in (
        like, g_small[:-1], [m[k] for k in SMALL], [v[k] for k in SMALL])))
    d_small, m_small, v_small = ([t.reshape(w[k].shape) for t, k in zip(ts, SMALL)] for ts in (d_small, m_small, v_small))
    q = place[0]
    g_meta = lax.dynamic_slice_in_dim(g_small[-1], q * (D_MODEL // N_SHARD), D_MODEL // N_SHARD, axis=1)
    d_meta, m_meta, v_meta = _adamw("adamw_meta", [w["meta_tokens"]], [g_meta], [m["meta_tokens"]],
                                    [v["meta_tokens"]], 1)

    grads, deltas, new_m, new_v = {}, {}, {}, {}
    for i, k in enumerate(SMALL):
        grads[k], deltas[k], new_m[k], new_v[k] = g_small[i], d_small[i], m_small[i], v_small[i]
    for i, k in enumerate(LARGE):
        back = (lambda t: t.T[None]) if k in ("w_gate", "w_up") else (lambda t: t[None])
        grads[k], deltas[k], new_m[k], new_v[k] = (back(t) for t in (g_large[i], d_large[i], m_large[i], v_large[i]))
    grads["meta_tokens"], deltas["meta_tokens"] = g_meta, d_meta[0]
    new_m["meta_tokens"], new_v["meta_tokens"] = m_meta[0], v_meta[0]
    return (loss, grad_x, *[grads[k] for k in WEIGHTS], *[deltas[k] for k in WEIGHTS],
            *[new_m[k] for k in WEIGHTS], *[new_v[k] for k in WEIGHTS])


def kernel(x, meta_tokens, norm1_g, w_in, ssm_lambda_re, ssm_lambda_im, ssm_log_step, ssm_b_re, ssm_b_im, ssm_c_re, ssm_c_im, ssm_d, ssm_glu_w, ssm_glu_b, ssm_norm_g, pool_w, pool_scale, pool_norm_g, w_out, norm2_g, w_gate, w_up, w_down, final_norm_g, loss_target, m_meta_tokens, m_norm1_g, m_w_in, m_ssm_lambda_re, m_ssm_lambda_im, m_ssm_log_step, m_ssm_b_re, m_ssm_b_im, m_ssm_c_re, m_ssm_c_im, m_ssm_d, m_ssm_glu_w, m_ssm_glu_b, m_ssm_norm_g, m_pool_w, m_pool_scale, m_pool_norm_g, m_w_out, m_norm2_g, m_w_gate, m_w_up, m_w_down, m_final_norm_g, v_meta_tokens, v_norm1_g, v_w_in, v_ssm_lambda_re, v_ssm_lambda_im, v_ssm_log_step, v_ssm_b_re, v_ssm_b_im, v_ssm_c_re, v_ssm_c_im, v_ssm_d, v_ssm_glu_w, v_ssm_glu_b, v_ssm_norm_g, v_pool_w, v_pool_scale, v_pool_norm_g, v_w_out, v_norm2_g, v_w_gate, v_w_up, v_w_down, v_final_norm_g):
    w = dict(meta_tokens=meta_tokens, norm1_g=norm1_g, w_in=w_in, ssm_lambda_re=ssm_lambda_re, ssm_lambda_im=ssm_lambda_im, ssm_log_step=ssm_log_step, ssm_b_re=ssm_b_re, ssm_b_im=ssm_b_im, ssm_c_re=ssm_c_re, ssm_c_im=ssm_c_im, ssm_d=ssm_d, ssm_glu_w=ssm_glu_w, ssm_glu_b=ssm_glu_b, ssm_norm_g=ssm_norm_g, pool_w=pool_w, pool_scale=pool_scale, pool_norm_g=pool_norm_g, w_out=w_out, norm2_g=norm2_g, w_gate=w_gate, w_up=w_up, w_down=w_down, final_norm_g=final_norm_g)
    m = dict(meta_tokens=m_meta_tokens, norm1_g=m_norm1_g, w_in=m_w_in, ssm_lambda_re=m_ssm_lambda_re, ssm_lambda_im=m_ssm_lambda_im, ssm_log_step=m_ssm_log_step, ssm_b_re=m_ssm_b_re, ssm_b_im=m_ssm_b_im, ssm_c_re=m_ssm_c_re, ssm_c_im=m_ssm_c_im, ssm_d=m_ssm_d, ssm_glu_w=m_ssm_glu_w, ssm_glu_b=m_ssm_glu_b, ssm_norm_g=m_ssm_norm_g, pool_w=m_pool_w, pool_scale=m_pool_scale, pool_norm_g=m_pool_norm_g, w_out=m_w_out, norm2_g=m_norm2_g, w_gate=m_w_gate, w_up=m_w_up, w_down=m_w_down, final_norm_g=m_final_norm_g)
    v = dict(meta_tokens=v_meta_tokens, norm1_g=v_norm1_g, w_in=v_w_in, ssm_lambda_re=v_ssm_lambda_re, ssm_lambda_im=v_ssm_lambda_im, ssm_log_step=v_ssm_log_step, ssm_b_re=v_ssm_b_re, ssm_b_im=v_ssm_b_im, ssm_c_re=v_ssm_c_re, ssm_c_im=v_ssm_c_im, ssm_d=v_ssm_d, ssm_glu_w=v_ssm_glu_w, ssm_glu_b=v_ssm_glu_b, ssm_norm_g=v_ssm_norm_g, pool_w=v_pool_w, pool_scale=v_pool_scale, pool_norm_g=v_pool_norm_g, w_out=v_w_out, norm2_g=v_norm2_g, w_gate=v_w_gate, w_up=v_w_up, w_down=v_w_down, final_norm_g=v_final_norm_g)
    return _step(x, loss_target, w, m, v)
```

```python
import functools
import math

import jax
import jax.numpy as jnp
from jax import lax
from jax.experimental import pallas as pl
from jax.experimental.pallas import tpu as pltpu

F32 = jnp.float32
BF16 = jnp.bfloat16
MESH = pl.DeviceIdType.MESH
AXES = ("x", "y", "c")

D_MODEL = 1024
D_SSM = 512
D_POOL = 512
N_META = 16
SSM_GROUP = 16
SSM_GROUPS = 32
SSM_STATE = 64
N_STATE = SSM_GROUPS * SSM_STATE
STATE_BLOCKS = N_STATE // 128
SUPER = 4
POOL_WINDOWS = (2, 4, 8, 16)
POOL_HALO = 16
D_FF = 2816
N_SHARD = 4
FF_SHARD = D_FF // N_SHARD
EPS = 1e-6
ADAM_LR, ADAM_B1, ADAM_B2, ADAM_EPS, ADAM_WD, ADAM_STEP = 0.001, 0.9, 0.999, 1e-08, 0.01, 10
VMEM_LIMIT = 56 * 1024 * 1024


def _plan(n_rows):
    if n_rows > 2048:
        tm, tp, tc, tg = 416, 832, 320, 1040
    else:
        tm, tp, tc, tg = 128, 128, 64, 128
    step = math.lcm(tm, tp, tc, tg)
    return -(-n_rows // step) * step, tm, tp, tc, tg


def _params(sem=None):
    return pltpu.CompilerParams(dimension_semantics=sem, vmem_limit_bytes=VMEM_LIMIT)


def _dot(a, b):
    return jnp.dot(a, b, preferred_element_type=F32)


def _dot_nt(a, b):
    return lax.dot_general(a, b, (((1,), (1,)), ((), ())), preferred_element_type=F32)


def _dot_tn(a, b):
    return lax.dot_general(a, b, (((0,), (0,)), ((), ())), preferred_element_type=F32)


def _sigmoid(x):
    return 0.5 * jnp.tanh(0.5 * x) + 0.5


_GELU_C = math.sqrt(2.0 / math.pi)


def _gelu_and_grad(y):
    y2 = y * y
    t = jnp.tanh(_GELU_C * (y + 0.044715 * y * y2))
    g = 0.5 * y * (1.0 + t)
    dg = 0.5 * (1.0 + t) + 0.5 * y * (1.0 - t * t) * (_GELU_C * (1.0 + 3.0 * 0.044715 * y2))
    return g, dg


def _rms(x):
    return lax.rsqrt(jnp.mean(x * x, axis=-1, keepdims=True) + EPS)


def _rms_bwd(dn, xhat, r):
    return r * (dn - xhat * jnp.mean(dn * xhat, axis=-1, keepdims=True))


def _full(shape):
    nd = len(shape)
    return pl.BlockSpec(shape, lambda *_: (0,) * nd)


def _fwd_in(h0, g1, w_in_b, tm, token):
    n_pad = h0.shape[0]

    def body(h_ref, g_ref, w_ref, token_ref, u_ref, v_ref):
        h = h_ref[...]
        n1 = (h * _rms(h) * g_ref[...]).astype(BF16)
        proj = _dot(n1, w_ref[...])
        for i in range(4):
            u_ref[i] = proj[:, 128 * i:128 * (i + 1)]
        v_ref[...] = proj[:, D_SSM:]

    row = lambda w: pl.BlockSpec((tm, w), lambda i: (i, 0))
    return pl.pallas_call(
        body, grid=(n_pad // tm,), name="fwd_in",
        in_specs=[row(D_MODEL), _full((1, D_MODEL)), _full((D_MODEL, D_MODEL)), _ANY],
        out_specs=[pl.BlockSpec((4, tm, 128), lambda i: (0, i, 0)), row(D_POOL)],
        out_shape=[jax.ShapeDtypeStruct((4, n_pad, 128), F32), jax.ShapeDtypeStruct((n_pad, D_POOL), F32)],
        compiler_params=_params(("parallel",)),
    )(h0, g1, w_in_b, token)


def _fwd_ffn(h0, ms, mp, w_out_b, g2, wg_b, wu_b, wd_b, gf, target, tm, n_valid):
    n_pad = h0.shape[0]
    nt = n_pad // tm

    def body(h0_ref, ms_ref, mp_ref, wo_ref, g2_ref, wg_ref, wu_ref, wd_ref, gf_ref, tgt_ref,
             h1_ref, n2_ref, a_ref, b_ref, ff_ref, dh2_ref, dh2b_ref, loss_ref, dgf_ref, acc):
        i, q = pl.program_id(0), pl.program_id(1)

        @pl.when((i == 0) & (q == 0))
        def _():
            loss_ref[...] = jnp.zeros_like(loss_ref)
            dgf_ref[...] = jnp.zeros_like(dgf_ref)

        @pl.when(q == 0)
        def _():
            h1 = h0_ref[...] + _dot(ms_ref[...], wo_ref[:D_SSM, :]) + _dot(mp_ref[...], wo_ref[D_SSM:, :])
            h1_ref[...] = h1
            acc[...] = h1
            n2_ref[...] = (h1 * _rms(h1) * g2_ref[...]).astype(BF16)

        n2 = n2_ref[...]
        a = _dot_nt(n2, wg_ref[0])
        b = _dot_nt(n2, wu_ref[0])
        a_ref[0] = a.astype(BF16)
        b_ref[0] = b.astype(BF16)
        ff = (a * _sigmoid(a) * b).astype(BF16)
        ff_ref[0] = ff
        acc[...] += _dot(ff, wd_ref[0])

        @pl.when(q == N_SHARD - 1)
        def _():
            h2 = acc[...]
            r = _rms(h2)
            xhat = h2 * r
            gf_row = gf_ref[...]
            rows = i * tm + lax.broadcasted_iota(jnp.int32, (tm, 1), 0)
            valid = (rows >= N_META) & (rows < n_valid)
            diff = jnp.where(valid, xhat * gf_row - tgt_ref[...], 0.0)
            loss_ref[...] += jnp.full(loss_ref.shape, 0.5 / D_MODEL, F32) * jnp.sum(diff * diff)
            dout = diff * (1.0 / D_MODEL)
            dgf_ref[...] += jnp.sum(dout * xhat, axis=0, keepdims=True)
            dh2 = _rms_bwd(dout * gf_row, xhat, r)
            dh2_ref[...] = dh2
            dh2b_ref[...] = dh2.astype(BF16)

    row = lambda w: pl.BlockSpec((tm, w), lambda i, q: (i, 0))
    shard_rows = pl.BlockSpec((1, FF_SHARD, D_MODEL), lambda i, q: (q, 0, 0))
    act = pl.BlockSpec((1, tm, FF_SHARD), lambda i, q: (q, i, 0))
    sds = jax.ShapeDtypeStruct
    return pl.pallas_call(
        body, grid=(nt, N_SHARD), name="fwd_ffn",
        in_specs=[row(D_MODEL), row(D_SSM), row(D_POOL), _full((D_MODEL, D_MODEL)), _full((1, D_MODEL)),
                  shard_rows, shard_rows, shard_rows, _full((1, D_MODEL)), row(D_MODEL)],
        out_specs=[row(D_MODEL), row(D_MODEL), act, act, act, row(D_MODEL), row(D_MODEL), _full((8, 128)),
                   _full((1, D_MODEL))],
        out_shape=[sds((n_pad, D_MODEL), F32), sds((n_pad, D_MODEL), BF16),
                   sds((N_SHARD, n_pad, FF_SHARD), BF16), sds((N_SHARD, n_pad, FF_SHARD), BF16),
                   sds((N_SHARD, n_pad, FF_SHARD), BF16), sds((n_pad, D_MODEL), F32), sds((n_pad, D_MODEL), BF16),
                   sds((8, 128), F32), sds((1, D_MODEL), F32)],
        scratch_shapes=[pltpu.VMEM((tm, D_MODEL), F32)],
        compiler_params=_params(("arbitrary", "arbitrary")),
    )(h0, ms, mp, w_out_b, g2, wg_b, wu_b, wd_b, gf, target)


def _bwd_ffn(dh2, a, b, wg_b, wu_b, wd_b, h1, g2, tm):
    n_pad = dh2.shape[0]

    def body(dh2_ref, a_ref, b_ref, wg_ref, wu_ref, wd_ref, h1_ref, g2_ref, da_ref, db_ref, dh1_ref, dg2_ref, acc):
        i, q = pl.program_id(0), pl.program_id(1)

        @pl.when((i == 0) & (q == 0))
        def _():
            dg2_ref[...] = jnp.zeros_like(dg2_ref)

        dff = _dot_nt(dh2_ref[...].astype(BF16), wd_ref[0])
        a_v, b_v = a_ref[0].astype(F32), b_ref[0].astype(F32)
        sig = _sigmoid(a_v)
        da = (dff * b_v * sig * (1.0 + a_v * (1.0 - sig))).astype(BF16)
        db = (dff * a_v * sig).astype(BF16)
        da_ref[0] = da
        db_ref[0] = db
        part = _dot(da, wg_ref[0]) + _dot(db, wu_ref[0])

        @pl.when(q == 0)
        def _():
            acc[...] = part

        @pl.when(q > 0)
        def _():
            acc[...] += part

        @pl.when(q == N_SHARD - 1)
        def _():
            h1 = h1_ref[...]
            r = _rms(h1)
            xhat = h1 * r
            dn2 = acc[...]
            dg2_ref[...] += jnp.sum(dn2 * xhat, axis=0, keepdims=True)
            dh1_ref[...] = dh2_ref[...] + _rms_bwd(dn2 * g2_ref[...], xhat, r)

    row = lambda w: pl.BlockSpec((tm, w), lambda i, q: (i, 0))
    shard_rows = pl.BlockSpec((1, FF_SHARD, D_MODEL), lambda i, q: (q, 0, 0))
    act = pl.BlockSpec((1, tm, FF_SHARD), lambda i, q: (q, i, 0))
    sds = jax.ShapeDtypeStruct
    return pl.pallas_call(
        body, grid=(n_pad // tm, N_SHARD), name="bwd_ffn",
        in_specs=[row(D_MODEL), act, act, shard_rows, shard_rows, shard_rows, row(D_MODEL), _full((1, D_MODEL))],
        out_specs=[act, act, row(D_MODEL), _full((1, D_MODEL))],
        out_shape=[sds((N_SHARD, n_pad, FF_SHARD), BF16), sds((N_SHARD, n_pad, FF_SHARD), BF16),
                   sds((n_pad, D_MODEL), F32), sds((1, D_MODEL), F32)],
        scratch_shapes=[pltpu.VMEM((tm, D_MODEL), F32)],
        compiler_params=_params(("arbitrary", "arbitrary")),
    )(dh2, a, b, wg_b, wu_b, wd_b, h1, g2)


def _grad_ffn(n2, da, db, ff, dh2b, tm):
    n_pad = n2.shape[0]

    def body(n2_ref, da_ref, db_ref, ff_ref, dh2_ref, dwg_ref, dwu_ref, dwd_ref):
        i = pl.program_id(1)
        n2_v = n2_ref[...]
        gg = _dot_tn(da_ref[0], n2_v)
        gu = _dot_tn(db_ref[0], n2_v)
        gd = _dot_tn(ff_ref[0], dh2_ref[...])

        @pl.when(i == 0)
        def _():
            dwg_ref[0] = gg
            dwu_ref[0] = gu
            dwd_ref[0] = gd

        @pl.when(i > 0)
        def _():
            dwg_ref[0] += gg
            dwu_ref[0] += gu
            dwd_ref[0] += gd

    row = lambda w: pl.BlockSpec((tm, w), lambda q, i: (i, 0))
    act = pl.BlockSpec((1, tm, FF_SHARD), lambda q, i: (q, i, 0))
    sds = jax.ShapeDtypeStruct
    return pl.pallas_call(
        body, grid=(N_SHARD, n_pad // tm), name="grad_ffn",
        in_specs=[row(D_MODEL), act, act, act, row(D_MODEL)],
        out_specs=[pl.BlockSpec((1, FF_SHARD, D_MODEL), lambda q, i: (q, 0, 0))] * 3,
        out_shape=[sds((N_SHARD, FF_SHARD, D_MODEL), F32)] * 3,
        compiler_params=_params(("parallel", "arbitrary")),
    )(n2, da, db, ff, dh2b)


def _bwd_out(dh1, ms, mp, w_out_b, tm, token):
    n_pad = dh1.shape[0]

    def body(dh1_ref, ms_ref, mp_ref, wo_ref, token_ref, dms_ref, dmp_ref, dwo_ref):
        i = pl.program_id(0)

        @pl.when(i == 0)
        def _():
            dwo_ref[...] = jnp.zeros_like(dwo_ref)

        d = dh1_ref[...].astype(BF16)
        dms = _dot_nt(d, wo_ref[:D_SSM, :])
        for k in range(4):
            dms_ref[k] = dms[:, 128 * k:128 * (k + 1)]
        dmp_ref[...] = _dot_nt(d, wo_ref[D_SSM:, :])
        dwo_ref[:D_SSM, :] += _dot_tn(ms_ref[...], d)
        dwo_ref[D_SSM:, :] += _dot_tn(mp_ref[...], d)

    row = lambda w: pl.BlockSpec((tm, w), lambda i: (i, 0))
    sds = jax.ShapeDtypeStruct
    return pl.pallas_call(
        body, grid=(n_pad // tm,), name="bwd_out",
        in_specs=[row(D_MODEL), row(D_SSM), row(D_POOL), _full((D_MODEL, D_MODEL)), _ANY],
        out_specs=[pl.BlockSpec((4, tm, 128), lambda i: (0, i, 0)), row(D_POOL), _full((D_MODEL, D_MODEL))],
        out_shape=[sds((4, n_pad, 128), F32), sds((n_pad, D_POOL), F32), sds((D_MODEL, D_MODEL), F32)],
        compiler_params=_params(("arbitrary",)),
    )(dh1, ms, mp, w_out_b, token)


def _bwd_in(du, dv, h0, dh1, g1, w_in_b, tm):
    n_pad = h0.shape[0]

    def body(du_ref, dv_ref, h0_ref, dh1_ref, g1_ref, w_ref, dh0_ref, dwi_ref, dg1_ref):
        i = pl.program_id(0)

        @pl.when(i == 0)
        def _():
            dwi_ref[...] = jnp.zeros_like(dwi_ref)
            dg1_ref[...] = jnp.zeros_like(dg1_ref)

        dub = du_ref[...].astype(BF16)
        dvb = dv_ref[...].astype(BF16)
        dn1 = _dot_nt(dub, w_ref[:, :D_SSM]) + _dot_nt(dvb, w_ref[:, D_SSM:])
        h = h0_ref[...]
        r = _rms(h)
        xhat = h * r
        g_row = g1_ref[...]
        n1 = (xhat * g_row).astype(BF16)
        dwi_ref[:, :D_SSM] += _dot_tn(n1, dub)
        dwi_ref[:, D_SSM:] += _dot_tn(n1, dvb)
        dg1_ref[...] += jnp.sum(dn1 * xhat, axis=0, keepdims=True)
        dh0_ref[...] = dh1_ref[...] + _rms_bwd(dn1 * g_row, xhat, r)

    row = lambda w: pl.BlockSpec((tm, w), lambda i: (i, 0))
    sds = jax.ShapeDtypeStruct
    return pl.pallas_call(
        body, grid=(n_pad // tm,), name="bwd_in",
        in_specs=[row(D_SSM), row(D_POOL), row(D_MODEL), row(D_MODEL), _full((1, D_MODEL)), _full((D_MODEL, D_MODEL))],
        out_specs=[row(D_MODEL), _full((D_MODEL, D_MODEL)), _full((1, D_MODEL))],
        out_shape=[sds((n_pad, D_MODEL), F32), sds((D_MODEL, D_MODEL), F32), sds((1, D_MODEL), F32)],
        compiler_params=_params(("arbitrary",)),
    )(du, dv, h0, dh1, g1, w_in_b)


SEGMENTS = 8


def _interleaved(ref, seg):
    return jnp.concatenate(
        [jnp.concatenate([ref[i, pl.ds(j, SEGMENTS, stride=seg), :] for i in range(4)], axis=1) for j in range(seg)],
        axis=0)


def _time_order(scratch, val, seg):
    for i in range(4):
        scratch[i] = val[:, 128 * i:128 * (i + 1)]
    tiles = []
    for m in range(val.shape[0] // 8):
        s, j0 = divmod(8 * m, seg)
        tiles.append(jnp.concatenate(
            [scratch[i, pl.ds(8 * j0 + s, 8, stride=SEGMENTS), :] for i in range(4)], axis=1))
    return jnp.concatenate(tiles, axis=0)


def _power_table(lam_ref, pw_r, pw_i, seg):
    a_r = jnp.broadcast_to(lam_ref[0:1, :], (SEGMENTS, N_STATE))
    a_i = jnp.broadcast_to(lam_ref[1:2, :], (SEGMENTS, N_STATE))
    p_r, p_i = a_r, a_i
    for k in range(seg):
        pw_r[SEGMENTS * k:SEGMENTS * (k + 1), :] = p_r
        pw_i[SEGMENTS * k:SEGMENTS * (k + 1), :] = p_i
        p_r, p_i = p_r * a_r - p_i * a_i, p_r * a_i + p_i * a_r


def _segment_scan(xr_ref, xi_ref, cols, pw_r, pw_i, hr_s, hi_s, seg, reverse):
    sign = -1.0 if reverse else 1.0
    a_r, a_i = pw_r[0:SEGMENTS, cols], sign * pw_i[0:SEGMENTS, cols]

    def step(n, carry):
        hr, hi = carry
        o = pl.multiple_of((seg - 1 - n if reverse else n) * SEGMENTS, SEGMENTS)
        nr = a_r * hr - a_i * hi + xr_ref[pl.ds(o, SEGMENTS), cols]
        ni = a_r * hi + a_i * hr + xi_ref[pl.ds(o, SEGMENTS), cols]
        xr_ref[pl.ds(o, SEGMENTS), cols] = nr
        xi_ref[pl.ds(o, SEGMENTS), cols] = ni
        return nr, ni

    zero = jnp.zeros((SEGMENTS, cols.stop - cols.start), F32)
    e_r, e_i = lax.fori_loop(0, seg, step, (zero, zero), unroll=2)

    top = SEGMENTS * (seg - 1)
    ls_r, ls_i = pw_r[top:top + 1, cols], sign * pw_i[top:top + 1, cols]
    c_r, c_i = hr_s[0:1, cols], hi_s[0:1, cols]
    in_r, in_i = [None] * SEGMENTS, [None] * SEGMENTS
    for s in (range(SEGMENTS - 1, -1, -1) if reverse else range(SEGMENTS)):
        in_r[s], in_i[s] = c_r, c_i
        c_r, c_i = (e_r[s:s + 1, :] + ls_r * c_r - ls_i * c_i, e_i[s:s + 1, :] + ls_r * c_i + ls_i * c_r)
    hr_s[0:1, cols] = c_r
    hi_s[0:1, cols] = c_i
    cm_r, cm_i = jnp.concatenate(in_r, axis=0), jnp.concatenate(in_i, axis=0)

    def fix(jj, _):
        o = pl.multiple_of(jj * SEGMENTS, SEGMENTS)
        k = pl.multiple_of((seg - 1 - jj if reverse else jj) * SEGMENTS, SEGMENTS)
        p_r, p_i = pw_r[pl.ds(k, SEGMENTS), cols], sign * pw_i[pl.ds(k, SEGMENTS), cols]
        xr_ref[pl.ds(o, SEGMENTS), cols] += p_r * cm_r - p_i * cm_i
        xi_ref[pl.ds(o, SEGMENTS), cols] += p_r * cm_i + p_i * cm_r
        return 0

    lax.fori_loop(0, seg, fix, 0, unroll=2)


def _s5_tail(y, glu_ref, glub):
    g, dgelu = _gelu_and_grad(y)
    gb = g.astype(BF16)
    gate = jnp.concatenate([_dot(gb[:, 128 * j:128 * (j + 1)], glu_ref[j]) for j in range(SUPER)], axis=1) + glub
    sig = _sigmoid(gate)
    return g, gb, dgelu, sig, g * sig


def _s5_fwd(u4, lam, bbr, bbi, crt, cit, vecs, glu, tc):
    n_pad = u4.shape[1]
    seg = tc // SEGMENTS

    def body(u_ref, lam_ref, bbr_ref, bbi_ref, crt_ref, cit_ref, vec_ref, glu_ref,
             sr_ref, si_ref, y_ref, ms_ref, hr_s, hi_s, pw_r, pw_i, lanes):
        @pl.when(pl.program_id(0) == 0)
        def _():
            hr_s[...] = jnp.zeros_like(hr_s)
            hi_s[...] = jnp.zeros_like(hi_s)
            _power_table(lam_ref, pw_r, pw_i, seg)

        u_v = _interleaved(u_ref, seg)
        ub = u_v.astype(BF16)
        for j in range(SUPER):
            uj = ub[:, 128 * j:128 * (j + 1)]
            sr_ref[:, 512 * j:512 * (j + 1)] = _dot(uj, bbr_ref[j])
            si_ref[:, 512 * j:512 * (j + 1)] = _dot(uj, bbi_ref[j])
        for j in range(SUPER):
            _segment_scan(sr_ref, si_ref, slice(512 * j, 512 * (j + 1)), pw_r, pw_i, hr_s, hi_s, seg, False)

        d_row, glub, gs = vec_ref[0:1, :], vec_ref[1:2, :], vec_ref[2:3, :]
        ys_c = []
        for j in range(SUPER):
            sr_j = sr_ref[:, 512 * j:512 * (j + 1)].astype(BF16)
            si_j = si_ref[:, 512 * j:512 * (j + 1)].astype(BF16)
            ys_c.append(_dot(sr_j, crt_ref[j]) - _dot(si_j, cit_ref[j]))
        y = jnp.concatenate(ys_c, axis=1) + d_row * u_v
        y_ref[...] = y
        _, _, _, _, ys = _s5_tail(y, glu_ref, glub)
        ms_ref[...] = _time_order(lanes, ys * _rms(ys) * gs, seg).astype(BF16)

    chunk = lambda w: pl.BlockSpec((tc, w), lambda c: (c, 0))
    lane_blocks = pl.BlockSpec((4, tc, 128), lambda c: (0, c, 0))
    sds = jax.ShapeDtypeStruct
    return pl.pallas_call(
        body, grid=(n_pad // tc,), name="s5_fwd",
        in_specs=[lane_blocks, _full((8, N_STATE)), _full((SUPER, 128, 512)), _full((SUPER, 128, 512)),
                  _full((SUPER, 512, 128)), _full((SUPER, 512, 128)), _full((8, D_SSM)), _full((SUPER, 128, 128))],
        out_specs=[chunk(N_STATE), chunk(N_STATE), chunk(D_SSM), chunk(D_SSM)],
        out_shape=[sds((n_pad, N_STATE), F32), sds((n_pad, N_STATE), F32),
                   sds((n_pad, D_SSM), F32), sds((n_pad, D_SSM), BF16)],
        scratch_shapes=[pltpu.VMEM((8, N_STATE), F32), pltpu.VMEM((8, N_STATE), F32),
                        pltpu.VMEM((tc, N_STATE), F32), pltpu.VMEM((tc, N_STATE), F32),
                        pltpu.VMEM((4, tc, 128), F32)],
        compiler_params=_params(("arbitrary",)),
    )(u4, lam, bbr, bbi, crt, cit, vecs, glu)


def _s5_bwd(dms4, y, u4, sr, si, lam, bbr, bbi, crt, cit, vecs, glu, tc, token):
    n_pad = u4.shape[1]
    nc = n_pad // tc
    seg = tc // SEGMENTS

    def body(dms_ref, y_ref, u_ref, sr_ref, si_ref, pr_ref, pi_ref, lam_ref, bbr_ref, bbi_ref, crt_ref, cit_ref,
             vec_ref, glu_ref, token_ref, du_ref, dbbr_ref, dbbi_ref, dcrt_ref, dcit_ref, dglu_ref, dvec_ref, dlam_ref,
             qr_s, qi_s, cr_s, ci_s, pw_r, pw_i, lanes):
        c = pl.program_id(0)

        @pl.when(c == 0)
        def _():
            for ref in (dbbr_ref, dbbi_ref, dcrt_ref, dcit_ref, dglu_ref, dvec_ref, dlam_ref, cr_s, ci_s):
                ref[...] = jnp.zeros_like(ref)
            _power_table(lam_ref, pw_r, pw_i, seg)

        d_row, glub, gs = vec_ref[0:1, :], vec_ref[1:2, :], vec_ref[2:3, :]
        y_v, u_v = y_ref[...], _interleaved(u_ref, seg)
        ub = u_v.astype(BF16)
        g, gb, dgelu, sig, ys = _s5_tail(y_v, glu_ref, glub)
        r = _rms(ys)
        xhat = ys * r
        dm = _interleaved(dms_ref, seg)
        dys = _rms_bwd(dm * gs, xhat, r)
        dgate = dys * g * sig * (1.0 - sig)
        dgateb = dgate.astype(BF16)
        dg = dys * sig + jnp.concatenate(
            [_dot_nt(dgateb[:, 128 * j:128 * (j + 1)], glu_ref[j]) for j in range(SUPER)], axis=1)
        dy = dg * dgelu
        dyb = dy.astype(BF16)
        dvec_ref[0:1, :] += jnp.sum(dy * u_v, axis=0, keepdims=True)
        dvec_ref[1:2, :] += jnp.sum(dgate, axis=0, keepdims=True)
        dvec_ref[2:3, :] += jnp.sum(dm * xhat, axis=0, keepdims=True)

        for j in range(SUPER):
            cols, states = slice(128 * j, 128 * (j + 1)), slice(512 * j, 512 * (j + 1))
            dglu_ref[j] += _dot_tn(gb[:, cols], dgateb[:, cols])
            dcrt_ref[j] += _dot_tn(sr_ref[:, states].astype(BF16), dyb[:, cols])
            dcit_ref[j] -= _dot_tn(si_ref[:, states].astype(BF16), dyb[:, cols])
            qr_s[:, states] = _dot_nt(dyb[:, cols], crt_ref[j])
            qi_s[:, states] = -_dot_nt(dyb[:, cols], cit_ref[j])

        first = c == nc - 1
        row0 = lax.broadcasted_iota(jnp.int32, (SEGMENTS, 1), 0) == 0
        last = (seg - 1) * SEGMENTS
        for j in range(SUPER):
            states = slice(512 * j, 512 * (j + 1))
            _segment_scan(qr_s, qi_s, states, pw_r, pw_i, cr_s, ci_s, seg, True)

            before_r = jnp.where(first, 0.0, pltpu.roll(pr_ref[:, states], 1, 0))
            before_i = jnp.where(first, 0.0, pltpu.roll(pi_ref[:, states], 1, 0))
            hp_r = jnp.where(row0, before_r, pltpu.roll(sr_ref[pl.ds(last, SEGMENTS), states], 1, 0))
            hp_i = jnp.where(row0, before_i, pltpu.roll(si_ref[pl.ds(last, SEGMENTS), states], 1, 0))
            q_r, q_i = qr_s[pl.ds(0, SEGMENTS), states], qi_s[pl.ds(0, SEGMENTS), states]

            def dlam_step(jj, acc):
                o = pl.multiple_of(jj * SEGMENTS, SEGMENTS)
                above = pl.multiple_of((jj - 1) * SEGMENTS, SEGMENTS)
                h_r, h_i = sr_ref[pl.ds(above, SEGMENTS), states], si_ref[pl.ds(above, SEGMENTS), states]
                t_r, t_i = qr_s[pl.ds(o, SEGMENTS), states], qi_s[pl.ds(o, SEGMENTS), states]
                return acc[0] + t_r * h_r + t_i * h_i, acc[1] + t_i * h_r - t_r * h_i

            acc = lax.fori_loop(1, seg, dlam_step, (q_r * hp_r + q_i * hp_i, q_i * hp_r - q_r * hp_i), unroll=2)
            dlam_ref[0:SEGMENTS, states] += acc[0]
            dlam_ref[SEGMENTS:, states] += acc[1]

        du_c = []
        for j in range(SUPER):
            cols, states = slice(128 * j, 128 * (j + 1)), slice(512 * j, 512 * (j + 1))
            qr_j = qr_s[:, states].astype(BF16)
            qi_j = qi_s[:, states].astype(BF16)
            du_c.append(_dot_nt(qr_j, bbr_ref[j]) + _dot_nt(qi_j, bbi_ref[j]))
            dbbr_ref[j] += _dot_tn(ub[:, cols], qr_j)
            dbbi_ref[j] += _dot_tn(ub[:, cols], qi_j)
        du_ref[...] = _time_order(lanes, jnp.concatenate(du_c, axis=1) + dy * d_row, seg)

    rev = lambda c: nc - 1 - c
    chunk = lambda w: pl.BlockSpec((tc, w), lambda c: (rev(c), 0))
    lane_blocks = pl.BlockSpec((4, tc, 128), lambda c: (0, rev(c), 0))
    prev = pl.BlockSpec((SEGMENTS, N_STATE), lambda c: (jnp.maximum(rev(c) * seg - 1, 0), 0))
    sds = jax.ShapeDtypeStruct
    return pl.pallas_call(
        body, grid=(nc,), name="s5_bwd",
        in_specs=[lane_blocks, chunk(D_SSM), lane_blocks, chunk(N_STATE), chunk(N_STATE), prev, prev,
                  _full((8, N_STATE)), _full((SUPER, 128, 512)), _full((SUPER, 128, 512)),
                  _full((SUPER, 512, 128)), _full((SUPER, 512, 128)), _full((8, D_SSM)), _full((SUPER, 128, 128)), _ANY],
        out_specs=[chunk(D_SSM), _full((SUPER, 128, 512)), _full((SUPER, 128, 512)), _full((SUPER, 512, 128)),
                   _full((SUPER, 512, 128)), _full((SUPER, 128, 128)), _full((8, D_SSM)), _full((2 * SEGMENTS, N_STATE))],
        out_shape=[sds((n_pad, D_SSM), F32), sds((SUPER, 128, 512), F32), sds((SUPER, 128, 512), F32),
                   sds((SUPER, 512, 128), F32), sds((SUPER, 512, 128), F32), sds((SUPER, 128, 128), F32),
                   sds((8, D_SSM), F32), sds((2 * SEGMENTS, N_STATE), F32)],
        scratch_shapes=[pltpu.VMEM((tc, N_STATE), F32), pltpu.VMEM((tc, N_STATE), F32),
                        pltpu.VMEM((8, N_STATE), F32), pltpu.VMEM((8, N_STATE), F32),
                        pltpu.VMEM((tc, N_STATE), F32), pltpu.VMEM((tc, N_STATE), F32),
                        pltpu.VMEM((4, tc, 128), F32)],
        compiler_params=_params(("arbitrary",)),
    )(dms4, y, u4, sr, si, sr, si, lam, bbr, bbi, crt, cit, vecs, glu, token)


def _inv_count(c_idx, tc, w):
    t = c_idx * tc + lax.broadcasted_iota(jnp.int32, (tc, 1), 0)
    return 1.0 / jnp.minimum(t + 1, w).astype(F32)


def _pool_fwd(v, pw_b, vecs, tc):
    n_pad = v.shape[0]

    def body(v_ref, pw_ref, vec_ref, feat_ref, mp_ref, hist):
        c = pl.program_id(0)

        @pl.when(c == 0)
        def _():
            hist[...] = jnp.zeros_like(hist)

        v_v = v_ref[...]
        ext = jnp.concatenate([hist[...], v_v], axis=0)
        hist[...] = v_v[tc - POOL_HALO:, :]
        feats, ps = [], []
        for k, w in enumerate(POOL_WINDOWS):
            cols = slice(128 * k, 128 * (k + 1))
            s = ext[:, cols]
            sh = 1
            while sh < w:
                s = s + pltpu.roll(s, sh, 0)
                sh *= 2
            f = (s[POOL_HALO:, :] * _inv_count(c, tc, w) - v_v[:, cols]).astype(BF16)
            feats.append(f)
            ps.append(_dot(f, pw_ref[k]))
        feat_ref[...] = jnp.concatenate(feats, axis=1)
        yp = jnp.concatenate(ps, axis=1) * vec_ref[0:1, :]
        mp_ref[...] = (yp * _rms(yp) * vec_ref[1:2, :]).astype(BF16)

    chunk = lambda w: pl.BlockSpec((tc, w), lambda c: (c, 0))
    sds = jax.ShapeDtypeStruct
    return pl.pallas_call(
        body, grid=(n_pad // tc,), name="pool_fwd",
        in_specs=[chunk(D_POOL), _full((4, 128, 128)), _full((8, D_POOL))],
        out_specs=[chunk(D_POOL), chunk(D_POOL)],
        out_shape=[sds((n_pad, D_POOL), BF16), sds((n_pad, D_POOL), BF16)],
        scratch_shapes=[pltpu.VMEM((POOL_HALO, D_POOL), F32)],
        compiler_params=_params(("arbitrary",)),
    )(v, pw_b, vecs)


def _pool_bwd(dmp, feat, pw_b, vecs, tc):
    n_pad = dmp.shape[0]
    nc = n_pad // tc

    def body(dmp_ref, feat_ref, pw_ref, vec_ref, dv_ref, dpw_ref, dvec_ref, fut):
        c = pl.program_id(0)

        @pl.when(c == 0)
        def _():
            fut[...] = jnp.zeros_like(fut)
            dpw_ref[...] = jnp.zeros_like(dpw_ref)
            dvec_ref[...] = jnp.zeros_like(dvec_ref)

        scale, gp = vec_ref[0:1, :], vec_ref[1:2, :]
        feat_v = feat_ref[...]
        p = jnp.concatenate([_dot(feat_v[:, 128 * k:128 * (k + 1)], pw_ref[k]) for k in range(4)], axis=1)
        yp = p * scale
        r = _rms(yp)
        xhat = yp * r
        dm = dmp_ref[...]
        dyp = _rms_bwd(dm * gp, xhat, r)
        dvec_ref[0:1, :] += jnp.sum(dyp * p, axis=0, keepdims=True)
        dvec_ref[1:2, :] += jnp.sum(dm * xhat, axis=0, keepdims=True)
        dpb = (dyp * scale).astype(BF16)
        es, dfs = [], []
        for k, w in enumerate(POOL_WINDOWS):
            cols = slice(128 * k, 128 * (k + 1))
            dpw_ref[k] += _dot_tn(feat_v[:, cols], dpb[:, cols])
            df = _dot_nt(dpb[:, cols], pw_ref[k])
            dfs.append(df)
            es.append(df * _inv_count(nc - 1 - c, tc, w))
        e = jnp.concatenate(es, axis=1)
        ext = jnp.concatenate([e, fut[...]], axis=0)
        fut[...] = e[:POOL_HALO, :]
        n_ext = tc + POOL_HALO
        dvs = []
        for k, w in enumerate(POOL_WINDOWS):
            s = ext[:, 128 * k:128 * (k + 1)]
            sh = 1
            while sh < w:
                s = s + pltpu.roll(s, n_ext - sh, 0)
                sh *= 2
            dvs.append(s[:tc, :] - dfs[k])
        dv_ref[...] = jnp.concatenate(dvs, axis=1)

    chunk = lambda w: pl.BlockSpec((tc, w), lambda c: (nc - 1 - c, 0))
    sds = jax.ShapeDtypeStruct
    return pl.pallas_call(
        body, grid=(nc,), name="pool_bwd",
        in_specs=[chunk(D_POOL), chunk(D_POOL), _full((4, 128, 128)), _full((8, D_POOL))],
        out_specs=[chunk(D_POOL), _full((4, 128, 128)), _full((8, D_POOL))],
        out_shape=[sds((n_pad, D_POOL), F32), sds((4, 128, 128), F32), sds((8, D_POOL), F32)],
        scratch_shapes=[pltpu.VMEM((POOL_HALO, D_POOL), F32)],
        compiler_params=_params(("arbitrary",)),
    )(dmp, feat, pw_b, vecs)


def _place():
    x, y, c = lax.axis_index("x"), lax.axis_index("y"), lax.axis_index("c")
    chips = [(1 - x, y), (x, 1 - y), (1 - x, 1 - y)]
    return x, y, c, chips


_ANY = pl.BlockSpec(memory_space=pl.ANY)


def _cast_shards(shards, dtypes, place):
    n = len(shards)

    def body(place_ref, *refs):
        for i in range(n):
            refs[n + i][0] = refs[i][...].astype(dtypes[i])

    return pl.pallas_call(
        body, name="cast_shards",
        grid_spec=pltpu.PrefetchScalarGridSpec(
            num_scalar_prefetch=1, grid=(1,),
            in_specs=[pl.BlockSpec(s.shape, lambda i, p: (0, 0, 0)) for s in shards],
            out_specs=[pl.BlockSpec((1,) + s.shape, lambda i, p: (p[0], 0, 0, 0)) for s in shards]),
        out_shape=[jax.ShapeDtypeStruct((N_SHARD,) + s.shape, dt) for s, dt in zip(shards, dtypes)],
        compiler_params=_params(("arbitrary",)),
    )(place, *shards)


def _gather_shards(full):
    n = len(full)

    def body(*refs):
        outs = refs[n:2 * n]
        ici_send, ici_recv, d2d_send, d2d_recv = refs[2 * n:]
        x, y, c, chips = _place()
        q = 2 * x + y
        sibling = (x, y, 1 - c)

        def ici(i, j, shard, to):
            return pltpu.make_async_remote_copy(src_ref=outs[i].at[q, c], dst_ref=outs[i].at[shard, c],
                                                send_sem=ici_send.at[i, j], recv_sem=ici_recv.at[i, j],
                                                device_id=to, device_id_type=MESH)

        def d2d(i, j, shard, half):
            return pltpu.make_async_remote_copy(src_ref=outs[i].at[shard, c], dst_ref=outs[i].at[shard, half],
                                                send_sem=d2d_send.at[i, j], recv_sem=d2d_recv.at[i, j],
                                                device_id=sibling, device_id_type=MESH)

        sends = [ici(i, j, q, (*chip, c)) for i in range(n) for j, chip in enumerate(chips)]
        for cp in sends:
            cp.start()
        passed = []
        for i in range(n):
            for j, (cx, cy) in enumerate(chips):
                ici(i, j, 2 * cx + cy, (cx, cy, c)).wait_recv()
                cp = d2d(i, j, 2 * cx + cy, c)
                cp.start()
                passed.append(cp)
        for i in range(n):
            for j, (cx, cy) in enumerate(chips):
                d2d(i, j, 2 * cx + cy, 1 - c).wait_recv()
        for cp in sends + passed:
            cp.wait_send()

    return pl.pallas_call(
        body, name="gather_shards",
        in_specs=[_ANY] * n, out_specs=[_ANY] * n,
        out_shape=[jax.ShapeDtypeStruct(f.shape, f.dtype) for f in full],
        input_output_aliases={i: i for i in range(n)},
        scratch_shapes=[pltpu.SemaphoreType.DMA((n, 3)), pltpu.SemaphoreType.DMA((n, 3)),
                        pltpu.SemaphoreType.DMA((n, 3)), pltpu.SemaphoreType.DMA((n, 3))],
    )(*full)


def _forward_halves(full):
    n = len(full)

    def body(*refs):
        outs = refs[n:2 * n]
        send, recv = refs[2 * n:]
        x, y, c, chips = _place()

        def d2d(i, j, shard, half):
            return pltpu.make_async_remote_copy(src_ref=outs[i].at[shard, c], dst_ref=outs[i].at[shard, half],
                                                send_sem=send.at[i, j], recv_sem=recv.at[i, j],
                                                device_id=(x, y, 1 - c), device_id_type=MESH)

        cps = [d2d(i, j, 2 * cx + cy, c) for i in range(n) for j, (cx, cy) in enumerate(chips)]
        for cp in cps:
            cp.start()
        for i in range(n):
            for j, (cx, cy) in enumerate(chips):
                d2d(i, j, 2 * cx + cy, 1 - c).wait_recv()
        for cp in cps:
            cp.wait_send()

    return pl.pallas_call(
        body, name="forward_halves",
        in_specs=[_ANY] * n, out_specs=[_ANY] * n,
        out_shape=[jax.ShapeDtypeStruct(f.shape, f.dtype) for f in full],
        input_output_aliases={i: i for i in range(n)},
        scratch_shapes=[pltpu.SemaphoreType.DMA((n, 3)), pltpu.SemaphoreType.DMA((n, 3))],
    )(*full)


_HBM = pl.BlockSpec(memory_space=pltpu.HBM)
_SEM = pl.BlockSpec(memory_space=pltpu.SEMAPHORE)
_EFFECT = pltpu.SideEffectType.DATAFLOW_SIDE_EFFECTING


def _copies_start(name, arrays, sem_shape, build, after=None):
    n = len(arrays)
    extra = [] if after is None else [after]

    def body(*refs):
        outs = refs[n + len(extra):2 * n + len(extra)]
        send, recv, token = refs[2 * n + len(extra):]
        sends, _ = build(outs, send, recv)
        for cp in sends:
            cp.start()
        token[...] = jnp.zeros_like(token)

    out = pl.pallas_call(
        body, name=name, in_specs=[_HBM] * n + [_ANY] * len(extra),
        out_specs=[_HBM] * n + [_SEM, _SEM, pl.BlockSpec(memory_space=pltpu.VMEM)],
        out_shape=[pltpu.HBM(a.shape, a.dtype) for a in arrays]
        + [pltpu.SemaphoreType.DMA(sem_shape), pltpu.SemaphoreType.DMA(sem_shape), jax.ShapeDtypeStruct((8, 128), F32)],
        input_output_aliases={i: i for i in range(n)},
        compiler_params=pltpu.CompilerParams(has_side_effects=_EFFECT),
    )(*[pltpu.with_memory_space_constraint(a, pltpu.HBM) for a in arrays], *extra)
    return list(out[:n]), (out[n], out[n + 1]), out[n + 2]


def _copies_wait(name, arrays, sems, after, build):
    n = len(arrays)

    def body(*refs):
        ins = refs[:n]
        send, recv = refs[n], refs[n + 1]
        sends, recvs = build(ins, send, recv)
        for cp in sends:
            cp.wait_send()
        for cp in recvs:
            cp.wait_recv()

    return list(pl.pallas_call(
        body, name=name, in_specs=[_HBM] * n + [_SEM, _SEM] + [_ANY] * len(after), out_specs=[_HBM] * n,
        out_shape=[pltpu.HBM(a.shape, a.dtype) for a in arrays],
        input_output_aliases={i: i for i in range(n)},
        compiler_params=pltpu.CompilerParams(has_side_effects=_EFFECT),
    )(*arrays, *sems, *after))


def _remote(src, dst, send_sem, recv_sem, to):
    return pltpu.make_async_remote_copy(src_ref=src, dst_ref=dst, send_sem=send_sem, recv_sem=recv_sem,
                                        device_id=to, device_id_type=MESH)


def _build_gather(refs, send, recv):
    x, y, c, chips = _place()
    q = 2 * x + y
    pairs = [(i, j, chip) for i in range(len(refs)) for j, chip in enumerate(chips)]
    sends = [_remote(refs[i].at[q, c], refs[i].at[q, c], send.at[3 * i + j], recv.at[3 * i + j], (cx, cy, c))
             for i, j, (cx, cy) in pairs]
    recvs = [_remote(refs[i].at[q, c], refs[i].at[2 * cx + cy, c], send.at[3 * i + j], recv.at[3 * i + j], (cx, cy, c))
             for i, j, (cx, cy) in pairs]
    return sends, recvs


def _build_swap(refs, send, recv):
    x, y, c, _ = _place()
    n = len(refs) // 2
    cps = [_remote(refs[i].at[:, 1 - c], refs[n + i], send.at[i], recv.at[i], (x, y, 1 - c)) for i in range(n)]
    return cps, cps


def _build_exchange(refs, send, recv):
    x, y, c, chips = _place()
    n = len(refs) // 2
    cps = [_remote(refs[i].at[2 * cx + cy], refs[n + i].at[j], send.at[3 * i + j], recv.at[3 * i + j], (cx, cy, c))
           for i in range(n) for j, (cx, cy) in enumerate(chips)]
    return cps, cps


def _build_spread(refs, send, recv):
    x, y, c, _ = _place()
    flip = lambda bit, on: bit + on - 2 * bit * on
    cps = [_remote(refs[0], refs[1].at[r - 1], send.at[r - 1], recv.at[r - 1],
                   (flip(x, r >> 2 & 1), flip(y, r >> 1 & 1), flip(c, r & 1))) for r in (1, 2, 4, 6)]
    return cps, cps


def _forward_small(landed):
    def body(in_ref, out_ref, send, recv):
        x, y, c, _ = _place()
        cps = [_remote(out_ref.at[r - 1], out_ref.at[r], send.at[k], recv.at[k], (x, y, 1 - c))
               for k, r in enumerate((2, 4, 6))]
        for cp in cps:
            cp.start()
        for cp in cps:
            cp.wait()

    return pl.pallas_call(
        body, name="forward_small",
        in_specs=[_ANY], out_specs=_ANY, out_shape=jax.ShapeDtypeStruct(landed.shape, F32),
        input_output_aliases={0: 0},
        scratch_shapes=[pltpu.SemaphoreType.DMA((3,)), pltpu.SemaphoreType.DMA((3,))],
    )(landed)


def _swap_halves(grads):
    n = len(grads)

    def body(*refs):
        ins, outs = refs[:n], refs[n:2 * n]
        send, recv = refs[2 * n:]
        x, y, c, _ = _place()
        cps = [pltpu.make_async_remote_copy(src_ref=ins[i].at[:, 1 - c], dst_ref=outs[i], send_sem=send.at[i],
                                            recv_sem=recv.at[i], device_id=(x, y, 1 - c), device_id_type=MESH)
               for i in range(n)]
        for cp in cps:
            cp.start()
        for cp in cps:
            cp.wait()

    return pl.pallas_call(
        body, name="swap_halves",
        in_specs=[_ANY] * n, out_specs=[_ANY] * n,
        out_shape=[jax.ShapeDtypeStruct((N_SHARD,) + g.shape[2:], F32) for g in grads],
        scratch_shapes=[pltpu.SemaphoreType.DMA((n,)), pltpu.SemaphoreType.DMA((n,))],
    )(*grads)


def _join_halves(pairs):
    n = len(pairs)

    def body(*refs):
        outs = refs[n:2 * n]
        send, recv = refs[2 * n:]
        x, y, c, _ = _place()
        cps = [pltpu.make_async_remote_copy(src_ref=outs[i].at[c], dst_ref=outs[i].at[c], send_sem=send.at[i],
                                            recv_sem=recv.at[i], device_id=(x, y, 1 - c), device_id_type=MESH)
               for i in range(n)]
        for cp in cps:
            cp.start()
        for i in range(n):
            cps[i].wait_send()
            pltpu.make_async_remote_copy(src_ref=outs[i].at[c], dst_ref=outs[i].at[1 - c], send_sem=send.at[i],
                                         recv_sem=recv.at[i], device_id=(x, y, 1 - c), device_id_type=MESH).wait_recv()

    return pl.pallas_call(
        body, name="join_halves",
        in_specs=[_ANY] * n, out_specs=[_ANY] * n,
        out_shape=[jax.ShapeDtypeStruct(p.shape, F32) for p in pairs],
        input_output_aliases={i: i for i in range(n)},
        scratch_shapes=[pltpu.SemaphoreType.DMA((n,)), pltpu.SemaphoreType.DMA((n,))],
    )(*pairs)


N_SPLIT = 2


def _sum_siblings(tag, grads, recvd, place):
    n = len(grads)

    def body(place_ref, *refs):
        g_refs, r_refs, sb_refs, own_refs = (refs[k * n:(k + 1) * n] for k in range(4))
        s = pl.program_id(1)
        for i in range(n):
            tot = g_refs[i][0, 0] + r_refs[i][0]
            sb_refs[i][0] = tot.astype(BF16)

            @pl.when(s == place_ref[0])
            def _():
                own_refs[i][...] = tot

    in_specs, sb_specs, own_specs, sb_shapes, own_shapes = [], [], [], [], []
    for g in grads:
        _, _, r, cdim = g.shape
        rb = r // N_SPLIT
        in_specs.append(pl.BlockSpec((1, 1, rb, cdim), lambda b, s, p: (s, p[1], b, 0)))
        sb_specs.append(pl.BlockSpec((1, rb, cdim), lambda b, s, p: (s, b, 0)))
        own_specs.append(pl.BlockSpec((rb, cdim), lambda b, s, p: (b, 0)))
        sb_shapes.append(jax.ShapeDtypeStruct((N_SHARD, r, cdim), BF16))
        own_shapes.append(jax.ShapeDtypeStruct((r, cdim), F32))
    out = pl.pallas_call(
        body, name="sum_siblings_" + tag,
        grid_spec=pltpu.PrefetchScalarGridSpec(
            num_scalar_prefetch=1, grid=(N_SPLIT, N_SHARD),
            in_specs=in_specs + sb_specs, out_specs=sb_specs + own_specs),
        out_shape=sb_shapes + own_shapes,
        compiler_params=_params(("parallel", "arbitrary")),
    )(place, *grads, *recvd)
    return out[:n], out[n:]


def _sum_chips(own, recvd, place):
    n = len(own)

    def body(place_ref, *refs):
        o_refs, r_refs, out_refs = (refs[k * n:(k + 1) * n] for k in range(3))
        for i in range(n):
            tot = o_refs[i][...]
            for j in range(3):
                tot = tot + r_refs[i][j].astype(F32)
            out_refs[i][0] = tot

    o_specs, r_specs, out_specs = [], [], []
    for o in own:
        r, cdim = o.shape
        rb = r // N_SPLIT
        o_specs.append(pl.BlockSpec((rb, cdim), lambda b, p: (b, 0)))
        r_specs.append(pl.BlockSpec((3, rb, cdim), lambda b, p: (0, b, 0)))
        out_specs.append(pl.BlockSpec((1, rb, cdim), lambda b, p: (p[1], b, 0)))
    return pl.pallas_call(
        body, name="sum_chips",
        grid_spec=pltpu.PrefetchScalarGridSpec(num_scalar_prefetch=1, grid=(N_SPLIT,),
                                               in_specs=o_specs + r_specs, out_specs=out_specs),
        out_shape=[jax.ShapeDtypeStruct((2,) + o.shape, F32) for o in own],
        compiler_params=_params(("parallel",)),
    )(place, *own, *recvd)


def _adamw_math(w, g, m, v):
    m = ADAM_B1 * m + (1.0 - ADAM_B1) * g
    v = ADAM_B2 * v + (1.0 - ADAM_B2) * (g * g)
    m_hat = m / (1.0 - ADAM_B1 ** ADAM_STEP)
    v_hat = v / (1.0 - ADAM_B2 ** ADAM_STEP)
    delta = -ADAM_LR * (m_hat / (jnp.sqrt(v_hat) + ADAM_EPS) + ADAM_WD * w)
    return delta, m, v


def _adamw(name, ws, gs, ms, vs, n_split):
    n = len(ws)

    def body(*refs):
        w_r, g_r, m_r, v_r, d_o, m_o, v_o = (refs[k * n:(k + 1) * n] for k in range(7))
        for i in range(n):
            d, m, v = _adamw_math(w_r[i][...], g_r[i][...], m_r[i][...], v_r[i][...])
            d_o[i][...] = d
            m_o[i][...] = m
            v_o[i][...] = v

    specs = [pl.BlockSpec((w.shape[0] // n_split, w.shape[1]), lambda b: (b, 0)) for w in ws]
    shapes = [jax.ShapeDtypeStruct(w.shape, F32) for w in ws]
    out = pl.pallas_call(
        body, name=name, grid=(n_split,),
        in_specs=specs * 4, out_specs=specs * 3, out_shape=shapes * 3,
        compiler_params=_params(("parallel",)),
    )(*ws, *gs, *ms, *vs)
    return out[:n], out[n:2 * n], out[2 * n:]


def _reduce_small(own, received):
    def body(own_ref, recv_ref, g_out):
        me = 4 * lax.axis_index("x") + 2 * lax.axis_index("y") + lax.axis_index("c")
        g = None
        for k in range(8):
            mine = me == k
            part = jnp.where(mine, own_ref[...], recv_ref[jnp.where(mine, 0, jnp.bitwise_xor(me, k) - 1)])
            g = part if g is None else g + part
        g_out[...] = g

    return pl.pallas_call(
        body, name="reduce_small",
        out_shape=jax.ShapeDtypeStruct(own.shape, F32),
        compiler_params=_params(),
    )(own, received)


def _adamw_small(ws, gs, ms, vs):
    n = len(ws)

    def body(*refs):
        w_r, g_r, m_r, v_r, d_o, m_o, v_o = (refs[k * n:(k + 1) * n] for k in range(7))
        for i in range(n):
            d, mm, vv = _adamw_math(w_r[i][...], g_r[i][...], m_r[i][...], v_r[i][...])
            d_o[i][...] = d
            m_o[i][...] = mm
            v_o[i][...] = vv

    out = pl.pallas_call(
        body, name="adamw_small",
        out_shape=[jax.ShapeDtypeStruct(t.shape, F32) for t in ws] * 3,
        compiler_params=_params(),
    )(*ws, *gs, *ms, *vs)
    return out[:n], out[n:2 * n], out[2 * n:]


def _s5_operands(lam_re, lam_im, log_step, b_re, b_im, c_re, c_im, glu_w):
    lr = jnp.minimum(lam_re, -1e-4)
    li = lam_im
    step = jnp.exp(log_step)[:, None]
    mag = jnp.exp(lr * step)
    ang = li * step
    abr = mag * jnp.cos(ang)
    abi = mag * jnp.sin(ang)
    nr = abr - 1.0
    ni = abi
    den = lr * lr + li * li
    cr = ((nr * lr + ni * li) / den)[..., None]
    ci = ((ni * lr - nr * li) / den)[..., None]
    bbr = cr * b_re - ci * b_im
    bbi = cr * b_im + ci * b_re
    eye = jnp.eye(8, dtype=F32)
    g, h, p = SSM_GROUPS // SUPER, SSM_GROUP, SSM_STATE

    def b_layout(t):
        return jnp.einsum("ab,japh->jahbp", eye, t.reshape(SUPER, g, p, h)).reshape(SUPER, g * h, g * p)

    def c_layout(t):
        return jnp.einsum("ab,jahp->jbpah", eye, t.reshape(SUPER, g, h, p)).reshape(SUPER, g * p, g * h)

    glu = jnp.einsum("ab,jahk->jahbk", eye, glu_w.reshape(SUPER, g, h, h)).reshape(SUPER, g * h, g * h)
    lam = _pad_rows(jnp.concatenate([abr.reshape(1, N_STATE), abi.reshape(1, N_STATE)], axis=0), 8)
    return lam, b_layout(bbr), b_layout(bbi), c_layout(c_re), c_layout(c_im), glu


def _pad_rows(a, rows):
    return jnp.pad(a, ((0, rows - a.shape[0]), (0, 0)))


def _pack(parts):
    rows = []
    for a in parts:
        flat = a.reshape(-1)
        n = -(-flat.shape[0] // 128)
        rows.append(jnp.pad(flat, (0, n * 128 - flat.shape[0])).reshape(n, 128))
    out = jnp.concatenate(rows, axis=0)
    return _pad_rows(out, -(-out.shape[0] // 8) * 8)


def _unpack(packed, like):
    out, at = [], 0
    for a in like:
        n = -(-a.size // 128)
        out.append(packed[at:at + n].reshape(-1)[:a.size].reshape(a.shape))
        at += n
    return out


STORED = {"ssm_b_re": (0, 1, 3, 2), "ssm_b_im": (0, 1, 3, 2), "ssm_d": (0, 2, 1), "ssm_glu_b": (0, 2, 1),
          "ssm_glu_w": (0, 2, 3, 1)}


def _stored(k, a):
    return a.transpose(STORED[k]) if k in STORED else a


def _logical(k, a):
    return a.transpose(tuple(STORED[k].index(i) for i in range(a.ndim))) if k in STORED else a


SMALL = ("norm1_g", "ssm_lambda_re", "ssm_lambda_im", "ssm_log_step", "ssm_b_re", "ssm_b_im", "ssm_c_re", "ssm_c_im",
         "ssm_d", "ssm_glu_w", "ssm_glu_b", "ssm_norm_g", "pool_w", "pool_scale", "pool_norm_g", "norm2_g",
         "final_norm_g")
LARGE = ("w_in", "w_out", "w_gate", "w_up", "w_down")
WEIGHTS = ("meta_tokens", "norm1_g", "w_in", "ssm_lambda_re", "ssm_lambda_im", "ssm_log_step", "ssm_b_re", "ssm_b_im",
           "ssm_c_re", "ssm_c_im", "ssm_d", "ssm_glu_w", "ssm_glu_b", "ssm_norm_g", "pool_w", "pool_scale",
           "pool_norm_g", "w_out", "norm2_g", "w_gate", "w_up", "w_down", "final_norm_g")


def _step(x, target, w, m, v):
    seq = x.shape[1]
    n_rows = N_META + seq
    n_pad, tm, tp, tc, tg = _plan(n_rows)
    xq, yq, cq = lax.axis_index("x"), lax.axis_index("y"), lax.axis_index("c")
    place = jnp.stack([2 * xq + yq, cq]).astype(jnp.int32)

    def halves(a2d):
        return a2d.reshape(2, a2d.shape[0] // 2, a2d.shape[1])

    def local2d(t):
        return {"w_gate": lambda a: a[0].T, "w_up": lambda a: a[0].T}.get(t, lambda a: a[0])

    shards = [halves(local2d(k)(w[k])) for k in LARGE] + [halves(w["meta_tokens"])]
    full = _cast_shards(shards, [BF16] * len(LARGE) + [F32], place)
    w_in_full, meta_full = _gather_shards([full[0], full[5]])
    late, gather_sems, gather_token = _copies_start("gather_start", list(full[1:5]), (12,), _build_gather,
                                                    after=w_in_full)
    w_in_b = w_in_full.reshape(D_MODEL, D_MODEL)
    meta = meta_full.reshape(N_SHARD, N_META, D_MODEL // N_SHARD).transpose(1, 0, 2).reshape(N_META, D_MODEL)

    h0 = _pad_rows(jnp.concatenate([meta, x[0]], axis=0), n_pad)
    tgt = _pad_rows(jnp.concatenate([jnp.zeros((N_META, D_MODEL), F32), target[0]], axis=0), n_pad)
    s5_in = (w["ssm_lambda_re"][0], w["ssm_lambda_im"][0], w["ssm_log_step"][0], w["ssm_b_re"][0], w["ssm_b_im"][0],
             w["ssm_c_re"][0], w["ssm_c_im"][0], w["ssm_glu_w"][0])
    (lam, bbr, bbi, crt, cit, glu), s5_vjp = jax.vjp(_s5_operands, *s5_in)
    bbr_b, bbi_b, crt_b, cit_b, glu_b16 = (t.astype(BF16) for t in (bbr, bbi, crt, cit, glu))
    s5_vecs = _pad_rows(jnp.concatenate([w["ssm_d"].reshape(1, D_SSM), w["ssm_glu_b"].reshape(1, D_SSM),
                                         w["ssm_norm_g"].reshape(1, D_SSM)], axis=0), 8)
    pool_vecs = _pad_rows(jnp.concatenate([w["pool_scale"].reshape(1, D_POOL), w["pool_norm_g"].reshape(1, D_POOL)],
                                          axis=0), 8)
    pw_b = w["pool_w"][0].astype(BF16)
    g1, g2, gf = w["norm1_g"].reshape(1, D_MODEL), w["norm2_g"].reshape(1, D_MODEL), w["final_norm_g"].reshape(1, D_MODEL)

    u, vv = _fwd_in(h0, g1, w_in_b, tp, gather_token)
    sr, si, y, ms = _s5_fwd(u, lam, bbr_b, bbi_b, crt_b, cit_b, s5_vecs, glu_b16, tc)
    feat, mp = _pool_fwd(vv, pw_b, pool_vecs, tc)
    late = _forward_halves(_copies_wait("gather_wait", late, gather_sems, [ms, mp], _build_gather))
    w_out_b = late[0].reshape(D_MODEL, D_MODEL)
    wg_b, wu_b, wd_b = (t.reshape(N_SHARD, FF_SHARD, D_MODEL) for t in late[1:])
    h1, n2, a, b, ff, dh2, dh2b, loss_acc, dgf = _fwd_ffn(h0, ms, mp, w_out_b, g2, wg_b, wu_b, wd_b, gf, tgt, tm, n_rows)

    def quarters(t):
        if t.ndim == 2:
            t = t.reshape(N_SHARD, t.shape[0] // N_SHARD, t.shape[1])
        return t.reshape(N_SHARD, 2, t.shape[1] // 2, t.shape[2])

    def landing(like, lead, dtype):
        return [lax.empty((lead,) + t.shape[2:], dtype) for t in like]

    da, db, dh1, dg2 = _bwd_ffn(dh2, a, b, wg_b, wu_b, wd_b, h1, g2, tm)
    ffn_g = [quarters(t) for t in _grad_ffn(n2, da, db, ff, dh2b, tg)]
    nf = len(ffn_g)
    moved, swap_sems, swap_token = _copies_start("swap_start", ffn_g + landing(ffn_g, N_SHARD, F32), (nf,), _build_swap)
    dms, dmp, dwo = _bwd_out(dh1, ms, mp, w_out_b, tp, swap_token)
    moved = _copies_wait("swap_wait", moved, swap_sems, [dwo], _build_swap)
    ffn_parts, ffn_own = _sum_siblings("ffn", moved[:nf], moved[nf:], place)
    moved, exch_sems, exch_token = _copies_start("exchange_start", list(ffn_parts) + landing(ffn_g, 3, BF16), (3 * nf,),
                                                 _build_exchange)
    du, dbbr, dbbi, dcrt, dcit, dglu, ds5v, dlam = _s5_bwd(dms, y, u, sr, si, lam, bbr_b, bbi_b, crt_b, cit_b,
                                                           s5_vecs, glu_b16, tc, exch_token)
    dv, dpw, dpoolv = _pool_bwd(dmp, feat, pw_b, pool_vecs, tc)
    dh0, dwi, dg1 = _bwd_in(du, dv, h0, dh1, g1, w_in_b, tp)
    ffn_from_chips = _copies_wait("exchange_wait", moved, exch_sems, [dh0], _build_exchange)[nf:]
    dlam = _pad_rows(jnp.concatenate([jnp.sum(dlam[:SEGMENTS], axis=0, keepdims=True),
                                      jnp.sum(dlam[SEGMENTS:], axis=0, keepdims=True)], axis=0), 8)
    d_lre, d_lim, d_lstep, d_bre, d_bim, d_cre, d_cim, d_gluw = s5_vjp((dlam, dbbr, dbbi, dcrt, dcit, dglu))
    grad_x = dh0[N_META:n_rows][None]

    small_g = {
        "norm1_g": dg1, "ssm_lambda_re": d_lre, "ssm_lambda_im": d_lim, "ssm_log_step": d_lstep, "ssm_b_re": d_bre,
        "ssm_b_im": d_bim, "ssm_c_re": d_cre, "ssm_c_im": d_cim, "ssm_d": ds5v[0], "ssm_glu_w": d_gluw,
        "ssm_glu_b": ds5v[1], "ssm_norm_g": ds5v[2], "pool_w": dpw, "pool_scale": dpoolv[0], "pool_norm_g": dpoolv[1],
        "norm2_g": dg2, "final_norm_g": dgf,
    }
    like = [_stored(k, w[k]) for k in SMALL]
    packed_g = _pack([_stored(k, small_g[k].reshape(w[k].shape)) for k in SMALL] + [dh0[:N_META], loss_acc[0:1, 0:1]])

    mix_g = [quarters(t) for t in (dwi, dwo)]
    mix_parts, mix_own = _sum_siblings("mix", mix_g, _swap_halves(mix_g), place)
    moved, mix_sems, mix_token = _copies_start("mix_exchange_start", list(mix_parts) + landing(mix_g, 3, BF16),
                                               (3 * len(mix_g),), _build_exchange)
    spread, small_sems, small_token = _copies_start(
        "small_start", [packed_g, lax.empty((7,) + packed_g.shape, F32)], (7,), _build_spread, after=mix_token)
    mix_from_chips = _copies_wait("mix_exchange_wait", moved, mix_sems, [small_token], _build_exchange)[len(mix_g):]
    joined = _join_halves(_sum_chips(list(mix_own) + list(ffn_own), list(mix_from_chips) + list(ffn_from_chips), place))
    g_large = [j.reshape(j.shape[0] * j.shape[1], j.shape[2]) for j in joined]
    w2d, m2d, v2d = ([local2d(k)(t[k]) for k in LARGE] for t in (w, m, v))
    d_large, m_large, v_large = _adamw("adamw_large", w2d, g_large, m2d, v2d, 8)

    own_g, landed = _copies_wait("small_wait", spread, small_sems, [d_large[0]], _build_spread)
    g_pk = _reduce_small(own_g, _forward_small(landed))
    g_small = _unpack(g_pk, like + [jax.ShapeDtypeStruct((N_META, D_MODEL), F32), jax.ShapeDtypeStruct((1, 1), F32)])
    loss = g_small.pop()[0, 0]
    rows2d = lambda t: t.reshape(1, -1) if t.ndim == 1 else t
    d_small, m_small, v_small = _adamw_small(*([rows2d(t) for t in ts] for ts in (
        like, g_small[:-1], [_stored(k, m[k]) for k in SMALL], [_stored(k, v[k]) for k in SMALL])))
    g_small, d_small, m_small, v_small = ([_logical(k, t.reshape(a.shape)) for t, a, k in zip(ts, like, SMALL)] + ts[len(SMALL):]
                                          for ts in (g_small, list(d_small), list(m_small), list(v_small)))
    q = place[0]
    g_meta = lax.dynamic_slice_in_dim(g_small[-1], q * (D_MODEL // N_SHARD), D_MODEL // N_SHARD, axis=1)
    d_meta, m_meta, v_meta = _adamw("adamw_meta", [w["meta_tokens"]], [g_meta], [m["meta_tokens"]],
                                    [v["meta_tokens"]], 1)

    grads, deltas, new_m, new_v = {}, {}, {}, {}
    for i, k in enumerate(SMALL):
        grads[k], deltas[k], new_m[k], new_v[k] = g_small[i], d_small[i], m_small[i], v_small[i]
    for i, k in enumerate(LARGE):
        back = (lambda t: t.T[None]) if k in ("w_gate", "w_up") else (lambda t: t[None])
        grads[k], deltas[k], new_m[k], new_v[k] = (back(t) for t in (g_large[i], d_large[i], m_large[i], v_large[i]))
    grads["meta_tokens"], deltas["meta_tokens"] = g_meta, d_meta[0]
    new_m["meta_tokens"], new_v["meta_tokens"] = m_meta[0], v_meta[0]
    return (loss, grad_x, *[grads[k] for k in WEIGHTS], *[deltas[k] for k in WEIGHTS],
            *[new_m[k] for k in WEIGHTS], *[new_v[k] for k in WEIGHTS])


def kernel(x, meta_tokens, norm1_g, w_in, ssm_lambda_re, ssm_lambda_im, ssm_log_step, ssm_b_re, ssm_b_im, ssm_c_re, ssm_c_im, ssm_d, ssm_glu_w, ssm_glu_b, ssm_norm_g, pool_w, pool_scale, pool_norm_g, w_out, norm2_g, w_gate, w_up, w_down, final_norm_g, loss_target, m_meta_tokens, m_norm1_g, m_w_in, m_ssm_lambda_re, m_ssm_lambda_im, m_ssm_log_step, m_ssm_b_re, m_ssm_b_im, m_ssm_c_re, m_ssm_c_im, m_ssm_d, m_ssm_glu_w, m_ssm_glu_b, m_ssm_norm_g, m_pool_w, m_pool_scale, m_pool_norm_g, m_w_out, m_norm2_g, m_w_gate, m_w_up, m_w_down, m_final_norm_g, v_meta_tokens, v_norm1_g, v_w_in, v_ssm_lambda_re, v_ssm_lambda_im, v_ssm_log_step, v_ssm_b_re, v_ssm_b_im, v_ssm_c_re, v_ssm_c_im, v_ssm_d, v_ssm_glu_w, v_ssm_glu_b, v_ssm_norm_g, v_pool_w, v_pool_scale, v_pool_norm_g, v_w_out, v_norm2_g, v_w_gate, v_w_up, v_w_down, v_final_norm_g):
    w = dict(meta_tokens=meta_tokens, norm1_g=norm1_g, w_in=w_in, ssm_lambda_re=ssm_lambda_re, ssm_lambda_im=ssm_lambda_im, ssm_log_step=ssm_log_step, ssm_b_re=ssm_b_re, ssm_b_im=ssm_b_im, ssm_c_re=ssm_c_re, ssm_c_im=ssm_c_im, ssm_d=ssm_d, ssm_glu_w=ssm_glu_w, ssm_glu_b=ssm_glu_b, ssm_norm_g=ssm_norm_g, pool_w=pool_w, pool_scale=pool_scale, pool_norm_g=pool_norm_g, w_out=w_out, norm2_g=norm2_g, w_gate=w_gate, w_up=w_up, w_down=w_down, final_norm_g=final_norm_g)
    m = dict(meta_tokens=m_meta_tokens, norm1_g=m_norm1_g, w_in=m_w_in, ssm_lambda_re=m_ssm_lambda_re, ssm_lambda_im=m_ssm_lambda_im, ssm_log_step=m_ssm_log_step, ssm_b_re=m_ssm_b_re, ssm_b_im=m_ssm_b_im, ssm_c_re=m_ssm_c_re, ssm_c_im=m_ssm_c_im, ssm_d=m_ssm_d, ssm_glu_w=m_ssm_glu_w, ssm_glu_b=m_ssm_glu_b, ssm_norm_g=m_ssm_norm_g, pool_w=m_pool_w, pool_scale=m_pool_scale, pool_norm_g=m_pool_norm_g, w_out=m_w_out, norm2_g=m_norm2_g, w_gate=m_w_gate, w_up=m_w_up, w_down=m_w_down, final_norm_g=m_final_norm_g)
    v = dict(meta_tokens=v_meta_tokens, norm1_g=v_norm1_g, w_in=v_w_in, ssm_lambda_re=v_ssm_lambda_re, ssm_lambda_im=v_ssm_lambda_im, ssm_log_step=v_ssm_log_step, ssm_b_re=v_ssm_b_re, ssm_b_im=v_ssm_b_im, ssm_c_re=v_ssm_c_re, ssm_c_im=v_ssm_c_im, ssm_d=v_ssm_d, ssm_glu_w=v_ssm_glu_w, ssm_glu_b=v_ssm_glu_b, ssm_norm_g=v_ssm_norm_g, pool_w=v_pool_w, pool_scale=v_pool_scale, pool_norm_g=v_pool_norm_g, w_out=v_w_out, norm2_g=v_norm2_g, w_gate=v_w_gate, w_up=v_w_up, w_down=v_w_down, final_norm_g=v_final_norm_g)
    return _step(x, loss_target, w, m, v)
```

```python
import functools
import math

import jax
import jax.numpy as jnp
from jax import lax
from jax.experimental import pallas as pl
from jax.experimental.pallas import tpu as pltpu

F32 = jnp.float32
BF16 = jnp.bfloat16
MESH = pl.DeviceIdType.MESH
AXES = ("x", "y", "c")

D_MODEL = 1024
D_SSM = 512
D_POOL = 512
N_META = 16
SSM_GROUP = 16
SSM_GROUPS = 32
SSM_STATE = 64
N_STATE = SSM_GROUPS * SSM_STATE
STATE_BLOCKS = N_STATE // 128
SUPER = 4
POOL_WINDOWS = (2, 4, 8, 16)
POOL_HALO = 16
D_FF = 2816
N_SHARD = 4
FF_SHARD = D_FF // N_SHARD
EPS = 1e-6
ADAM_LR, ADAM_B1, ADAM_B2, ADAM_EPS, ADAM_WD, ADAM_STEP = 0.001, 0.9, 0.999, 1e-08, 0.01, 10
VMEM_LIMIT = 56 * 1024 * 1024


def _plan(n_rows):
    if n_rows > 2048:
        tm, tp, tc, tg = 416, 832, 320, 1040
    else:
        tm, tp, tc, tg = 128, 128, 64, 128
    step = math.lcm(tm, tp, tc, tg)
    return -(-n_rows // step) * step, tm, tp, tc, tg


def _params(sem=None):
    return pltpu.CompilerParams(dimension_semantics=sem, vmem_limit_bytes=VMEM_LIMIT)


def _dot(a, b):
    return jnp.dot(a, b, preferred_element_type=F32)


def _dot_nt(a, b):
    return lax.dot_general(a, b, (((1,), (1,)), ((), ())), preferred_element_type=F32)


def _dot_tn(a, b):
    return lax.dot_general(a, b, (((0,), (0,)), ((), ())), preferred_element_type=F32)


def _sigmoid(x):
    return 0.5 * jnp.tanh(0.5 * x) + 0.5


_GELU_C = math.sqrt(2.0 / math.pi)


def _gelu_and_grad(y):
    y2 = y * y
    t = jnp.tanh(_GELU_C * (y + 0.044715 * y * y2))
    g = 0.5 * y * (1.0 + t)
    dg = 0.5 * (1.0 + t) + 0.5 * y * (1.0 - t * t) * (_GELU_C * (1.0 + 3.0 * 0.044715 * y2))
    return g, dg


def _rms(x):
    return lax.rsqrt(jnp.mean(x * x, axis=-1, keepdims=True) + EPS)


def _rms_bwd(dn, xhat, r):
    return r * (dn - xhat * jnp.mean(dn * xhat, axis=-1, keepdims=True))


def _full(shape):
    nd = len(shape)
    return pl.BlockSpec(shape, lambda *_: (0,) * nd)


def _fwd_in(h0, g1, w_in_b, tm, token):
    n_pad = h0.shape[0]

    def body(h_ref, g_ref, w_ref, token_ref, u_ref, v_ref):
        h = h_ref[...]
        n1 = (h * _rms(h) * g_ref[...]).astype(BF16)
        proj = _dot(n1, w_ref[...])
        for i in range(4):
            u_ref[i] = proj[:, 128 * i:128 * (i + 1)]
        v_ref[...] = proj[:, D_SSM:]

    row = lambda w: pl.BlockSpec((tm, w), lambda i: (i, 0))
    return pl.pallas_call(
        body, grid=(n_pad // tm,), name="fwd_in",
        in_specs=[row(D_MODEL), _full((1, D_MODEL)), _full((D_MODEL, D_MODEL)), _ANY],
        out_specs=[pl.BlockSpec((4, tm, 128), lambda i: (0, i, 0)), row(D_POOL)],
        out_shape=[jax.ShapeDtypeStruct((4, n_pad, 128), F32), jax.ShapeDtypeStruct((n_pad, D_POOL), F32)],
        compiler_params=_params(("parallel",)),
    )(h0, g1, w_in_b, token)


def _fwd_ffn(h0, ms, mp, w_out_b, g2, wg_b, wu_b, wd_b, gf, target, tm, n_valid):
    n_pad = h0.shape[0]
    nt = n_pad // tm

    def body(h0_ref, ms_ref, mp_ref, wo_ref, g2_ref, wg_ref, wu_ref, wd_ref, gf_ref, tgt_ref,
             h1_ref, n2_ref, a_ref, b_ref, ff_ref, dh2_ref, dh2b_ref, loss_ref, dgf_ref, acc):
        i, q = pl.program_id(0), pl.program_id(1)

        @pl.when((i == 0) & (q == 0))
        def _():
            loss_ref[...] = jnp.zeros_like(loss_ref)
            dgf_ref[...] = jnp.zeros_like(dgf_ref)

        @pl.when(q == 0)
        def _():
            h1 = h0_ref[...] + _dot(ms_ref[...], wo_ref[:D_SSM, :]) + _dot(mp_ref[...], wo_ref[D_SSM:, :])
            h1_ref[...] = h1
            acc[...] = h1
            n2_ref[...] = (h1 * _rms(h1) * g2_ref[...]).astype(BF16)

        n2 = n2_ref[...]
        a = _dot_nt(n2, wg_ref[0])
        b = _dot_nt(n2, wu_ref[0])
        a_ref[0] = a.astype(BF16)
        b_ref[0] = b.astype(BF16)
        ff = (a * _sigmoid(a) * b).astype(BF16)
        ff_ref[0] = ff
        acc[...] += _dot(ff, wd_ref[0])

        @pl.when(q == N_SHARD - 1)
        def _():
            h2 = acc[...]
            r = _rms(h2)
            xhat = h2 * r
            gf_row = gf_ref[...]
            rows = i * tm + lax.broadcasted_iota(jnp.int32, (tm, 1), 0)
            valid = (rows >= N_META) & (rows < n_valid)
            diff = jnp.where(valid, xhat * gf_row - tgt_ref[...], 0.0)
            loss_ref[...] += jnp.full(loss_ref.shape, 0.5 / D_MODEL, F32) * jnp.sum(diff * diff)
            dout = diff * (1.0 / D_MODEL)
            dgf_ref[...] += jnp.sum(dout * xhat, axis=0, keepdims=True)
            dh2 = _rms_bwd(dout * gf_row, xhat, r)
            dh2_ref[...] = dh2
            dh2b_ref[...] = dh2.astype(BF16)

    row = lambda w: pl.BlockSpec((tm, w), lambda i, q: (i, 0))
    shard_rows = pl.BlockSpec((1, FF_SHARD, D_MODEL), lambda i, q: (q, 0, 0))
    act = pl.BlockSpec((1, tm, FF_SHARD), lambda i, q: (q, i, 0))
    sds = jax.ShapeDtypeStruct
    return pl.pallas_call(
        body, grid=(nt, N_SHARD), name="fwd_ffn",
        in_specs=[row(D_MODEL), row(D_SSM), row(D_POOL), _full((D_MODEL, D_MODEL)), _full((1, D_MODEL)),
                  shard_rows, shard_rows, shard_rows, _full((1, D_MODEL)), row(D_MODEL)],
        out_specs=[row(D_MODEL), row(D_MODEL), act, act, act, row(D_MODEL), row(D_MODEL), _full((8, 128)),
                   _full((1, D_MODEL))],
        out_shape=[sds((n_pad, D_MODEL), F32), sds((n_pad, D_MODEL), BF16),
                   sds((N_SHARD, n_pad, FF_SHARD), BF16), sds((N_SHARD, n_pad, FF_SHARD), BF16),
                   sds((N_SHARD, n_pad, FF_SHARD), BF16), sds((n_pad, D_MODEL), F32), sds((n_pad, D_MODEL), BF16),
                   sds((8, 128), F32), sds((1, D_MODEL), F32)],
        scratch_shapes=[pltpu.VMEM((tm, D_MODEL), F32)],
        compiler_params=_params(("arbitrary", "arbitrary")),
    )(h0, ms, mp, w_out_b, g2, wg_b, wu_b, wd_b, gf, target)


def _bwd_ffn(dh2, dh2b, a, b, wg_b, wu_b, wd_b, h1, g2, tm):
    n_pad = dh2.shape[0]

    def body(dh2_ref, dh2b_ref, a_ref, b_ref, wg_hbm, wu_hbm, wd_hbm, h1_ref, g2_ref, da_ref, db_ref, dh1_ref, dg2_ref,
             wg_ref, wu_ref, wd_ref, sems, acc):
        @pl.when(pl.program_id(0) == 0)
        def _():
            dg2_ref[...] = jnp.zeros_like(dg2_ref)
            cps = [pltpu.make_async_copy(src, dst, sems.at[k])
                   for k, (src, dst) in enumerate(((wg_hbm, wg_ref), (wu_hbm, wu_ref), (wd_hbm, wd_ref)))]
            for cp in cps:
                cp.start()
            for cp in cps:
                cp.wait()

        dh2b = dh2b_ref[...]
        for q in range(N_SHARD):
            dff = _dot_nt(dh2b, wd_ref[q])
            a_v, b_v = a_ref[q].astype(F32), b_ref[q].astype(F32)
            sig = _sigmoid(a_v)
            silu = a_v * sig
            da = (dff * b_v * (sig + silu * (1.0 - sig))).astype(BF16)
            db = (dff * silu).astype(BF16)
            da_ref[q] = da
            db_ref[q] = db
            part = _dot(da, wg_ref[q]) + _dot(db, wu_ref[q])
            if q == 0:
                acc[...] = part
            else:
                acc[...] += part

        h1 = h1_ref[...]
        r = _rms(h1)
        xhat = h1 * r
        dn2 = acc[...]
        dg2_ref[...] += jnp.sum(dn2 * xhat, axis=0, keepdims=True)
        dh1_ref[...] = dh2_ref[...] + _rms_bwd(dn2 * g2_ref[...], xhat, r)

    row = lambda w: pl.BlockSpec((tm, w), lambda i: (i, 0))
    act = pl.BlockSpec((N_SHARD, tm, FF_SHARD), lambda i: (0, i, 0))
    sds = jax.ShapeDtypeStruct
    return pl.pallas_call(
        body, grid=(n_pad // tm,), name="bwd_ffn",
        in_specs=[row(D_MODEL), row(D_MODEL), act, act, _ANY, _ANY, _ANY, row(D_MODEL), _full((1, D_MODEL))],
        out_specs=[act, act, row(D_MODEL), _full((1, D_MODEL))],
        out_shape=[sds((N_SHARD, n_pad, FF_SHARD), BF16), sds((N_SHARD, n_pad, FF_SHARD), BF16),
                   sds((n_pad, D_MODEL), F32), sds((1, D_MODEL), F32)],
        scratch_shapes=[pltpu.VMEM(wg_b.shape, BF16), pltpu.VMEM(wu_b.shape, BF16), pltpu.VMEM(wd_b.shape, BF16),
                        pltpu.SemaphoreType.DMA((3,)), pltpu.VMEM((tm, D_MODEL), F32)],
        compiler_params=_params(("arbitrary",)),
    )(dh2, dh2b, a, b, wg_b, wu_b, wd_b, h1, g2)


def _grad_ffn(n2, da, db, ff, dh2b, tm):
    n_pad = n2.shape[0]

    def body(n2_ref, da_ref, db_ref, ff_ref, dh2_ref, dwg_ref, dwu_ref, dwd_ref):
        i = pl.program_id(1)
        n2_v = n2_ref[...]
        gg = _dot_tn(da_ref[0], n2_v)
        gu = _dot_tn(db_ref[0], n2_v)
        gd = _dot_tn(ff_ref[0], dh2_ref[...])

        @pl.when(i == 0)
        def _():
            dwg_ref[0] = gg
            dwu_ref[0] = gu
            dwd_ref[0] = gd

        @pl.when(i > 0)
        def _():
            dwg_ref[0] += gg
            dwu_ref[0] += gu
            dwd_ref[0] += gd

    row = lambda w: pl.BlockSpec((tm, w), lambda q, i: (i, 0))
    act = pl.BlockSpec((1, tm, FF_SHARD), lambda q, i: (q, i, 0))
    sds = jax.ShapeDtypeStruct
    return pl.pallas_call(
        body, grid=(N_SHARD, n_pad // tm), name="grad_ffn",
        in_specs=[row(D_MODEL), act, act, act, row(D_MODEL)],
        out_specs=[pl.BlockSpec((1, FF_SHARD, D_MODEL), lambda q, i: (q, 0, 0))] * 3,
        out_shape=[sds((N_SHARD, FF_SHARD, D_MODEL), F32)] * 3,
        compiler_params=_params(("parallel", "arbitrary")),
    )(n2, da, db, ff, dh2b)


def _bwd_out(dh1, ms, mp, w_out_b, tm, token):
    n_pad = dh1.shape[0]

    def body(dh1_ref, ms_ref, mp_ref, wo_ref, token_ref, dms_ref, dmp_ref, dwo_ref):
        i = pl.program_id(0)

        @pl.when(i == 0)
        def _():
            dwo_ref[...] = jnp.zeros_like(dwo_ref)

        d = dh1_ref[...].astype(BF16)
        dms = _dot_nt(d, wo_ref[:D_SSM, :])
        for k in range(4):
            dms_ref[k] = dms[:, 128 * k:128 * (k + 1)]
        dmp_ref[...] = _dot_nt(d, wo_ref[D_SSM:, :])
        dwo_ref[:D_SSM, :] += _dot_tn(ms_ref[...], d)
        dwo_ref[D_SSM:, :] += _dot_tn(mp_ref[...], d)

    row = lambda w: pl.BlockSpec((tm, w), lambda i: (i, 0))
    sds = jax.ShapeDtypeStruct
    return pl.pallas_call(
        body, grid=(n_pad // tm,), name="bwd_out",
        in_specs=[row(D_MODEL), row(D_SSM), row(D_POOL), _full((D_MODEL, D_MODEL)), _ANY],
        out_specs=[pl.BlockSpec((4, tm, 128), lambda i: (0, i, 0)), row(D_POOL), _full((D_MODEL, D_MODEL))],
        out_shape=[sds((4, n_pad, 128), F32), sds((n_pad, D_POOL), F32), sds((D_MODEL, D_MODEL), F32)],
        compiler_params=_params(("arbitrary",)),
    )(dh1, ms, mp, w_out_b, token)


def _bwd_in(du, dv, h0, dh1, g1, w_in_b, tm):
    n_pad = h0.shape[0]

    def body(du_ref, dv_ref, h0_ref, dh1_ref, g1_ref, w_ref, dh0_ref, dwi_ref, dg1_ref):
        i = pl.program_id(0)

        @pl.when(i == 0)
        def _():
            dwi_ref[...] = jnp.zeros_like(dwi_ref)
            dg1_ref[...] = jnp.zeros_like(dg1_ref)

        dub = du_ref[...].astype(BF16)
        dvb = dv_ref[...].astype(BF16)
        dn1 = _dot_nt(dub, w_ref[:, :D_SSM]) + _dot_nt(dvb, w_ref[:, D_SSM:])
        h = h0_ref[...]
        r = _rms(h)
        xhat = h * r
        g_row = g1_ref[...]
        n1 = (xhat * g_row).astype(BF16)
        dwi_ref[:, :D_SSM] += _dot_tn(n1, dub)
        dwi_ref[:, D_SSM:] += _dot_tn(n1, dvb)
        dg1_ref[...] += jnp.sum(dn1 * xhat, axis=0, keepdims=True)
        dh0_ref[...] = dh1_ref[...] + _rms_bwd(dn1 * g_row, xhat, r)

    row = lambda w: pl.BlockSpec((tm, w), lambda i: (i, 0))
    sds = jax.ShapeDtypeStruct
    return pl.pallas_call(
        body, grid=(n_pad // tm,), name="bwd_in",
        in_specs=[row(D_SSM), row(D_POOL), row(D_MODEL), row(D_MODEL), _full((1, D_MODEL)), _full((D_MODEL, D_MODEL))],
        out_specs=[row(D_MODEL), _full((D_MODEL, D_MODEL)), _full((1, D_MODEL))],
        out_shape=[sds((n_pad, D_MODEL), F32), sds((D_MODEL, D_MODEL), F32), sds((1, D_MODEL), F32)],
        compiler_params=_params(("arbitrary",)),
    )(du, dv, h0, dh1, g1, w_in_b)


SEGMENTS = 8


def _interleaved(ref, seg):
    return jnp.concatenate(
        [jnp.concatenate([ref[i, pl.ds(j, SEGMENTS, stride=seg), :] for i in range(4)], axis=1) for j in range(seg)],
        axis=0)


def _time_order(scratch, val, seg):
    for i in range(4):
        scratch[i] = val[:, 128 * i:128 * (i + 1)]
    tiles = []
    for m in range(val.shape[0] // 8):
        s, j0 = divmod(8 * m, seg)
        tiles.append(jnp.concatenate(
            [scratch[i, pl.ds(8 * j0 + s, 8, stride=SEGMENTS), :] for i in range(4)], axis=1))
    return jnp.concatenate(tiles, axis=0)


def _power_table(lam_ref, pw_r, pw_i, seg):
    a_r = jnp.broadcast_to(lam_ref[0:1, :], (SEGMENTS, N_STATE))
    a_i = jnp.broadcast_to(lam_ref[1:2, :], (SEGMENTS, N_STATE))
    p_r, p_i = a_r, a_i
    for k in range(seg):
        pw_r[SEGMENTS * k:SEGMENTS * (k + 1), :] = p_r
        pw_i[SEGMENTS * k:SEGMENTS * (k + 1), :] = p_i
        p_r, p_i = p_r * a_r - p_i * a_i, p_r * a_i + p_i * a_r


def _segment_scan(xr_ref, xi_ref, cols, pw_r, pw_i, hr_s, hi_s, seg, reverse):
    sign = -1.0 if reverse else 1.0
    a_r, a_i = pw_r[0:SEGMENTS, cols], sign * pw_i[0:SEGMENTS, cols]

    def step(n, carry):
        hr, hi = carry
        o = pl.multiple_of((seg - 1 - n if reverse else n) * SEGMENTS, SEGMENTS)
        nr = a_r * hr - a_i * hi + xr_ref[pl.ds(o, SEGMENTS), cols]
        ni = a_r * hi + a_i * hr + xi_ref[pl.ds(o, SEGMENTS), cols]
        xr_ref[pl.ds(o, SEGMENTS), cols] = nr
        xi_ref[pl.ds(o, SEGMENTS), cols] = ni
        return nr, ni

    zero = jnp.zeros((SEGMENTS, cols.stop - cols.start), F32)
    e_r, e_i = lax.fori_loop(0, seg, step, (zero, zero), unroll=2)

    top = SEGMENTS * (seg - 1)
    ls_r, ls_i = pw_r[top:top + 1, cols], sign * pw_i[top:top + 1, cols]
    c_r, c_i = hr_s[0:1, cols], hi_s[0:1, cols]
    in_r, in_i = [None] * SEGMENTS, [None] * SEGMENTS
    for s in (range(SEGMENTS - 1, -1, -1) if reverse else range(SEGMENTS)):
        in_r[s], in_i[s] = c_r, c_i
        c_r, c_i = (e_r[s:s + 1, :] + ls_r * c_r - ls_i * c_i, e_i[s:s + 1, :] + ls_r * c_i + ls_i * c_r)
    hr_s[0:1, cols] = c_r
    hi_s[0:1, cols] = c_i
    cm_r, cm_i = jnp.concatenate(in_r, axis=0), jnp.concatenate(in_i, axis=0)

    def fix(jj, _):
        o = pl.multiple_of(jj * SEGMENTS, SEGMENTS)
        k = pl.multiple_of((seg - 1 - jj if reverse else jj) * SEGMENTS, SEGMENTS)
        p_r, p_i = pw_r[pl.ds(k, SEGMENTS), cols], sign * pw_i[pl.ds(k, SEGMENTS), cols]
        xr_ref[pl.ds(o, SEGMENTS), cols] += p_r * cm_r - p_i * cm_i
        xi_ref[pl.ds(o, SEGMENTS), cols] += p_r * cm_i + p_i * cm_r
        return 0

    lax.fori_loop(0, seg, fix, 0, unroll=2)


def _s5_tail(y, glu_ref, glub):
    g, dgelu = _gelu_and_grad(y)
    gb = g.astype(BF16)
    gate = jnp.concatenate([_dot(gb[:, 128 * j:128 * (j + 1)], glu_ref[j]) for j in range(SUPER)], axis=1) + glub
    sig = _sigmoid(gate)
    return g, gb, dgelu, sig, g * sig


def _s5_fwd(u4, lam, bbr, bbi, crt, cit, vecs, glu, tc):
    n_pad = u4.shape[1]
    seg = tc // SEGMENTS

    def body(u_ref, lam_ref, bbr_ref, bbi_ref, crt_ref, cit_ref, vec_ref, glu_ref,
             sr_ref, si_ref, y_ref, ms_ref, hr_s, hi_s, pw_r, pw_i, lanes):
        @pl.when(pl.program_id(0) == 0)
        def _():
            hr_s[...] = jnp.zeros_like(hr_s)
            hi_s[...] = jnp.zeros_like(hi_s)
            _power_table(lam_ref, pw_r, pw_i, seg)

        u_v = _interleaved(u_ref, seg)
        ub = u_v.astype(BF16)
        for j in range(SUPER):
            uj = ub[:, 128 * j:128 * (j + 1)]
            sr_ref[:, 512 * j:512 * (j + 1)] = _dot(uj, bbr_ref[j])
            si_ref[:, 512 * j:512 * (j + 1)] = _dot(uj, bbi_ref[j])
        for j in range(SUPER):
            _segment_scan(sr_ref, si_ref, slice(512 * j, 512 * (j + 1)), pw_r, pw_i, hr_s, hi_s, seg, False)

        d_row, glub, gs = vec_ref[0:1, :], vec_ref[1:2, :], vec_ref[2:3, :]
        ys_c = []
        for j in range(SUPER):
            sr_j = sr_ref[:, 512 * j:512 * (j + 1)].astype(BF16)
            si_j = si_ref[:, 512 * j:512 * (j + 1)].astype(BF16)
            ys_c.append(_dot(sr_j, crt_ref[j]) - _dot(si_j, cit_ref[j]))
        y = jnp.concatenate(ys_c, axis=1) + d_row * u_v
        y_ref[...] = y
        _, _, _, _, ys = _s5_tail(y, glu_ref, glub)
        ms_ref[...] = _time_order(lanes, ys * _rms(ys) * gs, seg).astype(BF16)

    chunk = lambda w: pl.BlockSpec((tc, w), lambda c: (c, 0))
    lane_blocks = pl.BlockSpec((4, tc, 128), lambda c: (0, c, 0))
    sds = jax.ShapeDtypeStruct
    return pl.pallas_call(
        body, grid=(n_pad // tc,), name="s5_fwd",
        in_specs=[lane_blocks, _full((8, N_STATE)), _full((SUPER, 128, 512)), _full((SUPER, 128, 512)),
                  _full((SUPER, 512, 128)), _full((SUPER, 512, 128)), _full((8, D_SSM)), _full((SUPER, 128, 128))],
        out_specs=[chunk(N_STATE), chunk(N_STATE), chunk(D_SSM), chunk(D_SSM)],
        out_shape=[sds((n_pad, N_STATE), F32), sds((n_pad, N_STATE), F32),
                   sds((n_pad, D_SSM), F32), sds((n_pad, D_SSM), BF16)],
        scratch_shapes=[pltpu.VMEM((8, N_STATE), F32), pltpu.VMEM((8, N_STATE), F32),
                        pltpu.VMEM((tc, N_STATE), F32), pltpu.VMEM((tc, N_STATE), F32),
                        pltpu.VMEM((4, tc, 128), F32)],
        compiler_params=_params(("arbitrary",)),
    )(u4, lam, bbr, bbi, crt, cit, vecs, glu)


def _s5_bwd(dms4, y, u4, sr, si, lam, bbr, bbi, crt, cit, vecs, glu, tc, token):
    n_pad = u4.shape[1]
    nc = n_pad // tc
    seg = tc // SEGMENTS

    def body(dms_ref, y_ref, u_ref, sr_ref, si_ref, pr_ref, pi_ref, lam_ref, bbr_ref, bbi_ref, crt_ref, cit_ref,
             vec_ref, glu_ref, token_ref, du_ref, dbbr_ref, dbbi_ref, dcrt_ref, dcit_ref, dglu_ref, dvec_ref, dlam_ref,
             qr_s, qi_s, cr_s, ci_s, pw_r, pw_i, lanes):
        c = pl.program_id(0)

        @pl.when(c == 0)
        def _():
            for ref in (dbbr_ref, dbbi_ref, dcrt_ref, dcit_ref, dglu_ref, dvec_ref, dlam_ref, cr_s, ci_s):
                ref[...] = jnp.zeros_like(ref)
            _power_table(lam_ref, pw_r, pw_i, seg)

        d_row, glub, gs = vec_ref[0:1, :], vec_ref[1:2, :], vec_ref[2:3, :]
        y_v, u_v = y_ref[...], _interleaved(u_ref, seg)
        ub = u_v.astype(BF16)
        g, gb, dgelu, sig, ys = _s5_tail(y_v, glu_ref, glub)
        r = _rms(ys)
        xhat = ys * r
        dm = _interleaved(dms_ref, seg)
        dys = _rms_bwd(dm * gs, xhat, r)
        dgate = dys * g * sig * (1.0 - sig)
        dgateb = dgate.astype(BF16)
        dg = dys * sig + jnp.concatenate(
            [_dot_nt(dgateb[:, 128 * j:128 * (j + 1)], glu_ref[j]) for j in range(SUPER)], axis=1)
        dy = dg * dgelu
        dyb = dy.astype(BF16)
        dvec_ref[0:1, :] += jnp.sum(dy * u_v, axis=0, keepdims=True)
        dvec_ref[1:2, :] += jnp.sum(dgate, axis=0, keepdims=True)
        dvec_ref[2:3, :] += jnp.sum(dm * xhat, axis=0, keepdims=True)

        for j in range(SUPER):
            cols, states = slice(128 * j, 128 * (j + 1)), slice(512 * j, 512 * (j + 1))
            dglu_ref[j] += _dot_tn(gb[:, cols], dgateb[:, cols])
            dcrt_ref[j] += _dot_tn(sr_ref[:, states].astype(BF16), dyb[:, cols])
            dcit_ref[j] -= _dot_tn(si_ref[:, states].astype(BF16), dyb[:, cols])
            qr_s[:, states] = _dot_nt(dyb[:, cols], crt_ref[j])
            qi_s[:, states] = -_dot_nt(dyb[:, cols], cit_ref[j])

        first = c == nc - 1
        row0 = lax.broadcasted_iota(jnp.int32, (SEGMENTS, 1), 0) == 0
        last = (seg - 1) * SEGMENTS
        for j in range(SUPER):
            states = slice(512 * j, 512 * (j + 1))
            _segment_scan(qr_s, qi_s, states, pw_r, pw_i, cr_s, ci_s, seg, True)

            before_r = jnp.where(first, 0.0, pltpu.roll(pr_ref[:, states], 1, 0))
            before_i = jnp.where(first, 0.0, pltpu.roll(pi_ref[:, states], 1, 0))
            hp_r = jnp.where(row0, before_r, pltpu.roll(sr_ref[pl.ds(last, SEGMENTS), states], 1, 0))
            hp_i = jnp.where(row0, before_i, pltpu.roll(si_ref[pl.ds(last, SEGMENTS), states], 1, 0))
            q_r, q_i = qr_s[pl.ds(0, SEGMENTS), states], qi_s[pl.ds(0, SEGMENTS), states]

            def dlam_step(jj, acc):
                o = pl.multiple_of(jj * SEGMENTS, SEGMENTS)
                above = pl.multiple_of((jj - 1) * SEGMENTS, SEGMENTS)
                h_r, h_i = sr_ref[pl.ds(above, SEGMENTS), states], si_ref[pl.ds(above, SEGMENTS), states]
                t_r, t_i = qr_s[pl.ds(o, SEGMENTS), states], qi_s[pl.ds(o, SEGMENTS), states]
                return acc[0] + t_r * h_r + t_i * h_i, acc[1] + t_i * h_r - t_r * h_i

            acc = lax.fori_loop(1, seg, dlam_step, (q_r * hp_r + q_i * hp_i, q_i * hp_r - q_r * hp_i), unroll=2)
            dlam_ref[0:SEGMENTS, states] += acc[0]
            dlam_ref[SEGMENTS:, states] += acc[1]

        du_c = []
        for j in range(SUPER):
            cols, states = slice(128 * j, 128 * (j + 1)), slice(512 * j, 512 * (j + 1))
            qr_j = qr_s[:, states].astype(BF16)
            qi_j = qi_s[:, states].astype(BF16)
            du_c.append(_dot_nt(qr_j, bbr_ref[j]) + _dot_nt(qi_j, bbi_ref[j]))
            dbbr_ref[j] += _dot_tn(ub[:, cols], qr_j)
            dbbi_ref[j] += _dot_tn(ub[:, cols], qi_j)
        du_ref[...] = _time_order(lanes, jnp.concatenate(du_c, axis=1) + dy * d_row, seg)

    rev = lambda c: nc - 1 - c
    chunk = lambda w: pl.BlockSpec((tc, w), lambda c: (rev(c), 0))
    lane_blocks = pl.BlockSpec((4, tc, 128), lambda c: (0, rev(c), 0))
    prev = pl.BlockSpec((SEGMENTS, N_STATE), lambda c: (jnp.maximum(rev(c) * seg - 1, 0), 0))
    sds = jax.ShapeDtypeStruct
    return pl.pallas_call(
        body, grid=(nc,), name="s5_bwd",
        in_specs=[lane_blocks, chunk(D_SSM), lane_blocks, chunk(N_STATE), chunk(N_STATE), prev, prev,
                  _full((8, N_STATE)), _full((SUPER, 128, 512)), _full((SUPER, 128, 512)),
                  _full((SUPER, 512, 128)), _full((SUPER, 512, 128)), _full((8, D_SSM)), _full((SUPER, 128, 128)), _ANY],
        out_specs=[chunk(D_SSM), _full((SUPER, 128, 512)), _full((SUPER, 128, 512)), _full((SUPER, 512, 128)),
                   _full((SUPER, 512, 128)), _full((SUPER, 128, 128)), _full((8, D_SSM)), _full((2 * SEGMENTS, N_STATE))],
        out_shape=[sds((n_pad, D_SSM), F32), sds((SUPER, 128, 512), F32), sds((SUPER, 128, 512), F32),
                   sds((SUPER, 512, 128), F32), sds((SUPER, 512, 128), F32), sds((SUPER, 128, 128), F32),
                   sds((8, D_SSM), F32), sds((2 * SEGMENTS, N_STATE), F32)],
        scratch_shapes=[pltpu.VMEM((tc, N_STATE), F32), pltpu.VMEM((tc, N_STATE), F32),
                        pltpu.VMEM((8, N_STATE), F32), pltpu.VMEM((8, N_STATE), F32),
                        pltpu.VMEM((tc, N_STATE), F32), pltpu.VMEM((tc, N_STATE), F32),
                        pltpu.VMEM((4, tc, 128), F32)],
        compiler_params=_params(("arbitrary",)),
    )(dms4, y, u4, sr, si, sr, si, lam, bbr, bbi, crt, cit, vecs, glu, token)


def _inv_count(c_idx, tc, w):
    t = c_idx * tc + lax.broadcasted_iota(jnp.int32, (tc, 1), 0)
    return 1.0 / jnp.minimum(t + 1, w).astype(F32)


def _pool_fwd(v, pw_b, vecs, tc):
    n_pad = v.shape[0]

    def body(v_ref, pw_ref, vec_ref, feat_ref, mp_ref, hist):
        c = pl.program_id(0)

        @pl.when(c == 0)
        def _():
            hist[...] = jnp.zeros_like(hist)

        v_v = v_ref[...]
        ext = jnp.concatenate([hist[...], v_v], axis=0)
        hist[...] = v_v[tc - POOL_HALO:, :]
        feats, ps = [], []
        for k, w in enumerate(POOL_WINDOWS):
            cols = slice(128 * k, 128 * (k + 1))
            s = ext[:, cols]
            sh = 1
            while sh < w:
                s = s + pltpu.roll(s, sh, 0)
                sh *= 2
            f = (s[POOL_HALO:, :] * _inv_count(c, tc, w) - v_v[:, cols]).astype(BF16)
            feats.append(f)
            ps.append(_dot(f, pw_ref[k]))
        feat_ref[...] = jnp.concatenate(feats, axis=1)
        yp = jnp.concatenate(ps, axis=1) * vec_ref[0:1, :]
        mp_ref[...] = (yp * _rms(yp) * vec_ref[1:2, :]).astype(BF16)

    chunk = lambda w: pl.BlockSpec((tc, w), lambda c: (c, 0))
    sds = jax.ShapeDtypeStruct
    return pl.pallas_call(
        body, grid=(n_pad // tc,), name="pool_fwd",
        in_specs=[chunk(D_POOL), _full((4, 128, 128)), _full((8, D_POOL))],
        out_specs=[chunk(D_POOL), chunk(D_POOL)],
        out_shape=[sds((n_pad, D_POOL), BF16), sds((n_pad, D_POOL), BF16)],
        scratch_shapes=[pltpu.VMEM((POOL_HALO, D_POOL), F32)],
        compiler_params=_params(("arbitrary",)),
    )(v, pw_b, vecs)


def _pool_bwd(dmp, feat, pw_b, vecs, tc):
    n_pad = dmp.shape[0]
    nc = n_pad // tc

    def body(dmp_ref, feat_ref, pw_ref, vec_ref, dv_ref, dpw_ref, dvec_ref, fut):
        c = pl.program_id(0)

        @pl.when(c == 0)
        def _():
            fut[...] = jnp.zeros_like(fut)
            dpw_ref[...] = jnp.zeros_like(dpw_ref)
            dvec_ref[...] = jnp.zeros_like(dvec_ref)

        scale, gp = vec_ref[0:1, :], vec_ref[1:2, :]
        feat_v = feat_ref[...]
        p = jnp.concatenate([_dot(feat_v[:, 128 * k:128 * (k + 1)], pw_ref[k]) for k in range(4)], axis=1)
        yp = p * scale
        r = _rms(yp)
        xhat = yp * r
        dm = dmp_ref[...]
        dyp = _rms_bwd(dm * gp, xhat, r)
        dvec_ref[0:1, :] += jnp.sum(dyp * p, axis=0, keepdims=True)
        dvec_ref[1:2, :] += jnp.sum(dm * xhat, axis=0, keepdims=True)
        dpb = (dyp * scale).astype(BF16)
        es, dfs = [], []
        for k, w in enumerate(POOL_WINDOWS):
            cols = slice(128 * k, 128 * (k + 1))
            dpw_ref[k] += _dot_tn(feat_v[:, cols], dpb[:, cols])
            df = _dot_nt(dpb[:, cols], pw_ref[k])
            dfs.append(df)
            es.append(df * _inv_count(nc - 1 - c, tc, w))
        e = jnp.concatenate(es, axis=1)
        ext = jnp.concatenate([e, fut[...]], axis=0)
        fut[...] = e[:POOL_HALO, :]
        n_ext = tc + POOL_HALO
        dvs = []
        for k, w in enumerate(POOL_WINDOWS):
            s = ext[:, 128 * k:128 * (k + 1)]
            sh = 1
            while sh < w:
                s = s + pltpu.roll(s, n_ext - sh, 0)
                sh *= 2
            dvs.append(s[:tc, :] - dfs[k])
        dv_ref[...] = jnp.concatenate(dvs, axis=1)

    chunk = lambda w: pl.BlockSpec((tc, w), lambda c: (nc - 1 - c, 0))
    sds = jax.ShapeDtypeStruct
    return pl.pallas_call(
        body, grid=(nc,), name="pool_bwd",
        in_specs=[chunk(D_POOL), chunk(D_POOL), _full((4, 128, 128)), _full((8, D_POOL))],
        out_specs=[chunk(D_POOL), _full((4, 128, 128)), _full((8, D_POOL))],
        out_shape=[sds((n_pad, D_POOL), F32), sds((4, 128, 128), F32), sds((8, D_POOL), F32)],
        scratch_shapes=[pltpu.VMEM((POOL_HALO, D_POOL), F32)],
        compiler_params=_params(("arbitrary",)),
    )(dmp, feat, pw_b, vecs)


def _place():
    x, y, c = lax.axis_index("x"), lax.axis_index("y"), lax.axis_index("c")
    chips = [(1 - x, y), (x, 1 - y), (1 - x, 1 - y)]
    return x, y, c, chips


_ANY = pl.BlockSpec(memory_space=pl.ANY)


def _cast_shards(shards, dtypes, place):
    n = len(shards)

    def body(place_ref, *refs):
        for i in range(n):
            refs[n + i][0] = refs[i][...].astype(dtypes[i])

    return pl.pallas_call(
        body, name="cast_shards",
        grid_spec=pltpu.PrefetchScalarGridSpec(
            num_scalar_prefetch=1, grid=(1,),
            in_specs=[pl.BlockSpec(s.shape, lambda i, p: (0, 0, 0)) for s in shards],
            out_specs=[pl.BlockSpec((1,) + s.shape, lambda i, p: (p[0], 0, 0, 0)) for s in shards]),
        out_shape=[jax.ShapeDtypeStruct((N_SHARD,) + s.shape, dt) for s, dt in zip(shards, dtypes)],
        compiler_params=_params(("arbitrary",)),
    )(place, *shards)


def _gather_shards(full):
    n = len(full)

    def body(*refs):
        outs = refs[n:2 * n]
        ici_send, ici_recv, d2d_send, d2d_recv = refs[2 * n:]
        x, y, c, chips = _place()
        q = 2 * x + y
        sibling = (x, y, 1 - c)

        def ici(i, j, shard, to):
            return pltpu.make_async_remote_copy(src_ref=outs[i].at[q, c], dst_ref=outs[i].at[shard, c],
                                                send_sem=ici_send.at[i, j], recv_sem=ici_recv.at[i, j],
                                                device_id=to, device_id_type=MESH)

        def d2d(i, j, shard, half):
            return pltpu.make_async_remote_copy(src_ref=outs[i].at[shard, c], dst_ref=outs[i].at[shard, half],
                                                send_sem=d2d_send.at[i, j], recv_sem=d2d_recv.at[i, j],
                                                device_id=sibling, device_id_type=MESH)

        sends = [ici(i, j, q, (*chip, c)) for i in range(n) for j, chip in enumerate(chips)]
        for cp in sends:
            cp.start()
        passed = []
        for i in range(n):
            for j, (cx, cy) in enumerate(chips):
                ici(i, j, 2 * cx + cy, (cx, cy, c)).wait_recv()
                cp = d2d(i, j, 2 * cx + cy, c)
                cp.start()
                passed.append(cp)
        for i in range(n):
            for j, (cx, cy) in enumerate(chips):
                d2d(i, j, 2 * cx + cy, 1 - c).wait_recv()
        for cp in sends + passed:
            cp.wait_send()

    return pl.pallas_call(
        body, name="gather_shards",
        in_specs=[_ANY] * n, out_specs=[_ANY] * n,
        out_shape=[jax.ShapeDtypeStruct(f.shape, f.dtype) for f in full],
        input_output_aliases={i: i for i in range(n)},
        scratch_shapes=[pltpu.SemaphoreType.DMA((n, 3)), pltpu.SemaphoreType.DMA((n, 3)),
                        pltpu.SemaphoreType.DMA((n, 3)), pltpu.SemaphoreType.DMA((n, 3))],
    )(*full)


def _forward_halves(full):
    n = len(full)

    def body(*refs):
        outs = refs[n:2 * n]
        send, recv = refs[2 * n:]
        x, y, c, chips = _place()

        def d2d(i, j, shard, half):
            return pltpu.make_async_remote_copy(src_ref=outs[i].at[shard, c], dst_ref=outs[i].at[shard, half],
                                                send_sem=send.at[i, j], recv_sem=recv.at[i, j],
                                                device_id=(x, y, 1 - c), device_id_type=MESH)

        cps = [d2d(i, j, 2 * cx + cy, c) for i in range(n) for j, (cx, cy) in enumerate(chips)]
        for cp in cps:
            cp.start()
        for i in range(n):
            for j, (cx, cy) in enumerate(chips):
                d2d(i, j, 2 * cx + cy, 1 - c).wait_recv()
        for cp in cps:
            cp.wait_send()

    return pl.pallas_call(
        body, name="forward_halves",
        in_specs=[_ANY] * n, out_specs=[_ANY] * n,
        out_shape=[jax.ShapeDtypeStruct(f.shape, f.dtype) for f in full],
        input_output_aliases={i: i for i in range(n)},
        scratch_shapes=[pltpu.SemaphoreType.DMA((n, 3)), pltpu.SemaphoreType.DMA((n, 3))],
    )(*full)


_HBM = pl.BlockSpec(memory_space=pltpu.HBM)
_SEM = pl.BlockSpec(memory_space=pltpu.SEMAPHORE)
_EFFECT = pltpu.SideEffectType.DATAFLOW_SIDE_EFFECTING


def _copies_start(name, arrays, sem_shape, build, after=None):
    n = len(arrays)
    extra = [] if after is None else [after]

    def body(*refs):
        outs = refs[n + len(extra):2 * n + len(extra)]
        send, recv, token = refs[2 * n + len(extra):]
        sends, _ = build(outs, send, recv)
        for cp in sends:
            cp.start()
        token[...] = jnp.zeros_like(token)

    out = pl.pallas_call(
        body, name=name, in_specs=[_HBM] * n + [_ANY] * len(extra),
        out_specs=[_HBM] * n + [_SEM, _SEM, pl.BlockSpec(memory_space=pltpu.VMEM)],
        out_shape=[pltpu.HBM(a.shape, a.dtype) for a in arrays]
        + [pltpu.SemaphoreType.DMA(sem_shape), pltpu.SemaphoreType.DMA(sem_shape), jax.ShapeDtypeStruct((8, 128), F32)],
        input_output_aliases={i: i for i in range(n)},
        compiler_params=pltpu.CompilerParams(has_side_effects=_EFFECT),
    )(*[pltpu.with_memory_space_constraint(a, pltpu.HBM) for a in arrays], *extra)
    return list(out[:n]), (out[n], out[n + 1]), out[n + 2]


def _copies_wait(name, arrays, sems, after, build):
    n = len(arrays)

    def body(*refs):
        ins = refs[:n]
        send, recv = refs[n], refs[n + 1]
        sends, recvs = build(ins, send, recv)
        for cp in sends:
            cp.wait_send()
        for cp in recvs:
            cp.wait_recv()

    return list(pl.pallas_call(
        body, name=name, in_specs=[_HBM] * n + [_SEM, _SEM] + [_ANY] * len(after), out_specs=[_HBM] * n,
        out_shape=[pltpu.HBM(a.shape, a.dtype) for a in arrays],
        input_output_aliases={i: i for i in range(n)},
        compiler_params=pltpu.CompilerParams(has_side_effects=_EFFECT),
    )(*arrays, *sems, *after))


def _remote(src, dst, send_sem, recv_sem, to):
    return pltpu.make_async_remote_copy(src_ref=src, dst_ref=dst, send_sem=send_sem, recv_sem=recv_sem,
                                        device_id=to, device_id_type=MESH)


def _build_gather(refs, send, recv):
    x, y, c, chips = _place()
    q = 2 * x + y
    pairs = [(i, j, chip) for i in range(len(refs)) for j, chip in enumerate(chips)]
    sends = [_remote(refs[i].at[q, c], refs[i].at[q, c], send.at[3 * i + j], recv.at[3 * i + j], (cx, cy, c))
             for i, j, (cx, cy) in pairs]
    recvs = [_remote(refs[i].at[q, c], refs[i].at[2 * cx + cy, c], send.at[3 * i + j], recv.at[3 * i + j], (cx, cy, c))
             for i, j, (cx, cy) in pairs]
    return sends, recvs


def _build_swap(refs, send, recv):
    x, y, c, _ = _place()
    n = len(refs) // 2
    cps = [_remote(refs[i].at[:, 1 - c], refs[n + i], send.at[i], recv.at[i], (x, y, 1 - c)) for i in range(n)]
    return cps, cps


def _build_exchange(refs, send, recv):
    x, y, c, chips = _place()
    n = len(refs) // 2
    cps = [_remote(refs[i].at[2 * cx + cy], refs[n + i].at[j], send.at[3 * i + j], recv.at[3 * i + j], (cx, cy, c))
           for i in range(n) for j, (cx, cy) in enumerate(chips)]
    return cps, cps


def _build_spread(refs, send, recv):
    x, y, c, _ = _place()
    flip = lambda bit, on: bit + on - 2 * bit * on
    cps = [_remote(refs[0], refs[1].at[r - 1], send.at[r - 1], recv.at[r - 1],
                   (flip(x, r >> 2 & 1), flip(y, r >> 1 & 1), flip(c, r & 1))) for r in (1, 2, 4, 6)]
    return cps, cps


def _forward_small(landed):
    def body(in_ref, out_ref, send, recv):
        x, y, c, _ = _place()
        cps = [_remote(out_ref.at[r - 1], out_ref.at[r], send.at[k], recv.at[k], (x, y, 1 - c))
               for k, r in enumerate((2, 4, 6))]
        for cp in cps:
            cp.start()
        for cp in cps:
            cp.wait()

    return pl.pallas_call(
        body, name="forward_small",
        in_specs=[_ANY], out_specs=_ANY, out_shape=jax.ShapeDtypeStruct(landed.shape, F32),
        input_output_aliases={0: 0},
        scratch_shapes=[pltpu.SemaphoreType.DMA((3,)), pltpu.SemaphoreType.DMA((3,))],
    )(landed)


def _swap_halves(grads):
    n = len(grads)

    def body(*refs):
        ins, outs = refs[:n], refs[n:2 * n]
        send, recv = refs[2 * n:]
        x, y, c, _ = _place()
        cps = [pltpu.make_async_remote_copy(src_ref=ins[i].at[:, 1 - c], dst_ref=outs[i], send_sem=send.at[i],
                                            recv_sem=recv.at[i], device_id=(x, y, 1 - c), device_id_type=MESH)
               for i in range(n)]
        for cp in cps:
            cp.start()
        for cp in cps:
            cp.wait()

    return pl.pallas_call(
        body, name="swap_halves",
        in_specs=[_ANY] * n, out_specs=[_ANY] * n,
        out_shape=[jax.ShapeDtypeStruct((N_SHARD,) + g.shape[2:], F32) for g in grads],
        scratch_shapes=[pltpu.SemaphoreType.DMA((n,)), pltpu.SemaphoreType.DMA((n,))],
    )(*grads)


def _join_halves(pairs):
    n = len(pairs)

    def body(*refs):
        outs = refs[n:2 * n]
        send, recv = refs[2 * n:]
        x, y, c, _ = _place()
        cps = [pltpu.make_async_remote_copy(src_ref=outs[i].at[c], dst_ref=outs[i].at[c], send_sem=send.at[i],
                                            recv_sem=recv.at[i], device_id=(x, y, 1 - c), device_id_type=MESH)
               for i in range(n)]
        for cp in cps:
            cp.start()
        for i in range(n):
            cps[i].wait_send()
            pltpu.make_async_remote_copy(src_ref=outs[i].at[c], dst_ref=outs[i].at[1 - c], send_sem=send.at[i],
                                         recv_sem=recv.at[i], device_id=(x, y, 1 - c), device_id_type=MESH).wait_recv()

    return pl.pallas_call(
        body, name="join_halves",
        in_specs=[_ANY] * n, out_specs=[_ANY] * n,
        out_shape=[jax.ShapeDtypeStruct(p.shape, F32) for p in pairs],
        input_output_aliases={i: i for i in range(n)},
        scratch_shapes=[pltpu.SemaphoreType.DMA((n,)), pltpu.SemaphoreType.DMA((n,))],
    )(*pairs)


N_SPLIT = 2


def _sum_siblings(tag, grads, recvd, place):
    n = len(grads)

    def body(place_ref, *refs):
        g_refs, r_refs, sb_refs, own_refs = (refs[k * n:(k + 1) * n] for k in range(4))
        s = pl.program_id(1)
        for i in range(n):
            tot = g_refs[i][0, 0] + r_refs[i][0]
            sb_refs[i][0] = tot.astype(BF16)

            @pl.when(s == place_ref[0])
            def _():
                own_refs[i][...] = tot

    in_specs, sb_specs, own_specs, sb_shapes, own_shapes = [], [], [], [], []
    for g in grads:
        _, _, r, cdim = g.shape
        rb = r // N_SPLIT
        in_specs.append(pl.BlockSpec((1, 1, rb, cdim), lambda b, s, p: (s, p[1], b, 0)))
        sb_specs.append(pl.BlockSpec((1, rb, cdim), lambda b, s, p: (s, b, 0)))
        own_specs.append(pl.BlockSpec((rb, cdim), lambda b, s, p: (b, 0)))
        sb_shapes.append(jax.ShapeDtypeStruct((N_SHARD, r, cdim), BF16))
        own_shapes.append(jax.ShapeDtypeStruct((r, cdim), F32))
    out = pl.pallas_call(
        body, name="sum_siblings_" + tag,
        grid_spec=pltpu.PrefetchScalarGridSpec(
            num_scalar_prefetch=1, grid=(N_SPLIT, N_SHARD),
            in_specs=in_specs + sb_specs, out_specs=sb_specs + own_specs),
        out_shape=sb_shapes + own_shapes,
        compiler_params=_params(("parallel", "arbitrary")),
    )(place, *grads, *recvd)
    return out[:n], out[n:]


def _sum_chips(own, recvd, place):
    n = len(own)

    def body(place_ref, *refs):
        o_refs, r_refs, out_refs = (refs[k * n:(k + 1) * n] for k in range(3))
        for i in range(n):
            tot = o_refs[i][...]
            for j in range(3):
                tot = tot + r_refs[i][j].astype(F32)
            out_refs[i][0] = tot

    o_specs, r_specs, out_specs = [], [], []
    for o in own:
        r, cdim = o.shape
        rb = r // N_SPLIT
        o_specs.append(pl.BlockSpec((rb, cdim), lambda b, p: (b, 0)))
        r_specs.append(pl.BlockSpec((3, rb, cdim), lambda b, p: (0, b, 0)))
        out_specs.append(pl.BlockSpec((1, rb, cdim), lambda b, p: (p[1], b, 0)))
    return pl.pallas_call(
        body, name="sum_chips",
        grid_spec=pltpu.PrefetchScalarGridSpec(num_scalar_prefetch=1, grid=(N_SPLIT,),
                                               in_specs=o_specs + r_specs, out_specs=out_specs),
        out_shape=[jax.ShapeDtypeStruct((2,) + o.shape, F32) for o in own],
        compiler_params=_params(("parallel",)),
    )(place, *own, *recvd)


def _adamw_math(w, g, m, v):
    m = ADAM_B1 * m + (1.0 - ADAM_B1) * g
    v = ADAM_B2 * v + (1.0 - ADAM_B2) * (g * g)
    m_hat = m / (1.0 - ADAM_B1 ** ADAM_STEP)
    v_hat = v / (1.0 - ADAM_B2 ** ADAM_STEP)
    delta = -ADAM_LR * (m_hat / (jnp.sqrt(v_hat) + ADAM_EPS) + ADAM_WD * w)
    return delta, m, v


def _adamw(name, ws, gs, ms, vs, n_split):
    n = len(ws)

    def body(*refs):
        w_r, g_r, m_r, v_r, d_o, m_o, v_o = (refs[k * n:(k + 1) * n] for k in range(7))
        for i in range(n):
            d, m, v = _adamw_math(w_r[i][...], g_r[i][...], m_r[i][...], v_r[i][...])
            d_o[i][...] = d
            m_o[i][...] = m
            v_o[i][...] = v

    specs = [pl.BlockSpec((w.shape[0] // n_split, w.shape[1]), lambda b: (b, 0)) for w in ws]
    shapes = [jax.ShapeDtypeStruct(w.shape, F32) for w in ws]
    out = pl.pallas_call(
        body, name=name, grid=(n_split,),
        in_specs=specs * 4, out_specs=specs * 3, out_shape=shapes * 3,
        compiler_params=_params(("parallel",)),
    )(*ws, *gs, *ms, *vs)
    return out[:n], out[n:2 * n], out[2 * n:]


def _reduce_small(own, received):
    def body(own_ref, recv_ref, g_out):
        me = 4 * lax.axis_index("x") + 2 * lax.axis_index("y") + lax.axis_index("c")
        g = None
        for k in range(8):
            mine = me == k
            part = jnp.where(mine, own_ref[...], recv_ref[jnp.where(mine, 0, jnp.bitwise_xor(me, k) - 1)])
            g = part if g is None else g + part
        g_out[...] = g

    return pl.pallas_call(
        body, name="reduce_small",
        out_shape=jax.ShapeDtypeStruct(own.shape, F32),
        compiler_params=_params(),
    )(own, received)


def _adamw_small(ws, gs, ms, vs):
    n = len(ws)

    def body(*refs):
        w_r, g_r, m_r, v_r, d_o, m_o, v_o = (refs[k * n:(k + 1) * n] for k in range(7))
        for i in range(n):
            d, mm, vv = _adamw_math(w_r[i][...], g_r[i][...], m_r[i][...], v_r[i][...])
            d_o[i][...] = d
            m_o[i][...] = mm
            v_o[i][...] = vv

    out = pl.pallas_call(
        body, name="adamw_small",
        out_shape=[jax.ShapeDtypeStruct(t.shape, F32) for t in ws] * 3,
        compiler_params=_params(),
    )(*ws, *gs, *ms, *vs)
    return out[:n], out[n:2 * n], out[2 * n:]


def _s5_operands(lam_re, lam_im, log_step, b_re, b_im, c_re, c_im, glu_w):
    lr = jnp.minimum(lam_re, -1e-4)
    li = lam_im
    step = jnp.exp(log_step)[:, None]
    mag = jnp.exp(lr * step)
    ang = li * step
    abr = mag * jnp.cos(ang)
    abi = mag * jnp.sin(ang)
    nr = abr - 1.0
    ni = abi
    den = lr * lr + li * li
    cr = ((nr * lr + ni * li) / den)[..., None]
    ci = ((ni * lr - nr * li) / den)[..., None]
    bbr = cr * b_re - ci * b_im
    bbi = cr * b_im + ci * b_re
    eye = jnp.eye(8, dtype=F32)
    g, h, p = SSM_GROUPS // SUPER, SSM_GROUP, SSM_STATE

    def b_layout(t):
        return jnp.einsum("ab,japh->jahbp", eye, t.reshape(SUPER, g, p, h)).reshape(SUPER, g * h, g * p)

    def c_layout(t):
        return jnp.einsum("ab,jahp->jbpah", eye, t.reshape(SUPER, g, h, p)).reshape(SUPER, g * p, g * h)

    glu = jnp.einsum("ab,jahk->jahbk", eye, glu_w.reshape(SUPER, g, h, h)).reshape(SUPER, g * h, g * h)
    lam = _pad_rows(jnp.concatenate([abr.reshape(1, N_STATE), abi.reshape(1, N_STATE)], axis=0), 8)
    return lam, b_layout(bbr), b_layout(bbi), c_layout(c_re), c_layout(c_im), glu


def _pad_rows(a, rows):
    return jnp.pad(a, ((0, rows - a.shape[0]), (0, 0)))


def _pack(parts):
    rows = []
    for a in parts:
        flat = a.reshape(-1)
        n = -(-flat.shape[0] // 128)
        rows.append(jnp.pad(flat, (0, n * 128 - flat.shape[0])).reshape(n, 128))
    out = jnp.concatenate(rows, axis=0)
    return _pad_rows(out, -(-out.shape[0] // 8) * 8)


def _unpack(packed, like):
    out, at = [], 0
    for a in like:
        n = -(-a.size // 128)
        out.append(packed[at:at + n].reshape(-1)[:a.size].reshape(a.shape))
        at += n
    return out


STORED = {"ssm_b_re": (0, 1, 3, 2), "ssm_b_im": (0, 1, 3, 2), "ssm_d": (0, 2, 1), "ssm_glu_b": (0, 2, 1),
          "ssm_glu_w": (0, 2, 3, 1)}


def _stored(k, a):
    return a.transpose(STORED[k]) if k in STORED else a


def _logical(k, a):
    return a.transpose(tuple(STORED[k].index(i) for i in range(a.ndim))) if k in STORED else a


SMALL = ("norm1_g", "ssm_lambda_re", "ssm_lambda_im", "ssm_log_step", "ssm_b_re", "ssm_b_im", "ssm_c_re", "ssm_c_im",
         "ssm_d", "ssm_glu_w", "ssm_glu_b", "ssm_norm_g", "pool_w", "pool_scale", "pool_norm_g", "norm2_g",
         "final_norm_g")
LARGE = ("w_in", "w_out", "w_gate", "w_up", "w_down")
WEIGHTS = ("meta_tokens", "norm1_g", "w_in", "ssm_lambda_re", "ssm_lambda_im", "ssm_log_step", "ssm_b_re", "ssm_b_im",
           "ssm_c_re", "ssm_c_im", "ssm_d", "ssm_glu_w", "ssm_glu_b", "ssm_norm_g", "pool_w", "pool_scale",
           "pool_norm_g", "w_out", "norm2_g", "w_gate", "w_up", "w_down", "final_norm_g")


def _step(x, target, w, m, v):
    seq = x.shape[1]
    n_rows = N_META + seq
    n_pad, tm, tp, tc, tg = _plan(n_rows)
    xq, yq, cq = lax.axis_index("x"), lax.axis_index("y"), lax.axis_index("c")
    place = jnp.stack([2 * xq + yq, cq]).astype(jnp.int32)

    def halves(a2d):
        return a2d.reshape(2, a2d.shape[0] // 2, a2d.shape[1])

    def local2d(t):
        return {"w_gate": lambda a: a[0].T, "w_up": lambda a: a[0].T}.get(t, lambda a: a[0])

    shards = [halves(local2d(k)(w[k])) for k in LARGE] + [halves(w["meta_tokens"])]
    full = _cast_shards(shards, [BF16] * len(LARGE) + [F32], place)
    w_in_full, meta_full = _gather_shards([full[0], full[5]])
    late, gather_sems, gather_token = _copies_start("gather_start", list(full[1:5]), (12,), _build_gather,
                                                    after=w_in_full)
    w_in_b = w_in_full.reshape(D_MODEL, D_MODEL)
    meta = meta_full.reshape(N_SHARD, N_META, D_MODEL // N_SHARD).transpose(1, 0, 2).reshape(N_META, D_MODEL)

    h0 = _pad_rows(jnp.concatenate([meta, x[0]], axis=0), n_pad)
    tgt = _pad_rows(jnp.concatenate([jnp.zeros((N_META, D_MODEL), F32), target[0]], axis=0), n_pad)
    s5_in = (w["ssm_lambda_re"][0], w["ssm_lambda_im"][0], w["ssm_log_step"][0], w["ssm_b_re"][0], w["ssm_b_im"][0],
             w["ssm_c_re"][0], w["ssm_c_im"][0], w["ssm_glu_w"][0])
    (lam, bbr, bbi, crt, cit, glu), s5_vjp = jax.vjp(_s5_operands, *s5_in)
    bbr_b, bbi_b, crt_b, cit_b, glu_b16 = (t.astype(BF16) for t in (bbr, bbi, crt, cit, glu))
    s5_vecs = _pad_rows(jnp.concatenate([w["ssm_d"].reshape(1, D_SSM), w["ssm_glu_b"].reshape(1, D_SSM),
                                         w["ssm_norm_g"].reshape(1, D_SSM)], axis=0), 8)
    pool_vecs = _pad_rows(jnp.concatenate([w["pool_scale"].reshape(1, D_POOL), w["pool_norm_g"].reshape(1, D_POOL)],
                                          axis=0), 8)
    pw_b = w["pool_w"][0].astype(BF16)
    g1, g2, gf = w["norm1_g"].reshape(1, D_MODEL), w["norm2_g"].reshape(1, D_MODEL), w["final_norm_g"].reshape(1, D_MODEL)

    u, vv = _fwd_in(h0, g1, w_in_b, tp, gather_token)
    sr, si, y, ms = _s5_fwd(u, lam, bbr_b, bbi_b, crt_b, cit_b, s5_vecs, glu_b16, tc)
    feat, mp = _pool_fwd(vv, pw_b, pool_vecs, tc)
    late = _forward_halves(_copies_wait("gather_wait", late, gather_sems, [ms, mp], _build_gather))
    w_out_b = late[0].reshape(D_MODEL, D_MODEL)
    wg_b, wu_b, wd_b = (t.reshape(N_SHARD, FF_SHARD, D_MODEL) for t in late[1:])
    h1, n2, a, b, ff, dh2, dh2b, loss_acc, dgf = _fwd_ffn(h0, ms, mp, w_out_b, g2, wg_b, wu_b, wd_b, gf, tgt, tm, n_rows)

    def quarters(t):
        if t.ndim == 2:
            t = t.reshape(N_SHARD, t.shape[0] // N_SHARD, t.shape[1])
        return t.reshape(N_SHARD, 2, t.shape[1] // 2, t.shape[2])

    def landing(like, lead, dtype):
        return [lax.empty((lead,) + t.shape[2:], dtype) for t in like]

    da, db, dh1, dg2 = _bwd_ffn(dh2, dh2b, a, b, wg_b, wu_b, wd_b, h1, g2, tc)
    ffn_g = [quarters(t) for t in _grad_ffn(n2, da, db, ff, dh2b, tg)]
    nf = len(ffn_g)
    moved, swap_sems, swap_token = _copies_start("swap_start", ffn_g + landing(ffn_g, N_SHARD, F32), (nf,), _build_swap)
    dms, dmp, dwo = _bwd_out(dh1, ms, mp, w_out_b, tp, swap_token)
    moved = _copies_wait("swap_wait", moved, swap_sems, [dwo], _build_swap)
    ffn_parts, ffn_own = _sum_siblings("ffn", moved[:nf], moved[nf:], place)
    moved, exch_sems, exch_token = _copies_start("exchange_start", list(ffn_parts) + landing(ffn_g, 3, BF16), (3 * nf,),
                                                 _build_exchange)
    du, dbbr, dbbi, dcrt, dcit, dglu, ds5v, dlam = _s5_bwd(dms, y, u, sr, si, lam, bbr_b, bbi_b, crt_b, cit_b,
                                                           s5_vecs, glu_b16, tc, exch_token)
    dv, dpw, dpoolv = _pool_bwd(dmp, feat, pw_b, pool_vecs, tc)
    dh0, dwi, dg1 = _bwd_in(du, dv, h0, dh1, g1, w_in_b, tp)
    ffn_from_chips = _copies_wait("exchange_wait", moved, exch_sems, [dh0], _build_exchange)[nf:]
    dlam = _pad_rows(jnp.concatenate([jnp.sum(dlam[:SEGMENTS], axis=0, keepdims=True),
                                      jnp.sum(dlam[SEGMENTS:], axis=0, keepdims=True)], axis=0), 8)
    d_lre, d_lim, d_lstep, d_bre, d_bim, d_cre, d_cim, d_gluw = s5_vjp((dlam, dbbr, dbbi, dcrt, dcit, dglu))
    grad_x = dh0[N_META:n_rows][None]

    small_g = {
        "norm1_g": dg1, "ssm_lambda_re": d_lre, "ssm_lambda_im": d_lim, "ssm_log_step": d_lstep, "ssm_b_re": d_bre,
        "ssm_b_im": d_bim, "ssm_c_re": d_cre, "ssm_c_im": d_cim, "ssm_d": ds5v[0], "ssm_glu_w": d_gluw,
        "ssm_glu_b": ds5v[1], "ssm_norm_g": ds5v[2], "pool_w": dpw, "pool_scale": dpoolv[0], "pool_norm_g": dpoolv[1],
        "norm2_g": dg2, "final_norm_g": dgf,
    }
    like = [_stored(k, w[k]) for k in SMALL]
    packed_g = _pack([_stored(k, small_g[k].reshape(w[k].shape)) for k in SMALL] + [dh0[:N_META], loss_acc[0:1, 0:1]])

    mix_g = [quarters(t) for t in (dwi, dwo)]
    mix_parts, mix_own = _sum_siblings("mix", mix_g, _swap_halves(mix_g), place)
    moved, mix_sems, mix_token = _copies_start("mix_exchange_start", list(mix_parts) + landing(mix_g, 3, BF16),
                                               (3 * len(mix_g),), _build_exchange)
    spread, small_sems, small_token = _copies_start(
        "small_start", [packed_g, lax.empty((7,) + packed_g.shape, F32)], (7,), _build_spread, after=mix_token)
    mix_from_chips = _copies_wait("mix_exchange_wait", moved, mix_sems, [small_token], _build_exchange)[len(mix_g):]
    joined = _join_halves(_sum_chips(list(mix_own) + list(ffn_own), list(mix_from_chips) + list(ffn_from_chips), place))
    g_large = [j.reshape(j.shape[0] * j.shape[1], j.shape[2]) for j in joined]
    w2d, m2d, v2d = ([local2d(k)(t[k]) for k in LARGE] for t in (w, m, v))
    d_large, m_large, v_large = _adamw("adamw_large", w2d, g_large, m2d, v2d, 8)

    own_g, landed = _copies_wait("small_wait", spread, small_sems, [d_large[0]], _build_spread)
    g_pk = _reduce_small(own_g, _forward_small(landed))
    g_small = _unpack(g_pk, like + [jax.ShapeDtypeStruct((N_META, D_MODEL), F32), jax.ShapeDtypeStruct((1, 1), F32)])
    loss = g_small.pop()[0, 0]
    rows2d = lambda t: t.reshape(1, -1) if t.ndim == 1 else t
    d_small, m_small, v_small = _adamw_small(*([rows2d(t) for t in ts] for ts in (
        like, g_small[:-1], [_stored(k, m[k]) for k in SMALL], [_stored(k, v[k]) for k in SMALL])))
    g_small, d_small, m_small, v_small = ([_logical(k, t.reshape(a.shape)) for t, a, k in zip(ts, like, SMALL)] + ts[len(SMALL):]
                                          for ts in (g_small, list(d_small), list(m_small), list(v_small)))
    q = place[0]
    g_meta = lax.dynamic_slice_in_dim(g_small[-1], q * (D_MODEL // N_SHARD), D_MODEL // N_SHARD, axis=1)
    d_meta, m_meta, v_meta = _adamw("adamw_meta", [w["meta_tokens"]], [g_meta], [m["meta_tokens"]],
                                    [v["meta_tokens"]], 1)

    grads, deltas, new_m, new_v = {}, {}, {}, {}
    for i, k in enumerate(SMALL):
        grads[k], deltas[k], new_m[k], new_v[k] = g_small[i], d_small[i], m_small[i], v_small[i]
    for i, k in enumerate(LARGE):
        back = (lambda t: t.T[None]) if k in ("w_gate", "w_up") else (lambda t: t[None])
        grads[k], deltas[k], new_m[k], new_v[k] = (back(t) for t in (g_large[i], d_large[i], m_large[i], v_large[i]))
    grads["meta_tokens"], deltas["meta_tokens"] = g_meta, d_meta[0]
    new_m["meta_tokens"], new_v["meta_tokens"] = m_meta[0], v_meta[0]
    return (loss, grad_x, *[grads[k] for k in WEIGHTS], *[deltas[k] for k in WEIGHTS],
            *[new_m[k] for k in WEIGHTS], *[new_v[k] for k in WEIGHTS])


def kernel(x, meta_tokens, norm1_g, w_in, ssm_lambda_re, ssm_lambda_im, ssm_log_step, ssm_b_re, ssm_b_im, ssm_c_re, ssm_c_im, ssm_d, ssm_glu_w, ssm_glu_b, ssm_norm_g, pool_w, pool_scale, pool_norm_g, w_out, norm2_g, w_gate, w_up, w_down, final_norm_g, loss_target, m_meta_tokens, m_norm1_g, m_w_in, m_ssm_lambda_re, m_ssm_lambda_im, m_ssm_log_step, m_ssm_b_re, m_ssm_b_im, m_ssm_c_re, m_ssm_c_im, m_ssm_d, m_ssm_glu_w, m_ssm_glu_b, m_ssm_norm_g, m_pool_w, m_pool_scale, m_pool_norm_g, m_w_out, m_norm2_g, m_w_gate, m_w_up, m_w_down, m_final_norm_g, v_meta_tokens, v_norm1_g, v_w_in, v_ssm_lambda_re, v_ssm_lambda_im, v_ssm_log_step, v_ssm_b_re, v_ssm_b_im, v_ssm_c_re, v_ssm_c_im, v_ssm_d, v_ssm_glu_w, v_ssm_glu_b, v_ssm_norm_g, v_pool_w, v_pool_scale, v_pool_norm_g, v_w_out, v_norm2_g, v_w_gate, v_w_up, v_w_down, v_final_norm_g):
    w = dict(meta_tokens=meta_tokens, norm1_g=norm1_g, w_in=w_in, ssm_lambda_re=ssm_lambda_re, ssm_lambda_im=ssm_lambda_im, ssm_log_step=ssm_log_step, ssm_b_re=ssm_b_re, ssm_b_im=ssm_b_im, ssm_c_re=ssm_c_re, ssm_c_im=ssm_c_im, ssm_d=ssm_d, ssm_glu_w=ssm_glu_w, ssm_glu_b=ssm_glu_b, ssm_norm_g=ssm_norm_g, pool_w=pool_w, pool_scale=pool_scale, pool_norm_g=pool_norm_g, w_out=w_out, norm2_g=norm2_g, w_gate=w_gate, w_up=w_up, w_down=w_down, final_norm_g=final_norm_g)
    m = dict(meta_tokens=m_meta_tokens, norm1_g=m_norm1_g, w_in=m_w_in, ssm_lambda_re=m_ssm_lambda_re, ssm_lambda_im=m_ssm_lambda_im, ssm_log_step=m_ssm_log_step, ssm_b_re=m_ssm_b_re, ssm_b_im=m_ssm_b_im, ssm_c_re=m_ssm_c_re, ssm_c_im=m_ssm_c_im, ssm_d=m_ssm_d, ssm_glu_w=m_ssm_glu_w, ssm_glu_b=m_ssm_glu_b, ssm_norm_g=m_ssm_norm_g, pool_w=m_pool_w, pool_scale=m_pool_scale, pool_norm_g=m_pool_norm_g, w_out=m_w_out, norm2_g=m_norm2_g, w_gate=m_w_gate, w_up=m_w_up, w_down=m_w_down, final_norm_g=m_final_norm_g)
    v = dict(meta_tokens=v_meta_tokens, norm1_g=v_norm1_g, w_in=v_w_in, ssm_lambda_re=v_ssm_lambda_re, ssm_lambda_im=v_ssm_lambda_im, ssm_log_step=v_ssm_log_step, ssm_b_re=v_ssm_b_re, ssm_b_im=v_ssm_b_im, ssm_c_re=v_ssm_c_re, ssm_c_im=v_ssm_c_im, ssm_d=v_ssm_d, ssm_glu_w=v_ssm_glu_w, ssm_glu_b=v_ssm_glu_b, ssm_norm_g=v_ssm_norm_g, pool_w=v_pool_w, pool_scale=v_pool_scale, pool_norm_g=v_pool_norm_g, w_out=v_w_out, norm2_g=v_norm2_g, w_gate=v_w_gate, w_up=v_w_up, w_down=v_w_down, final_norm_g=v_final_norm_g)
    return _step(x, loss_target, w, m, v)
```

```python
import functools
import math

import jax
import jax.numpy as jnp
from jax import lax
from jax.experimental import pallas as pl
from jax.experimental.pallas import tpu as pltpu

F32 = jnp.float32
BF16 = jnp.bfloat16
MESH = pl.DeviceIdType.MESH
AXES = ("x", "y", "c")

D_MODEL = 1024
D_SSM = 512
D_POOL = 512
N_META = 16
SSM_GROUP = 16
SSM_GROUPS = 32
SSM_STATE = 64
N_STATE = SSM_GROUPS * SSM_STATE
STATE_BLOCKS = N_STATE // 128
SUPER = 4
POOL_WINDOWS = (2, 4, 8, 16)
POOL_HALO = 16
D_FF = 2816
N_SHARD = 4
FF_SHARD = D_FF // N_SHARD
EPS = 1e-6
ADAM_LR, ADAM_B1, ADAM_B2, ADAM_EPS, ADAM_WD, ADAM_STEP = 0.001, 0.9, 0.999, 1e-08, 0.01, 10
VMEM_LIMIT = 56 * 1024 * 1024


def _plan(n_rows):
    if n_rows > 2048:
        tm, tp, tc, tg = 416, 832, 320, 1040
    else:
        tm, tp, tc, tg = 128, 128, 64, 128
    step = math.lcm(tm, tp, tc, tg)
    return -(-n_rows // step) * step, tm, tp, tc, tg


def _params(sem=None):
    return pltpu.CompilerParams(dimension_semantics=sem, vmem_limit_bytes=VMEM_LIMIT)


def _dot(a, b):
    return jnp.dot(a, b, preferred_element_type=F32)


def _dot_nt(a, b):
    return lax.dot_general(a, b, (((1,), (1,)), ((), ())), preferred_element_type=F32)


def _dot_tn(a, b):
    return lax.dot_general(a, b, (((0,), (0,)), ((), ())), preferred_element_type=F32)


def _sigmoid(x):
    return 0.5 * jnp.tanh(0.5 * x) + 0.5


_GELU_C = math.sqrt(2.0 / math.pi)


def _gelu_and_grad(y):
    y2 = y * y
    t = jnp.tanh(_GELU_C * (y + 0.044715 * y * y2))
    g = 0.5 * y * (1.0 + t)
    dg = 0.5 * (1.0 + t) + 0.5 * y * (1.0 - t * t) * (_GELU_C * (1.0 + 3.0 * 0.044715 * y2))
    return g, dg


def _rms(x):
    return lax.rsqrt(jnp.mean(x * x, axis=-1, keepdims=True) + EPS)


def _rms_bwd(dn, xhat, r):
    return r * (dn - xhat * jnp.mean(dn * xhat, axis=-1, keepdims=True))


def _full(shape):
    nd = len(shape)
    return pl.BlockSpec(shape, lambda *_: (0,) * nd)


def _fwd_in(h0, g1, w_in_b, tm, token):
    n_pad = h0.shape[0]

    def body(h_ref, g_ref, w_ref, token_ref, u_ref, v_ref):
        h = h_ref[...]
        n1 = (h * _rms(h) * g_ref[...]).astype(BF16)
        proj = _dot(n1, w_ref[...])
        for i in range(4):
            u_ref[i] = proj[:, 128 * i:128 * (i + 1)]
        v_ref[...] = proj[:, D_SSM:]

    row = lambda w: pl.BlockSpec((tm, w), lambda i: (i, 0))
    return pl.pallas_call(
        body, grid=(n_pad // tm,), name="fwd_in",
        in_specs=[row(D_MODEL), _full((1, D_MODEL)), _full((D_MODEL, D_MODEL)), _ANY],
        out_specs=[pl.BlockSpec((4, tm, 128), lambda i: (0, i, 0)), row(D_POOL)],
        out_shape=[jax.ShapeDtypeStruct((4, n_pad, 128), F32), jax.ShapeDtypeStruct((n_pad, D_POOL), F32)],
        compiler_params=_params(("parallel",)),
    )(h0, g1, w_in_b, token)


def _fwd_ffn(h0, ms, mp, w_out_b, g2, wg_b, wu_b, wd_b, gf, target, tm, n_valid):
    n_pad = h0.shape[0]
    nt = n_pad // tm

    def body(h0_ref, ms_ref, mp_ref, wo_ref, g2_ref, wg_ref, wu_ref, wd_ref, gf_ref, tgt_ref,
             h1_ref, n2_ref, a_ref, b_ref, ff_ref, dh2_ref, dh2b_ref, loss_ref, dgf_ref, acc):
        i, q = pl.program_id(0), pl.program_id(1)

        @pl.when((i == 0) & (q == 0))
        def _():
            loss_ref[...] = jnp.zeros_like(loss_ref)
            dgf_ref[...] = jnp.zeros_like(dgf_ref)

        @pl.when(q == 0)
        def _():
            h1 = h0_ref[...] + _dot(ms_ref[...], wo_ref[:D_SSM, :]) + _dot(mp_ref[...], wo_ref[D_SSM:, :])
            h1_ref[...] = h1
            acc[...] = h1
            n2_ref[...] = (h1 * _rms(h1) * g2_ref[...]).astype(BF16)

        n2 = n2_ref[...]
        a = _dot_nt(n2, wg_ref[0])
        b = _dot_nt(n2, wu_ref[0])
        a_ref[0] = a.astype(BF16)
        b_ref[0] = b.astype(BF16)
        ff = (a * _sigmoid(a) * b).astype(BF16)
        ff_ref[0] = ff
        acc[...] += _dot(ff, wd_ref[0])

        @pl.when(q == N_SHARD - 1)
        def _():
            h2 = acc[...]
            r = _rms(h2)
            xhat = h2 * r
            gf_row = gf_ref[...]
            rows = i * tm + lax.broadcasted_iota(jnp.int32, (tm, 1), 0)
            valid = (rows >= N_META) & (rows < n_valid)
            diff = jnp.where(valid, xhat * gf_row - tgt_ref[...], 0.0)
            loss_ref[...] += jnp.full(loss_ref.shape, 0.5 / D_MODEL, F32) * jnp.sum(diff * diff)
            dout = diff * (1.0 / D_MODEL)
            dgf_ref[...] += jnp.sum(dout * xhat, axis=0, keepdims=True)
            dh2 = _rms_bwd(dout * gf_row, xhat, r)
            dh2_ref[...] = dh2
            dh2b_ref[...] = dh2.astype(BF16)

    row = lambda w: pl.BlockSpec((tm, w), lambda i, q: (i, 0))
    shard_rows = pl.BlockSpec((1, FF_SHARD, D_MODEL), lambda i, q: (q, 0, 0))
    act = pl.BlockSpec((1, tm, FF_SHARD), lambda i, q: (q, i, 0))
    sds = jax.ShapeDtypeStruct
    return pl.pallas_call(
        body, grid=(nt, N_SHARD), name="fwd_ffn",
        in_specs=[row(D_MODEL), row(D_SSM), row(D_POOL), _full((D_MODEL, D_MODEL)), _full((1, D_MODEL)),
                  shard_rows, shard_rows, shard_rows, _full((1, D_MODEL)), row(D_MODEL)],
        out_specs=[row(D_MODEL), row(D_MODEL), act, act, act, row(D_MODEL), row(D_MODEL), _full((8, 128)),
                   _full((1, D_MODEL))],
        out_shape=[sds((n_pad, D_MODEL), F32), sds((n_pad, D_MODEL), BF16),
                   sds((N_SHARD, n_pad, FF_SHARD), BF16), sds((N_SHARD, n_pad, FF_SHARD), BF16),
                   sds((N_SHARD, n_pad, FF_SHARD), BF16), sds((n_pad, D_MODEL), F32), sds((n_pad, D_MODEL), BF16),
                   sds((8, 128), F32), sds((1, D_MODEL), F32)],
        scratch_shapes=[pltpu.VMEM((tm, D_MODEL), F32)],
        compiler_params=_params(("arbitrary", "arbitrary")),
    )(h0, ms, mp, w_out_b, g2, wg_b, wu_b, wd_b, gf, target)


def _bwd_ffn(dh2, dh2b, a, b, wg_b, wu_b, wd_b, h1, g2, tm):
    n_pad = dh2.shape[0]

    def body(dh2_ref, dh2b_ref, a_ref, b_ref, wg_hbm, wu_hbm, wd_hbm, h1_ref, g2_ref, da_ref, db_ref, dh1_ref, dg2_ref,
             wg_ref, wu_ref, wd_ref, sems, acc):
        @pl.when(pl.program_id(0) == 0)
        def _():
            dg2_ref[...] = jnp.zeros_like(dg2_ref)
            cps = [pltpu.make_async_copy(src, dst, sems.at[k])
                   for k, (src, dst) in enumerate(((wg_hbm, wg_ref), (wu_hbm, wu_ref), (wd_hbm, wd_ref)))]
            for cp in cps:
                cp.start()
            for cp in cps:
                cp.wait()

        dh2b = dh2b_ref[...]
        for q in range(N_SHARD):
            dff = _dot_nt(dh2b, wd_ref[q])
            a_v, b_v = a_ref[q].astype(F32), b_ref[q].astype(F32)
            sig = _sigmoid(a_v)
            silu = a_v * sig
            da = (dff * b_v * (sig + silu * (1.0 - sig))).astype(BF16)
            db = (dff * silu).astype(BF16)
            da_ref[q] = da
            db_ref[q] = db
            part = _dot(da, wg_ref[q]) + _dot(db, wu_ref[q])
            if q == 0:
                acc[...] = part
            else:
                acc[...] += part

        h1 = h1_ref[...]
        r = _rms(h1)
        xhat = h1 * r
        dn2 = acc[...]
        dg2_ref[...] += jnp.sum(dn2 * xhat, axis=0, keepdims=True)
        dh1_ref[...] = dh2_ref[...] + _rms_bwd(dn2 * g2_ref[...], xhat, r)

    row = lambda w: pl.BlockSpec((tm, w), lambda i: (i, 0))
    act = pl.BlockSpec((N_SHARD, tm, FF_SHARD), lambda i: (0, i, 0))
    sds = jax.ShapeDtypeStruct
    return pl.pallas_call(
        body, grid=(n_pad // tm,), name="bwd_ffn",
        in_specs=[row(D_MODEL), row(D_MODEL), act, act, _ANY, _ANY, _ANY, row(D_MODEL), _full((1, D_MODEL))],
        out_specs=[act, act, row(D_MODEL), _full((1, D_MODEL))],
        out_shape=[sds((N_SHARD, n_pad, FF_SHARD), BF16), sds((N_SHARD, n_pad, FF_SHARD), BF16),
                   sds((n_pad, D_MODEL), F32), sds((1, D_MODEL), F32)],
        scratch_shapes=[pltpu.VMEM(wg_b.shape, BF16), pltpu.VMEM(wu_b.shape, BF16), pltpu.VMEM(wd_b.shape, BF16),
                        pltpu.SemaphoreType.DMA((3,)), pltpu.VMEM((tm, D_MODEL), F32)],
        compiler_params=_params(("arbitrary",)),
    )(dh2, dh2b, a, b, wg_b, wu_b, wd_b, h1, g2)


def _grad_ffn(n2, da, db, ff, dh2b, tm):
    n_pad = n2.shape[0]

    def body(n2_ref, da_ref, db_ref, ff_ref, dh2_ref, dwg_ref, dwu_ref, dwd_ref):
        i = pl.program_id(1)
        n2_v = n2_ref[...]
        gg = _dot_tn(da_ref[0], n2_v)
        gu = _dot_tn(db_ref[0], n2_v)
        gd = _dot_tn(ff_ref[0], dh2_ref[...])

        @pl.when(i == 0)
        def _():
            dwg_ref[0] = gg
            dwu_ref[0] = gu
            dwd_ref[0] = gd

        @pl.when(i > 0)
        def _():
            dwg_ref[0] += gg
            dwu_ref[0] += gu
            dwd_ref[0] += gd

    row = lambda w: pl.BlockSpec((tm, w), lambda q, i: (i, 0))
    act = pl.BlockSpec((1, tm, FF_SHARD), lambda q, i: (q, i, 0))
    sds = jax.ShapeDtypeStruct
    return pl.pallas_call(
        body, grid=(N_SHARD, n_pad // tm), name="grad_ffn",
        in_specs=[row(D_MODEL), act, act, act, row(D_MODEL)],
        out_specs=[pl.BlockSpec((1, FF_SHARD, D_MODEL), lambda q, i: (q, 0, 0))] * 3,
        out_shape=[sds((N_SHARD, FF_SHARD, D_MODEL), F32)] * 3,
        compiler_params=_params(("parallel", "arbitrary")),
    )(n2, da, db, ff, dh2b)


def _bwd_out(dh1, ms, mp, w_out_b, tm, token):
    n_pad = dh1.shape[0]

    def body(dh1_ref, ms_ref, mp_ref, wo_ref, token_ref, dms_ref, dmp_ref, dwo_ref):
        i = pl.program_id(0)

        @pl.when(i == 0)
        def _():
            dwo_ref[...] = jnp.zeros_like(dwo_ref)

        d = dh1_ref[...].astype(BF16)
        dms = _dot_nt(d, wo_ref[:D_SSM, :])
        for k in range(4):
            dms_ref[k] = dms[:, 128 * k:128 * (k + 1)]
        dmp_ref[...] = _dot_nt(d, wo_ref[D_SSM:, :])
        dwo_ref[:D_SSM, :] += _dot_tn(ms_ref[...], d)
        dwo_ref[D_SSM:, :] += _dot_tn(mp_ref[...], d)

    row = lambda w: pl.BlockSpec((tm, w), lambda i: (i, 0))
    sds = jax.ShapeDtypeStruct
    return pl.pallas_call(
        body, grid=(n_pad // tm,), name="bwd_out",
        in_specs=[row(D_MODEL), row(D_SSM), row(D_POOL), _full((D_MODEL, D_MODEL)), _ANY],
        out_specs=[pl.BlockSpec((4, tm, 128), lambda i: (0, i, 0)), row(D_POOL), _full((D_MODEL, D_MODEL))],
        out_shape=[sds((4, n_pad, 128), F32), sds((n_pad, D_POOL), F32), sds((D_MODEL, D_MODEL), F32)],
        compiler_params=_params(("arbitrary",)),
    )(dh1, ms, mp, w_out_b, token)


def _bwd_in(du, dv, h0, dh1, g1, w_in_b, tm):
    n_pad = h0.shape[0]

    def body(du_ref, dv_ref, h0_ref, dh1_ref, g1_ref, w_ref, dh0_ref, dwi_ref, dg1_ref):
        i = pl.program_id(0)

        @pl.when(i == 0)
        def _():
            dwi_ref[...] = jnp.zeros_like(dwi_ref)
            dg1_ref[...] = jnp.zeros_like(dg1_ref)

        dub = du_ref[...].astype(BF16)
        dvb = dv_ref[...].astype(BF16)
        dn1 = _dot_nt(dub, w_ref[:, :D_SSM]) + _dot_nt(dvb, w_ref[:, D_SSM:])
        h = h0_ref[...]
        r = _rms(h)
        xhat = h * r
        g_row = g1_ref[...]
        n1 = (xhat * g_row).astype(BF16)
        dwi_ref[:, :D_SSM] += _dot_tn(n1, dub)
        dwi_ref[:, D_SSM:] += _dot_tn(n1, dvb)
        dg1_ref[...] += jnp.sum(dn1 * xhat, axis=0, keepdims=True)
        dh0_ref[...] = dh1_ref[...] + _rms_bwd(dn1 * g_row, xhat, r)

    row = lambda w: pl.BlockSpec((tm, w), lambda i: (i, 0))
    sds = jax.ShapeDtypeStruct
    return pl.pallas_call(
        body, grid=(n_pad // tm,), name="bwd_in",
        in_specs=[row(D_SSM), row(D_POOL), row(D_MODEL), row(D_MODEL), _full((1, D_MODEL)), _full((D_MODEL, D_MODEL))],
        out_specs=[row(D_MODEL), _full((D_MODEL, D_MODEL)), _full((1, D_MODEL))],
        out_shape=[sds((n_pad, D_MODEL), F32), sds((D_MODEL, D_MODEL), F32), sds((1, D_MODEL), F32)],
        compiler_params=_params(("arbitrary",)),
    )(du, dv, h0, dh1, g1, w_in_b)


SEGMENTS = 8


def _interleaved(ref, seg):
    return jnp.concatenate(
        [jnp.concatenate([ref[i, pl.ds(j, SEGMENTS, stride=seg), :] for i in range(4)], axis=1) for j in range(seg)],
        axis=0)


def _time_order(scratch, val, seg):
    for i in range(4):
        scratch[i] = val[:, 128 * i:128 * (i + 1)]
    tiles = []
    for m in range(val.shape[0] // 8):
        s, j0 = divmod(8 * m, seg)
        tiles.append(jnp.concatenate(
            [scratch[i, pl.ds(8 * j0 + s, 8, stride=SEGMENTS), :] for i in range(4)], axis=1))
    return jnp.concatenate(tiles, axis=0)


def _power_table(lam_ref, pw_r, pw_i, seg):
    a_r = jnp.broadcast_to(lam_ref[0:1, :], (SEGMENTS, N_STATE))
    a_i = jnp.broadcast_to(lam_ref[1:2, :], (SEGMENTS, N_STATE))
    p_r, p_i = a_r, a_i
    for k in range(seg):
        pw_r[SEGMENTS * k:SEGMENTS * (k + 1), :] = p_r
        pw_i[SEGMENTS * k:SEGMENTS * (k + 1), :] = p_i
        p_r, p_i = p_r * a_r - p_i * a_i, p_r * a_i + p_i * a_r


def _segment_scan(xr_ref, xi_ref, cols, pw_r, pw_i, hr_s, hi_s, seg, reverse):
    sign = -1.0 if reverse else 1.0
    a_r, a_i = pw_r[0:SEGMENTS, cols], sign * pw_i[0:SEGMENTS, cols]

    def step(n, carry):
        hr, hi = carry
        o = pl.multiple_of((seg - 1 - n if reverse else n) * SEGMENTS, SEGMENTS)
        nr = a_r * hr - a_i * hi + xr_ref[pl.ds(o, SEGMENTS), cols]
        ni = a_r * hi + a_i * hr + xi_ref[pl.ds(o, SEGMENTS), cols]
        xr_ref[pl.ds(o, SEGMENTS), cols] = nr
        xi_ref[pl.ds(o, SEGMENTS), cols] = ni
        return nr, ni

    zero = jnp.zeros((SEGMENTS, cols.stop - cols.start), F32)
    e_r, e_i = lax.fori_loop(0, seg, step, (zero, zero), unroll=2)

    top = SEGMENTS * (seg - 1)
    ls_r, ls_i = pw_r[top:top + 1, cols], sign * pw_i[top:top + 1, cols]
    c_r, c_i = hr_s[0:1, cols], hi_s[0:1, cols]
    in_r, in_i = [None] * SEGMENTS, [None] * SEGMENTS
    for s in (range(SEGMENTS - 1, -1, -1) if reverse else range(SEGMENTS)):
        in_r[s], in_i[s] = c_r, c_i
        c_r, c_i = (e_r[s:s + 1, :] + ls_r * c_r - ls_i * c_i, e_i[s:s + 1, :] + ls_r * c_i + ls_i * c_r)
    hr_s[0:1, cols] = c_r
    hi_s[0:1, cols] = c_i
    cm_r, cm_i = jnp.concatenate(in_r, axis=0), jnp.concatenate(in_i, axis=0)

    def fix(jj, _):
        o = pl.multiple_of(jj * SEGMENTS, SEGMENTS)
        k = pl.multiple_of((seg - 1 - jj if reverse else jj) * SEGMENTS, SEGMENTS)
        p_r, p_i = pw_r[pl.ds(k, SEGMENTS), cols], sign * pw_i[pl.ds(k, SEGMENTS), cols]
        xr_ref[pl.ds(o, SEGMENTS), cols] += p_r * cm_r - p_i * cm_i
        xi_ref[pl.ds(o, SEGMENTS), cols] += p_r * cm_i + p_i * cm_r
        return 0

    lax.fori_loop(0, seg, fix, 0, unroll=2)


def _s5_tail(y, glu_ref, glub):
    g, dgelu = _gelu_and_grad(y)
    gb = g.astype(BF16)
    gate = jnp.concatenate([_dot(gb[:, 128 * j:128 * (j + 1)], glu_ref[j]) for j in range(SUPER)], axis=1) + glub
    sig = _sigmoid(gate)
    return g, gb, dgelu, sig, g * sig


def _s5_fwd(u4, lam, bbr, bbi, crt, cit, vecs, glu, tc):
    n_pad = u4.shape[1]
    seg = tc // SEGMENTS

    def body(u_ref, lam_ref, bbr_ref, bbi_ref, crt_ref, cit_ref, vec_ref, glu_ref,
             sr_ref, si_ref, y_ref, ms_ref, hr_s, hi_s, pw_r, pw_i, lanes):
        @pl.when(pl.program_id(0) == 0)
        def _():
            hr_s[...] = jnp.zeros_like(hr_s)
            hi_s[...] = jnp.zeros_like(hi_s)
            _power_table(lam_ref, pw_r, pw_i, seg)

        u_v = _interleaved(u_ref, seg)
        ub = u_v.astype(BF16)
        for j in range(SUPER):
            uj = ub[:, 128 * j:128 * (j + 1)]
            sr_ref[:, 512 * j:512 * (j + 1)] = _dot(uj, bbr_ref[j])
            si_ref[:, 512 * j:512 * (j + 1)] = _dot(uj, bbi_ref[j])
        for j in range(SUPER):
            _segment_scan(sr_ref, si_ref, slice(512 * j, 512 * (j + 1)), pw_r, pw_i, hr_s, hi_s, seg, False)

        d_row, glub, gs = vec_ref[0:1, :], vec_ref[1:2, :], vec_ref[2:3, :]
        ys_c = []
        for j in range(SUPER):
            sr_j = sr_ref[:, 512 * j:512 * (j + 1)].astype(BF16)
            si_j = si_ref[:, 512 * j:512 * (j + 1)].astype(BF16)
            ys_c.append(_dot(sr_j, crt_ref[j]) - _dot(si_j, cit_ref[j]))
        y = jnp.concatenate(ys_c, axis=1) + d_row * u_v
        y_ref[...] = y
        _, _, _, _, ys = _s5_tail(y, glu_ref, glub)
        ms_ref[...] = _time_order(lanes, ys * _rms(ys) * gs, seg).astype(BF16)

    chunk = lambda w: pl.BlockSpec((tc, w), lambda c: (c, 0))
    lane_blocks = pl.BlockSpec((4, tc, 128), lambda c: (0, c, 0))
    sds = jax.ShapeDtypeStruct
    return pl.pallas_call(
        body, grid=(n_pad // tc,), name="s5_fwd",
        in_specs=[lane_blocks, _full((8, N_STATE)), _full((SUPER, 128, 512)), _full((SUPER, 128, 512)),
                  _full((SUPER, 512, 128)), _full((SUPER, 512, 128)), _full((8, D_SSM)), _full((SUPER, 128, 128))],
        out_specs=[chunk(N_STATE), chunk(N_STATE), chunk(D_SSM), chunk(D_SSM)],
        out_shape=[sds((n_pad, N_STATE), F32), sds((n_pad, N_STATE), F32),
                   sds((n_pad, D_SSM), F32), sds((n_pad, D_SSM), BF16)],
        scratch_shapes=[pltpu.VMEM((8, N_STATE), F32), pltpu.VMEM((8, N_STATE), F32),
                        pltpu.VMEM((tc, N_STATE), F32), pltpu.VMEM((tc, N_STATE), F32),
                        pltpu.VMEM((4, tc, 128), F32)],
        compiler_params=_params(("arbitrary",)),
    )(u4, lam, bbr, bbi, crt, cit, vecs, glu)


def _s5_bwd(dms4, y, u4, sr, si, lam, bbr, bbi, crt, cit, vecs, glu, tc, token):
    n_pad = u4.shape[1]
    nc = n_pad // tc
    seg = tc // SEGMENTS

    def body(dms_ref, y_ref, u_ref, sr_ref, si_ref, pr_ref, pi_ref, lam_ref, bbr_ref, bbi_ref, crt_ref, cit_ref,
             vec_ref, glu_ref, token_ref, du_ref, dbbr_ref, dbbi_ref, dcrt_ref, dcit_ref, dglu_ref, dvec_ref, dlam_ref,
             qr_s, qi_s, cr_s, ci_s, pw_r, pw_i, lanes):
        c = pl.program_id(0)

        @pl.when(c == 0)
        def _():
            for ref in (dbbr_ref, dbbi_ref, dcrt_ref, dcit_ref, dglu_ref, dvec_ref, dlam_ref, cr_s, ci_s):
                ref[...] = jnp.zeros_like(ref)
            _power_table(lam_ref, pw_r, pw_i, seg)

        d_row, glub, gs = vec_ref[0:1, :], vec_ref[1:2, :], vec_ref[2:3, :]
        y_v, u_v = y_ref[...], _interleaved(u_ref, seg)
        ub = u_v.astype(BF16)
        g, gb, dgelu, sig, ys = _s5_tail(y_v, glu_ref, glub)
        r = _rms(ys)
        xhat = ys * r
        dm = _interleaved(dms_ref, seg)
        dys = _rms_bwd(dm * gs, xhat, r)
        dgate = dys * g * sig * (1.0 - sig)
        dgateb = dgate.astype(BF16)
        dg = dys * sig + jnp.concatenate(
            [_dot_nt(dgateb[:, 128 * j:128 * (j + 1)], glu_ref[j]) for j in range(SUPER)], axis=1)
        dy = dg * dgelu
        dyb = dy.astype(BF16)
        dvec_ref[0:1, :] += jnp.sum(dy * u_v, axis=0, keepdims=True)
        dvec_ref[1:2, :] += jnp.sum(dgate, axis=0, keepdims=True)
        dvec_ref[2:3, :] += jnp.sum(dm * xhat, axis=0, keepdims=True)

        for j in range(SUPER):
            cols, states = slice(128 * j, 128 * (j + 1)), slice(512 * j, 512 * (j + 1))
            dglu_ref[j] += _dot_tn(gb[:, cols], dgateb[:, cols])
            dcrt_ref[j] += _dot_tn(sr_ref[:, states].astype(BF16), dyb[:, cols])
            dcit_ref[j] -= _dot_tn(si_ref[:, states].astype(BF16), dyb[:, cols])
            qr_s[:, states] = _dot_nt(dyb[:, cols], crt_ref[j])
            qi_s[:, states] = -_dot_nt(dyb[:, cols], cit_ref[j])

        first = c == nc - 1
        row0 = lax.broadcasted_iota(jnp.int32, (SEGMENTS, 1), 0) == 0
        last = (seg - 1) * SEGMENTS
        for j in range(SUPER):
            states = slice(512 * j, 512 * (j + 1))
            _segment_scan(qr_s, qi_s, states, pw_r, pw_i, cr_s, ci_s, seg, True)

            before_r = jnp.where(first, 0.0, pltpu.roll(pr_ref[:, states], 1, 0))
            before_i = jnp.where(first, 0.0, pltpu.roll(pi_ref[:, states], 1, 0))
            hp_r = jnp.where(row0, before_r, pltpu.roll(sr_ref[pl.ds(last, SEGMENTS), states], 1, 0))
            hp_i = jnp.where(row0, before_i, pltpu.roll(si_ref[pl.ds(last, SEGMENTS), states], 1, 0))
            q_r, q_i = qr_s[pl.ds(0, SEGMENTS), states], qi_s[pl.ds(0, SEGMENTS), states]

            def dlam_step(jj, acc):
                o = pl.multiple_of(jj * SEGMENTS, SEGMENTS)
                above = pl.multiple_of((jj - 1) * SEGMENTS, SEGMENTS)
                h_r, h_i = sr_ref[pl.ds(above, SEGMENTS), states], si_ref[pl.ds(above, SEGMENTS), states]
                t_r, t_i = qr_s[pl.ds(o, SEGMENTS), states], qi_s[pl.ds(o, SEGMENTS), states]
                return acc[0] + t_r * h_r + t_i * h_i, acc[1] + t_i * h_r - t_r * h_i

            acc = lax.fori_loop(1, seg, dlam_step, (q_r * hp_r + q_i * hp_i, q_i * hp_r - q_r * hp_i), unroll=2)
            dlam_ref[0:SEGMENTS, states] += acc[0]
            dlam_ref[SEGMENTS:, states] += acc[1]

        du_c = []
        for j in range(SUPER):
            cols, states = slice(128 * j, 128 * (j + 1)), slice(512 * j, 512 * (j + 1))
            qr_j = qr_s[:, states].astype(BF16)
            qi_j = qi_s[:, states].astype(BF16)
            du_c.append(_dot_nt(qr_j, bbr_ref[j]) + _dot_nt(qi_j, bbi_ref[j]))
            dbbr_ref[j] += _dot_tn(ub[:, cols], qr_j)
            dbbi_ref[j] += _dot_tn(ub[:, cols], qi_j)
        du_ref[...] = _time_order(lanes, jnp.concatenate(du_c, axis=1) + dy * d_row, seg)

    rev = lambda c: nc - 1 - c
    chunk = lambda w: pl.BlockSpec((tc, w), lambda c: (rev(c), 0))
    lane_blocks = pl.BlockSpec((4, tc, 128), lambda c: (0, rev(c), 0))
    prev = pl.BlockSpec((SEGMENTS, N_STATE), lambda c: (jnp.maximum(rev(c) * seg - 1, 0), 0))
    sds = jax.ShapeDtypeStruct
    return pl.pallas_call(
        body, grid=(nc,), name="s5_bwd",
        in_specs=[lane_blocks, chunk(D_SSM), lane_blocks, chunk(N_STATE), chunk(N_STATE), prev, prev,
                  _full((8, N_STATE)), _full((SUPER, 128, 512)), _full((SUPER, 128, 512)),
                  _full((SUPER, 512, 128)), _full((SUPER, 512, 128)), _full((8, D_SSM)), _full((SUPER, 128, 128)), _ANY],
        out_specs=[chunk(D_SSM), _full((SUPER, 128, 512)), _full((SUPER, 128, 512)), _full((SUPER, 512, 128)),
                   _full((SUPER, 512, 128)), _full((SUPER, 128, 128)), _full((8, D_SSM)), _full((2 * SEGMENTS, N_STATE))],
        out_shape=[sds((n_pad, D_SSM), F32), sds((SUPER, 128, 512), F32), sds((SUPER, 128, 512), F32),
                   sds((SUPER, 512, 128), F32), sds((SUPER, 512, 128), F32), sds((SUPER, 128, 128), F32),
                   sds((8, D_SSM), F32), sds((2 * SEGMENTS, N_STATE), F32)],
        scratch_shapes=[pltpu.VMEM((tc, N_STATE), F32), pltpu.VMEM((tc, N_STATE), F32),
                        pltpu.VMEM((8, N_STATE), F32), pltpu.VMEM((8, N_STATE), F32),
                        pltpu.VMEM((tc, N_STATE), F32), pltpu.VMEM((tc, N_STATE), F32),
                        pltpu.VMEM((4, tc, 128), F32)],
        compiler_params=_params(("arbitrary",)),
    )(dms4, y, u4, sr, si, sr, si, lam, bbr, bbi, crt, cit, vecs, glu, token)


def _inv_count(c_idx, tc, w):
    t = c_idx * tc + lax.broadcasted_iota(jnp.int32, (tc, 1), 0)
    return 1.0 / jnp.minimum(t + 1, w).astype(F32)


def _pool_fwd(v, pw_b, vecs, tc, token):
    n_pad = v.shape[0]

    def body(v_ref, pw_ref, vec_ref, token_ref, feat_ref, mp_ref, hist):
        c = pl.program_id(0)

        @pl.when(c == 0)
        def _():
            hist[...] = jnp.zeros_like(hist)

        v_v = v_ref[...]
        ext = jnp.concatenate([hist[...], v_v], axis=0)
        hist[...] = v_v[tc - POOL_HALO:, :]
        feats, ps = [], []
        for k, w in enumerate(POOL_WINDOWS):
            cols = slice(128 * k, 128 * (k + 1))
            s = ext[:, cols]
            sh = 1
            while sh < w:
                s = s + pltpu.roll(s, sh, 0)
                sh *= 2
            f = (s[POOL_HALO:, :] * _inv_count(c, tc, w) - v_v[:, cols]).astype(BF16)
            feats.append(f)
            ps.append(_dot(f, pw_ref[k]))
        feat_ref[...] = jnp.concatenate(feats, axis=1)
        yp = jnp.concatenate(ps, axis=1) * vec_ref[0:1, :]
        mp_ref[...] = (yp * _rms(yp) * vec_ref[1:2, :]).astype(BF16)

    chunk = lambda w: pl.BlockSpec((tc, w), lambda c: (c, 0))
    sds = jax.ShapeDtypeStruct
    return pl.pallas_call(
        body, grid=(n_pad // tc,), name="pool_fwd",
        in_specs=[chunk(D_POOL), _full((4, 128, 128)), _full((8, D_POOL)), _ANY],
        out_specs=[chunk(D_POOL), chunk(D_POOL)],
        out_shape=[sds((n_pad, D_POOL), BF16), sds((n_pad, D_POOL), BF16)],
        scratch_shapes=[pltpu.VMEM((POOL_HALO, D_POOL), F32)],
        compiler_params=_params(("arbitrary",)),
    )(v, pw_b, vecs, token)


def _pool_bwd(dmp, feat, pw_b, vecs, tc):
    n_pad = dmp.shape[0]
    nc = n_pad // tc

    def body(dmp_ref, feat_ref, pw_ref, vec_ref, dv_ref, dpw_ref, dvec_ref, fut):
        c = pl.program_id(0)

        @pl.when(c == 0)
        def _():
            fut[...] = jnp.zeros_like(fut)
            dpw_ref[...] = jnp.zeros_like(dpw_ref)
            dvec_ref[...] = jnp.zeros_like(dvec_ref)

        scale, gp = vec_ref[0:1, :], vec_ref[1:2, :]
        feat_v = feat_ref[...]
        p = jnp.concatenate([_dot(feat_v[:, 128 * k:128 * (k + 1)], pw_ref[k]) for k in range(4)], axis=1)
        yp = p * scale
        r = _rms(yp)
        xhat = yp * r
        dm = dmp_ref[...]
        dyp = _rms_bwd(dm * gp, xhat, r)
        dvec_ref[0:1, :] += jnp.sum(dyp * p, axis=0, keepdims=True)
        dvec_ref[1:2, :] += jnp.sum(dm * xhat, axis=0, keepdims=True)
        dpb = (dyp * scale).astype(BF16)
        es, dfs = [], []
        for k, w in enumerate(POOL_WINDOWS):
            cols = slice(128 * k, 128 * (k + 1))
            dpw_ref[k] += _dot_tn(feat_v[:, cols], dpb[:, cols])
            df = _dot_nt(dpb[:, cols], pw_ref[k])
            dfs.append(df)
            es.append(df * _inv_count(nc - 1 - c, tc, w))
        e = jnp.concatenate(es, axis=1)
        ext = jnp.concatenate([e, fut[...]], axis=0)
        fut[...] = e[:POOL_HALO, :]
        n_ext = tc + POOL_HALO
        dvs = []
        for k, w in enumerate(POOL_WINDOWS):
            s = ext[:, 128 * k:128 * (k + 1)]
            sh = 1
            while sh < w:
                s = s + pltpu.roll(s, n_ext - sh, 0)
                sh *= 2
            dvs.append(s[:tc, :] - dfs[k])
        dv_ref[...] = jnp.concatenate(dvs, axis=1)

    chunk = lambda w: pl.BlockSpec((tc, w), lambda c: (nc - 1 - c, 0))
    sds = jax.ShapeDtypeStruct
    return pl.pallas_call(
        body, grid=(nc,), name="pool_bwd",
        in_specs=[chunk(D_POOL), chunk(D_POOL), _full((4, 128, 128)), _full((8, D_POOL))],
        out_specs=[chunk(D_POOL), _full((4, 128, 128)), _full((8, D_POOL))],
        out_shape=[sds((n_pad, D_POOL), F32), sds((4, 128, 128), F32), sds((8, D_POOL), F32)],
        scratch_shapes=[pltpu.VMEM((POOL_HALO, D_POOL), F32)],
        compiler_params=_params(("arbitrary",)),
    )(dmp, feat, pw_b, vecs)


def _place():
    x, y, c = lax.axis_index("x"), lax.axis_index("y"), lax.axis_index("c")
    chips = [(1 - x, y), (x, 1 - y), (1 - x, 1 - y)]
    return x, y, c, chips


_ANY = pl.BlockSpec(memory_space=pl.ANY)


def _cast_shards(shards, dtypes, place):
    n = len(shards)

    def body(place_ref, *refs):
        for i in range(n):
            refs[n + i][0] = refs[i][...].astype(dtypes[i])

    return pl.pallas_call(
        body, name="cast_shards",
        grid_spec=pltpu.PrefetchScalarGridSpec(
            num_scalar_prefetch=1, grid=(1,),
            in_specs=[pl.BlockSpec(s.shape, lambda i, p: (0, 0, 0)) for s in shards],
            out_specs=[pl.BlockSpec((1,) + s.shape, lambda i, p: (p[0], 0, 0, 0)) for s in shards]),
        out_shape=[jax.ShapeDtypeStruct((N_SHARD,) + s.shape, dt) for s, dt in zip(shards, dtypes)],
        compiler_params=_params(("arbitrary",)),
    )(place, *shards)


def _gather_shards(full):
    n = len(full)

    def body(*refs):
        outs = refs[n:2 * n]
        ici_send, ici_recv, d2d_send, d2d_recv = refs[2 * n:]
        x, y, c, chips = _place()
        q = 2 * x + y
        sibling = (x, y, 1 - c)

        def ici(i, j, shard, to):
            return pltpu.make_async_remote_copy(src_ref=outs[i].at[q, c], dst_ref=outs[i].at[shard, c],
                                                send_sem=ici_send.at[i, j], recv_sem=ici_recv.at[i, j],
                                                device_id=to, device_id_type=MESH)

        def d2d(i, j, shard, half):
            return pltpu.make_async_remote_copy(src_ref=outs[i].at[shard, c], dst_ref=outs[i].at[shard, half],
                                                send_sem=d2d_send.at[i, j], recv_sem=d2d_recv.at[i, j],
                                                device_id=sibling, device_id_type=MESH)

        sends = [ici(i, j, q, (*chip, c)) for i in range(n) for j, chip in enumerate(chips)]
        for cp in sends:
            cp.start()
        passed = []
        for i in range(n):
            for j, (cx, cy) in enumerate(chips):
                ici(i, j, 2 * cx + cy, (cx, cy, c)).wait_recv()
                cp = d2d(i, j, 2 * cx + cy, c)
                cp.start()
                passed.append(cp)
        for i in range(n):
            for j, (cx, cy) in enumerate(chips):
                d2d(i, j, 2 * cx + cy, 1 - c).wait_recv()
        for cp in sends + passed:
            cp.wait_send()

    return pl.pallas_call(
        body, name="gather_shards",
        in_specs=[_ANY] * n, out_specs=[_ANY] * n,
        out_shape=[jax.ShapeDtypeStruct(f.shape, f.dtype) for f in full],
        input_output_aliases={i: i for i in range(n)},
        scratch_shapes=[pltpu.SemaphoreType.DMA((n, 3)), pltpu.SemaphoreType.DMA((n, 3)),
                        pltpu.SemaphoreType.DMA((n, 3)), pltpu.SemaphoreType.DMA((n, 3))],
    )(*full)


_HBM = pl.BlockSpec(memory_space=pltpu.HBM)
_SEM = pl.BlockSpec(memory_space=pltpu.SEMAPHORE)
_EFFECT = pltpu.SideEffectType.DATAFLOW_SIDE_EFFECTING


def _copies_start(name, arrays, sem_shape, build, after=None):
    n = len(arrays)
    extra = [] if after is None else [after]

    def body(*refs):
        outs = refs[n + len(extra):2 * n + len(extra)]
        send, recv, token = refs[2 * n + len(extra):]
        sends, _ = build(outs, send, recv)
        for cp in sends:
            cp.start()
        token[...] = jnp.zeros_like(token)

    out = pl.pallas_call(
        body, name=name, in_specs=[_HBM] * n + [_ANY] * len(extra),
        out_specs=[_HBM] * n + [_SEM, _SEM, pl.BlockSpec(memory_space=pltpu.VMEM)],
        out_shape=[pltpu.HBM(a.shape, a.dtype) for a in arrays]
        + [pltpu.SemaphoreType.DMA(sem_shape), pltpu.SemaphoreType.DMA(sem_shape), jax.ShapeDtypeStruct((8, 128), F32)],
        input_output_aliases={i: i for i in range(n)},
        compiler_params=pltpu.CompilerParams(has_side_effects=_EFFECT),
    )(*[pltpu.with_memory_space_constraint(a, pltpu.HBM) for a in arrays], *extra)
    return list(out[:n]), (out[n], out[n + 1]), out[n + 2]


def _copies_wait(name, arrays, sems, after, build):
    n = len(arrays)

    def body(*refs):
        ins = refs[:n]
        send, recv = refs[n], refs[n + 1]
        sends, recvs = build(ins, send, recv)
        for cp in sends:
            cp.wait_send()
        for cp in recvs:
            cp.wait_recv()

    return list(pl.pallas_call(
        body, name=name, in_specs=[_HBM] * n + [_SEM, _SEM] + [_ANY] * len(after), out_specs=[_HBM] * n,
        out_shape=[pltpu.HBM(a.shape, a.dtype) for a in arrays],
        input_output_aliases={i: i for i in range(n)},
        compiler_params=pltpu.CompilerParams(has_side_effects=_EFFECT),
    )(*arrays, *sems, *after))


def _remote(src, dst, send_sem, recv_sem, to):
    return pltpu.make_async_remote_copy(src_ref=src, dst_ref=dst, send_sem=send_sem, recv_sem=recv_sem,
                                        device_id=to, device_id_type=MESH)


def _build_gather(refs, send, recv):
    x, y, c, chips = _place()
    q = 2 * x + y
    pairs = [(i, j, chip) for i in range(len(refs)) for j, chip in enumerate(chips)]
    sends = [_remote(refs[i].at[q, c], refs[i].at[q, c], send.at[3 * i + j], recv.at[3 * i + j], (cx, cy, c))
             for i, j, (cx, cy) in pairs]
    recvs = [_remote(refs[i].at[q, c], refs[i].at[2 * cx + cy, c], send.at[3 * i + j], recv.at[3 * i + j], (cx, cy, c))
             for i, j, (cx, cy) in pairs]
    return sends, recvs


def _build_forward(refs, send, recv):
    x, y, c, chips = _place()
    pairs = [(i, j, 2 * cx + cy) for i in range(len(refs)) for j, (cx, cy) in enumerate(chips)]
    sends = [_remote(refs[i].at[s, c], refs[i].at[s, c], send.at[3 * i + j], recv.at[3 * i + j], (x, y, 1 - c))
             for i, j, s in pairs]
    recvs = [_remote(refs[i].at[s, c], refs[i].at[s, 1 - c], send.at[3 * i + j], recv.at[3 * i + j], (x, y, 1 - c))
             for i, j, s in pairs]
    return sends, recvs


def _build_swap(refs, send, recv):
    x, y, c, _ = _place()
    n = len(refs) // 2
    cps = [_remote(refs[i].at[:, 1 - c], refs[n + i], send.at[i], recv.at[i], (x, y, 1 - c)) for i in range(n)]
    return cps, cps


def _build_exchange(refs, send, recv):
    x, y, c, chips = _place()
    n = len(refs) // 2
    cps = [_remote(refs[i].at[2 * cx + cy], refs[n + i].at[j], send.at[3 * i + j], recv.at[3 * i + j], (cx, cy, c))
           for i in range(n) for j, (cx, cy) in enumerate(chips)]
    return cps, cps


def _build_spread(refs, send, recv):
    x, y, c, _ = _place()
    flip = lambda bit, on: bit + on - 2 * bit * on
    cps = [_remote(refs[0], refs[1].at[r - 1], send.at[r - 1], recv.at[r - 1],
                   (flip(x, r >> 2 & 1), flip(y, r >> 1 & 1), flip(c, r & 1))) for r in (1, 2, 4, 6)]
    return cps, cps


def _forward_small(landed):
    def body(in_ref, out_ref, send, recv):
        x, y, c, _ = _place()
        cps = [_remote(out_ref.at[r - 1], out_ref.at[r], send.at[k], recv.at[k], (x, y, 1 - c))
               for k, r in enumerate((2, 4, 6))]
        for cp in cps:
            cp.start()
        for cp in cps:
            cp.wait()

    return pl.pallas_call(
        body, name="forward_small",
        in_specs=[_ANY], out_specs=_ANY, out_shape=jax.ShapeDtypeStruct(landed.shape, F32),
        input_output_aliases={0: 0},
        scratch_shapes=[pltpu.SemaphoreType.DMA((3,)), pltpu.SemaphoreType.DMA((3,))],
    )(landed)


def _swap_halves(grads):
    n = len(grads)

    def body(*refs):
        ins, outs = refs[:n], refs[n:2 * n]
        send, recv = refs[2 * n:]
        x, y, c, _ = _place()
        cps = [pltpu.make_async_remote_copy(src_ref=ins[i].at[:, 1 - c], dst_ref=outs[i], send_sem=send.at[i],
                                            recv_sem=recv.at[i], device_id=(x, y, 1 - c), device_id_type=MESH)
               for i in range(n)]
        for cp in cps:
            cp.start()
        for cp in cps:
            cp.wait()

    return pl.pallas_call(
        body, name="swap_halves",
        in_specs=[_ANY] * n, out_specs=[_ANY] * n,
        out_shape=[jax.ShapeDtypeStruct((N_SHARD,) + g.shape[2:], F32) for g in grads],
        scratch_shapes=[pltpu.SemaphoreType.DMA((n,)), pltpu.SemaphoreType.DMA((n,))],
    )(*grads)


def _join_halves(pairs):
    n = len(pairs)

    def body(*refs):
        outs = refs[n:2 * n]
        send, recv = refs[2 * n:]
        x, y, c, _ = _place()
        cps = [pltpu.make_async_remote_copy(src_ref=outs[i].at[c], dst_ref=outs[i].at[c], send_sem=send.at[i],
                                            recv_sem=recv.at[i], device_id=(x, y, 1 - c), device_id_type=MESH)
               for i in range(n)]
        for cp in cps:
            cp.start()
        for i in range(n):
            cps[i].wait_send()
            pltpu.make_async_remote_copy(src_ref=outs[i].at[c], dst_ref=outs[i].at[1 - c], send_sem=send.at[i],
                                         recv_sem=recv.at[i], device_id=(x, y, 1 - c), device_id_type=MESH).wait_recv()

    return pl.pallas_call(
        body, name="join_halves",
        in_specs=[_ANY] * n, out_specs=[_ANY] * n,
        out_shape=[jax.ShapeDtypeStruct(p.shape, F32) for p in pairs],
        input_output_aliases={i: i for i in range(n)},
        scratch_shapes=[pltpu.SemaphoreType.DMA((n,)), pltpu.SemaphoreType.DMA((n,))],
    )(*pairs)


N_SPLIT = 2


def _sum_siblings(tag, grads, recvd, place):
    n = len(grads)

    def body(place_ref, *refs):
        g_refs, r_refs, sb_refs, own_refs = (refs[k * n:(k + 1) * n] for k in range(4))
        s = pl.program_id(1)
        for i in range(n):
            tot = g_refs[i][0, 0] + r_refs[i][0]
            sb_refs[i][0] = tot.astype(BF16)

            @pl.when(s == place_ref[0])
            def _():
                own_refs[i][...] = tot

    in_specs, sb_specs, own_specs, sb_shapes, own_shapes = [], [], [], [], []
    for g in grads:
        _, _, r, cdim = g.shape
        rb = r // N_SPLIT
        in_specs.append(pl.BlockSpec((1, 1, rb, cdim), lambda b, s, p: (s, p[1], b, 0)))
        sb_specs.append(pl.BlockSpec((1, rb, cdim), lambda b, s, p: (s, b, 0)))
        own_specs.append(pl.BlockSpec((rb, cdim), lambda b, s, p: (b, 0)))
        sb_shapes.append(jax.ShapeDtypeStruct((N_SHARD, r, cdim), BF16))
        own_shapes.append(jax.ShapeDtypeStruct((r, cdim), F32))
    out = pl.pallas_call(
        body, name="sum_siblings_" + tag,
        grid_spec=pltpu.PrefetchScalarGridSpec(
            num_scalar_prefetch=1, grid=(N_SPLIT, N_SHARD),
            in_specs=in_specs + sb_specs, out_specs=sb_specs + own_specs),
        out_shape=sb_shapes + own_shapes,
        compiler_params=_params(("parallel", "arbitrary")),
    )(place, *grads, *recvd)
    return out[:n], out[n:]


def _sum_chips(own, recvd, place):
    n = len(own)

    def body(place_ref, *refs):
        o_refs, r_refs, out_refs = (refs[k * n:(k + 1) * n] for k in range(3))
        for i in range(n):
            tot = o_refs[i][...]
            for j in range(3):
                tot = tot + r_refs[i][j].astype(F32)
            out_refs[i][0] = tot

    o_specs, r_specs, out_specs = [], [], []
    for o in own:
        r, cdim = o.shape
        rb = r // N_SPLIT
        o_specs.append(pl.BlockSpec((rb, cdim), lambda b, p: (b, 0)))
        r_specs.append(pl.BlockSpec((3, rb, cdim), lambda b, p: (0, b, 0)))
        out_specs.append(pl.BlockSpec((1, rb, cdim), lambda b, p: (p[1], b, 0)))
    return pl.pallas_call(
        body, name="sum_chips",
        grid_spec=pltpu.PrefetchScalarGridSpec(num_scalar_prefetch=1, grid=(N_SPLIT,),
                                               in_specs=o_specs + r_specs, out_specs=out_specs),
        out_shape=[jax.ShapeDtypeStruct((2,) + o.shape, F32) for o in own],
        compiler_params=_params(("parallel",)),
    )(place, *own, *recvd)


def _adamw_math(w, g, m, v):
    m = ADAM_B1 * m + (1.0 - ADAM_B1) * g
    v = ADAM_B2 * v + (1.0 - ADAM_B2) * (g * g)
    m_hat = m / (1.0 - ADAM_B1 ** ADAM_STEP)
    v_hat = v / (1.0 - ADAM_B2 ** ADAM_STEP)
    delta = -ADAM_LR * (m_hat / (jnp.sqrt(v_hat) + ADAM_EPS) + ADAM_WD * w)
    return delta, m, v


def _adamw(name, ws, gs, ms, vs, n_split):
    n = len(ws)

    def body(*refs):
        w_r, g_r, m_r, v_r, d_o, m_o, v_o = (refs[k * n:(k + 1) * n] for k in range(7))
        for i in range(n):
            d, m, v = _adamw_math(w_r[i][...], g_r[i][...], m_r[i][...], v_r[i][...])
            d_o[i][...] = d
            m_o[i][...] = m
            v_o[i][...] = v

    specs = [pl.BlockSpec((w.shape[0] // n_split, w.shape[1]), lambda b: (b, 0)) for w in ws]
    shapes = [jax.ShapeDtypeStruct(w.shape, F32) for w in ws]
    out = pl.pallas_call(
        body, name=name, grid=(n_split,),
        in_specs=specs * 4, out_specs=specs * 3, out_shape=shapes * 3,
        compiler_params=_params(("parallel",)),
    )(*ws, *gs, *ms, *vs)
    return out[:n], out[n:2 * n], out[2 * n:]


def _reduce_small(own, received):
    def body(own_ref, recv_ref, g_out):
        me = 4 * lax.axis_index("x") + 2 * lax.axis_index("y") + lax.axis_index("c")
        g = None
        for k in range(8):
            mine = me == k
            part = jnp.where(mine, own_ref[...], recv_ref[jnp.where(mine, 0, jnp.bitwise_xor(me, k) - 1)])
            g = part if g is None else g + part
        g_out[...] = g

    return pl.pallas_call(
        body, name="reduce_small",
        out_shape=jax.ShapeDtypeStruct(own.shape, F32),
        compiler_params=_params(),
    )(own, received)


def _adamw_small(ws, gs, ms, vs):
    n = len(ws)

    def body(*refs):
        w_r, g_r, m_r, v_r, d_o, m_o, v_o = (refs[k * n:(k + 1) * n] for k in range(7))
        for i in range(n):
            d, mm, vv = _adamw_math(w_r[i][...], g_r[i][...], m_r[i][...], v_r[i][...])
            d_o[i][...] = d
            m_o[i][...] = mm
            v_o[i][...] = vv

    out = pl.pallas_call(
        body, name="adamw_small",
        out_shape=[jax.ShapeDtypeStruct(t.shape, F32) for t in ws] * 3,
        compiler_params=_params(),
    )(*ws, *gs, *ms, *vs)
    return out[:n], out[n:2 * n], out[2 * n:]


def _s5_operands(lam_re, lam_im, log_step, b_re, b_im, c_re, c_im, glu_w):
    lr = jnp.minimum(lam_re, -1e-4)
    li = lam_im
    step = jnp.exp(log_step)[:, None]
    mag = jnp.exp(lr * step)
    ang = li * step
    abr = mag * jnp.cos(ang)
    abi = mag * jnp.sin(ang)
    nr = abr - 1.0
    ni = abi
    den = lr * lr + li * li
    cr = ((nr * lr + ni * li) / den)[..., None]
    ci = ((ni * lr - nr * li) / den)[..., None]
    bbr = cr * b_re - ci * b_im
    bbi = cr * b_im + ci * b_re
    eye = jnp.eye(8, dtype=F32)
    g, h, p = SSM_GROUPS // SUPER, SSM_GROUP, SSM_STATE

    def b_layout(t):
        return jnp.einsum("ab,japh->jahbp", eye, t.reshape(SUPER, g, p, h)).reshape(SUPER, g * h, g * p)

    def c_layout(t):
        return jnp.einsum("ab,jahp->jbpah", eye, t.reshape(SUPER, g, h, p)).reshape(SUPER, g * p, g * h)

    glu = jnp.einsum("ab,jahk->jahbk", eye, glu_w.reshape(SUPER, g, h, h)).reshape(SUPER, g * h, g * h)
    lam = _pad_rows(jnp.concatenate([abr.reshape(1, N_STATE), abi.reshape(1, N_STATE)], axis=0), 8)
    return lam, b_layout(bbr), b_layout(bbi), c_layout(c_re), c_layout(c_im), glu


def _pad_rows(a, rows):
    return jnp.pad(a, ((0, rows - a.shape[0]), (0, 0)))


def _pack(parts):
    rows = []
    for a in parts:
        flat = a.reshape(-1)
        n = -(-flat.shape[0] // 128)
        rows.append(jnp.pad(flat, (0, n * 128 - flat.shape[0])).reshape(n, 128))
    out = jnp.concatenate(rows, axis=0)
    return _pad_rows(out, -(-out.shape[0] // 8) * 8)


def _unpack(packed, like):
    out, at = [], 0
    for a in like:
        n = -(-a.size // 128)
        out.append(packed[at:at + n].reshape(-1)[:a.size].reshape(a.shape))
        at += n
    return out


STORED = {"ssm_b_re": (0, 1, 3, 2), "ssm_b_im": (0, 1, 3, 2), "ssm_d": (0, 2, 1), "ssm_glu_b": (0, 2, 1),
          "ssm_glu_w": (0, 2, 3, 1)}


def _stored(k, a):
    return a.transpose(STORED[k]) if k in STORED else a


def _logical(k, a):
    return a.transpose(tuple(STORED[k].index(i) for i in range(a.ndim))) if k in STORED else a


SMALL = ("norm1_g", "ssm_lambda_re", "ssm_lambda_im", "ssm_log_step", "ssm_b_re", "ssm_b_im", "ssm_c_re", "ssm_c_im",
         "ssm_d", "ssm_glu_w", "ssm_glu_b", "ssm_norm_g", "pool_w", "pool_scale", "pool_norm_g", "norm2_g",
         "final_norm_g")
LARGE = ("w_in", "w_out", "w_gate", "w_up", "w_down")
WEIGHTS = ("meta_tokens", "norm1_g", "w_in", "ssm_lambda_re", "ssm_lambda_im", "ssm_log_step", "ssm_b_re", "ssm_b_im",
           "ssm_c_re", "ssm_c_im", "ssm_d", "ssm_glu_w", "ssm_glu_b", "ssm_norm_g", "pool_w", "pool_scale",
           "pool_norm_g", "w_out", "norm2_g", "w_gate", "w_up", "w_down", "final_norm_g")


def _step(x, target, w, m, v):
    seq = x.shape[1]
    n_rows = N_META + seq
    n_pad, tm, tp, tc, tg = _plan(n_rows)
    xq, yq, cq = lax.axis_index("x"), lax.axis_index("y"), lax.axis_index("c")
    place = jnp.stack([2 * xq + yq, cq]).astype(jnp.int32)

    def halves(a2d):
        return a2d.reshape(2, a2d.shape[0] // 2, a2d.shape[1])

    def local2d(t):
        return {"w_gate": lambda a: a[0].T, "w_up": lambda a: a[0].T}.get(t, lambda a: a[0])

    shards = [halves(local2d(k)(w[k])) for k in LARGE] + [halves(w["meta_tokens"])]
    full = _cast_shards(shards, [BF16] * len(LARGE) + [F32], place)
    w_in_full, meta_full = _gather_shards([full[0], full[5]])
    late, gather_sems, gather_token = _copies_start("gather_start", list(full[1:5]), (12,), _build_gather,
                                                    after=w_in_full)
    w_in_b = w_in_full.reshape(D_MODEL, D_MODEL)
    meta = meta_full.reshape(N_SHARD, N_META, D_MODEL // N_SHARD).transpose(1, 0, 2).reshape(N_META, D_MODEL)

    h0 = _pad_rows(jnp.concatenate([meta, x[0]], axis=0), n_pad)
    tgt = _pad_rows(jnp.concatenate([jnp.zeros((N_META, D_MODEL), F32), target[0]], axis=0), n_pad)
    s5_in = (w["ssm_lambda_re"][0], w["ssm_lambda_im"][0], w["ssm_log_step"][0], w["ssm_b_re"][0], w["ssm_b_im"][0],
             w["ssm_c_re"][0], w["ssm_c_im"][0], w["ssm_glu_w"][0])
    (lam, bbr, bbi, crt, cit, glu), s5_vjp = jax.vjp(_s5_operands, *s5_in)
    bbr_b, bbi_b, crt_b, cit_b, glu_b16 = (t.astype(BF16) for t in (bbr, bbi, crt, cit, glu))
    s5_vecs = _pad_rows(jnp.concatenate([w["ssm_d"].reshape(1, D_SSM), w["ssm_glu_b"].reshape(1, D_SSM),
                                         w["ssm_norm_g"].reshape(1, D_SSM)], axis=0), 8)
    pool_vecs = _pad_rows(jnp.concatenate([w["pool_scale"].reshape(1, D_POOL), w["pool_norm_g"].reshape(1, D_POOL)],
                                          axis=0), 8)
    pw_b = w["pool_w"][0].astype(BF16)
    g1, g2, gf = w["norm1_g"].reshape(1, D_MODEL), w["norm2_g"].reshape(1, D_MODEL), w["final_norm_g"].reshape(1, D_MODEL)

    u, vv = _fwd_in(h0, g1, w_in_b, tp, gather_token)
    sr, si, y, ms = _s5_fwd(u, lam, bbr_b, bbi_b, crt_b, cit_b, s5_vecs, glu_b16, tc)
    late = _copies_wait("gather_wait", late, gather_sems, [ms], _build_gather)
    late, forward_sems, forward_token = _copies_start("forward_start", late, (12,), _build_forward)
    feat, mp = _pool_fwd(vv, pw_b, pool_vecs, tc, forward_token)
    late = _copies_wait("forward_wait", late, forward_sems, [mp], _build_forward)
    w_out_b = late[0].reshape(D_MODEL, D_MODEL)
    wg_b, wu_b, wd_b = (t.reshape(N_SHARD, FF_SHARD, D_MODEL) for t in late[1:])
    h1, n2, a, b, ff, dh2, dh2b, loss_acc, dgf = _fwd_ffn(h0, ms, mp, w_out_b, g2, wg_b, wu_b, wd_b, gf, tgt, tm, n_rows)

    def quarters(t):
        if t.ndim == 2:
            t = t.reshape(N_SHARD, t.shape[0] // N_SHARD, t.shape[1])
        return t.reshape(N_SHARD, 2, t.shape[1] // 2, t.shape[2])

    def landing(like, lead, dtype):
        return [lax.empty((lead,) + t.shape[2:], dtype) for t in like]

    da, db, dh1, dg2 = _bwd_ffn(dh2, dh2b, a, b, wg_b, wu_b, wd_b, h1, g2, tc)
    ffn_g = [quarters(t) for t in _grad_ffn(n2, da, db, ff, dh2b, tg)]
    nf = len(ffn_g)
    moved, swap_sems, swap_token = _copies_start("swap_start", ffn_g + landing(ffn_g, N_SHARD, F32), (nf,), _build_swap)
    dms, dmp, dwo = _bwd_out(dh1, ms, mp, w_out_b, tp, swap_token)
    moved = _copies_wait("swap_wait", moved, swap_sems, [dwo], _build_swap)
    ffn_parts, ffn_own = _sum_siblings("ffn", moved[:nf], moved[nf:], place)
    moved, exch_sems, exch_token = _copies_start("exchange_start", list(ffn_parts) + landing(ffn_g, 3, BF16), (3 * nf,),
                                                 _build_exchange)
    du, dbbr, dbbi, dcrt, dcit, dglu, ds5v, dlam = _s5_bwd(dms, y, u, sr, si, lam, bbr_b, bbi_b, crt_b, cit_b,
                                                           s5_vecs, glu_b16, tc, exch_token)
    dv, dpw, dpoolv = _pool_bwd(dmp, feat, pw_b, pool_vecs, tc)
    dh0, dwi, dg1 = _bwd_in(du, dv, h0, dh1, g1, w_in_b, tp)
    ffn_from_chips = _copies_wait("exchange_wait", moved, exch_sems, [dh0], _build_exchange)[nf:]
    dlam = _pad_rows(jnp.concatenate([jnp.sum(dlam[:SEGMENTS], axis=0, keepdims=True),
                                      jnp.sum(dlam[SEGMENTS:], axis=0, keepdims=True)], axis=0), 8)
    d_lre, d_lim, d_lstep, d_bre, d_bim, d_cre, d_cim, d_gluw = s5_vjp((dlam, dbbr, dbbi, dcrt, dcit, dglu))
    grad_x = dh0[N_META:n_rows][None]

    small_g = {
        "norm1_g": dg1, "ssm_lambda_re": d_lre, "ssm_lambda_im": d_lim, "ssm_log_step": d_lstep, "ssm_b_re": d_bre,
        "ssm_b_im": d_bim, "ssm_c_re": d_cre, "ssm_c_im": d_cim, "ssm_d": ds5v[0], "ssm_glu_w": d_gluw,
        "ssm_glu_b": ds5v[1], "ssm_norm_g": ds5v[2], "pool_w": dpw, "pool_scale": dpoolv[0], "pool_norm_g": dpoolv[1],
        "norm2_g": dg2, "final_norm_g": dgf,
    }
    like = [_stored(k, w[k]) for k in SMALL]
    packed_g = _pack([_stored(k, small_g[k].reshape(w[k].shape)) for k in SMALL] + [dh0[:N_META], loss_acc[0:1, 0:1]])

    mix_g = [quarters(t) for t in (dwi, dwo)]
    mix_parts, mix_own = _sum_siblings("mix", mix_g, _swap_halves(mix_g), place)
    moved, mix_sems, mix_token = _copies_start("mix_exchange_start", list(mix_parts) + landing(mix_g, 3, BF16),
                                               (3 * len(mix_g),), _build_exchange)
    spread, small_sems, small_token = _copies_start(
        "small_start", [packed_g, lax.empty((7,) + packed_g.shape, F32)], (7,), _build_spread, after=mix_token)
    mix_from_chips = _copies_wait("mix_exchange_wait", moved, mix_sems, [small_token], _build_exchange)[len(mix_g):]
    joined = _join_halves(_sum_chips(list(mix_own) + list(ffn_own), list(mix_from_chips) + list(ffn_from_chips), place))
    g_large = [j.reshape(j.shape[0] * j.shape[1], j.shape[2]) for j in joined]
    w2d, m2d, v2d = ([local2d(k)(t[k]) for k in LARGE] for t in (w, m, v))
    d_large, m_large, v_large = _adamw("adamw_large", w2d, g_large, m2d, v2d, 8)

    own_g, landed = _copies_wait("small_wait", spread, small_sems, [d_large[0]], _build_spread)
    g_pk = _reduce_small(own_g, _forward_small(landed))
    g_small = _unpack(g_pk, like + [jax.ShapeDtypeStruct((N_META, D_MODEL), F32), jax.ShapeDtypeStruct((1, 1), F32)])
    loss = g_small.pop()[0, 0]
    rows2d = lambda t: t.reshape(1, -1) if t.ndim == 1 else t
    d_small, m_small, v_small = _adamw_small(*([rows2d(t) for t in ts] for ts in (
        like, g_small[:-1], [_stored(k, m[k]) for k in SMALL], [_stored(k, v[k]) for k in SMALL])))
    g_small, d_small, m_small, v_small = ([_logical(k, t.reshape(a.shape)) for t, a, k in zip(ts, like, SMALL)] + ts[len(SMALL):]
                                          for ts in (g_small, list(d_small), list(m_small), list(v_small)))
    q = place[0]
    g_meta = lax.dynamic_slice_in_dim(g_small[-1], q * (D_MODEL // N_SHARD), D_MODEL // N_SHARD, axis=1)
    d_meta, m_meta, v_meta = _adamw("adamw_meta", [w["meta_tokens"]], [g_meta], [m["meta_tokens"]],
                                    [v["meta_tokens"]], 1)

    grads, deltas, new_m, new_v = {}, {}, {}, {}
    for i, k in enumerate(SMALL):
        grads[k], deltas[k], new_m[k], new_v[k] = g_small[i], d_small[i], m_small[i], v_small[i]
    for i, k in enumerate(LARGE):
        back = (lambda t: t.T[None]) if k in ("w_gate", "w_up") else (lambda t: t[None])
        grads[k], deltas[k], new_m[k], new_v[k] = (back(t) for t in (g_large[i], d_large[i], m_large[i], v_large[i]))
    grads["meta_tokens"], deltas["meta_tokens"] = g_meta, d_meta[0]
    new_m["meta_tokens"], new_v["meta_tokens"] = m_meta[0], v_meta[0]
    return (loss, grad_x, *[grads[k] for k in WEIGHTS], *[deltas[k] for k in WEIGHTS],
            *[new_m[k] for k in WEIGHTS], *[new_v[k] for k in WEIGHTS])


def kernel(x, meta_tokens, norm1_g, w_in, ssm_lambda_re, ssm_lambda_im, ssm_log_step, ssm_b_re, ssm_b_im, ssm_c_re, ssm_c_im, ssm_d, ssm_glu_w, ssm_glu_b, ssm_norm_g, pool_w, pool_scale, pool_norm_g, w_out, norm2_g, w_gate, w_up, w_down, final_norm_g, loss_target, m_meta_tokens, m_norm1_g, m_w_in, m_ssm_lambda_re, m_ssm_lambda_im, m_ssm_log_step, m_ssm_b_re, m_ssm_b_im, m_ssm_c_re, m_ssm_c_im, m_ssm_d, m_ssm_glu_w, m_ssm_glu_b, m_ssm_norm_g, m_pool_w, m_pool_scale, m_pool_norm_g, m_w_out, m_norm2_g, m_w_gate, m_w_up, m_w_down, m_final_norm_g, v_meta_tokens, v_norm1_g, v_w_in, v_ssm_lambda_re, v_ssm_lambda_im, v_ssm_log_step, v_ssm_b_re, v_ssm_b_im, v_ssm_c_re, v_ssm_c_im, v_ssm_d, v_ssm_glu_w, v_ssm_glu_b, v_ssm_norm_g, v_pool_w, v_pool_scale, v_pool_norm_g, v_w_out, v_norm2_g, v_w_gate, v_w_up, v_w_down, v_final_norm_g):
    w = dict(meta_tokens=meta_tokens, norm1_g=norm1_g, w_in=w_in, ssm_lambda_re=ssm_lambda_re, ssm_lambda_im=ssm_lambda_im, ssm_log_step=ssm_log_step, ssm_b_re=ssm_b_re, ssm_b_im=ssm_b_im, ssm_c_re=ssm_c_re, ssm_c_im=ssm_c_im, ssm_d=ssm_d, ssm_glu_w=ssm_glu_w, ssm_glu_b=ssm_glu_b, ssm_norm_g=ssm_norm_g, pool_w=pool_w, pool_scale=pool_scale, pool_norm_g=pool_norm_g, w_out=w_out, norm2_g=norm2_g, w_gate=w_gate, w_up=w_up, w_down=w_down, final_norm_g=final_norm_g)
    m = dict(meta_tokens=m_meta_tokens, norm1_g=m_norm1_g, w_in=m_w_in, ssm_lambda_re=m_ssm_lambda_re, ssm_lambda_im=m_ssm_lambda_im, ssm_log_step=m_ssm_log_step, ssm_b_re=m_ssm_b_re, ssm_b_im=m_ssm_b_im, ssm_c_re=m_ssm_c_re, ssm_c_im=m_ssm_c_im, ssm_d=m_ssm_d, ssm_glu_w=m_ssm_glu_w, ssm_glu_b=m_ssm_glu_b, ssm_norm_g=m_ssm_norm_g, pool_w=m_pool_w, pool_scale=m_pool_scale, pool_norm_g=m_pool_norm_g, w_out=m_w_out, norm2_g=m_norm2_g, w_gate=m_w_gate, w_up=m_w_up, w_down=m_w_down, final_norm_g=m_final_norm_g)
    v = dict(meta_tokens=v_meta_tokens, norm1_g=v_norm1_g, w_in=v_w_in, ssm_lambda_re=v_ssm_lambda_re, ssm_lambda_im=v_ssm_lambda_im, ssm_log_step=v_ssm_log_step, ssm_b_re=v_ssm_b_re, ssm_b_im=v_ssm_b_im, ssm_c_re=v_ssm_c_re, ssm_c_im=v_ssm_c_im, ssm_d=v_ssm_d, ssm_glu_w=v_ssm_glu_w, ssm_glu_b=v_ssm_glu_b, ssm_norm_g=v_ssm_norm_g, pool_w=v_pool_w, pool_scale=v_pool_scale, pool_norm_g=v_pool_norm_g, w_out=v_w_out, norm2_g=v_norm2_g, w_gate=v_w_gate, w_up=v_w_up, w_down=v_w_down, final_norm_g=v_final_norm_g)
    return _step(x, loss_target, w, m, v)
```

```python
import functools
import math

import jax
import jax.numpy as jnp
from jax import lax
from jax.experimental import pallas as pl
from jax.experimental.pallas import tpu as pltpu

F32 = jnp.float32
BF16 = jnp.bfloat16
MESH = pl.DeviceIdType.MESH
AXES = ("x", "y", "c")

D_MODEL = 1024
D_SSM = 512
D_POOL = 512
N_META = 16
SSM_GROUP = 16
SSM_GROUPS = 32
SSM_STATE = 64
N_STATE = SSM_GROUPS * SSM_STATE
STATE_BLOCKS = N_STATE // 128
SUPER = 4
POOL_WINDOWS = (2, 4, 8, 16)
POOL_HALO = 16
D_FF = 2816
N_SHARD = 4
FF_SHARD = D_FF // N_SHARD
EPS = 1e-6
ADAM_LR, ADAM_B1, ADAM_B2, ADAM_EPS, ADAM_WD, ADAM_STEP = 0.001, 0.9, 0.999, 1e-08, 0.01, 10
VMEM_LIMIT = 56 * 1024 * 1024


def _plan(n_rows):
    if n_rows > 2048:
        tm, tp, tc, tg = 416, 832, 320, 1040
    else:
        tm, tp, tc, tg = 128, 128, 64, 128
    step = math.lcm(tm, tp, tc, tg)
    return -(-n_rows // step) * step, tm, tp, tc, tg


def _params(sem=None):
    return pltpu.CompilerParams(dimension_semantics=sem, vmem_limit_bytes=VMEM_LIMIT)


def _dot(a, b):
    return jnp.dot(a, b, preferred_element_type=F32)


def _dot_nt(a, b):
    return lax.dot_general(a, b, (((1,), (1,)), ((), ())), preferred_element_type=F32)


def _dot_tn(a, b):
    return lax.dot_general(a, b, (((0,), (0,)), ((), ())), preferred_element_type=F32)


def _sigmoid(x):
    return 0.5 * jnp.tanh(0.5 * x) + 0.5


_GELU_C = math.sqrt(2.0 / math.pi)


def _gelu_and_grad(y):
    y2 = y * y
    t = jnp.tanh(_GELU_C * (y + 0.044715 * y * y2))
    g = 0.5 * y * (1.0 + t)
    dg = 0.5 * (1.0 + t) + 0.5 * y * (1.0 - t * t) * (_GELU_C * (1.0 + 3.0 * 0.044715 * y2))
    return g, dg


def _rms(x):
    return lax.rsqrt(jnp.mean(x * x, axis=-1, keepdims=True) + EPS)


def _rms_bwd(dn, xhat, r):
    return r * (dn - xhat * jnp.mean(dn * xhat, axis=-1, keepdims=True))


def _full(shape):
    nd = len(shape)
    return pl.BlockSpec(shape, lambda *_: (0,) * nd)


def _fwd_in(h0, g1, w_in_b, tm, token):
    n_pad = h0.shape[0]

    def body(h_ref, g_ref, w_ref, token_ref, u_ref, v_ref):
        h = h_ref[...]
        n1 = (h * _rms(h) * g_ref[...]).astype(BF16)
        proj = _dot(n1, w_ref[...])
        for i in range(4):
            u_ref[i] = proj[:, 128 * i:128 * (i + 1)]
        v_ref[...] = proj[:, D_SSM:]

    row = lambda w: pl.BlockSpec((tm, w), lambda i: (i, 0))
    return pl.pallas_call(
        body, grid=(n_pad // tm,), name="fwd_in",
        in_specs=[row(D_MODEL), _full((1, D_MODEL)), _full((D_MODEL, D_MODEL)), _ANY],
        out_specs=[pl.BlockSpec((4, tm, 128), lambda i: (0, i, 0)), row(D_POOL)],
        out_shape=[jax.ShapeDtypeStruct((4, n_pad, 128), F32), jax.ShapeDtypeStruct((n_pad, D_POOL), F32)],
        compiler_params=_params(("parallel",)),
    )(h0, g1, w_in_b, token)


def _fwd_ffn(h0, ms, mp, w_out_b, g2, wg_b, wu_b, wd_b, gf, target, tm, n_valid):
    n_pad = h0.shape[0]
    nt = n_pad // tm

    def body(h0_ref, ms_ref, mp_ref, wo_ref, g2_ref, wg_hbm, wu_hbm, wd_hbm, gf_ref, tgt_ref,
             h1_ref, n2_ref, a_ref, b_ref, ff_ref, dh2_ref, dh2b_ref, loss_ref, dgf_ref,
             wg_ref, wu_ref, wd_ref, sems, acc):
        i, phase = pl.program_id(0), pl.program_id(1)

        @pl.when((i == 0) & (phase == 0))
        def _():
            loss_ref[...] = jnp.zeros_like(loss_ref)
            dgf_ref[...] = jnp.zeros_like(dgf_ref)
            cps = [pltpu.make_async_copy(src, dst, sems.at[k])
                   for k, (src, dst) in enumerate(((wg_hbm, wg_ref), (wu_hbm, wu_ref), (wd_hbm, wd_ref)))]
            for cp in cps:
                cp.start()
            for cp in cps:
                cp.wait()

        @pl.when(phase == 0)
        def _():
            h1 = h0_ref[...] + _dot(ms_ref[...], wo_ref[:D_SSM, :]) + _dot(mp_ref[...], wo_ref[D_SSM:, :])
            h1_ref[...] = h1
            acc[...] = h1
            n2_ref[...] = (h1 * _rms(h1) * g2_ref[...]).astype(BF16)

        @pl.when(phase == 1)
        def _():
            n2 = n2_ref[...]
            for q in range(N_SHARD):
                a = _dot_nt(n2, wg_ref[q])
                b = _dot_nt(n2, wu_ref[q])
                a_ref[q] = a.astype(BF16)
                b_ref[q] = b.astype(BF16)
                ff = (a * _sigmoid(a) * b).astype(BF16)
                ff_ref[q] = ff
                acc[...] += _dot(ff, wd_ref[q])

        @pl.when(phase == 2)
        def _():
            h2 = acc[...]
            r = _rms(h2)
            xhat = h2 * r
            gf_row = gf_ref[...]
            rows = i * tm + lax.broadcasted_iota(jnp.int32, (tm, 1), 0)
            valid = (rows >= N_META) & (rows < n_valid)
            diff = jnp.where(valid, xhat * gf_row - tgt_ref[...], 0.0)
            loss_ref[...] += jnp.full(loss_ref.shape, 0.5 / D_MODEL, F32) * jnp.sum(diff * diff)
            dout = diff * (1.0 / D_MODEL)
            dgf_ref[...] += jnp.sum(dout * xhat, axis=0, keepdims=True)
            dh2 = _rms_bwd(dout * gf_row, xhat, r)
            dh2_ref[...] = dh2
            dh2b_ref[...] = dh2.astype(BF16)

    row = lambda w: pl.BlockSpec((tm, w), lambda i, p: (i, 0))
    act = pl.BlockSpec((N_SHARD, tm, FF_SHARD), lambda i, p: (0, i, 0))
    sds = jax.ShapeDtypeStruct
    return pl.pallas_call(
        body, grid=(nt, 3), name="fwd_ffn",
        in_specs=[row(D_MODEL), row(D_SSM), row(D_POOL), _full((D_MODEL, D_MODEL)), _full((1, D_MODEL)),
                  _ANY, _ANY, _ANY, _full((1, D_MODEL)), row(D_MODEL)],
        out_specs=[row(D_MODEL), row(D_MODEL), act, act, act, row(D_MODEL), row(D_MODEL), _full((8, 128)),
                   _full((1, D_MODEL))],
        out_shape=[sds((n_pad, D_MODEL), F32), sds((n_pad, D_MODEL), BF16),
                   sds((N_SHARD, n_pad, FF_SHARD), BF16), sds((N_SHARD, n_pad, FF_SHARD), BF16),
                   sds((N_SHARD, n_pad, FF_SHARD), BF16), sds((n_pad, D_MODEL), F32), sds((n_pad, D_MODEL), BF16),
                   sds((8, 128), F32), sds((1, D_MODEL), F32)],
        scratch_shapes=[pltpu.VMEM(wg_b.shape, BF16), pltpu.VMEM(wu_b.shape, BF16), pltpu.VMEM(wd_b.shape, BF16),
                        pltpu.SemaphoreType.DMA((3,)), pltpu.VMEM((tm, D_MODEL), F32)],
        compiler_params=_params(("arbitrary", "arbitrary")),
    )(h0, ms, mp, w_out_b, g2, wg_b, wu_b, wd_b, gf, target)


def _bwd_ffn(dh2, dh2b, a, b, wg_b, wu_b, wd_b, h1, g2, tm):
    n_pad = dh2.shape[0]

    def body(dh2_ref, dh2b_ref, a_ref, b_ref, wg_hbm, wu_hbm, wd_hbm, h1_ref, g2_ref, da_ref, db_ref, dh1_ref, dg2_ref,
             wg_ref, wu_ref, wd_ref, sems, acc):
        @pl.when(pl.program_id(0) == 0)
        def _():
            dg2_ref[...] = jnp.zeros_like(dg2_ref)
            cps = [pltpu.make_async_copy(src, dst, sems.at[k])
                   for k, (src, dst) in enumerate(((wg_hbm, wg_ref), (wu_hbm, wu_ref), (wd_hbm, wd_ref)))]
            for cp in cps:
                cp.start()
            for cp in cps:
                cp.wait()

        dh2b = dh2b_ref[...]
        for q in range(N_SHARD):
            dff = _dot_nt(dh2b, wd_ref[q])
            a_v, b_v = a_ref[q].astype(F32), b_ref[q].astype(F32)
            sig = _sigmoid(a_v)
            silu = a_v * sig
            da = (dff * b_v * (sig + silu * (1.0 - sig))).astype(BF16)
            db = (dff * silu).astype(BF16)
            da_ref[q] = da
            db_ref[q] = db
            part = _dot(da, wg_ref[q]) + _dot(db, wu_ref[q])
            if q == 0:
                acc[...] = part
            else:
                acc[...] += part

        h1 = h1_ref[...]
        r = _rms(h1)
        xhat = h1 * r
        dn2 = acc[...]
        dg2_ref[...] += jnp.sum(dn2 * xhat, axis=0, keepdims=True)
        dh1_ref[...] = dh2_ref[...] + _rms_bwd(dn2 * g2_ref[...], xhat, r)

    row = lambda w: pl.BlockSpec((tm, w), lambda i: (i, 0))
    act = pl.BlockSpec((N_SHARD, tm, FF_SHARD), lambda i: (0, i, 0))
    sds = jax.ShapeDtypeStruct
    return pl.pallas_call(
        body, grid=(n_pad // tm,), name="bwd_ffn",
        in_specs=[row(D_MODEL), row(D_MODEL), act, act, _ANY, _ANY, _ANY, row(D_MODEL), _full((1, D_MODEL))],
        out_specs=[act, act, row(D_MODEL), _full((1, D_MODEL))],
        out_shape=[sds((N_SHARD, n_pad, FF_SHARD), BF16), sds((N_SHARD, n_pad, FF_SHARD), BF16),
                   sds((n_pad, D_MODEL), F32), sds((1, D_MODEL), F32)],
        scratch_shapes=[pltpu.VMEM(wg_b.shape, BF16), pltpu.VMEM(wu_b.shape, BF16), pltpu.VMEM(wd_b.shape, BF16),
                        pltpu.SemaphoreType.DMA((3,)), pltpu.VMEM((tm, D_MODEL), F32)],
        compiler_params=_params(("arbitrary",)),
    )(dh2, dh2b, a, b, wg_b, wu_b, wd_b, h1, g2)


def _grad_ffn(n2, da, db, ff, dh2b, tm):
    n_pad = n2.shape[0]

    def body(n2_ref, da_ref, db_ref, ff_ref, dh2_ref, dwg_ref, dwu_ref, dwd_ref):
        i = pl.program_id(1)
        n2_v = n2_ref[...]
        gg = _dot_tn(da_ref[0], n2_v)
        gu = _dot_tn(db_ref[0], n2_v)
        gd = _dot_tn(ff_ref[0], dh2_ref[...])

        @pl.when(i == 0)
        def _():
            dwg_ref[0] = gg
            dwu_ref[0] = gu
            dwd_ref[0] = gd

        @pl.when(i > 0)
        def _():
            dwg_ref[0] += gg
            dwu_ref[0] += gu
            dwd_ref[0] += gd

    row = lambda w: pl.BlockSpec((tm, w), lambda q, i: (i, 0))
    act = pl.BlockSpec((1, tm, FF_SHARD), lambda q, i: (q, i, 0))
    sds = jax.ShapeDtypeStruct
    return pl.pallas_call(
        body, grid=(N_SHARD, n_pad // tm), name="grad_ffn",
        in_specs=[row(D_MODEL), act, act, act, row(D_MODEL)],
        out_specs=[pl.BlockSpec((1, FF_SHARD, D_MODEL), lambda q, i: (q, 0, 0))] * 3,
        out_shape=[sds((N_SHARD, FF_SHARD, D_MODEL), F32)] * 3,
        compiler_params=_params(("parallel", "arbitrary")),
    )(n2, da, db, ff, dh2b)


def _bwd_out(dh1, ms, mp, w_out_b, tm, token):
    n_pad = dh1.shape[0]

    def body(dh1_ref, ms_ref, mp_ref, wo_ref, token_ref, dms_ref, dmp_ref, dwo_ref):
        i = pl.program_id(0)

        @pl.when(i == 0)
        def _():
            dwo_ref[...] = jnp.zeros_like(dwo_ref)

        d = dh1_ref[...].astype(BF16)
        dms = _dot_nt(d, wo_ref[:D_SSM, :])
        for k in range(4):
            dms_ref[k] = dms[:, 128 * k:128 * (k + 1)]
        dmp_ref[...] = _dot_nt(d, wo_ref[D_SSM:, :])
        dwo_ref[:D_SSM, :] += _dot_tn(ms_ref[...], d)
        dwo_ref[D_SSM:, :] += _dot_tn(mp_ref[...], d)

    row = lambda w: pl.BlockSpec((tm, w), lambda i: (i, 0))
    sds = jax.ShapeDtypeStruct
    return pl.pallas_call(
        body, grid=(n_pad // tm,), name="bwd_out",
        in_specs=[row(D_MODEL), row(D_SSM), row(D_POOL), _full((D_MODEL, D_MODEL)), _ANY],
        out_specs=[pl.BlockSpec((4, tm, 128), lambda i: (0, i, 0)), row(D_POOL), _full((D_MODEL, D_MODEL))],
        out_shape=[sds((4, n_pad, 128), F32), sds((n_pad, D_POOL), F32), sds((D_MODEL, D_MODEL), F32)],
        compiler_params=_params(("arbitrary",)),
    )(dh1, ms, mp, w_out_b, token)


def _bwd_in(du, dv, h0, dh1, g1, w_in_b, tm):
    n_pad = h0.shape[0]

    def body(du_ref, dv_ref, h0_ref, dh1_ref, g1_ref, w_ref, dh0_ref, dwi_ref, dg1_ref):
        i = pl.program_id(0)

        @pl.when(i == 0)
        def _():
            dwi_ref[...] = jnp.zeros_like(dwi_ref)
            dg1_ref[...] = jnp.zeros_like(dg1_ref)

        dub = du_ref[...].astype(BF16)
        dvb = dv_ref[...].astype(BF16)
        dn1 = _dot_nt(dub, w_ref[:, :D_SSM]) + _dot_nt(dvb, w_ref[:, D_SSM:])
        h = h0_ref[...]
        r = _rms(h)
        xhat = h * r
        g_row = g1_ref[...]
        n1 = (xhat * g_row).astype(BF16)
        dwi_ref[:, :D_SSM] += _dot_tn(n1, dub)
        dwi_ref[:, D_SSM:] += _dot_tn(n1, dvb)
        dg1_ref[...] += jnp.sum(dn1 * xhat, axis=0, keepdims=True)
        dh0_ref[...] = dh1_ref[...] + _rms_bwd(dn1 * g_row, xhat, r)

    row = lambda w: pl.BlockSpec((tm, w), lambda i: (i, 0))
    sds = jax.ShapeDtypeStruct
    return pl.pallas_call(
        body, grid=(n_pad // tm,), name="bwd_in",
        in_specs=[row(D_SSM), row(D_POOL), row(D_MODEL), row(D_MODEL), _full((1, D_MODEL)), _full((D_MODEL, D_MODEL))],
        out_specs=[row(D_MODEL), _full((D_MODEL, D_MODEL)), _full((1, D_MODEL))],
        out_shape=[sds((n_pad, D_MODEL), F32), sds((D_MODEL, D_MODEL), F32), sds((1, D_MODEL), F32)],
        compiler_params=_params(("arbitrary",)),
    )(du, dv, h0, dh1, g1, w_in_b)


SEGMENTS = 8


def _interleaved(ref, seg):
    return jnp.concatenate(
        [jnp.concatenate([ref[i, pl.ds(j, SEGMENTS, stride=seg), :] for i in range(4)], axis=1) for j in range(seg)],
        axis=0)


def _time_order(scratch, val, seg):
    for i in range(4):
        scratch[i] = val[:, 128 * i:128 * (i + 1)]
    tiles = []
    for m in range(val.shape[0] // 8):
        s, j0 = divmod(8 * m, seg)
        tiles.append(jnp.concatenate(
            [scratch[i, pl.ds(8 * j0 + s, 8, stride=SEGMENTS), :] for i in range(4)], axis=1))
    return jnp.concatenate(tiles, axis=0)


def _power_table(lam_ref, pw_r, pw_i, seg):
    a_r = jnp.broadcast_to(lam_ref[0:1, :], (SEGMENTS, N_STATE))
    a_i = jnp.broadcast_to(lam_ref[1:2, :], (SEGMENTS, N_STATE))
    p_r, p_i = a_r, a_i
    for k in range(seg):
        pw_r[SEGMENTS * k:SEGMENTS * (k + 1), :] = p_r
        pw_i[SEGMENTS * k:SEGMENTS * (k + 1), :] = p_i
        p_r, p_i = p_r * a_r - p_i * a_i, p_r * a_i + p_i * a_r


def _segment_scan(xr_ref, xi_ref, cols, pw_r, pw_i, hr_s, hi_s, seg, reverse):
    sign = -1.0 if reverse else 1.0
    a_r, a_i = pw_r[0:SEGMENTS, cols], sign * pw_i[0:SEGMENTS, cols]

    def step(n, carry):
        hr, hi = carry
        o = pl.multiple_of((seg - 1 - n if reverse else n) * SEGMENTS, SEGMENTS)
        nr = a_r * hr - a_i * hi + xr_ref[pl.ds(o, SEGMENTS), cols]
        ni = a_r * hi + a_i * hr + xi_ref[pl.ds(o, SEGMENTS), cols]
        xr_ref[pl.ds(o, SEGMENTS), cols] = nr
        xi_ref[pl.ds(o, SEGMENTS), cols] = ni
        return nr, ni

    zero = jnp.zeros((SEGMENTS, cols.stop - cols.start), F32)
    e_r, e_i = lax.fori_loop(0, seg, step, (zero, zero), unroll=2)

    top = SEGMENTS * (seg - 1)
    ls_r, ls_i = pw_r[top:top + 1, cols], sign * pw_i[top:top + 1, cols]
    c_r, c_i = hr_s[0:1, cols], hi_s[0:1, cols]
    in_r, in_i = [None] * SEGMENTS, [None] * SEGMENTS
    for s in (range(SEGMENTS - 1, -1, -1) if reverse else range(SEGMENTS)):
        in_r[s], in_i[s] = c_r, c_i
        c_r, c_i = (e_r[s:s + 1, :] + ls_r * c_r - ls_i * c_i, e_i[s:s + 1, :] + ls_r * c_i + ls_i * c_r)
    hr_s[0:1, cols] = c_r
    hi_s[0:1, cols] = c_i
    cm_r, cm_i = jnp.concatenate(in_r, axis=0), jnp.concatenate(in_i, axis=0)

    def fix(jj, _):
        o = pl.multiple_of(jj * SEGMENTS, SEGMENTS)
        k = pl.multiple_of((seg - 1 - jj if reverse else jj) * SEGMENTS, SEGMENTS)
        p_r, p_i = pw_r[pl.ds(k, SEGMENTS), cols], sign * pw_i[pl.ds(k, SEGMENTS), cols]
        xr_ref[pl.ds(o, SEGMENTS), cols] += p_r * cm_r - p_i * cm_i
        xi_ref[pl.ds(o, SEGMENTS), cols] += p_r * cm_i + p_i * cm_r
        return 0

    lax.fori_loop(0, seg, fix, 0, unroll=2)


def _s5_tail(y, glu_ref, glub):
    g, dgelu = _gelu_and_grad(y)
    gb = g.astype(BF16)
    gate = jnp.concatenate([_dot(gb[:, 128 * j:128 * (j + 1)], glu_ref[j]) for j in range(SUPER)], axis=1) + glub
    sig = _sigmoid(gate)
    return g, gb, dgelu, sig, g * sig


def _s5_fwd(u4, lam, bbr, bbi, crt, cit, vecs, glu, tc):
    n_pad = u4.shape[1]
    seg = tc // SEGMENTS

    def body(u_ref, lam_ref, bbr_ref, bbi_ref, crt_ref, cit_ref, vec_ref, glu_ref,
             sr_ref, si_ref, y_ref, ms_ref, hr_s, hi_s, pw_r, pw_i, lanes):
        @pl.when(pl.program_id(0) == 0)
        def _():
            hr_s[...] = jnp.zeros_like(hr_s)
            hi_s[...] = jnp.zeros_like(hi_s)
            _power_table(lam_ref, pw_r, pw_i, seg)

        u_v = _interleaved(u_ref, seg)
        ub = u_v.astype(BF16)
        for j in range(SUPER):
            uj = ub[:, 128 * j:128 * (j + 1)]
            sr_ref[:, 512 * j:512 * (j + 1)] = _dot(uj, bbr_ref[j])
            si_ref[:, 512 * j:512 * (j + 1)] = _dot(uj, bbi_ref[j])
        for j in range(SUPER):
            _segment_scan(sr_ref, si_ref, slice(512 * j, 512 * (j + 1)), pw_r, pw_i, hr_s, hi_s, seg, False)

        d_row, glub, gs = vec_ref[0:1, :], vec_ref[1:2, :], vec_ref[2:3, :]
        ys_c = []
        for j in range(SUPER):
            sr_j = sr_ref[:, 512 * j:512 * (j + 1)].astype(BF16)
            si_j = si_ref[:, 512 * j:512 * (j + 1)].astype(BF16)
            ys_c.append(_dot(sr_j, crt_ref[j]) - _dot(si_j, cit_ref[j]))
        y = jnp.concatenate(ys_c, axis=1) + d_row * u_v
        y_ref[...] = y
        _, _, _, _, ys = _s5_tail(y, glu_ref, glub)
        ms_ref[...] = _time_order(lanes, ys * _rms(ys) * gs, seg).astype(BF16)

    chunk = lambda w: pl.BlockSpec((tc, w), lambda c: (c, 0))
    lane_blocks = pl.BlockSpec((4, tc, 128), lambda c: (0, c, 0))
    sds = jax.ShapeDtypeStruct
    return pl.pallas_call(
        body, grid=(n_pad // tc,), name="s5_fwd",
        in_specs=[lane_blocks, _full((8, N_STATE)), _full((SUPER, 128, 512)), _full((SUPER, 128, 512)),
                  _full((SUPER, 512, 128)), _full((SUPER, 512, 128)), _full((8, D_SSM)), _full((SUPER, 128, 128))],
        out_specs=[chunk(N_STATE), chunk(N_STATE), chunk(D_SSM), chunk(D_SSM)],
        out_shape=[sds((n_pad, N_STATE), F32), sds((n_pad, N_STATE), F32),
                   sds((n_pad, D_SSM), F32), sds((n_pad, D_SSM), BF16)],
        scratch_shapes=[pltpu.VMEM((8, N_STATE), F32), pltpu.VMEM((8, N_STATE), F32),
                        pltpu.VMEM((tc, N_STATE), F32), pltpu.VMEM((tc, N_STATE), F32),
                        pltpu.VMEM((4, tc, 128), F32)],
        compiler_params=_params(("arbitrary",)),
    )(u4, lam, bbr, bbi, crt, cit, vecs, glu)


def _s5_bwd(dms4, y, u4, sr, si, lam, bbr, bbi, crt, cit, vecs, glu, tc, token):
    n_pad = u4.shape[1]
    nc = n_pad // tc
    seg = tc // SEGMENTS

    def body(dms_ref, y_ref, u_ref, sr_ref, si_ref, pr_ref, pi_ref, lam_ref, bbr_ref, bbi_ref, crt_ref, cit_ref,
             vec_ref, glu_ref, token_ref, du_ref, dbbr_ref, dbbi_ref, dcrt_ref, dcit_ref, dglu_ref, dvec_ref, dlam_ref,
             qr_s, qi_s, cr_s, ci_s, pw_r, pw_i, lanes):
        c = pl.program_id(0)

        @pl.when(c == 0)
        def _():
            for ref in (dbbr_ref, dbbi_ref, dcrt_ref, dcit_ref, dglu_ref, dvec_ref, dlam_ref, cr_s, ci_s):
                ref[...] = jnp.zeros_like(ref)
            _power_table(lam_ref, pw_r, pw_i, seg)

        d_row, glub, gs = vec_ref[0:1, :], vec_ref[1:2, :], vec_ref[2:3, :]
        y_v, u_v = y_ref[...], _interleaved(u_ref, seg)
        ub = u_v.astype(BF16)
        g, gb, dgelu, sig, ys = _s5_tail(y_v, glu_ref, glub)
        r = _rms(ys)
        xhat = ys * r
        dm = _interleaved(dms_ref, seg)
        dys = _rms_bwd(dm * gs, xhat, r)
        dgate = dys * g * sig * (1.0 - sig)
        dgateb = dgate.astype(BF16)
        dg = dys * sig + jnp.concatenate(
            [_dot_nt(dgateb[:, 128 * j:128 * (j + 1)], glu_ref[j]) for j in range(SUPER)], axis=1)
        dy = dg * dgelu
        dyb = dy.astype(BF16)
        dvec_ref[0:1, :] += jnp.sum(dy * u_v, axis=0, keepdims=True)
        dvec_ref[1:2, :] += jnp.sum(dgate, axis=0, keepdims=True)
        dvec_ref[2:3, :] += jnp.sum(dm * xhat, axis=0, keepdims=True)

        for j in range(SUPER):
            cols, states = slice(128 * j, 128 * (j + 1)), slice(512 * j, 512 * (j + 1))
            dglu_ref[j] += _dot_tn(gb[:, cols], dgateb[:, cols])
            dcrt_ref[j] += _dot_tn(sr_ref[:, states].astype(BF16), dyb[:, cols])
            dcit_ref[j] -= _dot_tn(si_ref[:, states].astype(BF16), dyb[:, cols])
            qr_s[:, states] = _dot_nt(dyb[:, cols], crt_ref[j])
            qi_s[:, states] = -_dot_nt(dyb[:, cols], cit_ref[j])

        first = c == nc - 1
        row0 = lax.broadcasted_iota(jnp.int32, (SEGMENTS, 1), 0) == 0
        last = (seg - 1) * SEGMENTS
        for j in range(SUPER):
            states = slice(512 * j, 512 * (j + 1))
            _segment_scan(qr_s, qi_s, states, pw_r, pw_i, cr_s, ci_s, seg, True)

            before_r = jnp.where(first, 0.0, pltpu.roll(pr_ref[:, states], 1, 0))
            before_i = jnp.where(first, 0.0, pltpu.roll(pi_ref[:, states], 1, 0))
            hp_r = jnp.where(row0, before_r, pltpu.roll(sr_ref[pl.ds(last, SEGMENTS), states], 1, 0))
            hp_i = jnp.where(row0, before_i, pltpu.roll(si_ref[pl.ds(last, SEGMENTS), states], 1, 0))
            q_r, q_i = qr_s[pl.ds(0, SEGMENTS), states], qi_s[pl.ds(0, SEGMENTS), states]

            def dlam_step(jj, acc):
                o = pl.multiple_of(jj * SEGMENTS, SEGMENTS)
                above = pl.multiple_of((jj - 1) * SEGMENTS, SEGMENTS)
                h_r, h_i = sr_ref[pl.ds(above, SEGMENTS), states], si_ref[pl.ds(above, SEGMENTS), states]
                t_r, t_i = qr_s[pl.ds(o, SEGMENTS), states], qi_s[pl.ds(o, SEGMENTS), states]
                return acc[0] + t_r * h_r + t_i * h_i, acc[1] + t_i * h_r - t_r * h_i

            acc = lax.fori_loop(1, seg, dlam_step, (q_r * hp_r + q_i * hp_i, q_i * hp_r - q_r * hp_i), unroll=2)
            dlam_ref[0:SEGMENTS, states] += acc[0]
            dlam_ref[SEGMENTS:, states] += acc[1]

        du_c = []
        for j in range(SUPER):
            cols, states = slice(128 * j, 128 * (j + 1)), slice(512 * j, 512 * (j + 1))
            qr_j = qr_s[:, states].astype(BF16)
            qi_j = qi_s[:, states].astype(BF16)
            du_c.append(_dot_nt(qr_j, bbr_ref[j]) + _dot_nt(qi_j, bbi_ref[j]))
            dbbr_ref[j] += _dot_tn(ub[:, cols], qr_j)
            dbbi_ref[j] += _dot_tn(ub[:, cols], qi_j)
        du_ref[...] = _time_order(lanes, jnp.concatenate(du_c, axis=1) + dy * d_row, seg)

    rev = lambda c: nc - 1 - c
    chunk = lambda w: pl.BlockSpec((tc, w), lambda c: (rev(c), 0))
    lane_blocks = pl.BlockSpec((4, tc, 128), lambda c: (0, rev(c), 0))
    prev = pl.BlockSpec((SEGMENTS, N_STATE), lambda c: (jnp.maximum(rev(c) * seg - 1, 0), 0))
    sds = jax.ShapeDtypeStruct
    return pl.pallas_call(
        body, grid=(nc,), name="s5_bwd",
        in_specs=[lane_blocks, chunk(D_SSM), lane_blocks, chunk(N_STATE), chunk(N_STATE), prev, prev,
                  _full((8, N_STATE)), _full((SUPER, 128, 512)), _full((SUPER, 128, 512)),
                  _full((SUPER, 512, 128)), _full((SUPER, 512, 128)), _full((8, D_SSM)), _full((SUPER, 128, 128)), _ANY],
        out_specs=[chunk(D_SSM), _full((SUPER, 128, 512)), _full((SUPER, 128, 512)), _full((SUPER, 512, 128)),
                   _full((SUPER, 512, 128)), _full((SUPER, 128, 128)), _full((8, D_SSM)), _full((2 * SEGMENTS, N_STATE))],
        out_shape=[sds((n_pad, D_SSM), F32), sds((SUPER, 128, 512), F32), sds((SUPER, 128, 512), F32),
                   sds((SUPER, 512, 128), F32), sds((SUPER, 512, 128), F32), sds((SUPER, 128, 128), F32),
                   sds((8, D_SSM), F32), sds((2 * SEGMENTS, N_STATE), F32)],
        scratch_shapes=[pltpu.VMEM((tc, N_STATE), F32), pltpu.VMEM((tc, N_STATE), F32),
                        pltpu.VMEM((8, N_STATE), F32), pltpu.VMEM((8, N_STATE), F32),
                        pltpu.VMEM((tc, N_STATE), F32), pltpu.VMEM((tc, N_STATE), F32),
                        pltpu.VMEM((4, tc, 128), F32)],
        compiler_params=_params(("arbitrary",)),
    )(dms4, y, u4, sr, si, sr, si, lam, bbr, bbi, crt, cit, vecs, glu, token)


def _inv_count(c_idx, tc, w):
    t = c_idx * tc + lax.broadcasted_iota(jnp.int32, (tc, 1), 0)
    return 1.0 / jnp.minimum(t + 1, w).astype(F32)


def _pool_fwd(v, pw_b, vecs, tc, token):
    n_pad = v.shape[0]

    def body(v_ref, pw_ref, vec_ref, token_ref, feat_ref, mp_ref, hist):
        c = pl.program_id(0)

        @pl.when(c == 0)
        def _():
            hist[...] = jnp.zeros_like(hist)

        v_v = v_ref[...]
        ext = jnp.concatenate([hist[...], v_v], axis=0)
        hist[...] = v_v[tc - POOL_HALO:, :]
        feats, ps = [], []
        for k, w in enumerate(POOL_WINDOWS):
            cols = slice(128 * k, 128 * (k + 1))
            s = ext[:, cols]
            sh = 1
            while sh < w:
                s = s + pltpu.roll(s, sh, 0)
                sh *= 2
            f = (s[POOL_HALO:, :] * _inv_count(c, tc, w) - v_v[:, cols]).astype(BF16)
            feats.append(f)
            ps.append(_dot(f, pw_ref[k]))
        feat_ref[...] = jnp.concatenate(feats, axis=1)
        yp = jnp.concatenate(ps, axis=1) * vec_ref[0:1, :]
        mp_ref[...] = (yp * _rms(yp) * vec_ref[1:2, :]).astype(BF16)

    chunk = lambda w: pl.BlockSpec((tc, w), lambda c: (c, 0))
    sds = jax.ShapeDtypeStruct
    return pl.pallas_call(
        body, grid=(n_pad // tc,), name="pool_fwd",
        in_specs=[chunk(D_POOL), _full((4, 128, 128)), _full((8, D_POOL)), _ANY],
        out_specs=[chunk(D_POOL), chunk(D_POOL)],
        out_shape=[sds((n_pad, D_POOL), BF16), sds((n_pad, D_POOL), BF16)],
        scratch_shapes=[pltpu.VMEM((POOL_HALO, D_POOL), F32)],
        compiler_params=_params(("arbitrary",)),
    )(v, pw_b, vecs, token)


def _pool_bwd(dmp, feat, pw_b, vecs, tc):
    n_pad = dmp.shape[0]
    nc = n_pad // tc

    def body(dmp_ref, feat_ref, pw_ref, vec_ref, dv_ref, dpw_ref, dvec_ref, fut):
        c = pl.program_id(0)

        @pl.when(c == 0)
        def _():
            fut[...] = jnp.zeros_like(fut)
            dpw_ref[...] = jnp.zeros_like(dpw_ref)
            dvec_ref[...] = jnp.zeros_like(dvec_ref)

        scale, gp = vec_ref[0:1, :], vec_ref[1:2, :]
        feat_v = feat_ref[...]
        p = jnp.concatenate([_dot(feat_v[:, 128 * k:128 * (k + 1)], pw_ref[k]) for k in range(4)], axis=1)
        yp = p * scale
        r = _rms(yp)
        xhat = yp * r
        dm = dmp_ref[...]
        dyp = _rms_bwd(dm * gp, xhat, r)
        dvec_ref[0:1, :] += jnp.sum(dyp * p, axis=0, keepdims=True)
        dvec_ref[1:2, :] += jnp.sum(dm * xhat, axis=0, keepdims=True)
        dpb = (dyp * scale).astype(BF16)
        es, dfs = [], []
        for k, w in enumerate(POOL_WINDOWS):
            cols = slice(128 * k, 128 * (k + 1))
            dpw_ref[k] += _dot_tn(feat_v[:, cols], dpb[:, cols])
            df = _dot_nt(dpb[:, cols], pw_ref[k])
            dfs.append(df)
            es.append(df * _inv_count(nc - 1 - c, tc, w))
        e = jnp.concatenate(es, axis=1)
        ext = jnp.concatenate([e, fut[...]], axis=0)
        fut[...] = e[:POOL_HALO, :]
        n_ext = tc + POOL_HALO
        dvs = []
        for k, w in enumerate(POOL_WINDOWS):
            s = ext[:, 128 * k:128 * (k + 1)]
            sh = 1
            while sh < w:
                s = s + pltpu.roll(s, n_ext - sh, 0)
                sh *= 2
            dvs.append(s[:tc, :] - dfs[k])
        dv_ref[...] = jnp.concatenate(dvs, axis=1)

    chunk = lambda w: pl.BlockSpec((tc, w), lambda c: (nc - 1 - c, 0))
    sds = jax.ShapeDtypeStruct
    return pl.pallas_call(
        body, grid=(nc,), name="pool_bwd",
        in_specs=[chunk(D_POOL), chunk(D_POOL), _full((4, 128, 128)), _full((8, D_POOL))],
        out_specs=[chunk(D_POOL), _full((4, 128, 128)), _full((8, D_POOL))],
        out_shape=[sds((n_pad, D_POOL), F32), sds((4, 128, 128), F32), sds((8, D_POOL), F32)],
        scratch_shapes=[pltpu.VMEM((POOL_HALO, D_POOL), F32)],
        compiler_params=_params(("arbitrary",)),
    )(dmp, feat, pw_b, vecs)


def _place():
    x, y, c = lax.axis_index("x"), lax.axis_index("y"), lax.axis_index("c")
    chips = [(1 - x, y), (x, 1 - y), (1 - x, 1 - y)]
    return x, y, c, chips


_ANY = pl.BlockSpec(memory_space=pl.ANY)


def _cast_shards(shards, dtypes, place):
    n = len(shards)

    def body(place_ref, *refs):
        for i in range(n):
            refs[n + i][0] = refs[i][...].astype(dtypes[i])

    return pl.pallas_call(
        body, name="cast_shards",
        grid_spec=pltpu.PrefetchScalarGridSpec(
            num_scalar_prefetch=1, grid=(1,),
            in_specs=[pl.BlockSpec(s.shape, lambda i, p: (0, 0, 0)) for s in shards],
            out_specs=[pl.BlockSpec((1,) + s.shape, lambda i, p: (p[0], 0, 0, 0)) for s in shards]),
        out_shape=[jax.ShapeDtypeStruct((N_SHARD,) + s.shape, dt) for s, dt in zip(shards, dtypes)],
        compiler_params=_params(("arbitrary",)),
    )(place, *shards)


def _gather_shards(full):
    n = len(full)

    def body(*refs):
        outs = refs[n:2 * n]
        ici_send, ici_recv, d2d_send, d2d_recv = refs[2 * n:]
        x, y, c, chips = _place()
        q = 2 * x + y
        sibling = (x, y, 1 - c)

        def ici(i, j, shard, to):
            return pltpu.make_async_remote_copy(src_ref=outs[i].at[q, c], dst_ref=outs[i].at[shard, c],
                                                send_sem=ici_send.at[i, j], recv_sem=ici_recv.at[i, j],
                                                device_id=to, device_id_type=MESH)

        def d2d(i, j, shard, half):
            return pltpu.make_async_remote_copy(src_ref=outs[i].at[shard, c], dst_ref=outs[i].at[shard, half],
                                                send_sem=d2d_send.at[i, j], recv_sem=d2d_recv.at[i, j],
                                                device_id=sibling, device_id_type=MESH)

        sends = [ici(i, j, q, (*chip, c)) for i in range(n) for j, chip in enumerate(chips)]
        for cp in sends:
            cp.start()
        passed = []
        for i in range(n):
            for j, (cx, cy) in enumerate(chips):
                ici(i, j, 2 * cx + cy, (cx, cy, c)).wait_recv()
                cp = d2d(i, j, 2 * cx + cy, c)
                cp.start()
                passed.append(cp)
        for i in range(n):
            for j, (cx, cy) in enumerate(chips):
                d2d(i, j, 2 * cx + cy, 1 - c).wait_recv()
        for cp in sends + passed:
            cp.wait_send()

    return pl.pallas_call(
        body, name="gather_shards",
        in_specs=[_ANY] * n, out_specs=[_ANY] * n,
        out_shape=[jax.ShapeDtypeStruct(f.shape, f.dtype) for f in full],
        input_output_aliases={i: i for i in range(n)},
        scratch_shapes=[pltpu.SemaphoreType.DMA((n, 3)), pltpu.SemaphoreType.DMA((n, 3)),
                        pltpu.SemaphoreType.DMA((n, 3)), pltpu.SemaphoreType.DMA((n, 3))],
    )(*full)


_HBM = pl.BlockSpec(memory_space=pltpu.HBM)
_SEM = pl.BlockSpec(memory_space=pltpu.SEMAPHORE)
_EFFECT = pltpu.SideEffectType.DATAFLOW_SIDE_EFFECTING


def _copies_start(name, arrays, sem_shape, build, after=None):
    n = len(arrays)
    extra = [] if after is None else [after]

    def body(*refs):
        outs = refs[n + len(extra):2 * n + len(extra)]
        send, recv, token = refs[2 * n + len(extra):]
        sends, _ = build(outs, send, recv)
        for cp in sends:
            cp.start()
        token[...] = jnp.zeros_like(token)

    out = pl.pallas_call(
        body, name=name, in_specs=[_HBM] * n + [_ANY] * len(extra),
        out_specs=[_HBM] * n + [_SEM, _SEM, pl.BlockSpec(memory_space=pltpu.VMEM)],
        out_shape=[pltpu.HBM(a.shape, a.dtype) for a in arrays]
        + [pltpu.SemaphoreType.DMA(sem_shape), pltpu.SemaphoreType.DMA(sem_shape), jax.ShapeDtypeStruct((8, 128), F32)],
        input_output_aliases={i: i for i in range(n)},
        compiler_params=pltpu.CompilerParams(has_side_effects=_EFFECT),
    )(*[pltpu.with_memory_space_constraint(a, pltpu.HBM) for a in arrays], *extra)
    return list(out[:n]), (out[n], out[n + 1]), out[n + 2]


def _copies_wait(name, arrays, sems, after, build):
    n = len(arrays)

    def body(*refs):
        ins = refs[:n]
        send, recv = refs[n], refs[n + 1]
        sends, recvs = build(ins, send, recv)
        for cp in sends:
            cp.wait_send()
        for cp in recvs:
            cp.wait_recv()

    return list(pl.pallas_call(
        body, name=name, in_specs=[_HBM] * n + [_SEM, _SEM] + [_ANY] * len(after), out_specs=[_HBM] * n,
        out_shape=[pltpu.HBM(a.shape, a.dtype) for a in arrays],
        input_output_aliases={i: i for i in range(n)},
        compiler_params=pltpu.CompilerParams(has_side_effects=_EFFECT),
    )(*arrays, *sems, *after))


def _remote(src, dst, send_sem, recv_sem, to):
    return pltpu.make_async_remote_copy(src_ref=src, dst_ref=dst, send_sem=send_sem, recv_sem=recv_sem,
                                        device_id=to, device_id_type=MESH)


def _build_gather(refs, send, recv):
    x, y, c, chips = _place()
    q = 2 * x + y
    pairs = [(i, j, chip) for i in range(len(refs)) for j, chip in enumerate(chips)]
    sends = [_remote(refs[i].at[q, c], refs[i].at[q, c], send.at[3 * i + j], recv.at[3 * i + j], (cx, cy, c))
             for i, j, (cx, cy) in pairs]
    recvs = [_remote(refs[i].at[q, c], refs[i].at[2 * cx + cy, c], send.at[3 * i + j], recv.at[3 * i + j], (cx, cy, c))
             for i, j, (cx, cy) in pairs]
    return sends, recvs


def _build_forward(refs, send, recv):
    x, y, c, chips = _place()
    pairs = [(i, j, 2 * cx + cy) for i in range(len(refs)) for j, (cx, cy) in enumerate(chips)]
    sends = [_remote(refs[i].at[s, c], refs[i].at[s, c], send.at[3 * i + j], recv.at[3 * i + j], (x, y, 1 - c))
             for i, j, s in pairs]
    recvs = [_remote(refs[i].at[s, c], refs[i].at[s, 1 - c], send.at[3 * i + j], recv.at[3 * i + j], (x, y, 1 - c))
             for i, j, s in pairs]
    return sends, recvs


def _build_swap(refs, send, recv):
    x, y, c, _ = _place()
    n = len(refs) // 2
    cps = [_remote(refs[i].at[:, 1 - c], refs[n + i], send.at[i], recv.at[i], (x, y, 1 - c)) for i in range(n)]
    return cps, cps


def _build_exchange(refs, send, recv):
    x, y, c, chips = _place()
    n = len(refs) // 2
    cps = [_remote(refs[i].at[2 * cx + cy], refs[n + i].at[j], send.at[3 * i + j], recv.at[3 * i + j], (cx, cy, c))
           for i in range(n) for j, (cx, cy) in enumerate(chips)]
    return cps, cps


def _build_spread(refs, send, recv):
    x, y, c, _ = _place()
    flip = lambda bit, on: bit + on - 2 * bit * on
    cps = [_remote(refs[0], refs[1].at[r - 1], send.at[r - 1], recv.at[r - 1],
                   (flip(x, r >> 2 & 1), flip(y, r >> 1 & 1), flip(c, r & 1))) for r in (1, 2, 4, 6)]
    return cps, cps


def _forward_small(landed):
    def body(in_ref, out_ref, send, recv):
        x, y, c, _ = _place()
        cps = [_remote(out_ref.at[r - 1], out_ref.at[r], send.at[k], recv.at[k], (x, y, 1 - c))
               for k, r in enumerate((2, 4, 6))]
        for cp in cps:
            cp.start()
        for cp in cps:
            cp.wait()

    return pl.pallas_call(
        body, name="forward_small",
        in_specs=[_ANY], out_specs=_ANY, out_shape=jax.ShapeDtypeStruct(landed.shape, F32),
        input_output_aliases={0: 0},
        scratch_shapes=[pltpu.SemaphoreType.DMA((3,)), pltpu.SemaphoreType.DMA((3,))],
    )(landed)


def _swap_halves(grads):
    n = len(grads)

    def body(*refs):
        ins, outs = refs[:n], refs[n:2 * n]
        send, recv = refs[2 * n:]
        x, y, c, _ = _place()
        cps = [pltpu.make_async_remote_copy(src_ref=ins[i].at[:, 1 - c], dst_ref=outs[i], send_sem=send.at[i],
                                            recv_sem=recv.at[i], device_id=(x, y, 1 - c), device_id_type=MESH)
               for i in range(n)]
        for cp in cps:
            cp.start()
        for cp in cps:
            cp.wait()

    return pl.pallas_call(
        body, name="swap_halves",
        in_specs=[_ANY] * n, out_specs=[_ANY] * n,
        out_shape=[jax.ShapeDtypeStruct((N_SHARD,) + g.shape[2:], F32) for g in grads],
        scratch_shapes=[pltpu.SemaphoreType.DMA((n,)), pltpu.SemaphoreType.DMA((n,))],
    )(*grads)


def _join_halves(pairs):
    n = len(pairs)

    def body(*refs):
        outs = refs[n:2 * n]
        send, recv = refs[2 * n:]
        x, y, c, _ = _place()
        cps = [pltpu.make_async_remote_copy(src_ref=outs[i].at[c], dst_ref=outs[i].at[c], send_sem=send.at[i],
                                            recv_sem=recv.at[i], device_id=(x, y, 1 - c), device_id_type=MESH)
               for i in range(n)]
        for cp in cps:
            cp.start()
        for i in range(n):
            cps[i].wait_send()
            pltpu.make_async_remote_copy(src_ref=outs[i].at[c], dst_ref=outs[i].at[1 - c], send_sem=send.at[i],
                                         recv_sem=recv.at[i], device_id=(x, y, 1 - c), device_id_type=MESH).wait_recv()

    return pl.pallas_call(
        body, name="join_halves",
        in_specs=[_ANY] * n, out_specs=[_ANY] * n,
        out_shape=[jax.ShapeDtypeStruct(p.shape, F32) for p in pairs],
        input_output_aliases={i: i for i in range(n)},
        scratch_shapes=[pltpu.SemaphoreType.DMA((n,)), pltpu.SemaphoreType.DMA((n,))],
    )(*pairs)


N_SPLIT = 2


def _sum_siblings(tag, grads, recvd, place):
    n = len(grads)

    def body(place_ref, *refs):
        g_refs, r_refs, sb_refs, own_refs = (refs[k * n:(k + 1) * n] for k in range(4))
        s = pl.program_id(1)
        for i in range(n):
            tot = g_refs[i][0, 0] + r_refs[i][0]
            sb_refs[i][0] = tot.astype(BF16)

            @pl.when(s == place_ref[0])
            def _():
                own_refs[i][...] = tot

    in_specs, sb_specs, own_specs, sb_shapes, own_shapes = [], [], [], [], []
    for g in grads:
        _, _, r, cdim = g.shape
        rb = r // N_SPLIT
        in_specs.append(pl.BlockSpec((1, 1, rb, cdim), lambda b, s, p: (s, p[1], b, 0)))
        sb_specs.append(pl.BlockSpec((1, rb, cdim), lambda b, s, p: (s, b, 0)))
        own_specs.append(pl.BlockSpec((rb, cdim), lambda b, s, p: (b, 0)))
        sb_shapes.append(jax.ShapeDtypeStruct((N_SHARD, r, cdim), BF16))
        own_shapes.append(jax.ShapeDtypeStruct((r, cdim), F32))
    out = pl.pallas_call(
        body, name="sum_siblings_" + tag,
        grid_spec=pltpu.PrefetchScalarGridSpec(
            num_scalar_prefetch=1, grid=(N_SPLIT, N_SHARD),
            in_specs=in_specs + sb_specs, out_specs=sb_specs + own_specs),
        out_shape=sb_shapes + own_shapes,
        compiler_params=_params(("parallel", "arbitrary")),
    )(place, *grads, *recvd)
    return out[:n], out[n:]


def _sum_chips(own, recvd, place):
    n = len(own)

    def body(place_ref, *refs):
        o_refs, r_refs, out_refs = (refs[k * n:(k + 1) * n] for k in range(3))
        for i in range(n):
            tot = o_refs[i][...]
            for j in range(3):
                tot = tot + r_refs[i][j].astype(F32)
            out_refs[i][0] = tot

    o_specs, r_specs, out_specs = [], [], []
    for o in own:
        r, cdim = o.shape
        rb = r // N_SPLIT
        o_specs.append(pl.BlockSpec((rb, cdim), lambda b, p: (b, 0)))
        r_specs.append(pl.BlockSpec((3, rb, cdim), lambda b, p: (0, b, 0)))
        out_specs.append(pl.BlockSpec((1, rb, cdim), lambda b, p: (p[1], b, 0)))
    return pl.pallas_call(
        body, name="sum_chips",
        grid_spec=pltpu.PrefetchScalarGridSpec(num_scalar_prefetch=1, grid=(N_SPLIT,),
                                               in_specs=o_specs + r_specs, out_specs=out_specs),
        out_shape=[jax.ShapeDtypeStruct((2,) + o.shape, F32) for o in own],
        compiler_params=_params(("parallel",)),
    )(place, *own, *recvd)


def _adamw_math(w, g, m, v):
    m = ADAM_B1 * m + (1.0 - ADAM_B1) * g
    v = ADAM_B2 * v + (1.0 - ADAM_B2) * (g * g)
    m_hat = m / (1.0 - ADAM_B1 ** ADAM_STEP)
    v_hat = v / (1.0 - ADAM_B2 ** ADAM_STEP)
    delta = -ADAM_LR * (m_hat / (jnp.sqrt(v_hat) + ADAM_EPS) + ADAM_WD * w)
    return delta, m, v


def _adamw(name, ws, gs, ms, vs, n_split):
    n = len(ws)

    def body(*refs):
        w_r, g_r, m_r, v_r, d_o, m_o, v_o = (refs[k * n:(k + 1) * n] for k in range(7))
        for i in range(n):
            d, m, v = _adamw_math(w_r[i][...], g_r[i][...], m_r[i][...], v_r[i][...])
            d_o[i][...] = d
            m_o[i][...] = m
            v_o[i][...] = v

    specs = [pl.BlockSpec((w.shape[0] // n_split, w.shape[1]), lambda b: (b, 0)) for w in ws]
    shapes = [jax.ShapeDtypeStruct(w.shape, F32) for w in ws]
    out = pl.pallas_call(
        body, name=name, grid=(n_split,),
        in_specs=specs * 4, out_specs=specs * 3, out_shape=shapes * 3,
        compiler_params=_params(("parallel",)),
    )(*ws, *gs, *ms, *vs)
    return out[:n], out[n:2 * n], out[2 * n:]


def _reduce_small(own, received):
    def body(own_ref, recv_ref, g_out):
        me = 4 * lax.axis_index("x") + 2 * lax.axis_index("y") + lax.axis_index("c")
        g = None
        for k in range(8):
            mine = me == k
            part = jnp.where(mine, own_ref[...], recv_ref[jnp.where(mine, 0, jnp.bitwise_xor(me, k) - 1)])
            g = part if g is None else g + part
        g_out[...] = g

    return pl.pallas_call(
        body, name="reduce_small",
        out_shape=jax.ShapeDtypeStruct(own.shape, F32),
        compiler_params=_params(),
    )(own, received)


def _adamw_small(ws, gs, ms, vs):
    n = len(ws)

    def body(*refs):
        w_r, g_r, m_r, v_r, d_o, m_o, v_o = (refs[k * n:(k + 1) * n] for k in range(7))
        for i in range(n):
            d, mm, vv = _adamw_math(w_r[i][...], g_r[i][...], m_r[i][...], v_r[i][...])
            d_o[i][...] = d
            m_o[i][...] = mm
            v_o[i][...] = vv

    out = pl.pallas_call(
        body, name="adamw_small",
        out_shape=[jax.ShapeDtypeStruct(t.shape, F32) for t in ws] * 3,
        compiler_params=_params(),
    )(*ws, *gs, *ms, *vs)
    return out[:n], out[n:2 * n], out[2 * n:]


def _s5_operands(lam_re, lam_im, log_step, b_re, b_im, c_re, c_im, glu_w):
    lr = jnp.minimum(lam_re, -1e-4)
    li = lam_im
    step = jnp.exp(log_step)[:, None]
    mag = jnp.exp(lr * step)
    ang = li * step
    abr = mag * jnp.cos(ang)
    abi = mag * jnp.sin(ang)
    nr = abr - 1.0
    ni = abi
    den = lr * lr + li * li
    cr = ((nr * lr + ni * li) / den)[..., None]
    ci = ((ni * lr - nr * li) / den)[..., None]
    bbr = cr * b_re - ci * b_im
    bbi = cr * b_im + ci * b_re
    eye = jnp.eye(8, dtype=F32)
    g, h, p = SSM_GROUPS // SUPER, SSM_GROUP, SSM_STATE

    def b_layout(t):
        return jnp.einsum("ab,japh->jahbp", eye, t.reshape(SUPER, g, p, h)).reshape(SUPER, g * h, g * p)

    def c_layout(t):
        return jnp.einsum("ab,jahp->jbpah", eye, t.reshape(SUPER, g, h, p)).reshape(SUPER, g * p, g * h)

    glu = jnp.einsum("ab,jahk->jahbk", eye, glu_w.reshape(SUPER, g, h, h)).reshape(SUPER, g * h, g * h)
    lam = _pad_rows(jnp.concatenate([abr.reshape(1, N_STATE), abi.reshape(1, N_STATE)], axis=0), 8)
    return lam, b_layout(bbr), b_layout(bbi), c_layout(c_re), c_layout(c_im), glu


def _pad_rows(a, rows):
    return jnp.pad(a, ((0, rows - a.shape[0]), (0, 0)))


def _pack(parts):
    rows = []
    for a in parts:
        flat = a.reshape(-1)
        n = -(-flat.shape[0] // 128)
        rows.append(jnp.pad(flat, (0, n * 128 - flat.shape[0])).reshape(n, 128))
    out = jnp.concatenate(rows, axis=0)
    return _pad_rows(out, -(-out.shape[0] // 8) * 8)


def _unpack(packed, like):
    out, at = [], 0
    for a in like:
        n = -(-a.size // 128)
        out.append(packed[at:at + n].reshape(-1)[:a.size].reshape(a.shape))
        at += n
    return out


STORED = {"ssm_b_re": (0, 1, 3, 2), "ssm_b_im": (0, 1, 3, 2), "ssm_d": (0, 2, 1), "ssm_glu_b": (0, 2, 1),
          "ssm_glu_w": (0, 2, 3, 1)}


def _stored(k, a):
    return a.transpose(STORED[k]) if k in STORED else a


def _logical(k, a):
    return a.transpose(tuple(STORED[k].index(i) for i in range(a.ndim))) if k in STORED else a


SMALL = ("norm1_g", "ssm_lambda_re", "ssm_lambda_im", "ssm_log_step", "ssm_b_re", "ssm_b_im", "ssm_c_re", "ssm_c_im",
         "ssm_d", "ssm_glu_w", "ssm_glu_b", "ssm_norm_g", "pool_w", "pool_scale", "pool_norm_g", "norm2_g",
         "final_norm_g")
LARGE = ("w_in", "w_out", "w_gate", "w_up", "w_down")
WEIGHTS = ("meta_tokens", "norm1_g", "w_in", "ssm_lambda_re", "ssm_lambda_im", "ssm_log_step", "ssm_b_re", "ssm_b_im",
           "ssm_c_re", "ssm_c_im", "ssm_d", "ssm_glu_w", "ssm_glu_b", "ssm_norm_g", "pool_w", "pool_scale",
           "pool_norm_g", "w_out", "norm2_g", "w_gate", "w_up", "w_down", "final_norm_g")


def _step(x, target, w, m, v):
    seq = x.shape[1]
    n_rows = N_META + seq
    n_pad, tm, tp, tc, tg = _plan(n_rows)
    xq, yq, cq = lax.axis_index("x"), lax.axis_index("y"), lax.axis_index("c")
    place = jnp.stack([2 * xq + yq, cq]).astype(jnp.int32)

    def halves(a2d):
        return a2d.reshape(2, a2d.shape[0] // 2, a2d.shape[1])

    def local2d(t):
        return {"w_gate": lambda a: a[0].T, "w_up": lambda a: a[0].T}.get(t, lambda a: a[0])

    shards = [halves(local2d(k)(w[k])) for k in LARGE] + [halves(w["meta_tokens"])]
    full = _cast_shards(shards, [BF16] * len(LARGE) + [F32], place)
    w_in_full, meta_full = _gather_shards([full[0], full[5]])
    late, gather_sems, gather_token = _copies_start("gather_start", list(full[1:5]), (12,), _build_gather,
                                                    after=w_in_full)
    w_in_b = w_in_full.reshape(D_MODEL, D_MODEL)
    meta = meta_full.reshape(N_SHARD, N_META, D_MODEL // N_SHARD).transpose(1, 0, 2).reshape(N_META, D_MODEL)

    h0 = _pad_rows(jnp.concatenate([meta, x[0]], axis=0), n_pad)
    tgt = _pad_rows(jnp.concatenate([jnp.zeros((N_META, D_MODEL), F32), target[0]], axis=0), n_pad)
    s5_in = (w["ssm_lambda_re"][0], w["ssm_lambda_im"][0], w["ssm_log_step"][0], w["ssm_b_re"][0], w["ssm_b_im"][0],
             w["ssm_c_re"][0], w["ssm_c_im"][0], w["ssm_glu_w"][0])
    (lam, bbr, bbi, crt, cit, glu), s5_vjp = jax.vjp(_s5_operands, *s5_in)
    bbr_b, bbi_b, crt_b, cit_b, glu_b16 = (t.astype(BF16) for t in (bbr, bbi, crt, cit, glu))
    s5_vecs = _pad_rows(jnp.concatenate([w["ssm_d"].reshape(1, D_SSM), w["ssm_glu_b"].reshape(1, D_SSM),
                                         w["ssm_norm_g"].reshape(1, D_SSM)], axis=0), 8)
    pool_vecs = _pad_rows(jnp.concatenate([w["pool_scale"].reshape(1, D_POOL), w["pool_norm_g"].reshape(1, D_POOL)],
                                          axis=0), 8)
    pw_b = w["pool_w"][0].astype(BF16)
    g1, g2, gf = w["norm1_g"].reshape(1, D_MODEL), w["norm2_g"].reshape(1, D_MODEL), w["final_norm_g"].reshape(1, D_MODEL)

    u, vv = _fwd_in(h0, g1, w_in_b, tp, gather_token)
    sr, si, y, ms = _s5_fwd(u, lam, bbr_b, bbi_b, crt_b, cit_b, s5_vecs, glu_b16, tc)
    late = _copies_wait("gather_wait", late, gather_sems, [ms], _build_gather)
    late, forward_sems, forward_token = _copies_start("forward_start", late, (12,), _build_forward)
    feat, mp = _pool_fwd(vv, pw_b, pool_vecs, tc, forward_token)
    late = _copies_wait("forward_wait", late, forward_sems, [mp], _build_forward)
    w_out_b = late[0].reshape(D_MODEL, D_MODEL)
    wg_b, wu_b, wd_b = (t.reshape(N_SHARD, FF_SHARD, D_MODEL) for t in late[1:])
    h1, n2, a, b, ff, dh2, dh2b, loss_acc, dgf = _fwd_ffn(h0, ms, mp, w_out_b, g2, wg_b, wu_b, wd_b, gf, tgt, tc, n_rows)

    def quarters(t):
        if t.ndim == 2:
            t = t.reshape(N_SHARD, t.shape[0] // N_SHARD, t.shape[1])
        return t.reshape(N_SHARD, 2, t.shape[1] // 2, t.shape[2])

    def landing(like, lead, dtype):
        return [lax.empty((lead,) + t.shape[2:], dtype) for t in like]

    da, db, dh1, dg2 = _bwd_ffn(dh2, dh2b, a, b, wg_b, wu_b, wd_b, h1, g2, tc)
    ffn_g = [quarters(t) for t in _grad_ffn(n2, da, db, ff, dh2b, tg)]
    nf = len(ffn_g)
    moved, swap_sems, swap_token = _copies_start("swap_start", ffn_g + landing(ffn_g, N_SHARD, F32), (nf,), _build_swap)
    dms, dmp, dwo = _bwd_out(dh1, ms, mp, w_out_b, tp, swap_token)
    moved = _copies_wait("swap_wait", moved, swap_sems, [dwo], _build_swap)
    ffn_parts, ffn_own = _sum_siblings("ffn", moved[:nf], moved[nf:], place)
    moved, exch_sems, exch_token = _copies_start("exchange_start", list(ffn_parts) + landing(ffn_g, 3, BF16), (3 * nf,),
                                                 _build_exchange)
    du, dbbr, dbbi, dcrt, dcit, dglu, ds5v, dlam = _s5_bwd(dms, y, u, sr, si, lam, bbr_b, bbi_b, crt_b, cit_b,
                                                           s5_vecs, glu_b16, tc, exch_token)
    dv, dpw, dpoolv = _pool_bwd(dmp, feat, pw_b, pool_vecs, tc)
    dh0, dwi, dg1 = _bwd_in(du, dv, h0, dh1, g1, w_in_b, tp)
    ffn_from_chips = _copies_wait("exchange_wait", moved, exch_sems, [dh0], _build_exchange)[nf:]
    dlam = _pad_rows(jnp.concatenate([jnp.sum(dlam[:SEGMENTS], axis=0, keepdims=True),
                                      jnp.sum(dlam[SEGMENTS:], axis=0, keepdims=True)], axis=0), 8)
    d_lre, d_lim, d_lstep, d_bre, d_bim, d_cre, d_cim, d_gluw = s5_vjp((dlam, dbbr, dbbi, dcrt, dcit, dglu))
    grad_x = dh0[N_META:n_rows][None]

    small_g = {
        "norm1_g": dg1, "ssm_lambda_re": d_lre, "ssm_lambda_im": d_lim, "ssm_log_step": d_lstep, "ssm_b_re": d_bre,
        "ssm_b_im": d_bim, "ssm_c_re": d_cre, "ssm_c_im": d_cim, "ssm_d": ds5v[0], "ssm_glu_w": d_gluw,
        "ssm_glu_b": ds5v[1], "ssm_norm_g": ds5v[2], "pool_w": dpw, "pool_scale": dpoolv[0], "pool_norm_g": dpoolv[1],
        "norm2_g": dg2, "final_norm_g": dgf,
    }
    like = [_stored(k, w[k]) for k in SMALL]
    packed_g = _pack([_stored(k, small_g[k].reshape(w[k].shape)) for k in SMALL] + [dh0[:N_META], loss_acc[0:1, 0:1]])

    mix_g = [quarters(t) for t in (dwi, dwo)]
    mix_parts, mix_own = _sum_siblings("mix", mix_g, _swap_halves(mix_g), place)
    moved, mix_sems, mix_token = _copies_start("mix_exchange_start", list(mix_parts) + landing(mix_g, 3, BF16),
                                               (3 * len(mix_g),), _build_exchange)
    spread, small_sems, small_token = _copies_start(
        "small_start", [packed_g, lax.empty((7,) + packed_g.shape, F32)], (7,), _build_spread, after=mix_token)
    mix_from_chips = _copies_wait("mix_exchange_wait", moved, mix_sems, [small_token], _build_exchange)[len(mix_g):]
    joined = _join_halves(_sum_chips(list(mix_own) + list(ffn_own), list(mix_from_chips) + list(ffn_from_chips), place))
    g_large = [j.reshape(j.shape[0] * j.shape[1], j.shape[2]) for j in joined]
    w2d, m2d, v2d = ([local2d(k)(t[k]) for k in LARGE] for t in (w, m, v))
    d_large, m_large, v_large = _adamw("adamw_large", w2d, g_large, m2d, v2d, 8)

    own_g, landed = _copies_wait("small_wait", spread, small_sems, [d_large[0]], _build_spread)
    g_pk = _reduce_small(own_g, _forward_small(landed))
    g_small = _unpack(g_pk, like + [jax.ShapeDtypeStruct((N_META, D_MODEL), F32), jax.ShapeDtypeStruct((1, 1), F32)])
    loss = g_small.pop()[0, 0]
    rows2d = lambda t: t.reshape(1, -1) if t.ndim == 1 else t
    d_small, m_small, v_small = _adamw_small(*([rows2d(t) for t in ts] for ts in (
        like, g_small[:-1], [_stored(k, m[k]) for k in SMALL], [_stored(k, v[k]) for k in SMALL])))
    g_small, d_small, m_small, v_small = ([_logical(k, t.reshape(a.shape)) for t, a, k in zip(ts, like, SMALL)] + ts[len(SMALL):]
                                          for ts in (g_small, list(d_small), list(m_small), list(v_small)))
    q = place[0]
    g_meta = lax.dynamic_slice_in_dim(g_small[-1], q * (D_MODEL // N_SHARD), D_MODEL // N_SHARD, axis=1)
    d_meta, m_meta, v_meta = _adamw("adamw_meta", [w["meta_tokens"]], [g_meta], [m["meta_tokens"]],
                                    [v["meta_tokens"]], 1)

    grads, deltas, new_m, new_v = {}, {}, {}, {}
    for i, k in enumerate(SMALL):
        grads[k], deltas[k], new_m[k], new_v[k] = g_small[i], d_small[i], m_small[i], v_small[i]
    for i, k in enumerate(LARGE):
        back = (lambda t: t.T[None]) if k in ("w_gate", "w_up") else (lambda t: t[None])
        grads[k], deltas[k], new_m[k], new_v[k] = (back(t) for t in (g_large[i], d_large[i], m_large[i], v_large[i]))
    grads["meta_tokens"], deltas["meta_tokens"] = g_meta, d_meta[0]
    new_m["meta_tokens"], new_v["meta_tokens"] = m_meta[0], v_meta[0]
    return (loss, grad_x, *[grads[k] for k in WEIGHTS], *[deltas[k] for k in WEIGHTS],
            *[new_m[k] for k in WEIGHTS], *[new_v[k] for k in WEIGHTS])


def kernel(x, meta_tokens, norm1_g, w_in, ssm_lambda_re, ssm_lambda_im, ssm_log_step, ssm_b_re, ssm_b_im, ssm_c_re, ssm_c_im, ssm_d, ssm_glu_w, ssm_glu_b, ssm_norm_g, pool_w, pool_scale, pool_norm_g, w_out, norm2_g, w_gate, w_up, w_down, final_norm_g, loss_target, m_meta_tokens, m_norm1_g, m_w_in, m_ssm_lambda_re, m_ssm_lambda_im, m_ssm_log_step, m_ssm_b_re, m_ssm_b_im, m_ssm_c_re, m_ssm_c_im, m_ssm_d, m_ssm_glu_w, m_ssm_glu_b, m_ssm_norm_g, m_pool_w, m_pool_scale, m_pool_norm_g, m_w_out, m_norm2_g, m_w_gate, m_w_up, m_w_down, m_final_norm_g, v_meta_tokens, v_norm1_g, v_w_in, v_ssm_lambda_re, v_ssm_lambda_im, v_ssm_log_step, v_ssm_b_re, v_ssm_b_im, v_ssm_c_re, v_ssm_c_im, v_ssm_d, v_ssm_glu_w, v_ssm_glu_b, v_ssm_norm_g, v_pool_w, v_pool_scale, v_pool_norm_g, v_w_out, v_norm2_g, v_w_gate, v_w_up, v_w_down, v_final_norm_g):
    w = dict(meta_tokens=meta_tokens, norm1_g=norm1_g, w_in=w_in, ssm_lambda_re=ssm_lambda_re, ssm_lambda_im=ssm_lambda_im, ssm_log_step=ssm_log_step, ssm_b_re=ssm_b_re, ssm_b_im=ssm_b_im, ssm_c_re=ssm_c_re, ssm_c_im=ssm_c_im, ssm_d=ssm_d, ssm_glu_w=ssm_glu_w, ssm_glu_b=ssm_glu_b, ssm_norm_g=ssm_norm_g, pool_w=pool_w, pool_scale=pool_scale, pool_norm_g=pool_norm_g, w_out=w_out, norm2_g=norm2_g, w_gate=w_gate, w_up=w_up, w_down=w_down, final_norm_g=final_norm_g)
    m = dict(meta_tokens=m_meta_tokens, norm1_g=m_norm1_g, w_in=m_w_in, ssm_lambda_re=m_ssm_lambda_re, ssm_lambda_im=m_ssm_lambda_im, ssm_log_step=m_ssm_log_step, ssm_b_re=m_ssm_b_re, ssm_b_im=m_ssm_b_im, ssm_c_re=m_ssm_c_re, ssm_c_im=m_ssm_c_im, ssm_d=m_ssm_d, ssm_glu_w=m_ssm_glu_w, ssm_glu_b=m_ssm_glu_b, ssm_norm_g=m_ssm_norm_g, pool_w=m_pool_w, pool_scale=m_pool_scale, pool_norm_g=m_pool_norm_g, w_out=m_w_out, norm2_g=m_norm2_g, w_gate=m_w_gate, w_up=m_w_up, w_down=m_w_down, final_norm_g=m_final_norm_g)
    v = dict(meta_tokens=v_meta_tokens, norm1_g=v_norm1_g, w_in=v_w_in, ssm_lambda_re=v_ssm_lambda_re, ssm_lambda_im=v_ssm_lambda_im, ssm_log_step=v_ssm_log_step, ssm_b_re=v_ssm_b_re, ssm_b_im=v_ssm_b_im, ssm_c_re=v_ssm_c_re, ssm_c_im=v_ssm_c_im, ssm_d=v_ssm_d, ssm_glu_w=v_ssm_glu_w, ssm_glu_b=v_ssm_glu_b, ssm_norm_g=v_ssm_norm_g, pool_w=v_pool_w, pool_scale=v_pool_scale, pool_norm_g=v_pool_norm_g, w_out=v_w_out, norm2_g=v_norm2_g, w_gate=v_w_gate, w_up=v_w_up, w_down=v_w_down, final_norm_g=v_final_norm_g)
    return _step(x, loss_target, w, m, v)
```

```python
import functools
import math

import jax
import jax.numpy as jnp
from jax import lax
from jax.experimental import pallas as pl
from jax.experimental.pallas import tpu as pltpu

F32 = jnp.float32
BF16 = jnp.bfloat16
MESH = pl.DeviceIdType.MESH
AXES = ("x", "y", "c")

D_MODEL = 1024
D_SSM = 512
D_POOL = 512
N_META = 16
SSM_GROUP = 16
SSM_GROUPS = 32
SSM_STATE = 64
N_STATE = SSM_GROUPS * SSM_STATE
STATE_BLOCKS = N_STATE // 128
SUPER = 4
POOL_WINDOWS = (2, 4, 8, 16)
POOL_HALO = 16
D_FF = 2816
N_SHARD = 4
FF_SHARD = D_FF // N_SHARD
EPS = 1e-6
ADAM_LR, ADAM_B1, ADAM_B2, ADAM_EPS, ADAM_WD, ADAM_STEP = 0.001, 0.9, 0.999, 1e-08, 0.01, 10
VMEM_LIMIT = 56 * 1024 * 1024


def _plan(n_rows):
    if n_rows > 2048:
        tm, tp, tc, tg = 416, 832, 320, 1040
    else:
        tm, tp, tc, tg = 128, 128, 64, 128
    step = math.lcm(tm, tp, tc, tg)
    return -(-n_rows // step) * step, tm, tp, tc, tg


def _params(sem=None):
    return pltpu.CompilerParams(dimension_semantics=sem, vmem_limit_bytes=VMEM_LIMIT)


def _dot(a, b):
    return jnp.dot(a, b, preferred_element_type=F32)


def _dot_nt(a, b):
    return lax.dot_general(a, b, (((1,), (1,)), ((), ())), preferred_element_type=F32)


def _dot_tn(a, b):
    return lax.dot_general(a, b, (((0,), (0,)), ((), ())), preferred_element_type=F32)


def _sigmoid(x):
    return 0.5 * jnp.tanh(0.5 * x) + 0.5


_GELU_C = math.sqrt(2.0 / math.pi)


def _gelu_and_grad(y):
    y2 = y * y
    t = jnp.tanh(_GELU_C * (y + 0.044715 * y * y2))
    g = 0.5 * y * (1.0 + t)
    dg = 0.5 * (1.0 + t) + 0.5 * y * (1.0 - t * t) * (_GELU_C * (1.0 + 3.0 * 0.044715 * y2))
    return g, dg


def _rms(x):
    return lax.rsqrt(jnp.mean(x * x, axis=-1, keepdims=True) + EPS)


def _rms_bwd(dn, xhat, r):
    return r * (dn - xhat * jnp.mean(dn * xhat, axis=-1, keepdims=True))


def _full(shape):
    nd = len(shape)
    return pl.BlockSpec(shape, lambda *_: (0,) * nd)


def _fwd_in(h0, g1, w_in_b, tm, token):
    n_pad = h0.shape[0]

    def body(h_ref, g_ref, w_ref, token_ref, u_ref, v_ref):
        h = h_ref[...]
        n1 = (h * _rms(h) * g_ref[...]).astype(BF16)
        proj = _dot(n1, w_ref[...])
        for i in range(4):
            u_ref[i] = proj[:, 128 * i:128 * (i + 1)]
        v_ref[...] = proj[:, D_SSM:]

    row = lambda w: pl.BlockSpec((tm, w), lambda i: (i, 0))
    return pl.pallas_call(
        body, grid=(n_pad // tm,), name="fwd_in",
        in_specs=[row(D_MODEL), _full((1, D_MODEL)), _full((D_MODEL, D_MODEL)), _ANY],
        out_specs=[pl.BlockSpec((4, tm, 128), lambda i: (0, i, 0)), row(D_POOL)],
        out_shape=[jax.ShapeDtypeStruct((4, n_pad, 128), F32), jax.ShapeDtypeStruct((n_pad, D_POOL), F32)],
        compiler_params=_params(("parallel",)),
    )(h0, g1, w_in_b, token)


def _fwd_ffn(h0, ms, mp, w_out_b, g2, wg_b, wu_b, wd_b, gf, target, tm, n_valid):
    n_pad = h0.shape[0]
    nt = n_pad // tm

    def body(h0_ref, ms_ref, mp_ref, wo_ref, g2_ref, wg_hbm, wu_hbm, wd_hbm, gf_ref, tgt_ref,
             h1_ref, n2_ref, a_ref, b_ref, ff_ref, dh2_ref, dh2b_ref, loss_ref, dgf_ref,
             wg_ref, wu_ref, wd_ref, sems, acc):
        i, phase = pl.program_id(0), pl.program_id(1)

        @pl.when((i == 0) & (phase == 0))
        def _():
            loss_ref[...] = jnp.zeros_like(loss_ref)
            dgf_ref[...] = jnp.zeros_like(dgf_ref)
            cps = [pltpu.make_async_copy(src, dst, sems.at[k])
                   for k, (src, dst) in enumerate(((wg_hbm, wg_ref), (wu_hbm, wu_ref), (wd_hbm, wd_ref)))]
            for cp in cps:
                cp.start()
            for cp in cps:
                cp.wait()

        @pl.when(phase == 0)
        def _():
            h1 = h0_ref[...] + _dot(ms_ref[...], wo_ref[:D_SSM, :]) + _dot(mp_ref[...], wo_ref[D_SSM:, :])
            h1_ref[...] = h1
            acc[...] = h1
            n2_ref[...] = (h1 * _rms(h1) * g2_ref[...]).astype(BF16)
            n2 = n2_ref[...]
            for q in range(N_SHARD):
                a = _dot_nt(n2, wg_ref[q])
                b = _dot_nt(n2, wu_ref[q])
                a_ref[q] = a.astype(BF16)
                b_ref[q] = b.astype(BF16)
                ff = (a * _sigmoid(a) * b).astype(BF16)
                ff_ref[q] = ff
                acc[...] += _dot(ff, wd_ref[q])

        @pl.when(phase == 1)
        def _():
            h2 = acc[...]
            r = _rms(h2)
            xhat = h2 * r
            gf_row = gf_ref[...]
            rows = i * tm + lax.broadcasted_iota(jnp.int32, (tm, 1), 0)
            valid = (rows >= N_META) & (rows < n_valid)
            diff = jnp.where(valid, xhat * gf_row - tgt_ref[...], 0.0)
            loss_ref[...] += jnp.full(loss_ref.shape, 0.5 / D_MODEL, F32) * jnp.sum(diff * diff)
            dout = diff * (1.0 / D_MODEL)
            dgf_ref[...] += jnp.sum(dout * xhat, axis=0, keepdims=True)
            dh2 = _rms_bwd(dout * gf_row, xhat, r)
            dh2_ref[...] = dh2
            dh2b_ref[...] = dh2.astype(BF16)

    row = lambda w: pl.BlockSpec((tm, w), lambda i, p: (i, 0))
    act = pl.BlockSpec((N_SHARD, tm, FF_SHARD), lambda i, p: (0, i, 0))
    sds = jax.ShapeDtypeStruct
    return pl.pallas_call(
        body, grid=(nt, 2), name="fwd_ffn",
        in_specs=[row(D_MODEL), row(D_SSM), row(D_POOL), _full((D_MODEL, D_MODEL)), _full((1, D_MODEL)),
                  _ANY, _ANY, _ANY, _full((1, D_MODEL)), row(D_MODEL)],
        out_specs=[row(D_MODEL), row(D_MODEL), act, act, act, row(D_MODEL), row(D_MODEL), _full((8, 128)),
                   _full((1, D_MODEL))],
        out_shape=[sds((n_pad, D_MODEL), F32), sds((n_pad, D_MODEL), BF16),
                   sds((N_SHARD, n_pad, FF_SHARD), BF16), sds((N_SHARD, n_pad, FF_SHARD), BF16),
                   sds((N_SHARD, n_pad, FF_SHARD), BF16), sds((n_pad, D_MODEL), F32), sds((n_pad, D_MODEL), BF16),
                   sds((8, 128), F32), sds((1, D_MODEL), F32)],
        scratch_shapes=[pltpu.VMEM(wg_b.shape, BF16), pltpu.VMEM(wu_b.shape, BF16), pltpu.VMEM(wd_b.shape, BF16),
                        pltpu.SemaphoreType.DMA((3,)), pltpu.VMEM((tm, D_MODEL), F32)],
        compiler_params=_params(("arbitrary", "arbitrary")),
    )(h0, ms, mp, w_out_b, g2, wg_b, wu_b, wd_b, gf, target)


def _bwd_ffn(dh2, dh2b, a, b, wg_b, wu_b, wd_b, h1, g2, tm):
    n_pad = dh2.shape[0]

    def body(dh2_ref, dh2b_ref, a_ref, b_ref, wg_hbm, wu_hbm, wd_hbm, h1_ref, g2_ref, da_ref, db_ref, dh1_ref, dg2_ref,
             wg_ref, wu_ref, wd_ref, sems, acc):
        @pl.when(pl.program_id(0) == 0)
        def _():
            dg2_ref[...] = jnp.zeros_like(dg2_ref)
            cps = [pltpu.make_async_copy(src, dst, sems.at[k])
                   for k, (src, dst) in enumerate(((wg_hbm, wg_ref), (wu_hbm, wu_ref), (wd_hbm, wd_ref)))]
            for cp in cps:
                cp.start()
            for cp in cps:
                cp.wait()

        dh2b = dh2b_ref[...]
        for q in range(N_SHARD):
            dff = _dot_nt(dh2b, wd_ref[q])
            a_v, b_v = a_ref[q].astype(F32), b_ref[q].astype(F32)
            sig = _sigmoid(a_v)
            silu = a_v * sig
            da = (dff * b_v * (sig + silu * (1.0 - sig))).astype(BF16)
            db = (dff * silu).astype(BF16)
            da_ref[q] = da
            db_ref[q] = db
            part = _dot(da, wg_ref[q]) + _dot(db, wu_ref[q])
            if q == 0:
                acc[...] = part
            else:
                acc[...] += part

        h1 = h1_ref[...]
        r = _rms(h1)
        xhat = h1 * r
        dn2 = acc[...]
        dg2_ref[...] += jnp.sum(dn2 * xhat, axis=0, keepdims=True)
        dh1_ref[...] = dh2_ref[...] + _rms_bwd(dn2 * g2_ref[...], xhat, r)

    row = lambda w: pl.BlockSpec((tm, w), lambda i: (i, 0))
    act = pl.BlockSpec((N_SHARD, tm, FF_SHARD), lambda i: (0, i, 0))
    sds = jax.ShapeDtypeStruct
    return pl.pallas_call(
        body, grid=(n_pad // tm,), name="bwd_ffn",
        in_specs=[row(D_MODEL), row(D_MODEL), act, act, _ANY, _ANY, _ANY, row(D_MODEL), _full((1, D_MODEL))],
        out_specs=[act, act, row(D_MODEL), _full((1, D_MODEL))],
        out_shape=[sds((N_SHARD, n_pad, FF_SHARD), BF16), sds((N_SHARD, n_pad, FF_SHARD), BF16),
                   sds((n_pad, D_MODEL), F32), sds((1, D_MODEL), F32)],
        scratch_shapes=[pltpu.VMEM(wg_b.shape, BF16), pltpu.VMEM(wu_b.shape, BF16), pltpu.VMEM(wd_b.shape, BF16),
                        pltpu.SemaphoreType.DMA((3,)), pltpu.VMEM((tm, D_MODEL), F32)],
        compiler_params=_params(("arbitrary",)),
    )(dh2, dh2b, a, b, wg_b, wu_b, wd_b, h1, g2)


def _grad_ffn(n2, da, db, ff, dh2b, tm):
    n_pad = n2.shape[0]

    def body(n2_ref, da_ref, db_ref, ff_ref, dh2_ref, dwg_ref, dwu_ref, dwd_ref):
        i = pl.program_id(1)
        n2_v = n2_ref[...]
        gg = _dot_tn(da_ref[0], n2_v)
        gu = _dot_tn(db_ref[0], n2_v)
        gd = _dot_tn(ff_ref[0], dh2_ref[...])

        @pl.when(i == 0)
        def _():
            dwg_ref[0] = gg
            dwu_ref[0] = gu
            dwd_ref[0] = gd

        @pl.when(i > 0)
        def _():
            dwg_ref[0] += gg
            dwu_ref[0] += gu
            dwd_ref[0] += gd

    row = lambda w: pl.BlockSpec((tm, w), lambda q, i: (i, 0))
    act = pl.BlockSpec((1, tm, FF_SHARD), lambda q, i: (q, i, 0))
    sds = jax.ShapeDtypeStruct
    return pl.pallas_call(
        body, grid=(N_SHARD, n_pad // tm), name="grad_ffn",
        in_specs=[row(D_MODEL), act, act, act, row(D_MODEL)],
        out_specs=[pl.BlockSpec((1, FF_SHARD, D_MODEL), lambda q, i: (q, 0, 0))] * 3,
        out_shape=[sds((N_SHARD, FF_SHARD, D_MODEL), F32)] * 3,
        compiler_params=_params(("parallel", "arbitrary")),
    )(n2, da, db, ff, dh2b)


def _bwd_out(dh1, ms, mp, w_out_b, tm, token):
    n_pad = dh1.shape[0]

    def body(dh1_ref, ms_ref, mp_ref, wo_ref, token_ref, dms_ref, dmp_ref, dwo_ref):
        i = pl.program_id(0)

        @pl.when(i == 0)
        def _():
            dwo_ref[...] = jnp.zeros_like(dwo_ref)

        d = dh1_ref[...].astype(BF16)
        dms = _dot_nt(d, wo_ref[:D_SSM, :])
        for k in range(4):
            dms_ref[k] = dms[:, 128 * k:128 * (k + 1)]
        dmp_ref[...] = _dot_nt(d, wo_ref[D_SSM:, :])
        dwo_ref[:D_SSM, :] += _dot_tn(ms_ref[...], d)
        dwo_ref[D_SSM:, :] += _dot_tn(mp_ref[...], d)

    row = lambda w: pl.BlockSpec((tm, w), lambda i: (i, 0))
    sds = jax.ShapeDtypeStruct
    return pl.pallas_call(
        body, grid=(n_pad // tm,), name="bwd_out",
        in_specs=[row(D_MODEL), row(D_SSM), row(D_POOL), _full((D_MODEL, D_MODEL)), _ANY],
        out_specs=[pl.BlockSpec((4, tm, 128), lambda i: (0, i, 0)), row(D_POOL), _full((D_MODEL, D_MODEL))],
        out_shape=[sds((4, n_pad, 128), F32), sds((n_pad, D_POOL), F32), sds((D_MODEL, D_MODEL), F32)],
        compiler_params=_params(("arbitrary",)),
    )(dh1, ms, mp, w_out_b, token)


def _bwd_in(du, dv, h0, dh1, g1, w_in_b, tm):
    n_pad = h0.shape[0]

    def body(du_ref, dv_ref, h0_ref, dh1_ref, g1_ref, w_ref, dh0_ref, dwi_ref, dg1_ref):
        i = pl.program_id(0)

        @pl.when(i == 0)
        def _():
            dwi_ref[...] = jnp.zeros_like(dwi_ref)
            dg1_ref[...] = jnp.zeros_like(dg1_ref)

        dub = du_ref[...].astype(BF16)
        dvb = dv_ref[...].astype(BF16)
        dn1 = _dot_nt(dub, w_ref[:, :D_SSM]) + _dot_nt(dvb, w_ref[:, D_SSM:])
        h = h0_ref[...]
        r = _rms(h)
        xhat = h * r
        g_row = g1_ref[...]
        n1 = (xhat * g_row).astype(BF16)
        dwi_ref[:, :D_SSM] += _dot_tn(n1, dub)
        dwi_ref[:, D_SSM:] += _dot_tn(n1, dvb)
        dg1_ref[...] += jnp.sum(dn1 * xhat, axis=0, keepdims=True)
        dh0_ref[...] = dh1_ref[...] + _rms_bwd(dn1 * g_row, xhat, r)

    row = lambda w: pl.BlockSpec((tm, w), lambda i: (i, 0))
    sds = jax.ShapeDtypeStruct
    return pl.pallas_call(
        body, grid=(n_pad // tm,), name="bwd_in",
        in_specs=[row(D_SSM), row(D_POOL), row(D_MODEL), row(D_MODEL), _full((1, D_MODEL)), _full((D_MODEL, D_MODEL))],
        out_specs=[row(D_MODEL), _full((D_MODEL, D_MODEL)), _full((1, D_MODEL))],
        out_shape=[sds((n_pad, D_MODEL), F32), sds((D_MODEL, D_MODEL), F32), sds((1, D_MODEL), F32)],
        compiler_params=_params(("arbitrary",)),
    )(du, dv, h0, dh1, g1, w_in_b)


SEGMENTS = 8


def _interleaved(ref, seg):
    return jnp.concatenate(
        [jnp.concatenate([ref[i, pl.ds(j, SEGMENTS, stride=seg), :] for i in range(4)], axis=1) for j in range(seg)],
        axis=0)


def _time_order(scratch, val, seg):
    for i in range(4):
        scratch[i] = val[:, 128 * i:128 * (i + 1)]
    tiles = []
    for m in range(val.shape[0] // 8):
        s, j0 = divmod(8 * m, seg)
        tiles.append(jnp.concatenate(
            [scratch[i, pl.ds(8 * j0 + s, 8, stride=SEGMENTS), :] for i in range(4)], axis=1))
    return jnp.concatenate(tiles, axis=0)


def _power_table(lam_ref, pw_r, pw_i, seg):
    a_r = jnp.broadcast_to(lam_ref[0:1, :], (SEGMENTS, N_STATE))
    a_i = jnp.broadcast_to(lam_ref[1:2, :], (SEGMENTS, N_STATE))
    p_r, p_i = a_r, a_i
    for k in range(seg):
        pw_r[SEGMENTS * k:SEGMENTS * (k + 1), :] = p_r
        pw_i[SEGMENTS * k:SEGMENTS * (k + 1), :] = p_i
        p_r, p_i = p_r * a_r - p_i * a_i, p_r * a_i + p_i * a_r


def _segment_scan(xr_ref, xi_ref, cols, pw_r, pw_i, hr_s, hi_s, seg, reverse):
    sign = -1.0 if reverse else 1.0
    a_r, a_i = pw_r[0:SEGMENTS, cols], sign * pw_i[0:SEGMENTS, cols]

    def step(n, carry):
        hr, hi = carry
        o = pl.multiple_of((seg - 1 - n if reverse else n) * SEGMENTS, SEGMENTS)
        nr = a_r * hr - a_i * hi + xr_ref[pl.ds(o, SEGMENTS), cols]
        ni = a_r * hi + a_i * hr + xi_ref[pl.ds(o, SEGMENTS), cols]
        xr_ref[pl.ds(o, SEGMENTS), cols] = nr
        xi_ref[pl.ds(o, SEGMENTS), cols] = ni
        return nr, ni

    zero = jnp.zeros((SEGMENTS, cols.stop - cols.start), F32)
    e_r, e_i = lax.fori_loop(0, seg, step, (zero, zero), unroll=2)

    top = SEGMENTS * (seg - 1)
    ls_r, ls_i = pw_r[top:top + 1, cols], sign * pw_i[top:top + 1, cols]
    c_r, c_i = hr_s[0:1, cols], hi_s[0:1, cols]
    in_r, in_i = [None] * SEGMENTS, [None] * SEGMENTS
    for s in (range(SEGMENTS - 1, -1, -1) if reverse else range(SEGMENTS)):
        in_r[s], in_i[s] = c_r, c_i
        c_r, c_i = (e_r[s:s + 1, :] + ls_r * c_r - ls_i * c_i, e_i[s:s + 1, :] + ls_r * c_i + ls_i * c_r)
    hr_s[0:1, cols] = c_r
    hi_s[0:1, cols] = c_i
    cm_r, cm_i = jnp.concatenate(in_r, axis=0), jnp.concatenate(in_i, axis=0)

    def fix(jj, _):
        o = pl.multiple_of(jj * SEGMENTS, SEGMENTS)
        k = pl.multiple_of((seg - 1 - jj if reverse else jj) * SEGMENTS, SEGMENTS)
        p_r, p_i = pw_r[pl.ds(k, SEGMENTS), cols], sign * pw_i[pl.ds(k, SEGMENTS), cols]
        xr_ref[pl.ds(o, SEGMENTS), cols] += p_r * cm_r - p_i * cm_i
        xi_ref[pl.ds(o, SEGMENTS), cols] += p_r * cm_i + p_i * cm_r
        return 0

    lax.fori_loop(0, seg, fix, 0, unroll=2)


def _s5_tail(y, glu_ref, glub):
    g, dgelu = _gelu_and_grad(y)
    gb = g.astype(BF16)
    gate = jnp.concatenate([_dot(gb[:, 128 * j:128 * (j + 1)], glu_ref[j]) for j in range(SUPER)], axis=1) + glub
    sig = _sigmoid(gate)
    return g, gb, dgelu, sig, g * sig


def _s5_fwd(u4, lam, bbr, bbi, crt, cit, vecs, glu, tc):
    n_pad = u4.shape[1]
    seg = tc // SEGMENTS

    def body(u_ref, lam_ref, bbr_ref, bbi_ref, crt_ref, cit_ref, vec_ref, glu_ref,
             sr_ref, si_ref, y_ref, ms_ref, hr_s, hi_s, pw_r, pw_i, lanes):
        @pl.when(pl.program_id(0) == 0)
        def _():
            hr_s[...] = jnp.zeros_like(hr_s)
            hi_s[...] = jnp.zeros_like(hi_s)
            _power_table(lam_ref, pw_r, pw_i, seg)

        u_v = _interleaved(u_ref, seg)
        ub = u_v.astype(BF16)
        for j in range(SUPER):
            uj = ub[:, 128 * j:128 * (j + 1)]
            sr_ref[:, 512 * j:512 * (j + 1)] = _dot(uj, bbr_ref[j])
            si_ref[:, 512 * j:512 * (j + 1)] = _dot(uj, bbi_ref[j])
        for j in range(SUPER):
            _segment_scan(sr_ref, si_ref, slice(512 * j, 512 * (j + 1)), pw_r, pw_i, hr_s, hi_s, seg, False)

        d_row, glub, gs = vec_ref[0:1, :], vec_ref[1:2, :], vec_ref[2:3, :]
        ys_c = []
        for j in range(SUPER):
            sr_j = sr_ref[:, 512 * j:512 * (j + 1)].astype(BF16)
            si_j = si_ref[:, 512 * j:512 * (j + 1)].astype(BF16)
            ys_c.append(_dot(sr_j, crt_ref[j]) - _dot(si_j, cit_ref[j]))
        y = jnp.concatenate(ys_c, axis=1) + d_row * u_v
        y_ref[...] = y
        _, _, _, _, ys = _s5_tail(y, glu_ref, glub)
        ms_ref[...] = _time_order(lanes, ys * _rms(ys) * gs, seg).astype(BF16)

    chunk = lambda w: pl.BlockSpec((tc, w), lambda c: (c, 0))
    lane_blocks = pl.BlockSpec((4, tc, 128), lambda c: (0, c, 0))
    sds = jax.ShapeDtypeStruct
    return pl.pallas_call(
        body, grid=(n_pad // tc,), name="s5_fwd",
        in_specs=[lane_blocks, _full((8, N_STATE)), _full((SUPER, 128, 512)), _full((SUPER, 128, 512)),
                  _full((SUPER, 512, 128)), _full((SUPER, 512, 128)), _full((8, D_SSM)), _full((SUPER, 128, 128))],
        out_specs=[chunk(N_STATE), chunk(N_STATE), chunk(D_SSM), chunk(D_SSM)],
        out_shape=[sds((n_pad, N_STATE), F32), sds((n_pad, N_STATE), F32),
                   sds((n_pad, D_SSM), F32), sds((n_pad, D_SSM), BF16)],
        scratch_shapes=[pltpu.VMEM((8, N_STATE), F32), pltpu.VMEM((8, N_STATE), F32),
                        pltpu.VMEM((tc, N_STATE), F32), pltpu.VMEM((tc, N_STATE), F32),
                        pltpu.VMEM((4, tc, 128), F32)],
        compiler_params=_params(("arbitrary",)),
    )(u4, lam, bbr, bbi, crt, cit, vecs, glu)


def _s5_bwd(dms4, y, u4, sr, si, lam, bbr, bbi, crt, cit, vecs, glu, tc, token):
    n_pad = u4.shape[1]
    nc = n_pad // tc
    seg = tc // SEGMENTS

    def body(dms_ref, y_ref, u_ref, sr_ref, si_ref, pr_ref, pi_ref, lam_ref, bbr_ref, bbi_ref, crt_ref, cit_ref,
             vec_ref, glu_ref, token_ref, du_ref, dbbr_ref, dbbi_ref, dcrt_ref, dcit_ref, dglu_ref, dvec_ref, dlam_ref,
             qr_s, qi_s, cr_s, ci_s, pw_r, pw_i, lanes):
        c = pl.program_id(0)

        @pl.when(c == 0)
        def _():
            for ref in (dbbr_ref, dbbi_ref, dcrt_ref, dcit_ref, dglu_ref, dvec_ref, dlam_ref, cr_s, ci_s):
                ref[...] = jnp.zeros_like(ref)
            _power_table(lam_ref, pw_r, pw_i, seg)

        d_row, glub, gs = vec_ref[0:1, :], vec_ref[1:2, :], vec_ref[2:3, :]
        y_v, u_v = y_ref[...], _interleaved(u_ref, seg)
        ub = u_v.astype(BF16)
        g, gb, dgelu, sig, ys = _s5_tail(y_v, glu_ref, glub)
        r = _rms(ys)
        xhat = ys * r
        dm = _interleaved(dms_ref, seg)
        dys = _rms_bwd(dm * gs, xhat, r)
        dgate = dys * g * sig * (1.0 - sig)
        dgateb = dgate.astype(BF16)
        dg = dys * sig + jnp.concatenate(
            [_dot_nt(dgateb[:, 128 * j:128 * (j + 1)], glu_ref[j]) for j in range(SUPER)], axis=1)
        dy = dg * dgelu
        dyb = dy.astype(BF16)
        dvec_ref[0:1, :] += jnp.sum(dy * u_v, axis=0, keepdims=True)
        dvec_ref[1:2, :] += jnp.sum(dgate, axis=0, keepdims=True)
        dvec_ref[2:3, :] += jnp.sum(dm * xhat, axis=0, keepdims=True)

        for j in range(SUPER):
            cols, states = slice(128 * j, 128 * (j + 1)), slice(512 * j, 512 * (j + 1))
            dglu_ref[j] += _dot_tn(gb[:, cols], dgateb[:, cols])
            dcrt_ref[j] += _dot_tn(sr_ref[:, states].astype(BF16), dyb[:, cols])
            dcit_ref[j] -= _dot_tn(si_ref[:, states].astype(BF16), dyb[:, cols])
            qr_s[:, states] = _dot_nt(dyb[:, cols], crt_ref[j])
            qi_s[:, states] = -_dot_nt(dyb[:, cols], cit_ref[j])

        first = c == nc - 1
        row0 = lax.broadcasted_iota(jnp.int32, (SEGMENTS, 1), 0) == 0
        last = (seg - 1) * SEGMENTS
        for j in range(SUPER):
            states = slice(512 * j, 512 * (j + 1))
            _segment_scan(qr_s, qi_s, states, pw_r, pw_i, cr_s, ci_s, seg, True)

            before_r = jnp.where(first, 0.0, pltpu.roll(pr_ref[:, states], 1, 0))
            before_i = jnp.where(first, 0.0, pltpu.roll(pi_ref[:, states], 1, 0))
            hp_r = jnp.where(row0, before_r, pltpu.roll(sr_ref[pl.ds(last, SEGMENTS), states], 1, 0))
            hp_i = jnp.where(row0, before_i, pltpu.roll(si_ref[pl.ds(last, SEGMENTS), states], 1, 0))
            q_r, q_i = qr_s[pl.ds(0, SEGMENTS), states], qi_s[pl.ds(0, SEGMENTS), states]

            def dlam_step(jj, acc):
                o = pl.multiple_of(jj * SEGMENTS, SEGMENTS)
                above = pl.multiple_of((jj - 1) * SEGMENTS, SEGMENTS)
                h_r, h_i = sr_ref[pl.ds(above, SEGMENTS), states], si_ref[pl.ds(above, SEGMENTS), states]
                t_r, t_i = qr_s[pl.ds(o, SEGMENTS), states], qi_s[pl.ds(o, SEGMENTS), states]
                return acc[0] + t_r * h_r + t_i * h_i, acc[1] + t_i * h_r - t_r * h_i

            acc = lax.fori_loop(1, seg, dlam_step, (q_r * hp_r + q_i * hp_i, q_i * hp_r - q_r * hp_i), unroll=2)
            dlam_ref[0:SEGMENTS, states] += acc[0]
            dlam_ref[SEGMENTS:, states] += acc[1]

        du_c = []
        for j in range(SUPER):
            cols, states = slice(128 * j, 128 * (j + 1)), slice(512 * j, 512 * (j + 1))
            qr_j = qr_s[:, states].astype(BF16)
            qi_j = qi_s[:, states].astype(BF16)
            du_c.append(_dot_nt(qr_j, bbr_ref[j]) + _dot_nt(qi_j, bbi_ref[j]))
            dbbr_ref[j] += _dot_tn(ub[:, cols], qr_j)
            dbbi_ref[j] += _dot_tn(ub[:, cols], qi_j)
        du_ref[...] = _time_order(lanes, jnp.concatenate(du_c, axis=1) + dy * d_row, seg)

    rev = lambda c: nc - 1 - c
    chunk = lambda w: pl.BlockSpec((tc, w), lambda c: (rev(c), 0))
    lane_blocks = pl.BlockSpec((4, tc, 128), lambda c: (0, rev(c), 0))
    prev = pl.BlockSpec((SEGMENTS, N_STATE), lambda c: (jnp.maximum(rev(c) * seg - 1, 0), 0))
    sds = jax.ShapeDtypeStruct
    return pl.pallas_call(
        body, grid=(nc,), name="s5_bwd",
        in_specs=[lane_blocks, chunk(D_SSM), lane_blocks, chunk(N_STATE), chunk(N_STATE), prev, prev,
                  _full((8, N_STATE)), _full((SUPER, 128, 512)), _full((SUPER, 128, 512)),
                  _full((SUPER, 512, 128)), _full((SUPER, 512, 128)), _full((8, D_SSM)), _full((SUPER, 128, 128)), _ANY],
        out_specs=[chunk(D_SSM), _full((SUPER, 128, 512)), _full((SUPER, 128, 512)), _full((SUPER, 512, 128)),
                   _full((SUPER, 512, 128)), _full((SUPER, 128, 128)), _full((8, D_SSM)), _full((2 * SEGMENTS, N_STATE))],
        out_shape=[sds((n_pad, D_SSM), F32), sds((SUPER, 128, 512), F32), sds((SUPER, 128, 512), F32),
                   sds((SUPER, 512, 128), F32), sds((SUPER, 512, 128), F32), sds((SUPER, 128, 128), F32),
                   sds((8, D_SSM), F32), sds((2 * SEGMENTS, N_STATE), F32)],
        scratch_shapes=[pltpu.VMEM((tc, N_STATE), F32), pltpu.VMEM((tc, N_STATE), F32),
                        pltpu.VMEM((8, N_STATE), F32), pltpu.VMEM((8, N_STATE), F32),
                        pltpu.VMEM((tc, N_STATE), F32), pltpu.VMEM((tc, N_STATE), F32),
                        pltpu.VMEM((4, tc, 128), F32)],
        compiler_params=_params(("arbitrary",)),
    )(dms4, y, u4, sr, si, sr, si, lam, bbr, bbi, crt, cit, vecs, glu, token)


def _inv_count(c_idx, tc, w):
    t = c_idx * tc + lax.broadcasted_iota(jnp.int32, (tc, 1), 0)
    return 1.0 / jnp.minimum(t + 1, w).astype(F32)


def _pool_fwd(v, pw_b, vecs, tc, token):
    n_pad = v.shape[0]

    def body(v_ref, pw_ref, vec_ref, token_ref, feat_ref, mp_ref, hist):
        c = pl.program_id(0)

        @pl.when(c == 0)
        def _():
            hist[...] = jnp.zeros_like(hist)

        v_v = v_ref[...]
        ext = jnp.concatenate([hist[...], v_v], axis=0)
        hist[...] = v_v[tc - POOL_HALO:, :]
        feats, ps = [], []
        for k, w in enumerate(POOL_WINDOWS):
            cols = slice(128 * k, 128 * (k + 1))
            s = ext[:, cols]
            sh = 1
            while sh < w:
                s = s + pltpu.roll(s, sh, 0)
                sh *= 2
            f = (s[POOL_HALO:, :] * _inv_count(c, tc, w) - v_v[:, cols]).astype(BF16)
            feats.append(f)
            ps.append(_dot(f, pw_ref[k]))
        feat_ref[...] = jnp.concatenate(feats, axis=1)
        yp = jnp.concatenate(ps, axis=1) * vec_ref[0:1, :]
        mp_ref[...] = (yp * _rms(yp) * vec_ref[1:2, :]).astype(BF16)

    chunk = lambda w: pl.BlockSpec((tc, w), lambda c: (c, 0))
    sds = jax.ShapeDtypeStruct
    return pl.pallas_call(
        body, grid=(n_pad // tc,), name="pool_fwd",
        in_specs=[chunk(D_POOL), _full((4, 128, 128)), _full((8, D_POOL)), _ANY],
        out_specs=[chunk(D_POOL), chunk(D_POOL)],
        out_shape=[sds((n_pad, D_POOL), BF16), sds((n_pad, D_POOL), BF16)],
        scratch_shapes=[pltpu.VMEM((POOL_HALO, D_POOL), F32)],
        compiler_params=_params(("arbitrary",)),
    )(v, pw_b, vecs, token)


def _pool_bwd(dmp, feat, pw_b, vecs, tc):
    n_pad = dmp.shape[0]
    nc = n_pad // tc

    def body(dmp_ref, feat_ref, pw_ref, vec_ref, dv_ref, dpw_ref, dvec_ref, fut):
        c = pl.program_id(0)

        @pl.when(c == 0)
        def _():
            fut[...] = jnp.zeros_like(fut)
            dpw_ref[...] = jnp.zeros_like(dpw_ref)
            dvec_ref[...] = jnp.zeros_like(dvec_ref)

        scale, gp = vec_ref[0:1, :], vec_ref[1:2, :]
        feat_v = feat_ref[...]
        p = jnp.concatenate([_dot(feat_v[:, 128 * k:128 * (k + 1)], pw_ref[k]) for k in range(4)], axis=1)
        yp = p * scale
        r = _rms(yp)
        xhat = yp * r
        dm = dmp_ref[...]
        dyp = _rms_bwd(dm * gp, xhat, r)
        dvec_ref[0:1, :] += jnp.sum(dyp * p, axis=0, keepdims=True)
        dvec_ref[1:2, :] += jnp.sum(dm * xhat, axis=0, keepdims=True)
        dpb = (dyp * scale).astype(BF16)
        es, dfs = [], []
        for k, w in enumerate(POOL_WINDOWS):
            cols = slice(128 * k, 128 * (k + 1))
            dpw_ref[k] += _dot_tn(feat_v[:, cols], dpb[:, cols])
            df = _dot_nt(dpb[:, cols], pw_ref[k])
            dfs.append(df)
            es.append(df * _inv_count(nc - 1 - c, tc, w))
        e = jnp.concatenate(es, axis=1)
        ext = jnp.concatenate([e, fut[...]], axis=0)
        fut[...] = e[:POOL_HALO, :]
        n_ext = tc + POOL_HALO
        dvs = []
        for k, w in enumerate(POOL_WINDOWS):
            s = ext[:, 128 * k:128 * (k + 1)]
            sh = 1
            while sh < w:
                s = s + pltpu.roll(s, n_ext - sh, 0)
                sh *= 2
            dvs.append(s[:tc, :] - dfs[k])
        dv_ref[...] = jnp.concatenate(dvs, axis=1)

    chunk = lambda w: pl.BlockSpec((tc, w), lambda c: (nc - 1 - c, 0))
    sds = jax.ShapeDtypeStruct
    return pl.pallas_call(
        body, grid=(nc,), name="pool_bwd",
        in_specs=[chunk(D_POOL), chunk(D_POOL), _full((4, 128, 128)), _full((8, D_POOL))],
        out_specs=[chunk(D_POOL), _full((4, 128, 128)), _full((8, D_POOL))],
        out_shape=[sds((n_pad, D_POOL), F32), sds((4, 128, 128), F32), sds((8, D_POOL), F32)],
        scratch_shapes=[pltpu.VMEM((POOL_HALO, D_POOL), F32)],
        compiler_params=_params(("arbitrary",)),
    )(dmp, feat, pw_b, vecs)


def _place():
    x, y, c = lax.axis_index("x"), lax.axis_index("y"), lax.axis_index("c")
    chips = [(1 - x, y), (x, 1 - y), (1 - x, 1 - y)]
    return x, y, c, chips


_ANY = pl.BlockSpec(memory_space=pl.ANY)


def _cast_shards(shards, dtypes, place):
    n = len(shards)

    def body(place_ref, *refs):
        for i in range(n):
            refs[n + i][0] = refs[i][...].astype(dtypes[i])

    return pl.pallas_call(
        body, name="cast_shards",
        grid_spec=pltpu.PrefetchScalarGridSpec(
            num_scalar_prefetch=1, grid=(1,),
            in_specs=[pl.BlockSpec(s.shape, lambda i, p: (0, 0, 0)) for s in shards],
            out_specs=[pl.BlockSpec((1,) + s.shape, lambda i, p: (p[0], 0, 0, 0)) for s in shards]),
        out_shape=[jax.ShapeDtypeStruct((N_SHARD,) + s.shape, dt) for s, dt in zip(shards, dtypes)],
        compiler_params=_params(("arbitrary",)),
    )(place, *shards)


def _gather_shards(full):
    n = len(full)

    def body(*refs):
        outs = refs[n:2 * n]
        ici_send, ici_recv, d2d_send, d2d_recv = refs[2 * n:]
        x, y, c, chips = _place()
        q = 2 * x + y
        sibling = (x, y, 1 - c)

        def ici(i, j, shard, to):
            return pltpu.make_async_remote_copy(src_ref=outs[i].at[q, c], dst_ref=outs[i].at[shard, c],
                                                send_sem=ici_send.at[i, j], recv_sem=ici_recv.at[i, j],
                                                device_id=to, device_id_type=MESH)

        def d2d(i, j, shard, half):
            return pltpu.make_async_remote_copy(src_ref=outs[i].at[shard, c], dst_ref=outs[i].at[shard, half],
                                                send_sem=d2d_send.at[i, j], recv_sem=d2d_recv.at[i, j],
                                                device_id=sibling, device_id_type=MESH)

        sends = [ici(i, j, q, (*chip, c)) for i in range(n) for j, chip in enumerate(chips)]
        for cp in sends:
            cp.start()
        passed = []
        for i in range(n):
            for j, (cx, cy) in enumerate(chips):
                ici(i, j, 2 * cx + cy, (cx, cy, c)).wait_recv()
                cp = d2d(i, j, 2 * cx + cy, c)
                cp.start()
                passed.append(cp)
        for i in range(n):
            for j, (cx, cy) in enumerate(chips):
                d2d(i, j, 2 * cx + cy, 1 - c).wait_recv()
        for cp in sends + passed:
            cp.wait_send()

    return pl.pallas_call(
        body, name="gather_shards",
        in_specs=[_ANY] * n, out_specs=[_ANY] * n,
        out_shape=[jax.ShapeDtypeStruct(f.shape, f.dtype) for f in full],
        input_output_aliases={i: i for i in range(n)},
        scratch_shapes=[pltpu.SemaphoreType.DMA((n, 3)), pltpu.SemaphoreType.DMA((n, 3)),
                        pltpu.SemaphoreType.DMA((n, 3)), pltpu.SemaphoreType.DMA((n, 3))],
    )(*full)


_HBM = pl.BlockSpec(memory_space=pltpu.HBM)
_SEM = pl.BlockSpec(memory_space=pltpu.SEMAPHORE)
_EFFECT = pltpu.SideEffectType.DATAFLOW_SIDE_EFFECTING


def _copies_start(name, arrays, sem_shape, build, after=None):
    n = len(arrays)
    extra = [] if after is None else [after]

    def body(*refs):
        outs = refs[n + len(extra):2 * n + len(extra)]
        send, recv, token = refs[2 * n + len(extra):]
        sends, _ = build(outs, send, recv)
        for cp in sends:
            cp.start()
        token[...] = jnp.zeros_like(token)

    out = pl.pallas_call(
        body, name=name, in_specs=[_HBM] * n + [_ANY] * len(extra),
        out_specs=[_HBM] * n + [_SEM, _SEM, pl.BlockSpec(memory_space=pltpu.VMEM)],
        out_shape=[pltpu.HBM(a.shape, a.dtype) for a in arrays]
        + [pltpu.SemaphoreType.DMA(sem_shape), pltpu.SemaphoreType.DMA(sem_shape), jax.ShapeDtypeStruct((8, 128), F32)],
        input_output_aliases={i: i for i in range(n)},
        compiler_params=pltpu.CompilerParams(has_side_effects=_EFFECT),
    )(*[pltpu.with_memory_space_constraint(a, pltpu.HBM) for a in arrays], *extra)
    return list(out[:n]), (out[n], out[n + 1]), out[n + 2]


def _copies_wait(name, arrays, sems, after, build):
    n = len(arrays)

    def body(*refs):
        ins = refs[:n]
        send, recv = refs[n], refs[n + 1]
        sends, recvs = build(ins, send, recv)
        for cp in sends:
            cp.wait_send()
        for cp in recvs:
            cp.wait_recv()

    return list(pl.pallas_call(
        body, name=name, in_specs=[_HBM] * n + [_SEM, _SEM] + [_ANY] * len(after), out_specs=[_HBM] * n,
        out_shape=[pltpu.HBM(a.shape, a.dtype) for a in arrays],
        input_output_aliases={i: i for i in range(n)},
        compiler_params=pltpu.CompilerParams(has_side_effects=_EFFECT),
    )(*arrays, *sems, *after))


def _remote(src, dst, send_sem, recv_sem, to):
    return pltpu.make_async_remote_copy(src_ref=src, dst_ref=dst, send_sem=send_sem, recv_sem=recv_sem,
                                        device_id=to, device_id_type=MESH)


def _build_gather(refs, send, recv):
    x, y, c, chips = _place()
    q = 2 * x + y
    pairs = [(i, j, chip) for i in range(len(refs)) for j, chip in enumerate(chips)]
    sends = [_remote(refs[i].at[q, c], refs[i].at[q, c], send.at[3 * i + j], recv.at[3 * i + j], (cx, cy, c))
             for i, j, (cx, cy) in pairs]
    recvs = [_remote(refs[i].at[q, c], refs[i].at[2 * cx + cy, c], send.at[3 * i + j], recv.at[3 * i + j], (cx, cy, c))
             for i, j, (cx, cy) in pairs]
    return sends, recvs


def _build_forward(refs, send, recv):
    x, y, c, chips = _place()
    pairs = [(i, j, 2 * cx + cy) for i in range(len(refs)) for j, (cx, cy) in enumerate(chips)]
    sends = [_remote(refs[i].at[s, c], refs[i].at[s, c], send.at[3 * i + j], recv.at[3 * i + j], (x, y, 1 - c))
             for i, j, s in pairs]
    recvs = [_remote(refs[i].at[s, c], refs[i].at[s, 1 - c], send.at[3 * i + j], recv.at[3 * i + j], (x, y, 1 - c))
             for i, j, s in pairs]
    return sends, recvs


def _build_swap(refs, send, recv):
    x, y, c, _ = _place()
    n = len(refs) // 2
    cps = [_remote(refs[i].at[:, 1 - c], refs[n + i], send.at[i], recv.at[i], (x, y, 1 - c)) for i in range(n)]
    return cps, cps


def _build_exchange(refs, send, recv):
    x, y, c, chips = _place()
    n = len(refs) // 2
    cps = [_remote(refs[i].at[2 * cx + cy], refs[n + i].at[j], send.at[3 * i + j], recv.at[3 * i + j], (cx, cy, c))
           for i in range(n) for j, (cx, cy) in enumerate(chips)]
    return cps, cps


def _build_spread(refs, send, recv):
    x, y, c, _ = _place()
    flip = lambda bit, on: bit + on - 2 * bit * on
    cps = [_remote(refs[0], refs[1].at[r - 1], send.at[r - 1], recv.at[r - 1],
                   (flip(x, r >> 2 & 1), flip(y, r >> 1 & 1), flip(c, r & 1))) for r in (1, 2, 4, 6)]
    return cps, cps


def _forward_small(landed):
    def body(in_ref, out_ref, send, recv):
        x, y, c, _ = _place()
        cps = [_remote(out_ref.at[r - 1], out_ref.at[r], send.at[k], recv.at[k], (x, y, 1 - c))
               for k, r in enumerate((2, 4, 6))]
        for cp in cps:
            cp.start()
        for cp in cps:
            cp.wait()

    return pl.pallas_call(
        body, name="forward_small",
        in_specs=[_ANY], out_specs=_ANY, out_shape=jax.ShapeDtypeStruct(landed.shape, F32),
        input_output_aliases={0: 0},
        scratch_shapes=[pltpu.SemaphoreType.DMA((3,)), pltpu.SemaphoreType.DMA((3,))],
    )(landed)


def _swap_halves(grads):
    n = len(grads)

    def body(*refs):
        ins, outs = refs[:n], refs[n:2 * n]
        send, recv = refs[2 * n:]
        x, y, c, _ = _place()
        cps = [pltpu.make_async_remote_copy(src_ref=ins[i].at[:, 1 - c], dst_ref=outs[i], send_sem=send.at[i],
                                            recv_sem=recv.at[i], device_id=(x, y, 1 - c), device_id_type=MESH)
               for i in range(n)]
        for cp in cps:
            cp.start()
        for cp in cps:
            cp.wait()

    return pl.pallas_call(
        body, name="swap_halves",
        in_specs=[_ANY] * n, out_specs=[_ANY] * n,
        out_shape=[jax.ShapeDtypeStruct((N_SHARD,) + g.shape[2:], F32) for g in grads],
        scratch_shapes=[pltpu.SemaphoreType.DMA((n,)), pltpu.SemaphoreType.DMA((n,))],
    )(*grads)


def _join_halves(pairs):
    n = len(pairs)

    def body(*refs):
        outs = refs[n:2 * n]
        send, recv = refs[2 * n:]
        x, y, c, _ = _place()
        cps = [pltpu.make_async_remote_copy(src_ref=outs[i].at[c], dst_ref=outs[i].at[c], send_sem=send.at[i],
                                            recv_sem=recv.at[i], device_id=(x, y, 1 - c), device_id_type=MESH)
               for i in range(n)]
        for cp in cps:
            cp.start()
        for i in range(n):
            cps[i].wait_send()
            pltpu.make_async_remote_copy(src_ref=outs[i].at[c], dst_ref=outs[i].at[1 - c], send_sem=send.at[i],
                                         recv_sem=recv.at[i], device_id=(x, y, 1 - c), device_id_type=MESH).wait_recv()

    return pl.pallas_call(
        body, name="join_halves",
        in_specs=[_ANY] * n, out_specs=[_ANY] * n,
        out_shape=[jax.ShapeDtypeStruct(p.shape, F32) for p in pairs],
        input_output_aliases={i: i for i in range(n)},
        scratch_shapes=[pltpu.SemaphoreType.DMA((n,)), pltpu.SemaphoreType.DMA((n,))],
    )(*pairs)


N_SPLIT = 2


def _sum_siblings(tag, grads, recvd, place):
    n = len(grads)

    def body(place_ref, *refs):
        g_refs, r_refs, sb_refs, own_refs = (refs[k * n:(k + 1) * n] for k in range(4))
        s = pl.program_id(1)
        for i in range(n):
            tot = g_refs[i][0, 0] + r_refs[i][0]
            sb_refs[i][0] = tot.astype(BF16)

            @pl.when(s == place_ref[0])
            def _():
                own_refs[i][...] = tot

    in_specs, sb_specs, own_specs, sb_shapes, own_shapes = [], [], [], [], []
    for g in grads:
        _, _, r, cdim = g.shape
        rb = r // N_SPLIT
        in_specs.append(pl.BlockSpec((1, 1, rb, cdim), lambda b, s, p: (s, p[1], b, 0)))
        sb_specs.append(pl.BlockSpec((1, rb, cdim), lambda b, s, p: (s, b, 0)))
        own_specs.append(pl.BlockSpec((rb, cdim), lambda b, s, p: (b, 0)))
        sb_shapes.append(jax.ShapeDtypeStruct((N_SHARD, r, cdim), BF16))
        own_shapes.append(jax.ShapeDtypeStruct((r, cdim), F32))
    out = pl.pallas_call(
        body, name="sum_siblings_" + tag,
        grid_spec=pltpu.PrefetchScalarGridSpec(
            num_scalar_prefetch=1, grid=(N_SPLIT, N_SHARD),
            in_specs=in_specs + sb_specs, out_specs=sb_specs + own_specs),
        out_shape=sb_shapes + own_shapes,
        compiler_params=_params(("parallel", "arbitrary")),
    )(place, *grads, *recvd)
    return out[:n], out[n:]


def _sum_chips(own, recvd, place):
    n = len(own)

    def body(place_ref, *refs):
        o_refs, r_refs, out_refs = (refs[k * n:(k + 1) * n] for k in range(3))
        for i in range(n):
            tot = o_refs[i][...]
            for j in range(3):
                tot = tot + r_refs[i][j].astype(F32)
            out_refs[i][0] = tot

    o_specs, r_specs, out_specs = [], [], []
    for o in own:
        r, cdim = o.shape
        rb = r // N_SPLIT
        o_specs.append(pl.BlockSpec((rb, cdim), lambda b, p: (b, 0)))
        r_specs.append(pl.BlockSpec((3, rb, cdim), lambda b, p: (0, b, 0)))
        out_specs.append(pl.BlockSpec((1, rb, cdim), lambda b, p: (p[1], b, 0)))
    return pl.pallas_call(
        body, name="sum_chips",
        grid_spec=pltpu.PrefetchScalarGridSpec(num_scalar_prefetch=1, grid=(N_SPLIT,),
                                               in_specs=o_specs + r_specs, out_specs=out_specs),
        out_shape=[jax.ShapeDtypeStruct((2,) + o.shape, F32) for o in own],
        compiler_params=_params(("parallel",)),
    )(place, *own, *recvd)


def _adamw_math(w, g, m, v):
    m = ADAM_B1 * m + (1.0 - ADAM_B1) * g
    v = ADAM_B2 * v + (1.0 - ADAM_B2) * (g * g)
    m_hat = m / (1.0 - ADAM_B1 ** ADAM_STEP)
    v_hat = v / (1.0 - ADAM_B2 ** ADAM_STEP)
    delta = -ADAM_LR * (m_hat / (jnp.sqrt(v_hat) + ADAM_EPS) + ADAM_WD * w)
    return delta, m, v


def _adamw(name, ws, gs, ms, vs, n_split):
    n = len(ws)

    def body(*refs):
        w_r, g_r, m_r, v_r, d_o, m_o, v_o = (refs[k * n:(k + 1) * n] for k in range(7))
        for i in range(n):
            d, m, v = _adamw_math(w_r[i][...], g_r[i][...], m_r[i][...], v_r[i][...])
            d_o[i][...] = d
            m_o[i][...] = m
            v_o[i][...] = v

    specs = [pl.BlockSpec((w.shape[0] // n_split, w.shape[1]), lambda b: (b, 0)) for w in ws]
    shapes = [jax.ShapeDtypeStruct(w.shape, F32) for w in ws]
    out = pl.pallas_call(
        body, name=name, grid=(n_split,),
        in_specs=specs * 4, out_specs=specs * 3, out_shape=shapes * 3,
        compiler_params=_params(("parallel",)),
    )(*ws, *gs, *ms, *vs)
    return out[:n], out[n:2 * n], out[2 * n:]


def _reduce_small(own, received):
    def body(own_ref, recv_ref, g_out):
        me = 4 * lax.axis_index("x") + 2 * lax.axis_index("y") + lax.axis_index("c")
        g = None
        for k in range(8):
            mine = me == k
            part = jnp.where(mine, own_ref[...], recv_ref[jnp.where(mine, 0, jnp.bitwise_xor(me, k) - 1)])
            g = part if g is None else g + part
        g_out[...] = g

    return pl.pallas_call(
        body, name="reduce_small",
        out_shape=jax.ShapeDtypeStruct(own.shape, F32),
        compiler_params=_params(),
    )(own, received)


def _adamw_small(ws, gs, ms, vs):
    n = len(ws)

    def body(*refs):
        w_r, g_r, m_r, v_r, d_o, m_o, v_o = (refs[k * n:(k + 1) * n] for k in range(7))
        for i in range(n):
            d, mm, vv = _adamw_math(w_r[i][...], g_r[i][...], m_r[i][...], v_r[i][...])
            d_o[i][...] = d
            m_o[i][...] = mm
            v_o[i][...] = vv

    out = pl.pallas_call(
        body, name="adamw_small",
        out_shape=[jax.ShapeDtypeStruct(t.shape, F32) for t in ws] * 3,
        compiler_params=_params(),
    )(*ws, *gs, *ms, *vs)
    return out[:n], out[n:2 * n], out[2 * n:]


def _s5_operands(lam_re, lam_im, log_step, b_re, b_im, c_re, c_im, glu_w):
    lr = jnp.minimum(lam_re, -1e-4)
    li = lam_im
    step = jnp.exp(log_step)[:, None]
    mag = jnp.exp(lr * step)
    ang = li * step
    abr = mag * jnp.cos(ang)
    abi = mag * jnp.sin(ang)
    nr = abr - 1.0
    ni = abi
    den = lr * lr + li * li
    cr = ((nr * lr + ni * li) / den)[..., None]
    ci = ((ni * lr - nr * li) / den)[..., None]
    bbr = cr * b_re - ci * b_im
    bbi = cr * b_im + ci * b_re
    eye = jnp.eye(8, dtype=F32)
    g, h, p = SSM_GROUPS // SUPER, SSM_GROUP, SSM_STATE

    def b_layout(t):
        return jnp.einsum("ab,japh->jahbp", eye, t.reshape(SUPER, g, p, h)).reshape(SUPER, g * h, g * p)

    def c_layout(t):
        return jnp.einsum("ab,jahp->jbpah", eye, t.reshape(SUPER, g, h, p)).reshape(SUPER, g * p, g * h)

    glu = jnp.einsum("ab,jahk->jahbk", eye, glu_w.reshape(SUPER, g, h, h)).reshape(SUPER, g * h, g * h)
    lam = _pad_rows(jnp.concatenate([abr.reshape(1, N_STATE), abi.reshape(1, N_STATE)], axis=0), 8)
    return lam, b_layout(bbr), b_layout(bbi), c_layout(c_re), c_layout(c_im), glu


def _pad_rows(a, rows):
    return jnp.pad(a, ((0, rows - a.shape[0]), (0, 0)))


def _pack(parts):
    rows = []
    for a in parts:
        flat = a.reshape(-1)
        n = -(-flat.shape[0] // 128)
        rows.append(jnp.pad(flat, (0, n * 128 - flat.shape[0])).reshape(n, 128))
    out = jnp.concatenate(rows, axis=0)
    return _pad_rows(out, -(-out.shape[0] // 8) * 8)


def _unpack(packed, like):
    out, at = [], 0
    for a in like:
        n = -(-a.size // 128)
        out.append(packed[at:at + n].reshape(-1)[:a.size].reshape(a.shape))
        at += n
    return out


STORED = {"ssm_b_re": (0, 1, 3, 2), "ssm_b_im": (0, 1, 3, 2), "ssm_d": (0, 2, 1), "ssm_glu_b": (0, 2, 1),
          "ssm_glu_w": (0, 2, 3, 1)}


def _stored(k, a):
    return a.transpose(STORED[k]) if k in STORED else a


def _logical(k, a):
    return a.transpose(tuple(STORED[k].index(i) for i in range(a.ndim))) if k in STORED else a


SMALL = ("norm1_g", "ssm_lambda_re", "ssm_lambda_im", "ssm_log_step", "ssm_b_re", "ssm_b_im", "ssm_c_re", "ssm_c_im",
         "ssm_d", "ssm_glu_w", "ssm_glu_b", "ssm_norm_g", "pool_w", "pool_scale", "pool_norm_g", "norm2_g",
         "final_norm_g")
LARGE = ("w_in", "w_out", "w_gate", "w_up", "w_down")
WEIGHTS = ("meta_tokens", "norm1_g", "w_in", "ssm_lambda_re", "ssm_lambda_im", "ssm_log_step", "ssm_b_re", "ssm_b_im",
           "ssm_c_re", "ssm_c_im", "ssm_d", "ssm_glu_w", "ssm_glu_b", "ssm_norm_g", "pool_w", "pool_scale",
           "pool_norm_g", "w_out", "norm2_g", "w_gate", "w_up", "w_down", "final_norm_g")


def _step(x, target, w, m, v):
    seq = x.shape[1]
    n_rows = N_META + seq
    n_pad, tm, tp, tc, tg = _plan(n_rows)
    xq, yq, cq = lax.axis_index("x"), lax.axis_index("y"), lax.axis_index("c")
    place = jnp.stack([2 * xq + yq, cq]).astype(jnp.int32)

    def halves(a2d):
        return a2d.reshape(2, a2d.shape[0] // 2, a2d.shape[1])

    def local2d(t):
        return {"w_gate": lambda a: a[0].T, "w_up": lambda a: a[0].T}.get(t, lambda a: a[0])

    shards = [halves(local2d(k)(w[k])) for k in LARGE] + [halves(w["meta_tokens"])]
    full = _cast_shards(shards, [BF16] * len(LARGE) + [F32], place)
    w_in_full, meta_full = _gather_shards([full[0], full[5]])
    late, gather_sems, gather_token = _copies_start("gather_start", list(full[1:5]), (12,), _build_gather,
                                                    after=w_in_full)
    w_in_b = w_in_full.reshape(D_MODEL, D_MODEL)
    meta = meta_full.reshape(N_SHARD, N_META, D_MODEL // N_SHARD).transpose(1, 0, 2).reshape(N_META, D_MODEL)

    h0 = _pad_rows(jnp.concatenate([meta, x[0]], axis=0), n_pad)
    tgt = _pad_rows(jnp.concatenate([jnp.zeros((N_META, D_MODEL), F32), target[0]], axis=0), n_pad)
    s5_in = (w["ssm_lambda_re"][0], w["ssm_lambda_im"][0], w["ssm_log_step"][0], w["ssm_b_re"][0], w["ssm_b_im"][0],
             w["ssm_c_re"][0], w["ssm_c_im"][0], w["ssm_glu_w"][0])
    (lam, bbr, bbi, crt, cit, glu), s5_vjp = jax.vjp(_s5_operands, *s5_in)
    bbr_b, bbi_b, crt_b, cit_b, glu_b16 = (t.astype(BF16) for t in (bbr, bbi, crt, cit, glu))
    s5_vecs = _pad_rows(jnp.concatenate([w["ssm_d"].reshape(1, D_SSM), w["ssm_glu_b"].reshape(1, D_SSM),
                                         w["ssm_norm_g"].reshape(1, D_SSM)], axis=0), 8)
    pool_vecs = _pad_rows(jnp.concatenate([w["pool_scale"].reshape(1, D_POOL), w["pool_norm_g"].reshape(1, D_POOL)],
                                          axis=0), 8)
    pw_b = w["pool_w"][0].astype(BF16)
    g1, g2, gf = w["norm1_g"].reshape(1, D_MODEL), w["norm2_g"].reshape(1, D_MODEL), w["final_norm_g"].reshape(1, D_MODEL)

    u, vv = _fwd_in(h0, g1, w_in_b, tp, gather_token)
    sr, si, y, ms = _s5_fwd(u, lam, bbr_b, bbi_b, crt_b, cit_b, s5_vecs, glu_b16, tc)
    late = _copies_wait("gather_wait", late, gather_sems, [ms], _build_gather)
    late, forward_sems, forward_token = _copies_start("forward_start", late, (12,), _build_forward)
    feat, mp = _pool_fwd(vv, pw_b, pool_vecs, tc, forward_token)
    late = _copies_wait("forward_wait", late, forward_sems, [mp], _build_forward)
    w_out_b = late[0].reshape(D_MODEL, D_MODEL)
    wg_b, wu_b, wd_b = (t.reshape(N_SHARD, FF_SHARD, D_MODEL) for t in late[1:])
    h1, n2, a, b, ff, dh2, dh2b, loss_acc, dgf = _fwd_ffn(h0, ms, mp, w_out_b, g2, wg_b, wu_b, wd_b, gf, tgt, tc, n_rows)

    def quarters(t):
        if t.ndim == 2:
            t = t.reshape(N_SHARD, t.shape[0] // N_SHARD, t.shape[1])
        return t.reshape(N_SHARD, 2, t.shape[1] // 2, t.shape[2])

    def landing(like, lead, dtype):
        return [lax.empty((lead,) + t.shape[2:], dtype) for t in like]

    da, db, dh1, dg2 = _bwd_ffn(dh2, dh2b, a, b, wg_b, wu_b, wd_b, h1, g2, tc)
    ffn_g = [quarters(t) for t in _grad_ffn(n2, da, db, ff, dh2b, tg)]
    nf = len(ffn_g)
    moved, swap_sems, swap_token = _copies_start("swap_start", ffn_g + landing(ffn_g, N_SHARD, F32), (nf,), _build_swap)
    dms, dmp, dwo = _bwd_out(dh1, ms, mp, w_out_b, tp, swap_token)
    moved = _copies_wait("swap_wait", moved, swap_sems, [dwo], _build_swap)
    ffn_parts, ffn_own = _sum_siblings("ffn", moved[:nf], moved[nf:], place)
    moved, exch_sems, exch_token = _copies_start("exchange_start", list(ffn_parts) + landing(ffn_g, 3, BF16), (3 * nf,),
                                                 _build_exchange)
    du, dbbr, dbbi, dcrt, dcit, dglu, ds5v, dlam = _s5_bwd(dms, y, u, sr, si, lam, bbr_b, bbi_b, crt_b, cit_b,
                                                           s5_vecs, glu_b16, tc, exch_token)
    dv, dpw, dpoolv = _pool_bwd(dmp, feat, pw_b, pool_vecs, tc)
    dh0, dwi, dg1 = _bwd_in(du, dv, h0, dh1, g1, w_in_b, tp)
    ffn_from_chips = _copies_wait("exchange_wait", moved, exch_sems, [dh0], _build_exchange)[nf:]
    dlam = _pad_rows(jnp.concatenate([jnp.sum(dlam[:SEGMENTS], axis=0, keepdims=True),
                                      jnp.sum(dlam[SEGMENTS:], axis=0, keepdims=True)], axis=0), 8)
    d_lre, d_lim, d_lstep, d_bre, d_bim, d_cre, d_cim, d_gluw = s5_vjp((dlam, dbbr, dbbi, dcrt, dcit, dglu))
    grad_x = dh0[N_META:n_rows][None]

    small_g = {
        "norm1_g": dg1, "ssm_lambda_re": d_lre, "ssm_lambda_im": d_lim, "ssm_log_step": d_lstep, "ssm_b_re": d_bre,
        "ssm_b_im": d_bim, "ssm_c_re": d_cre, "ssm_c_im": d_cim, "ssm_d": ds5v[0], "ssm_glu_w": d_gluw,
        "ssm_glu_b": ds5v[1], "ssm_norm_g": ds5v[2], "pool_w": dpw, "pool_scale": dpoolv[0], "pool_norm_g": dpoolv[1],
        "norm2_g": dg2, "final_norm_g": dgf,
    }
    like = [_stored(k, w[k]) for k in SMALL]
    packed_g = _pack([_stored(k, small_g[k].reshape(w[k].shape)) for k in SMALL] + [dh0[:N_META], loss_acc[0:1, 0:1]])

    mix_g = [quarters(t) for t in (dwi, dwo)]
    mix_parts, mix_own = _sum_siblings("mix", mix_g, _swap_halves(mix_g), place)
    moved, mix_sems, mix_token = _copies_start("mix_exchange_start", list(mix_parts) + landing(mix_g, 3, BF16),
                                               (3 * len(mix_g),), _build_exchange)
    spread, small_sems, small_token = _copies_start(
        "small_start", [packed_g, lax.empty((7,) + packed_g.shape, F32)], (7,), _build_spread, after=mix_token)
    mix_from_chips = _copies_wait("mix_exchange_wait", moved, mix_sems, [small_token], _build_exchange)[len(mix_g):]
    joined = _join_halves(_sum_chips(list(mix_own) + list(ffn_own), list(mix_from_chips) + list(ffn_from_chips), place))
    g_large = [j.reshape(j.shape[0] * j.shape[1], j.shape[2]) for j in joined]
    w2d, m2d, v2d = ([local2d(k)(t[k]) for k in LARGE] for t in (w, m, v))
    d_large, m_large, v_large = _adamw("adamw_large", w2d, g_large, m2d, v2d, 8)

    own_g, landed = _copies_wait("small_wait", spread, small_sems, [d_large[0]], _build_spread)
    g_pk = _reduce_small(own_g, _forward_small(landed))
    g_small = _unpack(g_pk, like + [jax.ShapeDtypeStruct((N_META, D_MODEL), F32), jax.ShapeDtypeStruct((1, 1), F32)])
    loss = g_small.pop()[0, 0]
    rows2d = lambda t: t.reshape(1, -1) if t.ndim == 1 else t
    d_small, m_small, v_small = _adamw_small(*([rows2d(t) for t in ts] for ts in (
        like, g_small[:-1], [_stored(k, m[k]) for k in SMALL], [_stored(k, v[k]) for k in SMALL])))
    g_small, d_small, m_small, v_small = ([_logical(k, t.reshape(a.shape)) for t, a, k in zip(ts, like, SMALL)] + ts[len(SMALL):]
                                          for ts in (g_small, list(d_small), list(m_small), list(v_small)))
    q = place[0]
    g_meta = lax.dynamic_slice_in_dim(g_small[-1], q * (D_MODEL // N_SHARD), D_MODEL // N_SHARD, axis=1)
    d_meta, m_meta, v_meta = _adamw("adamw_meta", [w["meta_tokens"]], [g_meta], [m["meta_tokens"]],
                                    [v["meta_tokens"]], 1)

    grads, deltas, new_m, new_v = {}, {}, {}, {}
    for i, k in enumerate(SMALL):
        grads[k], deltas[k], new_m[k], new_v[k] = g_small[i], d_small[i], m_small[i], v_small[i]
    for i, k in enumerate(LARGE):
        back = (lambda t: t.T[None]) if k in ("w_gate", "w_up") else (lambda t: t[None])
        grads[k], deltas[k], new_m[k], new_v[k] = (back(t) for t in (g_large[i], d_large[i], m_large[i], v_large[i]))
    grads["meta_tokens"], deltas["meta_tokens"] = g_meta, d_meta[0]
    new_m["meta_tokens"], new_v["meta_tokens"] = m_meta[0], v_meta[0]
    return (loss, grad_x, *[grads[k] for k in WEIGHTS], *[deltas[k] for k in WEIGHTS],
            *[new_m[k] for k in WEIGHTS], *[new_v[k] for k in WEIGHTS])


def kernel(x, meta_tokens, norm1_g, w_in, ssm_lambda_re, ssm_lambda_im, ssm_log_step, ssm_b_re, ssm_b_im, ssm_c_re, ssm_c_im, ssm_d, ssm_glu_w, ssm_glu_b, ssm_norm_g, pool_w, pool_scale, pool_norm_g, w_out, norm2_g, w_gate, w_up, w_down, final_norm_g, loss_target, m_meta_tokens, m_norm1_g, m_w_in, m_ssm_lambda_re, m_ssm_lambda_im, m_ssm_log_step, m_ssm_b_re, m_ssm_b_im, m_ssm_c_re, m_ssm_c_im, m_ssm_d, m_ssm_glu_w, m_ssm_glu_b, m_ssm_norm_g, m_pool_w, m_pool_scale, m_pool_norm_g, m_w_out, m_norm2_g, m_w_gate, m_w_up, m_w_down, m_final_norm_g, v_meta_tokens, v_norm1_g, v_w_in, v_ssm_lambda_re, v_ssm_lambda_im, v_ssm_log_step, v_ssm_b_re, v_ssm_b_im, v_ssm_c_re, v_ssm_c_im, v_ssm_d, v_ssm_glu_w, v_ssm_glu_b, v_ssm_norm_g, v_pool_w, v_pool_scale, v_pool_norm_g, v_w_out, v_norm2_g, v_w_gate, v_w_up, v_w_down, v_final_norm_g):
    w = dict(meta_tokens=meta_tokens, norm1_g=norm1_g, w_in=w_in, ssm_lambda_re=ssm_lambda_re, ssm_lambda_im=ssm_lambda_im, ssm_log_step=ssm_log_step, ssm_b_re=ssm_b_re, ssm_b_im=ssm_b_im, ssm_c_re=ssm_c_re, ssm_c_im=ssm_c_im, ssm_d=ssm_d, ssm_glu_w=ssm_glu_w, ssm_glu_b=ssm_glu_b, ssm_norm_g=ssm_norm_g, pool_w=pool_w, pool_scale=pool_scale, pool_norm_g=pool_norm_g, w_out=w_out, norm2_g=norm2_g, w_gate=w_gate, w_up=w_up, w_down=w_down, final_norm_g=final_norm_g)
    m = dict(meta_tokens=m_meta_tokens, norm1_g=m_norm1_g, w_in=m_w_in, ssm_lambda_re=m_ssm_lambda_re, ssm_lambda_im=m_ssm_lambda_im, ssm_log_step=m_ssm_log_step, ssm_b_re=m_ssm_b_re, ssm_b_im=m_ssm_b_im, ssm_c_re=m_ssm_c_re, ssm_c_im=m_ssm_c_im, ssm_d=m_ssm_d, ssm_glu_w=m_ssm_glu_w, ssm_glu_b=m_ssm_glu_b, ssm_norm_g=m_ssm_norm_g, pool_w=m_pool_w, pool_scale=m_pool_scale, pool_norm_g=m_pool_norm_g, w_out=m_w_out, norm2_g=m_norm2_g, w_gate=m_w_gate, w_up=m_w_up, w_down=m_w_down, final_norm_g=m_final_norm_g)
    v = dict(meta_tokens=v_meta_tokens, norm1_g=v_norm1_g, w_in=v_w_in, ssm_lambda_re=v_ssm_lambda_re, ssm_lambda_im=v_ssm_lambda_im, ssm_log_step=v_ssm_log_step, ssm_b_re=v_ssm_b_re, ssm_b_im=v_ssm_b_im, ssm_c_re=v_ssm_c_re, ssm_c_im=v_ssm_c_im, ssm_d=v_ssm_d, ssm_glu_w=v_ssm_glu_w, ssm_glu_b=v_ssm_glu_b, ssm_norm_g=v_ssm_norm_g, pool_w=v_pool_w, pool_scale=v_pool_scale, pool_norm_g=v_pool_norm_g, w_out=v_w_out, norm2_g=v_norm2_g, w_gate=v_w_gate, w_up=v_w_up, w_down=v_w_down, final_norm_g=v_final_norm_g)
    return _step(x, loss_target, w, m, v)
```

```python
import functools
import math

import jax
import jax.numpy as jnp
from jax import lax
from jax.experimental import pallas as pl
from jax.experimental.pallas import tpu as pltpu

F32 = jnp.float32
BF16 = jnp.bfloat16
MESH = pl.DeviceIdType.MESH
AXES = ("x", "y", "c")

D_MODEL = 1024
D_SSM = 512
D_POOL = 512
N_META = 16
SSM_GROUP = 16
SSM_GROUPS = 32
SSM_STATE = 64
N_STATE = SSM_GROUPS * SSM_STATE
STATE_BLOCKS = N_STATE // 128
SUPER = 4
POOL_WINDOWS = (2, 4, 8, 16)
POOL_HALO = 16
D_FF = 2816
N_SHARD = 4
FF_SHARD = D_FF // N_SHARD
EPS = 1e-6
ADAM_LR, ADAM_B1, ADAM_B2, ADAM_EPS, ADAM_WD, ADAM_STEP = 0.001, 0.9, 0.999, 1e-08, 0.01, 10
VMEM_LIMIT = 56 * 1024 * 1024


def _plan(n_rows):
    if n_rows > 2048:
        tm, tp, tc, tg = 416, 832, 320, 1040
    else:
        tm, tp, tc, tg = 128, 128, 64, 128
    step = math.lcm(tm, tp, tc, tg)
    return -(-n_rows // step) * step, tm, tp, tc, tg


def _params(sem=None):
    return pltpu.CompilerParams(dimension_semantics=sem, vmem_limit_bytes=VMEM_LIMIT)


def _dot(a, b):
    return jnp.dot(a, b, preferred_element_type=F32)


def _dot_nt(a, b):
    return lax.dot_general(a, b, (((1,), (1,)), ((), ())), preferred_element_type=F32)


def _dot_tn(a, b):
    return lax.dot_general(a, b, (((0,), (0,)), ((), ())), preferred_element_type=F32)


def _sigmoid(x):
    return 0.5 * jnp.tanh(0.5 * x) + 0.5


_GELU_C = math.sqrt(2.0 / math.pi)


def _gelu_and_grad(y):
    y2 = y * y
    t = jnp.tanh(_GELU_C * (y + 0.044715 * y * y2))
    g = 0.5 * y * (1.0 + t)
    dg = 0.5 * (1.0 + t) + 0.5 * y * (1.0 - t * t) * (_GELU_C * (1.0 + 3.0 * 0.044715 * y2))
    return g, dg


def _rms(x):
    return lax.rsqrt(jnp.mean(x * x, axis=-1, keepdims=True) + EPS)


def _rms_bwd(dn, xhat, r):
    return r * (dn - xhat * jnp.mean(dn * xhat, axis=-1, keepdims=True))


def _full(shape):
    nd = len(shape)
    return pl.BlockSpec(shape, lambda *_: (0,) * nd)


def _fwd_in(h0, g1, w_in_b, tm, token):
    n_pad = h0.shape[0]

    def body(h_ref, g_ref, w_ref, token_ref, u_ref, v_ref):
        h = h_ref[...]
        n1 = (h * _rms(h) * g_ref[...]).astype(BF16)
        proj = _dot(n1, w_ref[...])
        for i in range(4):
            u_ref[i] = proj[:, 128 * i:128 * (i + 1)]
        v_ref[...] = proj[:, D_SSM:]

    row = lambda w: pl.BlockSpec((tm, w), lambda i: (i, 0))
    return pl.pallas_call(
        body, grid=(n_pad // tm,), name="fwd_in",
        in_specs=[row(D_MODEL), _full((1, D_MODEL)), _full((D_MODEL, D_MODEL)), _ANY],
        out_specs=[pl.BlockSpec((4, tm, 128), lambda i: (0, i, 0)), row(D_POOL)],
        out_shape=[jax.ShapeDtypeStruct((4, n_pad, 128), F32), jax.ShapeDtypeStruct((n_pad, D_POOL), F32)],
        compiler_params=_params(("parallel",)),
    )(h0, g1, w_in_b, token)


def _fwd_ffn(h0, ms, mp, w_out_b, g2, wg_b, wu_b, wd_b, gf, target, tm, n_valid):
    n_pad = h0.shape[0]
    nt = n_pad // tm

    def body(h0_ref, ms_ref, mp_ref, wo_ref, g2_ref, wg_hbm, wu_hbm, wd_hbm, gf_ref, tgt_ref,
             h1_ref, n2_ref, a_ref, b_ref, ff_ref, dh2_ref, dh2b_ref, loss_ref, dgf_ref,
             wg_ref, wu_ref, wd_ref, sems, acc):
        i, phase = pl.program_id(0), pl.program_id(1)

        @pl.when((i == 0) & (phase == 0))
        def _():
            loss_ref[...] = jnp.zeros_like(loss_ref)
            dgf_ref[...] = jnp.zeros_like(dgf_ref)
            cps = [pltpu.make_async_copy(src, dst, sems.at[k])
                   for k, (src, dst) in enumerate(((wg_hbm, wg_ref), (wu_hbm, wu_ref), (wd_hbm, wd_ref)))]
            for cp in cps:
                cp.start()
            for cp in cps:
                cp.wait()

        @pl.when(phase == 0)
        def _():
            h1 = h0_ref[...] + _dot(ms_ref[...], wo_ref[:D_SSM, :]) + _dot(mp_ref[...], wo_ref[D_SSM:, :])
            h1_ref[...] = h1
            acc[...] = h1
            n2_ref[...] = (h1 * _rms(h1) * g2_ref[...]).astype(BF16)
            n2 = n2_ref[...]
            for q in range(N_SHARD):
                a = _dot_nt(n2, wg_ref[q])
                b = _dot_nt(n2, wu_ref[q])
                a_ref[q] = a.astype(BF16)
                b_ref[q] = b.astype(BF16)
                ff = (a * _sigmoid(a) * b).astype(BF16)
                ff_ref[q] = ff
                acc[...] += _dot(ff, wd_ref[q])

        @pl.when(phase == 1)
        def _():
            h2 = acc[...]
            r = _rms(h2)
            xhat = h2 * r
            gf_row = gf_ref[...]
            rows = i * tm + lax.broadcasted_iota(jnp.int32, (tm, 1), 0)
            valid = (rows >= N_META) & (rows < n_valid)
            diff = jnp.where(valid, xhat * gf_row - tgt_ref[...], 0.0)
            loss_ref[...] += jnp.full(loss_ref.shape, 0.5 / D_MODEL, F32) * jnp.sum(diff * diff)
            dout = diff * (1.0 / D_MODEL)
            dgf_ref[...] += jnp.sum(dout * xhat, axis=0, keepdims=True)
            dh2 = _rms_bwd(dout * gf_row, xhat, r)
            dh2_ref[...] = dh2
            dh2b_ref[...] = dh2.astype(BF16)

    row = lambda w: pl.BlockSpec((tm, w), lambda i, p: (i, 0))
    act = pl.BlockSpec((N_SHARD, tm, FF_SHARD), lambda i, p: (0, i, 0))
    sds = jax.ShapeDtypeStruct
    return pl.pallas_call(
        body, grid=(nt, 2), name="fwd_ffn",
        in_specs=[row(D_MODEL), row(D_SSM), row(D_POOL), _full((D_MODEL, D_MODEL)), _full((1, D_MODEL)),
                  _ANY, _ANY, _ANY, _full((1, D_MODEL)), row(D_MODEL)],
        out_specs=[row(D_MODEL), row(D_MODEL), act, act, act, row(D_MODEL), row(D_MODEL), _full((8, 128)),
                   _full((1, D_MODEL))],
        out_shape=[sds((n_pad, D_MODEL), F32), sds((n_pad, D_MODEL), BF16),
                   sds((N_SHARD, n_pad, FF_SHARD), BF16), sds((N_SHARD, n_pad, FF_SHARD), BF16),
                   sds((N_SHARD, n_pad, FF_SHARD), BF16), sds((n_pad, D_MODEL), F32), sds((n_pad, D_MODEL), BF16),
                   sds((8, 128), F32), sds((1, D_MODEL), F32)],
        scratch_shapes=[pltpu.VMEM(wg_b.shape, BF16), pltpu.VMEM(wu_b.shape, BF16), pltpu.VMEM(wd_b.shape, BF16),
                        pltpu.SemaphoreType.DMA((3,)), pltpu.VMEM((tm, D_MODEL), F32)],
        compiler_params=_params(("arbitrary", "arbitrary")),
    )(h0, ms, mp, w_out_b, g2, wg_b, wu_b, wd_b, gf, target)


def _bwd_ffn(dh2, dh2b, a, b, wg_b, wu_b, wd_b, h1, g2, tm):
    n_pad = dh2.shape[0]

    def body(dh2_ref, dh2b_ref, a_ref, b_ref, wg_hbm, wu_hbm, wd_hbm, h1_ref, g2_ref, da_ref, db_ref, dh1_ref, dg2_ref,
             wg_ref, wu_ref, wd_ref, sems, acc):
        @pl.when(pl.program_id(0) == 0)
        def _():
            dg2_ref[...] = jnp.zeros_like(dg2_ref)
            cps = [pltpu.make_async_copy(src, dst, sems.at[k])
                   for k, (src, dst) in enumerate(((wg_hbm, wg_ref), (wu_hbm, wu_ref), (wd_hbm, wd_ref)))]
            for cp in cps:
                cp.start()
            for cp in cps:
                cp.wait()

        dh2b = dh2b_ref[...]
        for q in range(N_SHARD):
            dff = _dot_nt(dh2b, wd_ref[q])
            a_v, b_v = a_ref[q].astype(F32), b_ref[q].astype(F32)
            sig = _sigmoid(a_v)
            silu = a_v * sig
            da = (dff * b_v * (sig + silu * (1.0 - sig))).astype(BF16)
            db = (dff * silu).astype(BF16)
            da_ref[q] = da
            db_ref[q] = db
            part = _dot(da, wg_ref[q]) + _dot(db, wu_ref[q])
            if q == 0:
                acc[...] = part
            else:
                acc[...] += part

        h1 = h1_ref[...]
        r = _rms(h1)
        xhat = h1 * r
        dn2 = acc[...]
        dg2_ref[...] += jnp.sum(dn2 * xhat, axis=0, keepdims=True)
        dh1_ref[...] = dh2_ref[...] + _rms_bwd(dn2 * g2_ref[...], xhat, r)

    row = lambda w: pl.BlockSpec((tm, w), lambda i: (i, 0))
    act = pl.BlockSpec((N_SHARD, tm, FF_SHARD), lambda i: (0, i, 0))
    sds = jax.ShapeDtypeStruct
    return pl.pallas_call(
        body, grid=(n_pad // tm,), name="bwd_ffn",
        in_specs=[row(D_MODEL), row(D_MODEL), act, act, _ANY, _ANY, _ANY, row(D_MODEL), _full((1, D_MODEL))],
        out_specs=[act, act, row(D_MODEL), _full((1, D_MODEL))],
        out_shape=[sds((N_SHARD, n_pad, FF_SHARD), BF16), sds((N_SHARD, n_pad, FF_SHARD), BF16),
                   sds((n_pad, D_MODEL), F32), sds((1, D_MODEL), F32)],
        scratch_shapes=[pltpu.VMEM(wg_b.shape, BF16), pltpu.VMEM(wu_b.shape, BF16), pltpu.VMEM(wd_b.shape, BF16),
                        pltpu.SemaphoreType.DMA((3,)), pltpu.VMEM((tm, D_MODEL), F32)],
        compiler_params=_params(("arbitrary",)),
    )(dh2, dh2b, a, b, wg_b, wu_b, wd_b, h1, g2)


def _grad_ffn(n2, da, db, ff, dh2b, tm):
    n_pad = n2.shape[0]

    def body(n2_ref, da_ref, db_ref, ff_ref, dh2_ref, dwg_ref, dwu_ref, dwd_ref):
        i = pl.program_id(1)
        n2_v = n2_ref[...]
        gg = _dot_tn(da_ref[0], n2_v)
        gu = _dot_tn(db_ref[0], n2_v)
        gd = _dot_tn(ff_ref[0], dh2_ref[...])

        @pl.when(i == 0)
        def _():
            dwg_ref[0] = gg
            dwu_ref[0] = gu
            dwd_ref[0] = gd

        @pl.when(i > 0)
        def _():
            dwg_ref[0] += gg
            dwu_ref[0] += gu
            dwd_ref[0] += gd

    row = lambda w: pl.BlockSpec((tm, w), lambda q, i: (i, 0))
    act = pl.BlockSpec((1, tm, FF_SHARD), lambda q, i: (q, i, 0))
    sds = jax.ShapeDtypeStruct
    return pl.pallas_call(
        body, grid=(N_SHARD, n_pad // tm), name="grad_ffn",
        in_specs=[row(D_MODEL), act, act, act, row(D_MODEL)],
        out_specs=[pl.BlockSpec((1, FF_SHARD, D_MODEL), lambda q, i: (q, 0, 0))] * 3,
        out_shape=[sds((N_SHARD, FF_SHARD, D_MODEL), F32)] * 3,
        compiler_params=_params(("parallel", "arbitrary")),
    )(n2, da, db, ff, dh2b)


def _bwd_out(dh1, ms, mp, w_out_b, tm, token):
    n_pad = dh1.shape[0]

    def body(dh1_ref, ms_ref, mp_ref, wo_ref, token_ref, dms_ref, dmp_ref, dwo_ref):
        i = pl.program_id(0)

        @pl.when(i == 0)
        def _():
            dwo_ref[...] = jnp.zeros_like(dwo_ref)

        d = dh1_ref[...].astype(BF16)
        dms = _dot_nt(d, wo_ref[:D_SSM, :])
        for k in range(4):
            dms_ref[k] = dms[:, 128 * k:128 * (k + 1)]
        dmp_ref[...] = _dot_nt(d, wo_ref[D_SSM:, :])
        dwo_ref[:D_SSM, :] += _dot_tn(ms_ref[...], d)
        dwo_ref[D_SSM:, :] += _dot_tn(mp_ref[...], d)

    row = lambda w: pl.BlockSpec((tm, w), lambda i: (i, 0))
    sds = jax.ShapeDtypeStruct
    return pl.pallas_call(
        body, grid=(n_pad // tm,), name="bwd_out",
        in_specs=[row(D_MODEL), row(D_SSM), row(D_POOL), _full((D_MODEL, D_MODEL)), _ANY],
        out_specs=[pl.BlockSpec((4, tm, 128), lambda i: (0, i, 0)), row(D_POOL), _full((D_MODEL, D_MODEL))],
        out_shape=[sds((4, n_pad, 128), F32), sds((n_pad, D_POOL), F32), sds((D_MODEL, D_MODEL), F32)],
        compiler_params=_params(("arbitrary",)),
    )(dh1, ms, mp, w_out_b, token)


def _bwd_in(du, dv, h0, dh1, g1, w_in_b, tm):
    n_pad = h0.shape[0]

    def body(du_ref, dv_ref, h0_ref, dh1_ref, g1_ref, w_ref, dh0_ref, dwi_ref, dg1_ref):
        i = pl.program_id(0)

        @pl.when(i == 0)
        def _():
            dwi_ref[...] = jnp.zeros_like(dwi_ref)
            dg1_ref[...] = jnp.zeros_like(dg1_ref)

        dub = du_ref[...].astype(BF16)
        dvb = dv_ref[...].astype(BF16)
        dn1 = _dot_nt(dub, w_ref[:, :D_SSM]) + _dot_nt(dvb, w_ref[:, D_SSM:])
        h = h0_ref[...]
        r = _rms(h)
        xhat = h * r
        g_row = g1_ref[...]
        n1 = (xhat * g_row).astype(BF16)
        dwi_ref[:, :D_SSM] += _dot_tn(n1, dub)
        dwi_ref[:, D_SSM:] += _dot_tn(n1, dvb)
        dg1_ref[...] += jnp.sum(dn1 * xhat, axis=0, keepdims=True)
        dh0_ref[...] = dh1_ref[...] + _rms_bwd(dn1 * g_row, xhat, r)

    row = lambda w: pl.BlockSpec((tm, w), lambda i: (i, 0))
    sds = jax.ShapeDtypeStruct
    return pl.pallas_call(
        body, grid=(n_pad // tm,), name="bwd_in",
        in_specs=[row(D_SSM), row(D_POOL), row(D_MODEL), row(D_MODEL), _full((1, D_MODEL)), _full((D_MODEL, D_MODEL))],
        out_specs=[row(D_MODEL), _full((D_MODEL, D_MODEL)), _full((1, D_MODEL))],
        out_shape=[sds((n_pad, D_MODEL), F32), sds((D_MODEL, D_MODEL), F32), sds((1, D_MODEL), F32)],
        compiler_params=_params(("arbitrary",)),
    )(du, dv, h0, dh1, g1, w_in_b)


SEGMENTS = 8


def _interleaved(ref, seg):
    return jnp.concatenate(
        [jnp.concatenate([ref[i, pl.ds(j, SEGMENTS, stride=seg), :] for i in range(4)], axis=1) for j in range(seg)],
        axis=0)


def _time_order(scratch, val, seg):
    for i in range(4):
        scratch[i] = val[:, 128 * i:128 * (i + 1)]
    tiles = []
    for m in range(val.shape[0] // 8):
        s, j0 = divmod(8 * m, seg)
        tiles.append(jnp.concatenate(
            [scratch[i, pl.ds(8 * j0 + s, 8, stride=SEGMENTS), :] for i in range(4)], axis=1))
    return jnp.concatenate(tiles, axis=0)


def _power_table(lam_ref, pw_r, pw_i, seg):
    a_r = jnp.broadcast_to(lam_ref[0:1, :], (SEGMENTS, N_STATE))
    a_i = jnp.broadcast_to(lam_ref[1:2, :], (SEGMENTS, N_STATE))
    p_r, p_i = a_r, a_i
    for k in range(seg):
        pw_r[SEGMENTS * k:SEGMENTS * (k + 1), :] = p_r
        pw_i[SEGMENTS * k:SEGMENTS * (k + 1), :] = p_i
        p_r, p_i = p_r * a_r - p_i * a_i, p_r * a_i + p_i * a_r


def _segment_scan(xr_ref, xi_ref, cols, pw_r, pw_i, hr_s, hi_s, seg, reverse):
    sign = -1.0 if reverse else 1.0
    a_r, a_i = pw_r[0:SEGMENTS, cols], sign * pw_i[0:SEGMENTS, cols]

    def step(n, carry):
        hr, hi = carry
        o = pl.multiple_of((seg - 1 - n if reverse else n) * SEGMENTS, SEGMENTS)
        nr = a_r * hr - a_i * hi + xr_ref[pl.ds(o, SEGMENTS), cols]
        ni = a_r * hi + a_i * hr + xi_ref[pl.ds(o, SEGMENTS), cols]
        xr_ref[pl.ds(o, SEGMENTS), cols] = nr
        xi_ref[pl.ds(o, SEGMENTS), cols] = ni
        return nr, ni

    zero = jnp.zeros((SEGMENTS, cols.stop - cols.start), F32)
    e_r, e_i = lax.fori_loop(0, seg, step, (zero, zero), unroll=2)

    top = SEGMENTS * (seg - 1)
    ls_r, ls_i = pw_r[top:top + 1, cols], sign * pw_i[top:top + 1, cols]
    c_r, c_i = hr_s[0:1, cols], hi_s[0:1, cols]
    in_r, in_i = [None] * SEGMENTS, [None] * SEGMENTS
    for s in (range(SEGMENTS - 1, -1, -1) if reverse else range(SEGMENTS)):
        in_r[s], in_i[s] = c_r, c_i
        c_r, c_i = (e_r[s:s + 1, :] + ls_r * c_r - ls_i * c_i, e_i[s:s + 1, :] + ls_r * c_i + ls_i * c_r)
    hr_s[0:1, cols] = c_r
    hi_s[0:1, cols] = c_i
    cm_r, cm_i = jnp.concatenate(in_r, axis=0), jnp.concatenate(in_i, axis=0)

    def fix(jj, _):
        o = pl.multiple_of(jj * SEGMENTS, SEGMENTS)
        k = pl.multiple_of((seg - 1 - jj if reverse else jj) * SEGMENTS, SEGMENTS)
        p_r, p_i = pw_r[pl.ds(k, SEGMENTS), cols], sign * pw_i[pl.ds(k, SEGMENTS), cols]
        xr_ref[pl.ds(o, SEGMENTS), cols] += p_r * cm_r - p_i * cm_i
        xi_ref[pl.ds(o, SEGMENTS), cols] += p_r * cm_i + p_i * cm_r
        return 0

    lax.fori_loop(0, seg, fix, 0, unroll=2)


def _s5_tail(y, glu_ref, glub):
    g, dgelu = _gelu_and_grad(y)
    gb = g.astype(BF16)
    gate = jnp.concatenate([_dot(gb[:, 128 * j:128 * (j + 1)], glu_ref[j]) for j in range(SUPER)], axis=1) + glub
    sig = _sigmoid(gate)
    return g, gb, dgelu, sig, g * sig


def _s5_fwd(u4, lam, bbr, bbi, crt, cit, vecs, glu, tc):
    n_pad = u4.shape[1]
    seg = tc // SEGMENTS

    def body(u_ref, lam_ref, bbr_ref, bbi_ref, crt_ref, cit_ref, vec_ref, glu_ref,
             sr_ref, si_ref, y_ref, ms_ref, hr_s, hi_s, pw_r, pw_i, lanes):
        @pl.when(pl.program_id(0) == 0)
        def _():
            hr_s[...] = jnp.zeros_like(hr_s)
            hi_s[...] = jnp.zeros_like(hi_s)
            _power_table(lam_ref, pw_r, pw_i, seg)

        u_v = _interleaved(u_ref, seg)
        ub = u_v.astype(BF16)
        for j in range(SUPER):
            uj = ub[:, 128 * j:128 * (j + 1)]
            sr_ref[:, 512 * j:512 * (j + 1)] = _dot(uj, bbr_ref[j])
            si_ref[:, 512 * j:512 * (j + 1)] = _dot(uj, bbi_ref[j])
        for j in range(SUPER):
            _segment_scan(sr_ref, si_ref, slice(512 * j, 512 * (j + 1)), pw_r, pw_i, hr_s, hi_s, seg, False)

        d_row, glub, gs = vec_ref[0:1, :], vec_ref[1:2, :], vec_ref[2:3, :]
        ys_c = []
        for j in range(SUPER):
            sr_j = sr_ref[:, 512 * j:512 * (j + 1)].astype(BF16)
            si_j = si_ref[:, 512 * j:512 * (j + 1)].astype(BF16)
            ys_c.append(_dot(sr_j, crt_ref[j]) - _dot(si_j, cit_ref[j]))
        y = jnp.concatenate(ys_c, axis=1) + d_row * u_v
        y_ref[...] = y
        _, _, _, _, ys = _s5_tail(y, glu_ref, glub)
        ms_ref[...] = _time_order(lanes, ys * _rms(ys) * gs, seg).astype(BF16)

    chunk = lambda w: pl.BlockSpec((tc, w), lambda c: (c, 0))
    lane_blocks = pl.BlockSpec((4, tc, 128), lambda c: (0, c, 0))
    sds = jax.ShapeDtypeStruct
    return pl.pallas_call(
        body, grid=(n_pad // tc,), name="s5_fwd",
        in_specs=[lane_blocks, _full((8, N_STATE)), _full((SUPER, 128, 512)), _full((SUPER, 128, 512)),
                  _full((SUPER, 512, 128)), _full((SUPER, 512, 128)), _full((8, D_SSM)), _full((SUPER, 128, 128))],
        out_specs=[chunk(N_STATE), chunk(N_STATE), chunk(D_SSM), chunk(D_SSM)],
        out_shape=[sds((n_pad, N_STATE), F32), sds((n_pad, N_STATE), F32),
                   sds((n_pad, D_SSM), F32), sds((n_pad, D_SSM), BF16)],
        scratch_shapes=[pltpu.VMEM((8, N_STATE), F32), pltpu.VMEM((8, N_STATE), F32),
                        pltpu.VMEM((tc, N_STATE), F32), pltpu.VMEM((tc, N_STATE), F32),
                        pltpu.VMEM((4, tc, 128), F32)],
        compiler_params=_params(("arbitrary",)),
    )(u4, lam, bbr, bbi, crt, cit, vecs, glu)


def _s5_bwd(dms4, y, u4, sr, si, lam, bbr, bbi, crt, cit, vecs, glu, tc, token):
    n_pad = u4.shape[1]
    nc = n_pad // tc
    seg = tc // SEGMENTS

    def body(dms_ref, y_ref, u_ref, sr_ref, si_ref, pr_ref, pi_ref, lam_ref, bbr_ref, bbi_ref, crt_ref, cit_ref,
             vec_ref, glu_ref, token_ref, du_ref, dbbr_ref, dbbi_ref, dcrt_ref, dcit_ref, dglu_ref, dvec_ref, dlam_ref,
             qr_s, qi_s, cr_s, ci_s, pw_r, pw_i, lanes):
        c = pl.program_id(0)

        @pl.when(c == 0)
        def _():
            for ref in (dbbr_ref, dbbi_ref, dcrt_ref, dcit_ref, dglu_ref, dvec_ref, dlam_ref, cr_s, ci_s):
                ref[...] = jnp.zeros_like(ref)
            _power_table(lam_ref, pw_r, pw_i, seg)

        d_row, glub, gs = vec_ref[0:1, :], vec_ref[1:2, :], vec_ref[2:3, :]
        y_v, u_v = y_ref[...], _interleaved(u_ref, seg)
        ub = u_v.astype(BF16)
        g, gb, dgelu, sig, ys = _s5_tail(y_v, glu_ref, glub)
        r = _rms(ys)
        xhat = ys * r
        dm = _interleaved(dms_ref, seg)
        dys = _rms_bwd(dm * gs, xhat, r)
        dgate = dys * g * sig * (1.0 - sig)
        dgateb = dgate.astype(BF16)
        dg = dys * sig + jnp.concatenate(
            [_dot_nt(dgateb[:, 128 * j:128 * (j + 1)], glu_ref[j]) for j in range(SUPER)], axis=1)
        dy = dg * dgelu
        dyb = dy.astype(BF16)
        dvec_ref[0:1, :] += jnp.sum(dy * u_v, axis=0, keepdims=True)
        dvec_ref[1:2, :] += jnp.sum(dgate, axis=0, keepdims=True)
        dvec_ref[2:3, :] += jnp.sum(dm * xhat, axis=0, keepdims=True)

        for j in range(SUPER):
            cols, states = slice(128 * j, 128 * (j + 1)), slice(512 * j, 512 * (j + 1))
            dglu_ref[j] += _dot_tn(gb[:, cols], dgateb[:, cols])
            dcrt_ref[j] += _dot_tn(sr_ref[:, states].astype(BF16), dyb[:, cols])
            dcit_ref[j] -= _dot_tn(si_ref[:, states].astype(BF16), dyb[:, cols])
            qr_s[:, states] = _dot_nt(dyb[:, cols], crt_ref[j])
            qi_s[:, states] = -_dot_nt(dyb[:, cols], cit_ref[j])

        first = c == nc - 1
        row0 = lax.broadcasted_iota(jnp.int32, (SEGMENTS, 1), 0) == 0
        last = (seg - 1) * SEGMENTS
        for j in range(SUPER):
            states = slice(512 * j, 512 * (j + 1))
            _segment_scan(qr_s, qi_s, states, pw_r, pw_i, cr_s, ci_s, seg, True)

            before_r = jnp.where(first, 0.0, pltpu.roll(pr_ref[:, states], 1, 0))
            before_i = jnp.where(first, 0.0, pltpu.roll(pi_ref[:, states], 1, 0))
            hp_r = jnp.where(row0, before_r, pltpu.roll(sr_ref[pl.ds(last, SEGMENTS), states], 1, 0))
            hp_i = jnp.where(row0, before_i, pltpu.roll(si_ref[pl.ds(last, SEGMENTS), states], 1, 0))
            q_r, q_i = qr_s[pl.ds(0, SEGMENTS), states], qi_s[pl.ds(0, SEGMENTS), states]

            def dlam_step(jj, acc):
                o = pl.multiple_of(jj * SEGMENTS, SEGMENTS)
                above = pl.multiple_of((jj - 1) * SEGMENTS, SEGMENTS)
                h_r, h_i = sr_ref[pl.ds(above, SEGMENTS), states], si_ref[pl.ds(above, SEGMENTS), states]
                t_r, t_i = qr_s[pl.ds(o, SEGMENTS), states], qi_s[pl.ds(o, SEGMENTS), states]
                return acc[0] + t_r * h_r + t_i * h_i, acc[1] + t_i * h_r - t_r * h_i

            acc = lax.fori_loop(1, seg, dlam_step, (q_r * hp_r + q_i * hp_i, q_i * hp_r - q_r * hp_i), unroll=2)
            dlam_ref[0:SEGMENTS, states] += acc[0]
            dlam_ref[SEGMENTS:, states] += acc[1]

        du_c = []
        for j in range(SUPER):
            cols, states = slice(128 * j, 128 * (j + 1)), slice(512 * j, 512 * (j + 1))
            qr_j = qr_s[:, states].astype(BF16)
            qi_j = qi_s[:, states].astype(BF16)
            du_c.append(_dot_nt(qr_j, bbr_ref[j]) + _dot_nt(qi_j, bbi_ref[j]))
            dbbr_ref[j] += _dot_tn(ub[:, cols], qr_j)
            dbbi_ref[j] += _dot_tn(ub[:, cols], qi_j)
        du_ref[...] = _time_order(lanes, jnp.concatenate(du_c, axis=1) + dy * d_row, seg)

    rev = lambda c: nc - 1 - c
    chunk = lambda w: pl.BlockSpec((tc, w), lambda c: (rev(c), 0))
    lane_blocks = pl.BlockSpec((4, tc, 128), lambda c: (0, rev(c), 0))
    prev = pl.BlockSpec((SEGMENTS, N_STATE), lambda c: (jnp.maximum(rev(c) * seg - 1, 0), 0))
    sds = jax.ShapeDtypeStruct
    return pl.pallas_call(
        body, grid=(nc,), name="s5_bwd",
        in_specs=[lane_blocks, chunk(D_SSM), lane_blocks, chunk(N_STATE), chunk(N_STATE), prev, prev,
                  _full((8, N_STATE)), _full((SUPER, 128, 512)), _full((SUPER, 128, 512)),
                  _full((SUPER, 512, 128)), _full((SUPER, 512, 128)), _full((8, D_SSM)), _full((SUPER, 128, 128)), _ANY],
        out_specs=[chunk(D_SSM), _full((SUPER, 128, 512)), _full((SUPER, 128, 512)), _full((SUPER, 512, 128)),
                   _full((SUPER, 512, 128)), _full((SUPER, 128, 128)), _full((8, D_SSM)), _full((2 * SEGMENTS, N_STATE))],
        out_shape=[sds((n_pad, D_SSM), F32), sds((SUPER, 128, 512), F32), sds((SUPER, 128, 512), F32),
                   sds((SUPER, 512, 128), F32), sds((SUPER, 512, 128), F32), sds((SUPER, 128, 128), F32),
                   sds((8, D_SSM), F32), sds((2 * SEGMENTS, N_STATE), F32)],
        scratch_shapes=[pltpu.VMEM((tc, N_STATE), F32), pltpu.VMEM((tc, N_STATE), F32),
                        pltpu.VMEM((8, N_STATE), F32), pltpu.VMEM((8, N_STATE), F32),
                        pltpu.VMEM((tc, N_STATE), F32), pltpu.VMEM((tc, N_STATE), F32),
                        pltpu.VMEM((4, tc, 128), F32)],
        compiler_params=_params(("arbitrary",)),
    )(dms4, y, u4, sr, si, sr, si, lam, bbr, bbi, crt, cit, vecs, glu, token)


def _inv_count(c_idx, tc, w):
    t = c_idx * tc + lax.broadcasted_iota(jnp.int32, (tc, 1), 0)
    return 1.0 / jnp.minimum(t + 1, w).astype(F32)


def _pool_fwd(v, pw_b, vecs, tc, token):
    n_pad = v.shape[0]

    def body(v_ref, pw_ref, vec_ref, token_ref, feat_ref, mp_ref, hist):
        c = pl.program_id(0)

        @pl.when(c == 0)
        def _():
            hist[...] = jnp.zeros_like(hist)

        v_v = v_ref[...]
        ext = jnp.concatenate([hist[...], v_v], axis=0)
        hist[...] = v_v[tc - POOL_HALO:, :]
        feats, ps = [], []
        for k, w in enumerate(POOL_WINDOWS):
            cols = slice(128 * k, 128 * (k + 1))
            s = ext[:, cols]
            sh = 1
            while sh < w:
                s = s + pltpu.roll(s, sh, 0)
                sh *= 2
            f = (s[POOL_HALO:, :] * _inv_count(c, tc, w) - v_v[:, cols]).astype(BF16)
            feats.append(f)
            ps.append(_dot(f, pw_ref[k]))
        feat_ref[...] = jnp.concatenate(feats, axis=1)
        yp = jnp.concatenate(ps, axis=1) * vec_ref[0:1, :]
        mp_ref[...] = (yp * _rms(yp) * vec_ref[1:2, :]).astype(BF16)

    chunk = lambda w: pl.BlockSpec((tc, w), lambda c: (c, 0))
    sds = jax.ShapeDtypeStruct
    return pl.pallas_call(
        body, grid=(n_pad // tc,), name="pool_fwd",
        in_specs=[chunk(D_POOL), _full((4, 128, 128)), _full((8, D_POOL)), _ANY],
        out_specs=[chunk(D_POOL), chunk(D_POOL)],
        out_shape=[sds((n_pad, D_POOL), BF16), sds((n_pad, D_POOL), BF16)],
        scratch_shapes=[pltpu.VMEM((POOL_HALO, D_POOL), F32)],
        compiler_params=_params(("arbitrary",)),
    )(v, pw_b, vecs, token)


def _pool_bwd(dmp, feat, pw_b, vecs, tc):
    n_pad = dmp.shape[0]
    nc = n_pad // tc

    def body(dmp_ref, feat_ref, pw_ref, vec_ref, dv_ref, dpw_ref, dvec_ref, fut):
        c = pl.program_id(0)

        @pl.when(c == 0)
        def _():
            fut[...] = jnp.zeros_like(fut)
            dpw_ref[...] = jnp.zeros_like(dpw_ref)
            dvec_ref[...] = jnp.zeros_like(dvec_ref)

        scale, gp = vec_ref[0:1, :], vec_ref[1:2, :]
        feat_v = feat_ref[...]
        p = jnp.concatenate([_dot(feat_v[:, 128 * k:128 * (k + 1)], pw_ref[k]) for k in range(4)], axis=1)
        yp = p * scale
        r = _rms(yp)
        xhat = yp * r
        dm = dmp_ref[...]
        dyp = _rms_bwd(dm * gp, xhat, r)
        dvec_ref[0:1, :] += jnp.sum(dyp * p, axis=0, keepdims=True)
        dvec_ref[1:2, :] += jnp.sum(dm * xhat, axis=0, keepdims=True)
        dpb = (dyp * scale).astype(BF16)
        es, dfs = [], []
        for k, w in enumerate(POOL_WINDOWS):
            cols = slice(128 * k, 128 * (k + 1))
            dpw_ref[k] += _dot_tn(feat_v[:, cols], dpb[:, cols])
            df = _dot_nt(dpb[:, cols], pw_ref[k])
            dfs.append(df)
            es.append(df * _inv_count(nc - 1 - c, tc, w))
        e = jnp.concatenate(es, axis=1)
        ext = jnp.concatenate([e, fut[...]], axis=0)
        fut[...] = e[:POOL_HALO, :]
        n_ext = tc + POOL_HALO
        dvs = []
        for k, w in enumerate(POOL_WINDOWS):
            s = ext[:, 128 * k:128 * (k + 1)]
            sh = 1
            while sh < w:
                s = s + pltpu.roll(s, n_ext - sh, 0)
                sh *= 2
            dvs.append(s[:tc, :] - dfs[k])
        dv_ref[...] = jnp.concatenate(dvs, axis=1)

    chunk = lambda w: pl.BlockSpec((tc, w), lambda c: (nc - 1 - c, 0))
    sds = jax.ShapeDtypeStruct
    return pl.pallas_call(
        body, grid=(nc,), name="pool_bwd",
        in_specs=[chunk(D_POOL), chunk(D_POOL), _full((4, 128, 128)), _full((8, D_POOL))],
        out_specs=[chunk(D_POOL), _full((4, 128, 128)), _full((8, D_POOL))],
        out_shape=[sds((n_pad, D_POOL), F32), sds((4, 128, 128), F32), sds((8, D_POOL), F32)],
        scratch_shapes=[pltpu.VMEM((POOL_HALO, D_POOL), F32)],
        compiler_params=_params(("arbitrary",)),
    )(dmp, feat, pw_b, vecs)


def _place():
    x, y, c = lax.axis_index("x"), lax.axis_index("y"), lax.axis_index("c")
    chips = [(1 - x, y), (x, 1 - y), (1 - x, 1 - y)]
    return x, y, c, chips


_ANY = pl.BlockSpec(memory_space=pl.ANY)


def _cast_shards(shards, dtypes, place):
    n = len(shards)

    def body(place_ref, *refs):
        for i in range(n):
            refs[n + i][0] = refs[i][...].astype(dtypes[i])

    return pl.pallas_call(
        body, name="cast_shards",
        grid_spec=pltpu.PrefetchScalarGridSpec(
            num_scalar_prefetch=1, grid=(1,),
            in_specs=[pl.BlockSpec(s.shape, lambda i, p: (0, 0, 0)) for s in shards],
            out_specs=[pl.BlockSpec((1,) + s.shape, lambda i, p: (p[0], 0, 0, 0)) for s in shards]),
        out_shape=[jax.ShapeDtypeStruct((N_SHARD,) + s.shape, dt) for s, dt in zip(shards, dtypes)],
        compiler_params=_params(("arbitrary",)),
    )(place, *shards)


def _gather_shards(full):
    n = len(full)

    def body(*refs):
        outs = refs[n:2 * n]
        ici_send, ici_recv, d2d_send, d2d_recv = refs[2 * n:]
        x, y, c, chips = _place()
        q = 2 * x + y
        sibling = (x, y, 1 - c)

        def ici(i, j, shard, to):
            return pltpu.make_async_remote_copy(src_ref=outs[i].at[q, c], dst_ref=outs[i].at[shard, c],
                                                send_sem=ici_send.at[i, j], recv_sem=ici_recv.at[i, j],
                                                device_id=to, device_id_type=MESH)

        def d2d(i, j, shard, half):
            return pltpu.make_async_remote_copy(src_ref=outs[i].at[shard, c], dst_ref=outs[i].at[shard, half],
                                                send_sem=d2d_send.at[i, j], recv_sem=d2d_recv.at[i, j],
                                                device_id=sibling, device_id_type=MESH)

        sends = [ici(i, j, q, (*chip, c)) for i in range(n) for j, chip in enumerate(chips)]
        for cp in sends:
            cp.start()
        passed = []
        for i in range(n):
            for j, (cx, cy) in enumerate(chips):
                ici(i, j, 2 * cx + cy, (cx, cy, c)).wait_recv()
                cp = d2d(i, j, 2 * cx + cy, c)
                cp.start()
                passed.append(cp)
        for i in range(n):
            for j, (cx, cy) in enumerate(chips):
                d2d(i, j, 2 * cx + cy, 1 - c).wait_recv()
        for cp in sends + passed:
            cp.wait_send()

    return pl.pallas_call(
        body, name="gather_shards",
        in_specs=[_ANY] * n, out_specs=[_ANY] * n,
        out_shape=[jax.ShapeDtypeStruct(f.shape, f.dtype) for f in full],
        input_output_aliases={i: i for i in range(n)},
        scratch_shapes=[pltpu.SemaphoreType.DMA((n, 3)), pltpu.SemaphoreType.DMA((n, 3)),
                        pltpu.SemaphoreType.DMA((n, 3)), pltpu.SemaphoreType.DMA((n, 3))],
    )(*full)


_HBM = pl.BlockSpec(memory_space=pltpu.HBM)
_SEM = pl.BlockSpec(memory_space=pltpu.SEMAPHORE)
_EFFECT = pltpu.SideEffectType.DATAFLOW_SIDE_EFFECTING


def _copies_start(name, arrays, sem_shape, build, after=None):
    n = len(arrays)
    extra = [] if after is None else [after]

    def body(*refs):
        outs = refs[n + len(extra):2 * n + len(extra)]
        send, recv, token = refs[2 * n + len(extra):]
        sends, _ = build(outs, send, recv)
        for cp in sends:
            cp.start()
        token[...] = jnp.zeros_like(token)

    out = pl.pallas_call(
        body, name=name, in_specs=[_HBM] * n + [_ANY] * len(extra),
        out_specs=[_HBM] * n + [_SEM, _SEM, pl.BlockSpec(memory_space=pltpu.VMEM)],
        out_shape=[pltpu.HBM(a.shape, a.dtype) for a in arrays]
        + [pltpu.SemaphoreType.DMA(sem_shape), pltpu.SemaphoreType.DMA(sem_shape), jax.ShapeDtypeStruct((8, 128), F32)],
        input_output_aliases={i: i for i in range(n)},
        compiler_params=pltpu.CompilerParams(has_side_effects=_EFFECT),
    )(*[pltpu.with_memory_space_constraint(a, pltpu.HBM) for a in arrays], *extra)
    return list(out[:n]), (out[n], out[n + 1]), out[n + 2]


def _copies_wait(name, arrays, sems, after, build):
    n = len(arrays)

    def body(*refs):
        ins = refs[:n]
        send, recv = refs[n], refs[n + 1]
        sends, recvs = build(ins, send, recv)
        for cp in sends:
            cp.wait_send()
        for cp in recvs:
            cp.wait_recv()

    return list(pl.pallas_call(
        body, name=name, in_specs=[_HBM] * n + [_SEM, _SEM] + [_ANY] * len(after), out_specs=[_HBM] * n,
        out_shape=[pltpu.HBM(a.shape, a.dtype) for a in arrays],
        input_output_aliases={i: i for i in range(n)},
        compiler_params=pltpu.CompilerParams(has_side_effects=_EFFECT),
    )(*arrays, *sems, *after))


def _remote(src, dst, send_sem, recv_sem, to):
    return pltpu.make_async_remote_copy(src_ref=src, dst_ref=dst, send_sem=send_sem, recv_sem=recv_sem,
                                        device_id=to, device_id_type=MESH)


def _build_gather(refs, send, recv):
    x, y, c, chips = _place()
    q = 2 * x + y
    pairs = [(i, j, chip) for i in range(len(refs)) for j, chip in enumerate(chips)]
    sends = [_remote(refs[i].at[q, c], refs[i].at[q, c], send.at[3 * i + j], recv.at[3 * i + j], (cx, cy, c))
             for i, j, (cx, cy) in pairs]
    recvs = [_remote(refs[i].at[q, c], refs[i].at[2 * cx + cy, c], send.at[3 * i + j], recv.at[3 * i + j], (cx, cy, c))
             for i, j, (cx, cy) in pairs]
    return sends, recvs


def _build_forward(refs, send, recv):
    x, y, c, chips = _place()
    pairs = [(i, j, 2 * cx + cy) for i in range(len(refs)) for j, (cx, cy) in enumerate(chips)]
    sends = [_remote(refs[i].at[s, c], refs[i].at[s, c], send.at[3 * i + j], recv.at[3 * i + j], (x, y, 1 - c))
             for i, j, s in pairs]
    recvs = [_remote(refs[i].at[s, c], refs[i].at[s, 1 - c], send.at[3 * i + j], recv.at[3 * i + j], (x, y, 1 - c))
             for i, j, s in pairs]
    return sends, recvs


def _build_swap(refs, send, recv):
    x, y, c, _ = _place()
    n = len(refs) // 2
    cps = [_remote(refs[i].at[:, 1 - c], refs[n + i], send.at[i], recv.at[i], (x, y, 1 - c)) for i in range(n)]
    return cps, cps


def _build_exchange(refs, send, recv):
    x, y, c, chips = _place()
    n = len(refs) // 2
    cps = [_remote(refs[i].at[2 * cx + cy], refs[n + i].at[j], send.at[3 * i + j], recv.at[3 * i + j], (cx, cy, c))
           for i in range(n) for j, (cx, cy) in enumerate(chips)]
    return cps, cps


def _build_spread(refs, send, recv):
    x, y, c, _ = _place()
    flip = lambda bit, on: bit + on - 2 * bit * on
    cps = [_remote(refs[0], refs[1].at[r - 1], send.at[r - 1], recv.at[r - 1],
                   (flip(x, r >> 2 & 1), flip(y, r >> 1 & 1), flip(c, r & 1))) for r in (1, 2, 4, 6)]
    return cps, cps


def _forward_small(landed):
    def body(in_ref, out_ref, send, recv):
        x, y, c, _ = _place()
        cps = [_remote(out_ref.at[r - 1], out_ref.at[r], send.at[k], recv.at[k], (x, y, 1 - c))
               for k, r in enumerate((2, 4, 6))]
        for cp in cps:
            cp.start()
        for cp in cps:
            cp.wait()

    return pl.pallas_call(
        body, name="forward_small",
        in_specs=[_ANY], out_specs=_ANY, out_shape=jax.ShapeDtypeStruct(landed.shape, F32),
        input_output_aliases={0: 0},
        scratch_shapes=[pltpu.SemaphoreType.DMA((3,)), pltpu.SemaphoreType.DMA((3,))],
    )(landed)


def _swap_halves(grads):
    n = len(grads)

    def body(*refs):
        ins, outs = refs[:n], refs[n:2 * n]
        send, recv = refs[2 * n:]
        x, y, c, _ = _place()
        cps = [pltpu.make_async_remote_copy(src_ref=ins[i].at[:, 1 - c], dst_ref=outs[i], send_sem=send.at[i],
                                            recv_sem=recv.at[i], device_id=(x, y, 1 - c), device_id_type=MESH)
               for i in range(n)]
        for cp in cps:
            cp.start()
        for cp in cps:
            cp.wait()

    return pl.pallas_call(
        body, name="swap_halves",
        in_specs=[_ANY] * n, out_specs=[_ANY] * n,
        out_shape=[jax.ShapeDtypeStruct((N_SHARD,) + g.shape[2:], F32) for g in grads],
        scratch_shapes=[pltpu.SemaphoreType.DMA((n,)), pltpu.SemaphoreType.DMA((n,))],
    )(*grads)


def _join_halves(pairs):
    n = len(pairs)

    def body(*refs):
        outs = refs[n:2 * n]
        send, recv = refs[2 * n:]
        x, y, c, _ = _place()
        cps = [pltpu.make_async_remote_copy(src_ref=outs[i].at[c], dst_ref=outs[i].at[c], send_sem=send.at[i],
                                            recv_sem=recv.at[i], device_id=(x, y, 1 - c), device_id_type=MESH)
               for i in range(n)]
        for cp in cps:
            cp.start()
        for i in range(n):
            cps[i].wait_send()
            pltpu.make_async_remote_copy(src_ref=outs[i].at[c], dst_ref=outs[i].at[1 - c], send_sem=send.at[i],
                                         recv_sem=recv.at[i], device_id=(x, y, 1 - c), device_id_type=MESH).wait_recv()

    return pl.pallas_call(
        body, name="join_halves",
        in_specs=[_ANY] * n, out_specs=[_ANY] * n,
        out_shape=[jax.ShapeDtypeStruct(p.shape, F32) for p in pairs],
        input_output_aliases={i: i for i in range(n)},
        scratch_shapes=[pltpu.SemaphoreType.DMA((n,)), pltpu.SemaphoreType.DMA((n,))],
    )(*pairs)


N_SPLIT = 2


def _sum_siblings(tag, grads, recvd, place):
    n = len(grads)

    def body(place_ref, *refs):
        g_refs, r_refs, sb_refs, own_refs = (refs[k * n:(k + 1) * n] for k in range(4))
        s = pl.program_id(1)
        for i in range(n):
            tot = g_refs[i][0, 0] + r_refs[i][0]
            sb_refs[i][0] = tot.astype(BF16)

            @pl.when(s == place_ref[0])
            def _():
                own_refs[i][...] = tot

    in_specs, sb_specs, own_specs, sb_shapes, own_shapes = [], [], [], [], []
    for g in grads:
        _, _, r, cdim = g.shape
        rb = r // N_SPLIT
        in_specs.append(pl.BlockSpec((1, 1, rb, cdim), lambda b, s, p: (s, p[1], b, 0)))
        sb_specs.append(pl.BlockSpec((1, rb, cdim), lambda b, s, p: (s, b, 0)))
        own_specs.append(pl.BlockSpec((rb, cdim), lambda b, s, p: (b, 0)))
        sb_shapes.append(jax.ShapeDtypeStruct((N_SHARD, r, cdim), BF16))
        own_shapes.append(jax.ShapeDtypeStruct((r, cdim), F32))
    out = pl.pallas_call(
        body, name="sum_siblings_" + tag,
        grid_spec=pltpu.PrefetchScalarGridSpec(
            num_scalar_prefetch=1, grid=(N_SPLIT, N_SHARD),
            in_specs=in_specs + sb_specs, out_specs=sb_specs + own_specs),
        out_shape=sb_shapes + own_shapes,
        compiler_params=_params(("parallel", "arbitrary")),
    )(place, *grads, *recvd)
    return out[:n], out[n:]


def _sum_chips(own, recvd, place):
    n = len(own)

    def body(place_ref, *refs):
        o_refs, r_refs, out_refs = (refs[k * n:(k + 1) * n] for k in range(3))
        for i in range(n):
            tot = o_refs[i][...]
            for j in range(3):
                tot = tot + r_refs[i][j].astype(F32)
            out_refs[i][0] = tot

    o_specs, r_specs, out_specs = [], [], []
    for o in own:
        r, cdim = o.shape
        rb = r // N_SPLIT
        o_specs.append(pl.BlockSpec((rb, cdim), lambda b, p: (b, 0)))
        r_specs.append(pl.BlockSpec((3, rb, cdim), lambda b, p: (0, b, 0)))
        out_specs.append(pl.BlockSpec((1, rb, cdim), lambda b, p: (p[1], b, 0)))
    return pl.pallas_call(
        body, name="sum_chips",
        grid_spec=pltpu.PrefetchScalarGridSpec(num_scalar_prefetch=1, grid=(N_SPLIT,),
                                               in_specs=o_specs + r_specs, out_specs=out_specs),
        out_shape=[jax.ShapeDtypeStruct((2,) + o.shape, F32) for o in own],
        compiler_params=_params(("parallel",)),
    )(place, *own, *recvd)


def _adamw_math(w, g, m, v):
    m = ADAM_B1 * m + (1.0 - ADAM_B1) * g
    v = ADAM_B2 * v + (1.0 - ADAM_B2) * (g * g)
    m_hat = m / (1.0 - ADAM_B1 ** ADAM_STEP)
    v_hat = v / (1.0 - ADAM_B2 ** ADAM_STEP)
    delta = -ADAM_LR * (m_hat / (jnp.sqrt(v_hat) + ADAM_EPS) + ADAM_WD * w)
    return delta, m, v


def _adamw(name, ws, gs, ms, vs, n_split):
    n = len(ws)

    def body(*refs):
        w_r, g_r, m_r, v_r, d_o, m_o, v_o = (refs[k * n:(k + 1) * n] for k in range(7))
        for i in range(n):
            d, m, v = _adamw_math(w_r[i][...], g_r[i][...], m_r[i][...], v_r[i][...])
            d_o[i][...] = d
            m_o[i][...] = m
            v_o[i][...] = v

    specs = [pl.BlockSpec((w.shape[0] // n_split, w.shape[1]), lambda b: (b, 0)) for w in ws]
    shapes = [jax.ShapeDtypeStruct(w.shape, F32) for w in ws]
    out = pl.pallas_call(
        body, name=name, grid=(n_split,),
        in_specs=specs * 4, out_specs=specs * 3, out_shape=shapes * 3,
        compiler_params=_params(("parallel",)),
    )(*ws, *gs, *ms, *vs)
    return out[:n], out[n:2 * n], out[2 * n:]


def _reduce_small(own, received):
    def body(own_ref, recv_ref, g_out):
        me = 4 * lax.axis_index("x") + 2 * lax.axis_index("y") + lax.axis_index("c")
        g = None
        for k in range(8):
            mine = me == k
            part = jnp.where(mine, own_ref[...], recv_ref[jnp.where(mine, 0, jnp.bitwise_xor(me, k) - 1)])
            g = part if g is None else g + part
        g_out[...] = g

    return pl.pallas_call(
        body, name="reduce_small",
        out_shape=jax.ShapeDtypeStruct(own.shape, F32),
        compiler_params=_params(),
    )(own, received)


def _adamw_small(ws, gs, ms, vs):
    n = len(ws)

    def body(*refs):
        w_r, g_r, m_r, v_r, d_o, m_o, v_o = (refs[k * n:(k + 1) * n] for k in range(7))
        for i in range(n):
            d, mm, vv = _adamw_math(w_r[i][...], g_r[i][...], m_r[i][...], v_r[i][...])
            d_o[i][...] = d
            m_o[i][...] = mm
            v_o[i][...] = vv

    out = pl.pallas_call(
        body, name="adamw_small",
        out_shape=[jax.ShapeDtypeStruct(t.shape, F32) for t in ws] * 3,
        compiler_params=_params(),
    )(*ws, *gs, *ms, *vs)
    return out[:n], out[n:2 * n], out[2 * n:]


def _s5_operands(lam_re, lam_im, log_step, b_re, b_im, c_re, c_im, glu_w):
    lr = jnp.minimum(lam_re, -1e-4)
    li = lam_im
    step = jnp.exp(log_step)[:, None]
    mag = jnp.exp(lr * step)
    ang = li * step
    abr = mag * jnp.cos(ang)
    abi = mag * jnp.sin(ang)
    nr = abr - 1.0
    ni = abi
    den = lr * lr + li * li
    cr = ((nr * lr + ni * li) / den)[..., None]
    ci = ((ni * lr - nr * li) / den)[..., None]
    bbr = cr * b_re - ci * b_im
    bbi = cr * b_im + ci * b_re
    eye = jnp.eye(8, dtype=F32)
    g, h, p = SSM_GROUPS // SUPER, SSM_GROUP, SSM_STATE

    def b_layout(t):
        return jnp.einsum("ab,japh->jahbp", eye, t.reshape(SUPER, g, p, h)).reshape(SUPER, g * h, g * p)

    def c_layout(t):
        return jnp.einsum("ab,jahp->jbpah", eye, t.reshape(SUPER, g, h, p)).reshape(SUPER, g * p, g * h)

    glu = jnp.einsum("ab,jahk->jahbk", eye, glu_w.reshape(SUPER, g, h, h)).reshape(SUPER, g * h, g * h)
    lam = _pad_rows(jnp.concatenate([abr.reshape(1, N_STATE), abi.reshape(1, N_STATE)], axis=0), 8)
    return lam, b_layout(bbr), b_layout(bbi), c_layout(c_re), c_layout(c_im), glu


def _pad_rows(a, rows):
    return jnp.pad(a, ((0, rows - a.shape[0]), (0, 0)))


def _pack(parts):
    rows = []
    for a in parts:
        flat = a.reshape(-1)
        n = -(-flat.shape[0] // 128)
        rows.append(jnp.pad(flat, (0, n * 128 - flat.shape[0])).reshape(n, 128))
    out = jnp.concatenate(rows, axis=0)
    return _pad_rows(out, -(-out.shape[0] // 8) * 8)


def _unpack(packed, like):
    out, at = [], 0
    for a in like:
        n = -(-a.size // 128)
        out.append(packed[at:at + n].reshape(-1)[:a.size].reshape(a.shape))
        at += n
    return out


STORED = {"ssm_b_re": (0, 1, 3, 2), "ssm_b_im": (0, 1, 3, 2), "ssm_d": (0, 2, 1), "ssm_glu_b": (0, 2, 1),
          "ssm_glu_w": (0, 2, 3, 1)}


def _stored(k, a):
    return a.transpose(STORED[k]) if k in STORED else a


def _logical(k, a):
    return a.transpose(tuple(STORED[k].index(i) for i in range(a.ndim))) if k in STORED else a


SMALL = ("norm1_g", "ssm_lambda_re", "ssm_lambda_im", "ssm_log_step", "ssm_b_re", "ssm_b_im", "ssm_c_re", "ssm_c_im",
         "ssm_d", "ssm_glu_w", "ssm_glu_b", "ssm_norm_g", "pool_w", "pool_scale", "pool_norm_g", "norm2_g",
         "final_norm_g")
LARGE = ("w_in", "w_out", "w_gate", "w_up", "w_down")
WEIGHTS = ("meta_tokens", "norm1_g", "w_in", "ssm_lambda_re", "ssm_lambda_im", "ssm_log_step", "ssm_b_re", "ssm_b_im",
           "ssm_c_re", "ssm_c_im", "ssm_d", "ssm_glu_w", "ssm_glu_b", "ssm_norm_g", "pool_w", "pool_scale",
           "pool_norm_g", "w_out", "norm2_g", "w_gate", "w_up", "w_down", "final_norm_g")


def _step(x, target, w, m, v):
    seq = x.shape[1]
    n_rows = N_META + seq
    n_pad, tm, tp, tc, tg = _plan(n_rows)
    xq, yq, cq = lax.axis_index("x"), lax.axis_index("y"), lax.axis_index("c")
    place = jnp.stack([2 * xq + yq, cq]).astype(jnp.int32)

    def halves(a2d):
        return a2d.reshape(2, a2d.shape[0] // 2, a2d.shape[1])

    def local2d(t):
        return {"w_gate": lambda a: a[0].T, "w_up": lambda a: a[0].T}.get(t, lambda a: a[0])

    shards = [halves(local2d(k)(w[k])) for k in LARGE] + [halves(w["meta_tokens"])]
    full = _cast_shards(shards, [BF16] * len(LARGE) + [F32], place)
    w_in_full, meta_full = _gather_shards([full[0], full[5]])
    late, gather_sems, gather_token = _copies_start("gather_start", list(full[1:5]), (12,), _build_gather,
                                                    after=w_in_full)
    w_in_b = w_in_full.reshape(D_MODEL, D_MODEL)
    meta = meta_full.reshape(N_SHARD, N_META, D_MODEL // N_SHARD).transpose(1, 0, 2).reshape(N_META, D_MODEL)

    h0 = _pad_rows(jnp.concatenate([meta, x[0]], axis=0), n_pad)
    tgt = _pad_rows(jnp.concatenate([jnp.zeros((N_META, D_MODEL), F32), target[0]], axis=0), n_pad)
    s5_in = (w["ssm_lambda_re"][0], w["ssm_lambda_im"][0], w["ssm_log_step"][0], w["ssm_b_re"][0], w["ssm_b_im"][0],
             w["ssm_c_re"][0], w["ssm_c_im"][0], w["ssm_glu_w"][0])
    (lam, bbr, bbi, crt, cit, glu), s5_vjp = jax.vjp(_s5_operands, *s5_in)
    bbr_b, bbi_b, crt_b, cit_b, glu_b16 = (t.astype(BF16) for t in (bbr, bbi, crt, cit, glu))
    s5_vecs = _pad_rows(jnp.concatenate([w["ssm_d"].reshape(1, D_SSM), w["ssm_glu_b"].reshape(1, D_SSM),
                                         w["ssm_norm_g"].reshape(1, D_SSM)], axis=0), 8)
    pool_vecs = _pad_rows(jnp.concatenate([w["pool_scale"].reshape(1, D_POOL), w["pool_norm_g"].reshape(1, D_POOL)],
                                          axis=0), 8)
    pw_b = w["pool_w"][0].astype(BF16)
    g1, g2, gf = w["norm1_g"].reshape(1, D_MODEL), w["norm2_g"].reshape(1, D_MODEL), w["final_norm_g"].reshape(1, D_MODEL)

    u, vv = _fwd_in(h0, g1, w_in_b, tp, gather_token)
    sr, si, y, ms = _s5_fwd(u, lam, bbr_b, bbi_b, crt_b, cit_b, s5_vecs, glu_b16, tc)
    late = _copies_wait("gather_wait", late, gather_sems, [ms], _build_gather)
    late, forward_sems, forward_token = _copies_start("forward_start", late, (12,), _build_forward)
    feat, mp = _pool_fwd(vv, pw_b, pool_vecs, tc, forward_token)
    late = _copies_wait("forward_wait", late, forward_sems, [mp], _build_forward)
    w_out_b = late[0].reshape(D_MODEL, D_MODEL)
    wg_b, wu_b, wd_b = (t.reshape(N_SHARD, FF_SHARD, D_MODEL) for t in late[1:])
    h1, n2, a, b, ff, dh2, dh2b, loss_acc, dgf = _fwd_ffn(h0, ms, mp, w_out_b, g2, wg_b, wu_b, wd_b, gf, tgt, tc, n_rows)

    def quarters(t):
        if t.ndim == 2:
            t = t.reshape(N_SHARD, t.shape[0] // N_SHARD, t.shape[1])
        return t.reshape(N_SHARD, 2, t.shape[1] // 2, t.shape[2])

    def landing(like, lead, dtype):
        return [lax.empty((lead,) + t.shape[2:], dtype) for t in like]

    da, db, dh1, dg2 = _bwd_ffn(dh2, dh2b, a, b, wg_b, wu_b, wd_b, h1, g2, tm)
    ffn_g = [quarters(t) for t in _grad_ffn(n2, da, db, ff, dh2b, tg)]
    nf = len(ffn_g)
    moved, swap_sems, swap_token = _copies_start("swap_start", ffn_g + landing(ffn_g, N_SHARD, F32), (nf,), _build_swap)
    dms, dmp, dwo = _bwd_out(dh1, ms, mp, w_out_b, tp, swap_token)
    moved = _copies_wait("swap_wait", moved, swap_sems, [dwo], _build_swap)
    ffn_parts, ffn_own = _sum_siblings("ffn", moved[:nf], moved[nf:], place)
    moved, exch_sems, exch_token = _copies_start("exchange_start", list(ffn_parts) + landing(ffn_g, 3, BF16), (3 * nf,),
                                                 _build_exchange)
    du, dbbr, dbbi, dcrt, dcit, dglu, ds5v, dlam = _s5_bwd(dms, y, u, sr, si, lam, bbr_b, bbi_b, crt_b, cit_b,
                                                           s5_vecs, glu_b16, tc, exch_token)
    dv, dpw, dpoolv = _pool_bwd(dmp, feat, pw_b, pool_vecs, tc)
    dh0, dwi, dg1 = _bwd_in(du, dv, h0, dh1, g1, w_in_b, tp)
    ffn_from_chips = _copies_wait("exchange_wait", moved, exch_sems, [dh0], _build_exchange)[nf:]
    dlam = _pad_rows(jnp.concatenate([jnp.sum(dlam[:SEGMENTS], axis=0, keepdims=True),
                                      jnp.sum(dlam[SEGMENTS:], axis=0, keepdims=True)], axis=0), 8)
    d_lre, d_lim, d_lstep, d_bre, d_bim, d_cre, d_cim, d_gluw = s5_vjp((dlam, dbbr, dbbi, dcrt, dcit, dglu))
    grad_x = dh0[N_META:n_rows][None]

    small_g = {
        "norm1_g": dg1, "ssm_lambda_re": d_lre, "ssm_lambda_im": d_lim, "ssm_log_step": d_lstep, "ssm_b_re": d_bre,
        "ssm_b_im": d_bim, "ssm_c_re": d_cre, "ssm_c_im": d_cim, "ssm_d": ds5v[0], "ssm_glu_w": d_gluw,
        "ssm_glu_b": ds5v[1], "ssm_norm_g": ds5v[2], "pool_w": dpw, "pool_scale": dpoolv[0], "pool_norm_g": dpoolv[1],
        "norm2_g": dg2, "final_norm_g": dgf,
    }
    like = [_stored(k, w[k]) for k in SMALL]
    packed_g = _pack([_stored(k, small_g[k].reshape(w[k].shape)) for k in SMALL] + [dh0[:N_META], loss_acc[0:1, 0:1]])

    mix_g = [quarters(t) for t in (dwi, dwo)]
    mix_parts, mix_own = _sum_siblings("mix", mix_g, _swap_halves(mix_g), place)
    moved, mix_sems, mix_token = _copies_start("mix_exchange_start", list(mix_parts) + landing(mix_g, 3, BF16),
                                               (3 * len(mix_g),), _build_exchange)
    spread, small_sems, small_token = _copies_start(
        "small_start", [packed_g, lax.empty((7,) + packed_g.shape, F32)], (7,), _build_spread, after=mix_token)
    mix_from_chips = _copies_wait("mix_exchange_wait", moved, mix_sems, [small_token], _build_exchange)[len(mix_g):]
    joined = _join_halves(_sum_chips(list(mix_own) + list(ffn_own), list(mix_from_chips) + list(ffn_from_chips), place))
    g_large = [j.reshape(j.shape[0] * j.shape[1], j.shape[2]) for j in joined]
    w2d, m2d, v2d = ([local2d(k)(t[k]) for k in LARGE] for t in (w, m, v))
    d_large, m_large, v_large = _adamw("adamw_large", w2d, g_large, m2d, v2d, 8)

    own_g, landed = _copies_wait("small_wait", spread, small_sems, [d_large[0]], _build_spread)
    g_pk = _reduce_small(own_g, _forward_small(landed))
    g_small = _unpack(g_pk, like + [jax.ShapeDtypeStruct((N_META, D_MODEL), F32), jax.ShapeDtypeStruct((1, 1), F32)])
    loss = g_small.pop()[0, 0]
    rows2d = lambda t: t.reshape(1, -1) if t.ndim == 1 else t
    d_small, m_small, v_small = _adamw_small(*([rows2d(t) for t in ts] for ts in (
        like, g_small[:-1], [_stored(k, m[k]) for k in SMALL], [_stored(k, v[k]) for k in SMALL])))
    g_small, d_small, m_small, v_small = ([_logical(k, t.reshape(a.shape)) for t, a, k in zip(ts, like, SMALL)] + ts[len(SMALL):]
                                          for ts in (g_small, list(d_small), list(m_small), list(v_small)))
    q = place[0]
    g_meta = lax.dynamic_slice_in_dim(g_small[-1], q * (D_MODEL // N_SHARD), D_MODEL // N_SHARD, axis=1)
    d_meta, m_meta, v_meta = _adamw("adamw_meta", [w["meta_tokens"]], [g_meta], [m["meta_tokens"]],
                                    [v["meta_tokens"]], 1)

    grads, deltas, new_m, new_v = {}, {}, {}, {}
    for i, k in enumerate(SMALL):
        grads[k], deltas[k], new_m[k], new_v[k] = g_small[i], d_small[i], m_small[i], v_small[i]
    for i, k in enumerate(LARGE):
        back = (lambda t: t.T[None]) if k in ("w_gate", "w_up") else (lambda t: t[None])
        grads[k], deltas[k], new_m[k], new_v[k] = (back(t) for t in (g_large[i], d_large[i], m_large[i], v_large[i]))
    grads["meta_tokens"], deltas["meta_tokens"] = g_meta, d_meta[0]
    new_m["meta_tokens"], new_v["meta_tokens"] = m_meta[0], v_meta[0]
    return (loss, grad_x, *[grads[k] for k in WEIGHTS], *[deltas[k] for k in WEIGHTS],
            *[new_m[k] for k in WEIGHTS], *[new_v[k] for k in WEIGHTS])


def kernel(x, meta_tokens, norm1_g, w_in, ssm_lambda_re, ssm_lambda_im, ssm_log_step, ssm_b_re, ssm_b_im, ssm_c_re, ssm_c_im, ssm_d, ssm_glu_w, ssm_glu_b, ssm_norm_g, pool_w, pool_scale, pool_norm_g, w_out, norm2_g, w_gate, w_up, w_down, final_norm_g, loss_target, m_meta_tokens, m_norm1_g, m_w_in, m_ssm_lambda_re, m_ssm_lambda_im, m_ssm_log_step, m_ssm_b_re, m_ssm_b_im, m_ssm_c_re, m_ssm_c_im, m_ssm_d, m_ssm_glu_w, m_ssm_glu_b, m_ssm_norm_g, m_pool_w, m_pool_scale, m_pool_norm_g, m_w_out, m_norm2_g, m_w_gate, m_w_up, m_w_down, m_final_norm_g, v_meta_tokens, v_norm1_g, v_w_in, v_ssm_lambda_re, v_ssm_lambda_im, v_ssm_log_step, v_ssm_b_re, v_ssm_b_im, v_ssm_c_re, v_ssm_c_im, v_ssm_d, v_ssm_glu_w, v_ssm_glu_b, v_ssm_norm_g, v_pool_w, v_pool_scale, v_pool_norm_g, v_w_out, v_norm2_g, v_w_gate, v_w_up, v_w_down, v_final_norm_g):
    w = dict(meta_tokens=meta_tokens, norm1_g=norm1_g, w_in=w_in, ssm_lambda_re=ssm_lambda_re, ssm_lambda_im=ssm_lambda_im, ssm_log_step=ssm_log_step, ssm_b_re=ssm_b_re, ssm_b_im=ssm_b_im, ssm_c_re=ssm_c_re, ssm_c_im=ssm_c_im, ssm_d=ssm_d, ssm_glu_w=ssm_glu_w, ssm_glu_b=ssm_glu_b, ssm_norm_g=ssm_norm_g, pool_w=pool_w, pool_scale=pool_scale, pool_norm_g=pool_norm_g, w_out=w_out, norm2_g=norm2_g, w_gate=w_gate, w_up=w_up, w_down=w_down, final_norm_g=final_norm_g)
    m = dict(meta_tokens=m_meta_tokens, norm1_g=m_norm1_g, w_in=m_w_in, ssm_lambda_re=m_ssm_lambda_re, ssm_lambda_im=m_ssm_lambda_im, ssm_log_step=m_ssm_log_step, ssm_b_re=m_ssm_b_re, ssm_b_im=m_ssm_b_im, ssm_c_re=m_ssm_c_re, ssm_c_im=m_ssm_c_im, ssm_d=m_ssm_d, ssm_glu_w=m_ssm_glu_w, ssm_glu_b=m_ssm_glu_b, ssm_norm_g=m_ssm_norm_g, pool_w=m_pool_w, pool_scale=m_pool_scale, pool_norm_g=m_pool_norm_g, w_out=m_w_out, norm2_g=m_norm2_g, w_gate=m_w_gate, w_up=m_w_up, w_down=m_w_down, final_norm_g=m_final_norm_g)
    v = dict(meta_tokens=v_meta_tokens, norm1_g=v_norm1_g, w_in=v_w_in, ssm_lambda_re=v_ssm_lambda_re, ssm_lambda_im=v_ssm_lambda_im, ssm_log_step=v_ssm_log_step, ssm_b_re=v_ssm_b_re, ssm_b_im=v_ssm_b_im, ssm_c_re=v_ssm_c_re, ssm_c_im=v_ssm_c_im, ssm_d=v_ssm_d, ssm_glu_w=v_ssm_glu_w, ssm_glu_b=v_ssm_glu_b, ssm_norm_g=v_ssm_norm_g, pool_w=v_pool_w, pool_scale=v_pool_scale, pool_norm_g=v_pool_norm_g, w_out=v_w_out, norm2_g=v_norm2_g, w_gate=v_w_gate, w_up=v_w_up, w_down=v_w_down, final_norm_g=v_final_norm_g)
    return _step(x, loss_target, w, m, v)
```

```python
import functools
import math

import jax
import jax.numpy as jnp
from jax import lax
from jax.experimental import pallas as pl
from jax.experimental.pallas import tpu as pltpu

F32 = jnp.float32
BF16 = jnp.bfloat16
MESH = pl.DeviceIdType.MESH
AXES = ("x", "y", "c")

D_MODEL = 1024
D_SSM = 512
D_POOL = 512
N_META = 16
SSM_GROUP = 16
SSM_GROUPS = 32
SSM_STATE = 64
N_STATE = SSM_GROUPS * SSM_STATE
STATE_BLOCKS = N_STATE // 128
SUPER = 4
POOL_WINDOWS = (2, 4, 8, 16)
POOL_HALO = 16
D_FF = 2816
N_SHARD = 4
FF_SHARD = D_FF // N_SHARD
EPS = 1e-6
ADAM_LR, ADAM_B1, ADAM_B2, ADAM_EPS, ADAM_WD, ADAM_STEP = 0.001, 0.9, 0.999, 1e-08, 0.01, 10
VMEM_LIMIT = 56 * 1024 * 1024


def _plan(n_rows):
    if n_rows > 2048:
        tm, tp, tc, tg = 416, 832, 320, 1040
    else:
        tm, tp, tc, tg = 128, 128, 64, 128
    step = math.lcm(tm, tp, tc, tg)
    return -(-n_rows // step) * step, tm, tp, tc, tg


def _params(sem=None):
    return pltpu.CompilerParams(dimension_semantics=sem, vmem_limit_bytes=VMEM_LIMIT)


def _dot(a, b):
    return jnp.dot(a, b, preferred_element_type=F32)


def _dot_nt(a, b):
    return lax.dot_general(a, b, (((1,), (1,)), ((), ())), preferred_element_type=F32)


def _dot_tn(a, b):
    return lax.dot_general(a, b, (((0,), (0,)), ((), ())), preferred_element_type=F32)


def _sigmoid(x):
    return 0.5 * jnp.tanh(0.5 * x) + 0.5


_GELU_C = math.sqrt(2.0 / math.pi)


def _gelu_and_grad(y):
    y2 = y * y
    t = jnp.tanh(_GELU_C * (y + 0.044715 * y * y2))
    g = 0.5 * y * (1.0 + t)
    dg = 0.5 * (1.0 + t) + 0.5 * y * (1.0 - t * t) * (_GELU_C * (1.0 + 3.0 * 0.044715 * y2))
    return g, dg


def _rms(x):
    return lax.rsqrt(jnp.mean(x * x, axis=-1, keepdims=True) + EPS)


def _rms_bwd(dn, xhat, r):
    return r * (dn - xhat * jnp.mean(dn * xhat, axis=-1, keepdims=True))


def _full(shape):
    nd = len(shape)
    return pl.BlockSpec(shape, lambda *_: (0,) * nd)


def _fwd_in(h0, g1, w_in_b, tm, token):
    n_pad = h0.shape[0]

    def body(h_ref, g_ref, w_ref, token_ref, u_ref, v_ref):
        h = h_ref[...]
        n1 = (h * _rms(h) * g_ref[...]).astype(BF16)
        proj = _dot(n1, w_ref[...])
        for i in range(4):
            u_ref[i] = proj[:, 128 * i:128 * (i + 1)]
        v_ref[...] = proj[:, D_SSM:]

    row = lambda w: pl.BlockSpec((tm, w), lambda i: (i, 0))
    return pl.pallas_call(
        body, grid=(n_pad // tm,), name="fwd_in",
        in_specs=[row(D_MODEL), _full((1, D_MODEL)), _full((D_MODEL, D_MODEL)), _ANY],
        out_specs=[pl.BlockSpec((4, tm, 128), lambda i: (0, i, 0)), row(D_POOL)],
        out_shape=[jax.ShapeDtypeStruct((4, n_pad, 128), F32), jax.ShapeDtypeStruct((n_pad, D_POOL), F32)],
        compiler_params=_params(("parallel",)),
    )(h0, g1, w_in_b, token)


def _fwd_ffn(h0, ms, mp, w_out_b, g2, wg_b, wu_b, wd_b, gf, target, tm, n_valid):
    n_pad = h0.shape[0]
    nt = n_pad // tm

    def body(h0_ref, ms_ref, mp_ref, wo_ref, g2_ref, wg_hbm, wu_hbm, wd_hbm, gf_ref, tgt_ref,
             h1_ref, n2_ref, a_ref, b_ref, ff_ref, dh2_ref, dh2b_ref, loss_ref, dgf_ref,
             wg_ref, wu_ref, wd_ref, sems, acc):
        i, phase = pl.program_id(0), pl.program_id(1)

        @pl.when((i == 0) & (phase == 0))
        def _():
            loss_ref[...] = jnp.zeros_like(loss_ref)
            dgf_ref[...] = jnp.zeros_like(dgf_ref)
            cps = [pltpu.make_async_copy(src, dst, sems.at[k])
                   for k, (src, dst) in enumerate(((wg_hbm, wg_ref), (wu_hbm, wu_ref), (wd_hbm, wd_ref)))]
            for cp in cps:
                cp.start()
            for cp in cps:
                cp.wait()

        @pl.when(phase == 0)
        def _():
            h1 = h0_ref[...] + _dot(ms_ref[...], wo_ref[:D_SSM, :]) + _dot(mp_ref[...], wo_ref[D_SSM:, :])
            h1_ref[...] = h1
            acc[...] = h1
            n2_ref[...] = (h1 * _rms(h1) * g2_ref[...]).astype(BF16)
            n2 = n2_ref[...]
            for q in range(N_SHARD):
                a = _dot_nt(n2, wg_ref[q])
                b = _dot_nt(n2, wu_ref[q])
                a_ref[q] = a.astype(BF16)
                b_ref[q] = b.astype(BF16)
                ff = (a * _sigmoid(a) * b).astype(BF16)
                ff_ref[q] = ff
                acc[...] += _dot(ff, wd_ref[q])

        @pl.when(phase == 1)
        def _():
            h2 = acc[...]
            r = _rms(h2)
            xhat = h2 * r
            gf_row = gf_ref[...]
            rows = i * tm + lax.broadcasted_iota(jnp.int32, (tm, 1), 0)
            valid = (rows >= N_META) & (rows < n_valid)
            diff = jnp.where(valid, xhat * gf_row - tgt_ref[...], 0.0)
            loss_ref[...] += jnp.full(loss_ref.shape, 0.5 / D_MODEL, F32) * jnp.sum(diff * diff)
            dout = diff * (1.0 / D_MODEL)
            dgf_ref[...] += jnp.sum(dout * xhat, axis=0, keepdims=True)
            dh2 = _rms_bwd(dout * gf_row, xhat, r)
            dh2_ref[...] = dh2
            dh2b_ref[...] = dh2.astype(BF16)

    row = lambda w: pl.BlockSpec((tm, w), lambda i, p: (i, 0))
    act = pl.BlockSpec((N_SHARD, tm, FF_SHARD), lambda i, p: (0, i, 0))
    sds = jax.ShapeDtypeStruct
    return pl.pallas_call(
        body, grid=(nt, 2), name="fwd_ffn",
        in_specs=[row(D_MODEL), row(D_SSM), row(D_POOL), _full((D_MODEL, D_MODEL)), _full((1, D_MODEL)),
                  _ANY, _ANY, _ANY, _full((1, D_MODEL)), row(D_MODEL)],
        out_specs=[row(D_MODEL), row(D_MODEL), act, act, act, row(D_MODEL), row(D_MODEL), _full((8, 128)),
                   _full((1, D_MODEL))],
        out_shape=[sds((n_pad, D_MODEL), F32), sds((n_pad, D_MODEL), BF16),
                   sds((N_SHARD, n_pad, FF_SHARD), BF16), sds((N_SHARD, n_pad, FF_SHARD), BF16),
                   sds((N_SHARD, n_pad, FF_SHARD), BF16), sds((n_pad, D_MODEL), F32), sds((n_pad, D_MODEL), BF16),
                   sds((8, 128), F32), sds((1, D_MODEL), F32)],
        scratch_shapes=[pltpu.VMEM(wg_b.shape, BF16), pltpu.VMEM(wu_b.shape, BF16), pltpu.VMEM(wd_b.shape, BF16),
                        pltpu.SemaphoreType.DMA((3,)), pltpu.VMEM((tm, D_MODEL), F32)],
        compiler_params=_params(("arbitrary", "arbitrary")),
    )(h0, ms, mp, w_out_b, g2, wg_b, wu_b, wd_b, gf, target)


def _bwd_ffn(dh2, dh2b, a, b, wg_b, wu_b, wd_b, h1, g2, tm):
    n_pad = dh2.shape[0]

    def body(dh2_ref, dh2b_ref, a_ref, b_ref, wg_hbm, wu_hbm, wd_hbm, h1_ref, g2_ref, da_ref, db_ref, dh1_ref, dg2_ref,
             wg_ref, wu_ref, wd_ref, sems, acc):
        @pl.when(pl.program_id(0) == 0)
        def _():
            dg2_ref[...] = jnp.zeros_like(dg2_ref)
            cps = [pltpu.make_async_copy(src, dst, sems.at[k])
                   for k, (src, dst) in enumerate(((wg_hbm, wg_ref), (wu_hbm, wu_ref), (wd_hbm, wd_ref)))]
            for cp in cps:
                cp.start()
            for cp in cps:
                cp.wait()

        dh2b = dh2b_ref[...]
        for q in range(N_SHARD):
            dff = _dot_nt(dh2b, wd_ref[q])
            a_v, b_v = a_ref[q].astype(F32), b_ref[q].astype(F32)
            sig = _sigmoid(a_v)
            silu = a_v * sig
            da = (dff * b_v * (sig + silu * (1.0 - sig))).astype(BF16)
            db = (dff * silu).astype(BF16)
            da_ref[q] = da
            db_ref[q] = db
            part = _dot(da, wg_ref[q]) + _dot(db, wu_ref[q])
            if q == 0:
                acc[...] = part
            else:
                acc[...] += part

        h1 = h1_ref[...]
        r = _rms(h1)
        xhat = h1 * r
        dn2 = acc[...]
        dg2_ref[...] += jnp.sum(dn2 * xhat, axis=0, keepdims=True)
        dh1_ref[...] = dh2_ref[...] + _rms_bwd(dn2 * g2_ref[...], xhat, r)

    row = lambda w: pl.BlockSpec((tm, w), lambda i: (i, 0))
    act = pl.BlockSpec((N_SHARD, tm, FF_SHARD), lambda i: (0, i, 0))
    sds = jax.ShapeDtypeStruct
    return pl.pallas_call(
        body, grid=(n_pad // tm,), name="bwd_ffn",
        in_specs=[row(D_MODEL), row(D_MODEL), act, act, _ANY, _ANY, _ANY, row(D_MODEL), _full((1, D_MODEL))],
        out_specs=[act, act, row(D_MODEL), _full((1, D_MODEL))],
        out_shape=[sds((N_SHARD, n_pad, FF_SHARD), BF16), sds((N_SHARD, n_pad, FF_SHARD), BF16),
                   sds((n_pad, D_MODEL), F32), sds((1, D_MODEL), F32)],
        scratch_shapes=[pltpu.VMEM(wg_b.shape, BF16), pltpu.VMEM(wu_b.shape, BF16), pltpu.VMEM(wd_b.shape, BF16),
                        pltpu.SemaphoreType.DMA((3,)), pltpu.VMEM((tm, D_MODEL), F32)],
        compiler_params=_params(("arbitrary",)),
    )(dh2, dh2b, a, b, wg_b, wu_b, wd_b, h1, g2)


def _grad_ffn(n2, da, db, ff, dh2b, tm):
    n_pad = n2.shape[0]

    def body(n2_ref, da_ref, db_ref, ff_ref, dh2_ref, dwg_ref, dwu_ref, dwd_ref):
        i = pl.program_id(1)
        n2_v = n2_ref[...]
        gg = _dot_tn(da_ref[0], n2_v)
        gu = _dot_tn(db_ref[0], n2_v)
        gd = _dot_tn(ff_ref[0], dh2_ref[...])

        @pl.when(i == 0)
        def _():
            dwg_ref[0] = gg
            dwu_ref[0] = gu
            dwd_ref[0] = gd

        @pl.when(i > 0)
        def _():
            dwg_ref[0] += gg
            dwu_ref[0] += gu
            dwd_ref[0] += gd

    row = lambda w: pl.BlockSpec((tm, w), lambda q, i: (i, 0))
    act = pl.BlockSpec((1, tm, FF_SHARD), lambda q, i: (q, i, 0))
    sds = jax.ShapeDtypeStruct
    return pl.pallas_call(
        body, grid=(N_SHARD, n_pad // tm), name="grad_ffn",
        in_specs=[row(D_MODEL), act, act, act, row(D_MODEL)],
        out_specs=[pl.BlockSpec((1, FF_SHARD, D_MODEL), lambda q, i: (q, 0, 0))] * 3,
        out_shape=[sds((N_SHARD, FF_SHARD, D_MODEL), F32)] * 3,
        compiler_params=_params(("parallel", "arbitrary")),
    )(n2, da, db, ff, dh2b)


def _bwd_out(dh1, ms, mp, w_out_b, tm, token):
    n_pad = dh1.shape[0]

    def body(dh1_ref, ms_ref, mp_ref, wo_ref, token_ref, dms_ref, dmp_ref, dwo_ref):
        i = pl.program_id(0)

        @pl.when(i == 0)
        def _():
            dwo_ref[...] = jnp.zeros_like(dwo_ref)

        d = dh1_ref[...].astype(BF16)
        dms = _dot_nt(d, wo_ref[:D_SSM, :])
        for k in range(4):
            dms_ref[k] = dms[:, 128 * k:128 * (k + 1)]
        dmp_ref[...] = _dot_nt(d, wo_ref[D_SSM:, :])
        dwo_ref[:D_SSM, :] += _dot_tn(ms_ref[...], d)
        dwo_ref[D_SSM:, :] += _dot_tn(mp_ref[...], d)

    row = lambda w: pl.BlockSpec((tm, w), lambda i: (i, 0))
    sds = jax.ShapeDtypeStruct
    return pl.pallas_call(
        body, grid=(n_pad // tm,), name="bwd_out",
        in_specs=[row(D_MODEL), row(D_SSM), row(D_POOL), _full((D_MODEL, D_MODEL)), _ANY],
        out_specs=[pl.BlockSpec((4, tm, 128), lambda i: (0, i, 0)), row(D_POOL), _full((D_MODEL, D_MODEL))],
        out_shape=[sds((4, n_pad, 128), F32), sds((n_pad, D_POOL), F32), sds((D_MODEL, D_MODEL), F32)],
        compiler_params=_params(("arbitrary",)),
    )(dh1, ms, mp, w_out_b, token)


def _bwd_in(du, dv, h0, dh1, g1, w_in_b, tm):
    n_pad = h0.shape[0]

    def body(du_ref, dv_ref, h0_ref, dh1_ref, g1_ref, w_ref, dh0_ref, dwi_ref, dg1_ref):
        i = pl.program_id(0)

        @pl.when(i == 0)
        def _():
            dwi_ref[...] = jnp.zeros_like(dwi_ref)
            dg1_ref[...] = jnp.zeros_like(dg1_ref)

        dub = du_ref[...].astype(BF16)
        dvb = dv_ref[...].astype(BF16)
        dn1 = _dot_nt(dub, w_ref[:, :D_SSM]) + _dot_nt(dvb, w_ref[:, D_SSM:])
        h = h0_ref[...]
        r = _rms(h)
        xhat = h * r
        g_row = g1_ref[...]
        n1 = (xhat * g_row).astype(BF16)
        dwi_ref[:, :D_SSM] += _dot_tn(n1, dub)
        dwi_ref[:, D_SSM:] += _dot_tn(n1, dvb)
        dg1_ref[...] += jnp.sum(dn1 * xhat, axis=0, keepdims=True)
        dh0_ref[...] = dh1_ref[...] + _rms_bwd(dn1 * g_row, xhat, r)

    row = lambda w: pl.BlockSpec((tm, w), lambda i: (i, 0))
    sds = jax.ShapeDtypeStruct
    return pl.pallas_call(
        body, grid=(n_pad // tm,), name="bwd_in",
        in_specs=[row(D_SSM), row(D_POOL), row(D_MODEL), row(D_MODEL), _full((1, D_MODEL)), _full((D_MODEL, D_MODEL))],
        out_specs=[row(D_MODEL), _full((D_MODEL, D_MODEL)), _full((1, D_MODEL))],
        out_shape=[sds((n_pad, D_MODEL), F32), sds((D_MODEL, D_MODEL), F32), sds((1, D_MODEL), F32)],
        compiler_params=_params(("arbitrary",)),
    )(du, dv, h0, dh1, g1, w_in_b)


SEGMENTS = 8


def _interleaved(ref, seg):
    return jnp.concatenate(
        [jnp.concatenate([ref[i, pl.ds(j, SEGMENTS, stride=seg), :] for i in range(4)], axis=1) for j in range(seg)],
        axis=0)


def _time_order(scratch, val, seg):
    for i in range(4):
        scratch[i] = val[:, 128 * i:128 * (i + 1)]
    tiles = []
    for m in range(val.shape[0] // 8):
        s, j0 = divmod(8 * m, seg)
        tiles.append(jnp.concatenate(
            [scratch[i, pl.ds(8 * j0 + s, 8, stride=SEGMENTS), :] for i in range(4)], axis=1))
    return jnp.concatenate(tiles, axis=0)


def _power_table(lam_ref, pw_r, pw_i, seg):
    a_r = jnp.broadcast_to(lam_ref[0:1, :], (SEGMENTS, N_STATE))
    a_i = jnp.broadcast_to(lam_ref[1:2, :], (SEGMENTS, N_STATE))
    p_r, p_i = a_r, a_i
    for k in range(seg):
        pw_r[SEGMENTS * k:SEGMENTS * (k + 1), :] = p_r
        pw_i[SEGMENTS * k:SEGMENTS * (k + 1), :] = p_i
        p_r, p_i = p_r * a_r - p_i * a_i, p_r * a_i + p_i * a_r


def _segment_scan(xr_ref, xi_ref, cols, pw_r, pw_i, hr_s, hi_s, seg, reverse):
    sign = -1.0 if reverse else 1.0
    a_r, a_i = pw_r[0:SEGMENTS, cols], sign * pw_i[0:SEGMENTS, cols]

    def step(n, carry):
        hr, hi = carry
        o = pl.multiple_of((seg - 1 - n if reverse else n) * SEGMENTS, SEGMENTS)
        nr = a_r * hr - a_i * hi + xr_ref[pl.ds(o, SEGMENTS), cols]
        ni = a_r * hi + a_i * hr + xi_ref[pl.ds(o, SEGMENTS), cols]
        xr_ref[pl.ds(o, SEGMENTS), cols] = nr
        xi_ref[pl.ds(o, SEGMENTS), cols] = ni
        return nr, ni

    zero = jnp.zeros((SEGMENTS, cols.stop - cols.start), F32)
    e_r, e_i = lax.fori_loop(0, seg, step, (zero, zero), unroll=2)

    top = SEGMENTS * (seg - 1)
    ls_r, ls_i = pw_r[top:top + 1, cols], sign * pw_i[top:top + 1, cols]
    c_r, c_i = hr_s[0:1, cols], hi_s[0:1, cols]
    in_r, in_i = [None] * SEGMENTS, [None] * SEGMENTS
    for s in (range(SEGMENTS - 1, -1, -1) if reverse else range(SEGMENTS)):
        in_r[s], in_i[s] = c_r, c_i
        c_r, c_i = (e_r[s:s + 1, :] + ls_r * c_r - ls_i * c_i, e_i[s:s + 1, :] + ls_r * c_i + ls_i * c_r)
    hr_s[0:1, cols] = c_r
    hi_s[0:1, cols] = c_i
    cm_r, cm_i = jnp.concatenate(in_r, axis=0), jnp.concatenate(in_i, axis=0)

    def fix(jj, _):
        o = pl.multiple_of(jj * SEGMENTS, SEGMENTS)
        k = pl.multiple_of((seg - 1 - jj if reverse else jj) * SEGMENTS, SEGMENTS)
        p_r, p_i = pw_r[pl.ds(k, SEGMENTS), cols], sign * pw_i[pl.ds(k, SEGMENTS), cols]
        xr_ref[pl.ds(o, SEGMENTS), cols] += p_r * cm_r - p_i * cm_i
        xi_ref[pl.ds(o, SEGMENTS), cols] += p_r * cm_i + p_i * cm_r
        return 0

    lax.fori_loop(0, seg, fix, 0, unroll=4)


def _s5_tail(y, glu_ref, glub):
    g, dgelu = _gelu_and_grad(y)
    gb = g.astype(BF16)
    gate = jnp.concatenate([_dot(gb[:, 128 * j:128 * (j + 1)], glu_ref[j]) for j in range(SUPER)], axis=1) + glub
    sig = _sigmoid(gate)
    return g, gb, dgelu, sig, g * sig


def _s5_fwd(u4, lam, bbr, bbi, crt, cit, vecs, glu, tc):
    n_pad = u4.shape[1]
    seg = tc // SEGMENTS

    def body(u_ref, lam_ref, bbr_ref, bbi_ref, crt_ref, cit_ref, vec_ref, glu_ref,
             sr_ref, si_ref, y_ref, ms_ref, hr_s, hi_s, pw_r, pw_i, lanes):
        @pl.when(pl.program_id(0) == 0)
        def _():
            hr_s[...] = jnp.zeros_like(hr_s)
            hi_s[...] = jnp.zeros_like(hi_s)
            _power_table(lam_ref, pw_r, pw_i, seg)

        u_v = _interleaved(u_ref, seg)
        ub = u_v.astype(BF16)
        for j in range(SUPER):
            uj = ub[:, 128 * j:128 * (j + 1)]
            sr_ref[:, 512 * j:512 * (j + 1)] = _dot(uj, bbr_ref[j])
            si_ref[:, 512 * j:512 * (j + 1)] = _dot(uj, bbi_ref[j])
        for j in range(SUPER):
            _segment_scan(sr_ref, si_ref, slice(512 * j, 512 * (j + 1)), pw_r, pw_i, hr_s, hi_s, seg, False)

        d_row, glub, gs = vec_ref[0:1, :], vec_ref[1:2, :], vec_ref[2:3, :]
        ys_c = []
        for j in range(SUPER):
            sr_j = sr_ref[:, 512 * j:512 * (j + 1)].astype(BF16)
            si_j = si_ref[:, 512 * j:512 * (j + 1)].astype(BF16)
            ys_c.append(_dot(sr_j, crt_ref[j]) - _dot(si_j, cit_ref[j]))
        y = jnp.concatenate(ys_c, axis=1) + d_row * u_v
        y_ref[...] = y
        _, _, _, _, ys = _s5_tail(y, glu_ref, glub)
        ms_ref[...] = _time_order(lanes, ys * _rms(ys) * gs, seg).astype(BF16)

    chunk = lambda w: pl.BlockSpec((tc, w), lambda c: (c, 0))
    lane_blocks = pl.BlockSpec((4, tc, 128), lambda c: (0, c, 0))
    sds = jax.ShapeDtypeStruct
    return pl.pallas_call(
        body, grid=(n_pad // tc,), name="s5_fwd",
        in_specs=[lane_blocks, _full((8, N_STATE)), _full((SUPER, 128, 512)), _full((SUPER, 128, 512)),
                  _full((SUPER, 512, 128)), _full((SUPER, 512, 128)), _full((8, D_SSM)), _full((SUPER, 128, 128))],
        out_specs=[chunk(N_STATE), chunk(N_STATE), chunk(D_SSM), chunk(D_SSM)],
        out_shape=[sds((n_pad, N_STATE), F32), sds((n_pad, N_STATE), F32),
                   sds((n_pad, D_SSM), F32), sds((n_pad, D_SSM), BF16)],
        scratch_shapes=[pltpu.VMEM((8, N_STATE), F32), pltpu.VMEM((8, N_STATE), F32),
                        pltpu.VMEM((tc, N_STATE), F32), pltpu.VMEM((tc, N_STATE), F32),
                        pltpu.VMEM((4, tc, 128), F32)],
        compiler_params=_params(("arbitrary",)),
    )(u4, lam, bbr, bbi, crt, cit, vecs, glu)


def _s5_bwd(dms4, y, u4, sr, si, lam, bbr, bbi, crt, cit, vecs, glu, tc, token):
    n_pad = u4.shape[1]
    nc = n_pad // tc
    seg = tc // SEGMENTS

    def body(dms_ref, y_ref, u_ref, sr_ref, si_ref, pr_ref, pi_ref, lam_ref, bbr_ref, bbi_ref, crt_ref, cit_ref,
             vec_ref, glu_ref, token_ref, du_ref, dbbr_ref, dbbi_ref, dcrt_ref, dcit_ref, dglu_ref, dvec_ref, dlam_ref,
             qr_s, qi_s, cr_s, ci_s, pw_r, pw_i, lanes):
        c = pl.program_id(0)

        @pl.when(c == 0)
        def _():
            for ref in (dbbr_ref, dbbi_ref, dcrt_ref, dcit_ref, dglu_ref, dvec_ref, dlam_ref, cr_s, ci_s):
                ref[...] = jnp.zeros_like(ref)
            _power_table(lam_ref, pw_r, pw_i, seg)

        d_row, glub, gs = vec_ref[0:1, :], vec_ref[1:2, :], vec_ref[2:3, :]
        y_v, u_v = y_ref[...], _interleaved(u_ref, seg)
        ub = u_v.astype(BF16)
        g, gb, dgelu, sig, ys = _s5_tail(y_v, glu_ref, glub)
        r = _rms(ys)
        xhat = ys * r
        dm = _interleaved(dms_ref, seg)
        dys = _rms_bwd(dm * gs, xhat, r)
        dgate = dys * g * sig * (1.0 - sig)
        dgateb = dgate.astype(BF16)
        dg = dys * sig + jnp.concatenate(
            [_dot_nt(dgateb[:, 128 * j:128 * (j + 1)], glu_ref[j]) for j in range(SUPER)], axis=1)
        dy = dg * dgelu
        dyb = dy.astype(BF16)
        dvec_ref[0:1, :] += jnp.sum(dy * u_v, axis=0, keepdims=True)
        dvec_ref[1:2, :] += jnp.sum(dgate, axis=0, keepdims=True)
        dvec_ref[2:3, :] += jnp.sum(dm * xhat, axis=0, keepdims=True)

        for j in range(SUPER):
            cols, states = slice(128 * j, 128 * (j + 1)), slice(512 * j, 512 * (j + 1))
            dglu_ref[j] += _dot_tn(gb[:, cols], dgateb[:, cols])
            dcrt_ref[j] += _dot_tn(sr_ref[:, states].astype(BF16), dyb[:, cols])
            dcit_ref[j] -= _dot_tn(si_ref[:, states].astype(BF16), dyb[:, cols])
            qr_s[:, states] = _dot_nt(dyb[:, cols], crt_ref[j])
            qi_s[:, states] = -_dot_nt(dyb[:, cols], cit_ref[j])

        first = c == nc - 1
        row0 = lax.broadcasted_iota(jnp.int32, (SEGMENTS, 1), 0) == 0
        last = (seg - 1) * SEGMENTS
        for j in range(SUPER):
            states = slice(512 * j, 512 * (j + 1))
            _segment_scan(qr_s, qi_s, states, pw_r, pw_i, cr_s, ci_s, seg, True)

            before_r = jnp.where(first, 0.0, pltpu.roll(pr_ref[:, states], 1, 0))
            before_i = jnp.where(first, 0.0, pltpu.roll(pi_ref[:, states], 1, 0))
            hp_r = jnp.where(row0, before_r, pltpu.roll(sr_ref[pl.ds(last, SEGMENTS), states], 1, 0))
            hp_i = jnp.where(row0, before_i, pltpu.roll(si_ref[pl.ds(last, SEGMENTS), states], 1, 0))
            q_r, q_i = qr_s[pl.ds(0, SEGMENTS), states], qi_s[pl.ds(0, SEGMENTS), states]

            def dlam_step(jj, acc):
                o = pl.multiple_of(jj * SEGMENTS, SEGMENTS)
                above = pl.multiple_of((jj - 1) * SEGMENTS, SEGMENTS)
                h_r, h_i = sr_ref[pl.ds(above, SEGMENTS), states], si_ref[pl.ds(above, SEGMENTS), states]
                t_r, t_i = qr_s[pl.ds(o, SEGMENTS), states], qi_s[pl.ds(o, SEGMENTS), states]
                return acc[0] + t_r * h_r + t_i * h_i, acc[1] + t_i * h_r - t_r * h_i

            acc = lax.fori_loop(1, seg, dlam_step, (q_r * hp_r + q_i * hp_i, q_i * hp_r - q_r * hp_i), unroll=3)
            dlam_ref[0:SEGMENTS, states] += acc[0]
            dlam_ref[SEGMENTS:, states] += acc[1]

        du_c = []
        for j in range(SUPER):
            cols, states = slice(128 * j, 128 * (j + 1)), slice(512 * j, 512 * (j + 1))
            qr_j = qr_s[:, states].astype(BF16)
            qi_j = qi_s[:, states].astype(BF16)
            du_c.append(_dot_nt(qr_j, bbr_ref[j]) + _dot_nt(qi_j, bbi_ref[j]))
            dbbr_ref[j] += _dot_tn(ub[:, cols], qr_j)
            dbbi_ref[j] += _dot_tn(ub[:, cols], qi_j)
        du_ref[...] = _time_order(lanes, jnp.concatenate(du_c, axis=1) + dy * d_row, seg)

    rev = lambda c: nc - 1 - c
    chunk = lambda w: pl.BlockSpec((tc, w), lambda c: (rev(c), 0))
    lane_blocks = pl.BlockSpec((4, tc, 128), lambda c: (0, rev(c), 0))
    prev = pl.BlockSpec((SEGMENTS, N_STATE), lambda c: (jnp.maximum(rev(c) * seg - 1, 0), 0))
    sds = jax.ShapeDtypeStruct
    return pl.pallas_call(
        body, grid=(nc,), name="s5_bwd",
        in_specs=[lane_blocks, chunk(D_SSM), lane_blocks, chunk(N_STATE), chunk(N_STATE), prev, prev,
                  _full((8, N_STATE)), _full((SUPER, 128, 512)), _full((SUPER, 128, 512)),
                  _full((SUPER, 512, 128)), _full((SUPER, 512, 128)), _full((8, D_SSM)), _full((SUPER, 128, 128)), _ANY],
        out_specs=[chunk(D_SSM), _full((SUPER, 128, 512)), _full((SUPER, 128, 512)), _full((SUPER, 512, 128)),
                   _full((SUPER, 512, 128)), _full((SUPER, 128, 128)), _full((8, D_SSM)), _full((2 * SEGMENTS, N_STATE))],
        out_shape=[sds((n_pad, D_SSM), F32), sds((SUPER, 128, 512), F32), sds((SUPER, 128, 512), F32),
                   sds((SUPER, 512, 128), F32), sds((SUPER, 512, 128), F32), sds((SUPER, 128, 128), F32),
                   sds((8, D_SSM), F32), sds((2 * SEGMENTS, N_STATE), F32)],
        scratch_shapes=[pltpu.VMEM((tc, N_STATE), F32), pltpu.VMEM((tc, N_STATE), F32),
                        pltpu.VMEM((8, N_STATE), F32), pltpu.VMEM((8, N_STATE), F32),
                        pltpu.VMEM((tc, N_STATE), F32), pltpu.VMEM((tc, N_STATE), F32),
                        pltpu.VMEM((4, tc, 128), F32)],
        compiler_params=_params(("arbitrary",)),
    )(dms4, y, u4, sr, si, sr, si, lam, bbr, bbi, crt, cit, vecs, glu, token)


def _inv_count(c_idx, tc, w):
    t = c_idx * tc + lax.broadcasted_iota(jnp.int32, (tc, 1), 0)
    return 1.0 / jnp.minimum(t + 1, w).astype(F32)


def _pool_fwd(v, pw_b, vecs, tc, token):
    n_pad = v.shape[0]

    def body(v_ref, pw_ref, vec_ref, token_ref, feat_ref, mp_ref, hist):
        c = pl.program_id(0)

        @pl.when(c == 0)
        def _():
            hist[...] = jnp.zeros_like(hist)

        v_v = v_ref[...]
        ext = jnp.concatenate([hist[...], v_v], axis=0)
        hist[...] = v_v[tc - POOL_HALO:, :]
        feats, ps = [], []
        for k, w in enumerate(POOL_WINDOWS):
            cols = slice(128 * k, 128 * (k + 1))
            s = ext[:, cols]
            sh = 1
            while sh < w:
                s = s + pltpu.roll(s, sh, 0)
                sh *= 2
            f = (s[POOL_HALO:, :] * _inv_count(c, tc, w) - v_v[:, cols]).astype(BF16)
            feats.append(f)
            ps.append(_dot(f, pw_ref[k]))
        feat_ref[...] = jnp.concatenate(feats, axis=1)
        yp = jnp.concatenate(ps, axis=1) * vec_ref[0:1, :]
        mp_ref[...] = (yp * _rms(yp) * vec_ref[1:2, :]).astype(BF16)

    chunk = lambda w: pl.BlockSpec((tc, w), lambda c: (c, 0))
    sds = jax.ShapeDtypeStruct
    return pl.pallas_call(
        body, grid=(n_pad // tc,), name="pool_fwd",
        in_specs=[chunk(D_POOL), _full((4, 128, 128)), _full((8, D_POOL)), _ANY],
        out_specs=[chunk(D_POOL), chunk(D_POOL)],
        out_shape=[sds((n_pad, D_POOL), BF16), sds((n_pad, D_POOL), BF16)],
        scratch_shapes=[pltpu.VMEM((POOL_HALO, D_POOL), F32)],
        compiler_params=_params(("arbitrary",)),
    )(v, pw_b, vecs, token)


def _pool_bwd(dmp, feat, pw_b, vecs, tc):
    n_pad = dmp.shape[0]
    nc = n_pad // tc

    def body(dmp_ref, feat_ref, pw_ref, vec_ref, dv_ref, dpw_ref, dvec_ref, fut):
        c = pl.program_id(0)

        @pl.when(c == 0)
        def _():
            fut[...] = jnp.zeros_like(fut)
            dpw_ref[...] = jnp.zeros_like(dpw_ref)
            dvec_ref[...] = jnp.zeros_like(dvec_ref)

        scale, gp = vec_ref[0:1, :], vec_ref[1:2, :]
        feat_v = feat_ref[...]
        p = jnp.concatenate([_dot(feat_v[:, 128 * k:128 * (k + 1)], pw_ref[k]) for k in range(4)], axis=1)
        yp = p * scale
        r = _rms(yp)
        xhat = yp * r
        dm = dmp_ref[...]
        dyp = _rms_bwd(dm * gp, xhat, r)
        dvec_ref[0:1, :] += jnp.sum(dyp * p, axis=0, keepdims=True)
        dvec_ref[1:2, :] += jnp.sum(dm * xhat, axis=0, keepdims=True)
        dpb = (dyp * scale).astype(BF16)
        es, dfs = [], []
        for k, w in enumerate(POOL_WINDOWS):
            cols = slice(128 * k, 128 * (k + 1))
            dpw_ref[k] += _dot_tn(feat_v[:, cols], dpb[:, cols])
            df = _dot_nt(dpb[:, cols], pw_ref[k])
            dfs.append(df)
            es.append(df * _inv_count(nc - 1 - c, tc, w))
        e = jnp.concatenate(es, axis=1)
        ext = jnp.concatenate([e, fut[...]], axis=0)
        fut[...] = e[:POOL_HALO, :]
        n_ext = tc + POOL_HALO
        dvs = []
        for k, w in enumerate(POOL_WINDOWS):
            s = ext[:, 128 * k:128 * (k + 1)]
            sh = 1
            while sh < w:
                s = s + pltpu.roll(s, n_ext - sh, 0)
                sh *= 2
            dvs.append(s[:tc, :] - dfs[k])
        dv_ref[...] = jnp.concatenate(dvs, axis=1)

    chunk = lambda w: pl.BlockSpec((tc, w), lambda c: (nc - 1 - c, 0))
    sds = jax.ShapeDtypeStruct
    return pl.pallas_call(
        body, grid=(nc,), name="pool_bwd",
        in_specs=[chunk(D_POOL), chunk(D_POOL), _full((4, 128, 128)), _full((8, D_POOL))],
        out_specs=[chunk(D_POOL), _full((4, 128, 128)), _full((8, D_POOL))],
        out_shape=[sds((n_pad, D_POOL), F32), sds((4, 128, 128), F32), sds((8, D_POOL), F32)],
        scratch_shapes=[pltpu.VMEM((POOL_HALO, D_POOL), F32)],
        compiler_params=_params(("arbitrary",)),
    )(dmp, feat, pw_b, vecs)


def _place():
    x, y, c = lax.axis_index("x"), lax.axis_index("y"), lax.axis_index("c")
    chips = [(1 - x, y), (x, 1 - y), (1 - x, 1 - y)]
    return x, y, c, chips


_ANY = pl.BlockSpec(memory_space=pl.ANY)


def _cast_shards(shards, dtypes, place):
    n = len(shards)

    def body(place_ref, *refs):
        for i in range(n):
            refs[n + i][0] = refs[i][...].astype(dtypes[i])

    return pl.pallas_call(
        body, name="cast_shards",
        grid_spec=pltpu.PrefetchScalarGridSpec(
            num_scalar_prefetch=1, grid=(1,),
            in_specs=[pl.BlockSpec(s.shape, lambda i, p: (0, 0, 0)) for s in shards],
            out_specs=[pl.BlockSpec((1,) + s.shape, lambda i, p: (p[0], 0, 0, 0)) for s in shards]),
        out_shape=[jax.ShapeDtypeStruct((N_SHARD,) + s.shape, dt) for s, dt in zip(shards, dtypes)],
        compiler_params=_params(("arbitrary",)),
    )(place, *shards)


def _gather_shards(full):
    n = len(full)

    def body(*refs):
        outs = refs[n:2 * n]
        ici_send, ici_recv, d2d_send, d2d_recv = refs[2 * n:]
        x, y, c, chips = _place()
        q = 2 * x + y
        sibling = (x, y, 1 - c)

        def ici(i, j, shard, to):
            return pltpu.make_async_remote_copy(src_ref=outs[i].at[q, c], dst_ref=outs[i].at[shard, c],
                                                send_sem=ici_send.at[i, j], recv_sem=ici_recv.at[i, j],
                                                device_id=to, device_id_type=MESH)

        def d2d(i, j, shard, half):
            return pltpu.make_async_remote_copy(src_ref=outs[i].at[shard, c], dst_ref=outs[i].at[shard, half],
                                                send_sem=d2d_send.at[i, j], recv_sem=d2d_recv.at[i, j],
                                                device_id=sibling, device_id_type=MESH)

        sends = [ici(i, j, q, (*chip, c)) for i in range(n) for j, chip in enumerate(chips)]
        for cp in sends:
            cp.start()
        passed = []
        for i in range(n):
            for j, (cx, cy) in enumerate(chips):
                ici(i, j, 2 * cx + cy, (cx, cy, c)).wait_recv()
                cp = d2d(i, j, 2 * cx + cy, c)
                cp.start()
                passed.append(cp)
        for i in range(n):
            for j, (cx, cy) in enumerate(chips):
                d2d(i, j, 2 * cx + cy, 1 - c).wait_recv()
        for cp in sends + passed:
            cp.wait_send()

    return pl.pallas_call(
        body, name="gather_shards",
        in_specs=[_ANY] * n, out_specs=[_ANY] * n,
        out_shape=[jax.ShapeDtypeStruct(f.shape, f.dtype) for f in full],
        input_output_aliases={i: i for i in range(n)},
        scratch_shapes=[pltpu.SemaphoreType.DMA((n, 3)), pltpu.SemaphoreType.DMA((n, 3)),
                        pltpu.SemaphoreType.DMA((n, 3)), pltpu.SemaphoreType.DMA((n, 3))],
    )(*full)


_HBM = pl.BlockSpec(memory_space=pltpu.HBM)
_SEM = pl.BlockSpec(memory_space=pltpu.SEMAPHORE)
_EFFECT = pltpu.SideEffectType.DATAFLOW_SIDE_EFFECTING


def _copies_start(name, arrays, sem_shape, build, after=None):
    n = len(arrays)
    extra = [] if after is None else [after]

    def body(*refs):
        outs = refs[n + len(extra):2 * n + len(extra)]
        send, recv, token = refs[2 * n + len(extra):]
        sends, _ = build(outs, send, recv)
        for cp in sends:
            cp.start()
        token[...] = jnp.zeros_like(token)

    out = pl.pallas_call(
        body, name=name, in_specs=[_HBM] * n + [_ANY] * len(extra),
        out_specs=[_HBM] * n + [_SEM, _SEM, pl.BlockSpec(memory_space=pltpu.VMEM)],
        out_shape=[pltpu.HBM(a.shape, a.dtype) for a in arrays]
        + [pltpu.SemaphoreType.DMA(sem_shape), pltpu.SemaphoreType.DMA(sem_shape), jax.ShapeDtypeStruct((8, 128), F32)],
        input_output_aliases={i: i for i in range(n)},
        compiler_params=pltpu.CompilerParams(has_side_effects=_EFFECT),
    )(*[pltpu.with_memory_space_constraint(a, pltpu.HBM) for a in arrays], *extra)
    return list(out[:n]), (out[n], out[n + 1]), out[n + 2]


def _copies_wait(name, arrays, sems, after, build):
    n = len(arrays)

    def body(*refs):
        ins = refs[:n]
        send, recv = refs[n], refs[n + 1]
        sends, recvs = build(ins, send, recv)
        for cp in sends:
            cp.wait_send()
        for cp in recvs:
            cp.wait_recv()

    return list(pl.pallas_call(
        body, name=name, in_specs=[_HBM] * n + [_SEM, _SEM] + [_ANY] * len(after), out_specs=[_HBM] * n,
        out_shape=[pltpu.HBM(a.shape, a.dtype) for a in arrays],
        input_output_aliases={i: i for i in range(n)},
        compiler_params=pltpu.CompilerParams(has_side_effects=_EFFECT),
    )(*arrays, *sems, *after))


def _remote(src, dst, send_sem, recv_sem, to):
    return pltpu.make_async_remote_copy(src_ref=src, dst_ref=dst, send_sem=send_sem, recv_sem=recv_sem,
                                        device_id=to, device_id_type=MESH)


def _build_gather(refs, send, recv):
    x, y, c, chips = _place()
    q = 2 * x + y
    pairs = [(i, j, chip) for i in range(len(refs)) for j, chip in enumerate(chips)]
    sends = [_remote(refs[i].at[q, c], refs[i].at[q, c], send.at[3 * i + j], recv.at[3 * i + j], (cx, cy, c))
             for i, j, (cx, cy) in pairs]
    recvs = [_remote(refs[i].at[q, c], refs[i].at[2 * cx + cy, c], send.at[3 * i + j], recv.at[3 * i + j], (cx, cy, c))
             for i, j, (cx, cy) in pairs]
    return sends, recvs


def _build_forward(refs, send, recv):
    x, y, c, chips = _place()
    pairs = [(i, j, 2 * cx + cy) for i in range(len(refs)) for j, (cx, cy) in enumerate(chips)]
    sends = [_remote(refs[i].at[s, c], refs[i].at[s, c], send.at[3 * i + j], recv.at[3 * i + j], (x, y, 1 - c))
             for i, j, s in pairs]
    recvs = [_remote(refs[i].at[s, c], refs[i].at[s, 1 - c], send.at[3 * i + j], recv.at[3 * i + j], (x, y, 1 - c))
             for i, j, s in pairs]
    return sends, recvs


def _build_swap(refs, send, recv):
    x, y, c, _ = _place()
    n = len(refs) // 2
    cps = [_remote(refs[i].at[:, 1 - c], refs[n + i], send.at[i], recv.at[i], (x, y, 1 - c)) for i in range(n)]
    return cps, cps


def _build_exchange(refs, send, recv):
    x, y, c, chips = _place()
    n = len(refs) // 2
    cps = [_remote(refs[i].at[2 * cx + cy], refs[n + i].at[j], send.at[3 * i + j], recv.at[3 * i + j], (cx, cy, c))
           for i in range(n) for j, (cx, cy) in enumerate(chips)]
    return cps, cps


def _build_spread(refs, send, recv):
    x, y, c, _ = _place()
    flip = lambda bit, on: bit + on - 2 * bit * on
    cps = [_remote(refs[0], refs[1].at[r - 1], send.at[r - 1], recv.at[r - 1],
                   (flip(x, r >> 2 & 1), flip(y, r >> 1 & 1), flip(c, r & 1))) for r in (1, 2, 4, 6)]
    return cps, cps


def _forward_small(landed):
    def body(in_ref, out_ref, send, recv):
        x, y, c, _ = _place()
        cps = [_remote(out_ref.at[r - 1], out_ref.at[r], send.at[k], recv.at[k], (x, y, 1 - c))
               for k, r in enumerate((2, 4, 6))]
        for cp in cps:
            cp.start()
        for cp in cps:
            cp.wait()

    return pl.pallas_call(
        body, name="forward_small",
        in_specs=[_ANY], out_specs=_ANY, out_shape=jax.ShapeDtypeStruct(landed.shape, F32),
        input_output_aliases={0: 0},
        scratch_shapes=[pltpu.SemaphoreType.DMA((3,)), pltpu.SemaphoreType.DMA((3,))],
    )(landed)


def _swap_halves(grads):
    n = len(grads)

    def body(*refs):
        ins, outs = refs[:n], refs[n:2 * n]
        send, recv = refs[2 * n:]
        x, y, c, _ = _place()
        cps = [pltpu.make_async_remote_copy(src_ref=ins[i].at[:, 1 - c], dst_ref=outs[i], send_sem=send.at[i],
                                            recv_sem=recv.at[i], device_id=(x, y, 1 - c), device_id_type=MESH)
               for i in range(n)]
        for cp in cps:
            cp.start()
        for cp in cps:
            cp.wait()

    return pl.pallas_call(
        body, name="swap_halves",
        in_specs=[_ANY] * n, out_specs=[_ANY] * n,
        out_shape=[jax.ShapeDtypeStruct((N_SHARD,) + g.shape[2:], F32) for g in grads],
        scratch_shapes=[pltpu.SemaphoreType.DMA((n,)), pltpu.SemaphoreType.DMA((n,))],
    )(*grads)


def _join_halves(pairs):
    n = len(pairs)

    def body(*refs):
        outs = refs[n:2 * n]
        send, recv = refs[2 * n:]
        x, y, c, _ = _place()
        cps = [pltpu.make_async_remote_copy(src_ref=outs[i].at[c], dst_ref=outs[i].at[c], send_sem=send.at[i],
                                            recv_sem=recv.at[i], device_id=(x, y, 1 - c), device_id_type=MESH)
               for i in range(n)]
        for cp in cps:
            cp.start()
        for i in range(n):
            cps[i].wait_send()
            pltpu.make_async_remote_copy(src_ref=outs[i].at[c], dst_ref=outs[i].at[1 - c], send_sem=send.at[i],
                                         recv_sem=recv.at[i], device_id=(x, y, 1 - c), device_id_type=MESH).wait_recv()

    return pl.pallas_call(
        body, name="join_halves",
        in_specs=[_ANY] * n, out_specs=[_ANY] * n,
        out_shape=[jax.ShapeDtypeStruct(p.shape, F32) for p in pairs],
        input_output_aliases={i: i for i in range(n)},
        scratch_shapes=[pltpu.SemaphoreType.DMA((n,)), pltpu.SemaphoreType.DMA((n,))],
    )(*pairs)


N_SPLIT = 2


def _sum_siblings(tag, grads, recvd, place):
    n = len(grads)

    def body(place_ref, *refs):
        g_refs, r_refs, sb_refs, own_refs = (refs[k * n:(k + 1) * n] for k in range(4))
        s = pl.program_id(1)
        for i in range(n):
            tot = g_refs[i][0, 0] + r_refs[i][0]
            sb_refs[i][0] = tot.astype(BF16)

            @pl.when(s == place_ref[0])
            def _():
                own_refs[i][...] = tot

    in_specs, sb_specs, own_specs, sb_shapes, own_shapes = [], [], [], [], []
    for g in grads:
        _, _, r, cdim = g.shape
        rb = r // N_SPLIT
        in_specs.append(pl.BlockSpec((1, 1, rb, cdim), lambda b, s, p: (s, p[1], b, 0)))
        sb_specs.append(pl.BlockSpec((1, rb, cdim), lambda b, s, p: (s, b, 0)))
        own_specs.append(pl.BlockSpec((rb, cdim), lambda b, s, p: (b, 0)))
        sb_shapes.append(jax.ShapeDtypeStruct((N_SHARD, r, cdim), BF16))
        own_shapes.append(jax.ShapeDtypeStruct((r, cdim), F32))
    out = pl.pallas_call(
        body, name="sum_siblings_" + tag,
        grid_spec=pltpu.PrefetchScalarGridSpec(
            num_scalar_prefetch=1, grid=(N_SPLIT, N_SHARD),
            in_specs=in_specs + sb_specs, out_specs=sb_specs + own_specs),
        out_shape=sb_shapes + own_shapes,
        compiler_params=_params(("parallel", "arbitrary")),
    )(place, *grads, *recvd)
    return out[:n], out[n:]


def _sum_chips(own, recvd, place):
    n = len(own)

    def body(place_ref, *refs):
        o_refs, r_refs, out_refs = (refs[k * n:(k + 1) * n] for k in range(3))
        for i in range(n):
            tot = o_refs[i][...]
            for j in range(3):
                tot = tot + r_refs[i][j].astype(F32)
            out_refs[i][0] = tot

    o_specs, r_specs, out_specs = [], [], []
    for o in own:
        r, cdim = o.shape
        rb = r // N_SPLIT
        o_specs.append(pl.BlockSpec((rb, cdim), lambda b, p: (b, 0)))
        r_specs.append(pl.BlockSpec((3, rb, cdim), lambda b, p: (0, b, 0)))
        out_specs.append(pl.BlockSpec((1, rb, cdim), lambda b, p: (p[1], b, 0)))
    return pl.pallas_call(
        body, name="sum_chips",
        grid_spec=pltpu.PrefetchScalarGridSpec(num_scalar_prefetch=1, grid=(N_SPLIT,),
                                               in_specs=o_specs + r_specs, out_specs=out_specs),
        out_shape=[jax.ShapeDtypeStruct((2,) + o.shape, F32) for o in own],
        compiler_params=_params(("parallel",)),
    )(place, *own, *recvd)


def _adamw_math(w, g, m, v):
    m = ADAM_B1 * m + (1.0 - ADAM_B1) * g
    v = ADAM_B2 * v + (1.0 - ADAM_B2) * (g * g)
    m_hat = m / (1.0 - ADAM_B1 ** ADAM_STEP)
    v_hat = v / (1.0 - ADAM_B2 ** ADAM_STEP)
    delta = -ADAM_LR * (m_hat / (jnp.sqrt(v_hat) + ADAM_EPS) + ADAM_WD * w)
    return delta, m, v


def _adamw(name, ws, gs, ms, vs, n_split):
    n = len(ws)

    def body(*refs):
        w_r, g_r, m_r, v_r, d_o, m_o, v_o = (refs[k * n:(k + 1) * n] for k in range(7))
        for i in range(n):
            d, m, v = _adamw_math(w_r[i][...], g_r[i][...], m_r[i][...], v_r[i][...])
            d_o[i][...] = d
            m_o[i][...] = m
            v_o[i][...] = v

    specs = [pl.BlockSpec((w.shape[0] // n_split, w.shape[1]), lambda b: (b, 0)) for w in ws]
    shapes = [jax.ShapeDtypeStruct(w.shape, F32) for w in ws]
    out = pl.pallas_call(
        body, name=name, grid=(n_split,),
        in_specs=specs * 4, out_specs=specs * 3, out_shape=shapes * 3,
        compiler_params=_params(("parallel",)),
    )(*ws, *gs, *ms, *vs)
    return out[:n], out[n:2 * n], out[2 * n:]


def _reduce_small(own, received):
    def body(own_ref, recv_ref, g_out):
        me = 4 * lax.axis_index("x") + 2 * lax.axis_index("y") + lax.axis_index("c")
        g = None
        for k in range(8):
            mine = me == k
            part = jnp.where(mine, own_ref[...], recv_ref[jnp.where(mine, 0, jnp.bitwise_xor(me, k) - 1)])
            g = part if g is None else g + part
        g_out[...] = g

    return pl.pallas_call(
        body, name="reduce_small",
        out_shape=jax.ShapeDtypeStruct(own.shape, F32),
        compiler_params=_params(),
    )(own, received)


def _adamw_small(ws, gs, ms, vs):
    n = len(ws)

    def body(*refs):
        w_r, g_r, m_r, v_r, d_o, m_o, v_o = (refs[k * n:(k + 1) * n] for k in range(7))
        for i in range(n):
            d, mm, vv = _adamw_math(w_r[i][...], g_r[i][...], m_r[i][...], v_r[i][...])
            d_o[i][...] = d
            m_o[i][...] = mm
            v_o[i][...] = vv

    out = pl.pallas_call(
        body, name="adamw_small",
        out_shape=[jax.ShapeDtypeStruct(t.shape, F32) for t in ws] * 3,
        compiler_params=_params(),
    )(*ws, *gs, *ms, *vs)
    return out[:n], out[n:2 * n], out[2 * n:]


def _s5_operands(lam_re, lam_im, log_step, b_re, b_im, c_re, c_im, glu_w):
    lr = jnp.minimum(lam_re, -1e-4)
    li = lam_im
    step = jnp.exp(log_step)[:, None]
    mag = jnp.exp(lr * step)
    ang = li * step
    abr = mag * jnp.cos(ang)
    abi = mag * jnp.sin(ang)
    nr = abr - 1.0
    ni = abi
    den = lr * lr + li * li
    cr = ((nr * lr + ni * li) / den)[..., None]
    ci = ((ni * lr - nr * li) / den)[..., None]
    bbr = cr * b_re - ci * b_im
    bbi = cr * b_im + ci * b_re
    eye = jnp.eye(8, dtype=F32)
    g, h, p = SSM_GROUPS // SUPER, SSM_GROUP, SSM_STATE

    def b_layout(t):
        return jnp.einsum("ab,japh->jahbp", eye, t.reshape(SUPER, g, p, h)).reshape(SUPER, g * h, g * p)

    def c_layout(t):
        return jnp.einsum("ab,jahp->jbpah", eye, t.reshape(SUPER, g, h, p)).reshape(SUPER, g * p, g * h)

    glu = jnp.einsum("ab,jahk->jahbk", eye, glu_w.reshape(SUPER, g, h, h)).reshape(SUPER, g * h, g * h)
    lam = _pad_rows(jnp.concatenate([abr.reshape(1, N_STATE), abi.reshape(1, N_STATE)], axis=0), 8)
    return lam, b_layout(bbr), b_layout(bbi), c_layout(c_re), c_layout(c_im), glu


def _pad_rows(a, rows):
    return jnp.pad(a, ((0, rows - a.shape[0]), (0, 0)))


def _pack(parts):
    rows = []
    for a in parts:
        flat = a.reshape(-1)
        n = -(-flat.shape[0] // 128)
        rows.append(jnp.pad(flat, (0, n * 128 - flat.shape[0])).reshape(n, 128))
    out = jnp.concatenate(rows, axis=0)
    return _pad_rows(out, -(-out.shape[0] // 8) * 8)


def _unpack(packed, like):
    out, at = [], 0
    for a in like:
        n = -(-a.size // 128)
        out.append(packed[at:at + n].reshape(-1)[:a.size].reshape(a.shape))
        at += n
    return out


STORED = {"ssm_b_re": (0, 1, 3, 2), "ssm_b_im": (0, 1, 3, 2), "ssm_d": (0, 2, 1), "ssm_glu_b": (0, 2, 1),
          "ssm_glu_w": (0, 2, 3, 1)}


def _stored(k, a):
    return a.transpose(STORED[k]) if k in STORED else a


def _logical(k, a):
    return a.transpose(tuple(STORED[k].index(i) for i in range(a.ndim))) if k in STORED else a


SMALL = ("norm1_g", "ssm_lambda_re", "ssm_lambda_im", "ssm_log_step", "ssm_b_re", "ssm_b_im", "ssm_c_re", "ssm_c_im",
         "ssm_d", "ssm_glu_w", "ssm_glu_b", "ssm_norm_g", "pool_w", "pool_scale", "pool_norm_g", "norm2_g",
         "final_norm_g")
LARGE = ("w_in", "w_out", "w_gate", "w_up", "w_down")
WEIGHTS = ("meta_tokens", "norm1_g", "w_in", "ssm_lambda_re", "ssm_lambda_im", "ssm_log_step", "ssm_b_re", "ssm_b_im",
           "ssm_c_re", "ssm_c_im", "ssm_d", "ssm_glu_w", "ssm_glu_b", "ssm_norm_g", "pool_w", "pool_scale",
           "pool_norm_g", "w_out", "norm2_g", "w_gate", "w_up", "w_down", "final_norm_g")


def _step(x, target, w, m, v):
    seq = x.shape[1]
    n_rows = N_META + seq
    n_pad, tm, tp, tc, tg = _plan(n_rows)
    xq, yq, cq = lax.axis_index("x"), lax.axis_index("y"), lax.axis_index("c")
    place = jnp.stack([2 * xq + yq, cq]).astype(jnp.int32)

    def halves(a2d):
        return a2d.reshape(2, a2d.shape[0] // 2, a2d.shape[1])

    def local2d(t):
        return {"w_gate": lambda a: a[0].T, "w_up": lambda a: a[0].T}.get(t, lambda a: a[0])

    shards = [halves(local2d(k)(w[k])) for k in LARGE] + [halves(w["meta_tokens"])]
    full = _cast_shards(shards, [BF16] * len(LARGE) + [F32], place)
    w_in_full, meta_full = _gather_shards([full[0], full[5]])
    late, gather_sems, gather_token = _copies_start("gather_start", list(full[1:5]), (12,), _build_gather,
                                                    after=w_in_full)
    w_in_b = w_in_full.reshape(D_MODEL, D_MODEL)
    meta = meta_full.reshape(N_SHARD, N_META, D_MODEL // N_SHARD).transpose(1, 0, 2).reshape(N_META, D_MODEL)

    h0 = _pad_rows(jnp.concatenate([meta, x[0]], axis=0), n_pad)
    tgt = _pad_rows(jnp.concatenate([jnp.zeros((N_META, D_MODEL), F32), target[0]], axis=0), n_pad)
    s5_in = (w["ssm_lambda_re"][0], w["ssm_lambda_im"][0], w["ssm_log_step"][0], w["ssm_b_re"][0], w["ssm_b_im"][0],
             w["ssm_c_re"][0], w["ssm_c_im"][0], w["ssm_glu_w"][0])
    (lam, bbr, bbi, crt, cit, glu), s5_vjp = jax.vjp(_s5_operands, *s5_in)
    bbr_b, bbi_b, crt_b, cit_b, glu_b16 = (t.astype(BF16) for t in (bbr, bbi, crt, cit, glu))
    s5_vecs = _pad_rows(jnp.concatenate([w["ssm_d"].reshape(1, D_SSM), w["ssm_glu_b"].reshape(1, D_SSM),
                                         w["ssm_norm_g"].reshape(1, D_SSM)], axis=0), 8)
    pool_vecs = _pad_rows(jnp.concatenate([w["pool_scale"].reshape(1, D_POOL), w["pool_norm_g"].reshape(1, D_POOL)],
                                          axis=0), 8)
    pw_b = w["pool_w"][0].astype(BF16)
    g1, g2, gf = w["norm1_g"].reshape(1, D_MODEL), w["norm2_g"].reshape(1, D_MODEL), w["final_norm_g"].reshape(1, D_MODEL)

    u, vv = _fwd_in(h0, g1, w_in_b, tp, gather_token)
    sr, si, y, ms = _s5_fwd(u, lam, bbr_b, bbi_b, crt_b, cit_b, s5_vecs, glu_b16, tc)
    late = _copies_wait("gather_wait", late, gather_sems, [ms], _build_gather)
    late, forward_sems, forward_token = _copies_start("forward_start", late, (12,), _build_forward)
    feat, mp = _pool_fwd(vv, pw_b, pool_vecs, tc, forward_token)
    late = _copies_wait("forward_wait", late, forward_sems, [mp], _build_forward)
    w_out_b = late[0].reshape(D_MODEL, D_MODEL)
    wg_b, wu_b, wd_b = (t.reshape(N_SHARD, FF_SHARD, D_MODEL) for t in late[1:])
    h1, n2, a, b, ff, dh2, dh2b, loss_acc, dgf = _fwd_ffn(h0, ms, mp, w_out_b, g2, wg_b, wu_b, wd_b, gf, tgt, tc, n_rows)

    def quarters(t):
        if t.ndim == 2:
            t = t.reshape(N_SHARD, t.shape[0] // N_SHARD, t.shape[1])
        return t.reshape(N_SHARD, 2, t.shape[1] // 2, t.shape[2])

    def landing(like, lead, dtype):
        return [lax.empty((lead,) + t.shape[2:], dtype) for t in like]

    da, db, dh1, dg2 = _bwd_ffn(dh2, dh2b, a, b, wg_b, wu_b, wd_b, h1, g2, tc)
    ffn_g = [quarters(t) for t in _grad_ffn(n2, da, db, ff, dh2b, tg)]
    nf = len(ffn_g)
    moved, swap_sems, swap_token = _copies_start("swap_start", ffn_g + landing(ffn_g, N_SHARD, F32), (nf,), _build_swap)
    dms, dmp, dwo = _bwd_out(dh1, ms, mp, w_out_b, tp, swap_token)
    moved = _copies_wait("swap_wait", moved, swap_sems, [dwo], _build_swap)
    ffn_parts, ffn_own = _sum_siblings("ffn", moved[:nf], moved[nf:], place)
    moved, exch_sems, exch_token = _copies_start("exchange_start", list(ffn_parts) + landing(ffn_g, 3, BF16), (3 * nf,),
                                                 _build_exchange)
    du, dbbr, dbbi, dcrt, dcit, dglu, ds5v, dlam = _s5_bwd(dms, y, u, sr, si, lam, bbr_b, bbi_b, crt_b, cit_b,
                                                           s5_vecs, glu_b16, tc, exch_token)
    dv, dpw, dpoolv = _pool_bwd(dmp, feat, pw_b, pool_vecs, tc)
    dh0, dwi, dg1 = _bwd_in(du, dv, h0, dh1, g1, w_in_b, tp)
    ffn_from_chips = _copies_wait("exchange_wait", moved, exch_sems, [dh0], _build_exchange)[nf:]
    dlam = _pad_rows(jnp.concatenate([jnp.sum(dlam[:SEGMENTS], axis=0, keepdims=True),
                                      jnp.sum(dlam[SEGMENTS:], axis=0, keepdims=True)], axis=0), 8)
    d_lre, d_lim, d_lstep, d_bre, d_bim, d_cre, d_cim, d_gluw = s5_vjp((dlam, dbbr, dbbi, dcrt, dcit, dglu))
    grad_x = dh0[N_META:n_rows][None]

    small_g = {
        "norm1_g": dg1, "ssm_lambda_re": d_lre, "ssm_lambda_im": d_lim, "ssm_log_step": d_lstep, "ssm_b_re": d_bre,
        "ssm_b_im": d_bim, "ssm_c_re": d_cre, "ssm_c_im": d_cim, "ssm_d": ds5v[0], "ssm_glu_w": d_gluw,
        "ssm_glu_b": ds5v[1], "ssm_norm_g": ds5v[2], "pool_w": dpw, "pool_scale": dpoolv[0], "pool_norm_g": dpoolv[1],
        "norm2_g": dg2, "final_norm_g": dgf,
    }
    like = [_stored(k, w[k]) for k in SMALL]
    packed_g = _pack([_stored(k, small_g[k].reshape(w[k].shape)) for k in SMALL] + [dh0[:N_META], loss_acc[0:1, 0:1]])

    mix_g = [quarters(t) for t in (dwi, dwo)]
    mix_parts, mix_own = _sum_siblings("mix", mix_g, _swap_halves(mix_g), place)
    moved, mix_sems, mix_token = _copies_start("mix_exchange_start", list(mix_parts) + landing(mix_g, 3, BF16),
                                               (3 * len(mix_g),), _build_exchange)
    spread, small_sems, small_token = _copies_start(
        "small_start", [packed_g, lax.empty((7,) + packed_g.shape, F32)], (7,), _build_spread, after=mix_token)
    mix_from_chips = _copies_wait("mix_exchange_wait", moved, mix_sems, [small_token], _build_exchange)[len(mix_g):]
    joined = _join_halves(_sum_chips(list(mix_own) + list(ffn_own), list(mix_from_chips) + list(ffn_from_chips), place))
    g_large = [j.reshape(j.shape[0] * j.shape[1], j.shape[2]) for j in joined]
    w2d, m2d, v2d = ([local2d(k)(t[k]) for k in LARGE] for t in (w, m, v))
    d_large, m_large, v_large = _adamw("adamw_large", w2d, g_large, m2d, v2d, 8)

    own_g, landed = _copies_wait("small_wait", spread, small_sems, [d_large[0]], _build_spread)
    g_pk = _reduce_small(own_g, _forward_small(landed))
    g_small = _unpack(g_pk, like + [jax.ShapeDtypeStruct((N_META, D_MODEL), F32), jax.ShapeDtypeStruct((1, 1), F32)])
    loss = g_small.pop()[0, 0]
    rows2d = lambda t: t.reshape(1, -1) if t.ndim == 1 else t
    d_small, m_small, v_small = _adamw_small(*([rows2d(t) for t in ts] for ts in (
        like, g_small[:-1], [_stored(k, m[k]) for k in SMALL], [_stored(k, v[k]) for k in SMALL])))
    g_small, d_small, m_small, v_small = ([_logical(k, t.reshape(a.shape)) for t, a, k in zip(ts, like, SMALL)] + ts[len(SMALL):]
                                          for ts in (g_small, list(d_small), list(m_small), list(v_small)))
    q = place[0]
    g_meta = lax.dynamic_slice_in_dim(g_small[-1], q * (D_MODEL // N_SHARD), D_MODEL // N_SHARD, axis=1)
    d_meta, m_meta, v_meta = _adamw("adamw_meta", [w["meta_tokens"]], [g_meta], [m["meta_tokens"]],
                                    [v["meta_tokens"]], 1)

    grads, deltas, new_m, new_v = {}, {}, {}, {}
    for i, k in enumerate(SMALL):
        grads[k], deltas[k], new_m[k], new_v[k] = g_small[i], d_small[i], m_small[i], v_small[i]
    for i, k in enumerate(LARGE):
        back = (lambda t: t.T[None]) if k in ("w_gate", "w_up") else (lambda t: t[None])
        grads[k], deltas[k], new_m[k], new_v[k] = (back(t) for t in (g_large[i], d_large[i], m_large[i], v_large[i]))
    grads["meta_tokens"], deltas["meta_tokens"] = g_meta, d_meta[0]
    new_m["meta_tokens"], new_v["meta_tokens"] = m_meta[0], v_meta[0]
    return (loss, grad_x, *[grads[k] for k in WEIGHTS], *[deltas[k] for k in WEIGHTS],
            *[new_m[k] for k in WEIGHTS], *[new_v[k] for k in WEIGHTS])


def kernel(x, meta_tokens, norm1_g, w_in, ssm_lambda_re, ssm_lambda_im, ssm_log_step, ssm_b_re, ssm_b_im, ssm_c_re, ssm_c_im, ssm_d, ssm_glu_w, ssm_glu_b, ssm_norm_g, pool_w, pool_scale, pool_norm_g, w_out, norm2_g, w_gate, w_up, w_down, final_norm_g, loss_target, m_meta_tokens, m_norm1_g, m_w_in, m_ssm_lambda_re, m_ssm_lambda_im, m_ssm_log_step, m_ssm_b_re, m_ssm_b_im, m_ssm_c_re, m_ssm_c_im, m_ssm_d, m_ssm_glu_w, m_ssm_glu_b, m_ssm_norm_g, m_pool_w, m_pool_scale, m_pool_norm_g, m_w_out, m_norm2_g, m_w_gate, m_w_up, m_w_down, m_final_norm_g, v_meta_tokens, v_norm1_g, v_w_in, v_ssm_lambda_re, v_ssm_lambda_im, v_ssm_log_step, v_ssm_b_re, v_ssm_b_im, v_ssm_c_re, v_ssm_c_im, v_ssm_d, v_ssm_glu_w, v_ssm_glu_b, v_ssm_norm_g, v_pool_w, v_pool_scale, v_pool_norm_g, v_w_out, v_norm2_g, v_w_gate, v_w_up, v_w_down, v_final_norm_g):
    w = dict(meta_tokens=meta_tokens, norm1_g=norm1_g, w_in=w_in, ssm_lambda_re=ssm_lambda_re, ssm_lambda_im=ssm_lambda_im, ssm_log_step=ssm_log_step, ssm_b_re=ssm_b_re, ssm_b_im=ssm_b_im, ssm_c_re=ssm_c_re, ssm_c_im=ssm_c_im, ssm_d=ssm_d, ssm_glu_w=ssm_glu_w, ssm_glu_b=ssm_glu_b, ssm_norm_g=ssm_norm_g, pool_w=pool_w, pool_scale=pool_scale, pool_norm_g=pool_norm_g, w_out=w_out, norm2_g=norm2_g, w_gate=w_gate, w_up=w_up, w_down=w_down, final_norm_g=final_norm_g)
    m = dict(meta_tokens=m_meta_tokens, norm1_g=m_norm1_g, w_in=m_w_in, ssm_lambda_re=m_ssm_lambda_re, ssm_lambda_im=m_ssm_lambda_im, ssm_log_step=m_ssm_log_step, ssm_b_re=m_ssm_b_re, ssm_b_im=m_ssm_b_im, ssm_c_re=m_ssm_c_re, ssm_c_im=m_ssm_c_im, ssm_d=m_ssm_d, ssm_glu_w=m_ssm_glu_w, ssm_glu_b=m_ssm_glu_b, ssm_norm_g=m_ssm_norm_g, pool_w=m_pool_w, pool_scale=m_pool_scale, pool_norm_g=m_pool_norm_g, w_out=m_w_out, norm2_g=m_norm2_g, w_gate=m_w_gate, w_up=m_w_up, w_down=m_w_down, final_norm_g=m_final_norm_g)
    v = dict(meta_tokens=v_meta_tokens, norm1_g=v_norm1_g, w_in=v_w_in, ssm_lambda_re=v_ssm_lambda_re, ssm_lambda_im=v_ssm_lambda_im, ssm_log_step=v_ssm_log_step, ssm_b_re=v_ssm_b_re, ssm_b_im=v_ssm_b_im, ssm_c_re=v_ssm_c_re, ssm_c_im=v_ssm_c_im, ssm_d=v_ssm_d, ssm_glu_w=v_ssm_glu_w, ssm_glu_b=v_ssm_glu_b, ssm_norm_g=v_ssm_norm_g, pool_w=v_pool_w, pool_scale=v_pool_scale, pool_norm_g=v_pool_norm_g, w_out=v_w_out, norm2_g=v_norm2_g, w_gate=v_w_gate, w_up=v_w_up, w_down=v_w_down, final_norm_g=v_final_norm_g)
    return _step(x, loss_target, w, m, v)
```
